```python
import jax, jax.numpy as jnp
from jax import lax
import numpy as np

D_MODEL = 2048
BATCH = 8
SEQ = 4096
DEPTH = 2

N_EVEN = (DEPTH + 1) // 2
N_ODD = DEPTH // 2
RMS_EPS = 1e-6
NEG_BIG = -1e30

MLSTM_WIDTH = D_MODEL // 2
MLSTM_HEADS = 4
MLSTM_DV = MLSTM_WIDTH // MLSTM_HEADS
MLSTM_DK = MLSTM_DV // 2
MLSTM_CHUNK = 64
F_BIAS_INIT = 3.0

RNN_WIDTH = D_MODEL // 2
RNN_BLOCKS = 8
RNN_BLOCK = RNN_WIDTH // RNN_BLOCKS
CONV_WIDTH = 4
CONV_LEFT = 2
RGLRU_C = 8.0

IN_SIZES = (MLSTM_HEADS * MLSTM_DK,
            MLSTM_HEADS * MLSTM_DK,
            MLSTM_WIDTH,
            MLSTM_WIDTH,
            4 * MLSTM_HEADS,
            RNN_WIDTH,
            RNN_WIDTH)
IN_SPLITS = tuple(sum(IN_SIZES[:i + 1]) for i in range(len(IN_SIZES) - 1))
D_IN = sum(IN_SIZES)

ATTN_HEADS = 16
ATTN_DH = D_MODEL // ATTN_HEADS
DILATED_PATTERNS = ((128, 1), (512, 4), (2048, 16))
ATTN_BLOCK = 64
ALIBI_MAX_BIAS = 8.0

D_FF = ((8 * D_MODEL + 3 * 256 - 1) // (3 * 256)) * 256

kernel_name = 'hybrid_mlstm_rglru_dilated_encoder'


def rmsnorm(x, g):
    xf = x.astype(jnp.float32)
    y = xf * lax.rsqrt(jnp.mean(jnp.square(xf), axis=-1, keepdims=True) + RMS_EPS)
    return (y * g.astype(jnp.float32)).astype(x.dtype)


def mlstm_chunkwise(q, k, v, log_i, log_f):
    B, H, S, dk = q.shape
    dv = v.shape[-1]
    L = MLSTM_CHUNK
    nc = S // L

    def to_chunks(a):
        return jnp.moveaxis(a.reshape(a.shape[:2] + (nc, L) + a.shape[3:]), 2, 0)

    xs = tuple(to_chunks(a) for a in (q, k, v, log_i, log_f))
    lower = jnp.tril(jnp.ones((L, L), dtype=bool))

    def step(carry, chunk):
        C, n, m = carry
        qj, kj, vj, ij, fj = chunk
        b = jnp.cumsum(fj, axis=-1)
        D = b[..., :, None] - b[..., None, :] + ij[..., None, :]
        D = jnp.where(lower, D, NEG_BIG)
        inter = b + m[..., None]
        m_out = jnp.maximum(inter, jnp.max(D, axis=-1))
        w_intra = jnp.exp(D - m_out[..., None])
        w_inter = jnp.exp(inter - m_out)
        s = jnp.einsum('bhld,bhsd->bhls', qj, kj) * w_intra
        num = (jnp.einsum('bhls,bhsv->bhlv', s, vj)
               + w_inter[..., None] * jnp.einsum('bhvd,bhld->bhlv', C, qj))
        den = jnp.sum(s, axis=-1) + w_inter * jnp.einsum('bhd,bhld->bhl', n, qj)
        h = num / jnp.maximum(jnp.abs(den), jnp.exp(-m_out))[..., None]
        bL = b[..., -1]
        src = bL[..., None] - b + ij
        m_new = jnp.maximum(bL + m, jnp.max(src, axis=-1))
        w_src = jnp.exp(src - m_new[..., None])
        decay = jnp.exp(bL + m - m_new)
        C_new = decay[..., None, None] * C + jnp.einsum('bhl,bhlv,bhld->bhvd', w_src, vj, kj)
        n_new = decay[..., None] * n + jnp.einsum('bhl,bhld->bhd', w_src, kj)
        return (C_new, n_new, m_new), h

    init = (jnp.zeros((B, H, dv, dk), jnp.float32),
            jnp.zeros((B, H, dk), jnp.float32),
            jnp.full((B, H), NEG_BIG, jnp.float32))
    _, hc = lax.scan(step, init, xs)
    return jnp.moveaxis(hc, 0, 2).reshape(B, H, S, dv)


def bidirectional_mlstm(q, k, v, gates):
    g = jnp.transpose(gates, (2, 0, 3, 1))
    fwd = mlstm_chunkwise(q, k, v, g[0], jax.nn.log_sigmoid(g[1]))
    flip = lambda a: jnp.flip(a, axis=2)
    bwd = flip(mlstm_chunkwise(flip(q), flip(k), flip(v), flip(g[2]),
                               flip(jax.nn.log_sigmoid(g[3]))))
    return fwd + bwd


def centred_depthwise_conv(x, w, b):
    S = x.shape[1]
    xp = jnp.pad(x, ((0, 0), (CONV_LEFT, CONV_WIDTH - 1 - CONV_LEFT), (0, 0)))
    return sum(xp[:, j:j + S, :] * w[j] for j in range(CONV_WIDTH)) + b


def rglru_scan(x, w_a, b_a, w_x, b_x, lam, reverse):
    B, S, _ = x.shape
    xf = x.astype(jnp.float32)
    xb = xf.reshape(B, S, RNN_BLOCKS, RNN_BLOCK)
    pre_a = jnp.einsum('bsni,nij->bsnj', xb, w_a.astype(jnp.float32)).reshape(B, S, RNN_WIDTH)
    pre_x = jnp.einsum('bsni,nij->bsnj', xb, w_x.astype(jnp.float32)).reshape(B, S, RNN_WIDTH)
    r = jax.nn.sigmoid(pre_a + b_a.astype(jnp.float32))
    i = jax.nn.sigmoid(pre_x + b_x.astype(jnp.float32))
    log_a = -RGLRU_C * r * jax.nn.softplus(-lam.astype(jnp.float32))
    a = jnp.exp(log_a)
    u = jnp.sqrt(-jnp.expm1(2.0 * log_a)) * (i * xf)

    def combine(left, right):
        a_l, u_l = left
        a_r, u_r = right
        return a_l * a_r, a_r * u_l + u_r

    _, h = lax.associative_scan(combine, (a, u), axis=1, reverse=reverse)
    return h


def even_mixer(h, w_in, gate_b, conv_w, conv_b, rg_wa, rg_ba, rg_wx, rg_bx, rg_lam,
               head_g, w_out):
    B, S, _ = h.shape
    z = h @ w_in
    q, k, v, o, g, xr, gr = jnp.split(z, IN_SPLITS, axis=-1)
    heads = lambda a, d: jnp.transpose(a.reshape(B, S, MLSTM_HEADS, d), (0, 2, 1, 3)).astype(jnp.float32)
    qh = heads(q, MLSTM_DK) * (MLSTM_DK ** -0.5)
    kh = heads(k, MLSTM_DK)
    vh = heads(v, MLSTM_DV)
    gates = (g.astype(jnp.float32) + gate_b.astype(jnp.float32)).reshape(B, S, 4, MLSTM_HEADS)
    hm = bidirectional_mlstm(qh, kh, vh, gates)
    hm = hm * lax.rsqrt(jnp.mean(jnp.square(hm), axis=-1, keepdims=True) + RMS_EPS)
    hm = jnp.transpose(hm, (0, 2, 1, 3)).reshape(B, S, MLSTM_WIDTH) * head_g.astype(jnp.float32)
    y_a = hm * jax.nn.sigmoid(o.astype(jnp.float32))
    xc = centred_depthwise_conv(xr, conv_w, conv_b)
    hr = (rglru_scan(xc, rg_wa[0], rg_ba[0], rg_wx[0], rg_bx[0], rg_lam[0], False)
          + rglru_scan(xc, rg_wa[1], rg_ba[1], rg_wx[1], rg_bx[1], rg_lam[1], True))
    y_b = hr * jax.nn.gelu(gr.astype(jnp.float32))
    y = jnp.concatenate([y_a, y_b], axis=-1).astype(h.dtype)
    return y @ w_out


def dilated_branch(q, k, v, dil, half, slopes):
    B, S, H, dh = q.shape
    sp = S // dil
    nb = -(-sp // ATTN_BLOCK)
    sq = nb * ATTN_BLOCK
    kw = ATTN_BLOCK + 2 * half

    def by_residue(a):
        return jnp.transpose(a.reshape(B, sp, dil, H, dh), (0, 2, 3, 1, 4))

    qr = jnp.pad(by_residue(q), ((0, 0), (0, 0), (0, 0), (0, sq - sp), (0, 0)))
    kv_pad = ((0, 0), (0, 0), (0, 0), (half, half + sq - sp), (0, 0))
    kr = jnp.pad(by_residue(k), kv_pad)
    vr = jnp.pad(by_residue(v), kv_pad)
    idx = jnp.arange(nb)[:, None] * ATTN_BLOCK + jnp.arange(kw)[None, :]
    kb = kr[:, :, :, idx, :]
    vb = vr[:, :, :, idx, :]
    qb = qr.reshape(B, dil, H, nb, ATTN_BLOCK, dh)
    qpos = jnp.arange(sq).reshape(nb, ATTN_BLOCK)
    kpos = idx - half
    rel = kpos[:, None, :] - qpos[:, :, None]
    valid = (jnp.abs(rel) <= half) & (kpos[:, None, :] >= 0) & (kpos[:, None, :] < sp)
    penalty = slopes[:, None, None, None] * (jnp.abs(rel) * dil).astype(jnp.float32)
    s = jnp.einsum('bdhnqc,bdhnkc->bdhnqk', qb.astype(jnp.float32), kb.astype(jnp.float32)) - penalty
    s = jnp.where(valid, s, NEG_BIG)
    m = jnp.max(s, axis=-1, keepdims=True)
    p = jnp.exp(s - m)
    denom = jnp.sum(p, axis=-1)
    o = jnp.einsum('bdhnqk,bdhnkc->bdhnqc', p, vb.astype(jnp.float32)) / denom[..., None]
    lse = m[..., 0] + jnp.log(denom)
    o = o.reshape(B, dil, H, sq, dh)[:, :, :, :sp]
    lse = lse.reshape(B, dil, H, sq)[:, :, :, :sp]
    o = jnp.transpose(o, (0, 3, 1, 2, 4)).reshape(B, S, H, dh)
    lse = jnp.transpose(lse, (0, 3, 1, 2)).reshape(B, S, H)
    return o, lse


def dilated_attention(h, w_qkv, w_o):
    B, S, _ = h.shape
    qkv = (h @ w_qkv).reshape(B, S, 3, ATTN_HEADS, ATTN_DH)
    q = qkv[:, :, 0] * (ATTN_DH ** -0.5)
    k = qkv[:, :, 1]
    v = qkv[:, :, 2]
    slopes = jnp.exp2(-ALIBI_MAX_BIAS * jnp.arange(1, ATTN_HEADS + 1, dtype=jnp.float32) / ATTN_HEADS)
    outs, lses = [], []
    for window, dil in DILATED_PATTERNS:
        o_g, lse_g = dilated_branch(q, k, v, dil, window // (2 * dil), slopes)
        outs.append(o_g)
        lses.append(lse_g)
    wts = jax.nn.softmax(jnp.stack(lses, axis=0), axis=0)
    o = jnp.sum(wts[..., None] * jnp.stack(outs, axis=0), axis=0)
    return o.reshape(B, S, D_MODEL).astype(h.dtype) @ w_o


def swiglu(h, w1, w3, w2):
    return (jax.nn.silu(h @ w1) * (h @ w3)) @ w2


def setup_inputs(seed: int = 0) -> dict:
    key = jax.random.key(seed)
    ks = jax.random.split(key, 24)
    f32 = jnp.float32
    nrm = lambda k, shape, scale: scale * jax.random.normal(k, shape, f32)
    gain = lambda k, shape: 1.0 + nrm(k, shape, 0.02)
    gate_offset = jnp.repeat(jnp.array([0.0, F_BIAS_INIT, 0.0, F_BIAS_INIT], f32), MLSTM_HEADS)
    u = jax.random.uniform(ks[10], (N_EVEN, 2, RNN_WIDTH), f32, minval=0.9, maxval=0.999)
    a0 = u ** (1.0 / RGLRU_C)
    lam = jnp.log(a0) - jnp.log1p(-a0)
    return {
        'x': nrm(ks[0], (BATCH, SEQ, D_MODEL), 1.0),
        'e_norm': gain(ks[1], (N_EVEN, D_MODEL)),
        'e_w_in': nrm(ks[2], (N_EVEN, D_MODEL, D_IN), D_MODEL ** -0.5),
        'e_gate_b': gate_offset + nrm(ks[3], (N_EVEN, 4 * MLSTM_HEADS), 0.5),
        'e_conv_w': nrm(ks[4], (N_EVEN, CONV_WIDTH, RNN_WIDTH), 0.5),
        'e_conv_b': nrm(ks[5], (N_EVEN, RNN_WIDTH), 0.02),
        'e_rg_wa': nrm(ks[6], (N_EVEN, 2, RNN_BLOCKS, RNN_BLOCK, RNN_BLOCK), RNN_BLOCK ** -0.5),
        'e_rg_ba': nrm(ks[7], (N_EVEN, 2, RNN_WIDTH), 0.02),
        'e_rg_wx': nrm(ks[8], (N_EVEN, 2, RNN_BLOCKS, RNN_BLOCK, RNN_BLOCK), RNN_BLOCK ** -0.5),
        'e_rg_bx': nrm(ks[9], (N_EVEN, 2, RNN_WIDTH), 0.02),
        'e_rg_lam': lam,
        'e_head_g': gain(ks[11], (N_EVEN, MLSTM_WIDTH)),
        'e_w_out': nrm(ks[12], (N_EVEN, D_MODEL, D_MODEL), D_MODEL ** -0.5),
        'o_norm': gain(ks[13], (N_ODD, D_MODEL)),
        'o_w_qkv': nrm(ks[14], (N_ODD, D_MODEL, 3 * D_MODEL), D_MODEL ** -0.5),
        'o_w_o': nrm(ks[15], (N_ODD, D_MODEL, D_MODEL), D_MODEL ** -0.5),
        'f_norm': gain(ks[16], (DEPTH, D_MODEL)),
        'f_w1': nrm(ks[17], (DEPTH, D_MODEL, D_FF), D_MODEL ** -0.5),
        'f_w3': nrm(ks[18], (DEPTH, D_MODEL, D_FF), D_MODEL ** -0.5),
        'f_w2': nrm(ks[19], (DEPTH, D_FF, D_MODEL), D_FF ** -0.5),
        'final_norm': gain(ks[20], (D_MODEL,)),
    }


def reference(x, e_norm, e_w_in, e_gate_b, e_conv_w, e_conv_b, e_rg_wa, e_rg_ba, e_rg_wx,
              e_rg_bx, e_rg_lam, e_head_g, e_w_out, o_norm, o_w_qkv, o_w_o, f_norm, f_w1,
              f_w3, f_w2, final_norm):
    for l in range(DEPTH):
        if l % 2 == 0:
            e = l // 2
            h = rmsnorm(x, e_norm[e])
            x = x + even_mixer(h, e_w_in[e], e_gate_b[e], e_conv_w[e], e_conv_b[e], e_rg_wa[e],
                               e_rg_ba[e], e_rg_wx[e], e_rg_bx[e], e_rg_lam[e], e_head_g[e],
                               e_w_out[e]).astype(x.dtype)
        else:
            o = l // 2
            h = rmsnorm(x, o_norm[o])
            x = x + dilated_attention(h, o_w_qkv[o], o_w_o[o]).astype(x.dtype)
        h = rmsnorm(x, f_norm[l])
        x = x + swiglu(h, f_w1[l], f_w3[l], f_w2[l]).astype(x.dtype)
    return rmsnorm(x, final_norm)
```

```python
import functools
import math

import jax
import jax.numpy as jnp
from jax import lax
from jax.experimental import pallas as pl
from jax.experimental.pallas import tpu as pltpu

F32 = jnp.float32
BF16 = jnp.bfloat16

RMS_EPS = 1e-6
NEG_BIG = -1e30

MLSTM_HEADS = 4
MLSTM_DK = 128
MLSTM_DV = 256
MLSTM_CHUNK = 256
RNN_BLOCKS = 8
RNN_BLOCK = 128
CONV_WIDTH = 4
CONV_LEFT = 2
RGLRU_C = 8.0
ATTN_HEADS = 16
ATTN_DH = 128
ATTN_HALF = 64
ATTN_DILATIONS = (1, 4, 16)
ALIBI_MAX_BIAS = 8.0

LANES = 128
SUBLANES = 8
VMEM_LIMIT_BYTES = 56 * 1024 * 1024


def _cparams(*semantics):
    return pltpu.CompilerParams(dimension_semantics=semantics,
                                vmem_limit_bytes=VMEM_LIMIT_BYTES)


def _rmsnorm_body(x_ref, g_ref, o_ref):
    x = x_ref[...]
    ms = jnp.mean(x * x, axis=-1, keepdims=True)
    o_ref[...] = (x * lax.rsqrt(ms + RMS_EPS) * g_ref[...]).astype(o_ref.dtype)


def rmsnorm(x, g, out_dtype, tm=512):
    m, d = x.shape
    return pl.pallas_call(
        _rmsnorm_body,
        grid=(m // tm,),
        in_specs=[pl.BlockSpec((tm, d), lambda i: (i, 0)),
                  pl.BlockSpec((1, d), lambda i: (0, 0))],
        out_specs=pl.BlockSpec((tm, d), lambda i: (i, 0)),
        out_shape=jax.ShapeDtypeStruct((m, d), out_dtype),
        compiler_params=_cparams("parallel"),
        name="rmsnorm",
    )(x, g.reshape(1, d).astype(F32))


def _matmul_body(*refs, n_pairs, has_res, nk):
    pairs = [(refs[2 * p], refs[2 * p + 1]) for p in range(n_pairs)]
    rest = refs[2 * n_pairs:]
    r_ref = rest[0] if has_res else None
    o_ref = rest[1] if has_res else rest[0]
    acc_ref = rest[-1] if nk > 1 else None

    part = None
    for a_ref, w_ref in pairs:
        d = jnp.dot(a_ref[...], w_ref[...], preferred_element_type=F32)
        part = d if part is None else part + d

    def finish(acc):
        if has_res:
            acc = acc + r_ref[...]
        o_ref[...] = acc.astype(o_ref.dtype)

    if nk == 1:
        finish(part)
        return

    k = pl.program_id(2)

    @pl.when(k == 0)
    def _():
        acc_ref[...] = part

    @pl.when(jnp.logical_and(k > 0, k < nk - 1))
    def _():
        acc_ref[...] += part

    @pl.when(k == nk - 1)
    def _():
        finish(acc_ref[...] + part)


def matmul(pairs, res=None, out_dtype=F32, tm=1024, tn=1024, tk=None, name="matmul"):
    m, kdim = pairs[0][0].shape
    n = pairs[0][1].shape[1]
    tk = kdim if tk is None else tk
    tn = min(tn, n)
    nk = kdim // tk
    assert m % tm == 0 and n % tn == 0 and kdim % tk == 0
    in_specs, args = [], []
    for a, w in pairs:
        in_specs += [pl.BlockSpec((tm, tk), lambda i, j, k: (i, k)),
                     pl.BlockSpec((tk, tn), lambda i, j, k: (k, j))]
        args += [a, w]
    if res is not None:
        in_specs.append(pl.BlockSpec((tm, tn), lambda i, j, k: (i, j)))
        args.append(res)
    scratch = [pltpu.VMEM((tm, tn), F32)] if nk > 1 else []
    return pl.pallas_call(
        functools.partial(_matmul_body, n_pairs=len(pairs), has_res=res is not None, nk=nk),
        grid=(m // tm, n // tn, nk),
        in_specs=in_specs,
        out_specs=pl.BlockSpec((tm, tn), lambda i, j, k: (i, j)),
        out_shape=jax.ShapeDtypeStruct((m, n), out_dtype),
        scratch_shapes=scratch,
        compiler_params=_cparams("parallel", "parallel", "arbitrary"),
        name=name,
    )(*args)


def _swiglu_up_body(a_ref, w1_ref, w3_ref, o_ref):
    a = a_ref[...]
    u = jnp.dot(a, w1_ref[...], preferred_element_type=F32)
    v = jnp.dot(a, w3_ref[...], preferred_element_type=F32)
    o_ref[...] = (u * jax.nn.sigmoid(u) * v).astype(o_ref.dtype)


def swiglu_up(a, w1, w3, tm=1024, tn=512):
    m, kdim = a.shape
    n = w1.shape[1]
    assert m % tm == 0 and n % tn == 0
    return pl.pallas_call(
        _swiglu_up_body,
        grid=(m // tm, n // tn),
        in_specs=[pl.BlockSpec((tm, kdim), lambda i, j: (i, 0)),
                  pl.BlockSpec((kdim, tn), lambda i, j: (0, j)),
                  pl.BlockSpec((kdim, tn), lambda i, j: (0, j))],
        out_specs=pl.BlockSpec((tm, tn), lambda i, j: (i, j)),
        out_shape=jax.ShapeDtypeStruct((m, n), BF16),
        compiler_params=_cparams("parallel", "parallel"),
        name="swiglu_up",
    )(a, w1, w3)


def _log_sigmoid(x):
    return jnp.minimum(x, 0.0) - jnp.log1p(jnp.exp(-jnp.abs(x)))


def _lane_cumsum(x, reverse):
    width = x.shape[-1]
    lane = lax.broadcasted_iota(jnp.int32, x.shape, 1)
    sh = 1
    while sh < width:
        if reverse:
            x = x + jnp.where(lane < width - sh, pltpu.roll(x, width - sh, axis=1), 0.0)
        else:
            x = x + jnp.where(lane >= sh, pltpu.roll(x, sh, axis=1), 0.0)
        sh *= 2
    return x


def _mlstm_body(q_ref, k_ref, v_ref, og_ref, g_ref, gb_ref, hg_ref, y_ref,
                rows_ref, cols_ref, hf_ref, hb_ref, ct_ref, n_ref, *, nc, chunk):
    dk, dv = MLSTM_DK, MLSTM_DV
    scale = dk ** -0.5

    g = g_ref[0, 0] + gb_ref[0]
    i_f = g[0]
    b_f = _lane_cumsum(_log_sigmoid(g[1]), reverse=False)
    i_b = g[2]
    b_b = _lane_cumsum(_log_sigmoid(g[3]), reverse=True)
    rows_ref[0] = b_f
    rows_ref[1] = i_f
    rows_ref[2] = b_b
    rows_ref[3] = i_b
    rowmat = jnp.concatenate(
        [b_f, i_f, b_b, i_b, jnp.zeros((LANES - 4 * nc, chunk), F32)], axis=0)
    colmat = rowmat.T
    for c in range(nc):
        cols_ref[c] = colmat if c == 0 else pltpu.roll(colmat, LANES - c, axis=1)

    ct_ref[...] = jnp.zeros_like(ct_ref)
    n_ref[...] = jnp.zeros_like(n_ref)

    row_id = lax.broadcasted_iota(jnp.int32, (chunk, chunk), 0)
    col_id = lax.broadcasted_iota(jnp.int32, (chunk, chunk), 1)
    causal = (row_id >= col_id, row_id <= col_id)

    def chunk_step(c, m, direction, h_ref):
        r0 = pl.multiple_of(c * chunk, chunk)
        qf = q_ref[0, pl.ds(r0, chunk), :] * scale
        kf = k_ref[0, pl.ds(r0, chunk), :]
        vf = v_ref[0, pl.ds(r0, chunk), :]
        qb, kb, vb = qf.astype(BF16), kf.astype(BF16), vf.astype(BF16)
        cols = cols_ref[c]
        b_col = cols[:, 2 * direction * nc:2 * direction * nc + 1]
        i_col = cols[:, (2 * direction + 1) * nc:(2 * direction + 1) * nc + 1]
        b_row = rows_ref[2 * direction, pl.ds(c, 1), :]
        i_row = rows_ref[2 * direction + 1, pl.ds(c, 1), :]
        total = b_row[:, chunk - 1:chunk] if direction == 0 else b_row[:, 0:1]

        dmat = jnp.where(causal[direction], b_col + (i_row - b_row), NEG_BIG)
        inter = b_col + m
        m_out = jnp.maximum(inter, jnp.max(dmat, axis=-1, keepdims=True))
        w_intra = jnp.exp(dmat - m_out)
        w_inter = jnp.exp(inter - m_out)
        s = lax.dot_general(qb, kb, (((1,), (1,)), ((), ())),
                            preferred_element_type=F32) * w_intra
        ct = ct_ref[direction]
        nrow = n_ref[direction]
        num = (jnp.dot(s.astype(BF16), vb, preferred_element_type=F32)
               + w_inter * jnp.dot(qb, ct.astype(BF16), preferred_element_type=F32))
        den = (jnp.sum(s, axis=-1, keepdims=True)
               + w_inter * jnp.sum(qf * nrow, axis=-1, keepdims=True))
        inv = 1.0 / jnp.maximum(jnp.abs(den), jnp.exp(-m_out))
        h_ref[pl.ds(r0, chunk), :] = num * inv

        src = total - b_col + i_col
        m_new = jnp.maximum(total + m, jnp.max(src, axis=0, keepdims=True))
        w_src = jnp.exp(src - m_new)
        decay = jnp.exp(total + m - m_new)
        wv = (w_src * vf).astype(BF16)
        ct_ref[direction] = decay * ct + lax.dot_general(
            kb, wv, (((0,), (0,)), ((), ())), preferred_element_type=F32)
        n_ref[direction] = decay * nrow + jnp.sum(w_src * kf, axis=0, keepdims=True)
        return m_new

    def body(c, carry):
        m_f, m_b = carry
        m_f = chunk_step(c, m_f, 0, hf_ref)
        m_b = chunk_step(nc - 1 - c, m_b, 1, hb_ref)
        return m_f, m_b

    m0 = jnp.full((1, 1), NEG_BIG, F32)
    lax.fori_loop(0, nc, body, (m0, m0))

    hg = hg_ref[0]

    def finish(c, carry):
        r0 = pl.multiple_of(c * chunk, chunk)
        hm = hf_ref[pl.ds(r0, chunk), :] + hb_ref[pl.ds(r0, chunk), :]
        ms = jnp.mean(hm * hm, axis=-1, keepdims=True)
        hn = hm * lax.rsqrt(ms + RMS_EPS) * hg
        gate = jax.nn.sigmoid(og_ref[0, pl.ds(r0, chunk), :])
        y_ref[0, pl.ds(r0, chunk), :] = (hn * gate).astype(y_ref.dtype)
        return carry

    lax.fori_loop(0, nc, finish, 0)


def mlstm_mixer(z, gates_rows, gate_b_rows, head_g, chunk=MLSTM_CHUNK):
    b, s, _ = z.shape
    h, dk, dv = MLSTM_HEADS, MLSTM_DK, MLSTM_DV
    nc = s // chunk
    assert s % chunk == 0 and 4 * nc <= LANES and chunk % LANES == 0
    kq = h * dk // dk
    kv = 2 * h * dk // dv
    ko = kv + h
    return pl.pallas_call(
        functools.partial(_mlstm_body, nc=nc, chunk=chunk),
        grid=(b, h),
        in_specs=[
            pl.BlockSpec((1, s, dk), lambda i, j: (i, 0, j)),
            pl.BlockSpec((1, s, dk), lambda i, j: (i, 0, kq + j)),
            pl.BlockSpec((1, s, dv), lambda i, j: (i, 0, kv + j)),
            pl.BlockSpec((1, s, dv), lambda i, j: (i, 0, ko + j)),
            pl.BlockSpec((1, 1, 4, nc, chunk), lambda i, j: (i, j, 0, 0, 0)),
            pl.BlockSpec((1, 4, 1, chunk), lambda i, j: (j, 0, 0, 0)),
            pl.BlockSpec((1, 1, dv), lambda i, j: (j, 0, 0)),
        ],
        out_specs=pl.BlockSpec((1, s, dv), lambda i, j: (i, 0, j)),
        out_shape=jax.ShapeDtypeStruct((b, s, h * dv), BF16),
        scratch_shapes=[
            pltpu.VMEM((4, nc, chunk), F32),
            pltpu.VMEM((nc, chunk, LANES), F32),
            pltpu.VMEM((s, dv), F32),
            pltpu.VMEM((s, dv), F32),
            pltpu.VMEM((2, dk, dv), F32),
            pltpu.VMEM((2, 1, dk), F32),
        ],
        compiler_params=_cparams("parallel", "parallel"),
        name="mlstm",
    )(z, z, z, z, gates_rows, gate_b_rows, head_g)


RG_TILE = 512
RG_PAD = SUBLANES


def _softplus(x):
    return jnp.maximum(x, 0.0) + jnp.log1p(jnp.exp(-jnp.abs(x)))


def _gelu_tanh(x):
    c = math.sqrt(2.0 / math.pi)
    return x * (0.5 * (1.0 + jnp.tanh(c * (x + 0.044715 * (x * x * x)))))


def _rglru_body(x_ref, gr_ref, cw_ref, cb_ref, wa_ref, wx_ref, ba_ref, bx_ref, lam_ref, y_ref,
                xpad_ref, a_ref, u_ref, *, s):
    nt = s // RG_TILE
    zeros = jnp.zeros((RG_PAD, LANES), F32)
    xpad_ref[pl.ds(0, RG_PAD), :] = zeros
    xpad_ref[pl.ds(RG_PAD + s, RG_PAD), :] = zeros

    def copy_in(t, carry):
        r0 = pl.multiple_of(t * RG_TILE, RG_TILE)
        xpad_ref[pl.ds(RG_PAD + r0, RG_TILE), :] = x_ref[0, pl.ds(r0, RG_TILE), :]
        return carry

    lax.fori_loop(0, nt, copy_in, 0)

    cw = cw_ref[...]
    cb = cb_ref[...]
    neg_sp = [-RGLRU_C * _softplus(-lam_ref[d:d + 1, :]) for d in range(2)]

    def gates(t, carry):
        r0 = pl.multiple_of(t * RG_TILE, RG_TILE)
        xc = None
        for j in range(CONV_WIDTH):
            tap = xpad_ref[pl.ds(r0 + RG_PAD - CONV_LEFT + j, RG_TILE), :] * cw[j:j + 1, :]
            xc = tap if xc is None else xc + tap
        xc = xc + cb
        xcb = xc.astype(BF16)
        for d in range(2):
            r = jax.nn.sigmoid(jnp.dot(xcb, wa_ref[d, 0], preferred_element_type=F32)
                               + ba_ref[d:d + 1, :])
            i = jax.nn.sigmoid(jnp.dot(xcb, wx_ref[d, 0], preferred_element_type=F32)
                               + bx_ref[d:d + 1, :])
            log_a = neg_sp[d] * r
            a = jnp.exp(log_a)
            one_minus_a2 = -jnp.tanh(log_a) * (1.0 + a * a)
            a_ref[d, pl.ds(r0, RG_TILE), :] = a
            u_ref[d, pl.ds(r0, RG_TILE), :] = jnp.sqrt(one_minus_a2) * (i * xc)
        return carry

    lax.fori_loop(0, nt, gates, 0)

    row = lax.broadcasted_iota(jnp.int32, (SUBLANES, LANES), 0)
    nblk = s // SUBLANES

    def block_scan(a, u, reverse):
        sh = 1
        while sh < SUBLANES:
            if reverse:
                keep = row < SUBLANES - sh
                a_s = jnp.where(keep, pltpu.roll(a, SUBLANES - sh, axis=0), 1.0)
                u_s = jnp.where(keep, pltpu.roll(u, SUBLANES - sh, axis=0), 0.0)
            else:
                keep = row >= sh
                a_s = jnp.where(keep, pltpu.roll(a, sh, axis=0), 1.0)
                u_s = jnp.where(keep, pltpu.roll(u, sh, axis=0), 0.0)
            u = a * u_s + u
            a = a * a_s
            sh *= 2
        return a, u

    def scan(j, carry):
        h_f, h_b = carry
        rf = pl.multiple_of(j * SUBLANES, SUBLANES)
        a, u = block_scan(a_ref[0, pl.ds(rf, SUBLANES), :], u_ref[0, pl.ds(rf, SUBLANES), :], False)
        blk = a * h_f + u
        u_ref[0, pl.ds(rf, SUBLANES), :] = blk
        h_f = blk[SUBLANES - 1:SUBLANES, :]
        rb = pl.multiple_of((nblk - 1 - j) * SUBLANES, SUBLANES)
        a, u = block_scan(a_ref[1, pl.ds(rb, SUBLANES), :], u_ref[1, pl.ds(rb, SUBLANES), :], True)
        blk = a * h_b + u
        u_ref[1, pl.ds(rb, SUBLANES), :] = blk
        h_b = blk[0:1, :]
        return h_f, h_b

    h0 = jnp.zeros((1, LANES), F32)
    lax.fori_loop(0, nblk, scan, (h0, h0), unroll=8)

    def finish(t, carry):
        r0 = pl.multiple_of(t * RG_TILE, RG_TILE)
        hr = u_ref[0, pl.ds(r0, RG_TILE), :] + u_ref[1, pl.ds(r0, RG_TILE), :]
        y_ref[0, pl.ds(r0, RG_TILE), :] = (
            hr * _gelu_tanh(gr_ref[0, pl.ds(r0, RG_TILE), :])).astype(y_ref.dtype)
        return carry

    lax.fori_loop(0, nt, finish, 0)


def rglru_mixer(z, xr_block0, gr_block0, conv_w, conv_b, wa, wx, ba, bx, lam):
    b, s, _ = z.shape
    width = RNN_BLOCKS * RNN_BLOCK
    assert s % RG_TILE == 0
    return pl.pallas_call(
        functools.partial(_rglru_body, s=s),
        grid=(b, RNN_BLOCKS),
        in_specs=[
            pl.BlockSpec((1, s, RNN_BLOCK), lambda i, j: (i, 0, xr_block0 + j)),
            pl.BlockSpec((1, s, RNN_BLOCK), lambda i, j: (i, 0, gr_block0 + j)),
            pl.BlockSpec((CONV_WIDTH, RNN_BLOCK), lambda i, j: (0, j)),
            pl.BlockSpec((1, RNN_BLOCK), lambda i, j: (0, j)),
            pl.BlockSpec((2, 1, RNN_BLOCK, RNN_BLOCK), lambda i, j: (0, j, 0, 0)),
            pl.BlockSpec((2, 1, RNN_BLOCK, RNN_BLOCK), lambda i, j: (0, j, 0, 0)),
            pl.BlockSpec((2, RNN_BLOCK), lambda i, j: (0, j)),
            pl.BlockSpec((2, RNN_BLOCK), lambda i, j: (0, j)),
            pl.BlockSpec((2, RNN_BLOCK), lambda i, j: (0, j)),
        ],
        out_specs=pl.BlockSpec((1, s, RNN_BLOCK), lambda i, j: (i, 0, j)),
        out_shape=jax.ShapeDtypeStruct((b, s, width), BF16),
        scratch_shapes=[
            pltpu.VMEM((s + 2 * RG_PAD, RNN_BLOCK), F32),
            pltpu.VMEM((2, s, RNN_BLOCK), F32),
            pltpu.VMEM((2, s, RNN_BLOCK), F32),
        ],
        compiler_params=_cparams("parallel", "parallel"),
        name="rglru",
    )(z, z, conv_w, conv_b, wa, wx, ba, bx, lam)


ATTN_TQ = 128
ATTN_TK = ATTN_TQ + 2 * ATTN_HALF
ATTN_MERGE_TILE = 256


def _attn_body(q_ref, k_ref, v_ref, slope_ref, o_ref, bias_ref, og_ref, lse_ref, *, s):
    scale = ATTN_DH ** -0.5
    slope = slope_ref[0, 0:1, 0:1]
    qi = lax.broadcasted_iota(jnp.int32, (ATTN_TQ, ATTN_TK), 0)
    kj = lax.broadcasted_iota(jnp.int32, (ATTN_TQ, ATTN_TK), 1)
    for g, dil in enumerate(ATTN_DILATIONS):
        for e in range(3):
            rel = jnp.abs(kj - e * ATTN_HALF - qi)
            pen = slope * (rel * dil).astype(F32)
            bias_ref[g, e] = jnp.where(rel <= ATTN_HALF, -pen, NEG_BIG)

    ones = jnp.ones((ATTN_TK, ATTN_DH), BF16)

    for g, dil in enumerate(ATTN_DILATIONS):
        sp = s // dil
        nqb = sp // ATTN_TQ

        def rows(start, size, dil=dil):
            if dil == 1:
                return pl.ds(start, size)
            return pl.ds(start, size, stride=dil)

        def q_block(t, carry, g=g, dil=dil, sp=sp, nqb=nqb, rows=rows):
            r = t // nqb
            p0 = (t % nqb) * ATTN_TQ
            kstart = jnp.clip(p0 - ATTN_HALF, 0, sp - ATTN_TK)
            e = (p0 - kstart) // ATTN_HALF
            qrows = rows(r + dil * p0, ATTN_TQ)
            krows = rows(r + dil * kstart, ATTN_TK)
            qb = (q_ref[0, qrows, :] * scale).astype(BF16)
            kb = k_ref[0, krows, :].astype(BF16)
            vb = v_ref[0, krows, :].astype(BF16)
            sc = lax.dot_general(qb, kb, (((1,), (1,)), ((), ())),
                                 preferred_element_type=F32) + bias_ref[g, e]
            m = jnp.max(sc, axis=-1, keepdims=True)
            p = jnp.exp(sc - m).astype(BF16)
            pv = jnp.dot(p, jnp.concatenate([vb, ones], axis=1), preferred_element_type=F32)
            den = pv[:, ATTN_DH:]
            og_ref[g, qrows, :] = pv[:, :ATTN_DH] / den
            lse_ref[g, qrows, :] = m + jnp.log(den)
            return carry

        lax.fori_loop(0, dil * nqb, q_block, 0)

    def merge(t, carry):
        r0 = pl.multiple_of(t * ATTN_MERGE_TILE, ATTN_MERGE_TILE)
        sl = pl.ds(r0, ATTN_MERGE_TILE)
        l0, l1, l2 = lse_ref[0, sl, :], lse_ref[1, sl, :], lse_ref[2, sl, :]
        mx = jnp.maximum(jnp.maximum(l0, l1), l2)
        e0, e1, e2 = jnp.exp(l0 - mx), jnp.exp(l1 - mx), jnp.exp(l2 - mx)
        inv = 1.0 / (e0 + e1 + e2)
        o = (e0 * inv) * og_ref[0, sl, :] + (e1 * inv) * og_ref[1, sl, :] + (e2 * inv) * og_ref[2, sl, :]
        o_ref[0, sl, :] = o.astype(o_ref.dtype)
        return carry

    lax.fori_loop(0, s // ATTN_MERGE_TILE, merge, 0)


def dilated_attention(qkv, slopes):
    b, s, _ = qkv.shape
    h, dh = ATTN_HEADS, ATTN_DH
    assert s % (max(ATTN_DILATIONS) * ATTN_TK) == 0 and s % ATTN_MERGE_TILE == 0
    return pl.pallas_call(
        functools.partial(_attn_body, s=s),
        grid=(b, h),
        in_specs=[
            pl.BlockSpec((1, s, dh), lambda i, j: (i, 0, j)),
            pl.BlockSpec((1, s, dh), lambda i, j: (i, 0, h + j)),
            pl.BlockSpec((1, s, dh), lambda i, j: (i, 0, 2 * h + j)),
            pl.BlockSpec((1, SUBLANES, LANES), lambda i, j: (j, 0, 0)),
        ],
        out_specs=pl.BlockSpec((1, s, dh), lambda i, j: (i, 0, j)),
        out_shape=jax.ShapeDtypeStruct((b, s, h * dh), BF16),
        scratch_shapes=[
            pltpu.VMEM((len(ATTN_DILATIONS), 3, ATTN_TQ, ATTN_TK), F32),
            pltpu.VMEM((len(ATTN_DILATIONS), s, dh), F32),
            pltpu.VMEM((len(ATTN_DILATIONS), s, dh), F32),
        ],
        compiler_params=_cparams("parallel", "parallel"),
        name="dilated_attn",
    )(qkv, qkv, qkv, slopes)


def _ffn(x, norm_g, w1, w3, w2):
    h = rmsnorm(x, norm_g, BF16)
    u = swiglu_up(h, w1.astype(BF16), w3.astype(BF16))
    return matmul([(u, w2.astype(BF16))], res=x, tk=1408, name="ffn_down")


def _even_layer(x, bsz, seq, norm_g, w_in, gate_b, conv_w, conv_b, rg_wa, rg_ba, rg_wx, rg_bx,
                rg_lam, head_g, w_out):
    heads, dk, dv = MLSTM_HEADS, MLSTM_DK, MLSTM_DV
    n_qkvo = 2 * heads * dk + 2 * heads * dv
    n_gate = 4 * heads
    width = RNN_BLOCKS * RNN_BLOCK
    h = rmsnorm(x, norm_g, BF16)
    w_main = jnp.concatenate([w_in[:, :n_qkvo], w_in[:, n_qkvo + n_gate:]], axis=1).astype(BF16)
    w_gate = jnp.pad(w_in[:, n_qkvo:n_qkvo + n_gate], ((0, 0), (0, LANES - n_gate))).astype(BF16)
    z = matmul([(h, w_main)], name="in_proj").reshape(bsz, seq, -1)
    zg = matmul([(h, w_gate)], tn=LANES, name="gate_proj")

    chunk = MLSTM_CHUNK
    nc = seq // chunk
    gates_rows = jnp.transpose(zg[:, :n_gate].reshape(bsz, nc, chunk, 4, heads), (0, 4, 3, 1, 2))
    gate_b_rows = jnp.broadcast_to(
        jnp.transpose(gate_b.astype(F32).reshape(4, heads))[:, :, None, None], (heads, 4, 1, chunk))
    y_a = mlstm_mixer(z, gates_rows, gate_b_rows, head_g.astype(F32).reshape(heads, 1, dv), chunk)

    y_b = rglru_mixer(z, n_qkvo // RNN_BLOCK, (n_qkvo + width) // RNN_BLOCK,
                      conv_w.astype(F32), conv_b.astype(F32).reshape(1, width),
                      rg_wa.astype(BF16), rg_wx.astype(BF16),
                      rg_ba.astype(F32), rg_bx.astype(F32), rg_lam.astype(F32))

    w_out_b = w_out.astype(BF16)
    m = bsz * seq
    return matmul([(y_a.reshape(m, -1), w_out_b[:heads * dv]),
                   (y_b.reshape(m, -1), w_out_b[heads * dv:])], res=x, name="out_proj")


def _odd_layer(x, bsz, seq, norm_g, w_qkv, w_o):
    h = rmsnorm(x, norm_g, BF16)
    qkv = matmul([(h, w_qkv.astype(BF16))], name="qkv_proj").reshape(bsz, seq, -1)
    slopes = jnp.exp2(-ALIBI_MAX_BIAS * jnp.arange(1, ATTN_HEADS + 1, dtype=F32) / ATTN_HEADS)
    slopes = jnp.broadcast_to(slopes[:, None, None], (ATTN_HEADS, SUBLANES, LANES))
    o = dilated_attention(qkv, slopes)
    return matmul([(o.reshape(bsz * seq, -1), w_o.astype(BF16))], res=x, name="attn_out_proj")


def kernel(x, e_norm, e_w_in, e_gate_b, e_conv_w, e_conv_b, e_rg_wa, e_rg_ba, e_rg_wx, e_rg_bx,
           e_rg_lam, e_head_g, e_w_out, o_norm, o_w_qkv, o_w_o, f_norm, f_w1, f_w3, f_w2,
           final_norm):
    bsz, seq, d = x.shape
    depth = f_norm.shape[0]
    xs = x.reshape(bsz * seq, d).astype(F32)
    for l in range(depth):
        if l % 2 == 0:
            e = l // 2
            xs = _even_layer(xs, bsz, seq, e_norm[e], e_w_in[e], e_gate_b[e], e_conv_w[e],
                             e_conv_b[e], e_rg_wa[e], e_rg_ba[e], e_rg_wx[e], e_rg_bx[e],
                             e_rg_lam[e], e_head_g[e], e_w_out[e])
        else:
            o = l // 2
            xs = _odd_layer(xs, bsz, seq, o_norm[o], o_w_qkv[o], o_w_o[o])
        xs = _ffn(xs, f_norm[l], f_w1[l], f_w3[l], f_w2[l])
    return rmsnorm(xs, final_norm, x.dtype).reshape(bsz, seq, d)
```

```python
import functools
import math

import jax
import jax.numpy as jnp
from jax import lax
from jax.experimental import pallas as pl
from jax.experimental.pallas import tpu as pltpu

F32 = jnp.float32
BF16 = jnp.bfloat16

RMS_EPS = 1e-6
NEG_BIG = -1e30

MLSTM_HEADS = 4
MLSTM_DK = 128
MLSTM_DV = 256
MLSTM_CHUNK = 256
RNN_BLOCKS = 8
RNN_BLOCK = 128
CONV_WIDTH = 4
CONV_LEFT = 2
RGLRU_C = 8.0
ATTN_HEADS = 16
ATTN_DH = 128
ATTN_HALF = 64
ATTN_DILATIONS = (1, 4, 16)
ALIBI_MAX_BIAS = 8.0

LANES = 128
SUBLANES = 8
VMEM_LIMIT_BYTES = 56 * 1024 * 1024


def _cparams(*semantics):
    return pltpu.CompilerParams(dimension_semantics=semantics,
                                vmem_limit_bytes=VMEM_LIMIT_BYTES)


def _rmsnorm_body(x_ref, g_ref, o_ref):
    x = x_ref[...]
    ms = jnp.mean(x * x, axis=-1, keepdims=True)
    o_ref[...] = (x * lax.rsqrt(ms + RMS_EPS) * g_ref[...]).astype(o_ref.dtype)


def rmsnorm(x, g, out_dtype, tm=512):
    m, d = x.shape
    return pl.pallas_call(
        _rmsnorm_body,
        grid=(m // tm,),
        in_specs=[pl.BlockSpec((tm, d), lambda i: (i, 0)),
                  pl.BlockSpec((1, d), lambda i: (0, 0))],
        out_specs=pl.BlockSpec((tm, d), lambda i: (i, 0)),
        out_shape=jax.ShapeDtypeStruct((m, d), out_dtype),
        compiler_params=_cparams("parallel"),
        name="rmsnorm",
    )(x, g.reshape(1, d).astype(F32))


def _matmul_body(*refs, n_pairs, has_res, nk):
    pairs = [(refs[2 * p], refs[2 * p + 1]) for p in range(n_pairs)]
    rest = refs[2 * n_pairs:]
    r_ref = rest[0] if has_res else None
    o_ref = rest[1] if has_res else rest[0]
    acc_ref = rest[-1] if nk > 1 else None

    part = None
    for a_ref, w_ref in pairs:
        d = jnp.dot(a_ref[...], w_ref[...], preferred_element_type=F32)
        part = d if part is None else part + d

    def finish(acc):
        if has_res:
            acc = acc + r_ref[...]
        o_ref[...] = acc.astype(o_ref.dtype)

    if nk == 1:
        finish(part)
        return

    k = pl.program_id(2)

    @pl.when(k == 0)
    def _():
        acc_ref[...] = part

    @pl.when(jnp.logical_and(k > 0, k < nk - 1))
    def _():
        acc_ref[...] += part

    @pl.when(k == nk - 1)
    def _():
        finish(acc_ref[...] + part)


def matmul(pairs, res=None, out_dtype=F32, tm=1024, tn=1024, tk=None, name="matmul"):
    m, kdim = pairs[0][0].shape
    n = pairs[0][1].shape[1]
    tk = kdim if tk is None else tk
    tn = min(tn, n)
    nk = kdim // tk
    assert m % tm == 0 and n % tn == 0 and kdim % tk == 0
    in_specs, args = [], []
    for a, w in pairs:
        in_specs += [pl.BlockSpec((tm, tk), lambda i, j, k: (i, k)),
                     pl.BlockSpec((tk, tn), lambda i, j, k: (k, j))]
        args += [a, w]
    if res is not None:
        in_specs.append(pl.BlockSpec((tm, tn), lambda i, j, k: (i, j)))
        args.append(res)
    scratch = [pltpu.VMEM((tm, tn), F32)] if nk > 1 else []
    return pl.pallas_call(
        functools.partial(_matmul_body, n_pairs=len(pairs), has_res=res is not None, nk=nk),
        grid=(m // tm, n // tn, nk),
        in_specs=in_specs,
        out_specs=pl.BlockSpec((tm, tn), lambda i, j, k: (i, j)),
        out_shape=jax.ShapeDtypeStruct((m, n), out_dtype),
        scratch_shapes=scratch,
        compiler_params=_cparams("parallel", "parallel", "arbitrary"),
        name=name,
    )(*args)


def _swiglu_up_body(a_ref, w1_ref, w3_ref, o_ref):
    a = a_ref[...]
    u = jnp.dot(a, w1_ref[...], preferred_element_type=F32)
    v = jnp.dot(a, w3_ref[...], preferred_element_type=F32)
    o_ref[...] = (u * jax.nn.sigmoid(u) * v).astype(o_ref.dtype)


def swiglu_up(a, w1, w3, tm=1024, tn=512):
    m, kdim = a.shape
    n = w1.shape[1]
    assert m % tm == 0 and n % tn == 0
    return pl.pallas_call(
        _swiglu_up_body,
        grid=(m // tm, n // tn),
        in_specs=[pl.BlockSpec((tm, kdim), lambda i, j: (i, 0)),
                  pl.BlockSpec((kdim, tn), lambda i, j: (0, j)),
                  pl.BlockSpec((kdim, tn), lambda i, j: (0, j))],
        out_specs=pl.BlockSpec((tm, tn), lambda i, j: (i, j)),
        out_shape=jax.ShapeDtypeStruct((m, n), BF16),
        compiler_params=_cparams("parallel", "parallel"),
        name="swiglu_up",
    )(a, w1, w3)


def _log_sigmoid(x):
    return jnp.minimum(x, 0.0) - jnp.log1p(jnp.exp(-jnp.abs(x)))


def _lane_cumsum(x, reverse):
    width = x.shape[-1]
    lane = lax.broadcasted_iota(jnp.int32, x.shape, 1)
    sh = 1
    while sh < width:
        if reverse:
            x = x + jnp.where(lane < width - sh, pltpu.roll(x, width - sh, axis=1), 0.0)
        else:
            x = x + jnp.where(lane >= sh, pltpu.roll(x, sh, axis=1), 0.0)
        sh *= 2
    return x


def _mlstm_body(q_ref, k_ref, v_ref, og_ref, g_ref, gb_ref, hg_ref, y_ref,
                rows_ref, cols_ref, hf_ref, hb_ref, ct_ref, n_ref, *, nc, chunk):
    dk, dv = MLSTM_DK, MLSTM_DV
    scale = dk ** -0.5

    g = g_ref[0, 0] + gb_ref[0]
    i_f = g[0]
    b_f = _lane_cumsum(_log_sigmoid(g[1]), reverse=False)
    i_b = g[2]
    b_b = _lane_cumsum(_log_sigmoid(g[3]), reverse=True)
    rows_ref[0] = b_f
    rows_ref[1] = i_f
    rows_ref[2] = b_b
    rows_ref[3] = i_b
    rowmat = jnp.concatenate(
        [b_f, i_f, b_b, i_b, jnp.zeros((LANES - 4 * nc, chunk), F32)], axis=0)
    colmat = rowmat.T
    for c in range(nc):
        cols_ref[c] = colmat if c == 0 else pltpu.roll(colmat, LANES - c, axis=1)

    ct_ref[...] = jnp.zeros_like(ct_ref)
    n_ref[...] = jnp.zeros_like(n_ref)

    row_id = lax.broadcasted_iota(jnp.int32, (chunk, chunk), 0)
    col_id = lax.broadcasted_iota(jnp.int32, (chunk, chunk), 1)
    causal = (row_id >= col_id, row_id <= col_id)

    def chunk_step(c, m, direction, h_ref):
        r0 = pl.multiple_of(c * chunk, chunk)
        qf = q_ref[0, pl.ds(r0, chunk), :] * scale
        kf = k_ref[0, pl.ds(r0, chunk), :]
        vf = v_ref[0, pl.ds(r0, chunk), :]
        qb, kb, vb = qf.astype(BF16), kf.astype(BF16), vf.astype(BF16)
        cols = cols_ref[c]
        b_col = cols[:, 2 * direction * nc:2 * direction * nc + 1]
        i_col = cols[:, (2 * direction + 1) * nc:(2 * direction + 1) * nc + 1]
        b_row = rows_ref[2 * direction, pl.ds(c, 1), :]
        i_row = rows_ref[2 * direction + 1, pl.ds(c, 1), :]
        total = b_row[:, chunk - 1:chunk] if direction == 0 else b_row[:, 0:1]

        dmat = jnp.where(causal[direction], b_col + (i_row - b_row), NEG_BIG)
        inter = b_col + m
        m_out = jnp.maximum(inter, jnp.max(dmat, axis=-1, keepdims=True))
        w_intra = jnp.exp(dmat - m_out)
        w_inter = jnp.exp(inter - m_out)
        s = lax.dot_general(qb, kb, (((1,), (1,)), ((), ())),
                            preferred_element_type=F32) * w_intra
        ct = ct_ref[direction]
        nrow = n_ref[direction]
        num = (jnp.dot(s.astype(BF16), vb, preferred_element_type=F32)
               + w_inter * jnp.dot(qb, ct.astype(BF16), preferred_element_type=F32))
        den = (jnp.sum(s, axis=-1, keepdims=True)
               + w_inter * jnp.sum(qf * nrow, axis=-1, keepdims=True))
        inv = 1.0 / jnp.maximum(jnp.abs(den), jnp.exp(-m_out))
        h_ref[pl.ds(r0, chunk), :] = num * inv

        src = total - b_col + i_col
        m_new = jnp.maximum(total + m, jnp.max(src, axis=0, keepdims=True))
        w_src = jnp.exp(src - m_new)
        decay = jnp.exp(total + m - m_new)
        wv = (w_src * vf).astype(BF16)
        ct_ref[direction] = decay * ct + lax.dot_general(
            kb, wv, (((0,), (0,)), ((), ())), preferred_element_type=F32)
        n_ref[direction] = decay * nrow + jnp.sum(w_src * kf, axis=0, keepdims=True)
        return m_new

    def body(c, carry):
        m_f, m_b = carry
        m_f = chunk_step(c, m_f, 0, hf_ref)
        m_b = chunk_step(nc - 1 - c, m_b, 1, hb_ref)
        return m_f, m_b

    m0 = jnp.full((1, 1), NEG_BIG, F32)
    lax.fori_loop(0, nc, body, (m0, m0))

    hg = hg_ref[0]

    def finish(c, carry):
        r0 = pl.multiple_of(c * chunk, chunk)
        hm = hf_ref[pl.ds(r0, chunk), :] + hb_ref[pl.ds(r0, chunk), :]
        ms = jnp.mean(hm * hm, axis=-1, keepdims=True)
        hn = hm * lax.rsqrt(ms + RMS_EPS) * hg
        gate = jax.nn.sigmoid(og_ref[0, pl.ds(r0, chunk), :])
        y_ref[0, pl.ds(r0, chunk), :] = (hn * gate).astype(y_ref.dtype)
        return carry

    lax.fori_loop(0, nc, finish, 0)


def mlstm_mixer(z, gates_rows, gate_b_rows, head_g, chunk=MLSTM_CHUNK):
    b, s, _ = z.shape
    h, dk, dv = MLSTM_HEADS, MLSTM_DK, MLSTM_DV
    nc = s // chunk
    assert s % chunk == 0 and 4 * nc <= LANES and chunk % LANES == 0
    kq = h * dk // dk
    kv = 2 * h * dk // dv
    ko = kv + h
    return pl.pallas_call(
        functools.partial(_mlstm_body, nc=nc, chunk=chunk),
        grid=(b, h),
        in_specs=[
            pl.BlockSpec((1, s, dk), lambda i, j: (i, 0, j)),
            pl.BlockSpec((1, s, dk), lambda i, j: (i, 0, kq + j)),
            pl.BlockSpec((1, s, dv), lambda i, j: (i, 0, kv + j)),
            pl.BlockSpec((1, s, dv), lambda i, j: (i, 0, ko + j)),
            pl.BlockSpec((1, 1, 4, nc, chunk), lambda i, j: (i, j, 0, 0, 0)),
            pl.BlockSpec((1, 4, 1, chunk), lambda i, j: (j, 0, 0, 0)),
            pl.BlockSpec((1, 1, dv), lambda i, j: (j, 0, 0)),
        ],
        out_specs=pl.BlockSpec((1, s, dv), lambda i, j: (i, 0, j)),
        out_shape=jax.ShapeDtypeStruct((b, s, h * dv), BF16),
        scratch_shapes=[
            pltpu.VMEM((4, nc, chunk), F32),
            pltpu.VMEM((nc, chunk, LANES), F32),
            pltpu.VMEM((s, dv), F32),
            pltpu.VMEM((s, dv), F32),
            pltpu.VMEM((2, dk, dv), F32),
            pltpu.VMEM((2, 1, dk), F32),
        ],
        compiler_params=_cparams("parallel", "parallel"),
        name="mlstm",
    )(z, z, z, z, gates_rows, gate_b_rows, head_g)


RG_TILE = 512
RG_PAD = SUBLANES
RG_SCAN_UNROLL = 8


def _softplus(x):
    return jnp.maximum(x, 0.0) + jnp.log1p(jnp.exp(-jnp.abs(x)))


def _gelu_tanh(x):
    c = math.sqrt(2.0 / math.pi)
    return x * (0.5 * (1.0 + jnp.tanh(c * (x + 0.044715 * (x * x * x)))))


def _rglru_body(x_ref, gr_ref, cw_ref, cb_ref, wa_ref, wx_ref, ba_ref, bx_ref, lam_ref, y_ref,
                xpad_ref, a_ref, u_ref, h_ref, *, s):
    nt = s // RG_TILE
    zeros = jnp.zeros((RG_PAD, LANES), F32)
    xpad_ref[pl.ds(0, RG_PAD), :] = zeros
    xpad_ref[pl.ds(RG_PAD + s, RG_PAD), :] = zeros

    def copy_in(t, carry):
        r0 = pl.multiple_of(t * RG_TILE, RG_TILE)
        xpad_ref[pl.ds(RG_PAD + r0, RG_TILE), :] = x_ref[0, pl.ds(r0, RG_TILE), :]
        return carry

    lax.fori_loop(0, nt, copy_in, 0)

    cw = cw_ref[...]
    cb = cb_ref[...]
    neg_sp = [-RGLRU_C * _softplus(-lam_ref[d:d + 1, :]) for d in range(2)]

    def gates(t, carry):
        r0 = pl.multiple_of(t * RG_TILE, RG_TILE)
        xc = None
        for j in range(CONV_WIDTH):
            tap = xpad_ref[pl.ds(r0 + RG_PAD - CONV_LEFT + j, RG_TILE), :] * cw[j:j + 1, :]
            xc = tap if xc is None else xc + tap
        xc = xc + cb
        xcb = xc.astype(BF16)
        for d in range(2):
            r = jax.nn.sigmoid(jnp.dot(xcb, wa_ref[d, 0], preferred_element_type=F32)
                               + ba_ref[d:d + 1, :])
            i = jax.nn.sigmoid(jnp.dot(xcb, wx_ref[d, 0], preferred_element_type=F32)
                               + bx_ref[d:d + 1, :])
            log_a = neg_sp[d] * r
            a = jnp.exp(log_a)
            one_minus_a2 = -jnp.tanh(log_a) * (1.0 + a * a)
            a, u = block_scan(a, jnp.sqrt(one_minus_a2) * (i * xc), reverse=d == 1)
            a_ref[d, pl.ds(r0, RG_TILE), :] = a
            u_ref[d, pl.ds(r0, RG_TILE), :] = u
        return carry

    row = lax.broadcasted_iota(jnp.int32, (RG_TILE // SUBLANES, SUBLANES, LANES), 1)

    def block_scan(a, u, reverse):
        a = a.reshape(RG_TILE // SUBLANES, SUBLANES, LANES)
        u = u.reshape(RG_TILE // SUBLANES, SUBLANES, LANES)
        sh = 1
        while sh < SUBLANES:
            keep = row < SUBLANES - sh if reverse else row >= sh
            amount = SUBLANES - sh if reverse else sh
            a_s = jnp.where(keep, pltpu.roll(a, amount, axis=1), 1.0)
            u_s = jnp.where(keep, pltpu.roll(u, amount, axis=1), 0.0)
            u = a * u_s + u
            a = a * a_s
            sh *= 2
        return a.reshape(RG_TILE, LANES), u.reshape(RG_TILE, LANES)

    lax.fori_loop(0, nt, gates, 0)

    nblk = s // SUBLANES

    def scan(j, carry):
        h_f, h_b = carry
        for d, last, h in ((0, SUBLANES - 1, h_f), (1, 0, h_b)):
            blk = j if d == 0 else nblk - 1 - j
            rows = pl.ds(pl.multiple_of(blk * SUBLANES, SUBLANES), SUBLANES)
            a, u = a_ref[d, rows, :], u_ref[d, rows, :]
            h_ref[d, rows, :] = a * h + u
            a_end = jnp.broadcast_to(a[last:last + 1, :], (SUBLANES, LANES))
            u_end = jnp.broadcast_to(u[last:last + 1, :], (SUBLANES, LANES))
            if d == 0:
                h_f = a_end * h + u_end
            else:
                h_b = a_end * h + u_end
        return h_f, h_b

    h0 = jnp.zeros((SUBLANES, LANES), F32)
    lax.fori_loop(0, nblk, scan, (h0, h0), unroll=RG_SCAN_UNROLL)

    def finish(t, carry):
        r0 = pl.multiple_of(t * RG_TILE, RG_TILE)
        hr = h_ref[0, pl.ds(r0, RG_TILE), :] + h_ref[1, pl.ds(r0, RG_TILE), :]
        y_ref[0, pl.ds(r0, RG_TILE), :] = (
            hr * _gelu_tanh(gr_ref[0, pl.ds(r0, RG_TILE), :])).astype(y_ref.dtype)
        return carry

    lax.fori_loop(0, nt, finish, 0)


def rglru_mixer(z, xr_block0, gr_block0, conv_w, conv_b, wa, wx, ba, bx, lam):
    b, s, _ = z.shape
    width = RNN_BLOCKS * RNN_BLOCK
    assert s % RG_TILE == 0
    return pl.pallas_call(
        functools.partial(_rglru_body, s=s),
        grid=(b, RNN_BLOCKS),
        in_specs=[
            pl.BlockSpec((1, s, RNN_BLOCK), lambda i, j: (i, 0, xr_block0 + j)),
            pl.BlockSpec((1, s, RNN_BLOCK), lambda i, j: (i, 0, gr_block0 + j)),
            pl.BlockSpec((CONV_WIDTH, RNN_BLOCK), lambda i, j: (0, j)),
            pl.BlockSpec((1, RNN_BLOCK), lambda i, j: (0, j)),
            pl.BlockSpec((2, 1, RNN_BLOCK, RNN_BLOCK), lambda i, j: (0, j, 0, 0)),
            pl.BlockSpec((2, 1, RNN_BLOCK, RNN_BLOCK), lambda i, j: (0, j, 0, 0)),
            pl.BlockSpec((2, RNN_BLOCK), lambda i, j: (0, j)),
            pl.BlockSpec((2, RNN_BLOCK), lambda i, j: (0, j)),
            pl.BlockSpec((2, RNN_BLOCK), lambda i, j: (0, j)),
        ],
        out_specs=pl.BlockSpec((1, s, RNN_BLOCK), lambda i, j: (i, 0, j)),
        out_shape=jax.ShapeDtypeStruct((b, s, width), BF16),
        scratch_shapes=[
            pltpu.VMEM((s + 2 * RG_PAD, RNN_BLOCK), F32),
            pltpu.VMEM((2, s, RNN_BLOCK), F32),
            pltpu.VMEM((2, s, RNN_BLOCK), F32),
            pltpu.VMEM((2, s, RNN_BLOCK), F32),
        ],
        compiler_params=_cparams("parallel", "parallel"),
        name="rglru",
    )(z, z, conv_w, conv_b, wa, wx, ba, bx, lam)


ATTN_TQ = 128
ATTN_TK = ATTN_TQ + 2 * ATTN_HALF
ATTN_MERGE_TILE = 256
ATTN_UNROLL = 8
LOG2E = math.log2(math.e)


def _attn_body(q_ref, k_ref, v_ref, slope_ref, o_ref, bias_ref, x4_ref, og_ref, lse_ref, *, s):
    qscale = ATTN_DH ** -0.5 * LOG2E
    slope = slope_ref[0, 0:1, 0:1] * LOG2E
    qi = lax.broadcasted_iota(jnp.int32, (ATTN_TQ, ATTN_TK), 0)
    kj = lax.broadcasted_iota(jnp.int32, (ATTN_TQ, ATTN_TK), 1)
    for g, dil in enumerate(ATTN_DILATIONS):
        for e in range(3):
            rel = jnp.abs(kj - e * ATTN_HALF - qi)
            pen = slope * (rel * dil).astype(F32)
            bias_ref[g, e] = jnp.where(rel <= ATTN_HALF, -pen, NEG_BIG)

    s4 = s // 4
    srcs = (q_ref, k_ref, v_ref)

    def split4(t, carry):
        c = t // (s4 // ATTN_TK)
        p0 = (t % (s4 // ATTN_TK)) * ATTN_TK
        dst = pl.ds(pl.multiple_of(c * s4 + p0, ATTN_TK), ATTN_TK)
        for a in range(3):
            x = srcs[a][0, pl.ds(c + 4 * p0, ATTN_TK, stride=4), :]
            x4_ref[a, dst, :] = x * qscale if a == 0 else x
        return carry

    lax.fori_loop(0, 4 * (s4 // ATTN_TK), split4, 0)

    ones = jnp.ones((ATTN_TK, ATTN_DH), BF16)

    for g, dil in enumerate(ATTN_DILATIONS):
        sp = s // dil
        nqb = sp // ATTN_TQ

        def q_block(t, carry, g=g, dil=dil, sp=sp, nqb=nqb):
            r = t // nqb
            p0 = (t % nqb) * ATTN_TQ
            kstart = jnp.clip(p0 - ATTN_HALF, 0, sp - ATTN_TK)
            e = (p0 - kstart) // ATTN_HALF
            if dil == 1:
                qrows = pl.ds(pl.multiple_of(p0, ATTN_TQ), ATTN_TQ)
                krows = pl.ds(pl.multiple_of(kstart, ATTN_HALF), ATTN_TK)
                qf = q_ref[0, qrows, :] * qscale
                kf, vf = k_ref[0, krows, :], v_ref[0, krows, :]
                orows = qrows
            elif dil == 4:
                qrows = pl.ds(pl.multiple_of(r * s4 + p0, ATTN_TQ), ATTN_TQ)
                krows = pl.ds(pl.multiple_of(r * s4 + kstart, ATTN_HALF), ATTN_TK)
                qf, kf, vf = x4_ref[0, qrows, :], x4_ref[1, krows, :], x4_ref[2, krows, :]
                orows = pl.ds(r + 4 * p0, ATTN_TQ, stride=4)
            else:
                base = (r % 4) * s4 + r // 4
                qrows = pl.ds(base + 4 * p0, ATTN_TQ, stride=4)
                krows = pl.ds(base + 4 * kstart, ATTN_TK, stride=4)
                qf, kf, vf = x4_ref[0, qrows, :], x4_ref[1, krows, :], x4_ref[2, krows, :]
                orows = pl.ds(r + dil * p0, ATTN_TQ, stride=dil)
            qb, kb, vb = qf.astype(BF16), kf.astype(BF16), vf.astype(BF16)
            sc = lax.dot_general(qb, kb, (((1,), (1,)), ((), ())),
                                 preferred_element_type=F32) + bias_ref[g, e]
            m = jnp.max(sc, axis=-1, keepdims=True)
            p = jnp.exp2(sc - m).astype(BF16)
            pv = jnp.dot(p, jnp.concatenate([vb, ones], axis=1), preferred_element_type=F32)
            den = pv[:, ATTN_DH:]
            og_ref[g, orows, :] = pv[:, :ATTN_DH] / den
            lse_ref[g, orows, :] = m + jnp.log2(den)
            return carry

        lax.fori_loop(0, dil * nqb, q_block, 0, unroll=ATTN_UNROLL)

    def merge(t, carry):
        r0 = pl.multiple_of(t * ATTN_MERGE_TILE, ATTN_MERGE_TILE)
        sl = pl.ds(r0, ATTN_MERGE_TILE)
        l0, l1, l2 = lse_ref[0, sl, :], lse_ref[1, sl, :], lse_ref[2, sl, :]
        mx = jnp.maximum(jnp.maximum(l0, l1), l2)
        e0, e1, e2 = jnp.exp2(l0 - mx), jnp.exp2(l1 - mx), jnp.exp2(l2 - mx)
        inv = 1.0 / (e0 + e1 + e2)
        o = (e0 * inv) * og_ref[0, sl, :] + (e1 * inv) * og_ref[1, sl, :] + (e2 * inv) * og_ref[2, sl, :]
        o_ref[0, sl, :] = o.astype(o_ref.dtype)
        return carry

    lax.fori_loop(0, s // ATTN_MERGE_TILE, merge, 0)


def dilated_attention(qkv, slopes):
    b, s, _ = qkv.shape
    h, dh = ATTN_HEADS, ATTN_DH
    assert s % (max(ATTN_DILATIONS) * ATTN_TK) == 0 and s % ATTN_MERGE_TILE == 0
    return pl.pallas_call(
        functools.partial(_attn_body, s=s),
        grid=(b, h),
        in_specs=[
            pl.BlockSpec((1, s, dh), lambda i, j: (i, 0, j)),
            pl.BlockSpec((1, s, dh), lambda i, j: (i, 0, h + j)),
            pl.BlockSpec((1, s, dh), lambda i, j: (i, 0, 2 * h + j)),
            pl.BlockSpec((1, SUBLANES, LANES), lambda i, j: (j, 0, 0)),
        ],
        out_specs=pl.BlockSpec((1, s, dh), lambda i, j: (i, 0, j)),
        out_shape=jax.ShapeDtypeStruct((b, s, h * dh), BF16),
        scratch_shapes=[
            pltpu.VMEM((len(ATTN_DILATIONS), 3, ATTN_TQ, ATTN_TK), F32),
            pltpu.VMEM((3, s, dh), F32),
            pltpu.VMEM((len(ATTN_DILATIONS), s, dh), F32),
            pltpu.VMEM((len(ATTN_DILATIONS), s, dh), F32),
        ],
        compiler_params=_cparams("parallel", "parallel"),
        name="dilated_attn",
    )(qkv, qkv, qkv, slopes)


def _ffn(x, norm_g, w1, w3, w2):
    h = rmsnorm(x, norm_g, BF16)
    u = swiglu_up(h, w1.astype(BF16), w3.astype(BF16))
    return matmul([(u, w2.astype(BF16))], res=x, tk=1408, name="ffn_down")


def _even_layer(x, bsz, seq, norm_g, w_in, gate_b, conv_w, conv_b, rg_wa, rg_ba, rg_wx, rg_bx,
                rg_lam, head_g, w_out):
    heads, dk, dv = MLSTM_HEADS, MLSTM_DK, MLSTM_DV
    n_qkvo = 2 * heads * dk + 2 * heads * dv
    n_gate = 4 * heads
    width = RNN_BLOCKS * RNN_BLOCK
    h = rmsnorm(x, norm_g, BF16)
    w_main = jnp.concatenate([w_in[:, :n_qkvo], w_in[:, n_qkvo + n_gate:]], axis=1).astype(BF16)
    w_gate = jnp.pad(w_in[:, n_qkvo:n_qkvo + n_gate], ((0, 0), (0, LANES - n_gate))).astype(BF16)
    z = matmul([(h, w_main)], name="in_proj").reshape(bsz, seq, -1)
    zg = matmul([(h, w_gate)], tn=LANES, name="gate_proj")

    chunk = MLSTM_CHUNK
    nc = seq // chunk
    gates_rows = jnp.transpose(zg[:, :n_gate].reshape(bsz, nc, chunk, 4, heads), (0, 4, 3, 1, 2))
    gate_b_rows = jnp.broadcast_to(
        jnp.transpose(gate_b.astype(F32).reshape(4, heads))[:, :, None, None], (heads, 4, 1, chunk))
    y_a = mlstm_mixer(z, gates_rows, gate_b_rows, head_g.astype(F32).reshape(heads, 1, dv), chunk)

    y_b = rglru_mixer(z, n_qkvo // RNN_BLOCK, (n_qkvo + width) // RNN_BLOCK,
                      conv_w.astype(F32), conv_b.astype(F32).reshape(1, width),
                      rg_wa.astype(BF16), rg_wx.astype(BF16),
                      rg_ba.astype(F32), rg_bx.astype(F32), rg_lam.astype(F32))

    w_out_b = w_out.astype(BF16)
    m = bsz * seq
    return matmul([(y_a.reshape(m, -1), w_out_b[:heads * dv]),
                   (y_b.reshape(m, -1), w_out_b[heads * dv:])], res=x, name="out_proj")


def _odd_layer(x, bsz, seq, norm_g, w_qkv, w_o):
    h = rmsnorm(x, norm_g, BF16)
    qkv = matmul([(h, w_qkv.astype(BF16))], name="qkv_proj").reshape(bsz, seq, -1)
    slopes = jnp.exp2(-ALIBI_MAX_BIAS * jnp.arange(1, ATTN_HEADS + 1, dtype=F32) / ATTN_HEADS)
    slopes = jnp.broadcast_to(slopes[:, None, None], (ATTN_HEADS, SUBLANES, LANES))
    o = dilated_attention(qkv, slopes)
    return matmul([(o.reshape(bsz * seq, -1), w_o.astype(BF16))], res=x, name="attn_out_proj")


def kernel(x, e_norm, e_w_in, e_gate_b, e_conv_w, e_conv_b, e_rg_wa, e_rg_ba, e_rg_wx, e_rg_bx,
           e_rg_lam, e_head_g, e_w_out, o_norm, o_w_qkv, o_w_o, f_norm, f_w1, f_w3, f_w2,
           final_norm):
    bsz, seq, d = x.shape
    depth = f_norm.shape[0]
    xs = x.reshape(bsz * seq, d).astype(F32)
    for l in range(depth):
        if l % 2 == 0:
            e = l // 2
            xs = _even_layer(xs, bsz, seq, e_norm[e], e_w_in[e], e_gate_b[e], e_conv_w[e],
                             e_conv_b[e], e_rg_wa[e], e_rg_ba[e], e_rg_wx[e], e_rg_bx[e],
                             e_rg_lam[e], e_head_g[e], e_w_out[e])
        else:
            o = l // 2
            xs = _odd_layer(xs, bsz, seq, o_norm[o], o_w_qkv[o], o_w_o[o])
        xs = _ffn(xs, f_norm[l], f_w1[l], f_w3[l], f_w2[l])
    return rmsnorm(xs, final_norm, x.dtype).reshape(bsz, seq, d)
```

```python
import functools
import math

import jax
import jax.numpy as jnp
from jax import lax
from jax.experimental import pallas as pl
from jax.experimental.pallas import tpu as pltpu

F32 = jnp.float32
BF16 = jnp.bfloat16

RMS_EPS = 1e-6
NEG_BIG = -1e30

MLSTM_HEADS = 4
MLSTM_DK = 128
MLSTM_DV = 256
MLSTM_CHUNK = 256
RNN_BLOCKS = 8
RNN_BLOCK = 128
CONV_WIDTH = 4
CONV_LEFT = 2
RGLRU_C = 8.0
ATTN_HEADS = 16
ATTN_DH = 128
ATTN_HALF = 64
ATTN_DILATIONS = (1, 4, 16)
ALIBI_MAX_BIAS = 8.0

LANES = 128
SUBLANES = 8
VMEM_LIMIT_BYTES = 56 * 1024 * 1024


def _cparams(*semantics):
    return pltpu.CompilerParams(dimension_semantics=semantics,
                                vmem_limit_bytes=VMEM_LIMIT_BYTES)


def _rmsnorm_body(x_ref, g_ref, o_ref):
    x = x_ref[...]
    ms = jnp.mean(x * x, axis=-1, keepdims=True)
    o_ref[...] = (x * lax.rsqrt(ms + RMS_EPS) * g_ref[...]).astype(o_ref.dtype)


def rmsnorm(x, g, out_dtype, tm=512):
    m, d = x.shape
    return pl.pallas_call(
        _rmsnorm_body,
        grid=(m // tm,),
        in_specs=[pl.BlockSpec((tm, d), lambda i: (i, 0)),
                  pl.BlockSpec((1, d), lambda i: (0, 0))],
        out_specs=pl.BlockSpec((tm, d), lambda i: (i, 0)),
        out_shape=jax.ShapeDtypeStruct((m, d), out_dtype),
        compiler_params=_cparams("parallel"),
        name="rmsnorm",
    )(x, g.reshape(1, d).astype(F32))


def _matmul_body(*refs, n_pairs, has_res, nk):
    pairs = [(refs[2 * p], refs[2 * p + 1]) for p in range(n_pairs)]
    rest = refs[2 * n_pairs:]
    r_ref = rest[0] if has_res else None
    o_ref = rest[1] if has_res else rest[0]
    acc_ref = rest[-1] if nk > 1 else None

    part = None
    for a_ref, w_ref in pairs:
        d = jnp.dot(a_ref[...], w_ref[...], preferred_element_type=F32)
        part = d if part is None else part + d

    def finish(acc):
        if has_res:
            acc = acc + r_ref[...]
        o_ref[...] = acc.astype(o_ref.dtype)

    if nk == 1:
        finish(part)
        return

    k = pl.program_id(2)

    @pl.when(k == 0)
    def _():
        acc_ref[...] = part

    @pl.when(jnp.logical_and(k > 0, k < nk - 1))
    def _():
        acc_ref[...] += part

    @pl.when(k == nk - 1)
    def _():
        finish(acc_ref[...] + part)


def matmul(pairs, res=None, out_dtype=F32, tm=1024, tn=1024, tk=None, name="matmul"):
    m, kdim = pairs[0][0].shape
    n = pairs[0][1].shape[1]
    tk = kdim if tk is None else tk
    tn = min(tn, n)
    nk = kdim // tk
    assert m % tm == 0 and n % tn == 0 and kdim % tk == 0
    in_specs, args = [], []
    for a, w in pairs:
        in_specs += [pl.BlockSpec((tm, tk), lambda i, j, k: (i, k)),
                     pl.BlockSpec((tk, tn), lambda i, j, k: (k, j))]
        args += [a, w]
    if res is not None:
        in_specs.append(pl.BlockSpec((tm, tn), lambda i, j, k: (i, j)))
        args.append(res)
    scratch = [pltpu.VMEM((tm, tn), F32)] if nk > 1 else []
    return pl.pallas_call(
        functools.partial(_matmul_body, n_pairs=len(pairs), has_res=res is not None, nk=nk),
        grid=(m // tm, n // tn, nk),
        in_specs=in_specs,
        out_specs=pl.BlockSpec((tm, tn), lambda i, j, k: (i, j)),
        out_shape=jax.ShapeDtypeStruct((m, n), out_dtype),
        scratch_shapes=scratch,
        compiler_params=_cparams("parallel", "parallel", "arbitrary"),
        name=name,
    )(*args)


NORM_ROWS = 128


def _norm_rows_into(x_ref, g_ref, h_ref):
    g = g_ref[...]

    def rows(t, carry):
        sl = pl.ds(pl.multiple_of(t * NORM_ROWS, NORM_ROWS), NORM_ROWS)
        x = x_ref[sl, :]
        ms = jnp.mean(x * x, axis=-1, keepdims=True)
        h_ref[sl, :] = (x * lax.rsqrt(ms + RMS_EPS) * g).astype(h_ref.dtype)
        return carry

    lax.fori_loop(0, x_ref.shape[0] // NORM_ROWS, rows, 0)


def _norm_matmul_body(*refs, has_side):
    if has_side:
        x_ref, g_ref, w_ref, ws_ref, o_ref, os_ref, h_ref = refs
    else:
        x_ref, g_ref, w_ref, o_ref, h_ref = refs

    @pl.when(pl.program_id(1) == 0)
    def _():
        _norm_rows_into(x_ref, g_ref, h_ref)
        if has_side:
            os_ref[...] = jnp.dot(h_ref[...], ws_ref[...], preferred_element_type=F32)

    o_ref[...] = jnp.dot(h_ref[...], w_ref[...], preferred_element_type=F32).astype(o_ref.dtype)


def norm_matmul(x, g, w, w_side=None, out_dtype=F32, tm=1024, tn=1024, name="norm_matmul"):
    m, d = x.shape
    n = w.shape[1]
    assert m % tm == 0 and n % tn == 0 and tm % NORM_ROWS == 0
    in_specs = [pl.BlockSpec((tm, d), lambda i, j: (i, 0)),
                pl.BlockSpec((1, d), lambda i, j: (0, 0)),
                pl.BlockSpec((d, tn), lambda i, j: (0, j))]
    args = [x, g.reshape(1, d).astype(F32), w]
    out_specs = [pl.BlockSpec((tm, tn), lambda i, j: (i, j))]
    out_shape = [jax.ShapeDtypeStruct((m, n), out_dtype)]
    if w_side is not None:
        ns = w_side.shape[1]
        in_specs.append(pl.BlockSpec((d, ns), lambda i, j: (0, 0)))
        args.append(w_side)
        out_specs.append(pl.BlockSpec((tm, ns), lambda i, j: (i, 0)))
        out_shape.append(jax.ShapeDtypeStruct((m, ns), F32))
    outs = pl.pallas_call(
        functools.partial(_norm_matmul_body, has_side=w_side is not None),
        grid=(m // tm, n // tn),
        in_specs=in_specs,
        out_specs=out_specs,
        out_shape=out_shape,
        scratch_shapes=[pltpu.VMEM((tm, d), BF16)],
        compiler_params=_cparams("parallel", "arbitrary"),
        name=name,
    )(*args)
    return outs if w_side is not None else outs[0]


def _norm_swiglu_up_body(x_ref, g_ref, w1_ref, w3_ref, o_ref, h_ref):
    @pl.when(pl.program_id(1) == 0)
    def _():
        _norm_rows_into(x_ref, g_ref, h_ref)

    a = h_ref[...]
    u = jnp.dot(a, w1_ref[...], preferred_element_type=F32)
    v = jnp.dot(a, w3_ref[...], preferred_element_type=F32)
    o_ref[...] = (u * jax.nn.sigmoid(u) * v).astype(o_ref.dtype)


def norm_swiglu_up(x, g, w1, w3, tm=1024, tn=512):
    m, d = x.shape
    n = w1.shape[1]
    assert m % tm == 0 and n % tn == 0 and tm % NORM_ROWS == 0
    return pl.pallas_call(
        _norm_swiglu_up_body,
        grid=(m // tm, n // tn),
        in_specs=[pl.BlockSpec((tm, d), lambda i, j: (i, 0)),
                  pl.BlockSpec((1, d), lambda i, j: (0, 0)),
                  pl.BlockSpec((d, tn), lambda i, j: (0, j)),
                  pl.BlockSpec((d, tn), lambda i, j: (0, j))],
        out_specs=pl.BlockSpec((tm, tn), lambda i, j: (i, j)),
        out_shape=jax.ShapeDtypeStruct((m, n), BF16),
        scratch_shapes=[pltpu.VMEM((tm, d), BF16)],
        compiler_params=_cparams("parallel", "arbitrary"),
        name="norm_swiglu_up",
    )(x, g.reshape(1, d).astype(F32), w1, w3)


def _log_sigmoid(x):
    return jnp.minimum(x, 0.0) - jnp.log1p(jnp.exp(-jnp.abs(x)))


def _lane_cumsum(x, reverse):
    width = x.shape[-1]
    lane = lax.broadcasted_iota(jnp.int32, x.shape, 1)
    sh = 1
    while sh < width:
        if reverse:
            x = x + jnp.where(lane < width - sh, pltpu.roll(x, width - sh, axis=1), 0.0)
        else:
            x = x + jnp.where(lane >= sh, pltpu.roll(x, sh, axis=1), 0.0)
        sh *= 2
    return x


def _mlstm_body(q_ref, k_ref, v_ref, og_ref, g_ref, gb_ref, hg_ref, y_ref,
                rows_ref, cols_ref, hf_ref, hb_ref, ct_ref, n_ref, *, nc, chunk):
    dk, dv = MLSTM_DK, MLSTM_DV
    scale = dk ** -0.5

    g = g_ref[0, 0] + gb_ref[0]
    i_f = g[0]
    b_f = _lane_cumsum(_log_sigmoid(g[1]), reverse=False)
    i_b = g[2]
    b_b = _lane_cumsum(_log_sigmoid(g[3]), reverse=True)
    rows_ref[0] = b_f
    rows_ref[1] = i_f
    rows_ref[2] = b_b
    rows_ref[3] = i_b
    rowmat = jnp.concatenate(
        [b_f, i_f, b_b, i_b, jnp.zeros((LANES - 4 * nc, chunk), F32)], axis=0)
    colmat = rowmat.T
    for c in range(nc):
        cols_ref[c] = colmat if c == 0 else pltpu.roll(colmat, LANES - c, axis=1)

    ct_ref[...] = jnp.zeros_like(ct_ref)
    n_ref[...] = jnp.zeros_like(n_ref)

    row_id = lax.broadcasted_iota(jnp.int32, (chunk, chunk), 0)
    col_id = lax.broadcasted_iota(jnp.int32, (chunk, chunk), 1)
    causal = (row_id >= col_id, row_id <= col_id)

    def chunk_step(c, m, direction, h_ref):
        r0 = pl.multiple_of(c * chunk, chunk)
        qf = q_ref[0, pl.ds(r0, chunk), :] * scale
        kf = k_ref[0, pl.ds(r0, chunk), :]
        vf = v_ref[0, pl.ds(r0, chunk), :]
        qb, kb, vb = qf.astype(BF16), kf.astype(BF16), vf.astype(BF16)
        cols = cols_ref[c]
        b_col = cols[:, 2 * direction * nc:2 * direction * nc + 1]
        i_col = cols[:, (2 * direction + 1) * nc:(2 * direction + 1) * nc + 1]
        b_row = rows_ref[2 * direction, pl.ds(c, 1), :]
        i_row = rows_ref[2 * direction + 1, pl.ds(c, 1), :]
        total = b_row[:, chunk - 1:chunk] if direction == 0 else b_row[:, 0:1]

        dmat = jnp.where(causal[direction], b_col + (i_row - b_row), NEG_BIG)
        inter = b_col + m
        m_out = jnp.maximum(inter, jnp.max(dmat, axis=-1, keepdims=True))
        w_intra = jnp.exp(dmat - m_out)
        w_inter = jnp.exp(inter - m_out)
        s = lax.dot_general(qb, kb, (((1,), (1,)), ((), ())),
                            preferred_element_type=F32) * w_intra
        ct = ct_ref[direction]
        nrow = n_ref[direction]
        num = (jnp.dot(s.astype(BF16), vb, preferred_element_type=F32)
               + w_inter * jnp.dot(qb, ct.astype(BF16), preferred_element_type=F32))
        den = (jnp.sum(s, axis=-1, keepdims=True)
               + w_inter * jnp.sum(qf * nrow, axis=-1, keepdims=True))
        inv = 1.0 / jnp.maximum(jnp.abs(den), jnp.exp(-m_out))
        h_ref[pl.ds(r0, chunk), :] = num * inv

        src = total - b_col + i_col
        m_new = jnp.maximum(total + m, jnp.max(src, axis=0, keepdims=True))
        w_src = jnp.exp(src - m_new)
        decay = jnp.exp(total + m - m_new)
        wv = (w_src * vf).astype(BF16)
        ct_ref[direction] = decay * ct + lax.dot_general(
            kb, wv, (((0,), (0,)), ((), ())), preferred_element_type=F32)
        n_ref[direction] = decay * nrow + jnp.sum(w_src * kf, axis=0, keepdims=True)
        return m_new

    def body(c, carry):
        m_f, m_b = carry
        m_f = chunk_step(c, m_f, 0, hf_ref)
        m_b = chunk_step(nc - 1 - c, m_b, 1, hb_ref)
        return m_f, m_b

    m0 = jnp.full((1, 1), NEG_BIG, F32)
    lax.fori_loop(0, nc, body, (m0, m0))

    hg = hg_ref[0]

    def finish(c, carry):
        r0 = pl.multiple_of(c * chunk, chunk)
        hm = hf_ref[pl.ds(r0, chunk), :] + hb_ref[pl.ds(r0, chunk), :]
        ms = jnp.mean(hm * hm, axis=-1, keepdims=True)
        hn = hm * lax.rsqrt(ms + RMS_EPS) * hg
        gate = jax.nn.sigmoid(og_ref[0, pl.ds(r0, chunk), :])
        y_ref[0, pl.ds(r0, chunk), :] = (hn * gate).astype(y_ref.dtype)
        return carry

    lax.fori_loop(0, nc, finish, 0)


def mlstm_mixer(z, gates_rows, gate_b_rows, head_g, chunk=MLSTM_CHUNK):
    b, s, _ = z.shape
    h, dk, dv = MLSTM_HEADS, MLSTM_DK, MLSTM_DV
    nc = s // chunk
    assert s % chunk == 0 and 4 * nc <= LANES and chunk % LANES == 0
    kq = h * dk // dk
    kv = 2 * h * dk // dv
    ko = kv + h
    return pl.pallas_call(
        functools.partial(_mlstm_body, nc=nc, chunk=chunk),
        grid=(b, h),
        in_specs=[
            pl.BlockSpec((1, s, dk), lambda i, j: (i, 0, j)),
            pl.BlockSpec((1, s, dk), lambda i, j: (i, 0, kq + j)),
            pl.BlockSpec((1, s, dv), lambda i, j: (i, 0, kv + j)),
            pl.BlockSpec((1, s, dv), lambda i, j: (i, 0, ko + j)),
            pl.BlockSpec((1, 1, 4, nc, chunk), lambda i, j: (i, j, 0, 0, 0)),
            pl.BlockSpec((1, 4, 1, chunk), lambda i, j: (j, 0, 0, 0)),
            pl.BlockSpec((1, 1, dv), lambda i, j: (j, 0, 0)),
        ],
        out_specs=pl.BlockSpec((1, s, dv), lambda i, j: (i, 0, j)),
        out_shape=jax.ShapeDtypeStruct((b, s, h * dv), BF16),
        scratch_shapes=[
            pltpu.VMEM((4, nc, chunk), F32),
            pltpu.VMEM((nc, chunk, LANES), F32),
            pltpu.VMEM((s, dv), F32),
            pltpu.VMEM((s, dv), F32),
            pltpu.VMEM((2, dk, dv), F32),
            pltpu.VMEM((2, 1, dk), F32),
        ],
        compiler_params=_cparams("parallel", "parallel"),
        name="mlstm",
    )(z, z, z, z, gates_rows, gate_b_rows, head_g)


RG_TILE = 512
RG_PAD = SUBLANES
RG_SCAN_UNROLL = 8


def _softplus(x):
    return jnp.maximum(x, 0.0) + jnp.log1p(jnp.exp(-jnp.abs(x)))


def _gelu_tanh(x):
    c = math.sqrt(2.0 / math.pi)
    return x * (0.5 * (1.0 + jnp.tanh(c * (x + 0.044715 * (x * x * x)))))


def _rglru_body(x_ref, gr_ref, cw_ref, cb_ref, wa_ref, wx_ref, ba_ref, bx_ref, lam_ref, y_ref,
                xpad_ref, a_ref, u_ref, h_ref, *, s):
    nt = s // RG_TILE
    zeros = jnp.zeros((RG_PAD, LANES), F32)
    xpad_ref[pl.ds(0, RG_PAD), :] = zeros
    xpad_ref[pl.ds(RG_PAD + s, RG_PAD), :] = zeros

    def copy_in(t, carry):
        r0 = pl.multiple_of(t * RG_TILE, RG_TILE)
        xpad_ref[pl.ds(RG_PAD + r0, RG_TILE), :] = x_ref[0, pl.ds(r0, RG_TILE), :]
        return carry

    lax.fori_loop(0, nt, copy_in, 0)

    cw = cw_ref[...]
    cb = cb_ref[...]
    neg_sp = [-RGLRU_C * _softplus(-lam_ref[d:d + 1, :]) for d in range(2)]

    def gates(t, carry):
        r0 = pl.multiple_of(t * RG_TILE, RG_TILE)
        xc = None
        for j in range(CONV_WIDTH):
            tap = xpad_ref[pl.ds(r0 + RG_PAD - CONV_LEFT + j, RG_TILE), :] * cw[j:j + 1, :]
            xc = tap if xc is None else xc + tap
        xc = xc + cb
        xcb = xc.astype(BF16)
        for d in range(2):
            r = jax.nn.sigmoid(jnp.dot(xcb, wa_ref[d, 0], preferred_element_type=F32)
                               + ba_ref[d:d + 1, :])
            i = jax.nn.sigmoid(jnp.dot(xcb, wx_ref[d, 0], preferred_element_type=F32)
                               + bx_ref[d:d + 1, :])
            log_a = neg_sp[d] * r
            a = jnp.exp(log_a)
            one_minus_a2 = -jnp.tanh(log_a) * (1.0 + a * a)
            a, u = block_scan(a, jnp.sqrt(one_minus_a2) * (i * xc), reverse=d == 1)
            a_ref[d, pl.ds(r0, RG_TILE), :] = a
            u_ref[d, pl.ds(r0, RG_TILE), :] = u
        return carry

    row = lax.broadcasted_iota(jnp.int32, (RG_TILE // SUBLANES, SUBLANES, LANES), 1)

    def block_scan(a, u, reverse):
        a = a.reshape(RG_TILE // SUBLANES, SUBLANES, LANES)
        u = u.reshape(RG_TILE // SUBLANES, SUBLANES, LANES)
        sh = 1
        while sh < SUBLANES:
            keep = row < SUBLANES - sh if reverse else row >= sh
            amount = SUBLANES - sh if reverse else sh
            a_s = jnp.where(keep, pltpu.roll(a, amount, axis=1), 1.0)
            u_s = jnp.where(keep, pltpu.roll(u, amount, axis=1), 0.0)
            u = a * u_s + u
            a = a * a_s
            sh *= 2
        return a.reshape(RG_TILE, LANES), u.reshape(RG_TILE, LANES)

    lax.fori_loop(0, nt, gates, 0)

    nblk = s // SUBLANES

    def scan(j, carry):
        h_f, h_b = carry
        for d, last, h in ((0, SUBLANES - 1, h_f), (1, 0, h_b)):
            blk = j if d == 0 else nblk - 1 - j
            rows = pl.ds(pl.multiple_of(blk * SUBLANES, SUBLANES), SUBLANES)
            a, u = a_ref[d, rows, :], u_ref[d, rows, :]
            h_ref[d, rows, :] = a * h + u
            a_end = jnp.broadcast_to(a[last:last + 1, :], (SUBLANES, LANES))
            u_end = jnp.broadcast_to(u[last:last + 1, :], (SUBLANES, LANES))
            if d == 0:
                h_f = a_end * h + u_end
            else:
                h_b = a_end * h + u_end
        return h_f, h_b

    h0 = jnp.zeros((SUBLANES, LANES), F32)
    lax.fori_loop(0, nblk, scan, (h0, h0), unroll=RG_SCAN_UNROLL)

    def finish(t, carry):
        r0 = pl.multiple_of(t * RG_TILE, RG_TILE)
        hr = h_ref[0, pl.ds(r0, RG_TILE), :] + h_ref[1, pl.ds(r0, RG_TILE), :]
        y_ref[0, pl.ds(r0, RG_TILE), :] = (
            hr * _gelu_tanh(gr_ref[0, pl.ds(r0, RG_TILE), :])).astype(y_ref.dtype)
        return carry

    lax.fori_loop(0, nt, finish, 0)


def rglru_mixer(z, xr_block0, gr_block0, conv_w, conv_b, wa, wx, ba, bx, lam):
    b, s, _ = z.shape
    width = RNN_BLOCKS * RNN_BLOCK
    assert s % RG_TILE == 0
    return pl.pallas_call(
        functools.partial(_rglru_body, s=s),
        grid=(b, RNN_BLOCKS),
        in_specs=[
            pl.BlockSpec((1, s, RNN_BLOCK), lambda i, j: (i, 0, xr_block0 + j)),
            pl.BlockSpec((1, s, RNN_BLOCK), lambda i, j: (i, 0, gr_block0 + j)),
            pl.BlockSpec((CONV_WIDTH, RNN_BLOCK), lambda i, j: (0, j)),
            pl.BlockSpec((1, RNN_BLOCK), lambda i, j: (0, j)),
            pl.BlockSpec((2, 1, RNN_BLOCK, RNN_BLOCK), lambda i, j: (0, j, 0, 0)),
            pl.BlockSpec((2, 1, RNN_BLOCK, RNN_BLOCK), lambda i, j: (0, j, 0, 0)),
            pl.BlockSpec((2, RNN_BLOCK), lambda i, j: (0, j)),
            pl.BlockSpec((2, RNN_BLOCK), lambda i, j: (0, j)),
            pl.BlockSpec((2, RNN_BLOCK), lambda i, j: (0, j)),
        ],
        out_specs=pl.BlockSpec((1, s, RNN_BLOCK), lambda i, j: (i, 0, j)),
        out_shape=jax.ShapeDtypeStruct((b, s, width), BF16),
        scratch_shapes=[
            pltpu.VMEM((s + 2 * RG_PAD, RNN_BLOCK), F32),
            pltpu.VMEM((2, s, RNN_BLOCK), F32),
            pltpu.VMEM((2, s, RNN_BLOCK), F32),
            pltpu.VMEM((2, s, RNN_BLOCK), F32),
        ],
        compiler_params=_cparams("parallel", "parallel"),
        name="rglru",
    )(z, z, conv_w, conv_b, wa, wx, ba, bx, lam)


ATTN_TQ = 128
ATTN_TK = ATTN_TQ + 2 * ATTN_HALF
ATTN_MERGE_TILE = 256
ATTN_UNROLL = 8
LOG2E = math.log2(math.e)


def _attn_body(q_ref, k_ref, v_ref, slope_ref, o_ref, bias_ref, x4_ref, og_ref, lse_ref, *, s):
    qscale = ATTN_DH ** -0.5 * LOG2E
    slope = slope_ref[0, 0:1, 0:1] * LOG2E
    qi = lax.broadcasted_iota(jnp.int32, (ATTN_TQ, ATTN_TK), 0)
    kj = lax.broadcasted_iota(jnp.int32, (ATTN_TQ, ATTN_TK), 1)
    for g, dil in enumerate(ATTN_DILATIONS):
        for e in range(3):
            rel = jnp.abs(kj - e * ATTN_HALF - qi)
            pen = slope * (rel * dil).astype(F32)
            bias_ref[g, e] = jnp.where(rel <= ATTN_HALF, -pen, NEG_BIG)

    s4 = s // 4
    srcs = (q_ref, k_ref, v_ref)

    def split4(t, carry):
        c = t // (s4 // ATTN_TK)
        p0 = (t % (s4 // ATTN_TK)) * ATTN_TK
        dst = pl.ds(pl.multiple_of(c * s4 + p0, ATTN_TK), ATTN_TK)
        for a in range(3):
            x = srcs[a][0, pl.ds(c + 4 * p0, ATTN_TK, stride=4), :]
            x4_ref[a, dst, :] = x * qscale if a == 0 else x
        return carry

    lax.fori_loop(0, 4 * (s4 // ATTN_TK), split4, 0)

    ones = jnp.ones((ATTN_TK, ATTN_DH), BF16)

    for g, dil in enumerate(ATTN_DILATIONS):
        sp = s // dil
        nqb = sp // ATTN_TQ

        def q_block(t, carry, g=g, dil=dil, sp=sp, nqb=nqb):
            r = t // nqb
            p0 = (t % nqb) * ATTN_TQ
            kstart = jnp.clip(p0 - ATTN_HALF, 0, sp - ATTN_TK)
            e = (p0 - kstart) // ATTN_HALF
            if dil == 1:
                qrows = pl.ds(pl.multiple_of(p0, ATTN_TQ), ATTN_TQ)
                krows = pl.ds(pl.multiple_of(kstart, ATTN_HALF), ATTN_TK)
                qf = q_ref[0, qrows, :] * qscale
                kf, vf = k_ref[0, krows, :], v_ref[0, krows, :]
                orows = qrows
            elif dil == 4:
                qrows = pl.ds(pl.multiple_of(r * s4 + p0, ATTN_TQ), ATTN_TQ)
                krows = pl.ds(pl.multiple_of(r * s4 + kstart, ATTN_HALF), ATTN_TK)
                qf, kf, vf = x4_ref[0, qrows, :], x4_ref[1, krows, :], x4_ref[2, krows, :]
                orows = pl.ds(r + 4 * p0, ATTN_TQ, stride=4)
            else:
                base = (r % 4) * s4 + r // 4
                qrows = pl.ds(base + 4 * p0, ATTN_TQ, stride=4)
                krows = pl.ds(base + 4 * kstart, ATTN_TK, stride=4)
                qf, kf, vf = x4_ref[0, qrows, :], x4_ref[1, krows, :], x4_ref[2, krows, :]
                orows = pl.ds(r + dil * p0, ATTN_TQ, stride=dil)
            qb, kb, vb = qf.astype(BF16), kf.astype(BF16), vf.astype(BF16)
            sc = lax.dot_general(qb, kb, (((1,), (1,)), ((), ())),
                                 preferred_element_type=F32) + bias_ref[g, e]
            m = jnp.max(sc, axis=-1, keepdims=True)
            p = jnp.exp2(sc - m).astype(BF16)
            pv = jnp.dot(p, jnp.concatenate([vb, ones], axis=1), preferred_element_type=F32)
            den = pv[:, ATTN_DH:]
            og_ref[g, orows, :] = pv[:, :ATTN_DH] / den
            lse_ref[g, orows, :] = m + jnp.log2(den)
            return carry

        lax.fori_loop(0, dil * nqb, q_block, 0, unroll=ATTN_UNROLL)

    def merge(t, carry):
        r0 = pl.multiple_of(t * ATTN_MERGE_TILE, ATTN_MERGE_TILE)
        sl = pl.ds(r0, ATTN_MERGE_TILE)
        l0, l1, l2 = lse_ref[0, sl, :], lse_ref[1, sl, :], lse_ref[2, sl, :]
        mx = jnp.maximum(jnp.maximum(l0, l1), l2)
        e0, e1, e2 = jnp.exp2(l0 - mx), jnp.exp2(l1 - mx), jnp.exp2(l2 - mx)
        inv = 1.0 / (e0 + e1 + e2)
        o = (e0 * inv) * og_ref[0, sl, :] + (e1 * inv) * og_ref[1, sl, :] + (e2 * inv) * og_ref[2, sl, :]
        o_ref[0, sl, :] = o.astype(o_ref.dtype)
        return carry

    lax.fori_loop(0, s // ATTN_MERGE_TILE, merge, 0)


def dilated_attention(qkv, slopes):
    b, s, _ = qkv.shape
    h, dh = ATTN_HEADS, ATTN_DH
    assert s % (max(ATTN_DILATIONS) * ATTN_TK) == 0 and s % ATTN_MERGE_TILE == 0
    return pl.pallas_call(
        functools.partial(_attn_body, s=s),
        grid=(b, h),
        in_specs=[
            pl.BlockSpec((1, s, dh), lambda i, j: (i, 0, j)),
            pl.BlockSpec((1, s, dh), lambda i, j: (i, 0, h + j)),
            pl.BlockSpec((1, s, dh), lambda i, j: (i, 0, 2 * h + j)),
            pl.BlockSpec((1, SUBLANES, LANES), lambda i, j: (j, 0, 0)),
        ],
        out_specs=pl.BlockSpec((1, s, dh), lambda i, j: (i, 0, j)),
        out_shape=jax.ShapeDtypeStruct((b, s, h * dh), BF16),
        scratch_shapes=[
            pltpu.VMEM((len(ATTN_DILATIONS), 3, ATTN_TQ, ATTN_TK), F32),
            pltpu.VMEM((3, s, dh), F32),
            pltpu.VMEM((len(ATTN_DILATIONS), s, dh), F32),
            pltpu.VMEM((len(ATTN_DILATIONS), s, dh), F32),
        ],
        compiler_params=_cparams("parallel", "parallel"),
        name="dilated_attn",
    )(qkv, qkv, qkv, slopes)


def _ffn(x, norm_g, w1, w3, w2):
    u = norm_swiglu_up(x, norm_g, w1.astype(BF16), w3.astype(BF16))
    return matmul([(u, w2.astype(BF16))], res=x, tn=512, name="ffn_down")


def _even_layer(x, bsz, seq, norm_g, w_in, gate_b, conv_w, conv_b, rg_wa, rg_ba, rg_wx, rg_bx,
                rg_lam, head_g, w_out):
    heads, dk, dv = MLSTM_HEADS, MLSTM_DK, MLSTM_DV
    n_qkvo = 2 * heads * dk + 2 * heads * dv
    n_gate = 4 * heads
    width = RNN_BLOCKS * RNN_BLOCK
    w_main = jnp.concatenate([w_in[:, :n_qkvo], w_in[:, n_qkvo + n_gate:]], axis=1).astype(BF16)
    w_gate = jnp.pad(w_in[:, n_qkvo:n_qkvo + n_gate], ((0, 0), (0, LANES - n_gate))).astype(BF16)
    z, zg = norm_matmul(x, norm_g, w_main, w_side=w_gate, name="in_proj")
    z = z.reshape(bsz, seq, -1)

    chunk = MLSTM_CHUNK
    nc = seq // chunk
    gates_rows = jnp.transpose(zg[:, :n_gate].reshape(bsz, nc, chunk, 4, heads), (0, 4, 3, 1, 2))
    gate_b_rows = jnp.broadcast_to(
        jnp.transpose(gate_b.astype(F32).reshape(4, heads))[:, :, None, None], (heads, 4, 1, chunk))
    y_a = mlstm_mixer(z, gates_rows, gate_b_rows, head_g.astype(F32).reshape(heads, 1, dv), chunk)

    y_b = rglru_mixer(z, n_qkvo // RNN_BLOCK, (n_qkvo + width) // RNN_BLOCK,
                      conv_w.astype(F32), conv_b.astype(F32).reshape(1, width),
                      rg_wa.astype(BF16), rg_wx.astype(BF16),
                      rg_ba.astype(F32), rg_bx.astype(F32), rg_lam.astype(F32))

    w_out_b = w_out.astype(BF16)
    m = bsz * seq
    return matmul([(y_a.reshape(m, -1), w_out_b[:heads * dv]),
                   (y_b.reshape(m, -1), w_out_b[heads * dv:])], res=x, name="out_proj")


def _odd_layer(x, bsz, seq, norm_g, w_qkv, w_o):
    qkv = norm_matmul(x, norm_g, w_qkv.astype(BF16), name="qkv_proj").reshape(bsz, seq, -1)
    slopes = jnp.exp2(-ALIBI_MAX_BIAS * jnp.arange(1, ATTN_HEADS + 1, dtype=F32) / ATTN_HEADS)
    slopes = jnp.broadcast_to(slopes[:, None, None], (ATTN_HEADS, SUBLANES, LANES))
    o = dilated_attention(qkv, slopes)
    return matmul([(o.reshape(bsz * seq, -1), w_o.astype(BF16))], res=x, name="attn_out_proj")


def kernel(x, e_norm, e_w_in, e_gate_b, e_conv_w, e_conv_b, e_rg_wa, e_rg_ba, e_rg_wx, e_rg_bx,
           e_rg_lam, e_head_g, e_w_out, o_norm, o_w_qkv, o_w_o, f_norm, f_w1, f_w3, f_w2,
           final_norm):
    bsz, seq, d = x.shape
    depth = f_norm.shape[0]
    xs = x.reshape(bsz * seq, d).astype(F32)
    for l in range(depth):
        if l % 2 == 0:
            e = l // 2
            xs = _even_layer(xs, bsz, seq, e_norm[e], e_w_in[e], e_gate_b[e], e_conv_w[e],
                             e_conv_b[e], e_rg_wa[e], e_rg_ba[e], e_rg_wx[e], e_rg_bx[e],
                             e_rg_lam[e], e_head_g[e], e_w_out[e])
        else:
            o = l // 2
            xs = _odd_layer(xs, bsz, seq, o_norm[o], o_w_qkv[o], o_w_o[o])
        xs = _ffn(xs, f_norm[l], f_w1[l], f_w3[l], f_w2[l])
    return rmsnorm(xs, final_norm, x.dtype).reshape(bsz, seq, d)
```

```python
import functools
import math

import jax
import jax.numpy as jnp
from jax import lax
from jax.experimental import pallas as pl
from jax.experimental.pallas import tpu as pltpu

F32 = jnp.float32
BF16 = jnp.bfloat16

RMS_EPS = 1e-6
NEG_BIG = -1e30

MLSTM_HEADS = 4
MLSTM_DK = 128
MLSTM_DV = 256
MLSTM_CHUNK = 256
RNN_BLOCKS = 8
RNN_BLOCK = 128
CONV_WIDTH = 4
CONV_LEFT = 2
RGLRU_C = 8.0
ATTN_HEADS = 16
ATTN_DH = 128
ATTN_HALF = 64
ATTN_DILATIONS = (1, 4, 16)
ALIBI_MAX_BIAS = 8.0

LANES = 128
SUBLANES = 8
VMEM_LIMIT_BYTES = 56 * 1024 * 1024


def _cparams(*semantics):
    return pltpu.CompilerParams(dimension_semantics=semantics,
                                vmem_limit_bytes=VMEM_LIMIT_BYTES)


def _rmsnorm_body(x_ref, g_ref, o_ref):
    x = x_ref[...]
    ms = jnp.mean(x * x, axis=-1, keepdims=True)
    o_ref[...] = (x * lax.rsqrt(ms + RMS_EPS) * g_ref[...]).astype(o_ref.dtype)


def rmsnorm(x, g, out_dtype, tm=512):
    m, d = x.shape
    return pl.pallas_call(
        _rmsnorm_body,
        grid=(m // tm,),
        in_specs=[pl.BlockSpec((tm, d), lambda i: (i, 0)),
                  pl.BlockSpec((1, d), lambda i: (0, 0))],
        out_specs=pl.BlockSpec((tm, d), lambda i: (i, 0)),
        out_shape=jax.ShapeDtypeStruct((m, d), out_dtype),
        compiler_params=_cparams("parallel"),
        name="rmsnorm",
    )(x, g.reshape(1, d).astype(F32))


def _matmul_body(*refs, n_pairs, has_res, nk):
    pairs = [(refs[2 * p], refs[2 * p + 1]) for p in range(n_pairs)]
    rest = refs[2 * n_pairs:]
    r_ref = rest[0] if has_res else None
    o_ref = rest[1] if has_res else rest[0]
    acc_ref = rest[-1] if nk > 1 else None

    part = None
    for a_ref, w_ref in pairs:
        d = jnp.dot(a_ref[...], w_ref[...], preferred_element_type=F32)
        part = d if part is None else part + d

    def finish(acc):
        if has_res:
            acc = acc + r_ref[...]
        o_ref[...] = acc.astype(o_ref.dtype)

    if nk == 1:
        finish(part)
        return

    k = pl.program_id(2)

    @pl.when(k == 0)
    def _():
        acc_ref[...] = part

    @pl.when(jnp.logical_and(k > 0, k < nk - 1))
    def _():
        acc_ref[...] += part

    @pl.when(k == nk - 1)
    def _():
        finish(acc_ref[...] + part)


def matmul(pairs, res=None, out_dtype=F32, tm=1024, tn=1024, tk=None, name="matmul"):
    m, kdim = pairs[0][0].shape
    n = pairs[0][1].shape[1]
    tk = kdim if tk is None else tk
    tn = min(tn, n)
    nk = kdim // tk
    assert m % tm == 0 and n % tn == 0 and kdim % tk == 0
    in_specs, args = [], []
    for a, w in pairs:
        in_specs += [pl.BlockSpec((tm, tk), lambda i, j, k: (i, k)),
                     pl.BlockSpec((tk, tn), lambda i, j, k: (k, j))]
        args += [a, w]
    if res is not None:
        in_specs.append(pl.BlockSpec((tm, tn), lambda i, j, k: (i, j)))
        args.append(res)
    scratch = [pltpu.VMEM((tm, tn), F32)] if nk > 1 else []
    return pl.pallas_call(
        functools.partial(_matmul_body, n_pairs=len(pairs), has_res=res is not None, nk=nk),
        grid=(m // tm, n // tn, nk),
        in_specs=in_specs,
        out_specs=pl.BlockSpec((tm, tn), lambda i, j, k: (i, j)),
        out_shape=jax.ShapeDtypeStruct((m, n), out_dtype),
        scratch_shapes=scratch,
        compiler_params=_cparams("parallel", "parallel", "arbitrary"),
        name=name,
    )(*args)


NORM_ROWS = 128


def _norm_rows_into(x_ref, g_ref, h_ref):
    g = g_ref[...]

    def rows(t, carry):
        sl = pl.ds(pl.multiple_of(t * NORM_ROWS, NORM_ROWS), NORM_ROWS)
        x = x_ref[sl, :]
        ms = jnp.mean(x * x, axis=-1, keepdims=True)
        h_ref[sl, :] = (x * lax.rsqrt(ms + RMS_EPS) * g).astype(h_ref.dtype)
        return carry

    lax.fori_loop(0, x_ref.shape[0] // NORM_ROWS, rows, 0)


def _norm_matmul_body(*refs, has_side):
    if has_side:
        x_ref, g_ref, w_ref, ws_ref, o_ref, os_ref, h_ref = refs
    else:
        x_ref, g_ref, w_ref, o_ref, h_ref = refs

    @pl.when(pl.program_id(1) == 0)
    def _():
        _norm_rows_into(x_ref, g_ref, h_ref)
        if has_side:
            os_ref[...] = jnp.dot(h_ref[...], ws_ref[...], preferred_element_type=F32)

    o_ref[...] = jnp.dot(h_ref[...], w_ref[...], preferred_element_type=F32).astype(o_ref.dtype)


def norm_matmul(x, g, w, w_side=None, out_dtype=F32, tm=1024, tn=1024, name="norm_matmul"):
    m, d = x.shape
    n = w.shape[1]
    assert m % tm == 0 and n % tn == 0 and tm % NORM_ROWS == 0
    in_specs = [pl.BlockSpec((tm, d), lambda i, j: (i, 0)),
                pl.BlockSpec((1, d), lambda i, j: (0, 0)),
                pl.BlockSpec((d, tn), lambda i, j: (0, j))]
    args = [x, g.reshape(1, d).astype(F32), w]
    out_specs = [pl.BlockSpec((tm, tn), lambda i, j: (i, j))]
    out_shape = [jax.ShapeDtypeStruct((m, n), out_dtype)]
    if w_side is not None:
        ns = w_side.shape[1]
        in_specs.append(pl.BlockSpec((d, ns), lambda i, j: (0, 0)))
        args.append(w_side)
        out_specs.append(pl.BlockSpec((tm, ns), lambda i, j: (i, 0)))
        out_shape.append(jax.ShapeDtypeStruct((m, ns), F32))
    outs = pl.pallas_call(
        functools.partial(_norm_matmul_body, has_side=w_side is not None),
        grid=(m // tm, n // tn),
        in_specs=in_specs,
        out_specs=out_specs,
        out_shape=out_shape,
        scratch_shapes=[pltpu.VMEM((tm, d), BF16)],
        compiler_params=_cparams("parallel", "arbitrary"),
        name=name,
    )(*args)
    return outs if w_side is not None else outs[0]


def _norm_swiglu_up_body(x_ref, g_ref, w1_ref, w3_ref, o_ref, h_ref):
    @pl.when(pl.program_id(1) == 0)
    def _():
        _norm_rows_into(x_ref, g_ref, h_ref)

    a = h_ref[...]
    u = jnp.dot(a, w1_ref[...], preferred_element_type=F32)
    v = jnp.dot(a, w3_ref[...], preferred_element_type=F32)
    o_ref[...] = (u * jax.nn.sigmoid(u) * v).astype(o_ref.dtype)


def norm_swiglu_up(x, g, w1, w3, tm=1024, tn=512):
    m, d = x.shape
    n = w1.shape[1]
    assert m % tm == 0 and n % tn == 0 and tm % NORM_ROWS == 0
    return pl.pallas_call(
        _norm_swiglu_up_body,
        grid=(m // tm, n // tn),
        in_specs=[pl.BlockSpec((tm, d), lambda i, j: (i, 0)),
                  pl.BlockSpec((1, d), lambda i, j: (0, 0)),
                  pl.BlockSpec((d, tn), lambda i, j: (0, j)),
                  pl.BlockSpec((d, tn), lambda i, j: (0, j))],
        out_specs=pl.BlockSpec((tm, tn), lambda i, j: (i, j)),
        out_shape=jax.ShapeDtypeStruct((m, n), BF16),
        scratch_shapes=[pltpu.VMEM((tm, d), BF16)],
        compiler_params=_cparams("parallel", "arbitrary"),
        name="norm_swiglu_up",
    )(x, g.reshape(1, d).astype(F32), w1, w3)


def _log_sigmoid(x):
    return jnp.minimum(x, 0.0) - jnp.log1p(jnp.exp(-jnp.abs(x)))


def _lane_scan(x, combine, identity, reverse):
    width = x.shape[-1]
    lane = lax.broadcasted_iota(jnp.int32, x.shape, 1)
    sh = 1
    while sh < width:
        if reverse:
            shifted = jnp.where(lane < width - sh, pltpu.roll(x, width - sh, axis=1), identity)
        else:
            shifted = jnp.where(lane >= sh, pltpu.roll(x, sh, axis=1), identity)
        x = combine(x, shifted)
        sh *= 2
    return x


def _mlstm_body(q_ref, k_ref, v_ref, og_ref, g_ref, gb_ref, hg_ref, y_ref,
                rows_ref, cols_ref, hf_ref, hb_ref, ct_ref, *, nc, chunk):
    dk, dv = MLSTM_DK, MLSTM_DV
    scale = dk ** -0.5

    g = g_ref[0, 0] + gb_ref[0]
    rows = []
    for d in range(2):
        b = _lane_scan(_log_sigmoid(g[2 * d + 1]), jnp.add, 0.0, reverse=d == 1)
        gg = g[2 * d] - b
        gmax = _lane_scan(gg, jnp.maximum, -jnp.inf, reverse=d == 1)
        rows += [gg, b, gmax]
        for j, r in enumerate((gg, b, gmax)):
            rows_ref[3 * d + j] = r
    rowmat = jnp.concatenate(rows + [jnp.zeros((LANES - 6 * nc, chunk), F32)], axis=0)
    colmat = rowmat.T
    for c in range(nc):
        cols_ref[c] = colmat if c == 0 else pltpu.roll(colmat, LANES - c, axis=1)

    ct_ref[...] = jnp.zeros_like(ct_ref)

    row_id = lax.broadcasted_iota(jnp.int32, (chunk, chunk), 0)
    col_id = lax.broadcasted_iota(jnp.int32, (chunk, chunk), 1)
    causal = (row_id >= col_id, row_id <= col_id)
    ones = jnp.ones((chunk, LANES), BF16)

    def lanes2(x):
        return jnp.concatenate([x] * (dv // LANES), axis=1)

    def chunk_step(c, m, d, h_ref):
        r0 = pl.multiple_of(c * chunk, chunk)
        qb = (q_ref[0, pl.ds(r0, chunk), :] * scale).astype(BF16)
        kb = k_ref[0, pl.ds(r0, chunk), :].astype(BF16)
        vf = v_ref[0, pl.ds(r0, chunk), :]
        cols = cols_ref[c]

        def column(j):
            lane = (3 * d + j) * nc
            return jnp.broadcast_to(cols[:, lane:lane + 1], (chunk, LANES))

        gg_col, b_col, gmax_col = column(0), column(1), column(2)
        gg_row = rows_ref[3 * d, pl.ds(c, 1), :]
        end = chunk - 1 if d == 0 else 0
        total = rows_ref[3 * d + 1, pl.ds(c, 1), :][:, end:end + 1]
        gmax_end = rows_ref[3 * d + 2, pl.ds(c, 1), :][:, end:end + 1]

        m_row = jnp.maximum(gmax_col, m)
        w_intra = jnp.exp(jnp.where(causal[d], gg_row - lanes2(m_row), NEG_BIG))
        w_inter = jnp.exp(m - m_row)
        s = lax.dot_general(qb, kb, (((1,), (1,)), ((), ())),
                            preferred_element_type=F32) * w_intra
        ct = ct_ref[d]
        intra = jnp.dot(s.astype(BF16), jnp.concatenate([vf.astype(BF16), ones], axis=1),
                        preferred_element_type=F32)
        inter = jnp.dot(qb, ct.astype(BF16), preferred_element_type=F32)
        num = intra[:, :dv] + lanes2(w_inter) * inter[:, :dv]
        den = intra[:, dv:] + w_inter * inter[:, dv:]
        inv = 1.0 / jnp.maximum(jnp.abs(den), jnp.exp(-(b_col + m_row)))
        h_ref[pl.ds(r0, chunk), :] = num * lanes2(inv)

        m_new = jnp.maximum(total + m, total + gmax_end)
        w_src = jnp.exp(total + gg_col - m_new)
        decay = jnp.exp(total + m - m_new)
        wv = jnp.concatenate([lanes2(w_src) * vf, w_src], axis=1).astype(BF16)
        ct_ref[d] = decay * ct + lax.dot_general(
            kb, wv, (((0,), (0,)), ((), ())), preferred_element_type=F32)
        return m_new

    def body(c, carry):
        m_f, m_b = carry
        m_f = chunk_step(c, m_f, 0, hf_ref)
        m_b = chunk_step(nc - 1 - c, m_b, 1, hb_ref)
        return m_f, m_b

    m0 = jnp.full((1, 1), NEG_BIG, F32)
    lax.fori_loop(0, nc, body, (m0, m0))

    hg = hg_ref[0]

    def finish(c, carry):
        r0 = pl.multiple_of(c * chunk, chunk)
        hm = hf_ref[pl.ds(r0, chunk), :] + hb_ref[pl.ds(r0, chunk), :]
        ms = jnp.mean(hm * hm, axis=-1, keepdims=True)
        hn = hm * lax.rsqrt(ms + RMS_EPS) * hg
        gate = jax.nn.sigmoid(og_ref[0, pl.ds(r0, chunk), :])
        y_ref[0, pl.ds(r0, chunk), :] = (hn * gate).astype(y_ref.dtype)
        return carry

    lax.fori_loop(0, nc, finish, 0)


def mlstm_mixer(z, gates_rows, gate_b_rows, head_g, chunk=MLSTM_CHUNK):
    b, s, _ = z.shape
    h, dk, dv = MLSTM_HEADS, MLSTM_DK, MLSTM_DV
    nc = s // chunk
    assert s % chunk == 0 and 6 * nc <= LANES and chunk % LANES == 0
    kq = h * dk // dk
    kv = 2 * h * dk // dv
    ko = kv + h
    return pl.pallas_call(
        functools.partial(_mlstm_body, nc=nc, chunk=chunk),
        grid=(b, h),
        in_specs=[
            pl.BlockSpec((1, s, dk), lambda i, j: (i, 0, j)),
            pl.BlockSpec((1, s, dk), lambda i, j: (i, 0, kq + j)),
            pl.BlockSpec((1, s, dv), lambda i, j: (i, 0, kv + j)),
            pl.BlockSpec((1, s, dv), lambda i, j: (i, 0, ko + j)),
            pl.BlockSpec((1, 1, 4, nc, chunk), lambda i, j: (i, j, 0, 0, 0)),
            pl.BlockSpec((1, 4, 1, chunk), lambda i, j: (j, 0, 0, 0)),
            pl.BlockSpec((1, 1, dv), lambda i, j: (j, 0, 0)),
        ],
        out_specs=pl.BlockSpec((1, s, dv), lambda i, j: (i, 0, j)),
        out_shape=jax.ShapeDtypeStruct((b, s, h * dv), BF16),
        scratch_shapes=[
            pltpu.VMEM((6, nc, chunk), F32),
            pltpu.VMEM((nc, chunk, LANES), F32),
            pltpu.VMEM((s, dv), F32),
            pltpu.VMEM((s, dv), F32),
            pltpu.VMEM((2, dk, dv + LANES), F32),
        ],
        compiler_params=_cparams("parallel", "parallel"),
        name="mlstm",
    )(z, z, z, z, gates_rows, gate_b_rows, head_g)


RG_TILE = 512
RG_PAD = SUBLANES
RG_SCAN_UNROLL = 8


def _softplus(x):
    return jnp.maximum(x, 0.0) + jnp.log1p(jnp.exp(-jnp.abs(x)))


def _gelu_tanh(x):
    c = math.sqrt(2.0 / math.pi)
    return x * (0.5 * (1.0 + jnp.tanh(c * (x + 0.044715 * (x * x * x)))))


def _rglru_body(x_ref, gr_ref, cw_ref, cb_ref, wa_ref, wx_ref, ba_ref, bx_ref, lam_ref, y_ref,
                xpad_ref, a_ref, u_ref, h_ref, *, s):
    nt = s // RG_TILE
    zeros = jnp.zeros((RG_PAD, LANES), F32)
    xpad_ref[pl.ds(0, RG_PAD), :] = zeros
    xpad_ref[pl.ds(RG_PAD + s, RG_PAD), :] = zeros

    def copy_in(t, carry):
        r0 = pl.multiple_of(t * RG_TILE, RG_TILE)
        xpad_ref[pl.ds(RG_PAD + r0, RG_TILE), :] = x_ref[0, pl.ds(r0, RG_TILE), :]
        return carry

    lax.fori_loop(0, nt, copy_in, 0)

    cw = cw_ref[...]
    cb = cb_ref[...]
    neg_sp = [-RGLRU_C * _softplus(-lam_ref[d:d + 1, :]) for d in range(2)]

    def gates(t, carry):
        r0 = pl.multiple_of(t * RG_TILE, RG_TILE)
        xc = None
        for j in range(CONV_WIDTH):
            tap = xpad_ref[pl.ds(r0 + RG_PAD - CONV_LEFT + j, RG_TILE), :] * cw[j:j + 1, :]
            xc = tap if xc is None else xc + tap
        xc = xc + cb
        xcb = xc.astype(BF16)
        for d in range(2):
            r = jax.nn.sigmoid(jnp.dot(xcb, wa_ref[d, 0], preferred_element_type=F32)
                               + ba_ref[d:d + 1, :])
            i = jax.nn.sigmoid(jnp.dot(xcb, wx_ref[d, 0], preferred_element_type=F32)
                               + bx_ref[d:d + 1, :])
            log_a = neg_sp[d] * r
            a = jnp.exp(log_a)
            one_minus_a2 = -jnp.tanh(log_a) * (1.0 + a * a)
            a, u = block_scan(a, jnp.sqrt(one_minus_a2) * (i * xc), reverse=d == 1)
            a_ref[d, pl.ds(r0, RG_TILE), :] = a
            u_ref[d, pl.ds(r0, RG_TILE), :] = u
        return carry

    row = lax.broadcasted_iota(jnp.int32, (RG_TILE // SUBLANES, SUBLANES, LANES), 1)

    def block_scan(a, u, reverse):
        a = a.reshape(RG_TILE // SUBLANES, SUBLANES, LANES)
        u = u.reshape(RG_TILE // SUBLANES, SUBLANES, LANES)
        sh = 1
        while sh < SUBLANES:
            keep = row < SUBLANES - sh if reverse else row >= sh
            amount = SUBLANES - sh if reverse else sh
            a_s = jnp.where(keep, pltpu.roll(a, amount, axis=1), 1.0)
            u_s = jnp.where(keep, pltpu.roll(u, amount, axis=1), 0.0)
            u = a * u_s + u
            a = a * a_s
            sh *= 2
        return a.reshape(RG_TILE, LANES), u.reshape(RG_TILE, LANES)

    lax.fori_loop(0, nt, gates, 0)

    nblk = s // SUBLANES

    def scan(j, carry):
        h_f, h_b = carry
        for d, last, h in ((0, SUBLANES - 1, h_f), (1, 0, h_b)):
            blk = j if d == 0 else nblk - 1 - j
            rows = pl.ds(pl.multiple_of(blk * SUBLANES, SUBLANES), SUBLANES)
            a, u = a_ref[d, rows, :], u_ref[d, rows, :]
            h_ref[d, rows, :] = a * h + u
            a_end = jnp.broadcast_to(a[last:last + 1, :], (SUBLANES, LANES))
            u_end = jnp.broadcast_to(u[last:last + 1, :], (SUBLANES, LANES))
            if d == 0:
                h_f = a_end * h + u_end
            else:
                h_b = a_end * h + u_end
        return h_f, h_b

    h0 = jnp.zeros((SUBLANES, LANES), F32)
    lax.fori_loop(0, nblk, scan, (h0, h0), unroll=RG_SCAN_UNROLL)

    def finish(t, carry):
        r0 = pl.multiple_of(t * RG_TILE, RG_TILE)
        hr = h_ref[0, pl.ds(r0, RG_TILE), :] + h_ref[1, pl.ds(r0, RG_TILE), :]
        y_ref[0, pl.ds(r0, RG_TILE), :] = (
            hr * _gelu_tanh(gr_ref[0, pl.ds(r0, RG_TILE), :])).astype(y_ref.dtype)
        return carry

    lax.fori_loop(0, nt, finish, 0)


def rglru_mixer(z, xr_block0, gr_block0, conv_w, conv_b, wa, wx, ba, bx, lam):
    b, s, _ = z.shape
    width = RNN_BLOCKS * RNN_BLOCK
    assert s % RG_TILE == 0
    return pl.pallas_call(
        functools.partial(_rglru_body, s=s),
        grid=(b, RNN_BLOCKS),
        in_specs=[
            pl.BlockSpec((1, s, RNN_BLOCK), lambda i, j: (i, 0, xr_block0 + j)),
            pl.BlockSpec((1, s, RNN_BLOCK), lambda i, j: (i, 0, gr_block0 + j)),
            pl.BlockSpec((CONV_WIDTH, RNN_BLOCK), lambda i, j: (0, j)),
            pl.BlockSpec((1, RNN_BLOCK), lambda i, j: (0, j)),
            pl.BlockSpec((2, 1, RNN_BLOCK, RNN_BLOCK), lambda i, j: (0, j, 0, 0)),
            pl.BlockSpec((2, 1, RNN_BLOCK, RNN_BLOCK), lambda i, j: (0, j, 0, 0)),
            pl.BlockSpec((2, RNN_BLOCK), lambda i, j: (0, j)),
            pl.BlockSpec((2, RNN_BLOCK), lambda i, j: (0, j)),
            pl.BlockSpec((2, RNN_BLOCK), lambda i, j: (0, j)),
        ],
        out_specs=pl.BlockSpec((1, s, RNN_BLOCK), lambda i, j: (i, 0, j)),
        out_shape=jax.ShapeDtypeStruct((b, s, width), BF16),
        scratch_shapes=[
            pltpu.VMEM((s + 2 * RG_PAD, RNN_BLOCK), F32),
            pltpu.VMEM((2, s, RNN_BLOCK), F32),
            pltpu.VMEM((2, s, RNN_BLOCK), F32),
            pltpu.VMEM((2, s, RNN_BLOCK), F32),
        ],
        compiler_params=_cparams("parallel", "parallel"),
        name="rglru",
    )(z, z, conv_w, conv_b, wa, wx, ba, bx, lam)


ATTN_TQ = 128
ATTN_TK = ATTN_TQ + 2 * ATTN_HALF
ATTN_MERGE_TILE = 256
ATTN_UNROLL = 4
LOG2E = math.log2(math.e)


def _attn_body(q_ref, k_ref, v_ref, slope_ref, o_ref, bias_ref, x4_ref, og_ref, lse_ref,
               sca_ref, scb_ref, *, s):
    qscale = ATTN_DH ** -0.5 * LOG2E
    slope = slope_ref[0, 0:1, 0:1] * LOG2E
    qi = lax.broadcasted_iota(jnp.int32, (ATTN_TQ, ATTN_TK), 0)
    kj = lax.broadcasted_iota(jnp.int32, (ATTN_TQ, ATTN_TK), 1)
    for g, dil in enumerate(ATTN_DILATIONS):
        for e in range(3):
            rel = jnp.abs(kj - e * ATTN_HALF - qi)
            pen = slope * (rel * dil).astype(F32)
            bias_ref[g, e] = jnp.where(rel <= ATTN_HALF, -pen, NEG_BIG)

    s4 = s // 4
    srcs = (q_ref, k_ref, v_ref)

    def split4(t, carry):
        c = t // (s4 // ATTN_TK)
        p0 = (t % (s4 // ATTN_TK)) * ATTN_TK
        dst = pl.ds(pl.multiple_of(c * s4 + p0, ATTN_TK), ATTN_TK)
        for a in range(3):
            x = srcs[a][0, pl.ds(c + 4 * p0, ATTN_TK, stride=4), :]
            x4_ref[a, dst, :] = x * qscale if a == 0 else x
        return carry

    lax.fori_loop(0, 4 * (s4 // ATTN_TK), split4, 0)

    ones = jnp.ones((ATTN_TK, ATTN_DH), BF16)

    stages = []
    for g, dil in enumerate(ATTN_DILATIONS):
        sp = s // dil
        nqb = sp // ATTN_TQ

        def block_rows(t, dil=dil, sp=sp, nqb=nqb):
            r = t // nqb
            p0 = (t % nqb) * ATTN_TQ
            kstart = jnp.clip(p0 - ATTN_HALF, 0, sp - ATTN_TK)
            e = (p0 - kstart) // ATTN_HALF
            if dil == 1:
                qrows = pl.ds(pl.multiple_of(p0, ATTN_TQ), ATTN_TQ)
                krows = pl.ds(pl.multiple_of(kstart, ATTN_HALF), ATTN_TK)
                orows = qrows
            elif dil == 4:
                qrows = pl.ds(pl.multiple_of(r * s4 + p0, ATTN_TQ), ATTN_TQ)
                krows = pl.ds(pl.multiple_of(r * s4 + kstart, ATTN_HALF), ATTN_TK)
                orows = pl.ds(r + 4 * p0, ATTN_TQ, stride=4)
            else:
                base = (r % 4) * s4 + r // 4
                qrows = pl.ds(base + 4 * p0, ATTN_TQ, stride=4)
                krows = pl.ds(base + 4 * kstart, ATTN_TK, stride=4)
                orows = pl.ds(r + dil * p0, ATTN_TQ, stride=dil)
            return qrows, krows, orows, e

        def score_stage(trip, sc_ref, g=g, dil=dil, block_rows=block_rows):
            for u in range(ATTN_UNROLL):
                qrows, krows, _, e = block_rows(trip * ATTN_UNROLL + u)
                if dil == 1:
                    qf, kf = q_ref[0, qrows, :] * qscale, k_ref[0, krows, :]
                else:
                    qf, kf = x4_ref[0, qrows, :], x4_ref[1, krows, :]
                sc_ref[u] = lax.dot_general(
                    qf.astype(BF16), kf.astype(BF16), (((1,), (1,)), ((), ())),
                    preferred_element_type=F32) + bias_ref[g, e]

        def value_stage(trip, sc_ref, g=g, dil=dil, block_rows=block_rows):
            for u in range(ATTN_UNROLL):
                _, krows, orows, _ = block_rows(trip * ATTN_UNROLL + u)
                vf = v_ref[0, krows, :] if dil == 1 else x4_ref[2, krows, :]
                sc = sc_ref[u]
                m = jnp.max(sc, axis=-1, keepdims=True)
                p = jnp.exp2(sc - m).astype(BF16)
                pv = jnp.dot(p, jnp.concatenate([vf.astype(BF16), ones], axis=1),
                             preferred_element_type=F32)
                den = pv[:, ATTN_DH:]
                og_ref[g, orows, :] = pv[:, :ATTN_DH] / den
                lse_ref[g, orows, :] = m + jnp.log2(den)

        stages.append((score_stage, value_stage, dil * nqb // ATTN_UNROLL))

    stages[0][0](0, sca_ref)
    for g, (score_stage, value_stage, ntrip) in enumerate(stages):

        def trip_pair(i, carry, score_stage=score_stage, value_stage=value_stage):
            score_stage(2 * i + 1, scb_ref)
            value_stage(2 * i, sca_ref)
            score_stage(2 * i + 2, sca_ref)
            value_stage(2 * i + 1, scb_ref)
            return carry

        lax.fori_loop(0, ntrip // 2 - 1, trip_pair, 0)
        score_stage(ntrip - 1, scb_ref)
        value_stage(ntrip - 2, sca_ref)
        if g + 1 < len(stages):
            stages[g + 1][0](0, sca_ref)
        value_stage(ntrip - 1, scb_ref)

    def merge(t, carry):
        r0 = pl.multiple_of(t * ATTN_MERGE_TILE, ATTN_MERGE_TILE)
        sl = pl.ds(r0, ATTN_MERGE_TILE)
        l0, l1, l2 = lse_ref[0, sl, :], lse_ref[1, sl, :], lse_ref[2, sl, :]
        mx = jnp.maximum(jnp.maximum(l0, l1), l2)
        e0, e1, e2 = jnp.exp2(l0 - mx), jnp.exp2(l1 - mx), jnp.exp2(l2 - mx)
        inv = 1.0 / (e0 + e1 + e2)
        o = (e0 * inv) * og_ref[0, sl, :] + (e1 * inv) * og_ref[1, sl, :] + (e2 * inv) * og_ref[2, sl, :]
        o_ref[0, sl, :] = o.astype(o_ref.dtype)
        return carry

    lax.fori_loop(0, s // ATTN_MERGE_TILE, merge, 0)


def dilated_attention(qkv, slopes):
    b, s, _ = qkv.shape
    h, dh = ATTN_HEADS, ATTN_DH
    assert s % (max(ATTN_DILATIONS) * ATTN_TK) == 0 and s % ATTN_MERGE_TILE == 0
    return pl.pallas_call(
        functools.partial(_attn_body, s=s),
        grid=(b, h),
        in_specs=[
            pl.BlockSpec((1, s, dh), lambda i, j: (i, 0, j)),
            pl.BlockSpec((1, s, dh), lambda i, j: (i, 0, h + j)),
            pl.BlockSpec((1, s, dh), lambda i, j: (i, 0, 2 * h + j)),
            pl.BlockSpec((1, SUBLANES, LANES), lambda i, j: (j, 0, 0)),
        ],
        out_specs=pl.BlockSpec((1, s, dh), lambda i, j: (i, 0, j)),
        out_shape=jax.ShapeDtypeStruct((b, s, h * dh), BF16),
        scratch_shapes=[
            pltpu.VMEM((len(ATTN_DILATIONS), 3, ATTN_TQ, ATTN_TK), F32),
            pltpu.VMEM((3, s, dh), F32),
            pltpu.VMEM((len(ATTN_DILATIONS), s, dh), F32),
            pltpu.VMEM((len(ATTN_DILATIONS), s, dh), F32),
            pltpu.VMEM((ATTN_UNROLL, ATTN_TQ, ATTN_TK), F32),
            pltpu.VMEM((ATTN_UNROLL, ATTN_TQ, ATTN_TK), F32),
        ],
        compiler_params=_cparams("parallel", "parallel"),
        name="dilated_attn",
    )(qkv, qkv, qkv, slopes)


def _ffn(x, norm_g, w1, w3, w2):
    u = norm_swiglu_up(x, norm_g, w1.astype(BF16), w3.astype(BF16))
    return matmul([(u, w2.astype(BF16))], res=x, tn=512, name="ffn_down")


def _even_layer(x, bsz, seq, norm_g, w_in, gate_b, conv_w, conv_b, rg_wa, rg_ba, rg_wx, rg_bx,
                rg_lam, head_g, w_out):
    heads, dk, dv = MLSTM_HEADS, MLSTM_DK, MLSTM_DV
    n_qkvo = 2 * heads * dk + 2 * heads * dv
    n_gate = 4 * heads
    width = RNN_BLOCKS * RNN_BLOCK
    w_main = jnp.concatenate([w_in[:, :n_qkvo], w_in[:, n_qkvo + n_gate:]], axis=1).astype(BF16)
    w_gate = jnp.pad(w_in[:, n_qkvo:n_qkvo + n_gate], ((0, 0), (0, LANES - n_gate))).astype(BF16)
    z, zg = norm_matmul(x, norm_g, w_main, w_side=w_gate, name="in_proj")
    z = z.reshape(bsz, seq, -1)

    chunk = MLSTM_CHUNK
    nc = seq // chunk
    gates_rows = jnp.transpose(zg[:, :n_gate].reshape(bsz, nc, chunk, 4, heads), (0, 4, 3, 1, 2))
    gate_b_rows = jnp.broadcast_to(
        jnp.transpose(gate_b.astype(F32).reshape(4, heads))[:, :, None, None], (heads, 4, 1, chunk))
    y_a = mlstm_mixer(z, gates_rows, gate_b_rows, head_g.astype(F32).reshape(heads, 1, dv), chunk)

    y_b = rglru_mixer(z, n_qkvo // RNN_BLOCK, (n_qkvo + width) // RNN_BLOCK,
                      conv_w.astype(F32), conv_b.astype(F32).reshape(1, width),
                      rg_wa.astype(BF16), rg_wx.astype(BF16),
                      rg_ba.astype(F32), rg_bx.astype(F32), rg_lam.astype(F32))

    w_out_b = w_out.astype(BF16)
    m = bsz * seq
    return matmul([(y_a.reshape(m, -1), w_out_b[:heads * dv]),
                   (y_b.reshape(m, -1), w_out_b[heads * dv:])], res=x, name="out_proj")


def _odd_layer(x, bsz, seq, norm_g, w_qkv, w_o):
    qkv = norm_matmul(x, norm_g, w_qkv.astype(BF16), name="qkv_proj").reshape(bsz, seq, -1)
    slopes = jnp.exp2(-ALIBI_MAX_BIAS * jnp.arange(1, ATTN_HEADS + 1, dtype=F32) / ATTN_HEADS)
    slopes = jnp.broadcast_to(slopes[:, None, None], (ATTN_HEADS, SUBLANES, LANES))
    o = dilated_attention(qkv, slopes)
    return matmul([(o.reshape(bsz * seq, -1), w_o.astype(BF16))], res=x, name="attn_out_proj")


def kernel(x, e_norm, e_w_in, e_gate_b, e_conv_w, e_conv_b, e_rg_wa, e_rg_ba, e_rg_wx, e_rg_bx,
           e_rg_lam, e_head_g, e_w_out, o_norm, o_w_qkv, o_w_o, f_norm, f_w1, f_w3, f_w2,
           final_norm):
    bsz, seq, d = x.shape
    depth = f_norm.shape[0]
    xs = x.reshape(bsz * seq, d).astype(F32)
    for l in range(depth):
        if l % 2 == 0:
            e = l // 2
            xs = _even_layer(xs, bsz, seq, e_norm[e], e_w_in[e], e_gate_b[e], e_conv_w[e],
                             e_conv_b[e], e_rg_wa[e], e_rg_ba[e], e_rg_wx[e], e_rg_bx[e],
                             e_rg_lam[e], e_head_g[e], e_w_out[e])
        else:
            o = l // 2
            xs = _odd_layer(xs, bsz, seq, o_norm[o], o_w_qkv[o], o_w_o[o])
        xs = _ffn(xs, f_norm[l], f_w1[l], f_w3[l], f_w2[l])
    return rmsnorm(xs, final_norm, x.dtype).reshape(bsz, seq, d)
```

```python
import functools
import math

import jax
import jax.numpy as jnp
from jax import lax
from jax.experimental import pallas as pl
from jax.experimental.pallas import tpu as pltpu

F32 = jnp.float32
BF16 = jnp.bfloat16

RMS_EPS = 1e-6
NEG_BIG = -1e30

MLSTM_HEADS = 4
MLSTM_DK = 128
MLSTM_DV = 256
MLSTM_CHUNK = 256
RNN_BLOCKS = 8
RNN_BLOCK = 128
CONV_WIDTH = 4
CONV_LEFT = 2
RGLRU_C = 8.0
ATTN_HEADS = 16
ATTN_DH = 128
ATTN_HALF = 64
ATTN_DILATIONS = (1, 4, 16)
ALIBI_MAX_BIAS = 8.0

LANES = 128
SUBLANES = 8
VMEM_LIMIT_BYTES = 56 * 1024 * 1024


def _cparams(*semantics):
    return pltpu.CompilerParams(dimension_semantics=semantics,
                                vmem_limit_bytes=VMEM_LIMIT_BYTES)


def _rmsnorm_body(x_ref, g_ref, o_ref):
    x = x_ref[...]
    ms = jnp.mean(x * x, axis=-1, keepdims=True)
    o_ref[...] = (x * lax.rsqrt(ms + RMS_EPS) * g_ref[...]).astype(o_ref.dtype)


def rmsnorm(x, g, out_dtype, tm=512):
    m, d = x.shape
    return pl.pallas_call(
        _rmsnorm_body,
        grid=(m // tm,),
        in_specs=[pl.BlockSpec((tm, d), lambda i: (i, 0)),
                  pl.BlockSpec((1, d), lambda i: (0, 0))],
        out_specs=pl.BlockSpec((tm, d), lambda i: (i, 0)),
        out_shape=jax.ShapeDtypeStruct((m, d), out_dtype),
        compiler_params=_cparams("parallel"),
        name="rmsnorm",
    )(x, g.reshape(1, d).astype(F32))


NORM_ROWS = 128


def _norm_rows_into(x_ref, g_ref, h_ref):
    g = g_ref[...]

    def rows(t, carry):
        sl = pl.ds(pl.multiple_of(t * NORM_ROWS, NORM_ROWS), NORM_ROWS)
        x = x_ref[sl, :]
        ms = jnp.mean(x * x, axis=-1, keepdims=True)
        h_ref[sl, :] = (x * lax.rsqrt(ms + RMS_EPS) * g).astype(h_ref.dtype)
        return carry

    lax.fori_loop(0, x_ref.shape[0] // NORM_ROWS, rows, 0)


def _matmul_body(*refs, n_pairs, has_res, has_norm):
    pairs = [(refs[2 * p], refs[2 * p + 1]) for p in range(n_pairs)]
    rest = list(refs[2 * n_pairs:])
    r_ref = rest.pop(0) if has_res else None
    g_ref = rest.pop(0) if has_norm else None
    o_ref = rest.pop(0)

    acc = None
    for a_ref, w_ref in pairs:
        d = jnp.dot(a_ref[...], w_ref[...], preferred_element_type=F32)
        acc = d if acc is None else acc + d
    if has_res:
        acc = acc + r_ref[...]
    o_ref[...] = acc.astype(o_ref.dtype)
    if has_norm:
        _norm_rows_into(o_ref, g_ref, rest.pop(0))


def matmul(pairs, res=None, norm_g=None, out_dtype=F32, tm=1024, tn=1024, name="matmul"):
    m = pairs[0][0].shape[0]
    n = pairs[0][1].shape[1]
    tn = min(tn, n)
    assert m % tm == 0 and n % tn == 0
    assert norm_g is None or (tn == n and out_dtype == F32 and tm % NORM_ROWS == 0)
    in_specs, args = [], []
    for a, w in pairs:
        kdim = a.shape[1]
        in_specs += [pl.BlockSpec((tm, kdim), lambda i, j: (i, 0)),
                     pl.BlockSpec((kdim, tn), lambda i, j: (0, j))]
        args += [a, w]
    if res is not None:
        in_specs.append(pl.BlockSpec((tm, tn), lambda i, j: (i, j)))
        args.append(res)
    out_specs = [pl.BlockSpec((tm, tn), lambda i, j: (i, j))]
    out_shape = [jax.ShapeDtypeStruct((m, n), out_dtype)]
    if norm_g is not None:
        in_specs.append(pl.BlockSpec((1, n), lambda i, j: (0, 0)))
        args.append(norm_g.reshape(1, n).astype(F32))
        out_specs.append(pl.BlockSpec((tm, n), lambda i, j: (i, 0)))
        out_shape.append(jax.ShapeDtypeStruct((m, n), BF16))
    outs = pl.pallas_call(
        functools.partial(_matmul_body, n_pairs=len(pairs), has_res=res is not None,
                          has_norm=norm_g is not None),
        grid=(m // tm, n // tn),
        in_specs=in_specs,
        out_specs=out_specs,
        out_shape=out_shape,
        compiler_params=_cparams("parallel", "parallel"),
        name=name,
    )(*args)
    return outs if norm_g is not None else outs[0]


def _swiglu_up_body(a_ref, w1_ref, w3_ref, o_ref):
    a = a_ref[...]
    u = jnp.dot(a, w1_ref[...], preferred_element_type=F32)
    v = jnp.dot(a, w3_ref[...], preferred_element_type=F32)
    o_ref[...] = (u * jax.nn.sigmoid(u) * v).astype(o_ref.dtype)


def swiglu_up(a, w1, w3, tm=2048, tn=512):
    m, kdim = a.shape
    n = w1.shape[1]
    assert m % tm == 0 and n % tn == 0
    return pl.pallas_call(
        _swiglu_up_body,
        grid=(m // tm, n // tn),
        in_specs=[pl.BlockSpec((tm, kdim), lambda i, j: (i, 0)),
                  pl.BlockSpec((kdim, tn), lambda i, j: (0, j)),
                  pl.BlockSpec((kdim, tn), lambda i, j: (0, j))],
        out_specs=pl.BlockSpec((tm, tn), lambda i, j: (i, j)),
        out_shape=jax.ShapeDtypeStruct((m, n), BF16),
        compiler_params=_cparams("parallel", "parallel"),
        name="swiglu_up",
    )(a, w1, w3)


def _norm_matmul_body(*refs, has_side):
    if has_side:
        x_ref, g_ref, w_ref, ws_ref, o_ref, os_ref, h_ref = refs
    else:
        x_ref, g_ref, w_ref, o_ref, h_ref = refs

    @pl.when(pl.program_id(1) == 0)
    def _():
        _norm_rows_into(x_ref, g_ref, h_ref)
        if has_side:
            os_ref[...] = jnp.dot(h_ref[...], ws_ref[...], preferred_element_type=F32)

    o_ref[...] = jnp.dot(h_ref[...], w_ref[...], preferred_element_type=F32).astype(o_ref.dtype)


def norm_matmul(x, g, w, w_side=None, out_dtype=F32, tm=1024, tn=1024, name="norm_matmul"):
    m, d = x.shape
    n = w.shape[1]
    assert m % tm == 0 and n % tn == 0 and tm % NORM_ROWS == 0
    in_specs = [pl.BlockSpec((tm, d), lambda i, j: (i, 0)),
                pl.BlockSpec((1, d), lambda i, j: (0, 0)),
                pl.BlockSpec((d, tn), lambda i, j: (0, j))]
    args = [x, g.reshape(1, d).astype(F32), w]
    out_specs = [pl.BlockSpec((tm, tn), lambda i, j: (i, j))]
    out_shape = [jax.ShapeDtypeStruct((m, n), out_dtype)]
    if w_side is not None:
        ns = w_side.shape[1]
        in_specs.append(pl.BlockSpec((d, ns), lambda i, j: (0, 0)))
        args.append(w_side)
        out_specs.append(pl.BlockSpec((tm, ns), lambda i, j: (i, 0)))
        out_shape.append(jax.ShapeDtypeStruct((m, ns), F32))
    outs = pl.pallas_call(
        functools.partial(_norm_matmul_body, has_side=w_side is not None),
        grid=(m // tm, n // tn),
        in_specs=in_specs,
        out_specs=out_specs,
        out_shape=out_shape,
        scratch_shapes=[pltpu.VMEM((tm, d), BF16)],
        compiler_params=_cparams("parallel", "arbitrary"),
        name=name,
    )(*args)
    return outs if w_side is not None else outs[0]


def _log_sigmoid(x):
    return jnp.minimum(x, 0.0) - jnp.log1p(jnp.exp(-jnp.abs(x)))


def _lane_scan(x, combine, identity, reverse):
    width = x.shape[-1]
    lane = lax.broadcasted_iota(jnp.int32, x.shape, 1)
    sh = 1
    while sh < width:
        if reverse:
            shifted = jnp.where(lane < width - sh, pltpu.roll(x, width - sh, axis=1), identity)
        else:
            shifted = jnp.where(lane >= sh, pltpu.roll(x, sh, axis=1), identity)
        x = combine(x, shifted)
        sh *= 2
    return x


def _mlstm_body(q_ref, k_ref, v_ref, og_ref, g_ref, gb_ref, hg_ref, y_ref,
                rows_ref, cols_ref, hf_ref, hb_ref, ct_ref, *, nc, chunk):
    dk, dv = MLSTM_DK, MLSTM_DV
    scale = dk ** -0.5

    g = g_ref[0, 0] + gb_ref[0]
    rows = []
    for d in range(2):
        b = _lane_scan(_log_sigmoid(g[2 * d + 1]), jnp.add, 0.0, reverse=d == 1)
        gg = g[2 * d] - b
        gmax = _lane_scan(gg, jnp.maximum, -jnp.inf, reverse=d == 1)
        rows += [gg, b, gmax]
        for j, r in enumerate((gg, b, gmax)):
            rows_ref[3 * d + j] = r
    rowmat = jnp.concatenate(rows + [jnp.zeros((LANES - 6 * nc, chunk), F32)], axis=0)
    colmat = rowmat.T
    for c in range(nc):
        cols_ref[c] = colmat if c == 0 else pltpu.roll(colmat, LANES - c, axis=1)

    ct_ref[...] = jnp.zeros_like(ct_ref)

    row_id = lax.broadcasted_iota(jnp.int32, (chunk, chunk), 0)
    col_id = lax.broadcasted_iota(jnp.int32, (chunk, chunk), 1)
    causal = (row_id >= col_id, row_id <= col_id)
    ones = jnp.ones((chunk, LANES), BF16)

    def lanes2(x):
        return jnp.concatenate([x] * (dv // LANES), axis=1)

    def chunk_step(c, m, d, h_ref):
        r0 = pl.multiple_of(c * chunk, chunk)
        qb = (q_ref[0, pl.ds(r0, chunk), :] * scale).astype(BF16)
        kb = k_ref[0, pl.ds(r0, chunk), :].astype(BF16)
        vf = v_ref[0, pl.ds(r0, chunk), :]
        cols = cols_ref[c]

        def column(j):
            lane = (3 * d + j) * nc
            return jnp.broadcast_to(cols[:, lane:lane + 1], (chunk, LANES))

        gg_col, b_col, gmax_col = column(0), column(1), column(2)
        gg_row = rows_ref[3 * d, pl.ds(c, 1), :]
        end = chunk - 1 if d == 0 else 0
        total = rows_ref[3 * d + 1, pl.ds(c, 1), :][:, end:end + 1]
        gmax_end = rows_ref[3 * d + 2, pl.ds(c, 1), :][:, end:end + 1]

        m_row = jnp.maximum(gmax_col, m)
        w_intra = jnp.exp(jnp.where(causal[d], gg_row - lanes2(m_row), NEG_BIG))
        w_inter = jnp.exp(m - m_row)
        s = lax.dot_general(qb, kb, (((1,), (1,)), ((), ())),
                            preferred_element_type=F32) * w_intra
        ct = ct_ref[d]
        intra = jnp.dot(s.astype(BF16), jnp.concatenate([vf.astype(BF16), ones], axis=1),
                        preferred_element_type=F32)
        inter = jnp.dot(qb, ct.astype(BF16), preferred_element_type=F32)
        num = intra[:, :dv] + lanes2(w_inter) * inter[:, :dv]
        den = intra[:, dv:] + w_inter * inter[:, dv:]
        inv = 1.0 / jnp.maximum(jnp.abs(den), jnp.exp(-(b_col + m_row)))
        h_ref[pl.ds(r0, chunk), :] = num * lanes2(inv)

        m_new = jnp.maximum(total + m, total + gmax_end)
        w_src = jnp.exp(total + gg_col - m_new)
        decay = jnp.exp(total + m - m_new)
        wv = jnp.concatenate([lanes2(w_src) * vf, w_src], axis=1).astype(BF16)
        ct_ref[d] = decay * ct + lax.dot_general(
            kb, wv, (((0,), (0,)), ((), ())), preferred_element_type=F32)
        return m_new

    def body(c, carry):
        m_f, m_b = carry
        m_f = chunk_step(c, m_f, 0, hf_ref)
        m_b = chunk_step(nc - 1 - c, m_b, 1, hb_ref)
        return m_f, m_b

    m0 = jnp.full((1, 1), NEG_BIG, F32)
    lax.fori_loop(0, nc, body, (m0, m0))

    hg = hg_ref[0]

    def finish(c, carry):
        r0 = pl.multiple_of(c * chunk, chunk)
        hm = hf_ref[pl.ds(r0, chunk), :] + hb_ref[pl.ds(r0, chunk), :]
        ms = jnp.mean(hm * hm, axis=-1, keepdims=True)
        hn = hm * lax.rsqrt(ms + RMS_EPS) * hg
        gate = jax.nn.sigmoid(og_ref[0, pl.ds(r0, chunk), :])
        y_ref[0, pl.ds(r0, chunk), :] = (hn * gate).astype(y_ref.dtype)
        return carry

    lax.fori_loop(0, nc, finish, 0)


def mlstm_mixer(z, gates_rows, gate_b_rows, head_g, chunk=MLSTM_CHUNK):
    b, s, _ = z.shape
    h, dk, dv = MLSTM_HEADS, MLSTM_DK, MLSTM_DV
    nc = s // chunk
    assert s % chunk == 0 and 6 * nc <= LANES and chunk % LANES == 0
    kq = h * dk // dk
    kv = 2 * h * dk // dv
    ko = kv + h
    return pl.pallas_call(
        functools.partial(_mlstm_body, nc=nc, chunk=chunk),
        grid=(b, h),
        in_specs=[
            pl.BlockSpec((1, s, dk), lambda i, j: (i, 0, j)),
            pl.BlockSpec((1, s, dk), lambda i, j: (i, 0, kq + j)),
            pl.BlockSpec((1, s, dv), lambda i, j: (i, 0, kv + j)),
            pl.BlockSpec((1, s, dv), lambda i, j: (i, 0, ko + j)),
            pl.BlockSpec((1, 1, 4, nc, chunk), lambda i, j: (i, j, 0, 0, 0)),
            pl.BlockSpec((1, 4, 1, chunk), lambda i, j: (j, 0, 0, 0)),
            pl.BlockSpec((1, 1, dv), lambda i, j: (j, 0, 0)),
        ],
        out_specs=pl.BlockSpec((1, s, dv), lambda i, j: (i, 0, j)),
        out_shape=jax.ShapeDtypeStruct((b, s, h * dv), BF16),
        scratch_shapes=[
            pltpu.VMEM((6, nc, chunk), F32),
            pltpu.VMEM((nc, chunk, LANES), F32),
            pltpu.VMEM((s, dv), F32),
            pltpu.VMEM((s, dv), F32),
            pltpu.VMEM((2, dk, dv + LANES), F32),
        ],
        compiler_params=_cparams("parallel", "parallel"),
        name="mlstm",
    )(z, z, z, z, gates_rows, gate_b_rows, head_g)


RG_TILE = 512
RG_PAD = SUBLANES
RG_SCAN_UNROLL = 8


def _softplus(x):
    return jnp.maximum(x, 0.0) + jnp.log1p(jnp.exp(-jnp.abs(x)))


def _gelu_tanh(x):
    c = math.sqrt(2.0 / math.pi)
    return x * (0.5 * (1.0 + jnp.tanh(c * (x + 0.044715 * (x * x * x)))))


def _rglru_body(x_ref, gr_ref, cw_ref, cb_ref, wa_ref, wx_ref, ba_ref, bx_ref, lam_ref, y_ref,
                xpad_ref, a_ref, u_ref, h_ref, *, s):
    nt = s // RG_TILE
    zeros = jnp.zeros((RG_PAD, LANES), F32)
    xpad_ref[pl.ds(0, RG_PAD), :] = zeros
    xpad_ref[pl.ds(RG_PAD + s, RG_PAD), :] = zeros

    def copy_in(t, carry):
        r0 = pl.multiple_of(t * RG_TILE, RG_TILE)
        xpad_ref[pl.ds(RG_PAD + r0, RG_TILE), :] = x_ref[0, pl.ds(r0, RG_TILE), :]
        return carry

    lax.fori_loop(0, nt, copy_in, 0)

    cw = cw_ref[...]
    cb = cb_ref[...]
    neg_sp = [-RGLRU_C * _softplus(-lam_ref[d:d + 1, :]) for d in range(2)]

    def gates(t, carry):
        r0 = pl.multiple_of(t * RG_TILE, RG_TILE)
        xc = None
        for j in range(CONV_WIDTH):
            tap = xpad_ref[pl.ds(r0 + RG_PAD - CONV_LEFT + j, RG_TILE), :] * cw[j:j + 1, :]
            xc = tap if xc is None else xc + tap
        xc = xc + cb
        xcb = xc.astype(BF16)
        for d in range(2):
            r = jax.nn.sigmoid(jnp.dot(xcb, wa_ref[d, 0], preferred_element_type=F32)
                               + ba_ref[d:d + 1, :])
            i = jax.nn.sigmoid(jnp.dot(xcb, wx_ref[d, 0], preferred_element_type=F32)
                               + bx_ref[d:d + 1, :])
            log_a = neg_sp[d] * r
            a = jnp.exp(log_a)
            one_minus_a2 = -jnp.tanh(log_a) * (1.0 + a * a)
            a, u = block_scan(a, jnp.sqrt(one_minus_a2) * (i * xc), reverse=d == 1)
            a_ref[d, pl.ds(r0, RG_TILE), :] = a
            u_ref[d, pl.ds(r0, RG_TILE), :] = u
        return carry

    row = lax.broadcasted_iota(jnp.int32, (RG_TILE // SUBLANES, SUBLANES, LANES), 1)

    def block_scan(a, u, reverse):
        a = a.reshape(RG_TILE // SUBLANES, SUBLANES, LANES)
        u = u.reshape(RG_TILE // SUBLANES, SUBLANES, LANES)
        sh = 1
        while sh < SUBLANES:
            keep = row < SUBLANES - sh if reverse else row >= sh
            amount = SUBLANES - sh if reverse else sh
            a_s = jnp.where(keep, pltpu.roll(a, amount, axis=1), 1.0)
            u_s = jnp.where(keep, pltpu.roll(u, amount, axis=1), 0.0)
            u = a * u_s + u
            a = a * a_s
            sh *= 2
        return a.reshape(RG_TILE, LANES), u.reshape(RG_TILE, LANES)

    lax.fori_loop(0, nt, gates, 0)

    nblk = s // SUBLANES

    def scan(j, carry):
        h_f, h_b = carry
        for d, last, h in ((0, SUBLANES - 1, h_f), (1, 0, h_b)):
            blk = j if d == 0 else nblk - 1 - j
            rows = pl.ds(pl.multiple_of(blk * SUBLANES, SUBLANES), SUBLANES)
            a, u = a_ref[d, rows, :], u_ref[d, rows, :]
            h_ref[d, rows, :] = a * h + u
            a_end = jnp.broadcast_to(a[last:last + 1, :], (SUBLANES, LANES))
            u_end = jnp.broadcast_to(u[last:last + 1, :], (SUBLANES, LANES))
            if d == 0:
                h_f = a_end * h + u_end
            else:
                h_b = a_end * h + u_end
        return h_f, h_b

    h0 = jnp.zeros((SUBLANES, LANES), F32)
    lax.fori_loop(0, nblk, scan, (h0, h0), unroll=RG_SCAN_UNROLL)

    def finish(t, carry):
        r0 = pl.multiple_of(t * RG_TILE, RG_TILE)
        hr = h_ref[0, pl.ds(r0, RG_TILE), :] + h_ref[1, pl.ds(r0, RG_TILE), :]
        y_ref[0, pl.ds(r0, RG_TILE), :] = (
            hr * _gelu_tanh(gr_ref[0, pl.ds(r0, RG_TILE), :])).astype(y_ref.dtype)
        return carry

    lax.fori_loop(0, nt, finish, 0)


def rglru_mixer(z, xr_block0, gr_block0, conv_w, conv_b, wa, wx, ba, bx, lam):
    b, s, _ = z.shape
    width = RNN_BLOCKS * RNN_BLOCK
    assert s % RG_TILE == 0
    return pl.pallas_call(
        functools.partial(_rglru_body, s=s),
        grid=(b, RNN_BLOCKS),
        in_specs=[
            pl.BlockSpec((1, s, RNN_BLOCK), lambda i, j: (i, 0, xr_block0 + j)),
            pl.BlockSpec((1, s, RNN_BLOCK), lambda i, j: (i, 0, gr_block0 + j)),
            pl.BlockSpec((CONV_WIDTH, RNN_BLOCK), lambda i, j: (0, j)),
            pl.BlockSpec((1, RNN_BLOCK), lambda i, j: (0, j)),
            pl.BlockSpec((2, 1, RNN_BLOCK, RNN_BLOCK), lambda i, j: (0, j, 0, 0)),
            pl.BlockSpec((2, 1, RNN_BLOCK, RNN_BLOCK), lambda i, j: (0, j, 0, 0)),
            pl.BlockSpec((2, RNN_BLOCK), lambda i, j: (0, j)),
            pl.BlockSpec((2, RNN_BLOCK), lambda i, j: (0, j)),
            pl.BlockSpec((2, RNN_BLOCK), lambda i, j: (0, j)),
        ],
        out_specs=pl.BlockSpec((1, s, RNN_BLOCK), lambda i, j: (i, 0, j)),
        out_shape=jax.ShapeDtypeStruct((b, s, width), BF16),
        scratch_shapes=[
            pltpu.VMEM((s + 2 * RG_PAD, RNN_BLOCK), F32),
            pltpu.VMEM((2, s, RNN_BLOCK), F32),
            pltpu.VMEM((2, s, RNN_BLOCK), F32),
            pltpu.VMEM((2, s, RNN_BLOCK), F32),
        ],
        compiler_params=_cparams("parallel", "parallel"),
        name="rglru",
    )(z, z, conv_w, conv_b, wa, wx, ba, bx, lam)


ATTN_TQ = 128
ATTN_TK = ATTN_TQ + 2 * ATTN_HALF
ATTN_MERGE_TILE = 256
ATTN_UNROLL = 8
LOG2E = math.log2(math.e)


def _attn_body(q_ref, k_ref, v_ref, slope_ref, o_ref, bias_ref, x4_ref, og_ref, lse_ref, *, s):
    qscale = ATTN_DH ** -0.5 * LOG2E
    slope = slope_ref[0, 0:1, 0:1] * LOG2E
    qi = lax.broadcasted_iota(jnp.int32, (ATTN_TQ, ATTN_TK), 0)
    kj = lax.broadcasted_iota(jnp.int32, (ATTN_TQ, ATTN_TK), 1)
    for g, dil in enumerate(ATTN_DILATIONS):
        for e in range(3):
            rel = jnp.abs(kj - e * ATTN_HALF - qi)
            pen = slope * (rel * dil).astype(F32)
            bias_ref[g, e] = jnp.where(rel <= ATTN_HALF, -pen, NEG_BIG)

    s4 = s // 4
    srcs = (q_ref, k_ref, v_ref)

    def split4(t, carry):
        c = t // (s4 // ATTN_TK)
        p0 = (t % (s4 // ATTN_TK)) * ATTN_TK
        dst = pl.ds(pl.multiple_of(c * s4 + p0, ATTN_TK), ATTN_TK)
        for a in range(3):
            x = srcs[a][0, pl.ds(c + 4 * p0, ATTN_TK, stride=4), :]
            x4_ref[a, dst, :] = x * qscale if a == 0 else x
        return carry

    lax.fori_loop(0, 4 * (s4 // ATTN_TK), split4, 0)

    ones = jnp.ones((ATTN_TK, ATTN_DH), BF16)

    for g, dil in enumerate(ATTN_DILATIONS):
        sp = s // dil
        nqb = sp // ATTN_TQ

        def q_block(t, carry, g=g, dil=dil, sp=sp, nqb=nqb):
            r = t // nqb
            p0 = (t % nqb) * ATTN_TQ
            kstart = jnp.clip(p0 - ATTN_HALF, 0, sp - ATTN_TK)
            e = (p0 - kstart) // ATTN_HALF
            if dil == 1:
                qrows = pl.ds(pl.multiple_of(p0, ATTN_TQ), ATTN_TQ)
                krows = pl.ds(pl.multiple_of(kstart, ATTN_HALF), ATTN_TK)
                qf = q_ref[0, qrows, :] * qscale
                kf, vf = k_ref[0, krows, :], v_ref[0, krows, :]
                orows = qrows
            elif dil == 4:
                qrows = pl.ds(pl.multiple_of(r * s4 + p0, ATTN_TQ), ATTN_TQ)
                krows = pl.ds(pl.multiple_of(r * s4 + kstart, ATTN_HALF), ATTN_TK)
                qf, kf, vf = x4_ref[0, qrows, :], x4_ref[1, krows, :], x4_ref[2, krows, :]
                orows = pl.ds(r + 4 * p0, ATTN_TQ, stride=4)
            else:
                base = (r % 4) * s4 + r // 4
                qrows = pl.ds(base + 4 * p0, ATTN_TQ, stride=4)
                krows = pl.ds(base + 4 * kstart, ATTN_TK, stride=4)
                qf, kf, vf = x4_ref[0, qrows, :], x4_ref[1, krows, :], x4_ref[2, krows, :]
                orows = pl.ds(r + dil * p0, ATTN_TQ, stride=dil)
            qb, kb, vb = qf.astype(BF16), kf.astype(BF16), vf.astype(BF16)
            sc = lax.dot_general(qb, kb, (((1,), (1,)), ((), ())),
                                 preferred_element_type=F32) + bias_ref[g, e]
            m = jnp.max(sc, axis=-1, keepdims=True)
            p = jnp.exp2(sc - m).astype(BF16)
            pv = jnp.dot(p, jnp.concatenate([vb, ones], axis=1), preferred_element_type=F32)
            den = pv[:, ATTN_DH:]
            og_ref[g, orows, :] = pv[:, :ATTN_DH] / den
            lse_ref[g, orows, :] = m + jnp.log2(den)
            return carry

        lax.fori_loop(0, dil * nqb, q_block, 0, unroll=ATTN_UNROLL)

    def merge(t, carry):
        r0 = pl.multiple_of(t * ATTN_MERGE_TILE, ATTN_MERGE_TILE)
        sl = pl.ds(r0, ATTN_MERGE_TILE)
        l0, l1, l2 = lse_ref[0, sl, :], lse_ref[1, sl, :], lse_ref[2, sl, :]
        mx = jnp.maximum(jnp.maximum(l0, l1), l2)
        e0, e1, e2 = jnp.exp2(l0 - mx), jnp.exp2(l1 - mx), jnp.exp2(l2 - mx)
        inv = 1.0 / (e0 + e1 + e2)
        o = (e0 * inv) * og_ref[0, sl, :] + (e1 * inv) * og_ref[1, sl, :] + (e2 * inv) * og_ref[2, sl, :]
        o_ref[0, sl, :] = o.astype(o_ref.dtype)
        return carry

    lax.fori_loop(0, s // ATTN_MERGE_TILE, merge, 0)


def dilated_attention(qkv, slopes):
    b, s, _ = qkv.shape
    h, dh = ATTN_HEADS, ATTN_DH
    assert s % (max(ATTN_DILATIONS) * ATTN_TK) == 0 and s % ATTN_MERGE_TILE == 0
    return pl.pallas_call(
        functools.partial(_attn_body, s=s),
        grid=(b, h),
        in_specs=[
            pl.BlockSpec((1, s, dh), lambda i, j: (i, 0, j)),
            pl.BlockSpec((1, s, dh), lambda i, j: (i, 0, h + j)),
            pl.BlockSpec((1, s, dh), lambda i, j: (i, 0, 2 * h + j)),
            pl.BlockSpec((1, SUBLANES, LANES), lambda i, j: (j, 0, 0)),
        ],
        out_specs=pl.BlockSpec((1, s, dh), lambda i, j: (i, 0, j)),
        out_shape=jax.ShapeDtypeStruct((b, s, h * dh), BF16),
        scratch_shapes=[
            pltpu.VMEM((len(ATTN_DILATIONS), 3, ATTN_TQ, ATTN_TK), F32),
            pltpu.VMEM((3, s, dh), F32),
            pltpu.VMEM((len(ATTN_DILATIONS), s, dh), F32),
            pltpu.VMEM((len(ATTN_DILATIONS), s, dh), F32),
        ],
        compiler_params=_cparams("parallel", "parallel"),
        name="dilated_attn",
    )(qkv, qkv, qkv, slopes)


MIXER_OUT_TM = 512


def _ffn(x, h, w1, w3, w2):
    u = swiglu_up(h, w1.astype(BF16), w3.astype(BF16))
    return matmul([(u, w2.astype(BF16))], res=x, tn=512, name="ffn_down")


def _even_layer(x, bsz, seq, norm_g, w_in, gate_b, conv_w, conv_b, rg_wa, rg_ba, rg_wx, rg_bx,
                rg_lam, head_g, w_out, next_norm_g):
    heads, dk, dv = MLSTM_HEADS, MLSTM_DK, MLSTM_DV
    n_qkvo = 2 * heads * dk + 2 * heads * dv
    n_gate = 4 * heads
    width = RNN_BLOCKS * RNN_BLOCK
    w_main = jnp.concatenate([w_in[:, :n_qkvo], w_in[:, n_qkvo + n_gate:]], axis=1).astype(BF16)
    w_gate = jnp.pad(w_in[:, n_qkvo:n_qkvo + n_gate], ((0, 0), (0, LANES - n_gate))).astype(BF16)
    z, zg = norm_matmul(x, norm_g, w_main, w_side=w_gate, name="in_proj")
    z = z.reshape(bsz, seq, -1)

    chunk = MLSTM_CHUNK
    nc = seq // chunk
    gates_rows = jnp.transpose(zg[:, :n_gate].reshape(bsz, nc, chunk, 4, heads), (0, 4, 3, 1, 2))
    gate_b_rows = jnp.broadcast_to(
        jnp.transpose(gate_b.astype(F32).reshape(4, heads))[:, :, None, None], (heads, 4, 1, chunk))
    y_a = mlstm_mixer(z, gates_rows, gate_b_rows, head_g.astype(F32).reshape(heads, 1, dv), chunk)

    y_b = rglru_mixer(z, n_qkvo // RNN_BLOCK, (n_qkvo + width) // RNN_BLOCK,
                      conv_w.astype(F32), conv_b.astype(F32).reshape(1, width),
                      rg_wa.astype(BF16), rg_wx.astype(BF16),
                      rg_ba.astype(F32), rg_bx.astype(F32), rg_lam.astype(F32))

    w_out_b = w_out.astype(BF16)
    m = bsz * seq
    return matmul([(y_a.reshape(m, -1), w_out_b[:heads * dv]),
                   (y_b.reshape(m, -1), w_out_b[heads * dv:])], res=x, norm_g=next_norm_g,
                  tm=MIXER_OUT_TM, tn=w_out.shape[1], name="out_proj")


def _odd_layer(x, bsz, seq, norm_g, w_qkv, w_o, next_norm_g):
    qkv = norm_matmul(x, norm_g, w_qkv.astype(BF16), name="qkv_proj").reshape(bsz, seq, -1)
    slopes = jnp.exp2(-ALIBI_MAX_BIAS * jnp.arange(1, ATTN_HEADS + 1, dtype=F32) / ATTN_HEADS)
    slopes = jnp.broadcast_to(slopes[:, None, None], (ATTN_HEADS, SUBLANES, LANES))
    o = dilated_attention(qkv, slopes)
    return matmul([(o.reshape(bsz * seq, -1), w_o.astype(BF16))], res=x, norm_g=next_norm_g,
                  tm=MIXER_OUT_TM, tn=w_o.shape[1], name="attn_out_proj")


def kernel(x, e_norm, e_w_in, e_gate_b, e_conv_w, e_conv_b, e_rg_wa, e_rg_ba, e_rg_wx, e_rg_bx,
           e_rg_lam, e_head_g, e_w_out, o_norm, o_w_qkv, o_w_o, f_norm, f_w1, f_w3, f_w2,
           final_norm):
    bsz, seq, d = x.shape
    depth = f_norm.shape[0]
    xs = x.reshape(bsz * seq, d).astype(F32)
    for l in range(depth):
        if l % 2 == 0:
            e = l // 2
            xs, hs = _even_layer(xs, bsz, seq, e_norm[e], e_w_in[e], e_gate_b[e], e_conv_w[e],
                                 e_conv_b[e], e_rg_wa[e], e_rg_ba[e], e_rg_wx[e], e_rg_bx[e],
                                 e_rg_lam[e], e_head_g[e], e_w_out[e], f_norm[l])
        else:
            o = l // 2
            xs, hs = _odd_layer(xs, bsz, seq, o_norm[o], o_w_qkv[o], o_w_o[o], f_norm[l])
        xs = _ffn(xs, hs, f_w1[l], f_w3[l], f_w2[l])
    return rmsnorm(xs, final_norm, x.dtype).reshape(bsz, seq, d)
```

```python
import functools
import math

import jax
import jax.numpy as jnp
from jax import lax
from jax.experimental import pallas as pl
from jax.experimental.pallas import tpu as pltpu

F32 = jnp.float32
BF16 = jnp.bfloat16

RMS_EPS = 1e-6
NEG_BIG = -1e30

MLSTM_HEADS = 4
MLSTM_DK = 128
MLSTM_DV = 256
MLSTM_CHUNK = 256
RNN_BLOCKS = 8
RNN_BLOCK = 128
CONV_WIDTH = 4
CONV_LEFT = 2
RGLRU_C = 8.0
ATTN_HEADS = 16
ATTN_DH = 128
ATTN_HALF = 64
ATTN_DILATIONS = (1, 4, 16)
ALIBI_MAX_BIAS = 8.0

LANES = 128
SUBLANES = 8
VMEM_LIMIT_BYTES = 56 * 1024 * 1024


def _cparams(*semantics):
    return pltpu.CompilerParams(dimension_semantics=semantics,
                                vmem_limit_bytes=VMEM_LIMIT_BYTES)


def _rmsnorm_body(x_ref, g_ref, o_ref):
    x = x_ref[...]
    ms = jnp.mean(x * x, axis=-1, keepdims=True)
    o_ref[...] = (x * lax.rsqrt(ms + RMS_EPS) * g_ref[...]).astype(o_ref.dtype)


def rmsnorm(x, g, out_dtype, tm=512):
    m, d = x.shape
    return pl.pallas_call(
        _rmsnorm_body,
        grid=(m // tm,),
        in_specs=[pl.BlockSpec((tm, d), lambda i: (i, 0)),
                  pl.BlockSpec((1, d), lambda i: (0, 0))],
        out_specs=pl.BlockSpec((tm, d), lambda i: (i, 0)),
        out_shape=jax.ShapeDtypeStruct((m, d), out_dtype),
        compiler_params=_cparams("parallel"),
        name="rmsnorm",
    )(x, g.reshape(1, d).astype(F32))


NORM_ROWS = 128


def _norm_rows_into(x_ref, g_ref, h_ref):
    g = g_ref[...]

    def rows(t, carry):
        sl = pl.ds(pl.multiple_of(t * NORM_ROWS, NORM_ROWS), NORM_ROWS)
        x = x_ref[sl, :]
        ms = jnp.mean(x * x, axis=-1, keepdims=True)
        h_ref[sl, :] = (x * lax.rsqrt(ms + RMS_EPS) * g).astype(h_ref.dtype)
        return carry

    lax.fori_loop(0, x_ref.shape[0] // NORM_ROWS, rows, 0)


def _matmul_body(*refs, n_pairs, has_res, has_norm):
    pairs = [(refs[2 * p], refs[2 * p + 1]) for p in range(n_pairs)]
    rest = list(refs[2 * n_pairs:])
    r_ref = rest.pop(0) if has_res else None
    g_ref = rest.pop(0) if has_norm else None
    o_ref = rest.pop(0)

    acc = None
    for a_ref, w_ref in pairs:
        d = jnp.dot(a_ref[...], w_ref[...], preferred_element_type=F32)
        acc = d if acc is None else acc + d
    if has_res:
        acc = acc + r_ref[...]
    o_ref[...] = acc.astype(o_ref.dtype)
    if has_norm:
        _norm_rows_into(o_ref, g_ref, rest.pop(0))


def matmul(pairs, res=None, norm_g=None, out_dtype=F32, tm=1024, tn=1024, name="matmul"):
    m = pairs[0][0].shape[0]
    n = pairs[0][1].shape[1]
    tn = min(tn, n)
    assert m % tm == 0 and n % tn == 0
    assert norm_g is None or (tn == n and out_dtype == F32 and tm % NORM_ROWS == 0)
    in_specs, args = [], []
    for a, w in pairs:
        kdim = a.shape[1]
        in_specs += [pl.BlockSpec((tm, kdim), lambda i, j: (i, 0)),
                     pl.BlockSpec((kdim, tn), lambda i, j: (0, j))]
        args += [a, w]
    if res is not None:
        in_specs.append(pl.BlockSpec((tm, tn), lambda i, j: (i, j)))
        args.append(res)
    out_specs = [pl.BlockSpec((tm, tn), lambda i, j: (i, j))]
    out_shape = [jax.ShapeDtypeStruct((m, n), out_dtype)]
    if norm_g is not None:
        in_specs.append(pl.BlockSpec((1, n), lambda i, j: (0, 0)))
        args.append(norm_g.reshape(1, n).astype(F32))
        out_specs.append(pl.BlockSpec((tm, n), lambda i, j: (i, 0)))
        out_shape.append(jax.ShapeDtypeStruct((m, n), BF16))
    outs = pl.pallas_call(
        functools.partial(_matmul_body, n_pairs=len(pairs), has_res=res is not None,
                          has_norm=norm_g is not None),
        grid=(m // tm, n // tn),
        in_specs=in_specs,
        out_specs=out_specs,
        out_shape=out_shape,
        compiler_params=_cparams("parallel", "parallel"),
        name=name,
    )(*args)
    return outs if norm_g is not None else outs[0]


def _swiglu_up_body(a_ref, w1_ref, w3_ref, o_ref):
    a = a_ref[...]
    u = jnp.dot(a, w1_ref[...], preferred_element_type=F32)
    v = jnp.dot(a, w3_ref[...], preferred_element_type=F32)
    o_ref[...] = (u * jax.nn.sigmoid(u) * v).astype(o_ref.dtype)


def swiglu_up(a, w1, w3, tm=2048, tn=512):
    m, kdim = a.shape
    n = w1.shape[1]
    assert m % tm == 0 and n % tn == 0
    return pl.pallas_call(
        _swiglu_up_body,
        grid=(m // tm, n // tn),
        in_specs=[pl.BlockSpec((tm, kdim), lambda i, j: (i, 0)),
                  pl.BlockSpec((kdim, tn), lambda i, j: (0, j)),
                  pl.BlockSpec((kdim, tn), lambda i, j: (0, j))],
        out_specs=pl.BlockSpec((tm, tn), lambda i, j: (i, j)),
        out_shape=jax.ShapeDtypeStruct((m, n), BF16),
        compiler_params=_cparams("parallel", "parallel"),
        name="swiglu_up",
    )(a, w1, w3)


def _norm_matmul_body(*refs, has_side):
    if has_side:
        x_ref, g_ref, w_ref, ws_ref, o_ref, os_ref, h_ref = refs
    else:
        x_ref, g_ref, w_ref, o_ref, h_ref = refs

    @pl.when(pl.program_id(1) == 0)
    def _():
        _norm_rows_into(x_ref, g_ref, h_ref)
        if has_side:
            os_ref[...] = jnp.dot(h_ref[...], ws_ref[...], preferred_element_type=F32)

    o_ref[...] = jnp.dot(h_ref[...], w_ref[...], preferred_element_type=F32).astype(o_ref.dtype)


def norm_matmul(x, g, w, w_side=None, out_dtype=F32, tm=1024, tn=1024, name="norm_matmul"):
    m, d = x.shape
    n = w.shape[1]
    assert m % tm == 0 and n % tn == 0 and tm % NORM_ROWS == 0
    in_specs = [pl.BlockSpec((tm, d), lambda i, j: (i, 0)),
                pl.BlockSpec((1, d), lambda i, j: (0, 0)),
                pl.BlockSpec((d, tn), lambda i, j: (0, j))]
    args = [x, g.reshape(1, d).astype(F32), w]
    out_specs = [pl.BlockSpec((tm, tn), lambda i, j: (i, j))]
    out_shape = [jax.ShapeDtypeStruct((m, n), out_dtype)]
    if w_side is not None:
        ns = w_side.shape[1]
        in_specs.append(pl.BlockSpec((d, ns), lambda i, j: (0, 0)))
        args.append(w_side)
        out_specs.append(pl.BlockSpec((tm, ns), lambda i, j: (i, 0)))
        out_shape.append(jax.ShapeDtypeStruct((m, ns), F32))
    outs = pl.pallas_call(
        functools.partial(_norm_matmul_body, has_side=w_side is not None),
        grid=(m // tm, n // tn),
        in_specs=in_specs,
        out_specs=out_specs,
        out_shape=out_shape,
        scratch_shapes=[pltpu.VMEM((tm, d), BF16)],
        compiler_params=_cparams("parallel", "arbitrary"),
        name=name,
    )(*args)
    return outs if w_side is not None else outs[0]


def _log_sigmoid(x):
    return jnp.minimum(x, 0.0) - jnp.log1p(jnp.exp(-jnp.abs(x)))


def _lane_scan(x, combine, identity, reverse):
    width = x.shape[-1]
    lane = lax.broadcasted_iota(jnp.int32, x.shape, 1)
    sh = 1
    while sh < width:
        if reverse:
            shifted = jnp.where(lane < width - sh, pltpu.roll(x, width - sh, axis=1), identity)
        else:
            shifted = jnp.where(lane >= sh, pltpu.roll(x, sh, axis=1), identity)
        x = combine(x, shifted)
        sh *= 2
    return x


def _mlstm_body(q_ref, k_ref, v_ref, og_ref, g_ref, gb_ref, hg_ref, y_ref,
                rows_ref, cols_ref, hf_ref, hb_ref, ct_ref, *, nc, chunk):
    dk, dv = MLSTM_DK, MLSTM_DV
    scale = dk ** -0.5

    g = g_ref[0, 0] + gb_ref[0]
    rows = []
    for d in range(2):
        b = _lane_scan(_log_sigmoid(g[2 * d + 1]), jnp.add, 0.0, reverse=d == 1)
        gg = g[2 * d] - b
        gmax = _lane_scan(gg, jnp.maximum, -jnp.inf, reverse=d == 1)
        rows += [gg, b, gmax]
        for j, r in enumerate((gg, b, gmax)):
            rows_ref[3 * d + j] = r
    rowmat = jnp.concatenate(rows + [jnp.zeros((LANES - 6 * nc, chunk), F32)], axis=0)
    colmat = rowmat.T
    for c in range(nc):
        cols_ref[c] = colmat if c == 0 else pltpu.roll(colmat, LANES - c, axis=1)

    ct_ref[...] = jnp.zeros_like(ct_ref)

    row_id = lax.broadcasted_iota(jnp.int32, (chunk, chunk), 0)
    col_id = lax.broadcasted_iota(jnp.int32, (chunk, chunk), 1)
    causal = (row_id >= col_id, row_id <= col_id)
    ones = jnp.ones((chunk, LANES), BF16)

    def lanes2(x):
        return jnp.concatenate([x] * (dv // LANES), axis=1)

    def chunk_step(c, m, d, h_ref):
        r0 = pl.multiple_of(c * chunk, chunk)
        qb = (q_ref[0, pl.ds(r0, chunk), :] * scale).astype(BF16)
        kb = k_ref[0, pl.ds(r0, chunk), :].astype(BF16)
        vf = v_ref[0, pl.ds(r0, chunk), :]
        cols = cols_ref[c]

        def column(j):
            lane = (3 * d + j) * nc
            return jnp.broadcast_to(cols[:, lane:lane + 1], (chunk, LANES))

        gg_col, b_col, gmax_col = column(0), column(1), column(2)
        gg_row = rows_ref[3 * d, pl.ds(c, 1), :]
        end = chunk - 1 if d == 0 else 0
        total = rows_ref[3 * d + 1, pl.ds(c, 1), :][:, end:end + 1]
        gmax_end = rows_ref[3 * d + 2, pl.ds(c, 1), :][:, end:end + 1]

        m_row = jnp.maximum(gmax_col, m)
        w_intra = jnp.exp(jnp.where(causal[d], gg_row - lanes2(m_row), NEG_BIG))
        w_inter = jnp.exp(m - m_row)
        s = lax.dot_general(qb, kb, (((1,), (1,)), ((), ())),
                            preferred_element_type=F32) * w_intra
        ct = ct_ref[d]
        intra = jnp.dot(s.astype(BF16), jnp.concatenate([vf.astype(BF16), ones], axis=1),
                        preferred_element_type=F32)
        inter = jnp.dot(qb, ct.astype(BF16), preferred_element_type=F32)
        num = intra[:, :dv] + lanes2(w_inter) * inter[:, :dv]
        den = intra[:, dv:] + w_inter * inter[:, dv:]
        inv = 1.0 / jnp.maximum(jnp.abs(den), jnp.exp(-(b_col + m_row)))
        h_ref[pl.ds(r0, chunk), :] = num * lanes2(inv)

        m_new = jnp.maximum(total + m, total + gmax_end)
        w_src = jnp.exp(total + gg_col - m_new)
        decay = jnp.exp(total + m - m_new)
        wv = jnp.concatenate([lanes2(w_src) * vf, w_src], axis=1).astype(BF16)
        ct_ref[d] = decay * ct + lax.dot_general(
            kb, wv, (((0,), (0,)), ((), ())), preferred_element_type=F32)
        return m_new

    def body(c, carry):
        m_f, m_b = carry
        m_f = chunk_step(c, m_f, 0, hf_ref)
        m_b = chunk_step(nc - 1 - c, m_b, 1, hb_ref)
        return m_f, m_b

    m0 = jnp.full((1, 1), NEG_BIG, F32)
    lax.fori_loop(0, nc, body, (m0, m0))

    hg = hg_ref[0]

    def finish(c, carry):
        r0 = pl.multiple_of(c * chunk, chunk)
        hm = hf_ref[pl.ds(r0, chunk), :] + hb_ref[pl.ds(r0, chunk), :]
        ms = jnp.mean(hm * hm, axis=-1, keepdims=True)
        hn = hm * lax.rsqrt(ms + RMS_EPS) * hg
        gate = jax.nn.sigmoid(og_ref[0, pl.ds(r0, chunk), :])
        y_ref[0, pl.ds(r0, chunk), :] = (hn * gate).astype(y_ref.dtype)
        return carry

    lax.fori_loop(0, nc, finish, 0)


def mlstm_mixer(z, gates_rows, gate_b_rows, head_g, chunk=MLSTM_CHUNK):
    b, s, _ = z.shape
    h, dk, dv = MLSTM_HEADS, MLSTM_DK, MLSTM_DV
    nc = s // chunk
    assert s % chunk == 0 and 6 * nc <= LANES and chunk % LANES == 0
    kq = h * dk // dk
    kv = 2 * h * dk // dv
    ko = kv + h
    return pl.pallas_call(
        functools.partial(_mlstm_body, nc=nc, chunk=chunk),
        grid=(b, h),
        in_specs=[
            pl.BlockSpec((1, s, dk), lambda i, j: (i, 0, j)),
            pl.BlockSpec((1, s, dk), lambda i, j: (i, 0, kq + j)),
            pl.BlockSpec((1, s, dv), lambda i, j: (i, 0, kv + j)),
            pl.BlockSpec((1, s, dv), lambda i, j: (i, 0, ko + j)),
            pl.BlockSpec((1, 1, 4, nc, chunk), lambda i, j: (i, j, 0, 0, 0)),
            pl.BlockSpec((1, 4, 1, chunk), lambda i, j: (j, 0, 0, 0)),
            pl.BlockSpec((1, 1, dv), lambda i, j: (j, 0, 0)),
        ],
        out_specs=pl.BlockSpec((1, s, dv), lambda i, j: (i, 0, j)),
        out_shape=jax.ShapeDtypeStruct((b, s, h * dv), BF16),
        scratch_shapes=[
            pltpu.VMEM((6, nc, chunk), F32),
            pltpu.VMEM((nc, chunk, LANES), F32),
            pltpu.VMEM((s, dv), F32),
            pltpu.VMEM((s, dv), F32),
            pltpu.VMEM((2, dk, dv + LANES), F32),
        ],
        compiler_params=_cparams("parallel", "parallel"),
        name="mlstm",
    )(z, z, z, z, gates_rows, gate_b_rows, head_g)


RG_TILE = 512
RG_PAD = SUBLANES
RG_SCAN_UNROLL = 8


def _softplus(x):
    return jnp.maximum(x, 0.0) + jnp.log1p(jnp.exp(-jnp.abs(x)))


def _sigmoid(x):
    return 0.5 * jnp.tanh(0.5 * x) + 0.5


def _gelu_tanh(x):
    c = math.sqrt(2.0 / math.pi)
    return x * (0.5 * (1.0 + jnp.tanh(c * (x + 0.044715 * (x * x * x)))))


def _rglru_body(x_ref, gr_ref, cw_ref, cb_ref, wa_ref, wx_ref, ba_ref, bx_ref, lam_ref, y_ref,
                xpad_ref, a_ref, u_ref, as_ref, us_ref, h_ref, *, s):
    nt = s // RG_TILE
    zeros = jnp.zeros((RG_PAD, LANES), F32)
    xpad_ref[pl.ds(0, RG_PAD), :] = zeros
    xpad_ref[pl.ds(RG_PAD + s, RG_PAD), :] = zeros

    def copy_in(t, carry):
        r0 = pl.multiple_of(t * RG_TILE, RG_TILE)
        xpad_ref[pl.ds(RG_PAD + r0, RG_TILE), :] = x_ref[0, pl.ds(r0, RG_TILE), :]
        return carry

    lax.fori_loop(0, nt, copy_in, 0)

    cw = cw_ref[...]
    cb = cb_ref[...]
    sp = [RGLRU_C * _softplus(-lam_ref[d:d + 1, :]) for d in range(2)]

    def gates(t, carry):
        r0 = pl.multiple_of(t * RG_TILE, RG_TILE)
        xc = None
        for j in range(CONV_WIDTH):
            tap = xpad_ref[pl.ds(r0 + RG_PAD - CONV_LEFT + j, RG_TILE), :] * cw[j:j + 1, :]
            xc = tap if xc is None else xc + tap
        xc = xc + cb
        xcb = xc.astype(BF16)
        for d in range(2):
            r = _sigmoid(jnp.dot(xcb, wa_ref[d, 0], preferred_element_type=F32)
                         + ba_ref[d:d + 1, :])
            i = _sigmoid(jnp.dot(xcb, wx_ref[d, 0], preferred_element_type=F32)
                         + bx_ref[d:d + 1, :])
            neg_log_a = sp[d] * r
            a = jnp.exp(-neg_log_a)
            one_minus_a2 = jnp.tanh(neg_log_a) * (1.0 + a * a)
            root = jnp.where(one_minus_a2 > 0.0, one_minus_a2 * lax.rsqrt(one_minus_a2), 0.0)
            a_ref[d, pl.ds(r0, RG_TILE), :] = a
            u_ref[d, pl.ds(r0, RG_TILE), :] = root * (i * xc)
        return carry

    lax.fori_loop(0, nt, gates, 0)

    def block_scan(t, carry):
        r0 = pl.multiple_of(t * RG_TILE, RG_TILE)
        for d in range(2):
            a_prev = u_prev = None
            for r in (range(SUBLANES) if d == 0 else reversed(range(SUBLANES))):
                rows = pl.ds(r0 + r, RG_TILE // SUBLANES, stride=SUBLANES)
                a, u = a_ref[d, rows, :], u_ref[d, rows, :]
                if a_prev is not None:
                    u = a * u_prev + u
                    a = a * a_prev
                as_ref[d, rows, :] = a
                us_ref[d, rows, :] = u
                a_prev, u_prev = a, u
        return carry

    lax.fori_loop(0, nt, block_scan, 0)

    nblk = s // SUBLANES

    def scan(j, carry):
        h_f, h_b = carry
        for d, last, h in ((0, SUBLANES - 1, h_f), (1, 0, h_b)):
            blk = j if d == 0 else nblk - 1 - j
            rows = pl.ds(pl.multiple_of(blk * SUBLANES, SUBLANES), SUBLANES)
            a, u = as_ref[d, rows, :], us_ref[d, rows, :]
            h_ref[d, rows, :] = a * h + u
            a_end = jnp.broadcast_to(a[last:last + 1, :], (SUBLANES, LANES))
            u_end = jnp.broadcast_to(u[last:last + 1, :], (SUBLANES, LANES))
            if d == 0:
                h_f = a_end * h + u_end
            else:
                h_b = a_end * h + u_end
        return h_f, h_b

    h0 = jnp.zeros((SUBLANES, LANES), F32)
    lax.fori_loop(0, nblk, scan, (h0, h0), unroll=RG_SCAN_UNROLL)

    def finish(t, carry):
        r0 = pl.multiple_of(t * RG_TILE, RG_TILE)
        hr = h_ref[0, pl.ds(r0, RG_TILE), :] + h_ref[1, pl.ds(r0, RG_TILE), :]
        y_ref[0, pl.ds(r0, RG_TILE), :] = (
            hr * _gelu_tanh(gr_ref[0, pl.ds(r0, RG_TILE), :])).astype(y_ref.dtype)
        return carry

    lax.fori_loop(0, nt, finish, 0)


def rglru_mixer(z, xr_block0, gr_block0, conv_w, conv_b, wa, wx, ba, bx, lam):
    b, s, _ = z.shape
    width = RNN_BLOCKS * RNN_BLOCK
    assert s % RG_TILE == 0
    return pl.pallas_call(
        functools.partial(_rglru_body, s=s),
        grid=(b, RNN_BLOCKS),
        in_specs=[
            pl.BlockSpec((1, s, RNN_BLOCK), lambda i, j: (i, 0, xr_block0 + j)),
            pl.BlockSpec((1, s, RNN_BLOCK), lambda i, j: (i, 0, gr_block0 + j)),
            pl.BlockSpec((CONV_WIDTH, RNN_BLOCK), lambda i, j: (0, j)),
            pl.BlockSpec((1, RNN_BLOCK), lambda i, j: (0, j)),
            pl.BlockSpec((2, 1, RNN_BLOCK, RNN_BLOCK), lambda i, j: (0, j, 0, 0)),
            pl.BlockSpec((2, 1, RNN_BLOCK, RNN_BLOCK), lambda i, j: (0, j, 0, 0)),
            pl.BlockSpec((2, RNN_BLOCK), lambda i, j: (0, j)),
            pl.BlockSpec((2, RNN_BLOCK), lambda i, j: (0, j)),
            pl.BlockSpec((2, RNN_BLOCK), lambda i, j: (0, j)),
        ],
        out_specs=pl.BlockSpec((1, s, RNN_BLOCK), lambda i, j: (i, 0, j)),
        out_shape=jax.ShapeDtypeStruct((b, s, width), BF16),
        scratch_shapes=[
            pltpu.VMEM((s + 2 * RG_PAD, RNN_BLOCK), F32),
            pltpu.VMEM((2, s, RNN_BLOCK), F32),
            pltpu.VMEM((2, s, RNN_BLOCK), F32),
            pltpu.VMEM((2, s, RNN_BLOCK), F32),
            pltpu.VMEM((2, s, RNN_BLOCK), F32),
            pltpu.VMEM((2, s, RNN_BLOCK), F32),
        ],
        compiler_params=_cparams("parallel", "parallel"),
        name="rglru",
    )(z, z, conv_w, conv_b, wa, wx, ba, bx, lam)


ATTN_TQ = 128
ATTN_TK = ATTN_TQ + 2 * ATTN_HALF
ATTN_MERGE_TILE = 256
ATTN_UNROLL = 16
LOG2E = math.log2(math.e)


def _attn_body(q_ref, k_ref, v_ref, slope_ref, o_ref, bias_ref, x4_ref, og_ref, lse_ref, *, s):
    qscale = ATTN_DH ** -0.5 * LOG2E
    slope = slope_ref[0, 0:1, 0:1] * LOG2E
    qi = lax.broadcasted_iota(jnp.int32, (ATTN_TQ, ATTN_TK), 0)
    kj = lax.broadcasted_iota(jnp.int32, (ATTN_TQ, ATTN_TK), 1)
    for g, dil in enumerate(ATTN_DILATIONS):
        for e in range(3):
            rel = jnp.abs(kj - e * ATTN_HALF - qi)
            pen = slope * (rel * dil).astype(F32)
            bias_ref[g, e] = jnp.where(rel <= ATTN_HALF, -pen, NEG_BIG)

    s4 = s // 4
    srcs = (q_ref, k_ref, v_ref)

    def split4(t, carry):
        c = t // (s4 // ATTN_TK)
        p0 = (t % (s4 // ATTN_TK)) * ATTN_TK
        dst = pl.ds(pl.multiple_of(c * s4 + p0, ATTN_TK), ATTN_TK)
        for a in range(3):
            x = srcs[a][0, pl.ds(c + 4 * p0, ATTN_TK, stride=4), :]
            x4_ref[a, dst, :] = x * qscale if a == 0 else x
        return carry

    lax.fori_loop(0, 4 * (s4 // ATTN_TK), split4, 0)

    ones = jnp.ones((ATTN_TK, ATTN_DH), BF16)

    for g, dil in enumerate(ATTN_DILATIONS):
        sp = s // dil
        nqb = sp // ATTN_TQ

        def q_block(t, carry, g=g, dil=dil, sp=sp, nqb=nqb):
            r = t // nqb
            p0 = (t % nqb) * ATTN_TQ
            kstart = jnp.clip(p0 - ATTN_HALF, 0, sp - ATTN_TK)
            e = (p0 - kstart) // ATTN_HALF
            if dil == 1:
                qrows = pl.ds(pl.multiple_of(p0, ATTN_TQ), ATTN_TQ)
                krows = pl.ds(pl.multiple_of(kstart, ATTN_HALF), ATTN_TK)
                qf = q_ref[0, qrows, :] * qscale
                kf, vf = k_ref[0, krows, :], v_ref[0, krows, :]
                orows = qrows
            elif dil == 4:
                qrows = pl.ds(pl.multiple_of(r * s4 + p0, ATTN_TQ), ATTN_TQ)
                krows = pl.ds(pl.multiple_of(r * s4 + kstart, ATTN_HALF), ATTN_TK)
                qf, kf, vf = x4_ref[0, qrows, :], x4_ref[1, krows, :], x4_ref[2, krows, :]
                orows = pl.ds(r + 4 * p0, ATTN_TQ, stride=4)
            else:
                base = (r % 4) * s4 + r // 4
                qrows = pl.ds(base + 4 * p0, ATTN_TQ, stride=4)
                krows = pl.ds(base + 4 * kstart, ATTN_TK, stride=4)
                qf, kf, vf = x4_ref[0, qrows, :], x4_ref[1, krows, :], x4_ref[2, krows, :]
                orows = pl.ds(r + dil * p0, ATTN_TQ, stride=dil)
            qb, kb, vb = qf.astype(BF16), kf.astype(BF16), vf.astype(BF16)
            sc = lax.dot_general(qb, kb, (((1,), (1,)), ((), ())),
                                 preferred_element_type=F32) + bias_ref[g, e]
            m = jnp.max(sc, axis=-1, keepdims=True)
            p = jnp.exp2(sc - m).astype(BF16)
            pv = jnp.dot(p, jnp.concatenate([vb, ones], axis=1), preferred_element_type=F32)
            den = pv[:, ATTN_DH:]
            og_ref[g, orows, :] = pv[:, :ATTN_DH] / den
            lse_ref[g, orows, :] = m + jnp.log2(den)
            return carry

        lax.fori_loop(0, dil * nqb, q_block, 0, unroll=ATTN_UNROLL)

    def merge(t, carry):
        r0 = pl.multiple_of(t * ATTN_MERGE_TILE, ATTN_MERGE_TILE)
        sl = pl.ds(r0, ATTN_MERGE_TILE)
        l0, l1, l2 = lse_ref[0, sl, :], lse_ref[1, sl, :], lse_ref[2, sl, :]
        mx = jnp.maximum(jnp.maximum(l0, l1), l2)
        e0, e1, e2 = jnp.exp2(l0 - mx), jnp.exp2(l1 - mx), jnp.exp2(l2 - mx)
        inv = 1.0 / (e0 + e1 + e2)
        o = (e0 * inv) * og_ref[0, sl, :] + (e1 * inv) * og_ref[1, sl, :] + (e2 * inv) * og_ref[2, sl, :]
        o_ref[0, sl, :] = o.astype(o_ref.dtype)
        return carry

    lax.fori_loop(0, s // ATTN_MERGE_TILE, merge, 0)


def dilated_attention(qkv, slopes):
    b, s, _ = qkv.shape
    h, dh = ATTN_HEADS, ATTN_DH
    assert s % (max(ATTN_DILATIONS) * ATTN_TK) == 0 and s % ATTN_MERGE_TILE == 0
    return pl.pallas_call(
        functools.partial(_attn_body, s=s),
        grid=(b, h),
        in_specs=[
            pl.BlockSpec((1, s, dh), lambda i, j: (i, 0, j)),
            pl.BlockSpec((1, s, dh), lambda i, j: (i, 0, h + j)),
            pl.BlockSpec((1, s, dh), lambda i, j: (i, 0, 2 * h + j)),
            pl.BlockSpec((1, SUBLANES, LANES), lambda i, j: (j, 0, 0)),
        ],
        out_specs=pl.BlockSpec((1, s, dh), lambda i, j: (i, 0, j)),
        out_shape=jax.ShapeDtypeStruct((b, s, h * dh), BF16),
        scratch_shapes=[
            pltpu.VMEM((len(ATTN_DILATIONS), 3, ATTN_TQ, ATTN_TK), F32),
            pltpu.VMEM((3, s, dh), F32),
            pltpu.VMEM((len(ATTN_DILATIONS), s, dh), F32),
            pltpu.VMEM((len(ATTN_DILATIONS), s, dh), F32),
        ],
        compiler_params=_cparams("parallel", "parallel"),
        name="dilated_attn",
    )(qkv, qkv, qkv, slopes)


MIXER_OUT_TM = 512


def _ffn(x, h, w1, w3, w2):
    u = swiglu_up(h, w1.astype(BF16), w3.astype(BF16))
    return matmul([(u, w2.astype(BF16))], res=x, tn=512, name="ffn_down")


def _even_layer(x, bsz, seq, norm_g, w_in, gate_b, conv_w, conv_b, rg_wa, rg_ba, rg_wx, rg_bx,
                rg_lam, head_g, w_out, next_norm_g):
    heads, dk, dv = MLSTM_HEADS, MLSTM_DK, MLSTM_DV
    n_qkvo = 2 * heads * dk + 2 * heads * dv
    n_gate = 4 * heads
    width = RNN_BLOCKS * RNN_BLOCK
    w_main = jnp.concatenate([w_in[:, :n_qkvo], w_in[:, n_qkvo + n_gate:]], axis=1).astype(BF16)
    w_gate = jnp.pad(w_in[:, n_qkvo:n_qkvo + n_gate], ((0, 0), (0, LANES - n_gate))).astype(BF16)
    z, zg = norm_matmul(x, norm_g, w_main, w_side=w_gate, name="in_proj")
    z = z.reshape(bsz, seq, -1)

    chunk = MLSTM_CHUNK
    nc = seq // chunk
    gates_rows = jnp.transpose(zg[:, :n_gate].reshape(bsz, nc, chunk, 4, heads), (0, 4, 3, 1, 2))
    gate_b_rows = jnp.broadcast_to(
        jnp.transpose(gate_b.astype(F32).reshape(4, heads))[:, :, None, None], (heads, 4, 1, chunk))
    y_a = mlstm_mixer(z, gates_rows, gate_b_rows, head_g.astype(F32).reshape(heads, 1, dv), chunk)

    y_b = rglru_mixer(z, n_qkvo // RNN_BLOCK, (n_qkvo + width) // RNN_BLOCK,
                      conv_w.astype(F32), conv_b.astype(F32).reshape(1, width),
                      rg_wa.astype(BF16), rg_wx.astype(BF16),
                      rg_ba.astype(F32), rg_bx.astype(F32), rg_lam.astype(F32))

    w_out_b = w_out.astype(BF16)
    m = bsz * seq
    return matmul([(y_a.reshape(m, -1), w_out_b[:heads * dv]),
                   (y_b.reshape(m, -1), w_out_b[heads * dv:])], res=x, norm_g=next_norm_g,
                  tm=MIXER_OUT_TM, tn=w_out.shape[1], name="out_proj")


def _odd_layer(x, bsz, seq, norm_g, w_qkv, w_o, next_norm_g):
    qkv = norm_matmul(x, norm_g, w_qkv.astype(BF16), name="qkv_proj").reshape(bsz, seq, -1)
    slopes = jnp.exp2(-ALIBI_MAX_BIAS * jnp.arange(1, ATTN_HEADS + 1, dtype=F32) / ATTN_HEADS)
    slopes = jnp.broadcast_to(slopes[:, None, None], (ATTN_HEADS, SUBLANES, LANES))
    o = dilated_attention(qkv, slopes)
    return matmul([(o.reshape(bsz * seq, -1), w_o.astype(BF16))], res=x, norm_g=next_norm_g,
                  tm=MIXER_OUT_TM, tn=w_o.shape[1], name="attn_out_proj")


def kernel(x, e_norm, e_w_in, e_gate_b, e_conv_w, e_conv_b, e_rg_wa, e_rg_ba, e_rg_wx, e_rg_bx,
           e_rg_lam, e_head_g, e_w_out, o_norm, o_w_qkv, o_w_o, f_norm, f_w1, f_w3, f_w2,
           final_norm):
    bsz, seq, d = x.shape
    depth = f_norm.shape[0]
    xs = x.reshape(bsz * seq, d).astype(F32)
    for l in range(depth):
        if l % 2 == 0:
            e = l // 2
            xs, hs = _even_layer(xs, bsz, seq, e_norm[e], e_w_in[e], e_gate_b[e], e_conv_w[e],
                                 e_conv_b[e], e_rg_wa[e], e_rg_ba[e], e_rg_wx[e], e_rg_bx[e],
                                 e_rg_lam[e], e_head_g[e], e_w_out[e], f_norm[l])
        else:
            o = l // 2
            xs, hs = _odd_layer(xs, bsz, seq, o_norm[o], o_w_qkv[o], o_w_o[o], f_norm[l])
        xs = _ffn(xs, hs, f_w1[l], f_w3[l], f_w2[l])
    return rmsnorm(xs, final_norm, x.dtype).reshape(bsz, seq, d)
```

```python
import functools
import math

import jax
import jax.numpy as jnp
from jax import lax
from jax.experimental import pallas as pl
from jax.experimental.pallas import tpu as pltpu

F32 = jnp.float32
BF16 = jnp.bfloat16

RMS_EPS = 1e-6
NEG_BIG = -1e30

MLSTM_HEADS = 4
MLSTM_DK = 128
MLSTM_DV = 256
MLSTM_CHUNK = 256
RNN_BLOCKS = 8
RNN_BLOCK = 128
CONV_WIDTH = 4
CONV_LEFT = 2
RGLRU_C = 8.0
ATTN_HEADS = 16
ATTN_DH = 128
ATTN_HALF = 64
ATTN_DILATIONS = (1, 4, 16)
ALIBI_MAX_BIAS = 8.0

LANES = 128
SUBLANES = 8
VMEM_LIMIT_BYTES = 56 * 1024 * 1024


def _cparams(*semantics):
    return pltpu.CompilerParams(dimension_semantics=semantics,
                                vmem_limit_bytes=VMEM_LIMIT_BYTES)


def _rmsnorm_body(x_ref, g_ref, o_ref):
    x = x_ref[...]
    ms = jnp.mean(x * x, axis=-1, keepdims=True)
    o_ref[...] = (x * lax.rsqrt(ms + RMS_EPS) * g_ref[...]).astype(o_ref.dtype)


def rmsnorm(x, g, out_dtype, tm=512):
    m, d = x.shape
    return pl.pallas_call(
        _rmsnorm_body,
        grid=(m // tm,),
        in_specs=[pl.BlockSpec((tm, d), lambda i: (i, 0)),
                  pl.BlockSpec((1, d), lambda i: (0, 0))],
        out_specs=pl.BlockSpec((tm, d), lambda i: (i, 0)),
        out_shape=jax.ShapeDtypeStruct((m, d), out_dtype),
        compiler_params=_cparams("parallel"),
        name="rmsnorm",
    )(x, g.reshape(1, d).astype(F32))


NORM_ROWS = 128


def _norm_rows_into(x_ref, g_ref, h_ref):
    g = g_ref[...]

    def rows(t, carry):
        sl = pl.ds(pl.multiple_of(t * NORM_ROWS, NORM_ROWS), NORM_ROWS)
        x = x_ref[sl, :]
        ms = jnp.mean(x * x, axis=-1, keepdims=True)
        h_ref[sl, :] = (x * lax.rsqrt(ms + RMS_EPS) * g).astype(h_ref.dtype)
        return carry

    lax.fori_loop(0, x_ref.shape[0] // NORM_ROWS, rows, 0)


def _matmul_body(*refs, n_pairs, has_res, has_norm):
    pairs = [(refs[2 * p], refs[2 * p + 1]) for p in range(n_pairs)]
    rest = list(refs[2 * n_pairs:])
    r_ref = rest.pop(0) if has_res else None
    g_ref = rest.pop(0) if has_norm else None
    o_ref = rest.pop(0)

    acc = None
    for a_ref, w_ref in pairs:
        d = jnp.dot(a_ref[...], w_ref[...], preferred_element_type=F32)
        acc = d if acc is None else acc + d
    if has_res:
        acc = acc + r_ref[...]
    o_ref[...] = acc.astype(o_ref.dtype)
    if has_norm:
        _norm_rows_into(o_ref, g_ref, rest.pop(0))


def matmul(pairs, res=None, norm_g=None, out_dtype=F32, tm=1024, tn=1024, name="matmul"):
    m = pairs[0][0].shape[0]
    n = pairs[0][1].shape[1]
    tn = min(tn, n)
    assert m % tm == 0 and n % tn == 0
    assert norm_g is None or (tn == n and out_dtype == F32 and tm % NORM_ROWS == 0)
    in_specs, args = [], []
    for a, w in pairs:
        kdim = a.shape[1]
        in_specs += [pl.BlockSpec((tm, kdim), lambda i, j: (i, 0)),
                     pl.BlockSpec((kdim, tn), lambda i, j: (0, j))]
        args += [a, w]
    if res is not None:
        in_specs.append(pl.BlockSpec((tm, tn), lambda i, j: (i, j)))
        args.append(res)
    out_specs = [pl.BlockSpec((tm, tn), lambda i, j: (i, j))]
    out_shape = [jax.ShapeDtypeStruct((m, n), out_dtype)]
    if norm_g is not None:
        in_specs.append(pl.BlockSpec((1, n), lambda i, j: (0, 0)))
        args.append(norm_g.reshape(1, n).astype(F32))
        out_specs.append(pl.BlockSpec((tm, n), lambda i, j: (i, 0)))
        out_shape.append(jax.ShapeDtypeStruct((m, n), BF16))
    outs = pl.pallas_call(
        functools.partial(_matmul_body, n_pairs=len(pairs), has_res=res is not None,
                          has_norm=norm_g is not None),
        grid=(m // tm, n // tn),
        in_specs=in_specs,
        out_specs=out_specs,
        out_shape=out_shape,
        compiler_params=_cparams("parallel", "parallel"),
        name=name,
    )(*args)
    return outs if norm_g is not None else outs[0]


SWIGLU_ROWS = 1024


def _swiglu_up_body(a_ref, w1_ref, w3_ref, o_ref):
    for r0 in range(0, a_ref.shape[0], SWIGLU_ROWS):
        a = a_ref[pl.ds(r0, SWIGLU_ROWS), :]
        u = jnp.dot(a, w1_ref[...], preferred_element_type=F32)
        v = jnp.dot(a, w3_ref[...], preferred_element_type=F32)
        o_ref[pl.ds(r0, SWIGLU_ROWS), :] = (u * jax.nn.sigmoid(u) * v).astype(o_ref.dtype)


def swiglu_up(a, w1, w3, tm=2048, tn=512):
    m, kdim = a.shape
    n = w1.shape[1]
    assert m % tm == 0 and n % tn == 0
    return pl.pallas_call(
        _swiglu_up_body,
        grid=(m // tm, n // tn),
        in_specs=[pl.BlockSpec((tm, kdim), lambda i, j: (i, 0)),
                  pl.BlockSpec((kdim, tn), lambda i, j: (0, j)),
                  pl.BlockSpec((kdim, tn), lambda i, j: (0, j))],
        out_specs=pl.BlockSpec((tm, tn), lambda i, j: (i, j)),
        out_shape=jax.ShapeDtypeStruct((m, n), BF16),
        compiler_params=_cparams("parallel", "parallel"),
        name="swiglu_up",
    )(a, w1, w3)


def _norm_matmul_body(*refs, has_side):
    if has_side:
        x_ref, g_ref, w_ref, ws_ref, o_ref, os_ref, h_ref = refs
    else:
        x_ref, g_ref, w_ref, o_ref, h_ref = refs

    @pl.when(pl.program_id(1) == 0)
    def _():
        _norm_rows_into(x_ref, g_ref, h_ref)
        if has_side:
            os_ref[...] = jnp.dot(h_ref[...], ws_ref[...], preferred_element_type=F32)

    o_ref[...] = jnp.dot(h_ref[...], w_ref[...], preferred_element_type=F32).astype(o_ref.dtype)


def norm_matmul(x, g, w, w_side=None, out_dtype=F32, tm=1024, tn=1024, name="norm_matmul"):
    m, d = x.shape
    n = w.shape[1]
    assert m % tm == 0 and n % tn == 0 and tm % NORM_ROWS == 0
    in_specs = [pl.BlockSpec((tm, d), lambda i, j: (i, 0)),
                pl.BlockSpec((1, d), lambda i, j: (0, 0)),
                pl.BlockSpec((d, tn), lambda i, j: (0, j))]
    args = [x, g.reshape(1, d).astype(F32), w]
    out_specs = [pl.BlockSpec((tm, tn), lambda i, j: (i, j))]
    out_shape = [jax.ShapeDtypeStruct((m, n), out_dtype)]
    if w_side is not None:
        ns = w_side.shape[1]
        in_specs.append(pl.BlockSpec((d, ns), lambda i, j: (0, 0)))
        args.append(w_side)
        out_specs.append(pl.BlockSpec((tm, ns), lambda i, j: (i, 0)))
        out_shape.append(jax.ShapeDtypeStruct((m, ns), F32))
    outs = pl.pallas_call(
        functools.partial(_norm_matmul_body, has_side=w_side is not None),
        grid=(m // tm, n // tn),
        in_specs=in_specs,
        out_specs=out_specs,
        out_shape=out_shape,
        scratch_shapes=[pltpu.VMEM((tm, d), BF16)],
        compiler_params=_cparams("parallel", "arbitrary"),
        name=name,
    )(*args)
    return outs if w_side is not None else outs[0]


def _log_sigmoid(x):
    return jnp.minimum(x, 0.0) - jnp.log1p(jnp.exp(-jnp.abs(x)))


def _lane_scan(x, combine, identity, reverse):
    width = x.shape[-1]
    lane = lax.broadcasted_iota(jnp.int32, x.shape, 1)
    sh = 1
    while sh < width:
        if reverse:
            shifted = jnp.where(lane < width - sh, pltpu.roll(x, width - sh, axis=1), identity)
        else:
            shifted = jnp.where(lane >= sh, pltpu.roll(x, sh, axis=1), identity)
        x = combine(x, shifted)
        sh *= 2
    return x


def _mlstm_body(q_ref, k_ref, v_ref, og_ref, g_ref, gb_ref, hg_ref, y_ref,
                rows_ref, cols_ref, hf_ref, hb_ref, ct_ref, *, nc, chunk):
    dk, dv = MLSTM_DK, MLSTM_DV
    scale = dk ** -0.5

    g = g_ref[0, 0] + gb_ref[0]
    rows = []
    for d in range(2):
        b = _lane_scan(_log_sigmoid(g[2 * d + 1]), jnp.add, 0.0, reverse=d == 1)
        gg = g[2 * d] - b
        gmax = _lane_scan(gg, jnp.maximum, -jnp.inf, reverse=d == 1)
        rows += [gg, b, gmax]
        for j, r in enumerate((gg, b, gmax)):
            rows_ref[3 * d + j] = r
    rowmat = jnp.concatenate(rows + [jnp.zeros((LANES - 6 * nc, chunk), F32)], axis=0)
    colmat = rowmat.T
    for c in range(nc):
        cols_ref[c] = colmat if c == 0 else pltpu.roll(colmat, LANES - c, axis=1)

    ct_ref[...] = jnp.zeros_like(ct_ref)

    row_id = lax.broadcasted_iota(jnp.int32, (chunk, chunk), 0)
    col_id = lax.broadcasted_iota(jnp.int32, (chunk, chunk), 1)
    causal = (row_id >= col_id, row_id <= col_id)
    ones = jnp.ones((chunk, LANES), BF16)

    def lanes2(x):
        return jnp.concatenate([x] * (dv // LANES), axis=1)

    def chunk_step(c, m, d, h_ref):
        r0 = pl.multiple_of(c * chunk, chunk)
        qb = (q_ref[0, pl.ds(r0, chunk), :] * scale).astype(BF16)
        kb = k_ref[0, pl.ds(r0, chunk), :].astype(BF16)
        vf = v_ref[0, pl.ds(r0, chunk), :]
        cols = cols_ref[c]

        def column(j):
            lane = (3 * d + j) * nc
            return jnp.broadcast_to(cols[:, lane:lane + 1], (chunk, LANES))

        gg_col, b_col, gmax_col = column(0), column(1), column(2)
        gg_row = rows_ref[3 * d, pl.ds(c, 1), :]
        end = chunk - 1 if d == 0 else 0
        total = rows_ref[3 * d + 1, pl.ds(c, 1), :][:, end:end + 1]
        gmax_end = rows_ref[3 * d + 2, pl.ds(c, 1), :][:, end:end + 1]

        m_row = jnp.maximum(gmax_col, m)
        w_intra = jnp.exp(jnp.where(causal[d], gg_row - lanes2(m_row), NEG_BIG))
        w_inter = jnp.exp(m - m_row)
        s = lax.dot_general(qb, kb, (((1,), (1,)), ((), ())),
                            preferred_element_type=F32) * w_intra
        ct = ct_ref[d]
        intra = jnp.dot(s.astype(BF16), jnp.concatenate([vf.astype(BF16), ones], axis=1),
                        preferred_element_type=F32)
        inter = jnp.dot(qb, ct.astype(BF16), preferred_element_type=F32)
        num = intra[:, :dv] + lanes2(w_inter) * inter[:, :dv]
        den = intra[:, dv:] + w_inter * inter[:, dv:]
        inv = 1.0 / jnp.maximum(jnp.abs(den), jnp.exp(-(b_col + m_row)))
        h_ref[pl.ds(r0, chunk), :] = num * lanes2(inv)

        m_new = jnp.maximum(total + m, total + gmax_end)
        w_src = jnp.exp(total + gg_col - m_new)
        decay = jnp.exp(total + m - m_new)
        wv = jnp.concatenate([lanes2(w_src) * vf, w_src], axis=1).astype(BF16)
        ct_ref[d] = decay * ct + lax.dot_general(
            kb, wv, (((0,), (0,)), ((), ())), preferred_element_type=F32)
        return m_new

    def body(c, carry):
        m_f, m_b = carry
        m_f = chunk_step(c, m_f, 0, hf_ref)
        m_b = chunk_step(nc - 1 - c, m_b, 1, hb_ref)
        return m_f, m_b

    m0 = jnp.full((1, 1), NEG_BIG, F32)
    lax.fori_loop(0, nc, body, (m0, m0))

    hg = hg_ref[0]

    def finish(c, carry):
        r0 = pl.multiple_of(c * chunk, chunk)
        hm = hf_ref[pl.ds(r0, chunk), :] + hb_ref[pl.ds(r0, chunk), :]
        ms = jnp.mean(hm * hm, axis=-1, keepdims=True)
        hn = hm * lax.rsqrt(ms + RMS_EPS) * hg
        gate = jax.nn.sigmoid(og_ref[0, pl.ds(r0, chunk), :])
        y_ref[0, pl.ds(r0, chunk), :] = (hn * gate).astype(y_ref.dtype)
        return carry

    lax.fori_loop(0, nc, finish, 0)


def mlstm_mixer(z, gates_rows, gate_b_rows, head_g, chunk=MLSTM_CHUNK):
    b, s, _ = z.shape
    h, dk, dv = MLSTM_HEADS, MLSTM_DK, MLSTM_DV
    nc = s // chunk
    assert s % chunk == 0 and 6 * nc <= LANES and chunk % LANES == 0
    kq = h * dk // dk
    kv = 2 * h * dk // dv
    ko = kv + h
    return pl.pallas_call(
        functools.partial(_mlstm_body, nc=nc, chunk=chunk),
        grid=(b, h),
        in_specs=[
            pl.BlockSpec((1, s, dk), lambda i, j: (i, 0, j)),
            pl.BlockSpec((1, s, dk), lambda i, j: (i, 0, kq + j)),
            pl.BlockSpec((1, s, dv), lambda i, j: (i, 0, kv + j)),
            pl.BlockSpec((1, s, dv), lambda i, j: (i, 0, ko + j)),
            pl.BlockSpec((1, 1, 4, nc, chunk), lambda i, j: (i, j, 0, 0, 0)),
            pl.BlockSpec((1, 4, 1, chunk), lambda i, j: (j, 0, 0, 0)),
            pl.BlockSpec((1, 1, dv), lambda i, j: (j, 0, 0)),
        ],
        out_specs=pl.BlockSpec((1, s, dv), lambda i, j: (i, 0, j)),
        out_shape=jax.ShapeDtypeStruct((b, s, h * dv), BF16),
        scratch_shapes=[
            pltpu.VMEM((6, nc, chunk), F32),
            pltpu.VMEM((nc, chunk, LANES), F32),
            pltpu.VMEM((s, dv), F32),
            pltpu.VMEM((s, dv), F32),
            pltpu.VMEM((2, dk, dv + LANES), F32),
        ],
        compiler_params=_cparams("parallel", "parallel"),
        name="mlstm",
    )(z, z, z, z, gates_rows, gate_b_rows, head_g)


RG_TILE = 512
RG_PAD = SUBLANES
RG_SCAN_UNROLL = 8


def _softplus(x):
    return jnp.maximum(x, 0.0) + jnp.log1p(jnp.exp(-jnp.abs(x)))


def _sigmoid(x):
    return 0.5 * jnp.tanh(0.5 * x) + 0.5


def _gelu_tanh(x):
    c = math.sqrt(2.0 / math.pi)
    return x * (0.5 * (1.0 + jnp.tanh(c * (x + 0.044715 * (x * x * x)))))


def _rglru_body(x_ref, gr_ref, cw_ref, cb_ref, wa_ref, wx_ref, ba_ref, bx_ref, lam_ref, y_ref,
                xpad_ref, a_ref, u_ref, as_ref, us_ref, h_ref, *, s):
    nt = s // RG_TILE
    zeros = jnp.zeros((RG_PAD, LANES), F32)
    xpad_ref[pl.ds(0, RG_PAD), :] = zeros
    xpad_ref[pl.ds(RG_PAD + s, RG_PAD), :] = zeros

    def copy_in(t, carry):
        r0 = pl.multiple_of(t * RG_TILE, RG_TILE)
        xpad_ref[pl.ds(RG_PAD + r0, RG_TILE), :] = x_ref[0, pl.ds(r0, RG_TILE), :]
        return carry

    lax.fori_loop(0, nt, copy_in, 0)

    cw = cw_ref[...]
    cb = cb_ref[...]
    sp = [RGLRU_C * _softplus(-lam_ref[d:d + 1, :]) for d in range(2)]

    def gates(t, carry):
        r0 = pl.multiple_of(t * RG_TILE, RG_TILE)
        xc = None
        for j in range(CONV_WIDTH):
            tap = xpad_ref[pl.ds(r0 + RG_PAD - CONV_LEFT + j, RG_TILE), :] * cw[j:j + 1, :]
            xc = tap if xc is None else xc + tap
        xc = xc + cb
        xcb = xc.astype(BF16)
        for d in range(2):
            r = _sigmoid(jnp.dot(xcb, wa_ref[d, 0], preferred_element_type=F32)
                         + ba_ref[d:d + 1, :])
            i = _sigmoid(jnp.dot(xcb, wx_ref[d, 0], preferred_element_type=F32)
                         + bx_ref[d:d + 1, :])
            neg_log_a = sp[d] * r
            a = jnp.exp(-neg_log_a)
            one_minus_a2 = jnp.tanh(neg_log_a) * (1.0 + a * a)
            root = jnp.where(one_minus_a2 > 0.0, one_minus_a2 * lax.rsqrt(one_minus_a2), 0.0)
            a_ref[d, pl.ds(r0, RG_TILE), :] = a
            u_ref[d, pl.ds(r0, RG_TILE), :] = root * (i * xc)
        return carry

    lax.fori_loop(0, nt, gates, 0)

    def block_scan(t, carry):
        r0 = pl.multiple_of(t * RG_TILE, RG_TILE)
        for d in range(2):
            a_prev = u_prev = None
            for r in (range(SUBLANES) if d == 0 else reversed(range(SUBLANES))):
                rows = pl.ds(r0 + r, RG_TILE // SUBLANES, stride=SUBLANES)
                a, u = a_ref[d, rows, :], u_ref[d, rows, :]
                if a_prev is not None:
                    u = a * u_prev + u
                    a = a * a_prev
                as_ref[d, rows, :] = a
                us_ref[d, rows, :] = u
                a_prev, u_prev = a, u
        return carry

    lax.fori_loop(0, nt, block_scan, 0)

    nblk = s // SUBLANES

    def scan(j, carry):
        h_f, h_b = carry
        for d, last, h in ((0, SUBLANES - 1, h_f), (1, 0, h_b)):
            blk = j if d == 0 else nblk - 1 - j
            rows = pl.ds(pl.multiple_of(blk * SUBLANES, SUBLANES), SUBLANES)
            a, u = as_ref[d, rows, :], us_ref[d, rows, :]
            h_ref[d, rows, :] = a * h + u
            a_end = jnp.broadcast_to(a[last:last + 1, :], (SUBLANES, LANES))
            u_end = jnp.broadcast_to(u[last:last + 1, :], (SUBLANES, LANES))
            if d == 0:
                h_f = a_end * h + u_end
            else:
                h_b = a_end * h + u_end
        return h_f, h_b

    h0 = jnp.zeros((SUBLANES, LANES), F32)
    lax.fori_loop(0, nblk, scan, (h0, h0), unroll=RG_SCAN_UNROLL)

    def finish(t, carry):
        r0 = pl.multiple_of(t * RG_TILE, RG_TILE)
        hr = h_ref[0, pl.ds(r0, RG_TILE), :] + h_ref[1, pl.ds(r0, RG_TILE), :]
        y_ref[0, pl.ds(r0, RG_TILE), :] = (
            hr * _gelu_tanh(gr_ref[0, pl.ds(r0, RG_TILE), :])).astype(y_ref.dtype)
        return carry

    lax.fori_loop(0, nt, finish, 0)


def rglru_mixer(z, xr_block0, gr_block0, conv_w, conv_b, wa, wx, ba, bx, lam):
    b, s, _ = z.shape
    width = RNN_BLOCKS * RNN_BLOCK
    assert s % RG_TILE == 0
    return pl.pallas_call(
        functools.partial(_rglru_body, s=s),
        grid=(b, RNN_BLOCKS),
        in_specs=[
            pl.BlockSpec((1, s, RNN_BLOCK), lambda i, j: (i, 0, xr_block0 + j)),
            pl.BlockSpec((1, s, RNN_BLOCK), lambda i, j: (i, 0, gr_block0 + j)),
            pl.BlockSpec((CONV_WIDTH, RNN_BLOCK), lambda i, j: (0, j)),
            pl.BlockSpec((1, RNN_BLOCK), lambda i, j: (0, j)),
            pl.BlockSpec((2, 1, RNN_BLOCK, RNN_BLOCK), lambda i, j: (0, j, 0, 0)),
            pl.BlockSpec((2, 1, RNN_BLOCK, RNN_BLOCK), lambda i, j: (0, j, 0, 0)),
            pl.BlockSpec((2, RNN_BLOCK), lambda i, j: (0, j)),
            pl.BlockSpec((2, RNN_BLOCK), lambda i, j: (0, j)),
            pl.BlockSpec((2, RNN_BLOCK), lambda i, j: (0, j)),
        ],
        out_specs=pl.BlockSpec((1, s, RNN_BLOCK), lambda i, j: (i, 0, j)),
        out_shape=jax.ShapeDtypeStruct((b, s, width), BF16),
        scratch_shapes=[
            pltpu.VMEM((s + 2 * RG_PAD, RNN_BLOCK), F32),
            pltpu.VMEM((2, s, RNN_BLOCK), F32),
            pltpu.VMEM((2, s, RNN_BLOCK), F32),
            pltpu.VMEM((2, s, RNN_BLOCK), F32),
            pltpu.VMEM((2, s, RNN_BLOCK), F32),
            pltpu.VMEM((2, s, RNN_BLOCK), F32),
        ],
        compiler_params=_cparams("parallel", "parallel"),
        name="rglru",
    )(z, z, conv_w, conv_b, wa, wx, ba, bx, lam)


ATTN_TQ = 128
ATTN_TK = ATTN_TQ + 2 * ATTN_HALF
ATTN_MERGE_TILE = 256
ATTN_UNROLL = 32
LOG2E = math.log2(math.e)


def _attn_body(q_ref, k_ref, v_ref, slope_ref, o_ref, bias_ref, x4_ref, og_ref, lse_ref, *, s):
    qscale = ATTN_DH ** -0.5 * LOG2E
    slope = slope_ref[0, 0:1, 0:1] * LOG2E
    qi = lax.broadcasted_iota(jnp.int32, (ATTN_TQ, ATTN_TK), 0)
    kj = lax.broadcasted_iota(jnp.int32, (ATTN_TQ, ATTN_TK), 1)
    for g, dil in enumerate(ATTN_DILATIONS):
        for e in range(3):
            rel = jnp.abs(kj - e * ATTN_HALF - qi)
            pen = slope * (rel * dil).astype(F32)
            bias_ref[g, e] = jnp.where(rel <= ATTN_HALF, -pen, NEG_BIG)

    s4 = s // 4
    srcs = (q_ref, k_ref, v_ref)

    def split4(t, carry):
        c = t // (s4 // ATTN_TK)
        p0 = (t % (s4 // ATTN_TK)) * ATTN_TK
        dst = pl.ds(pl.multiple_of(c * s4 + p0, ATTN_TK), ATTN_TK)
        for a in range(3):
            x = srcs[a][0, pl.ds(c + 4 * p0, ATTN_TK, stride=4), :]
            x4_ref[a, dst, :] = x * qscale if a == 0 else x
        return carry

    lax.fori_loop(0, 4 * (s4 // ATTN_TK), split4, 0)

    ones = jnp.ones((ATTN_TK, ATTN_DH), BF16)

    for g, dil in enumerate(ATTN_DILATIONS):
        sp = s // dil
        nqb = sp // ATTN_TQ

        def q_block(t, carry, g=g, dil=dil, sp=sp, nqb=nqb):
            r = t // nqb
            p0 = (t % nqb) * ATTN_TQ
            kstart = jnp.clip(p0 - ATTN_HALF, 0, sp - ATTN_TK)
            e = (p0 - kstart) // ATTN_HALF
            if dil == 1:
                qrows = pl.ds(pl.multiple_of(p0, ATTN_TQ), ATTN_TQ)
                krows = pl.ds(pl.multiple_of(kstart, ATTN_HALF), ATTN_TK)
                qf = q_ref[0, qrows, :] * qscale
                kf, vf = k_ref[0, krows, :], v_ref[0, krows, :]
                orows = qrows
            elif dil == 4:
                qrows = pl.ds(pl.multiple_of(r * s4 + p0, ATTN_TQ), ATTN_TQ)
                krows = pl.ds(pl.multiple_of(r * s4 + kstart, ATTN_HALF), ATTN_TK)
                qf, kf, vf = x4_ref[0, qrows, :], x4_ref[1, krows, :], x4_ref[2, krows, :]
                orows = pl.ds(r + 4 * p0, ATTN_TQ, stride=4)
            else:
                base = (r % 4) * s4 + r // 4
                qrows = pl.ds(base + 4 * p0, ATTN_TQ, stride=4)
                krows = pl.ds(base + 4 * kstart, ATTN_TK, stride=4)
                qf, kf, vf = x4_ref[0, qrows, :], x4_ref[1, krows, :], x4_ref[2, krows, :]
                orows = pl.ds(r + dil * p0, ATTN_TQ, stride=dil)
            qb, kb, vb = qf.astype(BF16), kf.astype(BF16), vf.astype(BF16)
            sc = lax.dot_general(qb, kb, (((1,), (1,)), ((), ())),
                                 preferred_element_type=F32) + bias_ref[g, e]
            m = jnp.max(sc, axis=-1, keepdims=True)
            p = jnp.exp2(sc - m).astype(BF16)
            pv = jnp.dot(p, jnp.concatenate([vb, ones], axis=1), preferred_element_type=F32)
            den = pv[:, ATTN_DH:]
            og_ref[g, orows, :] = pv[:, :ATTN_DH] / den
            lse_ref[g, orows, :] = m + jnp.log2(den)
            return carry

        lax.fori_loop(0, dil * nqb, q_block, 0, unroll=ATTN_UNROLL)

    def merge(t, carry):
        r0 = pl.multiple_of(t * ATTN_MERGE_TILE, ATTN_MERGE_TILE)
        sl = pl.ds(r0, ATTN_MERGE_TILE)
        l0, l1, l2 = lse_ref[0, sl, :], lse_ref[1, sl, :], lse_ref[2, sl, :]
        mx = jnp.maximum(jnp.maximum(l0, l1), l2)
        e0, e1, e2 = jnp.exp2(l0 - mx), jnp.exp2(l1 - mx), jnp.exp2(l2 - mx)
        inv = 1.0 / (e0 + e1 + e2)
        o = (e0 * inv) * og_ref[0, sl, :] + (e1 * inv) * og_ref[1, sl, :] + (e2 * inv) * og_ref[2, sl, :]
        o_ref[0, sl, :] = o.astype(o_ref.dtype)
        return carry

    lax.fori_loop(0, s // ATTN_MERGE_TILE, merge, 0)


def dilated_attention(qkv, slopes):
    b, s, _ = qkv.shape
    h, dh = ATTN_HEADS, ATTN_DH
    assert s % (max(ATTN_DILATIONS) * ATTN_TK) == 0 and s % ATTN_MERGE_TILE == 0
    return pl.pallas_call(
        functools.partial(_attn_body, s=s),
        grid=(b, h),
        in_specs=[
            pl.BlockSpec((1, s, dh), lambda i, j: (i, 0, j)),
            pl.BlockSpec((1, s, dh), lambda i, j: (i, 0, h + j)),
            pl.BlockSpec((1, s, dh), lambda i, j: (i, 0, 2 * h + j)),
            pl.BlockSpec((1, SUBLANES, LANES), lambda i, j: (j, 0, 0)),
        ],
        out_specs=pl.BlockSpec((1, s, dh), lambda i, j: (i, 0, j)),
        out_shape=jax.ShapeDtypeStruct((b, s, h * dh), BF16),
        scratch_shapes=[
            pltpu.VMEM((len(ATTN_DILATIONS), 3, ATTN_TQ, ATTN_TK), F32),
            pltpu.VMEM((3, s, dh), F32),
            pltpu.VMEM((len(ATTN_DILATIONS), s, dh), F32),
            pltpu.VMEM((len(ATTN_DILATIONS), s, dh), F32),
        ],
        compiler_params=_cparams("parallel", "parallel"),
        name="dilated_attn",
    )(qkv, qkv, qkv, slopes)


MIXER_OUT_TM = 512


def _ffn(x, h, w1, w3, w2):
    u = swiglu_up(h, w1.astype(BF16), w3.astype(BF16))
    return matmul([(u, w2.astype(BF16))], res=x, tn=512, name="ffn_down")


def _even_layer(x, bsz, seq, norm_g, w_in, gate_b, conv_w, conv_b, rg_wa, rg_ba, rg_wx, rg_bx,
                rg_lam, head_g, w_out, next_norm_g):
    heads, dk, dv = MLSTM_HEADS, MLSTM_DK, MLSTM_DV
    n_qkvo = 2 * heads * dk + 2 * heads * dv
    n_gate = 4 * heads
    width = RNN_BLOCKS * RNN_BLOCK
    w_main = jnp.concatenate([w_in[:, :n_qkvo], w_in[:, n_qkvo + n_gate:]], axis=1).astype(BF16)
    w_gate = jnp.pad(w_in[:, n_qkvo:n_qkvo + n_gate], ((0, 0), (0, LANES - n_gate))).astype(BF16)
    z, zg = norm_matmul(x, norm_g, w_main, w_side=w_gate, name="in_proj")
    z = z.reshape(bsz, seq, -1)

    chunk = MLSTM_CHUNK
    nc = seq // chunk
    gates_rows = jnp.transpose(zg[:, :n_gate].reshape(bsz, nc, chunk, 4, heads), (0, 4, 3, 1, 2))
    gate_b_rows = jnp.broadcast_to(
        jnp.transpose(gate_b.astype(F32).reshape(4, heads))[:, :, None, None], (heads, 4, 1, chunk))
    y_a = mlstm_mixer(z, gates_rows, gate_b_rows, head_g.astype(F32).reshape(heads, 1, dv), chunk)

    y_b = rglru_mixer(z, n_qkvo // RNN_BLOCK, (n_qkvo + width) // RNN_BLOCK,
                      conv_w.astype(F32), conv_b.astype(F32).reshape(1, width),
                      rg_wa.astype(BF16), rg_wx.astype(BF16),
                      rg_ba.astype(F32), rg_bx.astype(F32), rg_lam.astype(F32))

    w_out_b = w_out.astype(BF16)
    m = bsz * seq
    return matmul([(y_a.reshape(m, -1), w_out_b[:heads * dv]),
                   (y_b.reshape(m, -1), w_out_b[heads * dv:])], res=x, norm_g=next_norm_g,
                  tm=MIXER_OUT_TM, tn=w_out.shape[1], name="out_proj")


def _odd_layer(x, bsz, seq, norm_g, w_qkv, w_o, next_norm_g):
    qkv = norm_matmul(x, norm_g, w_qkv.astype(BF16), name="qkv_proj").reshape(bsz, seq, -1)
    slopes = jnp.exp2(-ALIBI_MAX_BIAS * jnp.arange(1, ATTN_HEADS + 1, dtype=F32) / ATTN_HEADS)
    slopes = jnp.broadcast_to(slopes[:, None, None], (ATTN_HEADS, SUBLANES, LANES))
    o = dilated_attention(qkv, slopes)
    return matmul([(o.reshape(bsz * seq, -1), w_o.astype(BF16))], res=x, norm_g=next_norm_g,
                  tm=MIXER_OUT_TM, tn=w_o.shape[1], name="attn_out_proj")


def kernel(x, e_norm, e_w_in, e_gate_b, e_conv_w, e_conv_b, e_rg_wa, e_rg_ba, e_rg_wx, e_rg_bx,
           e_rg_lam, e_head_g, e_w_out, o_norm, o_w_qkv, o_w_o, f_norm, f_w1, f_w3, f_w2,
           final_norm):
    bsz, seq, d = x.shape
    depth = f_norm.shape[0]
    xs = x.reshape(bsz * seq, d).astype(F32)
    for l in range(depth):
        if l % 2 == 0:
            e = l // 2
            xs, hs = _even_layer(xs, bsz, seq, e_norm[e], e_w_in[e], e_gate_b[e], e_conv_w[e],
                                 e_conv_b[e], e_rg_wa[e], e_rg_ba[e], e_rg_wx[e], e_rg_bx[e],
                                 e_rg_lam[e], e_head_g[e], e_w_out[e], f_norm[l])
        else:
            o = l // 2
            xs, hs = _odd_layer(xs, bsz, seq, o_norm[o], o_w_qkv[o], o_w_o[o], f_norm[l])
        xs = _ffn(xs, hs, f_w1[l], f_w3[l], f_w2[l])
    return rmsnorm(xs, final_norm, x.dtype).reshape(bsz, seq, d)
```

```python
import functools
import math

import jax
import jax.numpy as jnp
from jax import lax
from jax.experimental import pallas as pl
from jax.experimental.pallas import tpu as pltpu

F32 = jnp.float32
BF16 = jnp.bfloat16

RMS_EPS = 1e-6
NEG_BIG = -1e30

MLSTM_HEADS = 4
MLSTM_DK = 128
MLSTM_DV = 256
MLSTM_CHUNK = 256
RNN_BLOCKS = 8
RNN_BLOCK = 128
CONV_WIDTH = 4
CONV_LEFT = 2
RGLRU_C = 8.0
ATTN_HEADS = 16
ATTN_DH = 128
ATTN_HALF = 64
ATTN_DILATIONS = (1, 4, 16)
ALIBI_MAX_BIAS = 8.0

LANES = 128
SUBLANES = 8
VMEM_LIMIT_BYTES = 56 * 1024 * 1024


def _cparams(*semantics):
    return pltpu.CompilerParams(dimension_semantics=semantics,
                                vmem_limit_bytes=VMEM_LIMIT_BYTES)


def _rmsnorm_body(x_ref, g_ref, o_ref):
    x = x_ref[...]
    ms = jnp.mean(x * x, axis=-1, keepdims=True)
    o_ref[...] = (x * lax.rsqrt(ms + RMS_EPS) * g_ref[...]).astype(o_ref.dtype)


def rmsnorm(x, g, out_dtype, tm=512):
    m, d = x.shape
    return pl.pallas_call(
        _rmsnorm_body,
        grid=(m // tm,),
        in_specs=[pl.BlockSpec((tm, d), lambda i: (i, 0)),
                  pl.BlockSpec((1, d), lambda i: (0, 0))],
        out_specs=pl.BlockSpec((tm, d), lambda i: (i, 0)),
        out_shape=jax.ShapeDtypeStruct((m, d), out_dtype),
        compiler_params=_cparams("parallel"),
        name="rmsnorm",
    )(x, g.reshape(1, d).astype(F32))


NORM_ROWS = 128


def _norm_rows_into(x_ref, g_ref, h_ref):
    g = g_ref[...]

    def rows(t, carry):
        sl = pl.ds(pl.multiple_of(t * NORM_ROWS, NORM_ROWS), NORM_ROWS)
        x = x_ref[sl, :]
        ms = jnp.mean(x * x, axis=-1, keepdims=True)
        h_ref[sl, :] = (x * lax.rsqrt(ms + RMS_EPS) * g).astype(h_ref.dtype)
        return carry

    lax.fori_loop(0, x_ref.shape[0] // NORM_ROWS, rows, 0)


def _matmul_body(*refs, n_pairs, has_res, has_norm):
    pairs = [(refs[2 * p], refs[2 * p + 1]) for p in range(n_pairs)]
    rest = list(refs[2 * n_pairs:])
    r_ref = rest.pop(0) if has_res else None
    g_ref = rest.pop(0) if has_norm else None
    o_ref = rest.pop(0)

    acc = None
    for a_ref, w_ref in pairs:
        d = jnp.dot(a_ref[...], w_ref[...], preferred_element_type=F32)
        acc = d if acc is None else acc + d
    if has_res:
        acc = acc + r_ref[...]
    o_ref[...] = acc.astype(o_ref.dtype)
    if has_norm:
        _norm_rows_into(o_ref, g_ref, rest.pop(0))


def matmul(pairs, res=None, norm_g=None, out_dtype=F32, tm=1024, tn=1024, name="matmul"):
    m = pairs[0][0].shape[0]
    n = pairs[0][1].shape[1]
    tn = min(tn, n)
    assert m % tm == 0 and n % tn == 0
    assert norm_g is None or (tn == n and out_dtype == F32 and tm % NORM_ROWS == 0)
    in_specs, args = [], []
    for a, w in pairs:
        kdim = a.shape[1]
        in_specs += [pl.BlockSpec((tm, kdim), lambda i, j: (i, 0)),
                     pl.BlockSpec((kdim, tn), lambda i, j: (0, j))]
        args += [a, w]
    if res is not None:
        in_specs.append(pl.BlockSpec((tm, tn), lambda i, j: (i, j)))
        args.append(res)
    out_specs = [pl.BlockSpec((tm, tn), lambda i, j: (i, j))]
    out_shape = [jax.ShapeDtypeStruct((m, n), out_dtype)]
    if norm_g is not None:
        in_specs.append(pl.BlockSpec((1, n), lambda i, j: (0, 0)))
        args.append(norm_g.reshape(1, n).astype(F32))
        out_specs.append(pl.BlockSpec((tm, n), lambda i, j: (i, 0)))
        out_shape.append(jax.ShapeDtypeStruct((m, n), BF16))
    outs = pl.pallas_call(
        functools.partial(_matmul_body, n_pairs=len(pairs), has_res=res is not None,
                          has_norm=norm_g is not None),
        grid=(m // tm, n // tn),
        in_specs=in_specs,
        out_specs=out_specs,
        out_shape=out_shape,
        compiler_params=_cparams("parallel", "parallel"),
        name=name,
    )(*args)
    return outs if norm_g is not None else outs[0]


SWIGLU_ROWS = 1024


def _swiglu_up_body(a_ref, w1_ref, w3_ref, o_ref):
    for r0 in range(0, a_ref.shape[0], SWIGLU_ROWS):
        a = a_ref[pl.ds(r0, SWIGLU_ROWS), :]
        u = jnp.dot(a, w1_ref[...], preferred_element_type=F32)
        v = jnp.dot(a, w3_ref[...], preferred_element_type=F32)
        o_ref[pl.ds(r0, SWIGLU_ROWS), :] = (u * jax.nn.sigmoid(u) * v).astype(o_ref.dtype)


def swiglu_up(a, w1, w3, tm=2048, tn=512):
    m, kdim = a.shape
    n = w1.shape[1]
    assert m % tm == 0 and n % tn == 0
    return pl.pallas_call(
        _swiglu_up_body,
        grid=(m // tm, n // tn),
        in_specs=[pl.BlockSpec((tm, kdim), lambda i, j: (i, 0)),
                  pl.BlockSpec((kdim, tn), lambda i, j: (0, j)),
                  pl.BlockSpec((kdim, tn), lambda i, j: (0, j))],
        out_specs=pl.BlockSpec((tm, tn), lambda i, j: (i, j)),
        out_shape=jax.ShapeDtypeStruct((m, n), BF16),
        compiler_params=_cparams("parallel", "parallel"),
        name="swiglu_up",
    )(a, w1, w3)


def _norm_matmul_body(*refs, has_side):
    if has_side:
        x_ref, g_ref, w_ref, ws_ref, o_ref, os_ref, h_ref = refs
    else:
        x_ref, g_ref, w_ref, o_ref, h_ref = refs

    @pl.when(pl.program_id(1) == 0)
    def _():
        _norm_rows_into(x_ref, g_ref, h_ref)
        if has_side:
            os_ref[...] = jnp.dot(h_ref[...], ws_ref[...], preferred_element_type=F32)

    o_ref[...] = jnp.dot(h_ref[...], w_ref[...], preferred_element_type=F32).astype(o_ref.dtype)


def norm_matmul(x, g, w, w_side=None, out_dtype=F32, tm=1024, tn=1024, name="norm_matmul"):
    m, d = x.shape
    n = w.shape[1]
    assert m % tm == 0 and n % tn == 0 and tm % NORM_ROWS == 0
    in_specs = [pl.BlockSpec((tm, d), lambda i, j: (i, 0)),
                pl.BlockSpec((1, d), lambda i, j: (0, 0)),
                pl.BlockSpec((d, tn), lambda i, j: (0, j))]
    args = [x, g.reshape(1, d).astype(F32), w]
    out_specs = [pl.BlockSpec((tm, tn), lambda i, j: (i, j))]
    out_shape = [jax.ShapeDtypeStruct((m, n), out_dtype)]
    if w_side is not None:
        ns = w_side.shape[1]
        in_specs.append(pl.BlockSpec((d, ns), lambda i, j: (0, 0)))
        args.append(w_side)
        out_specs.append(pl.BlockSpec((tm, ns), lambda i, j: (i, 0)))
        out_shape.append(jax.ShapeDtypeStruct((m, ns), F32))
    outs = pl.pallas_call(
        functools.partial(_norm_matmul_body, has_side=w_side is not None),
        grid=(m // tm, n // tn),
        in_specs=in_specs,
        out_specs=out_specs,
        out_shape=out_shape,
        scratch_shapes=[pltpu.VMEM((tm, d), BF16)],
        compiler_params=_cparams("parallel", "arbitrary"),
        name=name,
    )(*args)
    return outs if w_side is not None else outs[0]


def _log_sigmoid(x):
    return jnp.minimum(x, 0.0) - jnp.log1p(jnp.exp(-jnp.abs(x)))


def _lane_scan(x, combine, identity, reverse):
    width = x.shape[-1]
    lane = lax.broadcasted_iota(jnp.int32, x.shape, 1)
    sh = 1
    while sh < width:
        if reverse:
            shifted = jnp.where(lane < width - sh, pltpu.roll(x, width - sh, axis=1), identity)
        else:
            shifted = jnp.where(lane >= sh, pltpu.roll(x, sh, axis=1), identity)
        x = combine(x, shifted)
        sh *= 2
    return x


def _mlstm_body(q_ref, k_ref, v_ref, og_ref, g_ref, gb_ref, hg_ref, y_ref,
                rows_ref, cols_ref, hf_ref, hb_ref, ct_ref, *, nc, chunk):
    dk, dv = MLSTM_DK, MLSTM_DV
    scale = dk ** -0.5

    g = g_ref[0, 0] + gb_ref[0]
    rows = []
    for d in range(2):
        b = _lane_scan(_log_sigmoid(g[2 * d + 1]), jnp.add, 0.0, reverse=d == 1)
        gg = g[2 * d] - b
        gmax = _lane_scan(gg, jnp.maximum, -jnp.inf, reverse=d == 1)
        rows += [gg, b, gmax]
        for j, r in enumerate((gg, b, gmax)):
            rows_ref[3 * d + j] = r
    rowmat = jnp.concatenate(rows + [jnp.zeros((LANES - 6 * nc, chunk), F32)], axis=0)
    colmat = rowmat.T
    for c in range(nc):
        cols_ref[c] = colmat if c == 0 else pltpu.roll(colmat, LANES - c, axis=1)

    ct_ref[...] = jnp.zeros_like(ct_ref)

    row_id = lax.broadcasted_iota(jnp.int32, (chunk, chunk), 0)
    col_id = lax.broadcasted_iota(jnp.int32, (chunk, chunk), 1)
    causal = (row_id >= col_id, row_id <= col_id)
    ones = jnp.ones((chunk, LANES), BF16)

    def lanes2(x):
        return jnp.concatenate([x] * (dv // LANES), axis=1)

    def chunk_step(c, m, d, h_ref):
        r0 = pl.multiple_of(c * chunk, chunk)
        qb = (q_ref[0, pl.ds(r0, chunk), :] * scale).astype(BF16)
        kb = k_ref[0, pl.ds(r0, chunk), :].astype(BF16)
        vf = v_ref[0, pl.ds(r0, chunk), :]
        cols = cols_ref[c]

        def column(j):
            lane = (3 * d + j) * nc
            return jnp.broadcast_to(cols[:, lane:lane + 1], (chunk, LANES))

        gg_col, b_col, gmax_col = column(0), column(1), column(2)
        gg_row = rows_ref[3 * d, pl.ds(c, 1), :]
        end = chunk - 1 if d == 0 else 0
        total = rows_ref[3 * d + 1, pl.ds(c, 1), :][:, end:end + 1]
        gmax_end = rows_ref[3 * d + 2, pl.ds(c, 1), :][:, end:end + 1]

        m_row = jnp.maximum(gmax_col, m)
        w_intra = jnp.exp(jnp.where(causal[d], gg_row - lanes2(m_row), NEG_BIG))
        w_inter = jnp.exp(m - m_row)
        s = lax.dot_general(qb, kb, (((1,), (1,)), ((), ())),
                            preferred_element_type=F32) * w_intra
        ct = ct_ref[d]
        intra = jnp.dot(s.astype(BF16), jnp.concatenate([vf.astype(BF16), ones], axis=1),
                        preferred_element_type=F32)
        inter = jnp.dot(qb, ct.astype(BF16), preferred_element_type=F32)
        num = intra[:, :dv] + lanes2(w_inter) * inter[:, :dv]
        den = intra[:, dv:] + w_inter * inter[:, dv:]
        inv = 1.0 / jnp.maximum(jnp.abs(den), jnp.exp(-(b_col + m_row)))
        h_ref[pl.ds(r0, chunk), :] = num * lanes2(inv)

        m_new = jnp.maximum(total + m, total + gmax_end)
        w_src = jnp.exp(total + gg_col - m_new)
        decay = jnp.exp(total + m - m_new)
        wv = jnp.concatenate([lanes2(w_src) * vf, w_src], axis=1).astype(BF16)
        ct_ref[d] = decay * ct + lax.dot_general(
            kb, wv, (((0,), (0,)), ((), ())), preferred_element_type=F32)
        return m_new

    def body(c, carry):
        m_f, m_b = carry
        m_f = chunk_step(c, m_f, 0, hf_ref)
        m_b = chunk_step(nc - 1 - c, m_b, 1, hb_ref)
        return m_f, m_b

    m0 = jnp.full((1, 1), NEG_BIG, F32)
    lax.fori_loop(0, nc, body, (m0, m0), unroll=2)

    hg = hg_ref[0]

    def finish(c, carry):
        r0 = pl.multiple_of(c * chunk, chunk)
        hm = hf_ref[pl.ds(r0, chunk), :] + hb_ref[pl.ds(r0, chunk), :]
        ms = jnp.mean(hm * hm, axis=-1, keepdims=True)
        hn = hm * lax.rsqrt(ms + RMS_EPS) * hg
        gate = jax.nn.sigmoid(og_ref[0, pl.ds(r0, chunk), :])
        y_ref[0, pl.ds(r0, chunk), :] = (hn * gate).astype(y_ref.dtype)
        return carry

    lax.fori_loop(0, nc, finish, 0)


def mlstm_mixer(z, gates_rows, gate_b_rows, head_g, chunk=MLSTM_CHUNK):
    b, s, _ = z.shape
    h, dk, dv = MLSTM_HEADS, MLSTM_DK, MLSTM_DV
    nc = s // chunk
    assert s % chunk == 0 and 6 * nc <= LANES and chunk % LANES == 0
    kq = h * dk // dk
    kv = 2 * h * dk // dv
    ko = kv + h
    return pl.pallas_call(
        functools.partial(_mlstm_body, nc=nc, chunk=chunk),
        grid=(b, h),
        in_specs=[
            pl.BlockSpec((1, s, dk), lambda i, j: (i, 0, j)),
            pl.BlockSpec((1, s, dk), lambda i, j: (i, 0, kq + j)),
            pl.BlockSpec((1, s, dv), lambda i, j: (i, 0, kv + j)),
            pl.BlockSpec((1, s, dv), lambda i, j: (i, 0, ko + j)),
            pl.BlockSpec((1, 1, 4, nc, chunk), lambda i, j: (i, j, 0, 0, 0)),
            pl.BlockSpec((1, 4, 1, chunk), lambda i, j: (j, 0, 0, 0)),
            pl.BlockSpec((1, 1, dv), lambda i, j: (j, 0, 0)),
        ],
        out_specs=pl.BlockSpec((1, s, dv), lambda i, j: (i, 0, j)),
        out_shape=jax.ShapeDtypeStruct((b, s, h * dv), BF16),
        scratch_shapes=[
            pltpu.VMEM((6, nc, chunk), F32),
            pltpu.VMEM((nc, chunk, LANES), F32),
            pltpu.VMEM((s, dv), F32),
            pltpu.VMEM((s, dv), F32),
            pltpu.VMEM((2, dk, dv + LANES), F32),
        ],
        compiler_params=_cparams("parallel", "parallel"),
        name="mlstm",
    )(z, z, z, z, gates_rows, gate_b_rows, head_g)


RG_TILE = 512
RG_PAD = SUBLANES
RG_SCAN_UNROLL = 8


def _softplus(x):
    return jnp.maximum(x, 0.0) + jnp.log1p(jnp.exp(-jnp.abs(x)))


def _sigmoid(x):
    return 0.5 * jnp.tanh(0.5 * x) + 0.5


def _gelu_tanh(x):
    c = math.sqrt(2.0 / math.pi)
    return x * (0.5 * (1.0 + jnp.tanh(c * (x + 0.044715 * (x * x * x)))))


def _rglru_body(x_ref, gr_ref, cw_ref, cb_ref, wa_ref, wx_ref, ba_ref, bx_ref, lam_ref, y_ref,
                xpad_ref, a_ref, u_ref, as_ref, us_ref, h_ref, *, s):
    nt = s // RG_TILE
    zeros = jnp.zeros((RG_PAD, LANES), F32)
    xpad_ref[pl.ds(0, RG_PAD), :] = zeros
    xpad_ref[pl.ds(RG_PAD + s, RG_PAD), :] = zeros

    def copy_in(t, carry):
        r0 = pl.multiple_of(t * RG_TILE, RG_TILE)
        xpad_ref[pl.ds(RG_PAD + r0, RG_TILE), :] = x_ref[0, pl.ds(r0, RG_TILE), :]
        return carry

    lax.fori_loop(0, nt, copy_in, 0)

    cw = cw_ref[...]
    cb = cb_ref[...]
    sp = [RGLRU_C * _softplus(-lam_ref[d:d + 1, :]) for d in range(2)]

    def gates(t, carry):
        r0 = pl.multiple_of(t * RG_TILE, RG_TILE)
        xc = None
        for j in range(CONV_WIDTH):
            tap = xpad_ref[pl.ds(r0 + RG_PAD - CONV_LEFT + j, RG_TILE), :] * cw[j:j + 1, :]
            xc = tap if xc is None else xc + tap
        xc = xc + cb
        xcb = xc.astype(BF16)
        for d in range(2):
            r = _sigmoid(jnp.dot(xcb, wa_ref[d, 0], preferred_element_type=F32)
                         + ba_ref[d:d + 1, :])
            i = _sigmoid(jnp.dot(xcb, wx_ref[d, 0], preferred_element_type=F32)
                         + bx_ref[d:d + 1, :])
            neg_log_a = sp[d] * r
            a = jnp.exp(-neg_log_a)
            one_minus_a2 = jnp.tanh(neg_log_a) * (1.0 + a * a)
            root = jnp.where(one_minus_a2 > 0.0, one_minus_a2 * lax.rsqrt(one_minus_a2), 0.0)
            a_ref[d, pl.ds(r0, RG_TILE), :] = a
            u_ref[d, pl.ds(r0, RG_TILE), :] = root * (i * xc)
        return carry

    lax.fori_loop(0, nt, gates, 0)

    def block_scan(t, carry):
        r0 = pl.multiple_of(t * RG_TILE, RG_TILE)
        for d in range(2):
            a_prev = u_prev = None
            for r in (range(SUBLANES) if d == 0 else reversed(range(SUBLANES))):
                rows = pl.ds(r0 + r, RG_TILE // SUBLANES, stride=SUBLANES)
                a, u = a_ref[d, rows, :], u_ref[d, rows, :]
                if a_prev is not None:
                    u = a * u_prev + u
                    a = a * a_prev
                as_ref[d, rows, :] = a
                us_ref[d, rows, :] = u
                a_prev, u_prev = a, u
        return carry

    lax.fori_loop(0, nt, block_scan, 0)

    nblk = s // SUBLANES

    def scan(j, carry):
        h_f, h_b = carry
        for d, last, h in ((0, SUBLANES - 1, h_f), (1, 0, h_b)):
            blk = j if d == 0 else nblk - 1 - j
            rows = pl.ds(pl.multiple_of(blk * SUBLANES, SUBLANES), SUBLANES)
            a, u = as_ref[d, rows, :], us_ref[d, rows, :]
            h_ref[d, rows, :] = a * h + u
            a_end = jnp.broadcast_to(a[last:last + 1, :], (SUBLANES, LANES))
            u_end = jnp.broadcast_to(u[last:last + 1, :], (SUBLANES, LANES))
            if d == 0:
                h_f = a_end * h + u_end
            else:
                h_b = a_end * h + u_end
        return h_f, h_b

    h0 = jnp.zeros((SUBLANES, LANES), F32)
    lax.fori_loop(0, nblk, scan, (h0, h0), unroll=RG_SCAN_UNROLL)

    def finish(t, carry):
        r0 = pl.multiple_of(t * RG_TILE, RG_TILE)
        hr = h_ref[0, pl.ds(r0, RG_TILE), :] + h_ref[1, pl.ds(r0, RG_TILE), :]
        y_ref[0, pl.ds(r0, RG_TILE), :] = (
            hr * _gelu_tanh(gr_ref[0, pl.ds(r0, RG_TILE), :])).astype(y_ref.dtype)
        return carry

    lax.fori_loop(0, nt, finish, 0)


def rglru_mixer(z, xr_block0, gr_block0, conv_w, conv_b, wa, wx, ba, bx, lam):
    b, s, _ = z.shape
    width = RNN_BLOCKS * RNN_BLOCK
    assert s % RG_TILE == 0
    return pl.pallas_call(
        functools.partial(_rglru_body, s=s),
        grid=(b, RNN_BLOCKS),
        in_specs=[
            pl.BlockSpec((1, s, RNN_BLOCK), lambda i, j: (i, 0, xr_block0 + j)),
            pl.BlockSpec((1, s, RNN_BLOCK), lambda i, j: (i, 0, gr_block0 + j)),
            pl.BlockSpec((CONV_WIDTH, RNN_BLOCK), lambda i, j: (0, j)),
            pl.BlockSpec((1, RNN_BLOCK), lambda i, j: (0, j)),
            pl.BlockSpec((2, 1, RNN_BLOCK, RNN_BLOCK), lambda i, j: (0, j, 0, 0)),
            pl.BlockSpec((2, 1, RNN_BLOCK, RNN_BLOCK), lambda i, j: (0, j, 0, 0)),
            pl.BlockSpec((2, RNN_BLOCK), lambda i, j: (0, j)),
            pl.BlockSpec((2, RNN_BLOCK), lambda i, j: (0, j)),
            pl.BlockSpec((2, RNN_BLOCK), lambda i, j: (0, j)),
        ],
        out_specs=pl.BlockSpec((1, s, RNN_BLOCK), lambda i, j: (i, 0, j)),
        out_shape=jax.ShapeDtypeStruct((b, s, width), BF16),
        scratch_shapes=[
            pltpu.VMEM((s + 2 * RG_PAD, RNN_BLOCK), F32),
            pltpu.VMEM((2, s, RNN_BLOCK), F32),
            pltpu.VMEM((2, s, RNN_BLOCK), F32),
            pltpu.VMEM((2, s, RNN_BLOCK), F32),
            pltpu.VMEM((2, s, RNN_BLOCK), F32),
            pltpu.VMEM((2, s, RNN_BLOCK), F32),
        ],
        compiler_params=_cparams("parallel", "parallel"),
        name="rglru",
    )(z, z, conv_w, conv_b, wa, wx, ba, bx, lam)


ATTN_TQ = 128
ATTN_TK = ATTN_TQ + 2 * ATTN_HALF
ATTN_UNROLL = 32
LOG2E = math.log2(math.e)


def _attn_body(q_ref, k_ref, v_ref, slope_ref, o_ref, bias_ref, x4_ref, og_ref, lse_ref, *, s):
    qscale = ATTN_DH ** -0.5 * LOG2E
    slope = slope_ref[0, 0:1, 0:1] * LOG2E
    qi = lax.broadcasted_iota(jnp.int32, (ATTN_TQ, ATTN_TK), 0)
    kj = lax.broadcasted_iota(jnp.int32, (ATTN_TQ, ATTN_TK), 1)
    for g, dil in enumerate(ATTN_DILATIONS):
        for e in range(3):
            rel = jnp.abs(kj - e * ATTN_HALF - qi)
            pen = slope * (rel * dil).astype(F32)
            bias_ref[g, e] = jnp.where(rel <= ATTN_HALF, -pen, NEG_BIG)

    s4 = s // 4
    srcs = (q_ref, k_ref, v_ref)

    def split4(t, carry):
        c = t // (s4 // ATTN_TK)
        p0 = (t % (s4 // ATTN_TK)) * ATTN_TK
        dst = pl.ds(pl.multiple_of(c * s4 + p0, ATTN_TK), ATTN_TK)
        for a in range(3):
            x = srcs[a][0, pl.ds(c + 4 * p0, ATTN_TK, stride=4), :]
            x4_ref[a, dst, :] = x * qscale if a == 0 else x
        return carry

    lax.fori_loop(0, 4 * (s4 // ATTN_TK), split4, 0)

    ones = jnp.ones((ATTN_TK, ATTN_DH), BF16)

    for g, dil in reversed(list(enumerate(ATTN_DILATIONS))):
        sp = s // dil
        nqb = sp // ATTN_TQ

        def q_block(t, carry, g=g, dil=dil, sp=sp, nqb=nqb):
            r = t // nqb
            p0 = (t % nqb) * ATTN_TQ
            kstart = jnp.clip(p0 - ATTN_HALF, 0, sp - ATTN_TK)
            e = (p0 - kstart) // ATTN_HALF
            if dil == 1:
                qrows = pl.ds(pl.multiple_of(p0, ATTN_TQ), ATTN_TQ)
                krows = pl.ds(pl.multiple_of(kstart, ATTN_HALF), ATTN_TK)
                qf = q_ref[0, qrows, :] * qscale
                kf, vf = k_ref[0, krows, :], v_ref[0, krows, :]
                orows = qrows
            elif dil == 4:
                qrows = pl.ds(pl.multiple_of(r * s4 + p0, ATTN_TQ), ATTN_TQ)
                krows = pl.ds(pl.multiple_of(r * s4 + kstart, ATTN_HALF), ATTN_TK)
                qf, kf, vf = x4_ref[0, qrows, :], x4_ref[1, krows, :], x4_ref[2, krows, :]
                orows = pl.ds(r + 4 * p0, ATTN_TQ, stride=4)
            else:
                base = (r % 4) * s4 + r // 4
                qrows = pl.ds(base + 4 * p0, ATTN_TQ, stride=4)
                krows = pl.ds(base + 4 * kstart, ATTN_TK, stride=4)
                qf, kf, vf = x4_ref[0, qrows, :], x4_ref[1, krows, :], x4_ref[2, krows, :]
                orows = pl.ds(r + dil * p0, ATTN_TQ, stride=dil)
            qb, kb, vb = qf.astype(BF16), kf.astype(BF16), vf.astype(BF16)
            sc = lax.dot_general(qb, kb, (((1,), (1,)), ((), ())),
                                 preferred_element_type=F32) + bias_ref[g, e]
            m = jnp.max(sc, axis=-1, keepdims=True)
            p = jnp.exp2(sc - m).astype(BF16)
            pv = jnp.dot(p, jnp.concatenate([vb, ones], axis=1), preferred_element_type=F32)
            acc, den = pv[:, :ATTN_DH], pv[:, ATTN_DH:]
            if dil != 1:
                og_ref[g - 1, orows, :] = acc / den
                lse_ref[g - 1, orows, :] = m + jnp.log2(den)
                return carry
            l4, l16 = lse_ref[0, orows, :], lse_ref[1, orows, :]
            mx = jnp.maximum(jnp.maximum(l4, l16), m)
            e1, e4, e16 = jnp.exp2(m - mx), jnp.exp2(l4 - mx), jnp.exp2(l16 - mx)
            num = e1 * acc + e4 * og_ref[0, orows, :] + e16 * og_ref[1, orows, :]
            o_ref[0, orows, :] = (num * (1.0 / (e1 * den + e4 + e16))).astype(o_ref.dtype)
            return carry

        lax.fori_loop(0, dil * nqb, q_block, 0, unroll=ATTN_UNROLL)


def dilated_attention(qkv, slopes):
    b, s, _ = qkv.shape
    h, dh = ATTN_HEADS, ATTN_DH
    assert s % (max(ATTN_DILATIONS) * ATTN_TK) == 0
    return pl.pallas_call(
        functools.partial(_attn_body, s=s),
        grid=(b, h),
        in_specs=[
            pl.BlockSpec((1, s, dh), lambda i, j: (i, 0, j)),
            pl.BlockSpec((1, s, dh), lambda i, j: (i, 0, h + j)),
            pl.BlockSpec((1, s, dh), lambda i, j: (i, 0, 2 * h + j)),
            pl.BlockSpec((1, SUBLANES, LANES), lambda i, j: (j, 0, 0)),
        ],
        out_specs=pl.BlockSpec((1, s, dh), lambda i, j: (i, 0, j)),
        out_shape=jax.ShapeDtypeStruct((b, s, h * dh), BF16),
        scratch_shapes=[
            pltpu.VMEM((len(ATTN_DILATIONS), 3, ATTN_TQ, ATTN_TK), F32),
            pltpu.VMEM((3, s, dh), F32),
            pltpu.VMEM((len(ATTN_DILATIONS) - 1, s, dh), F32),
            pltpu.VMEM((len(ATTN_DILATIONS) - 1, s, dh), F32),
        ],
        compiler_params=_cparams("parallel", "parallel"),
        name="dilated_attn",
    )(qkv, qkv, qkv, slopes)


MIXER_OUT_TM = 512


def _ffn(x, h, w1, w3, w2):
    u = swiglu_up(h, w1.astype(BF16), w3.astype(BF16))
    return matmul([(u, w2.astype(BF16))], res=x, tn=512, name="ffn_down")


def _even_layer(x, bsz, seq, norm_g, w_in, gate_b, conv_w, conv_b, rg_wa, rg_ba, rg_wx, rg_bx,
                rg_lam, head_g, w_out, next_norm_g):
    heads, dk, dv = MLSTM_HEADS, MLSTM_DK, MLSTM_DV
    n_qkvo = 2 * heads * dk + 2 * heads * dv
    n_gate = 4 * heads
    width = RNN_BLOCKS * RNN_BLOCK
    w_main = jnp.concatenate([w_in[:, :n_qkvo], w_in[:, n_qkvo + n_gate:]], axis=1).astype(BF16)
    w_gate = jnp.pad(w_in[:, n_qkvo:n_qkvo + n_gate], ((0, 0), (0, LANES - n_gate))).astype(BF16)
    z, zg = norm_matmul(x, norm_g, w_main, w_side=w_gate, name="in_proj")
    z = z.reshape(bsz, seq, -1)

    chunk = MLSTM_CHUNK
    nc = seq // chunk
    gates_rows = jnp.transpose(zg[:, :n_gate].reshape(bsz, nc, chunk, 4, heads), (0, 4, 3, 1, 2))
    gate_b_rows = jnp.broadcast_to(
        jnp.transpose(gate_b.astype(F32).reshape(4, heads))[:, :, None, None], (heads, 4, 1, chunk))
    y_a = mlstm_mixer(z, gates_rows, gate_b_rows, head_g.astype(F32).reshape(heads, 1, dv), chunk)

    y_b = rglru_mixer(z, n_qkvo // RNN_BLOCK, (n_qkvo + width) // RNN_BLOCK,
                      conv_w.astype(F32), conv_b.astype(F32).reshape(1, width),
                      rg_wa.astype(BF16), rg_wx.astype(BF16),
                      rg_ba.astype(F32), rg_bx.astype(F32), rg_lam.astype(F32))

    w_out_b = w_out.astype(BF16)
    m = bsz * seq
    return matmul([(y_a.reshape(m, -1), w_out_b[:heads * dv]),
                   (y_b.reshape(m, -1), w_out_b[heads * dv:])], res=x, norm_g=next_norm_g,
                  tm=MIXER_OUT_TM, tn=w_out.shape[1], name="out_proj")


def _odd_layer(x, bsz, seq, norm_g, w_qkv, w_o, next_norm_g):
    qkv = norm_matmul(x, norm_g, w_qkv.astype(BF16), name="qkv_proj").reshape(bsz, seq, -1)
    slopes = jnp.exp2(-ALIBI_MAX_BIAS * jnp.arange(1, ATTN_HEADS + 1, dtype=F32) / ATTN_HEADS)
    slopes = jnp.broadcast_to(slopes[:, None, None], (ATTN_HEADS, SUBLANES, LANES))
    o = dilated_attention(qkv, slopes)
    return matmul([(o.reshape(bsz * seq, -1), w_o.astype(BF16))], res=x, norm_g=next_norm_g,
                  tm=MIXER_OUT_TM, tn=w_o.shape[1], name="attn_out_proj")


def kernel(x, e_norm, e_w_in, e_gate_b, e_conv_w, e_conv_b, e_rg_wa, e_rg_ba, e_rg_wx, e_rg_bx,
           e_rg_lam, e_head_g, e_w_out, o_norm, o_w_qkv, o_w_o, f_norm, f_w1, f_w3, f_w2,
           final_norm):
    bsz, seq, d = x.shape
    depth = f_norm.shape[0]
    xs = x.reshape(bsz * seq, d).astype(F32)
    for l in range(depth):
        if l % 2 == 0:
            e = l // 2
            xs, hs = _even_layer(xs, bsz, seq, e_norm[e], e_w_in[e], e_gate_b[e], e_conv_w[e],
                                 e_conv_b[e], e_rg_wa[e], e_rg_ba[e], e_rg_wx[e], e_rg_bx[e],
                                 e_rg_lam[e], e_head_g[e], e_w_out[e], f_norm[l])
        else:
            o = l // 2
            xs, hs = _odd_layer(xs, bsz, seq, o_norm[o], o_w_qkv[o], o_w_o[o], f_norm[l])
        xs = _ffn(xs, hs, f_w1[l], f_w3[l], f_w2[l])
    return rmsnorm(xs, final_norm, x.dtype).reshape(bsz, seq, d)
```

```python
import functools
import math

import jax
import jax.numpy as jnp
from jax import lax
from jax.experimental import pallas as pl
from jax.experimental.pallas import tpu as pltpu

F32 = jnp.float32
BF16 = jnp.bfloat16

RMS_EPS = 1e-6
NEG_BIG = -1e30

MLSTM_HEADS = 4
MLSTM_DK = 128
MLSTM_DV = 256
MLSTM_CHUNK = 256
RNN_BLOCKS = 8
RNN_BLOCK = 128
CONV_WIDTH = 4
CONV_LEFT = 2
RGLRU_C = 8.0
ATTN_HEADS = 16
ATTN_DH = 128
ATTN_HALF = 64
ATTN_DILATIONS = (1, 4, 16)
ALIBI_MAX_BIAS = 8.0

LANES = 128
SUBLANES = 8
VMEM_LIMIT_BYTES = 56 * 1024 * 1024


def _cparams(*semantics):
    return pltpu.CompilerParams(dimension_semantics=semantics,
                                vmem_limit_bytes=VMEM_LIMIT_BYTES)


NORM_ROWS = 128


def _norm_rows_into(x_ref, g_ref, h_ref):
    g = g_ref[...]

    def rows(t, carry):
        sl = pl.ds(pl.multiple_of(t * NORM_ROWS, NORM_ROWS), NORM_ROWS)
        x = x_ref[sl, :]
        ms = jnp.mean(x * x, axis=-1, keepdims=True)
        h_ref[sl, :] = (x * lax.rsqrt(ms + RMS_EPS) * g).astype(h_ref.dtype)
        return carry

    lax.fori_loop(0, x_ref.shape[0] // NORM_ROWS, rows, 0)


def _matmul_body(*refs, n_pairs, has_res, has_norm, keep_sum):
    pairs = [(refs[2 * p], refs[2 * p + 1]) for p in range(n_pairs)]
    rest = list(refs[2 * n_pairs:])
    r_ref = rest.pop(0) if has_res else None
    g_ref = rest.pop(0) if has_norm else None
    y_ref = rest.pop(0) if keep_sum else rest.pop(-1)

    acc = None
    for a_ref, w_ref in pairs:
        d = jnp.dot(a_ref[...], w_ref[...], preferred_element_type=F32)
        acc = d if acc is None else acc + d
    if has_res:
        acc = acc + r_ref[...]
    y_ref[...] = acc.astype(y_ref.dtype)
    if has_norm:
        _norm_rows_into(y_ref, g_ref, rest.pop(0))


def matmul(pairs, res=None, norm_g=None, norm_dtype=BF16, keep_sum=True, tm=1024, tn=1024,
           name="matmul"):
    m = pairs[0][0].shape[0]
    n = pairs[0][1].shape[1]
    tn = min(tn, n)
    assert m % tm == 0 and n % tn == 0 and (keep_sum or norm_g is not None)
    assert norm_g is None or (tn == n and tm % NORM_ROWS == 0)
    w_mode = dict(pipeline_mode=pl.Buffered(1)) if tn == n else {}
    in_specs, args = [], []
    for a, w in pairs:
        kdim = a.shape[1]
        in_specs += [pl.BlockSpec((tm, kdim), lambda i, j: (i, 0)),
                     pl.BlockSpec((kdim, tn), lambda i, j: (0, j), **w_mode)]
        args += [a, w]
    if res is not None:
        in_specs.append(pl.BlockSpec((tm, tn), lambda i, j: (i, j)))
        args.append(res)
    out_specs, out_shape, scratch = [], [], []
    if keep_sum:
        out_specs.append(pl.BlockSpec((tm, tn), lambda i, j: (i, j)))
        out_shape.append(jax.ShapeDtypeStruct((m, n), F32))
    else:
        scratch.append(pltpu.VMEM((tm, n), F32))
    if norm_g is not None:
        in_specs.append(pl.BlockSpec((1, n), lambda i, j: (0, 0)))
        args.append(norm_g.reshape(1, n).astype(F32))
        out_specs.append(pl.BlockSpec((tm, n), lambda i, j: (i, 0)))
        out_shape.append(jax.ShapeDtypeStruct((m, n), norm_dtype))
    outs = pl.pallas_call(
        functools.partial(_matmul_body, n_pairs=len(pairs), has_res=res is not None,
                          has_norm=norm_g is not None, keep_sum=keep_sum),
        grid=(m // tm, n // tn),
        in_specs=in_specs,
        out_specs=out_specs,
        out_shape=out_shape,
        scratch_shapes=scratch,
        compiler_params=_cparams("parallel", "parallel"),
        name=name,
    )(*args)
    return outs if len(outs) > 1 else outs[0]


SWIGLU_ROWS = 1024


def _swiglu_up_body(a_ref, w1_ref, w3_ref, o_ref):
    rows = min(SWIGLU_ROWS, a_ref.shape[0])
    for r0 in range(0, a_ref.shape[0], rows):
        a = a_ref[pl.ds(r0, rows), :]
        u = jnp.dot(a, w1_ref[...], preferred_element_type=F32)
        v = jnp.dot(a, w3_ref[...], preferred_element_type=F32)
        o_ref[pl.ds(r0, rows), :] = (u * jax.nn.sigmoid(u) * v).astype(o_ref.dtype)


def swiglu_up(a, w1, w3, tm=2048, tn=512):
    m, kdim = a.shape
    n = w1.shape[1]
    assert m % tm == 0 and n % tn == 0
    return pl.pallas_call(
        _swiglu_up_body,
        grid=(m // tm, n // tn),
        in_specs=[pl.BlockSpec((tm, kdim), lambda i, j: (i, 0)),
                  pl.BlockSpec((kdim, tn), lambda i, j: (0, j)),
                  pl.BlockSpec((kdim, tn), lambda i, j: (0, j))],
        out_specs=pl.BlockSpec((tm, tn), lambda i, j: (i, j)),
        out_shape=jax.ShapeDtypeStruct((m, n), BF16),
        compiler_params=_cparams("parallel", "parallel"),
        name="swiglu_up",
    )(a, w1, w3)


def _norm_matmul_body(*refs, has_side):
    if has_side:
        x_ref, g_ref, w_ref, ws_ref, o_ref, os_ref, h_ref = refs
    else:
        x_ref, g_ref, w_ref, o_ref, h_ref = refs

    @pl.when(pl.program_id(1) == 0)
    def _():
        _norm_rows_into(x_ref, g_ref, h_ref)
        if has_side:
            os_ref[...] = jnp.dot(h_ref[...], ws_ref[...], preferred_element_type=F32)

    o_ref[...] = jnp.dot(h_ref[...], w_ref[...], preferred_element_type=F32).astype(o_ref.dtype)


def norm_matmul(x, g, w, w_side=None, out_dtype=F32, tm=1024, tn=1024, name="norm_matmul"):
    m, d = x.shape
    n = w.shape[1]
    assert m % tm == 0 and n % tn == 0 and tm % NORM_ROWS == 0
    in_specs = [pl.BlockSpec((tm, d), lambda i, j: (i, 0)),
                pl.BlockSpec((1, d), lambda i, j: (0, 0)),
                pl.BlockSpec((d, tn), lambda i, j: (0, j))]
    args = [x, g.reshape(1, d).astype(F32), w]
    out_specs = [pl.BlockSpec((tm, tn), lambda i, j: (i, j))]
    out_shape = [jax.ShapeDtypeStruct((m, n), out_dtype)]
    if w_side is not None:
        ns = w_side.shape[1]
        in_specs.append(pl.BlockSpec((d, ns), lambda i, j: (0, 0)))
        args.append(w_side)
        out_specs.append(pl.BlockSpec((tm, ns), lambda i, j: (i, 0)))
        out_shape.append(jax.ShapeDtypeStruct((m, ns), F32))
    outs = pl.pallas_call(
        functools.partial(_norm_matmul_body, has_side=w_side is not None),
        grid=(m // tm, n // tn),
        in_specs=in_specs,
        out_specs=out_specs,
        out_shape=out_shape,
        scratch_shapes=[pltpu.VMEM((tm, d), BF16)],
        compiler_params=_cparams("parallel", "arbitrary"),
        name=name,
    )(*args)
    return outs if w_side is not None else outs[0]


def _log_sigmoid(x):
    return jnp.minimum(x, 0.0) - jnp.log1p(jnp.exp(-jnp.abs(x)))


def _lane_scan(x, combine, identity, reverse):
    width = x.shape[-1]
    lane = lax.broadcasted_iota(jnp.int32, x.shape, 1)
    sh = 1
    while sh < width:
        if reverse:
            shifted = jnp.where(lane < width - sh, pltpu.roll(x, width - sh, axis=1), identity)
        else:
            shifted = jnp.where(lane >= sh, pltpu.roll(x, sh, axis=1), identity)
        x = combine(x, shifted)
        sh *= 2
    return x


def _mlstm_body(q_ref, k_ref, v_ref, og_ref, g_ref, gb_ref, hg_ref, y_ref,
                rows_ref, cols_ref, hf_ref, hb_ref, ct_ref, *, nc, chunk):
    dk, dv = MLSTM_DK, MLSTM_DV
    scale = dk ** -0.5

    g = g_ref[0, 0] + gb_ref[0]
    rows = []
    for d in range(2):
        b = _lane_scan(_log_sigmoid(g[2 * d + 1]), jnp.add, 0.0, reverse=d == 1)
        gg = g[2 * d] - b
        gmax = _lane_scan(gg, jnp.maximum, -jnp.inf, reverse=d == 1)
        rows += [gg, b, gmax]
        for j, r in enumerate((gg, b, gmax)):
            rows_ref[3 * d + j] = r
    rowmat = jnp.concatenate(rows + [jnp.zeros((LANES - 6 * nc, chunk), F32)], axis=0)
    colmat = rowmat.T
    for c in range(nc):
        cols_ref[c] = colmat if c == 0 else pltpu.roll(colmat, LANES - c, axis=1)

    ct_ref[...] = jnp.zeros_like(ct_ref)

    row_id = lax.broadcasted_iota(jnp.int32, (chunk, chunk), 0)
    col_id = lax.broadcasted_iota(jnp.int32, (chunk, chunk), 1)
    causal = (row_id >= col_id, row_id <= col_id)
    ones = jnp.ones((chunk, LANES), BF16)

    def lanes2(x):
        return jnp.concatenate([x] * (dv // LANES), axis=1)

    def chunk_step(c, m, d, h_ref):
        r0 = pl.multiple_of(c * chunk, chunk)
        qb = (q_ref[0, pl.ds(r0, chunk), :] * scale).astype(BF16)
        kb = k_ref[0, pl.ds(r0, chunk), :].astype(BF16)
        vf = v_ref[0, pl.ds(r0, chunk), :]
        cols = cols_ref[c]

        def column(j):
            lane = (3 * d + j) * nc
            return jnp.broadcast_to(cols[:, lane:lane + 1], (chunk, LANES))

        gg_col, b_col, gmax_col = column(0), column(1), column(2)
        gg_row = rows_ref[3 * d, pl.ds(c, 1), :]
        end = chunk - 1 if d == 0 else 0
        total = rows_ref[3 * d + 1, pl.ds(c, 1), :][:, end:end + 1]
        gmax_end = rows_ref[3 * d + 2, pl.ds(c, 1), :][:, end:end + 1]

        m_row = jnp.maximum(gmax_col, m)
        w_intra = jnp.exp(jnp.where(causal[d], gg_row - lanes2(m_row), NEG_BIG))
        w_inter = jnp.exp(m - m_row)
        s = lax.dot_general(qb, kb, (((1,), (1,)), ((), ())),
                            preferred_element_type=F32) * w_intra
        ct = ct_ref[d]
        intra = jnp.dot(s.astype(BF16), jnp.concatenate([vf.astype(BF16), ones], axis=1),
                        preferred_element_type=F32)
        inter = jnp.dot(qb, ct.astype(BF16), preferred_element_type=F32)
        num = intra[:, :dv] + lanes2(w_inter) * inter[:, :dv]
        den = intra[:, dv:] + w_inter * inter[:, dv:]
        inv = 1.0 / jnp.maximum(jnp.abs(den), jnp.exp(-(b_col + m_row)))
        h_ref[pl.ds(r0, chunk), :] = num * lanes2(inv)

        m_new = jnp.maximum(total + m, total + gmax_end)
        w_src = jnp.exp(total + gg_col - m_new)
        decay = jnp.exp(total + m - m_new)
        wv = jnp.concatenate([lanes2(w_src) * vf, w_src], axis=1).astype(BF16)
        ct_ref[d] = decay * ct + lax.dot_general(
            kb, wv, (((0,), (0,)), ((), ())), preferred_element_type=F32)
        return m_new

    def body(c, carry):
        m_f, m_b = carry
        m_f = chunk_step(c, m_f, 0, hf_ref)
        m_b = chunk_step(nc - 1 - c, m_b, 1, hb_ref)
        return m_f, m_b

    m0 = jnp.full((1, 1), NEG_BIG, F32)
    lax.fori_loop(0, nc, body, (m0, m0), unroll=2)

    hg = hg_ref[0]

    def finish(c, carry):
        r0 = pl.multiple_of(c * chunk, chunk)
        hm = hf_ref[pl.ds(r0, chunk), :] + hb_ref[pl.ds(r0, chunk), :]
        ms = jnp.mean(hm * hm, axis=-1, keepdims=True)
        hn = hm * lax.rsqrt(ms + RMS_EPS) * hg
        gate = jax.nn.sigmoid(og_ref[0, pl.ds(r0, chunk), :])
        y_ref[0, pl.ds(r0, chunk), :] = (hn * gate).astype(y_ref.dtype)
        return carry

    lax.fori_loop(0, nc, finish, 0)


def mlstm_mixer(z, gates_rows, gate_b_rows, head_g, chunk=MLSTM_CHUNK):
    b, s, _ = z.shape
    h, dk, dv = MLSTM_HEADS, MLSTM_DK, MLSTM_DV
    nc = s // chunk
    assert s % chunk == 0 and 6 * nc <= LANES and chunk % LANES == 0
    kq = h * dk // dk
    kv = 2 * h * dk // dv
    ko = kv + h
    return pl.pallas_call(
        functools.partial(_mlstm_body, nc=nc, chunk=chunk),
        grid=(b, h),
        in_specs=[
            pl.BlockSpec((1, s, dk), lambda i, j: (i, 0, j)),
            pl.BlockSpec((1, s, dk), lambda i, j: (i, 0, kq + j)),
            pl.BlockSpec((1, s, dv), lambda i, j: (i, 0, kv + j)),
            pl.BlockSpec((1, s, dv), lambda i, j: (i, 0, ko + j)),
            pl.BlockSpec((1, 1, 4, nc, chunk), lambda i, j: (i, j, 0, 0, 0)),
            pl.BlockSpec((1, 4, 1, chunk), lambda i, j: (j, 0, 0, 0)),
            pl.BlockSpec((1, 1, dv), lambda i, j: (j, 0, 0)),
        ],
        out_specs=pl.BlockSpec((1, s, dv), lambda i, j: (i, 0, j)),
        out_shape=jax.ShapeDtypeStruct((b, s, h * dv), BF16),
        scratch_shapes=[
            pltpu.VMEM((6, nc, chunk), F32),
            pltpu.VMEM((nc, chunk, LANES), F32),
            pltpu.VMEM((s, dv), F32),
            pltpu.VMEM((s, dv), F32),
            pltpu.VMEM((2, dk, dv + LANES), F32),
        ],
        compiler_params=_cparams("parallel", "parallel"),
        name="mlstm",
    )(z, z, z, z, gates_rows, gate_b_rows, head_g)


RG_TILE = 512
RG_PAD = SUBLANES
RG_SCAN_UNROLL = 8


def _softplus(x):
    return jnp.maximum(x, 0.0) + jnp.log1p(jnp.exp(-jnp.abs(x)))


def _sigmoid(x):
    return 0.5 * jnp.tanh(0.5 * x) + 0.5


def _gelu_tanh(x):
    c = math.sqrt(2.0 / math.pi)
    return x * (0.5 * (1.0 + jnp.tanh(c * (x + 0.044715 * (x * x * x)))))


def _rglru_body(x_ref, gr_ref, cw_ref, cb_ref, wa_ref, wx_ref, ba_ref, bx_ref, lam_ref, y_ref,
                xpad_ref, a_ref, u_ref, as_ref, us_ref, h_ref, *, s):
    nt = s // RG_TILE
    zeros = jnp.zeros((RG_PAD, LANES), F32)
    xpad_ref[pl.ds(0, RG_PAD), :] = zeros
    xpad_ref[pl.ds(RG_PAD + s, RG_PAD), :] = zeros

    def copy_in(t, carry):
        r0 = pl.multiple_of(t * RG_TILE, RG_TILE)
        xpad_ref[pl.ds(RG_PAD + r0, RG_TILE), :] = x_ref[0, pl.ds(r0, RG_TILE), :]
        return carry

    lax.fori_loop(0, nt, copy_in, 0)

    cw = cw_ref[...]
    cb = cb_ref[...]
    sp = [RGLRU_C * _softplus(-lam_ref[d:d + 1, :]) for d in range(2)]

    def gates(t, carry):
        r0 = pl.multiple_of(t * RG_TILE, RG_TILE)
        xc = None
        for j in range(CONV_WIDTH):
            tap = xpad_ref[pl.ds(r0 + RG_PAD - CONV_LEFT + j, RG_TILE), :] * cw[j:j + 1, :]
            xc = tap if xc is None else xc + tap
        xc = xc + cb
        xcb = xc.astype(BF16)
        for d in range(2):
            r = _sigmoid(jnp.dot(xcb, wa_ref[d, 0], preferred_element_type=F32)
                         + ba_ref[d:d + 1, :])
            i = _sigmoid(jnp.dot(xcb, wx_ref[d, 0], preferred_element_type=F32)
                         + bx_ref[d:d + 1, :])
            neg_log_a = sp[d] * r
            a = jnp.exp(-neg_log_a)
            one_minus_a2 = jnp.tanh(neg_log_a) * (1.0 + a * a)
            root = jnp.where(one_minus_a2 > 0.0, one_minus_a2 * lax.rsqrt(one_minus_a2), 0.0)
            a_ref[d, pl.ds(r0, RG_TILE), :] = a
            u_ref[d, pl.ds(r0, RG_TILE), :] = root * (i * xc)
        return carry

    lax.fori_loop(0, nt, gates, 0)

    def block_scan(t, carry):
        r0 = pl.multiple_of(t * RG_TILE, RG_TILE)
        for d in range(2):
            a_prev = u_prev = None
            for r in (range(SUBLANES) if d == 0 else reversed(range(SUBLANES))):
                rows = pl.ds(r0 + r, RG_TILE // SUBLANES, stride=SUBLANES)
                a, u = a_ref[d, rows, :], u_ref[d, rows, :]
                if a_prev is not None:
                    u = a * u_prev + u
                    a = a * a_prev
                as_ref[d, rows, :] = a
                us_ref[d, rows, :] = u
                a_prev, u_prev = a, u
        return carry

    lax.fori_loop(0, nt, block_scan, 0)

    nblk = s // SUBLANES

    def scan(j, carry):
        h_f, h_b = carry
        for d, last, h in ((0, SUBLANES - 1, h_f), (1, 0, h_b)):
            blk = j if d == 0 else nblk - 1 - j
            rows = pl.ds(pl.multiple_of(blk * SUBLANES, SUBLANES), SUBLANES)
            a, u = as_ref[d, rows, :], us_ref[d, rows, :]
            h_ref[d, rows, :] = a * h + u
            a_end = jnp.broadcast_to(a[last:last + 1, :], (SUBLANES, LANES))
            u_end = jnp.broadcast_to(u[last:last + 1, :], (SUBLANES, LANES))
            if d == 0:
                h_f = a_end * h + u_end
            else:
                h_b = a_end * h + u_end
        return h_f, h_b

    h0 = jnp.zeros((SUBLANES, LANES), F32)
    lax.fori_loop(0, nblk, scan, (h0, h0), unroll=RG_SCAN_UNROLL)

    def finish(t, carry):
        r0 = pl.multiple_of(t * RG_TILE, RG_TILE)
        hr = h_ref[0, pl.ds(r0, RG_TILE), :] + h_ref[1, pl.ds(r0, RG_TILE), :]
        y_ref[0, pl.ds(r0, RG_TILE), :] = (
            hr * _gelu_tanh(gr_ref[0, pl.ds(r0, RG_TILE), :])).astype(y_ref.dtype)
        return carry

    lax.fori_loop(0, nt, finish, 0)


def rglru_mixer(z, xr_block0, gr_block0, conv_w, conv_b, wa, wx, ba, bx, lam):
    b, s, _ = z.shape
    width = RNN_BLOCKS * RNN_BLOCK
    assert s % RG_TILE == 0
    return pl.pallas_call(
        functools.partial(_rglru_body, s=s),
        grid=(b, RNN_BLOCKS),
        in_specs=[
            pl.BlockSpec((1, s, RNN_BLOCK), lambda i, j: (i, 0, xr_block0 + j)),
            pl.BlockSpec((1, s, RNN_BLOCK), lambda i, j: (i, 0, gr_block0 + j)),
            pl.BlockSpec((CONV_WIDTH, RNN_BLOCK), lambda i, j: (0, j)),
            pl.BlockSpec((1, RNN_BLOCK), lambda i, j: (0, j)),
            pl.BlockSpec((2, 1, RNN_BLOCK, RNN_BLOCK), lambda i, j: (0, j, 0, 0)),
            pl.BlockSpec((2, 1, RNN_BLOCK, RNN_BLOCK), lambda i, j: (0, j, 0, 0)),
            pl.BlockSpec((2, RNN_BLOCK), lambda i, j: (0, j)),
            pl.BlockSpec((2, RNN_BLOCK), lambda i, j: (0, j)),
            pl.BlockSpec((2, RNN_BLOCK), lambda i, j: (0, j)),
        ],
        out_specs=pl.BlockSpec((1, s, RNN_BLOCK), lambda i, j: (i, 0, j)),
        out_shape=jax.ShapeDtypeStruct((b, s, width), BF16),
        scratch_shapes=[
            pltpu.VMEM((s + 2 * RG_PAD, RNN_BLOCK), F32),
            pltpu.VMEM((2, s, RNN_BLOCK), F32),
            pltpu.VMEM((2, s, RNN_BLOCK), F32),
            pltpu.VMEM((2, s, RNN_BLOCK), F32),
            pltpu.VMEM((2, s, RNN_BLOCK), F32),
            pltpu.VMEM((2, s, RNN_BLOCK), F32),
        ],
        compiler_params=_cparams("parallel", "parallel"),
        name="rglru",
    )(z, z, conv_w, conv_b, wa, wx, ba, bx, lam)


ATTN_TQ = 128
ATTN_TK = ATTN_TQ + 2 * ATTN_HALF
ATTN_UNROLL = 32
LOG2E = math.log2(math.e)


def _attn_body(q_ref, k_ref, v_ref, slope_ref, o_ref, bias_ref, x4_ref, og_ref, lse_ref, *, s):
    qscale = ATTN_DH ** -0.5 * LOG2E
    slope = slope_ref[0, 0:1, 0:1] * LOG2E
    qi = lax.broadcasted_iota(jnp.int32, (ATTN_TQ, ATTN_TK), 0)
    kj = lax.broadcasted_iota(jnp.int32, (ATTN_TQ, ATTN_TK), 1)
    for g, dil in enumerate(ATTN_DILATIONS):
        for e in range(3):
            rel = jnp.abs(kj - e * ATTN_HALF - qi)
            pen = slope * (rel * dil).astype(F32)
            bias_ref[g, e] = jnp.where(rel <= ATTN_HALF, -pen, NEG_BIG)

    s4 = s // 4
    srcs = (q_ref, k_ref, v_ref)

    def split4(t, carry):
        c = t // (s4 // ATTN_TK)
        p0 = (t % (s4 // ATTN_TK)) * ATTN_TK
        dst = pl.ds(pl.multiple_of(c * s4 + p0, ATTN_TK), ATTN_TK)
        for a in range(3):
            x = srcs[a][0, pl.ds(c + 4 * p0, ATTN_TK, stride=4), :]
            x4_ref[a, dst, :] = x * qscale if a == 0 else x
        return carry

    lax.fori_loop(0, 4 * (s4 // ATTN_TK), split4, 0)

    ones = jnp.ones((ATTN_TK, ATTN_DH), BF16)

    for g, dil in reversed(list(enumerate(ATTN_DILATIONS))):
        sp = s // dil
        nqb = sp // ATTN_TQ

        def q_block(t, carry, g=g, dil=dil, sp=sp, nqb=nqb):
            r = t // nqb
            p0 = (t % nqb) * ATTN_TQ
            kstart = jnp.clip(p0 - ATTN_HALF, 0, sp - ATTN_TK)
            e = (p0 - kstart) // ATTN_HALF
            if dil == 1:
                qrows = pl.ds(pl.multiple_of(p0, ATTN_TQ), ATTN_TQ)
                krows = pl.ds(pl.multiple_of(kstart, ATTN_HALF), ATTN_TK)
                qf = q_ref[0, qrows, :] * qscale
                kf, vf = k_ref[0, krows, :], v_ref[0, krows, :]
                orows = qrows
            elif dil == 4:
                qrows = pl.ds(pl.multiple_of(r * s4 + p0, ATTN_TQ), ATTN_TQ)
                krows = pl.ds(pl.multiple_of(r * s4 + kstart, ATTN_HALF), ATTN_TK)
                qf, kf, vf = x4_ref[0, qrows, :], x4_ref[1, krows, :], x4_ref[2, krows, :]
                orows = pl.ds(r + 4 * p0, ATTN_TQ, stride=4)
            else:
                base = (r % 4) * s4 + r // 4
                qrows = pl.ds(base + 4 * p0, ATTN_TQ, stride=4)
                krows = pl.ds(base + 4 * kstart, ATTN_TK, stride=4)
                qf, kf, vf = x4_ref[0, qrows, :], x4_ref[1, krows, :], x4_ref[2, krows, :]
                orows = pl.ds(r + dil * p0, ATTN_TQ, stride=dil)
            qb, kb, vb = qf.astype(BF16), kf.astype(BF16), vf.astype(BF16)
            sc = lax.dot_general(qb, kb, (((1,), (1,)), ((), ())),
                                 preferred_element_type=F32) + bias_ref[g, e]
            m = jnp.max(sc, axis=-1, keepdims=True)
            p = jnp.exp2(sc - m).astype(BF16)
            pv = jnp.dot(p, jnp.concatenate([vb, ones], axis=1), preferred_element_type=F32)
            acc, den = pv[:, :ATTN_DH], pv[:, ATTN_DH:]
            if dil != 1:
                og_ref[g - 1, orows, :] = acc / den
                lse_ref[g - 1, orows, :] = m + jnp.log2(den)
                return carry
            l4, l16 = lse_ref[0, orows, :], lse_ref[1, orows, :]
            mx = jnp.maximum(jnp.maximum(l4, l16), m)
            e1, e4, e16 = jnp.exp2(m - mx), jnp.exp2(l4 - mx), jnp.exp2(l16 - mx)
            num = e1 * acc + e4 * og_ref[0, orows, :] + e16 * og_ref[1, orows, :]
            o_ref[0, orows, :] = (num * (1.0 / (e1 * den + e4 + e16))).astype(o_ref.dtype)
            return carry

        lax.fori_loop(0, dil * nqb, q_block, 0, unroll=ATTN_UNROLL)


def dilated_attention(qkv, slopes):
    b, s, _ = qkv.shape
    h, dh = ATTN_HEADS, ATTN_DH
    assert s % (max(ATTN_DILATIONS) * ATTN_TK) == 0
    return pl.pallas_call(
        functools.partial(_attn_body, s=s),
        grid=(b, h),
        in_specs=[
            pl.BlockSpec((1, s, dh), lambda i, j: (i, 0, j)),
            pl.BlockSpec((1, s, dh), lambda i, j: (i, 0, h + j)),
            pl.BlockSpec((1, s, dh), lambda i, j: (i, 0, 2 * h + j)),
            pl.BlockSpec((1, SUBLANES, LANES), lambda i, j: (j, 0, 0)),
        ],
        out_specs=pl.BlockSpec((1, s, dh), lambda i, j: (i, 0, j)),
        out_shape=jax.ShapeDtypeStruct((b, s, h * dh), BF16),
        scratch_shapes=[
            pltpu.VMEM((len(ATTN_DILATIONS), 3, ATTN_TQ, ATTN_TK), F32),
            pltpu.VMEM((3, s, dh), F32),
            pltpu.VMEM((len(ATTN_DILATIONS) - 1, s, dh), F32),
            pltpu.VMEM((len(ATTN_DILATIONS) - 1, s, dh), F32),
        ],
        compiler_params=_cparams("parallel", "parallel"),
        name="dilated_attn",
    )(qkv, qkv, qkv, slopes)


MIXER_OUT_TM = 512


FFN_DOWN_TM = 256


def _ffn(x, h, w1, w3, w2, next_norm_g, last):
    u = swiglu_up(h, w1.astype(BF16), w3.astype(BF16))
    return matmul([(u, w2.astype(BF16))], res=x, norm_g=next_norm_g,
                  norm_dtype=F32 if last else BF16, keep_sum=not last,
                  tm=FFN_DOWN_TM, tn=w2.shape[1], name="ffn_down")


def _even_layer(x, bsz, seq, norm_g, w_in, gate_b, conv_w, conv_b, rg_wa, rg_ba, rg_wx, rg_bx,
                rg_lam, head_g, w_out, next_norm_g):
    heads, dk, dv = MLSTM_HEADS, MLSTM_DK, MLSTM_DV
    n_qkvo = 2 * heads * dk + 2 * heads * dv
    n_gate = 4 * heads
    width = RNN_BLOCKS * RNN_BLOCK
    w_main = jnp.concatenate([w_in[:, :n_qkvo], w_in[:, n_qkvo + n_gate:]], axis=1).astype(BF16)
    w_gate = jnp.pad(w_in[:, n_qkvo:n_qkvo + n_gate], ((0, 0), (0, LANES - n_gate))).astype(BF16)
    z, zg = norm_matmul(x, norm_g, w_main, w_side=w_gate, name="in_proj")
    z = z.reshape(bsz, seq, -1)

    chunk = MLSTM_CHUNK
    nc = seq // chunk
    gates_rows = jnp.transpose(zg[:, :n_gate].reshape(bsz, nc, chunk, 4, heads), (0, 4, 3, 1, 2))
    gate_b_rows = jnp.broadcast_to(
        jnp.transpose(gate_b.astype(F32).reshape(4, heads))[:, :, None, None], (heads, 4, 1, chunk))
    y_a = mlstm_mixer(z, gates_rows, gate_b_rows, head_g.astype(F32).reshape(heads, 1, dv), chunk)

    y_b = rglru_mixer(z, n_qkvo // RNN_BLOCK, (n_qkvo + width) // RNN_BLOCK,
                      conv_w.astype(F32), conv_b.astype(F32).reshape(1, width),
                      rg_wa.astype(BF16), rg_wx.astype(BF16),
                      rg_ba.astype(F32), rg_bx.astype(F32), rg_lam.astype(F32))

    w_out_b = w_out.astype(BF16)
    m = bsz * seq
    return matmul([(y_a.reshape(m, -1), w_out_b[:heads * dv]),
                   (y_b.reshape(m, -1), w_out_b[heads * dv:])], res=x, norm_g=next_norm_g,
                  tm=MIXER_OUT_TM, tn=w_out.shape[1], name="out_proj")


def _odd_layer(x, h, bsz, seq, norm_g, w_qkv, w_o, next_norm_g):
    if h is None:
        qkv = norm_matmul(x, norm_g, w_qkv.astype(BF16), name="qkv_proj")
    else:
        qkv = matmul([(h, w_qkv.astype(BF16))], name="qkv_proj")
    qkv = qkv.reshape(bsz, seq, -1)
    slopes = jnp.exp2(-ALIBI_MAX_BIAS * jnp.arange(1, ATTN_HEADS + 1, dtype=F32) / ATTN_HEADS)
    slopes = jnp.broadcast_to(slopes[:, None, None], (ATTN_HEADS, SUBLANES, LANES))
    o = dilated_attention(qkv, slopes)
    return matmul([(o.reshape(bsz * seq, -1), w_o.astype(BF16))], res=x, norm_g=next_norm_g,
                  tm=MIXER_OUT_TM, tn=w_o.shape[1], name="attn_out_proj")


def kernel(x, e_norm, e_w_in, e_gate_b, e_conv_w, e_conv_b, e_rg_wa, e_rg_ba, e_rg_wx, e_rg_bx,
           e_rg_lam, e_head_g, e_w_out, o_norm, o_w_qkv, o_w_o, f_norm, f_w1, f_w3, f_w2,
           final_norm):
    bsz, seq, d = x.shape
    depth = f_norm.shape[0]
    xs = x.reshape(bsz * seq, d).astype(F32)
    hs = None
    for l in range(depth):
        if l % 2 == 0:
            e = l // 2
            xs, hs = _even_layer(xs, bsz, seq, e_norm[e], e_w_in[e], e_gate_b[e], e_conv_w[e],
                                 e_conv_b[e], e_rg_wa[e], e_rg_ba[e], e_rg_wx[e], e_rg_bx[e],
                                 e_rg_lam[e], e_head_g[e], e_w_out[e], f_norm[l])
        else:
            o = l // 2
            xs, hs = _odd_layer(xs, hs, bsz, seq, o_norm[o], o_w_qkv[o], o_w_o[o], f_norm[l])
        if l == depth - 1:
            out = _ffn(xs, hs, f_w1[l], f_w3[l], f_w2[l], final_norm, last=True)
            return out.astype(x.dtype).reshape(bsz, seq, d)
        if (l + 1) % 2:
            xs, hs = _ffn(xs, hs, f_w1[l], f_w3[l], f_w2[l], o_norm[(l + 1) // 2], last=False)
        else:
            xs, hs = _ffn(xs, hs, f_w1[l], f_w3[l], f_w2[l], None, last=False), None
```

```python
import functools
import math

import jax
import jax.numpy as jnp
from jax import lax
from jax.experimental import pallas as pl
from jax.experimental.pallas import tpu as pltpu

F32 = jnp.float32
BF16 = jnp.bfloat16

RMS_EPS = 1e-6
NEG_BIG = -1e30

MLSTM_HEADS = 4
MLSTM_DK = 128
MLSTM_DV = 256
MLSTM_CHUNK = 256
RNN_BLOCKS = 8
RNN_BLOCK = 128
CONV_WIDTH = 4
CONV_LEFT = 2
RGLRU_C = 8.0
ATTN_HEADS = 16
ATTN_DH = 128
ATTN_HALF = 64
ATTN_DILATIONS = (1, 4, 16)
ALIBI_MAX_BIAS = 8.0

LANES = 128
SUBLANES = 8
VMEM_LIMIT_BYTES = 56 * 1024 * 1024


def _cparams(*semantics):
    return pltpu.CompilerParams(dimension_semantics=semantics,
                                vmem_limit_bytes=VMEM_LIMIT_BYTES)


NORM_ROWS = 128


def _norm_rows_into(x_ref, g_ref, h_ref):
    g = g_ref[...]

    def rows(t, carry):
        sl = pl.ds(pl.multiple_of(t * NORM_ROWS, NORM_ROWS), NORM_ROWS)
        x = x_ref[sl, :]
        ms = jnp.mean(x * x, axis=-1, keepdims=True)
        h_ref[sl, :] = (x * lax.rsqrt(ms + RMS_EPS) * g).astype(h_ref.dtype)
        return carry

    lax.fori_loop(0, x_ref.shape[0] // NORM_ROWS, rows, 0)


def _matmul_body(*refs, n_pairs, has_res, has_norm, keep_sum):
    pairs = [(refs[2 * p], refs[2 * p + 1]) for p in range(n_pairs)]
    rest = list(refs[2 * n_pairs:])
    r_ref = rest.pop(0) if has_res else None
    g_ref = rest.pop(0) if has_norm else None
    y_ref = rest.pop(0) if keep_sum else rest.pop(-1)

    acc = None
    for a_ref, w_ref in pairs:
        d = jnp.dot(a_ref[...], w_ref[...], preferred_element_type=F32)
        acc = d if acc is None else acc + d
    if has_res:
        acc = acc + r_ref[...]
    y_ref[...] = acc.astype(y_ref.dtype)
    if has_norm:
        _norm_rows_into(y_ref, g_ref, rest.pop(0))


def matmul(pairs, res=None, norm_g=None, norm_dtype=BF16, keep_sum=True, tm=1024, tn=1024,
           name="matmul"):
    m = pairs[0][0].shape[0]
    n = pairs[0][1].shape[1]
    tn = min(tn, n)
    assert m % tm == 0 and n % tn == 0 and (keep_sum or norm_g is not None)
    assert norm_g is None or (tn == n and tm % NORM_ROWS == 0)
    w_mode = dict(pipeline_mode=pl.Buffered(1)) if tn == n else {}
    in_specs, args = [], []
    for a, w, *w_row_block in pairs:
        kdim = a.shape[1]
        kb = w_row_block[0] if w_row_block else 0
        in_specs += [pl.BlockSpec((tm, kdim), lambda i, j: (i, 0)),
                     pl.BlockSpec((kdim, tn), lambda i, j, kb=kb: (kb, j), **w_mode)]
        args += [a, w]
    if res is not None:
        in_specs.append(pl.BlockSpec((tm, tn), lambda i, j: (i, j)))
        args.append(res)
    out_specs, out_shape, scratch = [], [], []
    if keep_sum:
        out_specs.append(pl.BlockSpec((tm, tn), lambda i, j: (i, j)))
        out_shape.append(jax.ShapeDtypeStruct((m, n), F32))
    else:
        scratch.append(pltpu.VMEM((tm, n), F32))
    if norm_g is not None:
        in_specs.append(pl.BlockSpec((1, n), lambda i, j: (0, 0)))
        args.append(norm_g.reshape(1, n).astype(F32))
        out_specs.append(pl.BlockSpec((tm, n), lambda i, j: (i, 0)))
        out_shape.append(jax.ShapeDtypeStruct((m, n), norm_dtype))
    outs = pl.pallas_call(
        functools.partial(_matmul_body, n_pairs=len(pairs), has_res=res is not None,
                          has_norm=norm_g is not None, keep_sum=keep_sum),
        grid=(m // tm, n // tn),
        in_specs=in_specs,
        out_specs=out_specs,
        out_shape=out_shape,
        scratch_shapes=scratch,
        compiler_params=_cparams("parallel", "parallel"),
        name=name,
    )(*args)
    return outs if len(outs) > 1 else outs[0]


SWIGLU_ROWS = 1024


def _swiglu_up_body(a_ref, w1_ref, w3_ref, o_ref):
    rows = min(SWIGLU_ROWS, a_ref.shape[0])
    for r0 in range(0, a_ref.shape[0], rows):
        a = a_ref[pl.ds(r0, rows), :]
        u = jnp.dot(a, w1_ref[...], preferred_element_type=F32)
        v = jnp.dot(a, w3_ref[...], preferred_element_type=F32)
        o_ref[pl.ds(r0, rows), :] = (u * jax.nn.sigmoid(u) * v).astype(o_ref.dtype)


def swiglu_up(a, w1, w3, tm=4096, tn=512):
    m, kdim = a.shape
    n = w1.shape[1]
    assert m % tm == 0 and n % tn == 0
    return pl.pallas_call(
        _swiglu_up_body,
        grid=(m // tm, n // tn),
        in_specs=[pl.BlockSpec((tm, kdim), lambda i, j: (i, 0)),
                  pl.BlockSpec((kdim, tn), lambda i, j: (0, j)),
                  pl.BlockSpec((kdim, tn), lambda i, j: (0, j))],
        out_specs=pl.BlockSpec((tm, tn), lambda i, j: (i, j)),
        out_shape=jax.ShapeDtypeStruct((m, n), BF16),
        compiler_params=_cparams("parallel", "parallel"),
        name="swiglu_up",
    )(a, w1, w3)


def _norm_matmul_body(*refs, has_side):
    if has_side:
        x_ref, g_ref, w_ref, ws_ref, o_ref, os_ref, h_ref = refs
    else:
        x_ref, g_ref, w_ref, o_ref, h_ref = refs

    @pl.when(pl.program_id(1) == 0)
    def _():
        _norm_rows_into(x_ref, g_ref, h_ref)
        if has_side:
            os_ref[...] = jnp.dot(h_ref[...], ws_ref[...], preferred_element_type=F32)

    o_ref[...] = jnp.dot(h_ref[...], w_ref[...], preferred_element_type=F32).astype(o_ref.dtype)


def norm_matmul(x, g, w, w_side=None, out_dtype=F32, tm=1024, tn=1024, name="norm_matmul"):
    m, d = x.shape
    n = w.shape[1]
    assert m % tm == 0 and n % tn == 0 and tm % NORM_ROWS == 0
    in_specs = [pl.BlockSpec((tm, d), lambda i, j: (i, 0)),
                pl.BlockSpec((1, d), lambda i, j: (0, 0)),
                pl.BlockSpec((d, tn), lambda i, j: (0, j))]
    args = [x, g.reshape(1, d).astype(F32), w]
    out_specs = [pl.BlockSpec((tm, tn), lambda i, j: (i, j))]
    out_shape = [jax.ShapeDtypeStruct((m, n), out_dtype)]
    if w_side is not None:
        ns = w_side.shape[1]
        in_specs.append(pl.BlockSpec((d, ns), lambda i, j: (0, 0)))
        args.append(w_side)
        out_specs.append(pl.BlockSpec((tm, ns), lambda i, j: (i, 0)))
        out_shape.append(jax.ShapeDtypeStruct((m, ns), F32))
    outs = pl.pallas_call(
        functools.partial(_norm_matmul_body, has_side=w_side is not None),
        grid=(m // tm, n // tn),
        in_specs=in_specs,
        out_specs=out_specs,
        out_shape=out_shape,
        scratch_shapes=[pltpu.VMEM((tm, d), BF16)],
        compiler_params=_cparams("parallel", "arbitrary"),
        name=name,
    )(*args)
    return outs if w_side is not None else outs[0]


def _log_sigmoid(x):
    return jnp.minimum(x, 0.0) - jnp.log1p(jnp.exp(-jnp.abs(x)))


def _lane_scan(x, combine, identity, reverse):
    width = x.shape[-1]
    lane = lax.broadcasted_iota(jnp.int32, x.shape, 1)
    sh = 1
    while sh < width:
        if reverse:
            shifted = jnp.where(lane < width - sh, pltpu.roll(x, width - sh, axis=1), identity)
        else:
            shifted = jnp.where(lane >= sh, pltpu.roll(x, sh, axis=1), identity)
        x = combine(x, shifted)
        sh *= 2
    return x


def _mlstm_body(q_ref, k_ref, v_ref, og_ref, g_ref, gb_ref, hg_ref, y_ref,
                rows_ref, cols_ref, hf_ref, hb_ref, ct_ref, *, nc, chunk):
    dk, dv = MLSTM_DK, MLSTM_DV
    scale = dk ** -0.5

    g = g_ref[0, 0] + gb_ref[0]
    rows = []
    for d in range(2):
        b = _lane_scan(_log_sigmoid(g[2 * d + 1]), jnp.add, 0.0, reverse=d == 1)
        gg = g[2 * d] - b
        gmax = _lane_scan(gg, jnp.maximum, -jnp.inf, reverse=d == 1)
        rows += [gg, b, gmax]
        for j, r in enumerate((gg, b, gmax)):
            rows_ref[3 * d + j] = r
    rowmat = jnp.concatenate(rows + [jnp.zeros((LANES - 6 * nc, chunk), F32)], axis=0)
    colmat = rowmat.T
    for c in range(nc):
        cols_ref[c] = colmat if c == 0 else pltpu.roll(colmat, LANES - c, axis=1)

    ct_ref[...] = jnp.zeros_like(ct_ref)

    row_id = lax.broadcasted_iota(jnp.int32, (chunk, chunk), 0)
    col_id = lax.broadcasted_iota(jnp.int32, (chunk, chunk), 1)
    causal = (row_id >= col_id, row_id <= col_id)
    ones = jnp.ones((chunk, LANES), BF16)

    def lanes2(x):
        return jnp.concatenate([x] * (dv // LANES), axis=1)

    def chunk_step(c, m, d, h_ref):
        r0 = pl.multiple_of(c * chunk, chunk)
        qb = (q_ref[0, pl.ds(r0, chunk), :] * scale).astype(BF16)
        kb = k_ref[0, pl.ds(r0, chunk), :].astype(BF16)
        vf = v_ref[0, pl.ds(r0, chunk), :]
        cols = cols_ref[c]

        def column(j):
            lane = (3 * d + j) * nc
            return jnp.broadcast_to(cols[:, lane:lane + 1], (chunk, LANES))

        gg_col, b_col, gmax_col = column(0), column(1), column(2)
        gg_row = rows_ref[3 * d, pl.ds(c, 1), :]
        end = chunk - 1 if d == 0 else 0
        total = rows_ref[3 * d + 1, pl.ds(c, 1), :][:, end:end + 1]
        gmax_end = rows_ref[3 * d + 2, pl.ds(c, 1), :][:, end:end + 1]

        m_row = jnp.maximum(gmax_col, m)
        w_intra = jnp.exp(jnp.where(causal[d], gg_row - lanes2(m_row), NEG_BIG))
        w_inter = jnp.exp(m - m_row)
        s = lax.dot_general(qb, kb, (((1,), (1,)), ((), ())),
                            preferred_element_type=F32) * w_intra
        ct = ct_ref[d]
        intra = jnp.dot(s.astype(BF16), jnp.concatenate([vf.astype(BF16), ones], axis=1),
                        preferred_element_type=F32)
        inter = jnp.dot(qb, ct.astype(BF16), preferred_element_type=F32)
        num = intra[:, :dv] + lanes2(w_inter) * inter[:, :dv]
        den = intra[:, dv:] + w_inter * inter[:, dv:]
        inv = 1.0 / jnp.maximum(jnp.abs(den), jnp.exp(-(b_col + m_row)))
        h_ref[pl.ds(r0, chunk), :] = num * lanes2(inv)

        m_new = jnp.maximum(total + m, total + gmax_end)
        w_src = jnp.exp(total + gg_col - m_new)
        decay = jnp.exp(total + m - m_new)
        wv = jnp.concatenate([lanes2(w_src) * vf, w_src], axis=1).astype(BF16)
        ct_ref[d] = decay * ct + lax.dot_general(
            kb, wv, (((0,), (0,)), ((), ())), preferred_element_type=F32)
        return m_new

    def body(c, carry):
        m_f, m_b = carry
        m_f = chunk_step(c, m_f, 0, hf_ref)
        m_b = chunk_step(nc - 1 - c, m_b, 1, hb_ref)
        return m_f, m_b

    m0 = jnp.full((1, 1), NEG_BIG, F32)
    lax.fori_loop(0, nc, body, (m0, m0), unroll=2)

    hg = hg_ref[0]

    def finish(c, carry):
        r0 = pl.multiple_of(c * chunk, chunk)
        hm = hf_ref[pl.ds(r0, chunk), :] + hb_ref[pl.ds(r0, chunk), :]
        ms = jnp.mean(hm * hm, axis=-1, keepdims=True)
        hn = hm * lax.rsqrt(ms + RMS_EPS) * hg
        gate = jax.nn.sigmoid(og_ref[0, pl.ds(r0, chunk), :])
        y_ref[0, pl.ds(r0, chunk), :] = (hn * gate).astype(y_ref.dtype)
        return carry

    lax.fori_loop(0, nc, finish, 0)


def mlstm_mixer(z, gates_rows, gate_b_rows, head_g, chunk=MLSTM_CHUNK):
    b, s, _ = z.shape
    h, dk, dv = MLSTM_HEADS, MLSTM_DK, MLSTM_DV
    nc = s // chunk
    assert s % chunk == 0 and 6 * nc <= LANES and chunk % LANES == 0
    kq = h * dk // dk
    kv = 2 * h * dk // dv
    ko = kv + h
    return pl.pallas_call(
        functools.partial(_mlstm_body, nc=nc, chunk=chunk),
        grid=(b, h),
        in_specs=[
            pl.BlockSpec((1, s, dk), lambda i, j: (i, 0, j)),
            pl.BlockSpec((1, s, dk), lambda i, j: (i, 0, kq + j)),
            pl.BlockSpec((1, s, dv), lambda i, j: (i, 0, kv + j)),
            pl.BlockSpec((1, s, dv), lambda i, j: (i, 0, ko + j)),
            pl.BlockSpec((1, 1, 4, nc, chunk), lambda i, j: (i, j, 0, 0, 0)),
            pl.BlockSpec((1, 4, 1, chunk), lambda i, j: (j, 0, 0, 0)),
            pl.BlockSpec((1, 1, dv), lambda i, j: (j, 0, 0)),
        ],
        out_specs=pl.BlockSpec((1, s, dv), lambda i, j: (i, 0, j)),
        out_shape=jax.ShapeDtypeStruct((b, s, h * dv), BF16),
        scratch_shapes=[
            pltpu.VMEM((6, nc, chunk), F32),
            pltpu.VMEM((nc, chunk, LANES), F32),
            pltpu.VMEM((s, dv), F32),
            pltpu.VMEM((s, dv), F32),
            pltpu.VMEM((2, dk, dv + LANES), F32),
        ],
        compiler_params=_cparams("parallel", "parallel"),
        name="mlstm",
    )(z, z, z, z, gates_rows, gate_b_rows, head_g)


RG_TILE = 512
RG_PAD = SUBLANES
RG_SCAN_UNROLL = 8


def _softplus(x):
    return jnp.maximum(x, 0.0) + jnp.log1p(jnp.exp(-jnp.abs(x)))


def _sigmoid(x):
    return 0.5 * jnp.tanh(0.5 * x) + 0.5


def _gelu_tanh(x):
    c = math.sqrt(2.0 / math.pi)
    return x * (0.5 * (1.0 + jnp.tanh(c * (x + 0.044715 * (x * x * x)))))


def _rglru_body(x_ref, gr_ref, cw_ref, cb_ref, wa_ref, wx_ref, ba_ref, bx_ref, lam_ref, y_ref,
                xpad_ref, a_ref, u_ref, as_ref, us_ref, h_ref, *, s):
    nt = s // RG_TILE
    zeros = jnp.zeros((RG_PAD, LANES), F32)
    xpad_ref[pl.ds(0, RG_PAD), :] = zeros
    xpad_ref[pl.ds(RG_PAD + s, RG_PAD), :] = zeros

    def copy_in(t, carry):
        r0 = pl.multiple_of(t * RG_TILE, RG_TILE)
        xpad_ref[pl.ds(RG_PAD + r0, RG_TILE), :] = x_ref[0, pl.ds(r0, RG_TILE), :]
        return carry

    lax.fori_loop(0, nt, copy_in, 0)

    cw = cw_ref[...]
    cb = cb_ref[...]
    sp = [RGLRU_C * _softplus(-lam_ref[d:d + 1, :]) for d in range(2)]

    def gates(t, carry):
        r0 = pl.multiple_of(t * RG_TILE, RG_TILE)
        xc = None
        for j in range(CONV_WIDTH):
            tap = xpad_ref[pl.ds(r0 + RG_PAD - CONV_LEFT + j, RG_TILE), :] * cw[j:j + 1, :]
            xc = tap if xc is None else xc + tap
        xc = xc + cb
        xcb = xc.astype(BF16)
        for d in range(2):
            r = _sigmoid(jnp.dot(xcb, wa_ref[d, 0], preferred_element_type=F32)
                         + ba_ref[d:d + 1, :])
            i = _sigmoid(jnp.dot(xcb, wx_ref[d, 0], preferred_element_type=F32)
                         + bx_ref[d:d + 1, :])
            neg_log_a = sp[d] * r
            a = jnp.exp(-neg_log_a)
            one_minus_a2 = jnp.tanh(neg_log_a) * (1.0 + a * a)
            root = jnp.where(one_minus_a2 > 0.0, one_minus_a2 * lax.rsqrt(one_minus_a2), 0.0)
            a_ref[d, pl.ds(r0, RG_TILE), :] = a
            u_ref[d, pl.ds(r0, RG_TILE), :] = root * (i * xc)
        return carry

    def block_scan(t, carry):
        r0 = pl.multiple_of(t * RG_TILE, RG_TILE)
        for d in range(2):
            a_prev = u_prev = None
            for r in (range(SUBLANES) if d == 0 else reversed(range(SUBLANES))):
                rows = pl.ds(r0 + r, RG_TILE // SUBLANES, stride=SUBLANES)
                a, u = a_ref[d, rows, :], u_ref[d, rows, :]
                if a_prev is not None:
                    u = a * u_prev + u
                    a = a * a_prev
                as_ref[d, rows, :] = a
                us_ref[d, rows, :] = u
                a_prev, u_prev = a, u
        return carry

    lax.fori_loop(0, nt, gates, 0)
    lax.fori_loop(0, nt, block_scan, 0)

    nblk = s // SUBLANES

    def scan(j, carry):
        h_f, h_b = carry
        for d, last, h in ((0, SUBLANES - 1, h_f), (1, 0, h_b)):
            blk = j if d == 0 else nblk - 1 - j
            rows = pl.ds(pl.multiple_of(blk * SUBLANES, SUBLANES), SUBLANES)
            a, u = as_ref[d, rows, :], us_ref[d, rows, :]
            h_ref[d, rows, :] = a * h + u
            a_end = jnp.broadcast_to(a[last:last + 1, :], (SUBLANES, LANES))
            u_end = jnp.broadcast_to(u[last:last + 1, :], (SUBLANES, LANES))
            if d == 0:
                h_f = a_end * h + u_end
            else:
                h_b = a_end * h + u_end
        return h_f, h_b

    h0 = jnp.zeros((SUBLANES, LANES), F32)
    lax.fori_loop(0, nblk, scan, (h0, h0), unroll=RG_SCAN_UNROLL)

    def finish(t, carry):
        r0 = pl.multiple_of(t * RG_TILE, RG_TILE)
        hr = h_ref[0, pl.ds(r0, RG_TILE), :] + h_ref[1, pl.ds(r0, RG_TILE), :]
        y_ref[0, pl.ds(r0, RG_TILE), :] = (
            hr * _gelu_tanh(gr_ref[0, pl.ds(r0, RG_TILE), :])).astype(y_ref.dtype)
        return carry

    lax.fori_loop(0, nt, finish, 0)


def rglru_mixer(z, xr_block0, gr_block0, conv_w, conv_b, wa, wx, ba, bx, lam):
    b, s, _ = z.shape
    width = RNN_BLOCKS * RNN_BLOCK
    assert s % RG_TILE == 0
    return pl.pallas_call(
        functools.partial(_rglru_body, s=s),
        grid=(b, RNN_BLOCKS),
        in_specs=[
            pl.BlockSpec((1, s, RNN_BLOCK), lambda i, j: (i, 0, xr_block0 + j)),
            pl.BlockSpec((1, s, RNN_BLOCK), lambda i, j: (i, 0, gr_block0 + j)),
            pl.BlockSpec((CONV_WIDTH, RNN_BLOCK), lambda i, j: (0, j)),
            pl.BlockSpec((1, RNN_BLOCK), lambda i, j: (0, j)),
            pl.BlockSpec((2, 1, RNN_BLOCK, RNN_BLOCK), lambda i, j: (0, j, 0, 0)),
            pl.BlockSpec((2, 1, RNN_BLOCK, RNN_BLOCK), lambda i, j: (0, j, 0, 0)),
            pl.BlockSpec((2, RNN_BLOCK), lambda i, j: (0, j)),
            pl.BlockSpec((2, RNN_BLOCK), lambda i, j: (0, j)),
            pl.BlockSpec((2, RNN_BLOCK), lambda i, j: (0, j)),
        ],
        out_specs=pl.BlockSpec((1, s, RNN_BLOCK), lambda i, j: (i, 0, j)),
        out_shape=jax.ShapeDtypeStruct((b, s, width), BF16),
        scratch_shapes=[
            pltpu.VMEM((s + 2 * RG_PAD, RNN_BLOCK), F32),
            pltpu.VMEM((2, s, RNN_BLOCK), F32),
            pltpu.VMEM((2, s, RNN_BLOCK), F32),
            pltpu.VMEM((2, s, RNN_BLOCK), F32),
            pltpu.VMEM((2, s, RNN_BLOCK), F32),
            pltpu.VMEM((2, s, RNN_BLOCK), F32),
        ],
        compiler_params=_cparams("parallel", "parallel"),
        name="rglru",
    )(z, z, conv_w, conv_b, wa, wx, ba, bx, lam)


ATTN_TQ = 128
ATTN_TK = ATTN_TQ + 2 * ATTN_HALF
ATTN_UNROLL = 32
LOG2E = math.log2(math.e)


def _attn_body(q_ref, k_ref, v_ref, slope_ref, o_ref, bias_ref, x4_ref, og_ref, lse_ref, *, s):
    qscale = ATTN_DH ** -0.5 * LOG2E

    @pl.when(pl.program_id(1) == 0)
    def _():
        slope = slope_ref[0, 0:1, 0:1] * LOG2E
        qi = lax.broadcasted_iota(jnp.int32, (ATTN_TQ, ATTN_TK), 0)
        kj = lax.broadcasted_iota(jnp.int32, (ATTN_TQ, ATTN_TK), 1)
        for g, dil in enumerate(ATTN_DILATIONS):
            for e in range(3):
                rel = jnp.abs(kj - e * ATTN_HALF - qi)
                pen = slope * (rel * dil).astype(F32)
                bias_ref[g, e] = jnp.where(rel <= ATTN_HALF, -pen, NEG_BIG)

    s4 = s // 4
    srcs = (q_ref, k_ref, v_ref)

    def split4(t, carry):
        c = t // (s4 // ATTN_TK)
        p0 = (t % (s4 // ATTN_TK)) * ATTN_TK
        dst = pl.ds(pl.multiple_of(c * s4 + p0, ATTN_TK), ATTN_TK)
        for a in range(3):
            x = srcs[a][0, pl.ds(c + 4 * p0, ATTN_TK, stride=4), :]
            x4_ref[a, dst, :] = x * qscale if a == 0 else x
        return carry

    lax.fori_loop(0, 4 * (s4 // ATTN_TK), split4, 0)

    ones = jnp.ones((ATTN_TK, ATTN_DH), BF16)

    for g, dil in reversed(list(enumerate(ATTN_DILATIONS))):
        sp = s // dil
        nqb = sp // ATTN_TQ

        def q_block(t, carry, g=g, dil=dil, sp=sp, nqb=nqb):
            r = t // nqb
            p0 = (t % nqb) * ATTN_TQ
            kstart = jnp.clip(p0 - ATTN_HALF, 0, sp - ATTN_TK)
            e = (p0 - kstart) // ATTN_HALF
            if dil == 1:
                qrows = pl.ds(pl.multiple_of(p0, ATTN_TQ), ATTN_TQ)
                krows = pl.ds(pl.multiple_of(kstart, ATTN_HALF), ATTN_TK)
                qf = q_ref[0, qrows, :] * qscale
                kf, vf = k_ref[0, krows, :], v_ref[0, krows, :]
                orows = qrows
            elif dil == 4:
                qrows = pl.ds(pl.multiple_of(r * s4 + p0, ATTN_TQ), ATTN_TQ)
                krows = pl.ds(pl.multiple_of(r * s4 + kstart, ATTN_HALF), ATTN_TK)
                qf, kf, vf = x4_ref[0, qrows, :], x4_ref[1, krows, :], x4_ref[2, krows, :]
                orows = pl.ds(r + 4 * p0, ATTN_TQ, stride=4)
            else:
                base = (r % 4) * s4 + r // 4
                qrows = pl.ds(base + 4 * p0, ATTN_TQ, stride=4)
                krows = pl.ds(base + 4 * kstart, ATTN_TK, stride=4)
                qf, kf, vf = x4_ref[0, qrows, :], x4_ref[1, krows, :], x4_ref[2, krows, :]
                orows = pl.ds(r + dil * p0, ATTN_TQ, stride=dil)
            qb, kb, vb = qf.astype(BF16), kf.astype(BF16), vf.astype(BF16)
            sc = lax.dot_general(qb, kb, (((1,), (1,)), ((), ())),
                                 preferred_element_type=F32) + bias_ref[g, e]
            m = jnp.max(sc, axis=-1, keepdims=True)
            p = jnp.exp2(sc - m).astype(BF16)
            pv = jnp.dot(p, jnp.concatenate([vb, ones], axis=1), preferred_element_type=F32)
            acc, den = pv[:, :ATTN_DH], pv[:, ATTN_DH:]
            if dil != 1:
                og_ref[g - 1, orows, :] = acc / den
                lse_ref[g - 1, orows, :] = m + jnp.log2(den)
                return carry
            l4, l16 = lse_ref[0, orows, :], lse_ref[1, orows, :]
            mx = jnp.maximum(jnp.maximum(l4, l16), m)
            e1, e4, e16 = jnp.exp2(m - mx), jnp.exp2(l4 - mx), jnp.exp2(l16 - mx)
            num = e1 * acc + e4 * og_ref[0, orows, :] + e16 * og_ref[1, orows, :]
            o_ref[0, orows, :] = (num * (1.0 / (e1 * den + e4 + e16))).astype(o_ref.dtype)
            return carry

        lax.fori_loop(0, dil * nqb, q_block, 0, unroll=ATTN_UNROLL)


def dilated_attention(qkv, slopes):
    b, s, _ = qkv.shape
    h, dh = ATTN_HEADS, ATTN_DH
    assert s % (max(ATTN_DILATIONS) * ATTN_TK) == 0
    return pl.pallas_call(
        functools.partial(_attn_body, s=s),
        grid=(h, b),
        in_specs=[
            pl.BlockSpec((1, s, dh), lambda j, i: (i, 0, j)),
            pl.BlockSpec((1, s, dh), lambda j, i: (i, 0, h + j)),
            pl.BlockSpec((1, s, dh), lambda j, i: (i, 0, 2 * h + j)),
            pl.BlockSpec((1, SUBLANES, LANES), lambda j, i: (j, 0, 0)),
        ],
        out_specs=pl.BlockSpec((1, s, dh), lambda j, i: (i, 0, j)),
        out_shape=jax.ShapeDtypeStruct((b, s, h * dh), BF16),
        scratch_shapes=[
            pltpu.VMEM((len(ATTN_DILATIONS), 3, ATTN_TQ, ATTN_TK), F32),
            pltpu.VMEM((3, s, dh), F32),
            pltpu.VMEM((len(ATTN_DILATIONS) - 1, s, dh), F32),
            pltpu.VMEM((len(ATTN_DILATIONS) - 1, s, dh), F32),
        ],
        compiler_params=_cparams("parallel", "arbitrary"),
        name="dilated_attn",
    )(qkv, qkv, qkv, slopes)


MIXER_OUT_TM = 512


FFN_DOWN_TM = 256


def _ffn(x, h, w1, w3, w2, next_norm_g, last):
    u = swiglu_up(h, w1.astype(BF16), w3.astype(BF16))
    return matmul([(u, w2.astype(BF16))], res=x, norm_g=next_norm_g,
                  norm_dtype=F32 if last else BF16, keep_sum=not last,
                  tm=FFN_DOWN_TM, tn=w2.shape[1], name="ffn_down")


def _even_layer(x, bsz, seq, norm_g, w_in, gate_b, conv_w, conv_b, rg_wa, rg_ba, rg_wx, rg_bx,
                rg_lam, head_g, w_out, next_norm_g):
    heads, dk, dv = MLSTM_HEADS, MLSTM_DK, MLSTM_DV
    n_qkvo = 2 * heads * dk + 2 * heads * dv
    n_gate = 4 * heads
    width = RNN_BLOCKS * RNN_BLOCK
    w_main = jnp.concatenate([w_in[:, :n_qkvo], w_in[:, n_qkvo + n_gate:]], axis=1).astype(BF16)
    w_gate = jnp.pad(w_in[:, n_qkvo:n_qkvo + n_gate], ((0, 0), (0, LANES - n_gate))).astype(BF16)
    z, zg = norm_matmul(x, norm_g, w_main, w_side=w_gate, name="in_proj")
    z = z.reshape(bsz, seq, -1)

    chunk = MLSTM_CHUNK
    nc = seq // chunk
    gates_rows = jnp.transpose(zg[:, :n_gate].reshape(bsz, nc, chunk, 4, heads), (0, 4, 3, 1, 2))
    gate_b_rows = jnp.broadcast_to(
        jnp.transpose(gate_b.astype(F32).reshape(4, heads))[:, :, None, None], (heads, 4, 1, chunk))
    y_a = mlstm_mixer(z, gates_rows, gate_b_rows, head_g.astype(F32).reshape(heads, 1, dv), chunk)

    y_b = rglru_mixer(z, n_qkvo // RNN_BLOCK, (n_qkvo + width) // RNN_BLOCK,
                      conv_w.astype(F32), conv_b.astype(F32).reshape(1, width),
                      rg_wa.astype(BF16), rg_wx.astype(BF16),
                      rg_ba.astype(F32), rg_bx.astype(F32), rg_lam.astype(F32))

    w_out_b = w_out.astype(BF16)
    m = bsz * seq
    assert heads * dv == width
    return matmul([(y_a.reshape(m, -1), w_out_b, 0),
                   (y_b.reshape(m, -1), w_out_b, 1)], res=x, norm_g=next_norm_g,
                  tm=MIXER_OUT_TM, tn=w_out.shape[1], name="out_proj")


def _odd_layer(x, h, bsz, seq, norm_g, w_qkv, w_o, next_norm_g):
    if h is None:
        qkv = norm_matmul(x, norm_g, w_qkv.astype(BF16), name="qkv_proj")
    else:
        qkv = matmul([(h, w_qkv.astype(BF16))], name="qkv_proj")
    qkv = qkv.reshape(bsz, seq, -1)
    slopes = jnp.exp2(-ALIBI_MAX_BIAS * jnp.arange(1, ATTN_HEADS + 1, dtype=F32) / ATTN_HEADS)
    slopes = jnp.broadcast_to(slopes[:, None, None], (ATTN_HEADS, SUBLANES, LANES))
    o = dilated_attention(qkv, slopes)
    return matmul([(o.reshape(bsz * seq, -1), w_o.astype(BF16))], res=x, norm_g=next_norm_g,
                  tm=MIXER_OUT_TM, tn=w_o.shape[1], name="attn_out_proj")


def kernel(x, e_norm, e_w_in, e_gate_b, e_conv_w, e_conv_b, e_rg_wa, e_rg_ba, e_rg_wx, e_rg_bx,
           e_rg_lam, e_head_g, e_w_out, o_norm, o_w_qkv, o_w_o, f_norm, f_w1, f_w3, f_w2,
           final_norm):
    bsz, seq, d = x.shape
    depth = f_norm.shape[0]
    xs = x.reshape(bsz * seq, d).astype(F32)
    hs = None
    for l in range(depth):
        if l % 2 == 0:
            e = l // 2
            xs, hs = _even_layer(xs, bsz, seq, e_norm[e], e_w_in[e], e_gate_b[e], e_conv_w[e],
                                 e_conv_b[e], e_rg_wa[e], e_rg_ba[e], e_rg_wx[e], e_rg_bx[e],
                                 e_rg_lam[e], e_head_g[e], e_w_out[e], f_norm[l])
        else:
            o = l // 2
            xs, hs = _odd_layer(xs, hs, bsz, seq, o_norm[o], o_w_qkv[o], o_w_o[o], f_norm[l])
        if l == depth - 1:
            out = _ffn(xs, hs, f_w1[l], f_w3[l], f_w2[l], final_norm, last=True)
            return out.astype(x.dtype).reshape(bsz, seq, d)
        if (l + 1) % 2:
            xs, hs = _ffn(xs, hs, f_w1[l], f_w3[l], f_w2[l], o_norm[(l + 1) // 2], last=False)
        else:
            xs, hs = _ffn(xs, hs, f_w1[l], f_w3[l], f_w2[l], None, last=False), None
```

```python
import functools
import math

import jax
import jax.numpy as jnp
from jax import lax
from jax.experimental import pallas as pl
from jax.experimental.pallas import tpu as pltpu

F32 = jnp.float32
BF16 = jnp.bfloat16

RMS_EPS = 1e-6
NEG_BIG = -1e30

MLSTM_HEADS = 4
MLSTM_DK = 128
MLSTM_DV = 256
MLSTM_CHUNK = 256
RNN_BLOCKS = 8
RNN_BLOCK = 128
CONV_WIDTH = 4
CONV_LEFT = 2
RGLRU_C = 8.0
ATTN_HEADS = 16
ATTN_DH = 128
ATTN_HALF = 64
ATTN_DILATIONS = (1, 4, 16)
ALIBI_MAX_BIAS = 8.0

LANES = 128
SUBLANES = 8
VMEM_LIMIT_BYTES = 56 * 1024 * 1024


def _cparams(*semantics):
    return pltpu.CompilerParams(dimension_semantics=semantics,
                                vmem_limit_bytes=VMEM_LIMIT_BYTES)


NORM_ROWS = 128


def _norm_rows_into(x_ref, g_ref, h_ref):
    g = g_ref[...]

    def rows(t, carry):
        sl = pl.ds(pl.multiple_of(t * NORM_ROWS, NORM_ROWS), NORM_ROWS)
        x = x_ref[sl, :]
        ms = jnp.mean(x * x, axis=-1, keepdims=True)
        h_ref[sl, :] = (x * lax.rsqrt(ms + RMS_EPS) * g).astype(h_ref.dtype)
        return carry

    lax.fori_loop(0, x_ref.shape[0] // NORM_ROWS, rows, 0)


def _matmul_body(*refs, n_pairs, has_res, has_norm, keep_sum):
    pairs = [(refs[2 * p], refs[2 * p + 1]) for p in range(n_pairs)]
    rest = list(refs[2 * n_pairs:])
    r_ref = rest.pop(0) if has_res else None
    g_ref = rest.pop(0) if has_norm else None
    y_ref = rest.pop(0) if keep_sum else rest.pop(-1)

    acc = None
    for a_ref, w_ref in pairs:
        d = jnp.dot(a_ref[...], w_ref[...], preferred_element_type=F32)
        acc = d if acc is None else acc + d
    if has_res:
        acc = acc + r_ref[...]
    y_ref[...] = acc.astype(y_ref.dtype)
    if has_norm:
        _norm_rows_into(y_ref, g_ref, rest.pop(0))


def matmul(pairs, res=None, norm_g=None, norm_dtype=BF16, keep_sum=True, tm=1024, tn=1024,
           name="matmul"):
    m = pairs[0][0].shape[0]
    n = pairs[0][1].shape[1]
    tn = min(tn, n)
    assert m % tm == 0 and n % tn == 0 and (keep_sum or norm_g is not None)
    assert norm_g is None or (tn == n and tm % NORM_ROWS == 0)
    w_mode = dict(pipeline_mode=pl.Buffered(1)) if tn == n else {}
    in_specs, args = [], []
    for a, w, *w_row_block in pairs:
        kdim = a.shape[1]
        kb = w_row_block[0] if w_row_block else 0
        in_specs += [pl.BlockSpec((tm, kdim), lambda i, j: (i, 0)),
                     pl.BlockSpec((kdim, tn), lambda i, j, kb=kb: (kb, j), **w_mode)]
        args += [a, w]
    if res is not None:
        in_specs.append(pl.BlockSpec((tm, tn), lambda i, j: (i, j)))
        args.append(res)
    out_specs, out_shape, scratch = [], [], []
    if keep_sum:
        out_specs.append(pl.BlockSpec((tm, tn), lambda i, j: (i, j)))
        out_shape.append(jax.ShapeDtypeStruct((m, n), F32))
    else:
        scratch.append(pltpu.VMEM((tm, n), F32))
    if norm_g is not None:
        in_specs.append(pl.BlockSpec((1, n), lambda i, j: (0, 0)))
        args.append(norm_g.reshape(1, n).astype(F32))
        out_specs.append(pl.BlockSpec((tm, n), lambda i, j: (i, 0)))
        out_shape.append(jax.ShapeDtypeStruct((m, n), norm_dtype))
    outs = pl.pallas_call(
        functools.partial(_matmul_body, n_pairs=len(pairs), has_res=res is not None,
                          has_norm=norm_g is not None, keep_sum=keep_sum),
        grid=(m // tm, n // tn),
        in_specs=in_specs,
        out_specs=out_specs,
        out_shape=out_shape,
        scratch_shapes=scratch,
        compiler_params=_cparams("parallel", "parallel"),
        name=name,
    )(*args)
    return outs if len(outs) > 1 else outs[0]


SWIGLU_ROWS = 1024


def _swiglu_up_body(a_ref, w1_ref, w3_ref, o_ref):
    rows = min(SWIGLU_ROWS, a_ref.shape[0])
    for r0 in range(0, a_ref.shape[0], rows):
        a = a_ref[pl.ds(r0, rows), :]
        u = jnp.dot(a, w1_ref[...], preferred_element_type=F32)
        v = jnp.dot(a, w3_ref[...], preferred_element_type=F32)
        o_ref[pl.ds(r0, rows), :] = (u * jax.nn.sigmoid(u) * v).astype(o_ref.dtype)


def swiglu_up(a, w1, w3, tm=4096, tn=512):
    m, kdim = a.shape
    n = w1.shape[1]
    assert m % tm == 0 and n % tn == 0
    return pl.pallas_call(
        _swiglu_up_body,
        grid=(m // tm, n // tn),
        in_specs=[pl.BlockSpec((tm, kdim), lambda i, j: (i, 0)),
                  pl.BlockSpec((kdim, tn), lambda i, j: (0, j)),
                  pl.BlockSpec((kdim, tn), lambda i, j: (0, j))],
        out_specs=pl.BlockSpec((tm, tn), lambda i, j: (i, j)),
        out_shape=jax.ShapeDtypeStruct((m, n), BF16),
        compiler_params=_cparams("parallel", "parallel"),
        name="swiglu_up",
    )(a, w1, w3)


NORM_CHUNK = 256


def _norm_matmul_body(*refs, has_side):
    if has_side:
        x_ref, g_ref, w_ref, ws_ref, o_ref, os_ref, h_ref = refs
    else:
        x_ref, g_ref, w_ref, o_ref, h_ref = refs

    @pl.when(pl.program_id(1) == 0)
    def _():
        g = g_ref[...]
        for r0 in range(0, x_ref.shape[0], NORM_CHUNK):
            sl = pl.ds(r0, NORM_CHUNK)
            x = x_ref[sl, :]
            ms = jnp.mean(x * x, axis=-1, keepdims=True)
            h = (x * lax.rsqrt(ms + RMS_EPS) * g).astype(h_ref.dtype)
            h_ref[sl, :] = h
            o_ref[sl, :] = jnp.dot(h, w_ref[...], preferred_element_type=F32).astype(o_ref.dtype)
            if has_side:
                os_ref[:, sl] = lax.dot_general(ws_ref[...], h, (((1,), (1,)), ((), ())),
                                                preferred_element_type=F32)

    @pl.when(pl.program_id(1) > 0)
    def _():
        o_ref[...] = jnp.dot(h_ref[...], w_ref[...],
                             preferred_element_type=F32).astype(o_ref.dtype)


def norm_matmul(x, g, w, w_side_t=None, out_dtype=F32, tm=1024, tn=1024, name="norm_matmul"):
    m, d = x.shape
    n = w.shape[1]
    assert m % tm == 0 and n % tn == 0 and tm % NORM_CHUNK == 0
    in_specs = [pl.BlockSpec((tm, d), lambda i, j: (i, 0)),
                pl.BlockSpec((1, d), lambda i, j: (0, 0)),
                pl.BlockSpec((d, tn), lambda i, j: (0, j))]
    args = [x, g.reshape(1, d).astype(F32), w]
    out_specs = [pl.BlockSpec((tm, tn), lambda i, j: (i, j))]
    out_shape = [jax.ShapeDtypeStruct((m, n), out_dtype)]
    if w_side_t is not None:
        ns = w_side_t.shape[0]
        in_specs.append(pl.BlockSpec((ns, d), lambda i, j: (0, 0)))
        args.append(w_side_t)
        out_specs.append(pl.BlockSpec((ns, tm), lambda i, j: (0, i)))
        out_shape.append(jax.ShapeDtypeStruct((ns, m), F32))
    outs = pl.pallas_call(
        functools.partial(_norm_matmul_body, has_side=w_side_t is not None),
        grid=(m // tm, n // tn),
        in_specs=in_specs,
        out_specs=out_specs,
        out_shape=out_shape,
        scratch_shapes=[pltpu.VMEM((tm, d), BF16)],
        compiler_params=_cparams("parallel", "arbitrary"),
        name=name,
    )(*args)
    return outs if w_side_t is not None else outs[0]


def _log_sigmoid(x):
    return jnp.minimum(x, 0.0) - jnp.log1p(jnp.exp(-jnp.abs(x)))


SCAN_RADIX = 4


def _lane_scan(x, combine, identity, reverse):
    width = x.shape[-1]
    lane = lax.broadcasted_iota(jnp.int32, x.shape, 1)

    def shifted(v, sh):
        if reverse:
            return jnp.where(lane < width - sh, pltpu.roll(v, width - sh, axis=1), identity)
        return jnp.where(lane >= sh, pltpu.roll(v, sh, axis=1), identity)

    sh = 1
    while sh < width:
        parts = [shifted(x, k * sh) for k in range(1, SCAN_RADIX) if k * sh < width]
        for p in parts:
            x = combine(x, p)
        sh *= SCAN_RADIX
    return x


def _mlstm_body(q_ref, k_ref, v_ref, og_ref, g_ref, gb_ref, hg_ref, y_ref,
                rows_ref, cols_ref, hf_ref, hb_ref, ct_ref, *, nc, chunk):
    dk, dv = MLSTM_DK, MLSTM_DV
    scale = dk ** -0.5

    g = g_ref[0, 0] + gb_ref[0]
    rows = []
    for d in range(2):
        b = _lane_scan(_log_sigmoid(g[2 * d + 1]), jnp.add, 0.0, reverse=d == 1)
        gg = g[2 * d] - b
        gmax = _lane_scan(gg, jnp.maximum, -jnp.inf, reverse=d == 1)
        rows += [gg, b, gmax]
        for j, r in enumerate((gg, b, gmax)):
            rows_ref[3 * d + j] = r
    rowmat = jnp.concatenate(rows + [jnp.zeros((LANES - 6 * nc, chunk), F32)], axis=0)
    colmat = rowmat.T
    for c in range(nc):
        cols_ref[c] = colmat if c == 0 else pltpu.roll(colmat, LANES - c, axis=1)

    ct_ref[...] = jnp.zeros_like(ct_ref)

    row_id = lax.broadcasted_iota(jnp.int32, (chunk, chunk), 0)
    col_id = lax.broadcasted_iota(jnp.int32, (chunk, chunk), 1)
    causal = (row_id >= col_id, row_id <= col_id)
    ones = jnp.ones((chunk, LANES), BF16)

    def lanes2(x):
        return jnp.concatenate([x] * (dv // LANES), axis=1)

    hg = hg_ref[0]

    def emit(r0, h, own_ref, other_ref):
        rows = pl.ds(r0, chunk)
        if other_ref is None:
            own_ref[rows, :] = h
            return
        hm = h + other_ref[rows, :]
        ms = jnp.mean(hm * hm, axis=-1, keepdims=True)
        hn = hm * lax.rsqrt(ms + RMS_EPS) * hg
        y_ref[0, rows, :] = (hn * jax.nn.sigmoid(og_ref[0, rows, :])).astype(y_ref.dtype)

    def chunk_step(c, m, d, own_ref, other_ref):
        r0 = pl.multiple_of(c * chunk, chunk)
        qb = (q_ref[0, pl.ds(r0, chunk), :] * scale).astype(BF16)
        kb = k_ref[0, pl.ds(r0, chunk), :].astype(BF16)
        vf = v_ref[0, pl.ds(r0, chunk), :]
        cols = cols_ref[c]

        def column(j):
            lane = (3 * d + j) * nc
            return jnp.broadcast_to(cols[:, lane:lane + 1], (chunk, LANES))

        gg_col, b_col, gmax_col = column(0), column(1), column(2)
        gg_row = rows_ref[3 * d, pl.ds(c, 1), :]
        end = chunk - 1 if d == 0 else 0
        total = rows_ref[3 * d + 1, pl.ds(c, 1), :][:, end:end + 1]
        gmax_end = rows_ref[3 * d + 2, pl.ds(c, 1), :][:, end:end + 1]

        m_row = jnp.maximum(gmax_col, m)
        w_intra = jnp.exp(jnp.where(causal[d], gg_row - lanes2(m_row), NEG_BIG))
        w_inter = jnp.exp(m - m_row)
        s = lax.dot_general(qb, kb, (((1,), (1,)), ((), ())),
                            preferred_element_type=F32) * w_intra
        ct = ct_ref[d]
        intra = jnp.dot(s.astype(BF16), jnp.concatenate([vf.astype(BF16), ones], axis=1),
                        preferred_element_type=F32)
        inter = jnp.dot(qb, ct.astype(BF16), preferred_element_type=F32)
        num = intra[:, :dv] + lanes2(w_inter) * inter[:, :dv]
        den = intra[:, dv:] + w_inter * inter[:, dv:]
        inv = 1.0 / jnp.maximum(jnp.abs(den), jnp.exp(-(b_col + m_row)))
        emit(r0, num * lanes2(inv), own_ref, other_ref)

        m_new = jnp.maximum(total + m, total + gmax_end)
        w_src = jnp.exp(total + gg_col - m_new)
        decay = jnp.exp(total + m - m_new)
        wv = jnp.concatenate([lanes2(w_src) * vf, w_src], axis=1).astype(BF16)
        ct_ref[d] = decay * ct + lax.dot_general(
            kb, wv, (((0,), (0,)), ((), ())), preferred_element_type=F32)
        return m_new

    def first_half(c, carry):
        m_f, m_b = carry
        m_f = chunk_step(c, m_f, 0, hf_ref, None)
        m_b = chunk_step(nc - 1 - c, m_b, 1, hb_ref, None)
        return m_f, m_b

    def second_half(c, carry):
        m_f, m_b = carry
        m_f = chunk_step(c, m_f, 0, None, hb_ref)
        m_b = chunk_step(nc - 1 - c, m_b, 1, None, hf_ref)
        return m_f, m_b

    m0 = jnp.full((1, 1), NEG_BIG, F32)
    carry = lax.fori_loop(0, nc // 2, first_half, (m0, m0), unroll=2)
    lax.fori_loop(nc // 2, nc, second_half, carry, unroll=2)


def mlstm_mixer(z, gates_rows, gate_b_rows, head_g, chunk=MLSTM_CHUNK):
    b, s, _ = z.shape
    h, dk, dv = MLSTM_HEADS, MLSTM_DK, MLSTM_DV
    nc = s // chunk
    assert s % chunk == 0 and 6 * nc <= LANES and chunk % LANES == 0 and nc % 2 == 0
    kq = h * dk // dk
    kv = 2 * h * dk // dv
    ko = kv + h
    return pl.pallas_call(
        functools.partial(_mlstm_body, nc=nc, chunk=chunk),
        grid=(b, h),
        in_specs=[
            pl.BlockSpec((1, s, dk), lambda i, j: (i, 0, j)),
            pl.BlockSpec((1, s, dk), lambda i, j: (i, 0, kq + j)),
            pl.BlockSpec((1, s, dv), lambda i, j: (i, 0, kv + j)),
            pl.BlockSpec((1, s, dv), lambda i, j: (i, 0, ko + j)),
            pl.BlockSpec((1, 1, 4, nc, chunk), lambda i, j: (i, j, 0, 0, 0)),
            pl.BlockSpec((1, 4, 1, chunk), lambda i, j: (j, 0, 0, 0)),
            pl.BlockSpec((1, 1, dv), lambda i, j: (j, 0, 0)),
        ],
        out_specs=pl.BlockSpec((1, s, dv), lambda i, j: (i, 0, j)),
        out_shape=jax.ShapeDtypeStruct((b, s, h * dv), BF16),
        scratch_shapes=[
            pltpu.VMEM((6, nc, chunk), F32),
            pltpu.VMEM((nc, chunk, LANES), F32),
            pltpu.VMEM((s, dv), F32),
            pltpu.VMEM((s, dv), F32),
            pltpu.VMEM((2, dk, dv + LANES), F32),
        ],
        compiler_params=_cparams("parallel", "parallel"),
        name="mlstm",
    )(z, z, z, z, gates_rows, gate_b_rows, head_g)


RG_TILE = 512
RG_PAD = SUBLANES
RG_SCAN_UNROLL = 8


def _softplus(x):
    return jnp.maximum(x, 0.0) + jnp.log1p(jnp.exp(-jnp.abs(x)))


def _sigmoid(x):
    return 0.5 * jnp.tanh(0.5 * x) + 0.5


def _gelu_tanh(x):
    c = math.sqrt(2.0 / math.pi)
    return x * (0.5 * (1.0 + jnp.tanh(c * (x + 0.044715 * (x * x * x)))))


def _rglru_body(x_ref, gr_ref, cw_ref, cb_ref, wa_ref, wx_ref, ba_ref, bx_ref, lam_ref, y_ref,
                xpad_ref, a_ref, u_ref, as_ref, us_ref, h_ref, *, s):
    nt = s // RG_TILE
    zeros = jnp.zeros((RG_PAD, LANES), F32)
    xpad_ref[pl.ds(0, RG_PAD), :] = zeros
    xpad_ref[pl.ds(RG_PAD + s, RG_PAD), :] = zeros

    def copy_in(t, carry):
        r0 = pl.multiple_of(t * RG_TILE, RG_TILE)
        xpad_ref[pl.ds(RG_PAD + r0, RG_TILE), :] = x_ref[0, pl.ds(r0, RG_TILE), :]
        return carry

    lax.fori_loop(0, nt, copy_in, 0)

    cw = cw_ref[...]
    cb = cb_ref[...]
    sp = [RGLRU_C * _softplus(-lam_ref[d:d + 1, :]) for d in range(2)]

    def gates(t, carry):
        r0 = pl.multiple_of(t * RG_TILE, RG_TILE)
        xc = None
        for j in range(CONV_WIDTH):
            tap = xpad_ref[pl.ds(r0 + RG_PAD - CONV_LEFT + j, RG_TILE), :] * cw[j:j + 1, :]
            xc = tap if xc is None else xc + tap
        xc = xc + cb
        xcb = xc.astype(BF16)
        for d in range(2):
            r = _sigmoid(jnp.dot(xcb, wa_ref[d, 0], preferred_element_type=F32)
                         + ba_ref[d:d + 1, :])
            i = _sigmoid(jnp.dot(xcb, wx_ref[d, 0], preferred_element_type=F32)
                         + bx_ref[d:d + 1, :])
            neg_log_a = sp[d] * r
            a = jnp.exp(-neg_log_a)
            one_minus_a2 = jnp.tanh(neg_log_a) * (1.0 + a * a)
            root = jnp.where(one_minus_a2 > 0.0, one_minus_a2 * lax.rsqrt(one_minus_a2), 0.0)
            a_ref[d, pl.ds(r0, RG_TILE), :] = a
            u_ref[d, pl.ds(r0, RG_TILE), :] = root * (i * xc)
        return carry

    def block_scan(t, carry):
        r0 = pl.multiple_of(t * RG_TILE, RG_TILE)
        for d in range(2):
            a_prev = u_prev = None
            for r in (range(SUBLANES) if d == 0 else reversed(range(SUBLANES))):
                rows = pl.ds(r0 + r, RG_TILE // SUBLANES, stride=SUBLANES)
                a, u = a_ref[d, rows, :], u_ref[d, rows, :]
                if a_prev is not None:
                    u = a * u_prev + u
                    a = a * a_prev
                as_ref[d, rows, :] = a
                us_ref[d, rows, :] = u
                a_prev, u_prev = a, u
        return carry

    lax.fori_loop(0, nt, gates, 0)
    lax.fori_loop(0, nt, block_scan, 0)

    nblk = s // SUBLANES

    def scan(j, carry):
        h_f, h_b = carry
        for d, last, h in ((0, SUBLANES - 1, h_f), (1, 0, h_b)):
            blk = j if d == 0 else nblk - 1 - j
            rows = pl.ds(pl.multiple_of(blk * SUBLANES, SUBLANES), SUBLANES)
            a, u = as_ref[d, rows, :], us_ref[d, rows, :]
            h_ref[d, rows, :] = a * h + u
            a_end = jnp.broadcast_to(a[last:last + 1, :], (SUBLANES, LANES))
            u_end = jnp.broadcast_to(u[last:last + 1, :], (SUBLANES, LANES))
            if d == 0:
                h_f = a_end * h + u_end
            else:
                h_b = a_end * h + u_end
        return h_f, h_b

    h0 = jnp.zeros((SUBLANES, LANES), F32)
    lax.fori_loop(0, nblk, scan, (h0, h0), unroll=RG_SCAN_UNROLL)

    def finish(t, carry):
        r0 = pl.multiple_of(t * RG_TILE, RG_TILE)
        hr = h_ref[0, pl.ds(r0, RG_TILE), :] + h_ref[1, pl.ds(r0, RG_TILE), :]
        y_ref[0, pl.ds(r0, RG_TILE), :] = (
            hr * _gelu_tanh(gr_ref[0, pl.ds(r0, RG_TILE), :])).astype(y_ref.dtype)
        return carry

    lax.fori_loop(0, nt, finish, 0)


def rglru_mixer(z, xr_block0, gr_block0, conv_w, conv_b, wa, wx, ba, bx, lam):
    b, s, _ = z.shape
    width = RNN_BLOCKS * RNN_BLOCK
    assert s % RG_TILE == 0
    return pl.pallas_call(
        functools.partial(_rglru_body, s=s),
        grid=(b, RNN_BLOCKS),
        in_specs=[
            pl.BlockSpec((1, s, RNN_BLOCK), lambda i, j: (i, 0, xr_block0 + j)),
            pl.BlockSpec((1, s, RNN_BLOCK), lambda i, j: (i, 0, gr_block0 + j)),
            pl.BlockSpec((CONV_WIDTH, RNN_BLOCK), lambda i, j: (0, j)),
            pl.BlockSpec((1, RNN_BLOCK), lambda i, j: (0, j)),
            pl.BlockSpec((2, 1, RNN_BLOCK, RNN_BLOCK), lambda i, j: (0, j, 0, 0)),
            pl.BlockSpec((2, 1, RNN_BLOCK, RNN_BLOCK), lambda i, j: (0, j, 0, 0)),
            pl.BlockSpec((2, RNN_BLOCK), lambda i, j: (0, j)),
            pl.BlockSpec((2, RNN_BLOCK), lambda i, j: (0, j)),
            pl.BlockSpec((2, RNN_BLOCK), lambda i, j: (0, j)),
        ],
        out_specs=pl.BlockSpec((1, s, RNN_BLOCK), lambda i, j: (i, 0, j)),
        out_shape=jax.ShapeDtypeStruct((b, s, width), BF16),
        scratch_shapes=[
            pltpu.VMEM((s + 2 * RG_PAD, RNN_BLOCK), F32),
            pltpu.VMEM((2, s, RNN_BLOCK), F32),
            pltpu.VMEM((2, s, RNN_BLOCK), F32),
            pltpu.VMEM((2, s, RNN_BLOCK), F32),
            pltpu.VMEM((2, s, RNN_BLOCK), F32),
            pltpu.VMEM((2, s, RNN_BLOCK), F32),
        ],
        compiler_params=_cparams("parallel", "parallel"),
        name="rglru",
    )(z, z, conv_w, conv_b, wa, wx, ba, bx, lam)


ATTN_TQ = 128
ATTN_TK = ATTN_TQ + 2 * ATTN_HALF
ATTN_UNROLL = 32
LOG2E = math.log2(math.e)


def _attn_body(q_ref, k_ref, v_ref, slope_ref, o_ref, bias_ref, x4_ref, og_ref, lse_ref, *, s):
    qscale = ATTN_DH ** -0.5 * LOG2E

    @pl.when(pl.program_id(1) == 0)
    def _():
        slope = slope_ref[0, 0:1, 0:1] * LOG2E
        qi = lax.broadcasted_iota(jnp.int32, (ATTN_TQ, ATTN_TK), 0)
        kj = lax.broadcasted_iota(jnp.int32, (ATTN_TQ, ATTN_TK), 1)
        for g, dil in enumerate(ATTN_DILATIONS):
            for e in range(3):
                rel = jnp.abs(kj - e * ATTN_HALF - qi)
                pen = slope * (rel * dil).astype(F32)
                bias_ref[g, e] = jnp.where(rel <= ATTN_HALF, -pen, NEG_BIG)

    s4 = s // 4
    srcs = (q_ref, k_ref, v_ref)

    def split4(t, carry):
        c = t // (s4 // ATTN_TK)
        p0 = (t % (s4 // ATTN_TK)) * ATTN_TK
        dst = pl.ds(pl.multiple_of(c * s4 + p0, ATTN_TK), ATTN_TK)
        for a in range(3):
            x = srcs[a][0, pl.ds(c + 4 * p0, ATTN_TK, stride=4), :]
            x4_ref[a, dst, :] = x * qscale if a == 0 else x
        return carry

    lax.fori_loop(0, 4 * (s4 // ATTN_TK), split4, 0)

    ones = jnp.ones((ATTN_TK, ATTN_DH), BF16)

    for g, dil in reversed(list(enumerate(ATTN_DILATIONS))):
        sp = s // dil
        nqb = sp // ATTN_TQ

        def q_block(t, carry, g=g, dil=dil, sp=sp, nqb=nqb):
            r = t // nqb
            p0 = (t % nqb) * ATTN_TQ
            kstart = jnp.clip(p0 - ATTN_HALF, 0, sp - ATTN_TK)
            e = (p0 - kstart) // ATTN_HALF
            if dil == 1:
                qrows = pl.ds(pl.multiple_of(p0, ATTN_TQ), ATTN_TQ)
                krows = pl.ds(pl.multiple_of(kstart, ATTN_HALF), ATTN_TK)
                qf = q_ref[0, qrows, :] * qscale
                kf, vf = k_ref[0, krows, :], v_ref[0, krows, :]
                orows = qrows
            elif dil == 4:
                qrows = pl.ds(pl.multiple_of(r * s4 + p0, ATTN_TQ), ATTN_TQ)
                krows = pl.ds(pl.multiple_of(r * s4 + kstart, ATTN_HALF), ATTN_TK)
                qf, kf, vf = x4_ref[0, qrows, :], x4_ref[1, krows, :], x4_ref[2, krows, :]
                orows = pl.ds(r + 4 * p0, ATTN_TQ, stride=4)
            else:
                base = (r % 4) * s4 + r // 4
                qrows = pl.ds(base + 4 * p0, ATTN_TQ, stride=4)
                krows = pl.ds(base + 4 * kstart, ATTN_TK, stride=4)
                qf, kf, vf = x4_ref[0, qrows, :], x4_ref[1, krows, :], x4_ref[2, krows, :]
                orows = pl.ds(r + dil * p0, ATTN_TQ, stride=dil)
            qb, kb, vb = qf.astype(BF16), kf.astype(BF16), vf.astype(BF16)
            sc = lax.dot_general(qb, kb, (((1,), (1,)), ((), ())),
                                 preferred_element_type=F32) + bias_ref[g, e]
            m = jnp.max(sc, axis=-1, keepdims=True)
            p = jnp.exp2(sc - m).astype(BF16)
            pv = jnp.dot(p, jnp.concatenate([vb, ones], axis=1), preferred_element_type=F32)
            acc, den = pv[:, :ATTN_DH], pv[:, ATTN_DH:]
            if dil != 1:
                og_ref[g - 1, orows, :] = acc / den
                lse_ref[g - 1, orows, :] = m + jnp.log2(den)
                return carry
            l4, l16 = lse_ref[0, orows, :], lse_ref[1, orows, :]
            mx = jnp.maximum(jnp.maximum(l4, l16), m)
            e1, e4, e16 = jnp.exp2(m - mx), jnp.exp2(l4 - mx), jnp.exp2(l16 - mx)
            num = e1 * acc + e4 * og_ref[0, orows, :] + e16 * og_ref[1, orows, :]
            o_ref[0, orows, :] = (num * (1.0 / (e1 * den + e4 + e16))).astype(o_ref.dtype)
            return carry

        lax.fori_loop(0, dil * nqb, q_block, 0, unroll=ATTN_UNROLL)


def dilated_attention(qkv, slopes):
    b, s, _ = qkv.shape
    h, dh = ATTN_HEADS, ATTN_DH
    assert s % (max(ATTN_DILATIONS) * ATTN_TK) == 0
    return pl.pallas_call(
        functools.partial(_attn_body, s=s),
        grid=(h, b),
        in_specs=[
            pl.BlockSpec((1, s, dh), lambda j, i: (i, 0, j)),
            pl.BlockSpec((1, s, dh), lambda j, i: (i, 0, h + j)),
            pl.BlockSpec((1, s, dh), lambda j, i: (i, 0, 2 * h + j)),
            pl.BlockSpec((1, SUBLANES, LANES), lambda j, i: (j, 0, 0)),
        ],
        out_specs=pl.BlockSpec((1, s, dh), lambda j, i: (i, 0, j)),
        out_shape=jax.ShapeDtypeStruct((b, s, h * dh), BF16),
        scratch_shapes=[
            pltpu.VMEM((len(ATTN_DILATIONS), 3, ATTN_TQ, ATTN_TK), F32),
            pltpu.VMEM((3, s, dh), F32),
            pltpu.VMEM((len(ATTN_DILATIONS) - 1, s, dh), F32),
            pltpu.VMEM((len(ATTN_DILATIONS) - 1, s, dh), F32),
        ],
        compiler_params=_cparams("parallel", "arbitrary"),
        name="dilated_attn",
    )(qkv, qkv, qkv, slopes)


MIXER_OUT_TM = 512


FFN_DOWN_TM = 256


def _ffn(x, h, w1, w3, w2, next_norm_g, last):
    u = swiglu_up(h, w1.astype(BF16), w3.astype(BF16))
    return matmul([(u, w2.astype(BF16))], res=x, norm_g=next_norm_g,
                  norm_dtype=F32 if last else BF16, keep_sum=not last,
                  tm=FFN_DOWN_TM, tn=w2.shape[1], name="ffn_down")


def _even_layer(x, bsz, seq, norm_g, w_in, gate_b, conv_w, conv_b, rg_wa, rg_ba, rg_wx, rg_bx,
                rg_lam, head_g, w_out, next_norm_g):
    heads, dk, dv = MLSTM_HEADS, MLSTM_DK, MLSTM_DV
    n_qkvo = 2 * heads * dk + 2 * heads * dv
    n_gate = 4 * heads
    width = RNN_BLOCKS * RNN_BLOCK
    w_main = jnp.concatenate([w_in[:, :n_qkvo], w_in[:, n_qkvo + n_gate:]], axis=1).astype(BF16)
    w_gate_t = jnp.transpose(w_in[:, n_qkvo:n_qkvo + n_gate]).astype(BF16)
    z, zg_t = norm_matmul(x, norm_g, w_main, w_side_t=w_gate_t, name="in_proj")
    z = z.reshape(bsz, seq, -1)

    chunk = MLSTM_CHUNK
    nc = seq // chunk
    gates_rows = jnp.transpose(zg_t[:n_gate].reshape(4, heads, bsz, nc, chunk), (2, 1, 0, 3, 4))
    gate_b_rows = jnp.broadcast_to(
        jnp.transpose(gate_b.astype(F32).reshape(4, heads))[:, :, None, None], (heads, 4, 1, chunk))
    y_a = mlstm_mixer(z, gates_rows, gate_b_rows, head_g.astype(F32).reshape(heads, 1, dv), chunk)

    y_b = rglru_mixer(z, n_qkvo // RNN_BLOCK, (n_qkvo + width) // RNN_BLOCK,
                      conv_w.astype(F32), conv_b.astype(F32).reshape(1, width),
                      rg_wa.astype(BF16), rg_wx.astype(BF16),
                      rg_ba.astype(F32), rg_bx.astype(F32), rg_lam.astype(F32))

    w_out_b = w_out.astype(BF16)
    m = bsz * seq
    assert heads * dv == width
    return matmul([(y_a.reshape(m, -1), w_out_b, 0),
                   (y_b.reshape(m, -1), w_out_b, 1)], res=x, norm_g=next_norm_g,
                  tm=MIXER_OUT_TM, tn=w_out.shape[1], name="out_proj")


def _odd_layer(x, h, bsz, seq, norm_g, w_qkv, w_o, next_norm_g):
    if h is None:
        qkv = norm_matmul(x, norm_g, w_qkv.astype(BF16), name="qkv_proj")
    else:
        qkv = matmul([(h, w_qkv.astype(BF16))], name="qkv_proj")
    qkv = qkv.reshape(bsz, seq, -1)
    slopes = jnp.exp2(-ALIBI_MAX_BIAS * jnp.arange(1, ATTN_HEADS + 1, dtype=F32) / ATTN_HEADS)
    slopes = jnp.broadcast_to(slopes[:, None, None], (ATTN_HEADS, SUBLANES, LANES))
    o = dilated_attention(qkv, slopes)
    return matmul([(o.reshape(bsz * seq, -1), w_o.astype(BF16))], res=x, norm_g=next_norm_g,
                  tm=MIXER_OUT_TM, tn=w_o.shape[1], name="attn_out_proj")


def kernel(x, e_norm, e_w_in, e_gate_b, e_conv_w, e_conv_b, e_rg_wa, e_rg_ba, e_rg_wx, e_rg_bx,
           e_rg_lam, e_head_g, e_w_out, o_norm, o_w_qkv, o_w_o, f_norm, f_w1, f_w3, f_w2,
           final_norm):
    bsz, seq, d = x.shape
    depth = f_norm.shape[0]
    xs = x.reshape(bsz * seq, d).astype(F32)
    hs = None
    for l in range(depth):
        if l % 2 == 0:
            e = l // 2
            xs, hs = _even_layer(xs, bsz, seq, e_norm[e], e_w_in[e], e_gate_b[e], e_conv_w[e],
                                 e_conv_b[e], e_rg_wa[e], e_rg_ba[e], e_rg_wx[e], e_rg_bx[e],
                                 e_rg_lam[e], e_head_g[e], e_w_out[e], f_norm[l])
        else:
            o = l // 2
            xs, hs = _odd_layer(xs, hs, bsz, seq, o_norm[o], o_w_qkv[o], o_w_o[o], f_norm[l])
        if l == depth - 1:
            out = _ffn(xs, hs, f_w1[l], f_w3[l], f_w2[l], final_norm, last=True)
            return out.astype(x.dtype).reshape(bsz, seq, d)
        if (l + 1) % 2:
            xs, hs = _ffn(xs, hs, f_w1[l], f_w3[l], f_w2[l], o_norm[(l + 1) // 2], last=False)
        else:
            xs, hs = _ffn(xs, hs, f_w1[l], f_w3[l], f_w2[l], None, last=False), None
```

```python
import functools
import math

import jax
import jax.numpy as jnp
from jax import lax
from jax.experimental import pallas as pl
from jax.experimental.pallas import tpu as pltpu

F32 = jnp.float32
BF16 = jnp.bfloat16

RMS_EPS = 1e-6
NEG_BIG = -1e30

MLSTM_HEADS = 4
MLSTM_DK = 128
MLSTM_DV = 256
MLSTM_CHUNK = 256
RNN_BLOCKS = 8
RNN_BLOCK = 128
CONV_WIDTH = 4
CONV_LEFT = 2
RGLRU_C = 8.0
ATTN_HEADS = 16
ATTN_DH = 128
ATTN_HALF = 64
ATTN_DILATIONS = (1, 4, 16)
ALIBI_MAX_BIAS = 8.0

LANES = 128
SUBLANES = 8
VMEM_LIMIT_BYTES = 56 * 1024 * 1024


def _cparams(*semantics):
    return pltpu.CompilerParams(dimension_semantics=semantics,
                                vmem_limit_bytes=VMEM_LIMIT_BYTES)


NORM_ROWS = 128


def _norm_rows_into(x_ref, g_ref, h_ref):
    g = g_ref[...]

    def rows(t, carry):
        sl = pl.ds(pl.multiple_of(t * NORM_ROWS, NORM_ROWS), NORM_ROWS)
        x = x_ref[sl, :]
        ms = jnp.mean(x * x, axis=-1, keepdims=True)
        h_ref[sl, :] = (x * lax.rsqrt(ms + RMS_EPS) * g).astype(h_ref.dtype)
        return carry

    lax.fori_loop(0, x_ref.shape[0] // NORM_ROWS, rows, 0)


def _matmul_body(*refs, n_pairs, has_res, has_norm, keep_sum):
    pairs = [(refs[2 * p], refs[2 * p + 1]) for p in range(n_pairs)]
    rest = list(refs[2 * n_pairs:])
    r_ref = rest.pop(0) if has_res else None
    g_ref = rest.pop(0) if has_norm else None
    y_ref = rest.pop(0) if keep_sum else rest.pop(-1)

    acc = None
    for a_ref, w_ref in pairs:
        d = jnp.dot(a_ref[...], w_ref[...], preferred_element_type=F32)
        acc = d if acc is None else acc + d
    if has_res:
        acc = acc + r_ref[...]
    y_ref[...] = acc.astype(y_ref.dtype)
    if has_norm:
        _norm_rows_into(y_ref, g_ref, rest.pop(0))


def matmul(pairs, res=None, norm_g=None, norm_dtype=BF16, keep_sum=True, tm=1024, tn=1024,
           name="matmul"):
    m = pairs[0][0].shape[0]
    n = pairs[0][1].shape[1]
    tn = min(tn, n)
    assert m % tm == 0 and n % tn == 0 and (keep_sum or norm_g is not None)
    assert norm_g is None or (tn == n and tm % NORM_ROWS == 0)
    w_mode = dict(pipeline_mode=pl.Buffered(1)) if tn == n else {}
    in_specs, args = [], []
    for a, w, *w_row_block in pairs:
        kdim = a.shape[1]
        kb = w_row_block[0] if w_row_block else 0
        in_specs += [pl.BlockSpec((tm, kdim), lambda i, j: (i, 0)),
                     pl.BlockSpec((kdim, tn), lambda i, j, kb=kb: (kb, j), **w_mode)]
        args += [a, w]
    if res is not None:
        in_specs.append(pl.BlockSpec((tm, tn), lambda i, j: (i, j)))
        args.append(res)
    out_specs, out_shape, scratch = [], [], []
    if keep_sum:
        out_specs.append(pl.BlockSpec((tm, tn), lambda i, j: (i, j)))
        out_shape.append(jax.ShapeDtypeStruct((m, n), F32))
    else:
        scratch.append(pltpu.VMEM((tm, n), F32))
    if norm_g is not None:
        in_specs.append(pl.BlockSpec((1, n), lambda i, j: (0, 0)))
        args.append(norm_g.reshape(1, n).astype(F32))
        out_specs.append(pl.BlockSpec((tm, n), lambda i, j: (i, 0)))
        out_shape.append(jax.ShapeDtypeStruct((m, n), norm_dtype))
    outs = pl.pallas_call(
        functools.partial(_matmul_body, n_pairs=len(pairs), has_res=res is not None,
                          has_norm=norm_g is not None, keep_sum=keep_sum),
        grid=(m // tm, n // tn),
        in_specs=in_specs,
        out_specs=out_specs,
        out_shape=out_shape,
        scratch_shapes=scratch,
        compiler_params=_cparams("parallel", "parallel"),
        name=name,
    )(*args)
    return outs if len(outs) > 1 else outs[0]


SWIGLU_ROWS = 1024


def _swiglu_up_body(a_ref, w1_ref, w3_ref, o_ref):
    rows = min(SWIGLU_ROWS, a_ref.shape[0])
    for r0 in range(0, a_ref.shape[0], rows):
        a = a_ref[pl.ds(r0, rows), :]
        u = jnp.dot(a, w1_ref[...], preferred_element_type=F32)
        v = jnp.dot(a, w3_ref[...], preferred_element_type=F32)
        o_ref[pl.ds(r0, rows), :] = (u * jax.nn.sigmoid(u) * v).astype(o_ref.dtype)


def swiglu_up(a, w1, w3, tm=4096, tn=512):
    m, kdim = a.shape
    n = w1.shape[1]
    assert m % tm == 0 and n % tn == 0
    return pl.pallas_call(
        _swiglu_up_body,
        grid=(m // tm, n // tn),
        in_specs=[pl.BlockSpec((tm, kdim), lambda i, j: (i, 0)),
                  pl.BlockSpec((kdim, tn), lambda i, j: (0, j)),
                  pl.BlockSpec((kdim, tn), lambda i, j: (0, j))],
        out_specs=pl.BlockSpec((tm, tn), lambda i, j: (i, j)),
        out_shape=jax.ShapeDtypeStruct((m, n), BF16),
        compiler_params=_cparams("parallel", "parallel"),
        name="swiglu_up",
    )(a, w1, w3)


NORM_CHUNK = 256


def _norm_matmul_body(*refs, has_side):
    if has_side:
        x_ref, g_ref, w_ref, ws_ref, o_ref, os_ref, h_ref = refs
    else:
        x_ref, g_ref, w_ref, o_ref, h_ref = refs

    @pl.when(pl.program_id(1) == 0)
    def _():
        g = g_ref[...]
        for r0 in range(0, x_ref.shape[0], NORM_CHUNK):
            sl = pl.ds(r0, NORM_CHUNK)
            x = x_ref[sl, :]
            ms = jnp.mean(x * x, axis=-1, keepdims=True)
            h = (x * lax.rsqrt(ms + RMS_EPS) * g).astype(h_ref.dtype)
            h_ref[sl, :] = h
            o_ref[sl, :] = jnp.dot(h, w_ref[...], preferred_element_type=F32).astype(o_ref.dtype)
            if has_side:
                os_ref[:, sl] = lax.dot_general(ws_ref[...], h, (((1,), (1,)), ((), ())),
                                                preferred_element_type=F32)

    @pl.when(pl.program_id(1) > 0)
    def _():
        o_ref[...] = jnp.dot(h_ref[...], w_ref[...],
                             preferred_element_type=F32).astype(o_ref.dtype)


def norm_matmul(x, g, w, w_side_t=None, out_dtype=F32, tm=1024, tn=1024, name="norm_matmul"):
    m, d = x.shape
    n = w.shape[1]
    assert m % tm == 0 and n % tn == 0 and tm % NORM_CHUNK == 0
    in_specs = [pl.BlockSpec((tm, d), lambda i, j: (i, 0)),
                pl.BlockSpec((1, d), lambda i, j: (0, 0)),
                pl.BlockSpec((d, tn), lambda i, j: (0, j))]
    args = [x, g.reshape(1, d).astype(F32), w]
    out_specs = [pl.BlockSpec((tm, tn), lambda i, j: (i, j))]
    out_shape = [jax.ShapeDtypeStruct((m, n), out_dtype)]
    if w_side_t is not None:
        ns = w_side_t.shape[0]
        in_specs.append(pl.BlockSpec((ns, d), lambda i, j: (0, 0)))
        args.append(w_side_t)
        out_specs.append(pl.BlockSpec((ns, tm), lambda i, j: (0, i)))
        out_shape.append(jax.ShapeDtypeStruct((ns, m), F32))
    outs = pl.pallas_call(
        functools.partial(_norm_matmul_body, has_side=w_side_t is not None),
        grid=(m // tm, n // tn),
        in_specs=in_specs,
        out_specs=out_specs,
        out_shape=out_shape,
        scratch_shapes=[pltpu.VMEM((tm, d), BF16)],
        compiler_params=_cparams("parallel", "arbitrary"),
        name=name,
    )(*args)
    return outs if w_side_t is not None else outs[0]


def _log_sigmoid(x):
    return jnp.minimum(x, 0.0) - jnp.log1p(jnp.exp(-jnp.abs(x)))


SCAN_RADIX = 4


def _lane_scan(x, combine, identity, reverse):
    width = x.shape[-1]
    lane = lax.broadcasted_iota(jnp.int32, x.shape, 1)

    def shifted(v, sh):
        if reverse:
            return jnp.where(lane < width - sh, pltpu.roll(v, width - sh, axis=1), identity)
        return jnp.where(lane >= sh, pltpu.roll(v, sh, axis=1), identity)

    sh = 1
    while sh < width:
        parts = [shifted(x, k * sh) for k in range(1, SCAN_RADIX) if k * sh < width]
        for p in parts:
            x = combine(x, p)
        sh *= SCAN_RADIX
    return x


def _mlstm_body(q_ref, k_ref, v_ref, og_ref, g_ref, gb_ref, hg_ref, y_ref,
                rows_ref, cols_ref, hf_ref, hb_ref, ct_ref, *, nc, chunk):
    dk, dv = MLSTM_DK, MLSTM_DV
    scale = dk ** -0.5

    g = g_ref[0, 0] + gb_ref[0]
    rows = []
    for d in range(2):
        b = _lane_scan(_log_sigmoid(g[2 * d + 1]), jnp.add, 0.0, reverse=d == 1)
        gg = g[2 * d] - b
        gmax = _lane_scan(gg, jnp.maximum, -jnp.inf, reverse=d == 1)
        rows += [gg, b, gmax]
        for j, r in enumerate((gg, b, gmax)):
            rows_ref[3 * d + j] = r
    rowmat = jnp.concatenate(rows + [jnp.zeros((LANES - 6 * nc, chunk), F32)], axis=0)
    colmat = rowmat.T
    for c in range(nc):
        cols_ref[c] = colmat if c == 0 else pltpu.roll(colmat, LANES - c, axis=1)

    ct_ref[...] = jnp.zeros_like(ct_ref)

    row_id = lax.broadcasted_iota(jnp.int32, (chunk, chunk), 0)
    col_id = lax.broadcasted_iota(jnp.int32, (chunk, chunk), 1)
    causal = (row_id >= col_id, row_id <= col_id)
    ones = jnp.ones((chunk, LANES), BF16)

    def lanes2(x):
        return jnp.concatenate([x] * (dv // LANES), axis=1)

    hg = hg_ref[0]

    def emit(r0, h, own_ref, other_ref):
        rows = pl.ds(r0, chunk)
        if other_ref is None:
            own_ref[rows, :] = h
            return
        hm = h + other_ref[rows, :]
        ms = jnp.mean(hm * hm, axis=-1, keepdims=True)
        hn = hm * lax.rsqrt(ms + RMS_EPS) * hg
        y_ref[0, rows, :] = (hn * jax.nn.sigmoid(og_ref[0, rows, :])).astype(y_ref.dtype)

    def chunk_step(c, m, d, own_ref, other_ref):
        r0 = pl.multiple_of(c * chunk, chunk)
        qb = (q_ref[0, pl.ds(r0, chunk), :] * scale).astype(BF16)
        kb = k_ref[0, pl.ds(r0, chunk), :].astype(BF16)
        vf = v_ref[0, pl.ds(r0, chunk), :]
        cols = cols_ref[c]

        def column(j):
            lane = (3 * d + j) * nc
            return jnp.broadcast_to(cols[:, lane:lane + 1], (chunk, LANES))

        gg_col, b_col, gmax_col = column(0), column(1), column(2)
        gg_row = rows_ref[3 * d, pl.ds(c, 1), :]
        end = chunk - 1 if d == 0 else 0
        total = rows_ref[3 * d + 1, pl.ds(c, 1), :][:, end:end + 1]
        gmax_end = rows_ref[3 * d + 2, pl.ds(c, 1), :][:, end:end + 1]

        m_row = jnp.maximum(gmax_col, m)
        w_intra = jnp.exp(jnp.where(causal[d], gg_row - lanes2(m_row), NEG_BIG))
        w_inter = jnp.exp(m - m_row)
        s = lax.dot_general(qb, kb, (((1,), (1,)), ((), ())),
                            preferred_element_type=F32) * w_intra
        ct = ct_ref[d]
        intra = jnp.dot(s.astype(BF16), jnp.concatenate([vf.astype(BF16), ones], axis=1),
                        preferred_element_type=F32)
        inter = jnp.dot(qb, ct.astype(BF16), preferred_element_type=F32)
        num = intra[:, :dv] + lanes2(w_inter) * inter[:, :dv]
        den = intra[:, dv:] + w_inter * inter[:, dv:]
        inv = 1.0 / jnp.maximum(jnp.abs(den), jnp.exp(-(b_col + m_row)))
        emit(r0, num * lanes2(inv), own_ref, other_ref)

        m_new = jnp.maximum(total + m, total + gmax_end)
        w_src = jnp.exp(total + gg_col - m_new)
        decay = jnp.exp(total + m - m_new)
        wv = jnp.concatenate([lanes2(w_src) * vf, w_src], axis=1).astype(BF16)
        ct_ref[d] = decay * ct + lax.dot_general(
            kb, wv, (((0,), (0,)), ((), ())), preferred_element_type=F32)
        return m_new

    def first_half(c, carry):
        m_f, m_b = carry
        m_f = chunk_step(c, m_f, 0, hf_ref, None)
        m_b = chunk_step(nc - 1 - c, m_b, 1, hb_ref, None)
        return m_f, m_b

    def second_half(c, carry):
        m_f, m_b = carry
        m_f = chunk_step(c, m_f, 0, None, hb_ref)
        m_b = chunk_step(nc - 1 - c, m_b, 1, None, hf_ref)
        return m_f, m_b

    m0 = jnp.full((1, 1), NEG_BIG, F32)
    carry = lax.fori_loop(0, nc // 2, first_half, (m0, m0), unroll=2)
    lax.fori_loop(nc // 2, nc, second_half, carry, unroll=2)


def mlstm_mixer(z, gates_rows, gate_b_rows, head_g, chunk=MLSTM_CHUNK):
    b, s, _ = z.shape
    h, dk, dv = MLSTM_HEADS, MLSTM_DK, MLSTM_DV
    nc = s // chunk
    assert s % chunk == 0 and 6 * nc <= LANES and chunk % LANES == 0 and nc % 2 == 0
    kq = h * dk // dk
    kv = 2 * h * dk // dv
    ko = kv + h
    return pl.pallas_call(
        functools.partial(_mlstm_body, nc=nc, chunk=chunk),
        grid=(b, h),
        in_specs=[
            pl.BlockSpec((1, s, dk), lambda i, j: (i, 0, j)),
            pl.BlockSpec((1, s, dk), lambda i, j: (i, 0, kq + j)),
            pl.BlockSpec((1, s, dv), lambda i, j: (i, 0, kv + j)),
            pl.BlockSpec((1, s, dv), lambda i, j: (i, 0, ko + j)),
            pl.BlockSpec((1, 1, 4, nc, chunk), lambda i, j: (i, j, 0, 0, 0)),
            pl.BlockSpec((1, 4, 1, chunk), lambda i, j: (j, 0, 0, 0)),
            pl.BlockSpec((1, 1, dv), lambda i, j: (j, 0, 0)),
        ],
        out_specs=pl.BlockSpec((1, s, dv), lambda i, j: (i, 0, j)),
        out_shape=jax.ShapeDtypeStruct((b, s, h * dv), BF16),
        scratch_shapes=[
            pltpu.VMEM((6, nc, chunk), F32),
            pltpu.VMEM((nc, chunk, LANES), F32),
            pltpu.VMEM((s, dv), F32),
            pltpu.VMEM((s, dv), F32),
            pltpu.VMEM((2, dk, dv + LANES), F32),
        ],
        compiler_params=_cparams("parallel", "parallel"),
        name="mlstm",
    )(z, z, z, z, gates_rows, gate_b_rows, head_g)


RG_TILE = 512
RG_PAD = SUBLANES
RG_SCAN_UNROLL = 8


def _softplus(x):
    return jnp.maximum(x, 0.0) + jnp.log1p(jnp.exp(-jnp.abs(x)))


def _sigmoid(x):
    return 0.5 * jnp.tanh(0.5 * x) + 0.5


def _gelu_tanh(x):
    c = math.sqrt(2.0 / math.pi)
    return x * (0.5 * (1.0 + jnp.tanh(c * (x + 0.044715 * (x * x * x)))))


def _rglru_body(x_ref, gr_ref, cw_ref, cb_ref, wa_ref, wx_ref, ba_ref, bx_ref, lam_ref, y_ref,
                xpad_ref, a_ref, u_ref, as_ref, us_ref, h_ref, *, s):
    nt = s // RG_TILE
    zeros = jnp.zeros((RG_PAD, LANES), F32)
    xpad_ref[pl.ds(0, RG_PAD), :] = zeros
    xpad_ref[pl.ds(RG_PAD + s, RG_PAD), :] = zeros

    def copy_in(t, carry):
        r0 = pl.multiple_of(t * RG_TILE, RG_TILE)
        xpad_ref[pl.ds(RG_PAD + r0, RG_TILE), :] = x_ref[0, pl.ds(r0, RG_TILE), :]
        return carry

    lax.fori_loop(0, nt, copy_in, 0)

    cw = cw_ref[...]
    cb = cb_ref[...]
    sp = [RGLRU_C * _softplus(-lam_ref[d:d + 1, :]) for d in range(2)]

    def gates(t, carry):
        r0 = pl.multiple_of(t * RG_TILE, RG_TILE)
        xc = None
        for j in range(CONV_WIDTH):
            tap = xpad_ref[pl.ds(r0 + RG_PAD - CONV_LEFT + j, RG_TILE), :] * cw[j:j + 1, :]
            xc = tap if xc is None else xc + tap
        xc = xc + cb
        xcb = xc.astype(BF16)
        for d in range(2):
            r = _sigmoid(jnp.dot(xcb, wa_ref[d, 0], preferred_element_type=F32)
                         + ba_ref[d:d + 1, :])
            i = _sigmoid(jnp.dot(xcb, wx_ref[d, 0], preferred_element_type=F32)
                         + bx_ref[d:d + 1, :])
            neg_log_a = sp[d] * r
            a = jnp.exp(-neg_log_a)
            one_minus_a2 = jnp.tanh(neg_log_a) * (1.0 + a * a)
            root = jnp.where(one_minus_a2 > 0.0, one_minus_a2 * lax.rsqrt(one_minus_a2), 0.0)
            a_ref[d, pl.ds(r0, RG_TILE), :] = a
            u_ref[d, pl.ds(r0, RG_TILE), :] = root * (i * xc)
        return carry

    def block_scan(t, carry):
        r0 = pl.multiple_of(t * RG_TILE, RG_TILE)
        for d in range(2):
            a_prev = u_prev = None
            for r in (range(SUBLANES) if d == 0 else reversed(range(SUBLANES))):
                rows = pl.ds(r0 + r, RG_TILE // SUBLANES, stride=SUBLANES)
                a, u = a_ref[d, rows, :], u_ref[d, rows, :]
                if a_prev is not None:
                    u = a * u_prev + u
                    a = a * a_prev
                as_ref[d, rows, :] = a
                us_ref[d, rows, :] = u
                a_prev, u_prev = a, u
        return carry

    lax.fori_loop(0, nt, gates, 0)
    lax.fori_loop(0, nt, block_scan, 0)

    nblk = s // SUBLANES

    def scan(j, carry):
        h_f, h_b = carry
        for d, last, h in ((0, SUBLANES - 1, h_f), (1, 0, h_b)):
            blk = j if d == 0 else nblk - 1 - j
            rows = pl.ds(pl.multiple_of(blk * SUBLANES, SUBLANES), SUBLANES)
            a, u = as_ref[d, rows, :], us_ref[d, rows, :]
            h_ref[d, rows, :] = a * h + u
            a_end = jnp.broadcast_to(a[last:last + 1, :], (SUBLANES, LANES))
            u_end = jnp.broadcast_to(u[last:last + 1, :], (SUBLANES, LANES))
            if d == 0:
                h_f = a_end * h + u_end
            else:
                h_b = a_end * h + u_end
        return h_f, h_b

    h0 = jnp.zeros((SUBLANES, LANES), F32)
    lax.fori_loop(0, nblk, scan, (h0, h0), unroll=RG_SCAN_UNROLL)

    def finish(t, carry):
        r0 = pl.multiple_of(t * RG_TILE, RG_TILE)
        hr = h_ref[0, pl.ds(r0, RG_TILE), :] + h_ref[1, pl.ds(r0, RG_TILE), :]
        y_ref[0, pl.ds(r0, RG_TILE), :] = (
            hr * _gelu_tanh(gr_ref[0, pl.ds(r0, RG_TILE), :])).astype(y_ref.dtype)
        return carry

    lax.fori_loop(0, nt, finish, 0)


def rglru_mixer(z, xr_block0, gr_block0, conv_w, conv_b, wa, wx, ba, bx, lam):
    b, s, _ = z.shape
    width = RNN_BLOCKS * RNN_BLOCK
    assert s % RG_TILE == 0
    return pl.pallas_call(
        functools.partial(_rglru_body, s=s),
        grid=(b, RNN_BLOCKS),
        in_specs=[
            pl.BlockSpec((1, s, RNN_BLOCK), lambda i, j: (i, 0, xr_block0 + j)),
            pl.BlockSpec((1, s, RNN_BLOCK), lambda i, j: (i, 0, gr_block0 + j)),
            pl.BlockSpec((CONV_WIDTH, RNN_BLOCK), lambda i, j: (0, j)),
            pl.BlockSpec((1, RNN_BLOCK), lambda i, j: (0, j)),
            pl.BlockSpec((2, 1, RNN_BLOCK, RNN_BLOCK), lambda i, j: (0, j, 0, 0)),
            pl.BlockSpec((2, 1, RNN_BLOCK, RNN_BLOCK), lambda i, j: (0, j, 0, 0)),
            pl.BlockSpec((2, RNN_BLOCK), lambda i, j: (0, j)),
            pl.BlockSpec((2, RNN_BLOCK), lambda i, j: (0, j)),
            pl.BlockSpec((2, RNN_BLOCK), lambda i, j: (0, j)),
        ],
        out_specs=pl.BlockSpec((1, s, RNN_BLOCK), lambda i, j: (i, 0, j)),
        out_shape=jax.ShapeDtypeStruct((b, s, width), BF16),
        scratch_shapes=[
            pltpu.VMEM((s + 2 * RG_PAD, RNN_BLOCK), F32),
            pltpu.VMEM((2, s, RNN_BLOCK), F32),
            pltpu.VMEM((2, s, RNN_BLOCK), F32),
            pltpu.VMEM((2, s, RNN_BLOCK), F32),
            pltpu.VMEM((2, s, RNN_BLOCK), F32),
            pltpu.VMEM((2, s, RNN_BLOCK), F32),
        ],
        compiler_params=_cparams("parallel", "parallel"),
        name="rglru",
    )(z, z, conv_w, conv_b, wa, wx, ba, bx, lam)


ATTN_TQ = 128
ATTN_TK = ATTN_TQ + 2 * ATTN_HALF
ATTN_UNROLL = 32
LOG2E = math.log2(math.e)


def _attn_body(q_ref, k_ref, v_ref, slope_ref, o_ref, bias_ref, x4_ref, og_ref, dg_ref, mg_ref,
               *, s):
    qscale = ATTN_DH ** -0.5 * LOG2E

    @pl.when(pl.program_id(1) == 0)
    def _():
        slope = slope_ref[0, 0:1, 0:1] * LOG2E
        qi = lax.broadcasted_iota(jnp.int32, (ATTN_TQ, ATTN_TK), 0)
        kj = lax.broadcasted_iota(jnp.int32, (ATTN_TQ, ATTN_TK), 1)
        for g, dil in enumerate(ATTN_DILATIONS):
            for e in range(3):
                rel = jnp.abs(kj - e * ATTN_HALF - qi)
                pen = slope * (rel * dil).astype(F32)
                bias_ref[g, e] = jnp.where(rel <= ATTN_HALF, -pen, NEG_BIG)

    s4 = s // 4
    srcs = (q_ref, k_ref, v_ref)

    def split4(t, carry):
        c = t // (s4 // ATTN_TK)
        p0 = (t % (s4 // ATTN_TK)) * ATTN_TK
        dst = pl.ds(pl.multiple_of(c * s4 + p0, ATTN_TK), ATTN_TK)
        for a in range(3):
            x = srcs[a][0, pl.ds(c + 4 * p0, ATTN_TK, stride=4), :]
            x4_ref[a, dst, :] = x * qscale if a == 0 else x
        return carry

    lax.fori_loop(0, 4 * (s4 // ATTN_TK), split4, 0)

    ones = jnp.ones((ATTN_TK, ATTN_DH), BF16)

    for g, dil in reversed(list(enumerate(ATTN_DILATIONS))):
        sp = s // dil
        nqb = sp // ATTN_TQ

        def q_block(t, carry, g=g, dil=dil, sp=sp, nqb=nqb):
            r = t // nqb
            p0 = (t % nqb) * ATTN_TQ
            kstart = jnp.clip(p0 - ATTN_HALF, 0, sp - ATTN_TK)
            e = (p0 - kstart) // ATTN_HALF
            if dil == 1:
                qrows = pl.ds(pl.multiple_of(p0, ATTN_TQ), ATTN_TQ)
                krows = pl.ds(pl.multiple_of(kstart, ATTN_HALF), ATTN_TK)
                qf = q_ref[0, qrows, :] * qscale
                kf, vf = k_ref[0, krows, :], v_ref[0, krows, :]
                orows = qrows
            elif dil == 4:
                qrows = pl.ds(pl.multiple_of(r * s4 + p0, ATTN_TQ), ATTN_TQ)
                krows = pl.ds(pl.multiple_of(r * s4 + kstart, ATTN_HALF), ATTN_TK)
                qf, kf, vf = x4_ref[0, qrows, :], x4_ref[1, krows, :], x4_ref[2, krows, :]
                orows = pl.ds(r + 4 * p0, ATTN_TQ, stride=4)
            else:
                base = (r % 4) * s4 + r // 4
                qrows = pl.ds(base + 4 * p0, ATTN_TQ, stride=4)
                krows = pl.ds(base + 4 * kstart, ATTN_TK, stride=4)
                qf, kf, vf = x4_ref[0, qrows, :], x4_ref[1, krows, :], x4_ref[2, krows, :]
                orows = pl.ds(r + dil * p0, ATTN_TQ, stride=dil)
            qb, kb, vb = qf.astype(BF16), kf.astype(BF16), vf.astype(BF16)
            sc = lax.dot_general(qb, kb, (((1,), (1,)), ((), ())),
                                 preferred_element_type=F32) + bias_ref[g, e]
            m = jnp.max(sc, axis=-1, keepdims=True)
            p = jnp.exp2(sc - m).astype(BF16)
            pv = jnp.dot(p, jnp.concatenate([vb, ones], axis=1), preferred_element_type=F32)
            acc, den = pv[:, :ATTN_DH], pv[:, ATTN_DH:]
            if dil != 1:
                og_ref[g - 1, orows, :] = acc
                dg_ref[g - 1, orows, :] = den
                mg_ref[g - 1, orows, :] = jnp.broadcast_to(m, den.shape)
                return carry
            m4, m16 = mg_ref[0, orows, :], mg_ref[1, orows, :]
            mx = jnp.maximum(jnp.maximum(m4, m16), m)
            e1, e4, e16 = jnp.exp2(m - mx), jnp.exp2(m4 - mx), jnp.exp2(m16 - mx)
            num = e1 * acc + e4 * og_ref[0, orows, :] + e16 * og_ref[1, orows, :]
            tot = e1 * den + e4 * dg_ref[0, orows, :] + e16 * dg_ref[1, orows, :]
            o_ref[0, orows, :] = (num * (1.0 / tot)).astype(o_ref.dtype)
            return carry

        lax.fori_loop(0, dil * nqb, q_block, 0, unroll=ATTN_UNROLL)


def dilated_attention(qkv, slopes):
    b, s, _ = qkv.shape
    h, dh = ATTN_HEADS, ATTN_DH
    assert s % (max(ATTN_DILATIONS) * ATTN_TK) == 0
    return pl.pallas_call(
        functools.partial(_attn_body, s=s),
        grid=(h, b),
        in_specs=[
            pl.BlockSpec((1, s, dh), lambda j, i: (i, 0, j)),
            pl.BlockSpec((1, s, dh), lambda j, i: (i, 0, h + j)),
            pl.BlockSpec((1, s, dh), lambda j, i: (i, 0, 2 * h + j)),
            pl.BlockSpec((1, SUBLANES, LANES), lambda j, i: (j, 0, 0)),
        ],
        out_specs=pl.BlockSpec((1, s, dh), lambda j, i: (i, 0, j)),
        out_shape=jax.ShapeDtypeStruct((b, s, h * dh), BF16),
        scratch_shapes=[
            pltpu.VMEM((len(ATTN_DILATIONS), 3, ATTN_TQ, ATTN_TK), F32),
            pltpu.VMEM((3, s, dh), F32),
            pltpu.VMEM((len(ATTN_DILATIONS) - 1, s, dh), F32),
            pltpu.VMEM((len(ATTN_DILATIONS) - 1, s, dh), F32),
            pltpu.VMEM((len(ATTN_DILATIONS) - 1, s, dh), F32),
        ],
        compiler_params=_cparams("parallel", "arbitrary"),
        name="dilated_attn",
    )(qkv, qkv, qkv, slopes)


MIXER_OUT_TM = 512


FFN_DOWN_TM = 256


def _ffn(x, h, w1, w3, w2, next_norm_g, last):
    u = swiglu_up(h, w1.astype(BF16), w3.astype(BF16))
    return matmul([(u, w2.astype(BF16))], res=x, norm_g=next_norm_g,
                  norm_dtype=F32 if last else BF16, keep_sum=not last,
                  tm=FFN_DOWN_TM, tn=w2.shape[1], name="ffn_down")


def _even_layer(x, bsz, seq, norm_g, w_in, gate_b, conv_w, conv_b, rg_wa, rg_ba, rg_wx, rg_bx,
                rg_lam, head_g, w_out, next_norm_g):
    heads, dk, dv = MLSTM_HEADS, MLSTM_DK, MLSTM_DV
    n_qkvo = 2 * heads * dk + 2 * heads * dv
    n_gate = 4 * heads
    width = RNN_BLOCKS * RNN_BLOCK
    w_main = jnp.concatenate([w_in[:, :n_qkvo], w_in[:, n_qkvo + n_gate:]], axis=1).astype(BF16)
    w_gate_t = jnp.transpose(w_in[:, n_qkvo:n_qkvo + n_gate]).astype(BF16)
    z, zg_t = norm_matmul(x, norm_g, w_main, w_side_t=w_gate_t, name="in_proj")
    z = z.reshape(bsz, seq, -1)

    chunk = MLSTM_CHUNK
    nc = seq // chunk
    gates_rows = jnp.transpose(zg_t[:n_gate].reshape(4, heads, bsz, nc, chunk), (2, 1, 0, 3, 4))
    gate_b_rows = jnp.broadcast_to(
        jnp.transpose(gate_b.astype(F32).reshape(4, heads))[:, :, None, None], (heads, 4, 1, chunk))
    y_a = mlstm_mixer(z, gates_rows, gate_b_rows, head_g.astype(F32).reshape(heads, 1, dv), chunk)

    y_b = rglru_mixer(z, n_qkvo // RNN_BLOCK, (n_qkvo + width) // RNN_BLOCK,
                      conv_w.astype(F32), conv_b.astype(F32).reshape(1, width),
                      rg_wa.astype(BF16), rg_wx.astype(BF16),
                      rg_ba.astype(F32), rg_bx.astype(F32), rg_lam.astype(F32))

    w_out_b = w_out.astype(BF16)
    m = bsz * seq
    assert heads * dv == width
    return matmul([(y_a.reshape(m, -1), w_out_b, 0),
                   (y_b.reshape(m, -1), w_out_b, 1)], res=x, norm_g=next_norm_g,
                  tm=MIXER_OUT_TM, tn=w_out.shape[1], name="out_proj")


def _odd_layer(x, h, bsz, seq, norm_g, w_qkv, w_o, next_norm_g):
    if h is None:
        qkv = norm_matmul(x, norm_g, w_qkv.astype(BF16), name="qkv_proj")
    else:
        qkv = matmul([(h, w_qkv.astype(BF16))], name="qkv_proj")
    qkv = qkv.reshape(bsz, seq, -1)
    slopes = jnp.exp2(-ALIBI_MAX_BIAS * jnp.arange(1, ATTN_HEADS + 1, dtype=F32) / ATTN_HEADS)
    slopes = jnp.broadcast_to(slopes[:, None, None], (ATTN_HEADS, SUBLANES, LANES))
    o = dilated_attention(qkv, slopes)
    return matmul([(o.reshape(bsz * seq, -1), w_o.astype(BF16))], res=x, norm_g=next_norm_g,
                  tm=MIXER_OUT_TM, tn=w_o.shape[1], name="attn_out_proj")


def kernel(x, e_norm, e_w_in, e_gate_b, e_conv_w, e_conv_b, e_rg_wa, e_rg_ba, e_rg_wx, e_rg_bx,
           e_rg_lam, e_head_g, e_w_out, o_norm, o_w_qkv, o_w_o, f_norm, f_w1, f_w3, f_w2,
           final_norm):
    bsz, seq, d = x.shape
    depth = f_norm.shape[0]
    xs = x.reshape(bsz * seq, d).astype(F32)
    hs = None
    for l in range(depth):
        if l % 2 == 0:
            e = l // 2
            xs, hs = _even_layer(xs, bsz, seq, e_norm[e], e_w_in[e], e_gate_b[e], e_conv_w[e],
                                 e_conv_b[e], e_rg_wa[e], e_rg_ba[e], e_rg_wx[e], e_rg_bx[e],
                                 e_rg_lam[e], e_head_g[e], e_w_out[e], f_norm[l])
        else:
            o = l // 2
            xs, hs = _odd_layer(xs, hs, bsz, seq, o_norm[o], o_w_qkv[o], o_w_o[o], f_norm[l])
        if l == depth - 1:
            out = _ffn(xs, hs, f_w1[l], f_w3[l], f_w2[l], final_norm, last=True)
            return out.astype(x.dtype).reshape(bsz, seq, d)
        if (l + 1) % 2:
            xs, hs = _ffn(xs, hs, f_w1[l], f_w3[l], f_w2[l], o_norm[(l + 1) // 2], last=False)
        else:
            xs, hs = _ffn(xs, hs, f_w1[l], f_w3[l], f_w2[l], None, last=False), None
```

```python
import functools
import math

import jax
import jax.numpy as jnp
from jax import lax
from jax.experimental import pallas as pl
from jax.experimental.pallas import tpu as pltpu

F32 = jnp.float32
BF16 = jnp.bfloat16

RMS_EPS = 1e-6
NEG_BIG = -1e30

MLSTM_HEADS = 4
MLSTM_DK = 128
MLSTM_DV = 256
MLSTM_CHUNK = 256
RNN_BLOCKS = 8
RNN_BLOCK = 128
CONV_WIDTH = 4
CONV_LEFT = 2
RGLRU_C = 8.0
ATTN_HEADS = 16
ATTN_DH = 128
ATTN_HALF = 64
ATTN_DILATIONS = (1, 4, 16)
ALIBI_MAX_BIAS = 8.0

LANES = 128
SUBLANES = 8
VMEM_LIMIT_BYTES = 56 * 1024 * 1024


def _cparams(*semantics):
    return pltpu.CompilerParams(dimension_semantics=semantics,
                                vmem_limit_bytes=VMEM_LIMIT_BYTES)


NORM_ROWS = 128


def _norm_rows_into(x_ref, g_ref, h_ref):
    g = g_ref[...]

    def rows(t, carry):
        sl = pl.ds(pl.multiple_of(t * NORM_ROWS, NORM_ROWS), NORM_ROWS)
        x = x_ref[sl, :]
        ms = jnp.mean(x * x, axis=-1, keepdims=True)
        h_ref[sl, :] = (x * lax.rsqrt(ms + RMS_EPS) * g).astype(h_ref.dtype)
        return carry

    lax.fori_loop(0, x_ref.shape[0] // NORM_ROWS, rows, 0)


def _weight_spec(w, kdim, tn, row_block=0, layer=None, **mode):
    if w.ndim == 2:
        return pl.BlockSpec((kdim, tn), lambda i, j: (row_block, j), **mode)
    return pl.BlockSpec((None, kdim, tn), lambda i, j: (layer, row_block, j), **mode)


def _matmul_body(*refs, n_pairs, has_res, has_norm, keep_sum):
    pairs = [(refs[2 * p], refs[2 * p + 1]) for p in range(n_pairs)]
    rest = list(refs[2 * n_pairs:])
    r_ref = rest.pop(0) if has_res else None
    g_ref = rest.pop(0) if has_norm else None
    y_ref = rest.pop(0) if keep_sum else rest.pop(-1)

    acc = None
    for a_ref, w_ref in pairs:
        d = jnp.dot(a_ref[...], w_ref[...], preferred_element_type=F32)
        acc = d if acc is None else acc + d
    if has_res:
        acc = acc + r_ref[...]
    y_ref[...] = acc.astype(y_ref.dtype)
    if has_norm:
        _norm_rows_into(y_ref, g_ref, rest.pop(0))


def matmul(pairs, res=None, norm_g=None, norm_dtype=BF16, keep_sum=True, tm=1024, tn=1024,
           name="matmul"):
    m = pairs[0][0].shape[0]
    n = pairs[0][1].shape[-1]
    tn = min(tn, n)
    assert m % tm == 0 and n % tn == 0 and (keep_sum or norm_g is not None)
    assert norm_g is None or (tn == n and tm % NORM_ROWS == 0)
    w_mode = dict(pipeline_mode=pl.Buffered(1)) if tn == n else {}
    in_specs, args = [], []
    for a, w, *where in pairs:
        kdim = a.shape[1]
        kb, layer = (list(where) + [0, None])[:2] if where else (0, None)
        in_specs.append(pl.BlockSpec((tm, kdim), lambda i, j: (i, 0)))
        in_specs.append(_weight_spec(w, kdim, tn, kb, layer, **w_mode))
        args += [a, w]
    if res is not None:
        in_specs.append(pl.BlockSpec((tm, tn), lambda i, j: (i, j)))
        args.append(res)
    out_specs, out_shape, scratch = [], [], []
    if keep_sum:
        out_specs.append(pl.BlockSpec((tm, tn), lambda i, j: (i, j)))
        out_shape.append(jax.ShapeDtypeStruct((m, n), F32))
    else:
        scratch.append(pltpu.VMEM((tm, n), F32))
    if norm_g is not None:
        in_specs.append(pl.BlockSpec((1, n), lambda i, j: (0, 0)))
        args.append(norm_g.reshape(1, n).astype(F32))
        out_specs.append(pl.BlockSpec((tm, n), lambda i, j: (i, 0)))
        out_shape.append(jax.ShapeDtypeStruct((m, n), norm_dtype))
    outs = pl.pallas_call(
        functools.partial(_matmul_body, n_pairs=len(pairs), has_res=res is not None,
                          has_norm=norm_g is not None, keep_sum=keep_sum),
        grid=(m // tm, n // tn),
        in_specs=in_specs,
        out_specs=out_specs,
        out_shape=out_shape,
        scratch_shapes=scratch,
        compiler_params=_cparams("parallel", "parallel"),
        name=name,
    )(*args)
    return outs if len(outs) > 1 else outs[0]


SWIGLU_ROWS = 1024


def _swiglu_up_body(a_ref, w1_ref, w3_ref, o_ref):
    rows = min(SWIGLU_ROWS, a_ref.shape[0])
    for r0 in range(0, a_ref.shape[0], rows):
        a = a_ref[pl.ds(r0, rows), :]
        u = jnp.dot(a, w1_ref[...], preferred_element_type=F32)
        v = jnp.dot(a, w3_ref[...], preferred_element_type=F32)
        o_ref[pl.ds(r0, rows), :] = (u * jax.nn.sigmoid(u) * v).astype(o_ref.dtype)


def swiglu_up(a, w1, w3, layer, tm=4096, tn=512):
    m, kdim = a.shape
    n = w1.shape[-1]
    assert m % tm == 0 and n % tn == 0
    return pl.pallas_call(
        _swiglu_up_body,
        grid=(m // tm, n // tn),
        in_specs=[pl.BlockSpec((tm, kdim), lambda i, j: (i, 0)),
                  _weight_spec(w1, kdim, tn, layer=layer),
                  _weight_spec(w3, kdim, tn, layer=layer)],
        out_specs=pl.BlockSpec((tm, tn), lambda i, j: (i, j)),
        out_shape=jax.ShapeDtypeStruct((m, n), BF16),
        compiler_params=_cparams("parallel", "parallel"),
        name="swiglu_up",
    )(a, w1, w3)


NORM_CHUNK = 256


def _norm_matmul_body(*refs, has_side):
    if has_side:
        x_ref, g_ref, w_ref, ws_ref, o_ref, os_ref, h_ref = refs
    else:
        x_ref, g_ref, w_ref, o_ref, h_ref = refs

    @pl.when(pl.program_id(1) == 0)
    def _():
        g = g_ref[...]
        for r0 in range(0, x_ref.shape[0], NORM_CHUNK):
            sl = pl.ds(r0, NORM_CHUNK)
            x = x_ref[sl, :]
            ms = jnp.mean(x * x, axis=-1, keepdims=True)
            h = (x * lax.rsqrt(ms + RMS_EPS) * g).astype(h_ref.dtype)
            h_ref[sl, :] = h
            o_ref[sl, :] = jnp.dot(h, w_ref[...], preferred_element_type=F32).astype(o_ref.dtype)
            if has_side:
                os_ref[:, sl] = lax.dot_general(ws_ref[...], h, (((1,), (1,)), ((), ())),
                                                preferred_element_type=F32)

    @pl.when(pl.program_id(1) > 0)
    def _():
        o_ref[...] = jnp.dot(h_ref[...], w_ref[...],
                             preferred_element_type=F32).astype(o_ref.dtype)


def norm_matmul(x, g, w, w_side_t=None, out_dtype=F32, tm=1024, tn=1024, name="norm_matmul"):
    m, d = x.shape
    n = w.shape[1]
    assert m % tm == 0 and n % tn == 0 and tm % NORM_CHUNK == 0
    in_specs = [pl.BlockSpec((tm, d), lambda i, j: (i, 0)),
                pl.BlockSpec((1, d), lambda i, j: (0, 0)),
                pl.BlockSpec((d, tn), lambda i, j: (0, j))]
    args = [x, g.reshape(1, d).astype(F32), w]
    out_specs = [pl.BlockSpec((tm, tn), lambda i, j: (i, j))]
    out_shape = [jax.ShapeDtypeStruct((m, n), out_dtype)]
    if w_side_t is not None:
        ns = w_side_t.shape[0]
        in_specs.append(pl.BlockSpec((ns, d), lambda i, j: (0, 0)))
        args.append(w_side_t)
        out_specs.append(pl.BlockSpec((ns, tm), lambda i, j: (0, i)))
        out_shape.append(jax.ShapeDtypeStruct((ns, m), F32))
    outs = pl.pallas_call(
        functools.partial(_norm_matmul_body, has_side=w_side_t is not None),
        grid=(m // tm, n // tn),
        in_specs=in_specs,
        out_specs=out_specs,
        out_shape=out_shape,
        scratch_shapes=[pltpu.VMEM((tm, d), BF16)],
        compiler_params=_cparams("parallel", "arbitrary"),
        name=name,
    )(*args)
    return outs if w_side_t is not None else outs[0]


def _log_sigmoid(x):
    return jnp.minimum(x, 0.0) - jnp.log1p(jnp.exp(-jnp.abs(x)))


SCAN_RADIX = 4


def _lane_scan(x, combine, identity, reverse):
    width = x.shape[-1]
    lane = lax.broadcasted_iota(jnp.int32, x.shape, 1)

    def shifted(v, sh):
        if reverse:
            return jnp.where(lane < width - sh, pltpu.roll(v, width - sh, axis=1), identity)
        return jnp.where(lane >= sh, pltpu.roll(v, sh, axis=1), identity)

    sh = 1
    while sh < width:
        parts = [shifted(x, k * sh) for k in range(1, SCAN_RADIX) if k * sh < width]
        for p in parts:
            x = combine(x, p)
        sh *= SCAN_RADIX
    return x


def _mlstm_body(q_ref, k_ref, v_ref, og_ref, g_ref, gb_ref, hg_ref, y_ref,
                rows_ref, cols_ref, hf_ref, hb_ref, ct_ref, *, nc, chunk):
    dk, dv = MLSTM_DK, MLSTM_DV
    scale = dk ** -0.5

    g = g_ref[0, 0] + gb_ref[0]
    rows = []
    for d in range(2):
        b = _lane_scan(_log_sigmoid(g[2 * d + 1]), jnp.add, 0.0, reverse=d == 1)
        gg = g[2 * d] - b
        gmax = _lane_scan(gg, jnp.maximum, -jnp.inf, reverse=d == 1)
        rows += [gg, b, gmax]
        for j, r in enumerate((gg, b, gmax)):
            rows_ref[3 * d + j] = r
    rowmat = jnp.concatenate(rows + [jnp.zeros((LANES - 6 * nc, chunk), F32)], axis=0)
    colmat = rowmat.T
    for c in range(nc):
        cols_ref[c] = colmat if c == 0 else pltpu.roll(colmat, LANES - c, axis=1)

    ct_ref[...] = jnp.zeros_like(ct_ref)

    row_id = lax.broadcasted_iota(jnp.int32, (chunk, chunk), 0)
    col_id = lax.broadcasted_iota(jnp.int32, (chunk, chunk), 1)
    causal = (row_id >= col_id, row_id <= col_id)
    ones = jnp.ones((chunk, LANES), BF16)

    def lanes2(x):
        return jnp.concatenate([x] * (dv // LANES), axis=1)

    hg = hg_ref[0]

    def emit(r0, h, own_ref, other_ref):
        rows = pl.ds(r0, chunk)
        if other_ref is None:
            own_ref[rows, :] = h
            return
        hm = h + other_ref[rows, :]
        ms = jnp.mean(hm * hm, axis=-1, keepdims=True)
        hn = hm * lax.rsqrt(ms + RMS_EPS) * hg
        y_ref[0, rows, :] = (hn * jax.nn.sigmoid(og_ref[0, rows, :])).astype(y_ref.dtype)

    def chunk_step(c, m, d, own_ref, other_ref):
        r0 = pl.multiple_of(c * chunk, chunk)
        qb = (q_ref[0, pl.ds(r0, chunk), :] * scale).astype(BF16)
        kb = k_ref[0, pl.ds(r0, chunk), :].astype(BF16)
        vf = v_ref[0, pl.ds(r0, chunk), :]
        cols = cols_ref[c]

        def column(j):
            lane = (3 * d + j) * nc
            return jnp.broadcast_to(cols[:, lane:lane + 1], (chunk, LANES))

        gg_col, b_col, gmax_col = column(0), column(1), column(2)
        gg_row = rows_ref[3 * d, pl.ds(c, 1), :]
        end = chunk - 1 if d == 0 else 0
        total = rows_ref[3 * d + 1, pl.ds(c, 1), :][:, end:end + 1]
        gmax_end = rows_ref[3 * d + 2, pl.ds(c, 1), :][:, end:end + 1]

        m_row = jnp.maximum(gmax_col, m)
        w_intra = jnp.exp(jnp.where(causal[d], gg_row - lanes2(m_row), NEG_BIG))
        w_inter = jnp.exp(m - m_row)
        s = lax.dot_general(qb, kb, (((1,), (1,)), ((), ())),
                            preferred_element_type=F32) * w_intra
        ct = ct_ref[d]
        intra = jnp.dot(s.astype(BF16), jnp.concatenate([vf.astype(BF16), ones], axis=1),
                        preferred_element_type=F32)
        inter = jnp.dot(qb, ct.astype(BF16), preferred_element_type=F32)
        num = intra[:, :dv] + lanes2(w_inter) * inter[:, :dv]
        den = intra[:, dv:] + w_inter * inter[:, dv:]
        inv = 1.0 / jnp.maximum(jnp.abs(den), jnp.exp(-(b_col + m_row)))
        emit(r0, num * lanes2(inv), own_ref, other_ref)

        m_new = jnp.maximum(total + m, total + gmax_end)
        w_src = jnp.exp(total + gg_col - m_new)
        decay = jnp.exp(total + m - m_new)
        wv = jnp.concatenate([lanes2(w_src) * vf, w_src], axis=1).astype(BF16)
        ct_ref[d] = decay * ct + lax.dot_general(
            kb, wv, (((0,), (0,)), ((), ())), preferred_element_type=F32)
        return m_new

    def first_half(c, carry):
        m_f, m_b = carry
        m_f = chunk_step(c, m_f, 0, hf_ref, None)
        m_b = chunk_step(nc - 1 - c, m_b, 1, hb_ref, None)
        return m_f, m_b

    def second_half(c, carry):
        m_f, m_b = carry
        m_f = chunk_step(c, m_f, 0, None, hb_ref)
        m_b = chunk_step(nc - 1 - c, m_b, 1, None, hf_ref)
        return m_f, m_b

    m0 = jnp.full((1, 1), NEG_BIG, F32)
    carry = lax.fori_loop(0, nc // 2, first_half, (m0, m0), unroll=2)
    lax.fori_loop(nc // 2, nc, second_half, carry, unroll=2)


def mlstm_mixer(z, gates_rows, gate_b_rows, head_g, chunk=MLSTM_CHUNK):
    b, s, _ = z.shape
    h, dk, dv = MLSTM_HEADS, MLSTM_DK, MLSTM_DV
    nc = s // chunk
    assert s % chunk == 0 and 6 * nc <= LANES and chunk % LANES == 0 and nc % 2 == 0
    kq = h * dk // dk
    kv = 2 * h * dk // dv
    ko = kv + h
    return pl.pallas_call(
        functools.partial(_mlstm_body, nc=nc, chunk=chunk),
        grid=(b, h),
        in_specs=[
            pl.BlockSpec((1, s, dk), lambda i, j: (i, 0, j)),
            pl.BlockSpec((1, s, dk), lambda i, j: (i, 0, kq + j)),
            pl.BlockSpec((1, s, dv), lambda i, j: (i, 0, kv + j)),
            pl.BlockSpec((1, s, dv), lambda i, j: (i, 0, ko + j)),
            pl.BlockSpec((1, 1, 4, nc, chunk), lambda i, j: (i, j, 0, 0, 0)),
            pl.BlockSpec((1, 4, 1, chunk), lambda i, j: (j, 0, 0, 0)),
            pl.BlockSpec((1, 1, dv), lambda i, j: (j, 0, 0)),
        ],
        out_specs=pl.BlockSpec((1, s, dv), lambda i, j: (i, 0, j)),
        out_shape=jax.ShapeDtypeStruct((b, s, h * dv), BF16),
        scratch_shapes=[
            pltpu.VMEM((6, nc, chunk), F32),
            pltpu.VMEM((nc, chunk, LANES), F32),
            pltpu.VMEM((s, dv), F32),
            pltpu.VMEM((s, dv), F32),
            pltpu.VMEM((2, dk, dv + LANES), F32),
        ],
        compiler_params=_cparams("parallel", "parallel"),
        name="mlstm",
    )(z, z, z, z, gates_rows, gate_b_rows, head_g)


RG_TILE = 512
RG_PAD = SUBLANES
RG_SCAN_UNROLL = 8


def _softplus(x):
    return jnp.maximum(x, 0.0) + jnp.log1p(jnp.exp(-jnp.abs(x)))


def _sigmoid(x):
    return 0.5 * jnp.tanh(0.5 * x) + 0.5


def _gelu_tanh(x):
    c = math.sqrt(2.0 / math.pi)
    return x * (0.5 * (1.0 + jnp.tanh(c * (x + 0.044715 * (x * x * x)))))


def _rglru_body(x_ref, gr_ref, cw_ref, cb_ref, wa_ref, wx_ref, ba_ref, bx_ref, lam_ref, y_ref,
                xpad_ref, a_ref, u_ref, as_ref, us_ref, h_ref, *, s):
    nt = s // RG_TILE
    zeros = jnp.zeros((RG_PAD, LANES), F32)
    xpad_ref[pl.ds(0, RG_PAD), :] = zeros
    xpad_ref[pl.ds(RG_PAD + s, RG_PAD), :] = zeros

    def copy_in(t, carry):
        r0 = pl.multiple_of(t * RG_TILE, RG_TILE)
        xpad_ref[pl.ds(RG_PAD + r0, RG_TILE), :] = x_ref[0, pl.ds(r0, RG_TILE), :]
        return carry

    lax.fori_loop(0, nt, copy_in, 0)

    cw = cw_ref[...]
    cb = cb_ref[...]
    sp = [RGLRU_C * _softplus(-lam_ref[d:d + 1, :]) for d in range(2)]

    def gates(t, carry):
        r0 = pl.multiple_of(t * RG_TILE, RG_TILE)
        xc = None
        for j in range(CONV_WIDTH):
            tap = xpad_ref[pl.ds(r0 + RG_PAD - CONV_LEFT + j, RG_TILE), :] * cw[j:j + 1, :]
            xc = tap if xc is None else xc + tap
        xc = xc + cb
        xcb = xc.astype(BF16)
        for d in range(2):
            r = _sigmoid(jnp.dot(xcb, wa_ref[d, 0], preferred_element_type=F32)
                         + ba_ref[d:d + 1, :])
            i = _sigmoid(jnp.dot(xcb, wx_ref[d, 0], preferred_element_type=F32)
                         + bx_ref[d:d + 1, :])
            neg_log_a = sp[d] * r
            a = jnp.exp(-neg_log_a)
            one_minus_a2 = jnp.tanh(neg_log_a) * (1.0 + a * a)
            root = jnp.where(one_minus_a2 > 0.0, one_minus_a2 * lax.rsqrt(one_minus_a2), 0.0)
            a_ref[d, pl.ds(r0, RG_TILE), :] = a
            u_ref[d, pl.ds(r0, RG_TILE), :] = root * (i * xc)
        return carry

    def block_scan(t, carry):
        r0 = pl.multiple_of(t * RG_TILE, RG_TILE)
        for d in range(2):
            a_prev = u_prev = None
            for r in (range(SUBLANES) if d == 0 else reversed(range(SUBLANES))):
                rows = pl.ds(r0 + r, RG_TILE // SUBLANES, stride=SUBLANES)
                a, u = a_ref[d, rows, :], u_ref[d, rows, :]
                if a_prev is not None:
                    u = a * u_prev + u
                    a = a * a_prev
                as_ref[d, rows, :] = a
                us_ref[d, rows, :] = u
                a_prev, u_prev = a, u
        return carry

    lax.fori_loop(0, nt, gates, 0)
    lax.fori_loop(0, nt, block_scan, 0)

    nblk = s // SUBLANES

    def scan(j, carry):
        h_f, h_b = carry
        for d, last, h in ((0, SUBLANES - 1, h_f), (1, 0, h_b)):
            blk = j if d == 0 else nblk - 1 - j
            rows = pl.ds(pl.multiple_of(blk * SUBLANES, SUBLANES), SUBLANES)
            a, u = as_ref[d, rows, :], us_ref[d, rows, :]
            h_ref[d, rows, :] = a * h + u
            a_end = jnp.broadcast_to(a[last:last + 1, :], (SUBLANES, LANES))
            u_end = jnp.broadcast_to(u[last:last + 1, :], (SUBLANES, LANES))
            if d == 0:
                h_f = a_end * h + u_end
            else:
                h_b = a_end * h + u_end
        return h_f, h_b

    h0 = jnp.zeros((SUBLANES, LANES), F32)
    lax.fori_loop(0, nblk, scan, (h0, h0), unroll=RG_SCAN_UNROLL)

    def finish(t, carry):
        r0 = pl.multiple_of(t * RG_TILE, RG_TILE)
        hr = h_ref[0, pl.ds(r0, RG_TILE), :] + h_ref[1, pl.ds(r0, RG_TILE), :]
        y_ref[0, pl.ds(r0, RG_TILE), :] = (
            hr * _gelu_tanh(gr_ref[0, pl.ds(r0, RG_TILE), :])).astype(y_ref.dtype)
        return carry

    lax.fori_loop(0, nt, finish, 0)


def rglru_mixer(z, xr_block0, gr_block0, conv_w, conv_b, wa, wx, ba, bx, lam):
    b, s, _ = z.shape
    width = RNN_BLOCKS * RNN_BLOCK
    assert s % RG_TILE == 0
    return pl.pallas_call(
        functools.partial(_rglru_body, s=s),
        grid=(b, RNN_BLOCKS),
        in_specs=[
            pl.BlockSpec((1, s, RNN_BLOCK), lambda i, j: (i, 0, xr_block0 + j)),
            pl.BlockSpec((1, s, RNN_BLOCK), lambda i, j: (i, 0, gr_block0 + j)),
            pl.BlockSpec((CONV_WIDTH, RNN_BLOCK), lambda i, j: (0, j)),
            pl.BlockSpec((1, RNN_BLOCK), lambda i, j: (0, j)),
            pl.BlockSpec((2, 1, RNN_BLOCK, RNN_BLOCK), lambda i, j: (0, j, 0, 0)),
            pl.BlockSpec((2, 1, RNN_BLOCK, RNN_BLOCK), lambda i, j: (0, j, 0, 0)),
            pl.BlockSpec((2, RNN_BLOCK), lambda i, j: (0, j)),
            pl.BlockSpec((2, RNN_BLOCK), lambda i, j: (0, j)),
            pl.BlockSpec((2, RNN_BLOCK), lambda i, j: (0, j)),
        ],
        out_specs=pl.BlockSpec((1, s, RNN_BLOCK), lambda i, j: (i, 0, j)),
        out_shape=jax.ShapeDtypeStruct((b, s, width), BF16),
        scratch_shapes=[
            pltpu.VMEM((s + 2 * RG_PAD, RNN_BLOCK), F32),
            pltpu.VMEM((2, s, RNN_BLOCK), F32),
            pltpu.VMEM((2, s, RNN_BLOCK), F32),
            pltpu.VMEM((2, s, RNN_BLOCK), F32),
            pltpu.VMEM((2, s, RNN_BLOCK), F32),
            pltpu.VMEM((2, s, RNN_BLOCK), F32),
        ],
        compiler_params=_cparams("parallel", "parallel"),
        name="rglru",
    )(z, z, conv_w, conv_b, wa, wx, ba, bx, lam)


ATTN_TQ = 128
ATTN_TK = ATTN_TQ + 2 * ATTN_HALF
ATTN_UNROLL = 32
LOG2E = math.log2(math.e)


def _attn_body(q_ref, k_ref, v_ref, slope_ref, o_ref, bias_ref, x4_ref, og_ref, dg_ref, mg_ref,
               *, s):
    qscale = ATTN_DH ** -0.5 * LOG2E

    @pl.when(pl.program_id(1) == 0)
    def _():
        slope = slope_ref[0, 0:1, 0:1] * LOG2E
        qi = lax.broadcasted_iota(jnp.int32, (ATTN_TQ, ATTN_TK), 0)
        kj = lax.broadcasted_iota(jnp.int32, (ATTN_TQ, ATTN_TK), 1)
        for g, dil in enumerate(ATTN_DILATIONS):
            for e in range(3):
                rel = jnp.abs(kj - e * ATTN_HALF - qi)
                pen = slope * (rel * dil).astype(F32)
                bias_ref[g, e] = jnp.where(rel <= ATTN_HALF, -pen, NEG_BIG)

    s4 = s // 4
    srcs = (q_ref, k_ref, v_ref)

    def split4(t, carry):
        c = t // (s4 // ATTN_TK)
        p0 = (t % (s4 // ATTN_TK)) * ATTN_TK
        dst = pl.ds(pl.multiple_of(c * s4 + p0, ATTN_TK), ATTN_TK)
        for a in range(3):
            x = srcs[a][0, pl.ds(c + 4 * p0, ATTN_TK, stride=4), :]
            x4_ref[a, dst, :] = x * qscale if a == 0 else x
        return carry

    lax.fori_loop(0, 4 * (s4 // ATTN_TK), split4, 0)

    ones = jnp.ones((ATTN_TK, ATTN_DH), BF16)

    for g, dil in reversed(list(enumerate(ATTN_DILATIONS))):
        sp = s // dil
        nqb = sp // ATTN_TQ

        def q_block(t, carry, g=g, dil=dil, sp=sp, nqb=nqb):
            r = t // nqb
            p0 = (t % nqb) * ATTN_TQ
            kstart = jnp.clip(p0 - ATTN_HALF, 0, sp - ATTN_TK)
            e = (p0 - kstart) // ATTN_HALF
            if dil == 1:
                qrows = pl.ds(pl.multiple_of(p0, ATTN_TQ), ATTN_TQ)
                krows = pl.ds(pl.multiple_of(kstart, ATTN_HALF), ATTN_TK)
                qf = q_ref[0, qrows, :] * qscale
                kf, vf = k_ref[0, krows, :], v_ref[0, krows, :]
                orows = qrows
            elif dil == 4:
                qrows = pl.ds(pl.multiple_of(r * s4 + p0, ATTN_TQ), ATTN_TQ)
                krows = pl.ds(pl.multiple_of(r * s4 + kstart, ATTN_HALF), ATTN_TK)
                qf, kf, vf = x4_ref[0, qrows, :], x4_ref[1, krows, :], x4_ref[2, krows, :]
                orows = pl.ds(r + 4 * p0, ATTN_TQ, stride=4)
            else:
                base = (r % 4) * s4 + r // 4
                qrows = pl.ds(base + 4 * p0, ATTN_TQ, stride=4)
                krows = pl.ds(base + 4 * kstart, ATTN_TK, stride=4)
                qf, kf, vf = x4_ref[0, qrows, :], x4_ref[1, krows, :], x4_ref[2, krows, :]
                orows = pl.ds(r + dil * p0, ATTN_TQ, stride=dil)
            qb, kb, vb = qf.astype(BF16), kf.astype(BF16), vf.astype(BF16)
            sc = lax.dot_general(qb, kb, (((1,), (1,)), ((), ())),
                                 preferred_element_type=F32) + bias_ref[g, e]
            m = jnp.max(sc, axis=-1, keepdims=True)
            p = jnp.exp2(sc - m).astype(BF16)
            pv = jnp.dot(p, jnp.concatenate([vb, ones], axis=1), preferred_element_type=F32)
            acc, den = pv[:, :ATTN_DH], pv[:, ATTN_DH:]
            if dil != 1:
                og_ref[g - 1, orows, :] = acc
                dg_ref[g - 1, orows, :] = den
                mg_ref[g - 1, orows, :] = jnp.broadcast_to(m, den.shape)
                return carry
            m4, m16 = mg_ref[0, orows, :], mg_ref[1, orows, :]
            mx = jnp.maximum(jnp.maximum(m4, m16), m)
            e1, e4, e16 = jnp.exp2(m - mx), jnp.exp2(m4 - mx), jnp.exp2(m16 - mx)
            num = e1 * acc + e4 * og_ref[0, orows, :] + e16 * og_ref[1, orows, :]
            tot = e1 * den + e4 * dg_ref[0, orows, :] + e16 * dg_ref[1, orows, :]
            o_ref[0, orows, :] = (num * (1.0 / tot)).astype(o_ref.dtype)
            return carry

        lax.fori_loop(0, dil * nqb, q_block, 0, unroll=ATTN_UNROLL)


def dilated_attention(qkv, slopes):
    b, s, _ = qkv.shape
    h, dh = ATTN_HEADS, ATTN_DH
    assert s % (max(ATTN_DILATIONS) * ATTN_TK) == 0
    return pl.pallas_call(
        functools.partial(_attn_body, s=s),
        grid=(h, b),
        in_specs=[
            pl.BlockSpec((1, s, dh), lambda j, i: (i, 0, j)),
            pl.BlockSpec((1, s, dh), lambda j, i: (i, 0, h + j)),
            pl.BlockSpec((1, s, dh), lambda j, i: (i, 0, 2 * h + j)),
            pl.BlockSpec((1, SUBLANES, LANES), lambda j, i: (j, 0, 0)),
        ],
        out_specs=pl.BlockSpec((1, s, dh), lambda j, i: (i, 0, j)),
        out_shape=jax.ShapeDtypeStruct((b, s, h * dh), BF16),
        scratch_shapes=[
            pltpu.VMEM((len(ATTN_DILATIONS), 3, ATTN_TQ, ATTN_TK), F32),
            pltpu.VMEM((3, s, dh), F32),
            pltpu.VMEM((len(ATTN_DILATIONS) - 1, s, dh), F32),
            pltpu.VMEM((len(ATTN_DILATIONS) - 1, s, dh), F32),
            pltpu.VMEM((len(ATTN_DILATIONS) - 1, s, dh), F32),
        ],
        compiler_params=_cparams("parallel", "arbitrary"),
        name="dilated_attn",
    )(qkv, qkv, qkv, slopes)


MIXER_OUT_TM = 512


FFN_DOWN_TM = 256


def _ffn(x, h, w1, w3, w2, layer, next_norm_g, last):
    u = swiglu_up(h, w1, w3, layer)
    return matmul([(u, w2, 0, layer)], res=x, norm_g=next_norm_g,
                  norm_dtype=F32 if last else BF16, keep_sum=not last,
                  tm=FFN_DOWN_TM, tn=w2.shape[-1], name="ffn_down")


def _even_layer(x, bsz, seq, norm_g, w_in, gate_b, conv_w, conv_b, rg_wa, rg_ba, rg_wx, rg_bx,
                rg_lam, head_g, w_out, next_norm_g):
    heads, dk, dv = MLSTM_HEADS, MLSTM_DK, MLSTM_DV
    n_qkvo = 2 * heads * dk + 2 * heads * dv
    n_gate = 4 * heads
    width = RNN_BLOCKS * RNN_BLOCK
    w_in_b = lax.optimization_barrier(w_in.astype(BF16))
    w_main = jnp.concatenate([w_in_b[:, :n_qkvo], w_in_b[:, n_qkvo + n_gate:]], axis=1)
    w_gate_t = jnp.transpose(w_in_b[:, n_qkvo:n_qkvo + n_gate])
    z, zg_t = norm_matmul(x, norm_g, w_main, w_side_t=w_gate_t, name="in_proj")
    z = z.reshape(bsz, seq, -1)

    chunk = MLSTM_CHUNK
    nc = seq // chunk
    gates_rows = jnp.transpose(zg_t[:n_gate].reshape(4, heads, bsz, nc, chunk), (2, 1, 0, 3, 4))
    gate_b_rows = jnp.broadcast_to(
        jnp.transpose(gate_b.astype(F32).reshape(4, heads))[:, :, None, None], (heads, 4, 1, chunk))
    y_a = mlstm_mixer(z, gates_rows, gate_b_rows, head_g.astype(F32).reshape(heads, 1, dv), chunk)

    y_b = rglru_mixer(z, n_qkvo // RNN_BLOCK, (n_qkvo + width) // RNN_BLOCK,
                      conv_w.astype(F32), conv_b.astype(F32).reshape(1, width),
                      rg_wa.astype(BF16), rg_wx.astype(BF16),
                      rg_ba.astype(F32), rg_bx.astype(F32), rg_lam.astype(F32))

    w_out_b = w_out.astype(BF16)
    m = bsz * seq
    assert heads * dv == width
    return matmul([(y_a.reshape(m, -1), w_out_b, 0),
                   (y_b.reshape(m, -1), w_out_b, 1)], res=x, norm_g=next_norm_g,
                  tm=MIXER_OUT_TM, tn=w_out.shape[1], name="out_proj")


def _odd_layer(x, h, bsz, seq, norm_g, w_qkv, w_o, next_norm_g):
    if h is None:
        qkv = norm_matmul(x, norm_g, w_qkv.astype(BF16), name="qkv_proj")
    else:
        qkv = matmul([(h, w_qkv.astype(BF16))], name="qkv_proj")
    qkv = qkv.reshape(bsz, seq, -1)
    slopes = jnp.exp2(-ALIBI_MAX_BIAS * jnp.arange(1, ATTN_HEADS + 1, dtype=F32) / ATTN_HEADS)
    slopes = jnp.broadcast_to(slopes[:, None, None], (ATTN_HEADS, SUBLANES, LANES))
    o = dilated_attention(qkv, slopes)
    return matmul([(o.reshape(bsz * seq, -1), w_o.astype(BF16))], res=x, norm_g=next_norm_g,
                  tm=MIXER_OUT_TM, tn=w_o.shape[1], name="attn_out_proj")


def kernel(x, e_norm, e_w_in, e_gate_b, e_conv_w, e_conv_b, e_rg_wa, e_rg_ba, e_rg_wx, e_rg_bx,
           e_rg_lam, e_head_g, e_w_out, o_norm, o_w_qkv, o_w_o, f_norm, f_w1, f_w3, f_w2,
           final_norm):
    bsz, seq, d = x.shape
    depth = f_norm.shape[0]
    xs = x.reshape(bsz * seq, d).astype(F32)
    w1, w3, w2 = f_w1.astype(BF16), f_w3.astype(BF16), f_w2.astype(BF16)
    hs = None
    for l in range(depth):
        if l % 2 == 0:
            e = l // 2
            xs, hs = _even_layer(xs, bsz, seq, e_norm[e], e_w_in[e], e_gate_b[e], e_conv_w[e],
                                 e_conv_b[e], e_rg_wa[e], e_rg_ba[e], e_rg_wx[e], e_rg_bx[e],
                                 e_rg_lam[e], e_head_g[e], e_w_out[e], f_norm[l])
        else:
            o = l // 2
            xs, hs = _odd_layer(xs, hs, bsz, seq, o_norm[o], o_w_qkv[o], o_w_o[o], f_norm[l])
        if l == depth - 1:
            out = _ffn(xs, hs, w1, w3, w2, l, final_norm, last=True)
            return out.astype(x.dtype).reshape(bsz, seq, d)
        if (l + 1) % 2:
            xs, hs = _ffn(xs, hs, w1, w3, w2, l, o_norm[(l + 1) // 2], last=False)
        else:
            xs, hs = _ffn(xs, hs, w1, w3, w2, l, None, last=False), None
```

```python
import functools
import math

import jax
import jax.numpy as jnp
from jax import lax
from jax.experimental import pallas as pl
from jax.experimental.pallas import tpu as pltpu

F32 = jnp.float32
BF16 = jnp.bfloat16

RMS_EPS = 1e-6
NEG_BIG = -1e30

MLSTM_HEADS = 4
MLSTM_DK = 128
MLSTM_DV = 256
MLSTM_CHUNK = 256
RNN_BLOCKS = 8
RNN_BLOCK = 128
CONV_WIDTH = 4
CONV_LEFT = 2
RGLRU_C = 8.0
ATTN_HEADS = 16
ATTN_DH = 128
ATTN_HALF = 64
ATTN_DILATIONS = (1, 4, 16)
ALIBI_MAX_BIAS = 8.0

LANES = 128
SUBLANES = 8
VMEM_LIMIT_BYTES = 56 * 1024 * 1024


def _cparams(*semantics):
    return pltpu.CompilerParams(dimension_semantics=semantics,
                                vmem_limit_bytes=VMEM_LIMIT_BYTES)


NORM_ROWS = 128


def _norm_rows_into(x_ref, g_ref, h_ref):
    g = g_ref[...]

    def rows(t, carry):
        sl = pl.ds(pl.multiple_of(t * NORM_ROWS, NORM_ROWS), NORM_ROWS)
        x = x_ref[sl, :]
        ms = jnp.mean(x * x, axis=-1, keepdims=True)
        h_ref[sl, :] = (x * lax.rsqrt(ms + RMS_EPS) * g).astype(h_ref.dtype)
        return carry

    lax.fori_loop(0, x_ref.shape[0] // NORM_ROWS, rows, 0)


def _weight_spec(w, kdim, tn, row_block=0, layer=None, **mode):
    if w.ndim == 2:
        return pl.BlockSpec((kdim, tn), lambda i, j: (row_block, j), **mode)
    return pl.BlockSpec((None, kdim, tn), lambda i, j: (layer, row_block, j), **mode)


def _matmul_body(*refs, n_pairs, has_res, has_norm, keep_sum):
    pairs = [(refs[2 * p], refs[2 * p + 1]) for p in range(n_pairs)]
    rest = list(refs[2 * n_pairs:])
    r_ref = rest.pop(0) if has_res else None
    g_ref = rest.pop(0) if has_norm else None
    y_ref = rest.pop(0) if keep_sum else rest.pop(-1)

    acc = None
    for a_ref, w_ref in pairs:
        d = jnp.dot(a_ref[...], w_ref[...], preferred_element_type=F32)
        acc = d if acc is None else acc + d
    if has_res:
        acc = acc + r_ref[...]
    y_ref[...] = acc.astype(y_ref.dtype)
    if has_norm:
        _norm_rows_into(y_ref, g_ref, rest.pop(0))


def matmul(pairs, res=None, norm_g=None, norm_dtype=BF16, keep_sum=True, tm=1024, tn=1024,
           name="matmul"):
    m = pairs[0][0].shape[0]
    n = pairs[0][1].shape[-1]
    tn = min(tn, n)
    assert m % tm == 0 and n % tn == 0 and (keep_sum or norm_g is not None)
    assert norm_g is None or (tn == n and tm % NORM_ROWS == 0)
    w_mode = dict(pipeline_mode=pl.Buffered(1)) if tn == n else {}
    in_specs, args = [], []
    for a, w, *where in pairs:
        kdim = a.shape[1]
        kb, layer = (list(where) + [0, None])[:2] if where else (0, None)
        in_specs.append(pl.BlockSpec((tm, kdim), lambda i, j: (i, 0)))
        in_specs.append(_weight_spec(w, kdim, tn, kb, layer, **w_mode))
        args += [a, w]
    if res is not None:
        in_specs.append(pl.BlockSpec((tm, tn), lambda i, j: (i, j)))
        args.append(res)
    out_specs, out_shape, scratch = [], [], []
    if keep_sum:
        out_specs.append(pl.BlockSpec((tm, tn), lambda i, j: (i, j)))
        out_shape.append(jax.ShapeDtypeStruct((m, n), F32))
    else:
        scratch.append(pltpu.VMEM((tm, n), F32))
    if norm_g is not None:
        in_specs.append(pl.BlockSpec((1, n), lambda i, j: (0, 0)))
        args.append(norm_g.reshape(1, n).astype(F32))
        out_specs.append(pl.BlockSpec((tm, n), lambda i, j: (i, 0)))
        out_shape.append(jax.ShapeDtypeStruct((m, n), norm_dtype))
    outs = pl.pallas_call(
        functools.partial(_matmul_body, n_pairs=len(pairs), has_res=res is not None,
                          has_norm=norm_g is not None, keep_sum=keep_sum),
        grid=(m // tm, n // tn),
        in_specs=in_specs,
        out_specs=out_specs,
        out_shape=out_shape,
        scratch_shapes=scratch,
        compiler_params=_cparams("parallel", "parallel"),
        name=name,
    )(*args)
    return outs if len(outs) > 1 else outs[0]


SWIGLU_ROWS = 1024


def _swiglu_up_body(a_ref, w1_ref, w3_ref, o_ref):
    rows = min(SWIGLU_ROWS, a_ref.shape[0])
    for r0 in range(0, a_ref.shape[0], rows):
        a = a_ref[pl.ds(r0, rows), :]
        u = jnp.dot(a, w1_ref[...], preferred_element_type=F32)
        v = jnp.dot(a, w3_ref[...], preferred_element_type=F32)
        o_ref[pl.ds(r0, rows), :] = (u * jax.nn.sigmoid(u) * v).astype(o_ref.dtype)


def swiglu_up(a, w1, w3, layer, tm=4096, tn=512):
    m, kdim = a.shape
    n = w1.shape[-1]
    assert m % tm == 0 and n % tn == 0
    return pl.pallas_call(
        _swiglu_up_body,
        grid=(m // tm, n // tn),
        in_specs=[pl.BlockSpec((tm, kdim), lambda i, j: (i, 0)),
                  _weight_spec(w1, kdim, tn, layer=layer),
                  _weight_spec(w3, kdim, tn, layer=layer)],
        out_specs=pl.BlockSpec((tm, tn), lambda i, j: (i, j)),
        out_shape=jax.ShapeDtypeStruct((m, n), BF16),
        compiler_params=_cparams("parallel", "parallel"),
        name="swiglu_up",
    )(a, w1, w3)


NORM_CHUNK = 256


def _norm_matmul_body(*refs, spans, has_side):
    x_ref, g_ref = refs[:2]
    w_refs = refs[2:2 + len(spans)]
    rest = list(refs[2 + len(spans):])
    ws_ref = rest.pop(0) if has_side else None
    o_ref = rest.pop(0)
    os_ref = rest.pop(0) if has_side else None
    h_ref = rest.pop(0)
    j = pl.program_id(1)

    @pl.when(j == 0)
    def _():
        g = g_ref[...]
        for r0 in range(0, x_ref.shape[0], NORM_CHUNK):
            sl = pl.ds(r0, NORM_CHUNK)
            x = x_ref[sl, :]
            ms = jnp.mean(x * x, axis=-1, keepdims=True)
            h = (x * lax.rsqrt(ms + RMS_EPS) * g).astype(h_ref.dtype)
            h_ref[sl, :] = h
            o_ref[sl, :] = jnp.dot(h, w_refs[0][...],
                                   preferred_element_type=F32).astype(o_ref.dtype)
            if has_side:
                os_ref[:, sl] = lax.dot_general(ws_ref[...], h, (((1,), (1,)), ((), ())),
                                                preferred_element_type=F32)

    for w_ref, (start, count) in zip(w_refs, spans):
        @pl.when(jnp.logical_and(j >= max(start, 1), j < start + count))
        def _(w_ref=w_ref):
            o_ref[...] = jnp.dot(h_ref[...], w_ref[...],
                                 preferred_element_type=F32).astype(o_ref.dtype)


def norm_matmul(x, g, weights, w_side_t=None, out_dtype=F32, tm=1024, tn=1024,
                name="norm_matmul"):
    m, d = x.shape
    assert m % tm == 0 and tm % NORM_CHUNK == 0
    in_specs = [pl.BlockSpec((tm, d), lambda i, j: (i, 0)),
                pl.BlockSpec((1, d), lambda i, j: (0, 0))]
    args = [x, g.reshape(1, d).astype(F32)]
    spans, start = [], 0
    for w, first, count in weights:
        in_specs.append(pl.BlockSpec(
            (d, tn), lambda i, j, s=start, f=first, c=count: (0, f + jnp.clip(j - s, 0, c - 1))))
        args.append(w)
        spans.append((start, count))
        start += count
    n = start * tn
    out_specs = [pl.BlockSpec((tm, tn), lambda i, j: (i, j))]
    out_shape = [jax.ShapeDtypeStruct((m, n), out_dtype)]
    if w_side_t is not None:
        ns = w_side_t.shape[0]
        in_specs.append(pl.BlockSpec((ns, d), lambda i, j: (0, 0)))
        args.append(w_side_t)
        out_specs.append(pl.BlockSpec((ns, tm), lambda i, j: (0, i)))
        out_shape.append(jax.ShapeDtypeStruct((ns, m), F32))
    outs = pl.pallas_call(
        functools.partial(_norm_matmul_body, spans=tuple(spans), has_side=w_side_t is not None),
        grid=(m // tm, start),
        in_specs=in_specs,
        out_specs=out_specs,
        out_shape=out_shape,
        scratch_shapes=[pltpu.VMEM((tm, d), BF16)],
        compiler_params=_cparams("parallel", "arbitrary"),
        name=name,
    )(*args)
    return outs if w_side_t is not None else outs[0]


def _log_sigmoid(x):
    return jnp.minimum(x, 0.0) - jnp.log1p(jnp.exp(-jnp.abs(x)))


SCAN_RADIX = 4


def _lane_scan(x, combine, identity, reverse):
    width = x.shape[-1]
    lane = lax.broadcasted_iota(jnp.int32, x.shape, 1)

    def shifted(v, sh):
        if reverse:
            return jnp.where(lane < width - sh, pltpu.roll(v, width - sh, axis=1), identity)
        return jnp.where(lane >= sh, pltpu.roll(v, sh, axis=1), identity)

    sh = 1
    while sh < width:
        parts = [shifted(x, k * sh) for k in range(1, SCAN_RADIX) if k * sh < width]
        for p in parts:
            x = combine(x, p)
        sh *= SCAN_RADIX
    return x


def _mlstm_body(q_ref, k_ref, v_ref, og_ref, g_ref, gb_ref, hg_ref, y_ref,
                rows_ref, cols_ref, hf_ref, hb_ref, ct_ref, *, nc, chunk):
    dk, dv = MLSTM_DK, MLSTM_DV
    scale = dk ** -0.5

    g = g_ref[0, 0] + gb_ref[0]
    rows = []
    for d in range(2):
        b = _lane_scan(_log_sigmoid(g[2 * d + 1]), jnp.add, 0.0, reverse=d == 1)
        gg = g[2 * d] - b
        gmax = _lane_scan(gg, jnp.maximum, -jnp.inf, reverse=d == 1)
        rows += [gg, b, gmax]
        for j, r in enumerate((gg, b, gmax)):
            rows_ref[3 * d + j] = r
    rowmat = jnp.concatenate(rows + [jnp.zeros((LANES - 6 * nc, chunk), F32)], axis=0)
    colmat = rowmat.T
    for c in range(nc):
        cols_ref[c] = colmat if c == 0 else pltpu.roll(colmat, LANES - c, axis=1)

    ct_ref[...] = jnp.zeros_like(ct_ref)

    row_id = lax.broadcasted_iota(jnp.int32, (chunk, chunk), 0)
    col_id = lax.broadcasted_iota(jnp.int32, (chunk, chunk), 1)
    causal = (row_id >= col_id, row_id <= col_id)
    ones = jnp.ones((chunk, LANES), BF16)

    def lanes2(x):
        return jnp.concatenate([x] * (dv // LANES), axis=1)

    hg = hg_ref[0]

    def emit(r0, h, own_ref, other_ref):
        rows = pl.ds(r0, chunk)
        if other_ref is None:
            own_ref[rows, :] = h
            return
        hm = h + other_ref[rows, :]
        ms = jnp.mean(hm * hm, axis=-1, keepdims=True)
        hn = hm * lax.rsqrt(ms + RMS_EPS) * hg
        y_ref[0, rows, :] = (hn * jax.nn.sigmoid(og_ref[0, rows, :])).astype(y_ref.dtype)

    def chunk_step(c, m, d, own_ref, other_ref):
        r0 = pl.multiple_of(c * chunk, chunk)
        qb = (q_ref[0, pl.ds(r0, chunk), :] * scale).astype(BF16)
        kb = k_ref[0, pl.ds(r0, chunk), :].astype(BF16)
        vf = v_ref[0, pl.ds(r0, chunk), :]
        cols = cols_ref[c]

        def column(j):
            lane = (3 * d + j) * nc
            return jnp.broadcast_to(cols[:, lane:lane + 1], (chunk, LANES))

        gg_col, b_col, gmax_col = column(0), column(1), column(2)
        gg_row = rows_ref[3 * d, pl.ds(c, 1), :]
        end = chunk - 1 if d == 0 else 0
        total = rows_ref[3 * d + 1, pl.ds(c, 1), :][:, end:end + 1]
        gmax_end = rows_ref[3 * d + 2, pl.ds(c, 1), :][:, end:end + 1]

        m_row = jnp.maximum(gmax_col, m)
        w_intra = jnp.exp(jnp.where(causal[d], gg_row - lanes2(m_row), NEG_BIG))
        w_inter = jnp.exp(m - m_row)
        s = lax.dot_general(qb, kb, (((1,), (1,)), ((), ())),
                            preferred_element_type=F32) * w_intra
        ct = ct_ref[d]
        intra = jnp.dot(s.astype(BF16), jnp.concatenate([vf.astype(BF16), ones], axis=1),
                        preferred_element_type=F32)
        inter = jnp.dot(qb, ct.astype(BF16), preferred_element_type=F32)
        num = intra[:, :dv] + lanes2(w_inter) * inter[:, :dv]
        den = intra[:, dv:] + w_inter * inter[:, dv:]
        inv = 1.0 / jnp.maximum(jnp.abs(den), jnp.exp(-(b_col + m_row)))
        emit(r0, num * lanes2(inv), own_ref, other_ref)

        m_new = jnp.maximum(total + m, total + gmax_end)
        w_src = jnp.exp(total + gg_col - m_new)
        decay = jnp.exp(total + m - m_new)
        wv = jnp.concatenate([lanes2(w_src) * vf, w_src], axis=1).astype(BF16)
        ct_ref[d] = decay * ct + lax.dot_general(
            kb, wv, (((0,), (0,)), ((), ())), preferred_element_type=F32)
        return m_new

    def first_half(c, carry):
        m_f, m_b = carry
        m_f = chunk_step(c, m_f, 0, hf_ref, None)
        m_b = chunk_step(nc - 1 - c, m_b, 1, hb_ref, None)
        return m_f, m_b

    def second_half(c, carry):
        m_f, m_b = carry
        m_f = chunk_step(c, m_f, 0, None, hb_ref)
        m_b = chunk_step(nc - 1 - c, m_b, 1, None, hf_ref)
        return m_f, m_b

    m0 = jnp.full((1, 1), NEG_BIG, F32)
    carry = lax.fori_loop(0, nc // 2, first_half, (m0, m0), unroll=2)
    lax.fori_loop(nc // 2, nc, second_half, carry, unroll=2)


def mlstm_mixer(z, gates_rows, gate_b_rows, head_g, chunk=MLSTM_CHUNK):
    b, s, _ = z.shape
    h, dk, dv = MLSTM_HEADS, MLSTM_DK, MLSTM_DV
    nc = s // chunk
    assert s % chunk == 0 and 6 * nc <= LANES and chunk % LANES == 0 and nc % 2 == 0
    kq = h * dk // dk
    kv = 2 * h * dk // dv
    ko = kv + h
    return pl.pallas_call(
        functools.partial(_mlstm_body, nc=nc, chunk=chunk),
        grid=(b, h),
        in_specs=[
            pl.BlockSpec((1, s, dk), lambda i, j: (i, 0, j)),
            pl.BlockSpec((1, s, dk), lambda i, j: (i, 0, kq + j)),
            pl.BlockSpec((1, s, dv), lambda i, j: (i, 0, kv + j)),
            pl.BlockSpec((1, s, dv), lambda i, j: (i, 0, ko + j)),
            pl.BlockSpec((1, 1, 4, nc, chunk), lambda i, j: (i, j, 0, 0, 0)),
            pl.BlockSpec((1, 4, 1, chunk), lambda i, j: (j, 0, 0, 0)),
            pl.BlockSpec((1, 1, dv), lambda i, j: (j, 0, 0)),
        ],
        out_specs=pl.BlockSpec((1, s, dv), lambda i, j: (i, 0, j)),
        out_shape=jax.ShapeDtypeStruct((b, s, h * dv), BF16),
        scratch_shapes=[
            pltpu.VMEM((6, nc, chunk), F32),
            pltpu.VMEM((nc, chunk, LANES), F32),
            pltpu.VMEM((s, dv), F32),
            pltpu.VMEM((s, dv), F32),
            pltpu.VMEM((2, dk, dv + LANES), F32),
        ],
        compiler_params=_cparams("parallel", "parallel"),
        name="mlstm",
    )(z, z, z, z, gates_rows, gate_b_rows, head_g)


RG_TILE = 512
RG_PAD = SUBLANES
RG_SCAN_UNROLL = 8


def _softplus(x):
    return jnp.maximum(x, 0.0) + jnp.log1p(jnp.exp(-jnp.abs(x)))


def _sigmoid(x):
    return 0.5 * jnp.tanh(0.5 * x) + 0.5


def _gelu_tanh(x):
    c = math.sqrt(2.0 / math.pi)
    return x * (0.5 * (1.0 + jnp.tanh(c * (x + 0.044715 * (x * x * x)))))


def _rglru_body(x_ref, gr_ref, cw_ref, cb_ref, wa_ref, wx_ref, ba_ref, bx_ref, lam_ref, y_ref,
                xpad_ref, a_ref, u_ref, as_ref, us_ref, h_ref, *, s):
    nt = s // RG_TILE
    zeros = jnp.zeros((RG_PAD, LANES), F32)
    xpad_ref[pl.ds(0, RG_PAD), :] = zeros
    xpad_ref[pl.ds(RG_PAD + s, RG_PAD), :] = zeros

    def copy_in(t, carry):
        r0 = pl.multiple_of(t * RG_TILE, RG_TILE)
        xpad_ref[pl.ds(RG_PAD + r0, RG_TILE), :] = x_ref[0, pl.ds(r0, RG_TILE), :]
        return carry

    lax.fori_loop(0, nt, copy_in, 0)

    cw = cw_ref[...]
    cb = cb_ref[...]
    sp = [RGLRU_C * _softplus(-lam_ref[d:d + 1, :]) for d in range(2)]

    def gates(t, carry):
        r0 = pl.multiple_of(t * RG_TILE, RG_TILE)
        xc = None
        for j in range(CONV_WIDTH):
            tap = xpad_ref[pl.ds(r0 + RG_PAD - CONV_LEFT + j, RG_TILE), :] * cw[j:j + 1, :]
            xc = tap if xc is None else xc + tap
        xc = xc + cb
        xcb = xc.astype(BF16)
        for d in range(2):
            r = _sigmoid(jnp.dot(xcb, wa_ref[d, 0], preferred_element_type=F32)
                         + ba_ref[d:d + 1, :])
            i = _sigmoid(jnp.dot(xcb, wx_ref[d, 0], preferred_element_type=F32)
                         + bx_ref[d:d + 1, :])
            neg_log_a = sp[d] * r
            a = jnp.exp(-neg_log_a)
            one_minus_a2 = jnp.tanh(neg_log_a) * (1.0 + a * a)
            root = jnp.where(one_minus_a2 > 0.0, one_minus_a2 * lax.rsqrt(one_minus_a2), 0.0)
            a_ref[d, pl.ds(r0, RG_TILE), :] = a
            u_ref[d, pl.ds(r0, RG_TILE), :] = root * (i * xc)
        return carry

    def block_scan(t, carry):
        r0 = pl.multiple_of(t * RG_TILE, RG_TILE)
        for d in range(2):
            a_prev = u_prev = None
            for r in (range(SUBLANES) if d == 0 else reversed(range(SUBLANES))):
                rows = pl.ds(r0 + r, RG_TILE // SUBLANES, stride=SUBLANES)
                a, u = a_ref[d, rows, :], u_ref[d, rows, :]
                if a_prev is not None:
                    u = a * u_prev + u
                    a = a * a_prev
                as_ref[d, rows, :] = a
                us_ref[d, rows, :] = u
                a_prev, u_prev = a, u
        return carry

    lax.fori_loop(0, nt, gates, 0)
    lax.fori_loop(0, nt, block_scan, 0)

    nblk = s // SUBLANES

    def scan(j, carry):
        h_f, h_b = carry
        for d, last, h in ((0, SUBLANES - 1, h_f), (1, 0, h_b)):
            blk = j if d == 0 else nblk - 1 - j
            rows = pl.ds(pl.multiple_of(blk * SUBLANES, SUBLANES), SUBLANES)
            a, u = as_ref[d, rows, :], us_ref[d, rows, :]
            h_ref[d, rows, :] = a * h + u
            a_end = jnp.broadcast_to(a[last:last + 1, :], (SUBLANES, LANES))
            u_end = jnp.broadcast_to(u[last:last + 1, :], (SUBLANES, LANES))
            if d == 0:
                h_f = a_end * h + u_end
            else:
                h_b = a_end * h + u_end
        return h_f, h_b

    h0 = jnp.zeros((SUBLANES, LANES), F32)
    lax.fori_loop(0, nblk, scan, (h0, h0), unroll=RG_SCAN_UNROLL)

    def finish(t, carry):
        r0 = pl.multiple_of(t * RG_TILE, RG_TILE)
        hr = h_ref[0, pl.ds(r0, RG_TILE), :] + h_ref[1, pl.ds(r0, RG_TILE), :]
        y_ref[0, pl.ds(r0, RG_TILE), :] = (
            hr * _gelu_tanh(gr_ref[0, pl.ds(r0, RG_TILE), :])).astype(y_ref.dtype)
        return carry

    lax.fori_loop(0, nt, finish, 0)


def rglru_mixer(z, xr_block0, gr_block0, conv_w, conv_b, wa, wx, ba, bx, lam):
    b, s, _ = z.shape
    width = RNN_BLOCKS * RNN_BLOCK
    assert s % RG_TILE == 0
    return pl.pallas_call(
        functools.partial(_rglru_body, s=s),
        grid=(b, RNN_BLOCKS),
        in_specs=[
            pl.BlockSpec((1, s, RNN_BLOCK), lambda i, j: (i, 0, xr_block0 + j)),
            pl.BlockSpec((1, s, RNN_BLOCK), lambda i, j: (i, 0, gr_block0 + j)),
            pl.BlockSpec((CONV_WIDTH, RNN_BLOCK), lambda i, j: (0, j)),
            pl.BlockSpec((1, RNN_BLOCK), lambda i, j: (0, j)),
            pl.BlockSpec((2, 1, RNN_BLOCK, RNN_BLOCK), lambda i, j: (0, j, 0, 0)),
            pl.BlockSpec((2, 1, RNN_BLOCK, RNN_BLOCK), lambda i, j: (0, j, 0, 0)),
            pl.BlockSpec((2, RNN_BLOCK), lambda i, j: (0, j)),
            pl.BlockSpec((2, RNN_BLOCK), lambda i, j: (0, j)),
            pl.BlockSpec((2, RNN_BLOCK), lambda i, j: (0, j)),
        ],
        out_specs=pl.BlockSpec((1, s, RNN_BLOCK), lambda i, j: (i, 0, j)),
        out_shape=jax.ShapeDtypeStruct((b, s, width), BF16),
        scratch_shapes=[
            pltpu.VMEM((s + 2 * RG_PAD, RNN_BLOCK), F32),
            pltpu.VMEM((2, s, RNN_BLOCK), F32),
            pltpu.VMEM((2, s, RNN_BLOCK), F32),
            pltpu.VMEM((2, s, RNN_BLOCK), F32),
            pltpu.VMEM((2, s, RNN_BLOCK), F32),
            pltpu.VMEM((2, s, RNN_BLOCK), F32),
        ],
        compiler_params=_cparams("parallel", "parallel"),
        name="rglru",
    )(z, z, conv_w, conv_b, wa, wx, ba, bx, lam)


ATTN_TQ = 128
ATTN_TK = ATTN_TQ + 2 * ATTN_HALF
ATTN_UNROLL = 32
LOG2E = math.log2(math.e)


def _attn_body(q_ref, k_ref, v_ref, slope_ref, o_ref, bias_ref, x4_ref, og_ref, dg_ref, mg_ref,
               *, s):
    qscale = ATTN_DH ** -0.5 * LOG2E

    @pl.when(pl.program_id(1) == 0)
    def _():
        slope = slope_ref[0, 0:1, 0:1] * LOG2E
        qi = lax.broadcasted_iota(jnp.int32, (ATTN_TQ, ATTN_TK), 0)
        kj = lax.broadcasted_iota(jnp.int32, (ATTN_TQ, ATTN_TK), 1)
        for g, dil in enumerate(ATTN_DILATIONS):
            for e in range(3):
                rel = jnp.abs(kj - e * ATTN_HALF - qi)
                pen = slope * (rel * dil).astype(F32)
                bias_ref[g, e] = jnp.where(rel <= ATTN_HALF, -pen, NEG_BIG)

    s4 = s // 4
    srcs = (q_ref, k_ref, v_ref)

    def split4(t, carry):
        c = t // (s4 // ATTN_TK)
        p0 = (t % (s4 // ATTN_TK)) * ATTN_TK
        dst = pl.ds(pl.multiple_of(c * s4 + p0, ATTN_TK), ATTN_TK)
        for a in range(3):
            x = srcs[a][0, pl.ds(c + 4 * p0, ATTN_TK, stride=4), :]
            x4_ref[a, dst, :] = x * qscale if a == 0 else x
        return carry

    lax.fori_loop(0, 4 * (s4 // ATTN_TK), split4, 0)

    ones = jnp.ones((ATTN_TK, ATTN_DH), BF16)

    for g, dil in reversed(list(enumerate(ATTN_DILATIONS))):
        sp = s // dil
        nqb = sp // ATTN_TQ

        def q_block(t, carry, g=g, dil=dil, sp=sp, nqb=nqb):
            r = t // nqb
            p0 = (t % nqb) * ATTN_TQ
            kstart = jnp.clip(p0 - ATTN_HALF, 0, sp - ATTN_TK)
            e = (p0 - kstart) // ATTN_HALF
            if dil == 1:
                qrows = pl.ds(pl.multiple_of(p0, ATTN_TQ), ATTN_TQ)
                krows = pl.ds(pl.multiple_of(kstart, ATTN_HALF), ATTN_TK)
                qf = q_ref[0, qrows, :] * qscale
                kf, vf = k_ref[0, krows, :], v_ref[0, krows, :]
                orows = qrows
            elif dil == 4:
                qrows = pl.ds(pl.multiple_of(r * s4 + p0, ATTN_TQ), ATTN_TQ)
                krows = pl.ds(pl.multiple_of(r * s4 + kstart, ATTN_HALF), ATTN_TK)
                qf, kf, vf = x4_ref[0, qrows, :], x4_ref[1, krows, :], x4_ref[2, krows, :]
                orows = pl.ds(r + 4 * p0, ATTN_TQ, stride=4)
            else:
                base = (r % 4) * s4 + r // 4
                qrows = pl.ds(base + 4 * p0, ATTN_TQ, stride=4)
                krows = pl.ds(base + 4 * kstart, ATTN_TK, stride=4)
                qf, kf, vf = x4_ref[0, qrows, :], x4_ref[1, krows, :], x4_ref[2, krows, :]
                orows = pl.ds(r + dil * p0, ATTN_TQ, stride=dil)
            qb, kb, vb = qf.astype(BF16), kf.astype(BF16), vf.astype(BF16)
            sc = lax.dot_general(qb, kb, (((1,), (1,)), ((), ())),
                                 preferred_element_type=F32) + bias_ref[g, e]
            m = jnp.max(sc, axis=-1, keepdims=True)
            p = jnp.exp2(sc - m).astype(BF16)
            pv = jnp.dot(p, jnp.concatenate([vb, ones], axis=1), preferred_element_type=F32)
            acc, den = pv[:, :ATTN_DH], pv[:, ATTN_DH:]
            if dil != 1:
                og_ref[g - 1, orows, :] = acc
                dg_ref[g - 1, orows, :] = den
                mg_ref[g - 1, orows, :] = jnp.broadcast_to(m, den.shape)
                return carry
            m4, m16 = mg_ref[0, orows, :], mg_ref[1, orows, :]
            mx = jnp.maximum(jnp.maximum(m4, m16), m)
            e1, e4, e16 = jnp.exp2(m - mx), jnp.exp2(m4 - mx), jnp.exp2(m16 - mx)
            num = e1 * acc + e4 * og_ref[0, orows, :] + e16 * og_ref[1, orows, :]
            tot = e1 * den + e4 * dg_ref[0, orows, :] + e16 * dg_ref[1, orows, :]
            o_ref[0, orows, :] = (num * (1.0 / tot)).astype(o_ref.dtype)
            return carry

        lax.fori_loop(0, dil * nqb, q_block, 0, unroll=ATTN_UNROLL)


def dilated_attention(qkv, slopes):
    b, s, _ = qkv.shape
    h, dh = ATTN_HEADS, ATTN_DH
    assert s % (max(ATTN_DILATIONS) * ATTN_TK) == 0
    return pl.pallas_call(
        functools.partial(_attn_body, s=s),
        grid=(h, b),
        in_specs=[
            pl.BlockSpec((1, s, dh), lambda j, i: (i, 0, j)),
            pl.BlockSpec((1, s, dh), lambda j, i: (i, 0, h + j)),
            pl.BlockSpec((1, s, dh), lambda j, i: (i, 0, 2 * h + j)),
            pl.BlockSpec((1, SUBLANES, LANES), lambda j, i: (j, 0, 0)),
        ],
        out_specs=pl.BlockSpec((1, s, dh), lambda j, i: (i, 0, j)),
        out_shape=jax.ShapeDtypeStruct((b, s, h * dh), BF16),
        scratch_shapes=[
            pltpu.VMEM((len(ATTN_DILATIONS), 3, ATTN_TQ, ATTN_TK), F32),
            pltpu.VMEM((3, s, dh), F32),
            pltpu.VMEM((len(ATTN_DILATIONS) - 1, s, dh), F32),
            pltpu.VMEM((len(ATTN_DILATIONS) - 1, s, dh), F32),
            pltpu.VMEM((len(ATTN_DILATIONS) - 1, s, dh), F32),
        ],
        compiler_params=_cparams("parallel", "arbitrary"),
        name="dilated_attn",
    )(qkv, qkv, qkv, slopes)


MIXER_OUT_TM = 512


FFN_DOWN_TM = 256


def _ffn(x, h, w1, w3, w2, layer, next_norm_g, last):
    u = swiglu_up(h, w1, w3, layer)
    return matmul([(u, w2, 0, layer)], res=x, norm_g=next_norm_g,
                  norm_dtype=F32 if last else BF16, keep_sum=not last,
                  tm=FFN_DOWN_TM, tn=w2.shape[-1], name="ffn_down")


def _even_layer(x, bsz, seq, norm_g, w_in, gate_b, conv_w, conv_b, rg_wa, rg_ba, rg_wx, rg_bx,
                rg_lam, head_g, w_out, next_norm_g):
    heads, dk, dv = MLSTM_HEADS, MLSTM_DK, MLSTM_DV
    n_qkvo = 2 * heads * dk + 2 * heads * dv
    n_gate = 4 * heads
    width = RNN_BLOCKS * RNN_BLOCK
    tn = 1024
    assert n_qkvo % tn == 0 and (2 * width) % tn == 0
    w_in_b = w_in.astype(BF16)
    w_rest = w_in_b[:, n_qkvo + n_gate:]
    w_gate_t = jnp.transpose(w_in_b[:, n_qkvo:n_qkvo + n_gate])
    z, zg_t = norm_matmul(x, norm_g,
                          [(w_in_b, 0, n_qkvo // tn), (w_rest, 0, 2 * width // tn)],
                          w_side_t=w_gate_t, tn=tn, name="in_proj")
    z = z.reshape(bsz, seq, -1)

    chunk = MLSTM_CHUNK
    nc = seq // chunk
    gates_rows = jnp.transpose(zg_t[:n_gate].reshape(4, heads, bsz, nc, chunk), (2, 1, 0, 3, 4))
    gate_b_rows = jnp.broadcast_to(
        jnp.transpose(gate_b.astype(F32).reshape(4, heads))[:, :, None, None], (heads, 4, 1, chunk))
    y_a = mlstm_mixer(z, gates_rows, gate_b_rows, head_g.astype(F32).reshape(heads, 1, dv), chunk)

    y_b = rglru_mixer(z, n_qkvo // RNN_BLOCK, (n_qkvo + width) // RNN_BLOCK,
                      conv_w.astype(F32), conv_b.astype(F32).reshape(1, width),
                      rg_wa.astype(BF16), rg_wx.astype(BF16),
                      rg_ba.astype(F32), rg_bx.astype(F32), rg_lam.astype(F32))

    w_out_b = w_out.astype(BF16)
    m = bsz * seq
    assert heads * dv == width
    return matmul([(y_a.reshape(m, -1), w_out_b, 0),
                   (y_b.reshape(m, -1), w_out_b, 1)], res=x, norm_g=next_norm_g,
                  tm=MIXER_OUT_TM, tn=w_out.shape[1], name="out_proj")


def _odd_layer(x, h, bsz, seq, norm_g, w_qkv, w_o, next_norm_g):
    if h is None:
        qkv = norm_matmul(x, norm_g, [(w_qkv.astype(BF16), 0, w_qkv.shape[1] // 1024)],
                          name="qkv_proj")
    else:
        qkv = matmul([(h, w_qkv.astype(BF16))], name="qkv_proj")
    qkv = qkv.reshape(bsz, seq, -1)
    slopes = jnp.exp2(-ALIBI_MAX_BIAS * jnp.arange(1, ATTN_HEADS + 1, dtype=F32) / ATTN_HEADS)
    slopes = jnp.broadcast_to(slopes[:, None, None], (ATTN_HEADS, SUBLANES, LANES))
    o = dilated_attention(qkv, slopes)
    return matmul([(o.reshape(bsz * seq, -1), w_o.astype(BF16))], res=x, norm_g=next_norm_g,
                  tm=MIXER_OUT_TM, tn=w_o.shape[1], name="attn_out_proj")


def kernel(x, e_norm, e_w_in, e_gate_b, e_conv_w, e_conv_b, e_rg_wa, e_rg_ba, e_rg_wx, e_rg_bx,
           e_rg_lam, e_head_g, e_w_out, o_norm, o_w_qkv, o_w_o, f_norm, f_w1, f_w3, f_w2,
           final_norm):
    bsz, seq, d = x.shape
    depth = f_norm.shape[0]
    xs = x.reshape(bsz * seq, d).astype(F32)
    w1, w3, w2 = f_w1.astype(BF16), f_w3.astype(BF16), f_w2.astype(BF16)
    hs = None
    for l in range(depth):
        if l % 2 == 0:
            e = l // 2
            xs, hs = _even_layer(xs, bsz, seq, e_norm[e], e_w_in[e], e_gate_b[e], e_conv_w[e],
                                 e_conv_b[e], e_rg_wa[e], e_rg_ba[e], e_rg_wx[e], e_rg_bx[e],
                                 e_rg_lam[e], e_head_g[e], e_w_out[e], f_norm[l])
        else:
            o = l // 2
            xs, hs = _odd_layer(xs, hs, bsz, seq, o_norm[o], o_w_qkv[o], o_w_o[o], f_norm[l])
        if l == depth - 1:
            out = _ffn(xs, hs, w1, w3, w2, l, final_norm, last=True)
            return out.astype(x.dtype).reshape(bsz, seq, d)
        if (l + 1) % 2:
            xs, hs = _ffn(xs, hs, w1, w3, w2, l, o_norm[(l + 1) // 2], last=False)
        else:
            xs, hs = _ffn(xs, hs, w1, w3, w2, l, None, last=False), None
```

```python
import functools
import math

import jax
import jax.numpy as jnp
from jax import lax
from jax.experimental import pallas as pl
from jax.experimental.pallas import tpu as pltpu

F32 = jnp.float32
BF16 = jnp.bfloat16

RMS_EPS = 1e-6
NEG_BIG = -1e30

MLSTM_HEADS = 4
MLSTM_DK = 128
MLSTM_DV = 256
MLSTM_CHUNK = 256
RNN_BLOCKS = 8
RNN_BLOCK = 128
CONV_WIDTH = 4
CONV_LEFT = 2
RGLRU_C = 8.0
ATTN_HEADS = 16
ATTN_DH = 128
ATTN_HALF = 64
ATTN_DILATIONS = (1, 4, 16)
ALIBI_MAX_BIAS = 8.0

LANES = 128
SUBLANES = 8
VMEM_LIMIT_BYTES = 56 * 1024 * 1024


def _cparams(*semantics):
    return pltpu.CompilerParams(dimension_semantics=semantics,
                                vmem_limit_bytes=VMEM_LIMIT_BYTES)


NORM_ROWS = 128


def _norm_rows_into(x_ref, g_ref, h_ref):
    g = g_ref[...]

    def rows(t, carry):
        sl = pl.ds(pl.multiple_of(t * NORM_ROWS, NORM_ROWS), NORM_ROWS)
        x = x_ref[sl, :]
        ms = jnp.mean(x * x, axis=-1, keepdims=True)
        h_ref[sl, :] = (x * lax.rsqrt(ms + RMS_EPS) * g).astype(h_ref.dtype)
        return carry

    lax.fori_loop(0, x_ref.shape[0] // NORM_ROWS, rows, 0)


def _weight_spec(w, kdim, tn, row_block=0, layer=None, **mode):
    if w.ndim == 2:
        return pl.BlockSpec((kdim, tn), lambda i, j: (row_block, j), **mode)
    return pl.BlockSpec((None, kdim, tn), lambda i, j: (layer, row_block, j), **mode)


def _matmul_body(*refs, n_pairs, has_res, has_norm, keep_sum):
    pairs = [(refs[2 * p], refs[2 * p + 1]) for p in range(n_pairs)]
    rest = list(refs[2 * n_pairs:])
    r_ref = rest.pop(0) if has_res else None
    g_ref = rest.pop(0) if has_norm else None
    y_ref = rest.pop(0) if keep_sum else rest.pop(-1)

    acc = None
    for a_ref, w_ref in pairs:
        d = jnp.dot(a_ref[...], w_ref[...], preferred_element_type=F32)
        acc = d if acc is None else acc + d
    if has_res:
        acc = acc + r_ref[...]
    y_ref[...] = acc.astype(y_ref.dtype)
    if has_norm:
        _norm_rows_into(y_ref, g_ref, rest.pop(0))


def matmul(pairs, res=None, norm_g=None, norm_dtype=BF16, keep_sum=True, tm=1024, tn=1024,
           name="matmul"):
    m = pairs[0][0].shape[0]
    n = pairs[0][1].shape[-1]
    tn = min(tn, n)
    assert m % tm == 0 and n % tn == 0 and (keep_sum or norm_g is not None)
    assert norm_g is None or (tn == n and tm % NORM_ROWS == 0)
    w_mode = dict(pipeline_mode=pl.Buffered(1)) if tn == n else {}
    in_specs, args = [], []
    for a, w, *where in pairs:
        kdim = a.shape[1]
        kb, layer = (list(where) + [0, None])[:2] if where else (0, None)
        in_specs.append(pl.BlockSpec((tm, kdim), lambda i, j: (i, 0)))
        in_specs.append(_weight_spec(w, kdim, tn, kb, layer, **w_mode))
        args += [a, w]
    if res is not None:
        in_specs.append(pl.BlockSpec((tm, tn), lambda i, j: (i, j)))
        args.append(res)
    out_specs, out_shape, scratch = [], [], []
    if keep_sum:
        out_specs.append(pl.BlockSpec((tm, tn), lambda i, j: (i, j)))
        out_shape.append(jax.ShapeDtypeStruct((m, n), F32))
    else:
        scratch.append(pltpu.VMEM((tm, n), F32))
    if norm_g is not None:
        in_specs.append(pl.BlockSpec((1, n), lambda i, j: (0, 0)))
        args.append(norm_g.reshape(1, n).astype(F32))
        out_specs.append(pl.BlockSpec((tm, n), lambda i, j: (i, 0)))
        out_shape.append(jax.ShapeDtypeStruct((m, n), norm_dtype))
    outs = pl.pallas_call(
        functools.partial(_matmul_body, n_pairs=len(pairs), has_res=res is not None,
                          has_norm=norm_g is not None, keep_sum=keep_sum),
        grid=(m // tm, n // tn),
        in_specs=in_specs,
        out_specs=out_specs,
        out_shape=out_shape,
        scratch_shapes=scratch,
        compiler_params=_cparams("parallel", "parallel"),
        name=name,
    )(*args)
    return outs if len(outs) > 1 else outs[0]


SWIGLU_ROWS = 1024


def _swiglu_up_body(a_ref, w1_ref, w3_ref, o_ref):
    rows = min(SWIGLU_ROWS, a_ref.shape[0])
    for r0 in range(0, a_ref.shape[0], rows):
        a = a_ref[pl.ds(r0, rows), :]
        u = jnp.dot(a, w1_ref[...], preferred_element_type=F32)
        v = jnp.dot(a, w3_ref[...], preferred_element_type=F32)
        o_ref[pl.ds(r0, rows), :] = (u * jax.nn.sigmoid(u) * v).astype(o_ref.dtype)


def swiglu_up(a, w1, w3, layer, tm=4096, tn=512):
    m, kdim = a.shape
    n = w1.shape[-1]
    assert m % tm == 0 and n % tn == 0
    return pl.pallas_call(
        _swiglu_up_body,
        grid=(m // tm, n // tn),
        in_specs=[pl.BlockSpec((tm, kdim), lambda i, j: (i, 0)),
                  _weight_spec(w1, kdim, tn, layer=layer),
                  _weight_spec(w3, kdim, tn, layer=layer)],
        out_specs=pl.BlockSpec((tm, tn), lambda i, j: (i, j)),
        out_shape=jax.ShapeDtypeStruct((m, n), BF16),
        compiler_params=_cparams("parallel", "parallel"),
        name="swiglu_up",
    )(a, w1, w3)


def _split_weight_body(w_ref, main_ref, gate_ref, *, n_head, n_gate):
    main_ref[:, :n_head] = w_ref[:, :n_head].astype(main_ref.dtype)
    main_ref[:, n_head:] = w_ref[:, n_head + n_gate:].astype(main_ref.dtype)
    gate_ref[...] = w_ref[:, n_head:n_head + LANES]


def split_in_proj_weight(w, n_head, n_gate, tr=256):
    d, n = w.shape
    n_main = n - n_gate
    assert d % tr == 0 and n_head % LANES == 0 and n_main % LANES == 0 and n_head + LANES <= n
    return pl.pallas_call(
        functools.partial(_split_weight_body, n_head=n_head, n_gate=n_gate),
        grid=(d // tr,),
        in_specs=[pl.BlockSpec((tr, n), lambda i: (i, 0))],
        out_specs=[pl.BlockSpec((tr, n_main), lambda i: (i, 0)),
                   pl.BlockSpec((tr, LANES), lambda i: (i, 0))],
        out_shape=[jax.ShapeDtypeStruct((d, n_main), BF16),
                   jax.ShapeDtypeStruct((d, LANES), F32)],
        compiler_params=_cparams("parallel"),
        name="split_in_proj_weight",
    )(w)


NORM_CHUNK = 256


def _norm_matmul_body(*refs, has_side):
    if has_side:
        x_ref, g_ref, w_ref, ws_ref, o_ref, os_ref, h_ref = refs
    else:
        x_ref, g_ref, w_ref, o_ref, h_ref = refs

    @pl.when(pl.program_id(1) == 0)
    def _():
        g = g_ref[...]
        for r0 in range(0, x_ref.shape[0], NORM_CHUNK):
            sl = pl.ds(r0, NORM_CHUNK)
            x = x_ref[sl, :]
            ms = jnp.mean(x * x, axis=-1, keepdims=True)
            h = (x * lax.rsqrt(ms + RMS_EPS) * g).astype(h_ref.dtype)
            h_ref[sl, :] = h
            o_ref[sl, :] = jnp.dot(h, w_ref[...], preferred_element_type=F32).astype(o_ref.dtype)
            if has_side:
                os_ref[:, sl] = lax.dot_general(ws_ref[...], h, (((1,), (1,)), ((), ())),
                                                preferred_element_type=F32)

    @pl.when(pl.program_id(1) > 0)
    def _():
        o_ref[...] = jnp.dot(h_ref[...], w_ref[...],
                             preferred_element_type=F32).astype(o_ref.dtype)


def norm_matmul(x, g, w, w_side_t=None, out_dtype=F32, tm=1024, tn=1024, name="norm_matmul"):
    m, d = x.shape
    n = w.shape[1]
    assert m % tm == 0 and n % tn == 0 and tm % NORM_CHUNK == 0
    in_specs = [pl.BlockSpec((tm, d), lambda i, j: (i, 0)),
                pl.BlockSpec((1, d), lambda i, j: (0, 0)),
                pl.BlockSpec((d, tn), lambda i, j: (0, j))]
    args = [x, g.reshape(1, d).astype(F32), w]
    out_specs = [pl.BlockSpec((tm, tn), lambda i, j: (i, j))]
    out_shape = [jax.ShapeDtypeStruct((m, n), out_dtype)]
    if w_side_t is not None:
        ns = w_side_t.shape[0]
        in_specs.append(pl.BlockSpec((ns, d), lambda i, j: (0, 0)))
        args.append(w_side_t)
        out_specs.append(pl.BlockSpec((ns, tm), lambda i, j: (0, i)))
        out_shape.append(jax.ShapeDtypeStruct((ns, m), F32))
    outs = pl.pallas_call(
        functools.partial(_norm_matmul_body, has_side=w_side_t is not None),
        grid=(m // tm, n // tn),
        in_specs=in_specs,
        out_specs=out_specs,
        out_shape=out_shape,
        scratch_shapes=[pltpu.VMEM((tm, d), BF16)],
        compiler_params=_cparams("parallel", "arbitrary"),
        name=name,
    )(*args)
    return outs if w_side_t is not None else outs[0]


def _log_sigmoid(x):
    return jnp.minimum(x, 0.0) - jnp.log1p(jnp.exp(-jnp.abs(x)))


SCAN_RADIX = 4


def _lane_scan(x, combine, identity, reverse):
    width = x.shape[-1]
    lane = lax.broadcasted_iota(jnp.int32, x.shape, 1)

    def shifted(v, sh):
        if reverse:
            return jnp.where(lane < width - sh, pltpu.roll(v, width - sh, axis=1), identity)
        return jnp.where(lane >= sh, pltpu.roll(v, sh, axis=1), identity)

    sh = 1
    while sh < width:
        parts = [shifted(x, k * sh) for k in range(1, SCAN_RADIX) if k * sh < width]
        for p in parts:
            x = combine(x, p)
        sh *= SCAN_RADIX
    return x


def _mlstm_body(q_ref, k_ref, v_ref, og_ref, g_ref, gb_ref, hg_ref, y_ref,
                rows_ref, cols_ref, hf_ref, hb_ref, ct_ref, *, nc, chunk):
    dk, dv = MLSTM_DK, MLSTM_DV
    scale = dk ** -0.5

    g = g_ref[0, 0] + gb_ref[0]
    rows = []
    for d in range(2):
        b = _lane_scan(_log_sigmoid(g[2 * d + 1]), jnp.add, 0.0, reverse=d == 1)
        gg = g[2 * d] - b
        gmax = _lane_scan(gg, jnp.maximum, -jnp.inf, reverse=d == 1)
        rows += [gg, b, gmax]
        for j, r in enumerate((gg, b, gmax)):
            rows_ref[3 * d + j] = r
    rowmat = jnp.concatenate(rows + [jnp.zeros((LANES - 6 * nc, chunk), F32)], axis=0)
    colmat = rowmat.T
    for c in range(nc):
        cols_ref[c] = colmat if c == 0 else pltpu.roll(colmat, LANES - c, axis=1)

    ct_ref[...] = jnp.zeros_like(ct_ref)

    row_id = lax.broadcasted_iota(jnp.int32, (chunk, chunk), 0)
    col_id = lax.broadcasted_iota(jnp.int32, (chunk, chunk), 1)
    causal = (row_id >= col_id, row_id <= col_id)
    ones = jnp.ones((chunk, LANES), BF16)

    def lanes2(x):
        return jnp.concatenate([x] * (dv // LANES), axis=1)

    hg = hg_ref[0]

    def emit(r0, h, own_ref, other_ref):
        rows = pl.ds(r0, chunk)
        if other_ref is None:
            own_ref[rows, :] = h
            return
        hm = h + other_ref[rows, :]
        ms = jnp.mean(hm * hm, axis=-1, keepdims=True)
        hn = hm * lax.rsqrt(ms + RMS_EPS) * hg
        y_ref[0, rows, :] = (hn * jax.nn.sigmoid(og_ref[0, rows, :])).astype(y_ref.dtype)

    def chunk_step(c, m, d, own_ref, other_ref):
        r0 = pl.multiple_of(c * chunk, chunk)
        qb = (q_ref[0, pl.ds(r0, chunk), :] * scale).astype(BF16)
        kb = k_ref[0, pl.ds(r0, chunk), :].astype(BF16)
        vf = v_ref[0, pl.ds(r0, chunk), :]
        cols = cols_ref[c]

        def column(j):
            lane = (3 * d + j) * nc
            return jnp.broadcast_to(cols[:, lane:lane + 1], (chunk, LANES))

        gg_col, b_col, gmax_col = column(0), column(1), column(2)
        gg_row = rows_ref[3 * d, pl.ds(c, 1), :]
        end = chunk - 1 if d == 0 else 0
        total = rows_ref[3 * d + 1, pl.ds(c, 1), :][:, end:end + 1]
        gmax_end = rows_ref[3 * d + 2, pl.ds(c, 1), :][:, end:end + 1]

        m_row = jnp.maximum(gmax_col, m)
        w_intra = jnp.exp(jnp.where(causal[d], gg_row - lanes2(m_row), NEG_BIG))
        w_inter = jnp.exp(m - m_row)
        s = lax.dot_general(qb, kb, (((1,), (1,)), ((), ())),
                            preferred_element_type=F32) * w_intra
        ct = ct_ref[d]
        intra = jnp.dot(s.astype(BF16), jnp.concatenate([vf.astype(BF16), ones], axis=1),
                        preferred_element_type=F32)
        inter = jnp.dot(qb, ct.astype(BF16), preferred_element_type=F32)
        num = intra[:, :dv] + lanes2(w_inter) * inter[:, :dv]
        den = intra[:, dv:] + w_inter * inter[:, dv:]
        inv = 1.0 / jnp.maximum(jnp.abs(den), jnp.exp(-(b_col + m_row)))
        emit(r0, num * lanes2(inv), own_ref, other_ref)

        m_new = jnp.maximum(total + m, total + gmax_end)
        w_src = jnp.exp(total + gg_col - m_new)
        decay = jnp.exp(total + m - m_new)
        wv = jnp.concatenate([lanes2(w_src) * vf, w_src], axis=1).astype(BF16)
        ct_ref[d] = decay * ct + lax.dot_general(
            kb, wv, (((0,), (0,)), ((), ())), preferred_element_type=F32)
        return m_new

    def first_half(c, carry):
        m_f, m_b = carry
        m_f = chunk_step(c, m_f, 0, hf_ref, None)
        m_b = chunk_step(nc - 1 - c, m_b, 1, hb_ref, None)
        return m_f, m_b

    def second_half(c, carry):
        m_f, m_b = carry
        m_f = chunk_step(c, m_f, 0, None, hb_ref)
        m_b = chunk_step(nc - 1 - c, m_b, 1, None, hf_ref)
        return m_f, m_b

    m0 = jnp.full((1, 1), NEG_BIG, F32)
    carry = lax.fori_loop(0, nc // 2, first_half, (m0, m0), unroll=2)
    lax.fori_loop(nc // 2, nc, second_half, carry, unroll=2)


def mlstm_mixer(z, gates_rows, gate_b_rows, head_g, chunk=MLSTM_CHUNK):
    b, s, _ = z.shape
    h, dk, dv = MLSTM_HEADS, MLSTM_DK, MLSTM_DV
    nc = s // chunk
    assert s % chunk == 0 and 6 * nc <= LANES and chunk % LANES == 0 and nc % 2 == 0
    kq = h * dk // dk
    kv = 2 * h * dk // dv
    ko = kv + h
    return pl.pallas_call(
        functools.partial(_mlstm_body, nc=nc, chunk=chunk),
        grid=(b, h),
        in_specs=[
            pl.BlockSpec((1, s, dk), lambda i, j: (i, 0, j)),
            pl.BlockSpec((1, s, dk), lambda i, j: (i, 0, kq + j)),
            pl.BlockSpec((1, s, dv), lambda i, j: (i, 0, kv + j)),
            pl.BlockSpec((1, s, dv), lambda i, j: (i, 0, ko + j)),
            pl.BlockSpec((1, 1, 4, nc, chunk), lambda i, j: (i, j, 0, 0, 0)),
            pl.BlockSpec((1, 4, 1, chunk), lambda i, j: (j, 0, 0, 0)),
            pl.BlockSpec((1, 1, dv), lambda i, j: (j, 0, 0)),
        ],
        out_specs=pl.BlockSpec((1, s, dv), lambda i, j: (i, 0, j)),
        out_shape=jax.ShapeDtypeStruct((b, s, h * dv), BF16),
        scratch_shapes=[
            pltpu.VMEM((6, nc, chunk), F32),
            pltpu.VMEM((nc, chunk, LANES), F32),
            pltpu.VMEM((s, dv), F32),
            pltpu.VMEM((s, dv), F32),
            pltpu.VMEM((2, dk, dv + LANES), F32),
        ],
        compiler_params=_cparams("parallel", "parallel"),
        name="mlstm",
    )(z, z, z, z, gates_rows, gate_b_rows, head_g)


RG_TILE = 512
RG_PAD = SUBLANES
RG_SCAN_UNROLL = 8


def _softplus(x):
    return jnp.maximum(x, 0.0) + jnp.log1p(jnp.exp(-jnp.abs(x)))


def _sigmoid(x):
    return 0.5 * jnp.tanh(0.5 * x) + 0.5


def _gelu_tanh(x):
    c = math.sqrt(2.0 / math.pi)
    return x * (0.5 * (1.0 + jnp.tanh(c * (x + 0.044715 * (x * x * x)))))


def _rglru_body(x_ref, gr_ref, cw_ref, cb_ref, wa_ref, wx_ref, ba_ref, bx_ref, lam_ref, y_ref,
                xpad_ref, a_ref, u_ref, as_ref, us_ref, h_ref, *, s):
    nt = s // RG_TILE
    zeros = jnp.zeros((RG_PAD, LANES), F32)
    xpad_ref[pl.ds(0, RG_PAD), :] = zeros
    xpad_ref[pl.ds(RG_PAD + s, RG_PAD), :] = zeros

    def copy_in(t, carry):
        r0 = pl.multiple_of(t * RG_TILE, RG_TILE)
        xpad_ref[pl.ds(RG_PAD + r0, RG_TILE), :] = x_ref[0, pl.ds(r0, RG_TILE), :]
        return carry

    lax.fori_loop(0, nt, copy_in, 0)

    cw = cw_ref[...]
    cb = cb_ref[...]
    sp = [RGLRU_C * _softplus(-lam_ref[d:d + 1, :]) for d in range(2)]

    def gates(t, carry):
        r0 = pl.multiple_of(t * RG_TILE, RG_TILE)
        xc = None
        for j in range(CONV_WIDTH):
            tap = xpad_ref[pl.ds(r0 + RG_PAD - CONV_LEFT + j, RG_TILE), :] * cw[j:j + 1, :]
            xc = tap if xc is None else xc + tap
        xc = xc + cb
        xcb = xc.astype(BF16)
        for d in range(2):
            r = _sigmoid(jnp.dot(xcb, wa_ref[d, 0], preferred_element_type=F32)
                         + ba_ref[d:d + 1, :])
            i = _sigmoid(jnp.dot(xcb, wx_ref[d, 0], preferred_element_type=F32)
                         + bx_ref[d:d + 1, :])
            neg_log_a = sp[d] * r
            a = jnp.exp(-neg_log_a)
            one_minus_a2 = jnp.tanh(neg_log_a) * (1.0 + a * a)
            root = jnp.where(one_minus_a2 > 0.0, one_minus_a2 * lax.rsqrt(one_minus_a2), 0.0)
            a_ref[d, pl.ds(r0, RG_TILE), :] = a
            u_ref[d, pl.ds(r0, RG_TILE), :] = root * (i * xc)
        return carry

    def block_scan(t, carry):
        r0 = pl.multiple_of(t * RG_TILE, RG_TILE)
        for d in range(2):
            a_prev = u_prev = None
            for r in (range(SUBLANES) if d == 0 else reversed(range(SUBLANES))):
                rows = pl.ds(r0 + r, RG_TILE // SUBLANES, stride=SUBLANES)
                a, u = a_ref[d, rows, :], u_ref[d, rows, :]
                if a_prev is not None:
                    u = a * u_prev + u
                    a = a * a_prev
                as_ref[d, rows, :] = a
                us_ref[d, rows, :] = u
                a_prev, u_prev = a, u
        return carry

    lax.fori_loop(0, nt, gates, 0)
    lax.fori_loop(0, nt, block_scan, 0)

    nblk = s // SUBLANES

    def scan(j, carry):
        h_f, h_b = carry
        for d, last, h in ((0, SUBLANES - 1, h_f), (1, 0, h_b)):
            blk = j if d == 0 else nblk - 1 - j
            rows = pl.ds(pl.multiple_of(blk * SUBLANES, SUBLANES), SUBLANES)
            a, u = as_ref[d, rows, :], us_ref[d, rows, :]
            h_ref[d, rows, :] = a * h + u
            a_end = jnp.broadcast_to(a[last:last + 1, :], (SUBLANES, LANES))
            u_end = jnp.broadcast_to(u[last:last + 1, :], (SUBLANES, LANES))
            if d == 0:
                h_f = a_end * h + u_end
            else:
                h_b = a_end * h + u_end
        return h_f, h_b

    h0 = jnp.zeros((SUBLANES, LANES), F32)
    lax.fori_loop(0, nblk, scan, (h0, h0), unroll=RG_SCAN_UNROLL)

    def finish(t, carry):
        r0 = pl.multiple_of(t * RG_TILE, RG_TILE)
        hr = h_ref[0, pl.ds(r0, RG_TILE), :] + h_ref[1, pl.ds(r0, RG_TILE), :]
        y_ref[0, pl.ds(r0, RG_TILE), :] = (
            hr * _gelu_tanh(gr_ref[0, pl.ds(r0, RG_TILE), :])).astype(y_ref.dtype)
        return carry

    lax.fori_loop(0, nt, finish, 0)


def rglru_mixer(z, xr_block0, gr_block0, conv_w, conv_b, wa, wx, ba, bx, lam):
    b, s, _ = z.shape
    width = RNN_BLOCKS * RNN_BLOCK
    assert s % RG_TILE == 0
    return pl.pallas_call(
        functools.partial(_rglru_body, s=s),
        grid=(b, RNN_BLOCKS),
        in_specs=[
            pl.BlockSpec((1, s, RNN_BLOCK), lambda i, j: (i, 0, xr_block0 + j)),
            pl.BlockSpec((1, s, RNN_BLOCK), lambda i, j: (i, 0, gr_block0 + j)),
            pl.BlockSpec((CONV_WIDTH, RNN_BLOCK), lambda i, j: (0, j)),
            pl.BlockSpec((1, RNN_BLOCK), lambda i, j: (0, j)),
            pl.BlockSpec((2, 1, RNN_BLOCK, RNN_BLOCK), lambda i, j: (0, j, 0, 0)),
            pl.BlockSpec((2, 1, RNN_BLOCK, RNN_BLOCK), lambda i, j: (0, j, 0, 0)),
            pl.BlockSpec((2, RNN_BLOCK), lambda i, j: (0, j)),
            pl.BlockSpec((2, RNN_BLOCK), lambda i, j: (0, j)),
            pl.BlockSpec((2, RNN_BLOCK), lambda i, j: (0, j)),
        ],
        out_specs=pl.BlockSpec((1, s, RNN_BLOCK), lambda i, j: (i, 0, j)),
        out_shape=jax.ShapeDtypeStruct((b, s, width), BF16),
        scratch_shapes=[
            pltpu.VMEM((s + 2 * RG_PAD, RNN_BLOCK), F32),
            pltpu.VMEM((2, s, RNN_BLOCK), F32),
            pltpu.VMEM((2, s, RNN_BLOCK), F32),
            pltpu.VMEM((2, s, RNN_BLOCK), F32),
            pltpu.VMEM((2, s, RNN_BLOCK), F32),
            pltpu.VMEM((2, s, RNN_BLOCK), F32),
        ],
        compiler_params=_cparams("parallel", "parallel"),
        name="rglru",
    )(z, z, conv_w, conv_b, wa, wx, ba, bx, lam)


ATTN_TQ = 128
ATTN_TK = ATTN_TQ + 2 * ATTN_HALF
ATTN_UNROLL = 32
LOG2E = math.log2(math.e)


def _attn_body(q_ref, k_ref, v_ref, slope_ref, o_ref, bias_ref, x4_ref, og_ref, dg_ref, mg_ref,
               *, s):
    qscale = ATTN_DH ** -0.5 * LOG2E

    @pl.when(pl.program_id(1) == 0)
    def _():
        slope = slope_ref[0, 0:1, 0:1] * LOG2E
        qi = lax.broadcasted_iota(jnp.int32, (ATTN_TQ, ATTN_TK), 0)
        kj = lax.broadcasted_iota(jnp.int32, (ATTN_TQ, ATTN_TK), 1)
        for g, dil in enumerate(ATTN_DILATIONS):
            for e in range(3):
                rel = jnp.abs(kj - e * ATTN_HALF - qi)
                pen = slope * (rel * dil).astype(F32)
                bias_ref[g, e] = jnp.where(rel <= ATTN_HALF, -pen, NEG_BIG)

    s4 = s // 4
    srcs = (q_ref, k_ref, v_ref)

    def split4(t, carry):
        c = t // (s4 // ATTN_TK)
        p0 = (t % (s4 // ATTN_TK)) * ATTN_TK
        dst = pl.ds(pl.multiple_of(c * s4 + p0, ATTN_TK), ATTN_TK)
        for a in range(3):
            x = srcs[a][0, pl.ds(c + 4 * p0, ATTN_TK, stride=4), :]
            x4_ref[a, dst, :] = x * qscale if a == 0 else x
        return carry

    lax.fori_loop(0, 4 * (s4 // ATTN_TK), split4, 0)

    ones = jnp.ones((ATTN_TK, ATTN_DH), BF16)

    for g, dil in reversed(list(enumerate(ATTN_DILATIONS))):
        sp = s // dil
        nqb = sp // ATTN_TQ

        def q_block(t, carry, g=g, dil=dil, sp=sp, nqb=nqb):
            r = t // nqb
            p0 = (t % nqb) * ATTN_TQ
            kstart = jnp.clip(p0 - ATTN_HALF, 0, sp - ATTN_TK)
            e = (p0 - kstart) // ATTN_HALF
            if dil == 1:
                qrows = pl.ds(pl.multiple_of(p0, ATTN_TQ), ATTN_TQ)
                krows = pl.ds(pl.multiple_of(kstart, ATTN_HALF), ATTN_TK)
                qf = q_ref[0, qrows, :] * qscale
                kf, vf = k_ref[0, krows, :], v_ref[0, krows, :]
                orows = qrows
            elif dil == 4:
                qrows = pl.ds(pl.multiple_of(r * s4 + p0, ATTN_TQ), ATTN_TQ)
                krows = pl.ds(pl.multiple_of(r * s4 + kstart, ATTN_HALF), ATTN_TK)
                qf, kf, vf = x4_ref[0, qrows, :], x4_ref[1, krows, :], x4_ref[2, krows, :]
                orows = pl.ds(r + 4 * p0, ATTN_TQ, stride=4)
            else:
                base = (r % 4) * s4 + r // 4
                qrows = pl.ds(base + 4 * p0, ATTN_TQ, stride=4)
                krows = pl.ds(base + 4 * kstart, ATTN_TK, stride=4)
                qf, kf, vf = x4_ref[0, qrows, :], x4_ref[1, krows, :], x4_ref[2, krows, :]
                orows = pl.ds(r + dil * p0, ATTN_TQ, stride=dil)
            qb, kb, vb = qf.astype(BF16), kf.astype(BF16), vf.astype(BF16)
            sc = lax.dot_general(qb, kb, (((1,), (1,)), ((), ())),
                                 preferred_element_type=F32) + bias_ref[g, e]
            m = jnp.max(sc, axis=-1, keepdims=True)
            p = jnp.exp2(sc - m).astype(BF16)
            pv = jnp.dot(p, jnp.concatenate([vb, ones], axis=1), preferred_element_type=F32)
            acc, den = pv[:, :ATTN_DH], pv[:, ATTN_DH:]
            if dil != 1:
                og_ref[g - 1, orows, :] = acc
                dg_ref[g - 1, orows, :] = den
                mg_ref[g - 1, orows, :] = jnp.broadcast_to(m, den.shape)
                return carry
            m4, m16 = mg_ref[0, orows, :], mg_ref[1, orows, :]
            mx = jnp.maximum(jnp.maximum(m4, m16), m)
            e1, e4, e16 = jnp.exp2(m - mx), jnp.exp2(m4 - mx), jnp.exp2(m16 - mx)
            num = e1 * acc + e4 * og_ref[0, orows, :] + e16 * og_ref[1, orows, :]
            tot = e1 * den + e4 * dg_ref[0, orows, :] + e16 * dg_ref[1, orows, :]
            o_ref[0, orows, :] = (num * (1.0 / tot)).astype(o_ref.dtype)
            return carry

        lax.fori_loop(0, dil * nqb, q_block, 0, unroll=ATTN_UNROLL)


def dilated_attention(qkv, slopes):
    b, s, _ = qkv.shape
    h, dh = ATTN_HEADS, ATTN_DH
    assert s % (max(ATTN_DILATIONS) * ATTN_TK) == 0
    return pl.pallas_call(
        functools.partial(_attn_body, s=s),
        grid=(h, b),
        in_specs=[
            pl.BlockSpec((1, s, dh), lambda j, i: (i, 0, j)),
            pl.BlockSpec((1, s, dh), lambda j, i: (i, 0, h + j)),
            pl.BlockSpec((1, s, dh), lambda j, i: (i, 0, 2 * h + j)),
            pl.BlockSpec((1, SUBLANES, LANES), lambda j, i: (j, 0, 0)),
        ],
        out_specs=pl.BlockSpec((1, s, dh), lambda j, i: (i, 0, j)),
        out_shape=jax.ShapeDtypeStruct((b, s, h * dh), BF16),
        scratch_shapes=[
            pltpu.VMEM((len(ATTN_DILATIONS), 3, ATTN_TQ, ATTN_TK), F32),
            pltpu.VMEM((3, s, dh), F32),
            pltpu.VMEM((len(ATTN_DILATIONS) - 1, s, dh), F32),
            pltpu.VMEM((len(ATTN_DILATIONS) - 1, s, dh), F32),
            pltpu.VMEM((len(ATTN_DILATIONS) - 1, s, dh), F32),
        ],
        compiler_params=_cparams("parallel", "arbitrary"),
        name="dilated_attn",
    )(qkv, qkv, qkv, slopes)


MIXER_OUT_TM = 512


FFN_DOWN_TM = 256


def _ffn(x, h, w1, w3, w2, layer, next_norm_g, last):
    u = swiglu_up(h, w1, w3, layer)
    return matmul([(u, w2, 0, layer)], res=x, norm_g=next_norm_g,
                  norm_dtype=F32 if last else BF16, keep_sum=not last,
                  tm=FFN_DOWN_TM, tn=w2.shape[-1], name="ffn_down")


def _even_layer(x, bsz, seq, norm_g, w_in, gate_b, conv_w, conv_b, rg_wa, rg_ba, rg_wx, rg_bx,
                rg_lam, head_g, w_out, next_norm_g):
    heads, dk, dv = MLSTM_HEADS, MLSTM_DK, MLSTM_DV
    n_qkvo = 2 * heads * dk + 2 * heads * dv
    n_gate = 4 * heads
    width = RNN_BLOCKS * RNN_BLOCK
    w_main, w_gate = split_in_proj_weight(w_in, n_qkvo, n_gate)
    w_gate_t = jnp.transpose(w_gate[:, :n_gate]).astype(BF16)
    z, zg_t = norm_matmul(x, norm_g, w_main, w_side_t=w_gate_t, name="in_proj")
    z = z.reshape(bsz, seq, -1)

    chunk = MLSTM_CHUNK
    nc = seq // chunk
    gates_rows = jnp.transpose(zg_t[:n_gate].reshape(4, heads, bsz, nc, chunk), (2, 1, 0, 3, 4))
    gate_b_rows = jnp.broadcast_to(
        jnp.transpose(gate_b.astype(F32).reshape(4, heads))[:, :, None, None], (heads, 4, 1, chunk))
    y_a = mlstm_mixer(z, gates_rows, gate_b_rows, head_g.astype(F32).reshape(heads, 1, dv), chunk)

    y_b = rglru_mixer(z, n_qkvo // RNN_BLOCK, (n_qkvo + width) // RNN_BLOCK,
                      conv_w.astype(F32), conv_b.astype(F32).reshape(1, width),
                      rg_wa.astype(BF16), rg_wx.astype(BF16),
                      rg_ba.astype(F32), rg_bx.astype(F32), rg_lam.astype(F32))

    w_out_b = w_out.astype(BF16)
    m = bsz * seq
    assert heads * dv == width
    return matmul([(y_a.reshape(m, -1), w_out_b, 0),
                   (y_b.reshape(m, -1), w_out_b, 1)], res=x, norm_g=next_norm_g,
                  tm=MIXER_OUT_TM, tn=w_out.shape[1], name="out_proj")


def _odd_layer(x, h, bsz, seq, norm_g, w_qkv, w_o, next_norm_g):
    if h is None:
        qkv = norm_matmul(x, norm_g, w_qkv.astype(BF16), name="qkv_proj")
    else:
        qkv = matmul([(h, w_qkv.astype(BF16))], name="qkv_proj")
    qkv = qkv.reshape(bsz, seq, -1)
    slopes = jnp.exp2(-ALIBI_MAX_BIAS * jnp.arange(1, ATTN_HEADS + 1, dtype=F32) / ATTN_HEADS)
    slopes = jnp.broadcast_to(slopes[:, None, None], (ATTN_HEADS, SUBLANES, LANES))
    o = dilated_attention(qkv, slopes)
    return matmul([(o.reshape(bsz * seq, -1), w_o.astype(BF16))], res=x, norm_g=next_norm_g,
                  tm=MIXER_OUT_TM, tn=w_o.shape[1], name="attn_out_proj")


def kernel(x, e_norm, e_w_in, e_gate_b, e_conv_w, e_conv_b, e_rg_wa, e_rg_ba, e_rg_wx, e_rg_bx,
           e_rg_lam, e_head_g, e_w_out, o_norm, o_w_qkv, o_w_o, f_norm, f_w1, f_w3, f_w2,
           final_norm):
    bsz, seq, d = x.shape
    depth = f_norm.shape[0]
    xs = x.reshape(bsz * seq, d).astype(F32)
    w1, w3, w2 = f_w1.astype(BF16), f_w3.astype(BF16), f_w2.astype(BF16)
    hs = None
    for l in range(depth):
        if l % 2 == 0:
            e = l // 2
            xs, hs = _even_layer(xs, bsz, seq, e_norm[e], e_w_in[e], e_gate_b[e], e_conv_w[e],
                                 e_conv_b[e], e_rg_wa[e], e_rg_ba[e], e_rg_wx[e], e_rg_bx[e],
                                 e_rg_lam[e], e_head_g[e], e_w_out[e], f_norm[l])
        else:
            o = l // 2
            xs, hs = _odd_layer(xs, hs, bsz, seq, o_norm[o], o_w_qkv[o], o_w_o[o], f_norm[l])
        if l == depth - 1:
            out = _ffn(xs, hs, w1, w3, w2, l, final_norm, last=True)
            return out.astype(x.dtype).reshape(bsz, seq, d)
        if (l + 1) % 2:
            xs, hs = _ffn(xs, hs, w1, w3, w2, l, o_norm[(l + 1) // 2], last=False)
        else:
            xs, hs = _ffn(xs, hs, w1, w3, w2, l, None, last=False), None
```

```python
import functools
import math

import jax
import jax.numpy as jnp
from jax import lax
from jax.experimental import pallas as pl
from jax.experimental.pallas import tpu as pltpu

F32 = jnp.float32
BF16 = jnp.bfloat16

RMS_EPS = 1e-6
NEG_BIG = -1e30

MLSTM_HEADS = 4
MLSTM_DK = 128
MLSTM_DV = 256
MLSTM_CHUNK = 256
RNN_BLOCKS = 8
RNN_BLOCK = 128
CONV_WIDTH = 4
CONV_LEFT = 2
RGLRU_C = 8.0
ATTN_HEADS = 16
ATTN_DH = 128
ATTN_HALF = 64
ATTN_DILATIONS = (1, 4, 16)
ALIBI_MAX_BIAS = 8.0

LANES = 128
SUBLANES = 8
VMEM_LIMIT_BYTES = 56 * 1024 * 1024


def _cparams(*semantics):
    return pltpu.CompilerParams(dimension_semantics=semantics,
                                vmem_limit_bytes=VMEM_LIMIT_BYTES)


NORM_ROWS = 128


def _norm_rows_into(x_ref, g_ref, h_ref):
    g = g_ref[...]

    def rows(t, carry):
        sl = pl.ds(pl.multiple_of(t * NORM_ROWS, NORM_ROWS), NORM_ROWS)
        x = x_ref[sl, :]
        ms = jnp.mean(x * x, axis=-1, keepdims=True)
        h_ref[sl, :] = (x * lax.rsqrt(ms + RMS_EPS) * g).astype(h_ref.dtype)
        return carry

    lax.fori_loop(0, x_ref.shape[0] // NORM_ROWS, rows, 0)


def _weight_spec(w, kdim, tn, row_block=0, layer=None, **mode):
    if w.ndim == 2:
        return pl.BlockSpec((kdim, tn), lambda i, j: (row_block, j), **mode)
    return pl.BlockSpec((None, kdim, tn), lambda i, j: (layer, row_block, j), **mode)


def _matmul_body(*refs, n_pairs, has_res, has_norm, keep_sum):
    pairs = [(refs[2 * p], refs[2 * p + 1]) for p in range(n_pairs)]
    rest = list(refs[2 * n_pairs:])
    r_ref = rest.pop(0) if has_res else None
    g_ref = rest.pop(0) if has_norm else None
    y_ref = rest.pop(0) if keep_sum else rest.pop(-1)

    acc = None
    for a_ref, w_ref in pairs:
        d = jnp.dot(a_ref[...], w_ref[...], preferred_element_type=F32)
        acc = d if acc is None else acc + d
    if has_res:
        acc = acc + r_ref[...]
    y_ref[...] = acc.astype(y_ref.dtype)
    if has_norm:
        _norm_rows_into(y_ref, g_ref, rest.pop(0))


def matmul(pairs, res=None, norm_g=None, norm_dtype=BF16, keep_sum=True, tm=1024, tn=1024,
           name="matmul"):
    m = pairs[0][0].shape[0]
    n = pairs[0][1].shape[-1]
    tn = min(tn, n)
    assert m % tm == 0 and n % tn == 0 and (keep_sum or norm_g is not None)
    assert norm_g is None or (tn == n and tm % NORM_ROWS == 0)
    w_mode = dict(pipeline_mode=pl.Buffered(1)) if tn == n else {}
    in_specs, args = [], []
    for a, w, *where in pairs:
        kdim = a.shape[1]
        kb, layer = (list(where) + [0, None])[:2] if where else (0, None)
        in_specs.append(pl.BlockSpec((tm, kdim), lambda i, j: (i, 0)))
        in_specs.append(_weight_spec(w, kdim, tn, kb, layer, **w_mode))
        args += [a, w]
    if res is not None:
        in_specs.append(pl.BlockSpec((tm, tn), lambda i, j: (i, j)))
        args.append(res)
    out_specs, out_shape, scratch = [], [], []
    if keep_sum:
        out_specs.append(pl.BlockSpec((tm, tn), lambda i, j: (i, j)))
        out_shape.append(jax.ShapeDtypeStruct((m, n), F32))
    else:
        scratch.append(pltpu.VMEM((tm, n), F32))
    if norm_g is not None:
        in_specs.append(pl.BlockSpec((1, n), lambda i, j: (0, 0)))
        args.append(norm_g.reshape(1, n).astype(F32))
        out_specs.append(pl.BlockSpec((tm, n), lambda i, j: (i, 0)))
        out_shape.append(jax.ShapeDtypeStruct((m, n), norm_dtype))
    outs = pl.pallas_call(
        functools.partial(_matmul_body, n_pairs=len(pairs), has_res=res is not None,
                          has_norm=norm_g is not None, keep_sum=keep_sum),
        grid=(m // tm, n // tn),
        in_specs=in_specs,
        out_specs=out_specs,
        out_shape=out_shape,
        scratch_shapes=scratch,
        compiler_params=_cparams("parallel", "parallel"),
        name=name,
    )(*args)
    return outs if len(outs) > 1 else outs[0]


SWIGLU_ROWS = 1024


def _swiglu_up_body(a_ref, w1_ref, w3_ref, o_ref):
    rows = min(SWIGLU_ROWS, a_ref.shape[0])
    for r0 in range(0, a_ref.shape[0], rows):
        a = a_ref[pl.ds(r0, rows), :]
        u = jnp.dot(a, w1_ref[...], preferred_element_type=F32)
        v = jnp.dot(a, w3_ref[...], preferred_element_type=F32)
        o_ref[pl.ds(r0, rows), :] = (u * jax.nn.sigmoid(u) * v).astype(o_ref.dtype)


def swiglu_up(a, w1, w3, layer, tm=4096, tn=512):
    m, kdim = a.shape
    n = w1.shape[-1]
    assert m % tm == 0 and n % tn == 0
    return pl.pallas_call(
        _swiglu_up_body,
        grid=(m // tm, n // tn),
        in_specs=[pl.BlockSpec((tm, kdim), lambda i, j: (i, 0)),
                  _weight_spec(w1, kdim, tn, layer=layer),
                  _weight_spec(w3, kdim, tn, layer=layer)],
        out_specs=pl.BlockSpec((tm, tn), lambda i, j: (i, j)),
        out_shape=jax.ShapeDtypeStruct((m, n), BF16),
        compiler_params=_cparams("parallel", "parallel"),
        name="swiglu_up",
    )(a, w1, w3)


NORM_CHUNK = 256


def _norm_matmul_body(*refs, has_side, w_transposed):
    if has_side:
        x_ref, g_ref, w_ref, ws_ref, o_ref, os_ref, h_ref = refs
    else:
        x_ref, g_ref, w_ref, o_ref, h_ref = refs
    w_dims = (((1,), (1,)), ((), ())) if w_transposed else (((1,), (0,)), ((), ()))

    @pl.when(pl.program_id(1) == 0)
    def _():
        g = g_ref[...]
        for r0 in range(0, x_ref.shape[0], NORM_CHUNK):
            sl = pl.ds(r0, NORM_CHUNK)
            x = x_ref[sl, :]
            ms = jnp.mean(x * x, axis=-1, keepdims=True)
            h = (x * lax.rsqrt(ms + RMS_EPS) * g).astype(h_ref.dtype)
            h_ref[sl, :] = h
            o_ref[sl, :] = lax.dot_general(h, w_ref[...], w_dims,
                                           preferred_element_type=F32).astype(o_ref.dtype)
            if has_side:
                os_ref[:, sl] = lax.dot_general(ws_ref[...], h, (((1,), (1,)), ((), ())),
                                                preferred_element_type=F32)

    @pl.when(pl.program_id(1) > 0)
    def _():
        o_ref[...] = lax.dot_general(h_ref[...], w_ref[...], w_dims,
                                     preferred_element_type=F32).astype(o_ref.dtype)


def norm_matmul(x, g, w, w_side_t=None, w_transposed=False, out_dtype=F32, tm=1024, tn=1024,
                name="norm_matmul"):
    m, d = x.shape
    n = w.shape[0] if w_transposed else w.shape[1]
    assert m % tm == 0 and n % tn == 0 and tm % NORM_CHUNK == 0
    in_specs = [pl.BlockSpec((tm, d), lambda i, j: (i, 0)),
                pl.BlockSpec((1, d), lambda i, j: (0, 0)),
                pl.BlockSpec((tn, d), lambda i, j: (j, 0)) if w_transposed
                else pl.BlockSpec((d, tn), lambda i, j: (0, j))]
    args = [x, g.reshape(1, d).astype(F32), w]
    out_specs = [pl.BlockSpec((tm, tn), lambda i, j: (i, j))]
    out_shape = [jax.ShapeDtypeStruct((m, n), out_dtype)]
    if w_side_t is not None:
        ns = w_side_t.shape[0]
        in_specs.append(pl.BlockSpec((ns, d), lambda i, j: (0, 0)))
        args.append(w_side_t)
        out_specs.append(pl.BlockSpec((ns, tm), lambda i, j: (0, i)))
        out_shape.append(jax.ShapeDtypeStruct((ns, m), F32))
    outs = pl.pallas_call(
        functools.partial(_norm_matmul_body, has_side=w_side_t is not None,
                          w_transposed=w_transposed),
        grid=(m // tm, n // tn),
        in_specs=in_specs,
        out_specs=out_specs,
        out_shape=out_shape,
        scratch_shapes=[pltpu.VMEM((tm, d), BF16)],
        compiler_params=_cparams("parallel", "arbitrary"),
        name=name,
    )(*args)
    return outs if w_side_t is not None else outs[0]


def _log_sigmoid(x):
    return jnp.minimum(x, 0.0) - jnp.log1p(jnp.exp(-jnp.abs(x)))


SCAN_RADIX = 4


def _lane_scan(x, combine, identity, reverse):
    width = x.shape[-1]
    lane = lax.broadcasted_iota(jnp.int32, x.shape, 1)

    def shifted(v, sh):
        if reverse:
            return jnp.where(lane < width - sh, pltpu.roll(v, width - sh, axis=1), identity)
        return jnp.where(lane >= sh, pltpu.roll(v, sh, axis=1), identity)

    sh = 1
    while sh < width:
        parts = [shifted(x, k * sh) for k in range(1, SCAN_RADIX) if k * sh < width]
        for p in parts:
            x = combine(x, p)
        sh *= SCAN_RADIX
    return x


def _mlstm_body(q_ref, k_ref, v_ref, og_ref, g_ref, gb_ref, hg_ref, y_ref,
                rows_ref, cols_ref, hf_ref, hb_ref, ct_ref, *, nc, chunk):
    dk, dv = MLSTM_DK, MLSTM_DV
    scale = dk ** -0.5

    g = g_ref[0, 0] + gb_ref[0]
    rows = []
    for d in range(2):
        b = _lane_scan(_log_sigmoid(g[2 * d + 1]), jnp.add, 0.0, reverse=d == 1)
        gg = g[2 * d] - b
        gmax = _lane_scan(gg, jnp.maximum, -jnp.inf, reverse=d == 1)
        rows += [gg, b, gmax]
        for j, r in enumerate((gg, b, gmax)):
            rows_ref[3 * d + j] = r
    rowmat = jnp.concatenate(rows + [jnp.zeros((LANES - 6 * nc, chunk), F32)], axis=0)
    colmat = rowmat.T
    for c in range(nc):
        cols_ref[c] = colmat if c == 0 else pltpu.roll(colmat, LANES - c, axis=1)

    ct_ref[...] = jnp.zeros_like(ct_ref)

    row_id = lax.broadcasted_iota(jnp.int32, (chunk, chunk), 0)
    col_id = lax.broadcasted_iota(jnp.int32, (chunk, chunk), 1)
    causal = (row_id >= col_id, row_id <= col_id)
    ones = jnp.ones((chunk, LANES), BF16)

    def lanes2(x):
        return jnp.concatenate([x] * (dv // LANES), axis=1)

    hg = hg_ref[0]

    def emit(r0, h, own_ref, other_ref):
        rows = pl.ds(r0, chunk)
        if other_ref is None:
            own_ref[rows, :] = h
            return
        hm = h + other_ref[rows, :]
        ms = jnp.mean(hm * hm, axis=-1, keepdims=True)
        hn = hm * lax.rsqrt(ms + RMS_EPS) * hg
        y_ref[0, rows, :] = (hn * jax.nn.sigmoid(og_ref[0, rows, :])).astype(y_ref.dtype)

    def chunk_step(c, m, d, own_ref, other_ref):
        r0 = pl.multiple_of(c * chunk, chunk)
        qb = (q_ref[0, pl.ds(r0, chunk), :] * scale).astype(BF16)
        kb = k_ref[0, pl.ds(r0, chunk), :].astype(BF16)
        vf = v_ref[0, pl.ds(r0, chunk), :]
        cols = cols_ref[c]

        def column(j):
            lane = (3 * d + j) * nc
            return jnp.broadcast_to(cols[:, lane:lane + 1], (chunk, LANES))

        gg_col, b_col, gmax_col = column(0), column(1), column(2)
        gg_row = rows_ref[3 * d, pl.ds(c, 1), :]
        end = chunk - 1 if d == 0 else 0
        total = rows_ref[3 * d + 1, pl.ds(c, 1), :][:, end:end + 1]
        gmax_end = rows_ref[3 * d + 2, pl.ds(c, 1), :][:, end:end + 1]

        m_row = jnp.maximum(gmax_col, m)
        w_intra = jnp.exp(jnp.where(causal[d], gg_row - lanes2(m_row), NEG_BIG))
        w_inter = jnp.exp(m - m_row)
        s = lax.dot_general(qb, kb, (((1,), (1,)), ((), ())),
                            preferred_element_type=F32) * w_intra
        ct = ct_ref[d]
        intra = jnp.dot(s.astype(BF16), jnp.concatenate([vf.astype(BF16), ones], axis=1),
                        preferred_element_type=F32)
        inter = jnp.dot(qb, ct.astype(BF16), preferred_element_type=F32)
        num = intra[:, :dv] + lanes2(w_inter) * inter[:, :dv]
        den = intra[:, dv:] + w_inter * inter[:, dv:]
        inv = 1.0 / jnp.maximum(jnp.abs(den), jnp.exp(-(b_col + m_row)))
        emit(r0, num * lanes2(inv), own_ref, other_ref)

        m_new = jnp.maximum(total + m, total + gmax_end)
        w_src = jnp.exp(total + gg_col - m_new)
        decay = jnp.exp(total + m - m_new)
        wv = jnp.concatenate([lanes2(w_src) * vf, w_src], axis=1).astype(BF16)
        ct_ref[d] = decay * ct + lax.dot_general(
            kb, wv, (((0,), (0,)), ((), ())), preferred_element_type=F32)
        return m_new

    def first_half(c, carry):
        m_f, m_b = carry
        m_f = chunk_step(c, m_f, 0, hf_ref, None)
        m_b = chunk_step(nc - 1 - c, m_b, 1, hb_ref, None)
        return m_f, m_b

    def second_half(c, carry):
        m_f, m_b = carry
        m_f = chunk_step(c, m_f, 0, None, hb_ref)
        m_b = chunk_step(nc - 1 - c, m_b, 1, None, hf_ref)
        return m_f, m_b

    m0 = jnp.full((1, 1), NEG_BIG, F32)
    carry = lax.fori_loop(0, nc // 2, first_half, (m0, m0), unroll=2)
    lax.fori_loop(nc // 2, nc, second_half, carry, unroll=2)


def mlstm_mixer(z, gates_rows, gate_b_rows, head_g, chunk=MLSTM_CHUNK):
    b, s, _ = z.shape
    h, dk, dv = MLSTM_HEADS, MLSTM_DK, MLSTM_DV
    nc = s // chunk
    assert s % chunk == 0 and 6 * nc <= LANES and chunk % LANES == 0 and nc % 2 == 0
    kq = h * dk // dk
    kv = 2 * h * dk // dv
    ko = kv + h
    return pl.pallas_call(
        functools.partial(_mlstm_body, nc=nc, chunk=chunk),
        grid=(b, h),
        in_specs=[
            pl.BlockSpec((1, s, dk), lambda i, j: (i, 0, j)),
            pl.BlockSpec((1, s, dk), lambda i, j: (i, 0, kq + j)),
            pl.BlockSpec((1, s, dv), lambda i, j: (i, 0, kv + j)),
            pl.BlockSpec((1, s, dv), lambda i, j: (i, 0, ko + j)),
            pl.BlockSpec((1, 1, 4, nc, chunk), lambda i, j: (i, j, 0, 0, 0)),
            pl.BlockSpec((1, 4, 1, chunk), lambda i, j: (j, 0, 0, 0)),
            pl.BlockSpec((1, 1, dv), lambda i, j: (j, 0, 0)),
        ],
        out_specs=pl.BlockSpec((1, s, dv), lambda i, j: (i, 0, j)),
        out_shape=jax.ShapeDtypeStruct((b, s, h * dv), BF16),
        scratch_shapes=[
            pltpu.VMEM((6, nc, chunk), F32),
            pltpu.VMEM((nc, chunk, LANES), F32),
            pltpu.VMEM((s, dv), F32),
            pltpu.VMEM((s, dv), F32),
            pltpu.VMEM((2, dk, dv + LANES), F32),
        ],
        compiler_params=_cparams("parallel", "parallel"),
        name="mlstm",
    )(z, z, z, z, gates_rows, gate_b_rows, head_g)


RG_TILE = 512
RG_PAD = SUBLANES
RG_SCAN_UNROLL = 8


def _softplus(x):
    return jnp.maximum(x, 0.0) + jnp.log1p(jnp.exp(-jnp.abs(x)))


def _sigmoid(x):
    return 0.5 * jnp.tanh(0.5 * x) + 0.5


def _gelu_tanh(x):
    c = math.sqrt(2.0 / math.pi)
    return x * (0.5 * (1.0 + jnp.tanh(c * (x + 0.044715 * (x * x * x)))))


def _rglru_body(x_ref, gr_ref, cw_ref, cb_ref, wa_ref, wx_ref, ba_ref, bx_ref, lam_ref, y_ref,
                xpad_ref, a_ref, u_ref, as_ref, us_ref, h_ref, *, s):
    nt = s // RG_TILE
    zeros = jnp.zeros((RG_PAD, LANES), F32)
    xpad_ref[pl.ds(0, RG_PAD), :] = zeros
    xpad_ref[pl.ds(RG_PAD + s, RG_PAD), :] = zeros

    def copy_in(t, carry):
        r0 = pl.multiple_of(t * RG_TILE, RG_TILE)
        xpad_ref[pl.ds(RG_PAD + r0, RG_TILE), :] = x_ref[0, pl.ds(r0, RG_TILE), :]
        return carry

    lax.fori_loop(0, nt, copy_in, 0)

    cw = cw_ref[...]
    cb = cb_ref[...]
    sp = [RGLRU_C * _softplus(-lam_ref[d:d + 1, :]) for d in range(2)]

    def gates(t, carry):
        r0 = pl.multiple_of(t * RG_TILE, RG_TILE)
        xc = None
        for j in range(CONV_WIDTH):
            tap = xpad_ref[pl.ds(r0 + RG_PAD - CONV_LEFT + j, RG_TILE), :] * cw[j:j + 1, :]
            xc = tap if xc is None else xc + tap
        xc = xc + cb
        xcb = xc.astype(BF16)
        for d in range(2):
            r = _sigmoid(jnp.dot(xcb, wa_ref[d, 0], preferred_element_type=F32)
                         + ba_ref[d:d + 1, :])
            i = _sigmoid(jnp.dot(xcb, wx_ref[d, 0], preferred_element_type=F32)
                         + bx_ref[d:d + 1, :])
            neg_log_a = sp[d] * r
            a = jnp.exp(-neg_log_a)
            one_minus_a2 = jnp.tanh(neg_log_a) * (1.0 + a * a)
            root = jnp.where(one_minus_a2 > 0.0, one_minus_a2 * lax.rsqrt(one_minus_a2), 0.0)
            a_ref[d, pl.ds(r0, RG_TILE), :] = a
            u_ref[d, pl.ds(r0, RG_TILE), :] = root * (i * xc)
        return carry

    def block_scan(t, carry):
        r0 = pl.multiple_of(t * RG_TILE, RG_TILE)
        for d in range(2):
            a_prev = u_prev = None
            for r in (range(SUBLANES) if d == 0 else reversed(range(SUBLANES))):
                rows = pl.ds(r0 + r, RG_TILE // SUBLANES, stride=SUBLANES)
                a, u = a_ref[d, rows, :], u_ref[d, rows, :]
                if a_prev is not None:
                    u = a * u_prev + u
                    a = a * a_prev
                as_ref[d, rows, :] = a
                us_ref[d, rows, :] = u
                a_prev, u_prev = a, u
        return carry

    lax.fori_loop(0, nt, gates, 0)
    lax.fori_loop(0, nt, block_scan, 0)

    nblk = s // SUBLANES

    def scan(j, carry):
        h_f, h_b = carry
        for d, last, h in ((0, SUBLANES - 1, h_f), (1, 0, h_b)):
            blk = j if d == 0 else nblk - 1 - j
            rows = pl.ds(pl.multiple_of(blk * SUBLANES, SUBLANES), SUBLANES)
            a, u = as_ref[d, rows, :], us_ref[d, rows, :]
            h_ref[d, rows, :] = a * h + u
            a_end = jnp.broadcast_to(a[last:last + 1, :], (SUBLANES, LANES))
            u_end = jnp.broadcast_to(u[last:last + 1, :], (SUBLANES, LANES))
            if d == 0:
                h_f = a_end * h + u_end
            else:
                h_b = a_end * h + u_end
        return h_f, h_b

    h0 = jnp.zeros((SUBLANES, LANES), F32)
    lax.fori_loop(0, nblk, scan, (h0, h0), unroll=RG_SCAN_UNROLL)

    def finish(t, carry):
        r0 = pl.multiple_of(t * RG_TILE, RG_TILE)
        hr = h_ref[0, pl.ds(r0, RG_TILE), :] + h_ref[1, pl.ds(r0, RG_TILE), :]
        y_ref[0, pl.ds(r0, RG_TILE), :] = (
            hr * _gelu_tanh(gr_ref[0, pl.ds(r0, RG_TILE), :])).astype(y_ref.dtype)
        return carry

    lax.fori_loop(0, nt, finish, 0)


def rglru_mixer(z, xr_block0, gr_block0, conv_w, conv_b, wa, wx, ba, bx, lam):
    b, s, _ = z.shape
    width = RNN_BLOCKS * RNN_BLOCK
    assert s % RG_TILE == 0
    return pl.pallas_call(
        functools.partial(_rglru_body, s=s),
        grid=(b, RNN_BLOCKS),
        in_specs=[
            pl.BlockSpec((1, s, RNN_BLOCK), lambda i, j: (i, 0, xr_block0 + j)),
            pl.BlockSpec((1, s, RNN_BLOCK), lambda i, j: (i, 0, gr_block0 + j)),
            pl.BlockSpec((CONV_WIDTH, RNN_BLOCK), lambda i, j: (0, j)),
            pl.BlockSpec((1, RNN_BLOCK), lambda i, j: (0, j)),
            pl.BlockSpec((2, 1, RNN_BLOCK, RNN_BLOCK), lambda i, j: (0, j, 0, 0)),
            pl.BlockSpec((2, 1, RNN_BLOCK, RNN_BLOCK), lambda i, j: (0, j, 0, 0)),
            pl.BlockSpec((2, RNN_BLOCK), lambda i, j: (0, j)),
            pl.BlockSpec((2, RNN_BLOCK), lambda i, j: (0, j)),
            pl.BlockSpec((2, RNN_BLOCK), lambda i, j: (0, j)),
        ],
        out_specs=pl.BlockSpec((1, s, RNN_BLOCK), lambda i, j: (i, 0, j)),
        out_shape=jax.ShapeDtypeStruct((b, s, width), BF16),
        scratch_shapes=[
            pltpu.VMEM((s + 2 * RG_PAD, RNN_BLOCK), F32),
            pltpu.VMEM((2, s, RNN_BLOCK), F32),
            pltpu.VMEM((2, s, RNN_BLOCK), F32),
            pltpu.VMEM((2, s, RNN_BLOCK), F32),
            pltpu.VMEM((2, s, RNN_BLOCK), F32),
            pltpu.VMEM((2, s, RNN_BLOCK), F32),
        ],
        compiler_params=_cparams("parallel", "parallel"),
        name="rglru",
    )(z, z, conv_w, conv_b, wa, wx, ba, bx, lam)


ATTN_TQ = 128
ATTN_TK = ATTN_TQ + 2 * ATTN_HALF
ATTN_UNROLL = 32
LOG2E = math.log2(math.e)


def _attn_body(q_ref, k_ref, v_ref, slope_ref, o_ref, bias_ref, x4_ref, og_ref, dg_ref, mg_ref,
               *, s):
    qscale = ATTN_DH ** -0.5 * LOG2E

    @pl.when(pl.program_id(1) == 0)
    def _():
        slope = slope_ref[0, 0:1, 0:1] * LOG2E
        qi = lax.broadcasted_iota(jnp.int32, (ATTN_TQ, ATTN_TK), 0)
        kj = lax.broadcasted_iota(jnp.int32, (ATTN_TQ, ATTN_TK), 1)
        for g, dil in enumerate(ATTN_DILATIONS):
            for e in range(3):
                rel = jnp.abs(kj - e * ATTN_HALF - qi)
                pen = slope * (rel * dil).astype(F32)
                bias_ref[g, e] = jnp.where(rel <= ATTN_HALF, -pen, NEG_BIG)

    s4 = s // 4
    srcs = (q_ref, k_ref, v_ref)

    def split4(t, carry):
        c = t // (s4 // ATTN_TK)
        p0 = (t % (s4 // ATTN_TK)) * ATTN_TK
        dst = pl.ds(pl.multiple_of(c * s4 + p0, ATTN_TK), ATTN_TK)
        for a in range(3):
            x = srcs[a][0, pl.ds(c + 4 * p0, ATTN_TK, stride=4), :]
            x4_ref[a, dst, :] = x * qscale if a == 0 else x
        return carry

    lax.fori_loop(0, 4 * (s4 // ATTN_TK), split4, 0)

    ones = jnp.ones((ATTN_TK, ATTN_DH), BF16)

    for g, dil in reversed(list(enumerate(ATTN_DILATIONS))):
        sp = s // dil
        nqb = sp // ATTN_TQ

        def q_block(t, carry, g=g, dil=dil, sp=sp, nqb=nqb):
            r = t // nqb
            p0 = (t % nqb) * ATTN_TQ
            kstart = jnp.clip(p0 - ATTN_HALF, 0, sp - ATTN_TK)
            e = (p0 - kstart) // ATTN_HALF
            if dil == 1:
                qrows = pl.ds(pl.multiple_of(p0, ATTN_TQ), ATTN_TQ)
                krows = pl.ds(pl.multiple_of(kstart, ATTN_HALF), ATTN_TK)
                qf = q_ref[0, qrows, :] * qscale
                kf, vf = k_ref[0, krows, :], v_ref[0, krows, :]
                orows = qrows
            elif dil == 4:
                qrows = pl.ds(pl.multiple_of(r * s4 + p0, ATTN_TQ), ATTN_TQ)
                krows = pl.ds(pl.multiple_of(r * s4 + kstart, ATTN_HALF), ATTN_TK)
                qf, kf, vf = x4_ref[0, qrows, :], x4_ref[1, krows, :], x4_ref[2, krows, :]
                orows = pl.ds(r + 4 * p0, ATTN_TQ, stride=4)
            else:
                base = (r % 4) * s4 + r // 4
                qrows = pl.ds(base + 4 * p0, ATTN_TQ, stride=4)
                krows = pl.ds(base + 4 * kstart, ATTN_TK, stride=4)
                qf, kf, vf = x4_ref[0, qrows, :], x4_ref[1, krows, :], x4_ref[2, krows, :]
                orows = pl.ds(r + dil * p0, ATTN_TQ, stride=dil)
            qb, kb, vb = qf.astype(BF16), kf.astype(BF16), vf.astype(BF16)
            sc = lax.dot_general(qb, kb, (((1,), (1,)), ((), ())),
                                 preferred_element_type=F32) + bias_ref[g, e]
            m = jnp.max(sc, axis=-1, keepdims=True)
            p = jnp.exp2(sc - m).astype(BF16)
            pv = jnp.dot(p, jnp.concatenate([vb, ones], axis=1), preferred_element_type=F32)
            acc, den = pv[:, :ATTN_DH], pv[:, ATTN_DH:]
            if dil != 1:
                og_ref[g - 1, orows, :] = acc
                dg_ref[g - 1, orows, :] = den
                mg_ref[g - 1, orows, :] = jnp.broadcast_to(m, den.shape)
                return carry
            m4, m16 = mg_ref[0, orows, :], mg_ref[1, orows, :]
            mx = jnp.maximum(jnp.maximum(m4, m16), m)
            e1, e4, e16 = jnp.exp2(m - mx), jnp.exp2(m4 - mx), jnp.exp2(m16 - mx)
            num = e1 * acc + e4 * og_ref[0, orows, :] + e16 * og_ref[1, orows, :]
            tot = e1 * den + e4 * dg_ref[0, orows, :] + e16 * dg_ref[1, orows, :]
            o_ref[0, orows, :] = (num * (1.0 / tot)).astype(o_ref.dtype)
            return carry

        lax.fori_loop(0, dil * nqb, q_block, 0, unroll=ATTN_UNROLL)


def dilated_attention(qkv, slopes):
    b, s, _ = qkv.shape
    h, dh = ATTN_HEADS, ATTN_DH
    assert s % (max(ATTN_DILATIONS) * ATTN_TK) == 0
    return pl.pallas_call(
        functools.partial(_attn_body, s=s),
        grid=(h, b),
        in_specs=[
            pl.BlockSpec((1, s, dh), lambda j, i: (i, 0, j)),
            pl.BlockSpec((1, s, dh), lambda j, i: (i, 0, h + j)),
            pl.BlockSpec((1, s, dh), lambda j, i: (i, 0, 2 * h + j)),
            pl.BlockSpec((1, SUBLANES, LANES), lambda j, i: (j, 0, 0)),
        ],
        out_specs=pl.BlockSpec((1, s, dh), lambda j, i: (i, 0, j)),
        out_shape=jax.ShapeDtypeStruct((b, s, h * dh), BF16),
        scratch_shapes=[
            pltpu.VMEM((len(ATTN_DILATIONS), 3, ATTN_TQ, ATTN_TK), F32),
            pltpu.VMEM((3, s, dh), F32),
            pltpu.VMEM((len(ATTN_DILATIONS) - 1, s, dh), F32),
            pltpu.VMEM((len(ATTN_DILATIONS) - 1, s, dh), F32),
            pltpu.VMEM((len(ATTN_DILATIONS) - 1, s, dh), F32),
        ],
        compiler_params=_cparams("parallel", "arbitrary"),
        name="dilated_attn",
    )(qkv, qkv, qkv, slopes)


MIXER_OUT_TM = 512


FFN_DOWN_TM = 256


def _ffn(x, h, w1, w3, w2, layer, next_norm_g, last):
    u = swiglu_up(h, w1, w3, layer)
    return matmul([(u, w2, 0, layer)], res=x, norm_g=next_norm_g,
                  norm_dtype=F32 if last else BF16, keep_sum=not last,
                  tm=FFN_DOWN_TM, tn=w2.shape[-1], name="ffn_down")


def _even_layer(x, bsz, seq, norm_g, w_in, gate_b, conv_w, conv_b, rg_wa, rg_ba, rg_wx, rg_bx,
                rg_lam, head_g, w_out, next_norm_g):
    heads, dk, dv = MLSTM_HEADS, MLSTM_DK, MLSTM_DV
    n_qkvo = 2 * heads * dk + 2 * heads * dv
    n_gate = 4 * heads
    width = RNN_BLOCKS * RNN_BLOCK
    w_in_t = jnp.transpose(w_in)
    w_main_t = jnp.concatenate([w_in_t[:n_qkvo], w_in_t[n_qkvo + n_gate:]], axis=0).astype(BF16)
    w_gate_t = w_in_t[n_qkvo:n_qkvo + n_gate].astype(BF16)
    z, zg_t = norm_matmul(x, norm_g, w_main_t, w_side_t=w_gate_t, w_transposed=True,
                          name="in_proj")
    z = z.reshape(bsz, seq, -1)

    chunk = MLSTM_CHUNK
    nc = seq // chunk
    gates_rows = jnp.transpose(zg_t[:n_gate].reshape(4, heads, bsz, nc, chunk), (2, 1, 0, 3, 4))
    gate_b_rows = jnp.broadcast_to(
        jnp.transpose(gate_b.astype(F32).reshape(4, heads))[:, :, None, None], (heads, 4, 1, chunk))
    y_a = mlstm_mixer(z, gates_rows, gate_b_rows, head_g.astype(F32).reshape(heads, 1, dv), chunk)

    y_b = rglru_mixer(z, n_qkvo // RNN_BLOCK, (n_qkvo + width) // RNN_BLOCK,
                      conv_w.astype(F32), conv_b.astype(F32).reshape(1, width),
                      rg_wa.astype(BF16), rg_wx.astype(BF16),
                      rg_ba.astype(F32), rg_bx.astype(F32), rg_lam.astype(F32))

    w_out_b = w_out.astype(BF16)
    m = bsz * seq
    assert heads * dv == width
    return matmul([(y_a.reshape(m, -1), w_out_b, 0),
                   (y_b.reshape(m, -1), w_out_b, 1)], res=x, norm_g=next_norm_g,
                  tm=MIXER_OUT_TM, tn=w_out.shape[1], name="out_proj")


def _odd_layer(x, h, bsz, seq, norm_g, w_qkv, w_o, next_norm_g):
    if h is None:
        qkv = norm_matmul(x, norm_g, w_qkv.astype(BF16), name="qkv_proj")
    else:
        qkv = matmul([(h, w_qkv.astype(BF16))], name="qkv_proj")
    qkv = qkv.reshape(bsz, seq, -1)
    slopes = jnp.exp2(-ALIBI_MAX_BIAS * jnp.arange(1, ATTN_HEADS + 1, dtype=F32) / ATTN_HEADS)
    slopes = jnp.broadcast_to(slopes[:, None, None], (ATTN_HEADS, SUBLANES, LANES))
    o = dilated_attention(qkv, slopes)
    return matmul([(o.reshape(bsz * seq, -1), w_o.astype(BF16))], res=x, norm_g=next_norm_g,
                  tm=MIXER_OUT_TM, tn=w_o.shape[1], name="attn_out_proj")


def kernel(x, e_norm, e_w_in, e_gate_b, e_conv_w, e_conv_b, e_rg_wa, e_rg_ba, e_rg_wx, e_rg_bx,
           e_rg_lam, e_head_g, e_w_out, o_norm, o_w_qkv, o_w_o, f_norm, f_w1, f_w3, f_w2,
           final_norm):
    bsz, seq, d = x.shape
    depth = f_norm.shape[0]
    xs = x.reshape(bsz * seq, d).astype(F32)
    w1, w3, w2 = f_w1.astype(BF16), f_w3.astype(BF16), f_w2.astype(BF16)
    hs = None
    for l in range(depth):
        if l % 2 == 0:
            e = l // 2
            xs, hs = _even_layer(xs, bsz, seq, e_norm[e], e_w_in[e], e_gate_b[e], e_conv_w[e],
                                 e_conv_b[e], e_rg_wa[e], e_rg_ba[e], e_rg_wx[e], e_rg_bx[e],
                                 e_rg_lam[e], e_head_g[e], e_w_out[e], f_norm[l])
        else:
            o = l // 2
            xs, hs = _odd_layer(xs, hs, bsz, seq, o_norm[o], o_w_qkv[o], o_w_o[o], f_norm[l])
        if l == depth - 1:
            out = _ffn(xs, hs, w1, w3, w2, l, final_norm, last=True)
            return out.astype(x.dtype).reshape(bsz, seq, d)
        if (l + 1) % 2:
            xs, hs = _ffn(xs, hs, w1, w3, w2, l, o_norm[(l + 1) // 2], last=False)
        else:
            xs, hs = _ffn(xs, hs, w1, w3, w2, l, None, last=False), None
```

```python
import functools
import math

import jax
import jax.numpy as jnp
from jax import lax
from jax.experimental import pallas as pl
from jax.experimental.pallas import tpu as pltpu

F32 = jnp.float32
BF16 = jnp.bfloat16

RMS_EPS = 1e-6
NEG_BIG = -1e30

MLSTM_HEADS = 4
MLSTM_DK = 128
MLSTM_DV = 256
MLSTM_CHUNK = 256
RNN_BLOCKS = 8
RNN_BLOCK = 128
CONV_WIDTH = 4
CONV_LEFT = 2
RGLRU_C = 8.0
ATTN_HEADS = 16
ATTN_DH = 128
ATTN_HALF = 64
ATTN_DILATIONS = (1, 4, 16)
ALIBI_MAX_BIAS = 8.0

LANES = 128
SUBLANES = 8
VMEM_LIMIT_BYTES = 56 * 1024 * 1024


def _cparams(*semantics):
    return pltpu.CompilerParams(dimension_semantics=semantics,
                                vmem_limit_bytes=VMEM_LIMIT_BYTES)


NORM_ROWS = 128


def _norm_rows_into(x_ref, g_ref, h_ref):
    g = g_ref[...]

    def rows(t, carry):
        sl = pl.ds(pl.multiple_of(t * NORM_ROWS, NORM_ROWS), NORM_ROWS)
        x = x_ref[sl, :]
        ms = jnp.mean(x * x, axis=-1, keepdims=True)
        h_ref[sl, :] = (x * lax.rsqrt(ms + RMS_EPS) * g).astype(h_ref.dtype)
        return carry

    lax.fori_loop(0, x_ref.shape[0] // NORM_ROWS, rows, 0)


def _weight_spec(w, kdim, tn, row_block=0, layer=None, **mode):
    if w.ndim == 2:
        return pl.BlockSpec((kdim, tn), lambda i, j: (row_block, j), **mode)
    return pl.BlockSpec((None, kdim, tn), lambda i, j: (layer, row_block, j), **mode)


def _matmul_body(*refs, n_pairs, has_res, has_norm, keep_sum):
    pairs = [(refs[2 * p], refs[2 * p + 1]) for p in range(n_pairs)]
    rest = list(refs[2 * n_pairs:])
    r_ref = rest.pop(0) if has_res else None
    g_ref = rest.pop(0) if has_norm else None
    y_ref = rest.pop(0) if keep_sum else rest.pop(-1)

    acc = None
    for a_ref, w_ref in pairs:
        d = jnp.dot(a_ref[...], w_ref[...], preferred_element_type=F32)
        acc = d if acc is None else acc + d
    if has_res:
        acc = acc + r_ref[...]
    y_ref[...] = acc.astype(y_ref.dtype)
    if has_norm:
        _norm_rows_into(y_ref, g_ref, rest.pop(0))


def matmul(pairs, res=None, norm_g=None, norm_dtype=BF16, keep_sum=True, tm=1024, tn=1024,
           name="matmul"):
    m = pairs[0][0].shape[0]
    n = pairs[0][1].shape[-1]
    tn = min(tn, n)
    assert m % tm == 0 and n % tn == 0 and (keep_sum or norm_g is not None)
    assert norm_g is None or (tn == n and tm % NORM_ROWS == 0)
    w_mode = dict(pipeline_mode=pl.Buffered(1)) if tn == n else {}
    in_specs, args = [], []
    for a, w, *where in pairs:
        kdim = a.shape[1]
        kb, layer = (list(where) + [0, None])[:2] if where else (0, None)
        in_specs.append(pl.BlockSpec((tm, kdim), lambda i, j: (i, 0)))
        in_specs.append(_weight_spec(w, kdim, tn, kb, layer, **w_mode))
        args += [a, w]
    if res is not None:
        in_specs.append(pl.BlockSpec((tm, tn), lambda i, j: (i, j)))
        args.append(res)
    out_specs, out_shape, scratch = [], [], []
    if keep_sum:
        out_specs.append(pl.BlockSpec((tm, tn), lambda i, j: (i, j)))
        out_shape.append(jax.ShapeDtypeStruct((m, n), F32))
    else:
        scratch.append(pltpu.VMEM((tm, n), F32))
    if norm_g is not None:
        in_specs.append(pl.BlockSpec((1, n), lambda i, j: (0, 0)))
        args.append(norm_g.reshape(1, n).astype(F32))
        out_specs.append(pl.BlockSpec((tm, n), lambda i, j: (i, 0)))
        out_shape.append(jax.ShapeDtypeStruct((m, n), norm_dtype))
    outs = pl.pallas_call(
        functools.partial(_matmul_body, n_pairs=len(pairs), has_res=res is not None,
                          has_norm=norm_g is not None, keep_sum=keep_sum),
        grid=(m // tm, n // tn),
        in_specs=in_specs,
        out_specs=out_specs,
        out_shape=out_shape,
        scratch_shapes=scratch,
        compiler_params=_cparams("parallel", "parallel"),
        name=name,
    )(*args)
    return outs if len(outs) > 1 else outs[0]


SWIGLU_ROWS = 1024


def _swiglu_up_body(a_ref, w1_ref, w3_ref, o_ref):
    rows = min(SWIGLU_ROWS, a_ref.shape[0])
    for r0 in range(0, a_ref.shape[0], rows):
        a = a_ref[pl.ds(r0, rows), :]
        u = jnp.dot(a, w1_ref[...], preferred_element_type=F32)
        v = jnp.dot(a, w3_ref[...], preferred_element_type=F32)
        o_ref[pl.ds(r0, rows), :] = (u * jax.nn.sigmoid(u) * v).astype(o_ref.dtype)


def swiglu_up(a, w1, w3, layer, tm=4096, tn=512):
    m, kdim = a.shape
    n = w1.shape[-1]
    assert m % tm == 0 and n % tn == 0
    return pl.pallas_call(
        _swiglu_up_body,
        grid=(m // tm, n // tn),
        in_specs=[pl.BlockSpec((tm, kdim), lambda i, j: (i, 0)),
                  _weight_spec(w1, kdim, tn, layer=layer),
                  _weight_spec(w3, kdim, tn, layer=layer)],
        out_specs=pl.BlockSpec((tm, tn), lambda i, j: (i, j)),
        out_shape=jax.ShapeDtypeStruct((m, n), BF16),
        compiler_params=_cparams("parallel", "parallel"),
        name="swiglu_up",
    )(a, w1, w3)


NORM_CHUNK = 256


def _norm_matmul_body(*refs, has_side, w_transposed):
    if has_side:
        x_ref, g_ref, w_ref, ws_ref, o_ref, os_ref, h_ref = refs
    else:
        x_ref, g_ref, w_ref, o_ref, h_ref = refs
    w_dims = (((1,), (1,)), ((), ())) if w_transposed else (((1,), (0,)), ((), ()))

    @pl.when(pl.program_id(1) == 0)
    def _():
        g = g_ref[...]
        for r0 in range(0, x_ref.shape[0], NORM_CHUNK):
            sl = pl.ds(r0, NORM_CHUNK)
            x = x_ref[sl, :]
            ms = jnp.mean(x * x, axis=-1, keepdims=True)
            h = (x * lax.rsqrt(ms + RMS_EPS) * g).astype(h_ref.dtype)
            h_ref[sl, :] = h
            o_ref[sl, :] = lax.dot_general(h, w_ref[...], w_dims,
                                           preferred_element_type=F32).astype(o_ref.dtype)
            if has_side:
                os_ref[:, sl] = lax.dot_general(ws_ref[...], h, (((1,), (1,)), ((), ())),
                                                preferred_element_type=F32)

    @pl.when(pl.program_id(1) > 0)
    def _():
        o_ref[...] = lax.dot_general(h_ref[...], w_ref[...], w_dims,
                                     preferred_element_type=F32).astype(o_ref.dtype)


def norm_matmul(x, g, w, w_side_t=None, w_transposed=False, out_dtype=F32, tm=1024, tn=1024,
                name="norm_matmul"):
    m, d = x.shape
    n = w.shape[0] if w_transposed else w.shape[1]
    assert m % tm == 0 and n % tn == 0 and tm % NORM_CHUNK == 0
    in_specs = [pl.BlockSpec((tm, d), lambda i, j: (i, 0)),
                pl.BlockSpec((1, d), lambda i, j: (0, 0)),
                pl.BlockSpec((tn, d), lambda i, j: (j, 0)) if w_transposed
                else pl.BlockSpec((d, tn), lambda i, j: (0, j))]
    args = [x, g.reshape(1, d).astype(F32), w]
    out_specs = [pl.BlockSpec((tm, tn), lambda i, j: (i, j))]
    out_shape = [jax.ShapeDtypeStruct((m, n), out_dtype)]
    if w_side_t is not None:
        ns = w_side_t.shape[0]
        in_specs.append(pl.BlockSpec((ns, d), lambda i, j: (0, 0)))
        args.append(w_side_t)
        out_specs.append(pl.BlockSpec((ns, tm), lambda i, j: (0, i)))
        out_shape.append(jax.ShapeDtypeStruct((ns, m), F32))
    outs = pl.pallas_call(
        functools.partial(_norm_matmul_body, has_side=w_side_t is not None,
                          w_transposed=w_transposed),
        grid=(m // tm, n // tn),
        in_specs=in_specs,
        out_specs=out_specs,
        out_shape=out_shape,
        scratch_shapes=[pltpu.VMEM((tm, d), BF16)],
        compiler_params=_cparams("parallel", "arbitrary"),
        name=name,
    )(*args)
    return outs if w_side_t is not None else outs[0]


def _log_sigmoid(x):
    return jnp.minimum(x, 0.0) - jnp.log1p(jnp.exp(-jnp.abs(x)))


SCAN_RADIX = 4


def _lane_scan(x, combine, identity, reverse):
    width = x.shape[-1]
    lane = lax.broadcasted_iota(jnp.int32, x.shape, 1)

    def shifted(v, sh):
        if reverse:
            return jnp.where(lane < width - sh, pltpu.roll(v, width - sh, axis=1), identity)
        return jnp.where(lane >= sh, pltpu.roll(v, sh, axis=1), identity)

    sh = 1
    while sh < width:
        parts = [shifted(x, k * sh) for k in range(1, SCAN_RADIX) if k * sh < width]
        for p in parts:
            x = combine(x, p)
        sh *= SCAN_RADIX
    return x


def _mlstm_body(q_ref, k_ref, v_ref, og_ref, g_ref, gb_ref, hg_ref, y_ref,
                rows_ref, cols_ref, hf_ref, hb_ref, ct_ref, *, nc, chunk):
    dk, dv = MLSTM_DK, MLSTM_DV
    scale = dk ** -0.5

    g = g_ref[0, 0] + gb_ref[0]
    rows = []
    for d in range(2):
        b = _lane_scan(_log_sigmoid(g[2 * d + 1]), jnp.add, 0.0, reverse=d == 1)
        gg = g[2 * d] - b
        gmax = _lane_scan(gg, jnp.maximum, -jnp.inf, reverse=d == 1)
        rows += [gg, b, gmax]
        for j, r in enumerate((gg, b, gmax)):
            rows_ref[3 * d + j] = r
    rowmat = jnp.concatenate(rows + [jnp.zeros((LANES - 6 * nc, chunk), F32)], axis=0)
    colmat = rowmat.T
    for c in range(nc):
        cols_ref[c] = colmat if c == 0 else pltpu.roll(colmat, LANES - c, axis=1)

    ct_ref[...] = jnp.zeros_like(ct_ref)

    row_id = lax.broadcasted_iota(jnp.int32, (chunk, chunk), 0)
    col_id = lax.broadcasted_iota(jnp.int32, (chunk, chunk), 1)
    causal = (row_id >= col_id, row_id <= col_id)
    ones = jnp.ones((chunk, LANES), BF16)

    def lanes2(x):
        return jnp.concatenate([x] * (dv // LANES), axis=1)

    hg = hg_ref[0]

    def emit(r0, h, own_ref, other_ref):
        rows = pl.ds(r0, chunk)
        if other_ref is None:
            own_ref[rows, :] = h
            return
        hm = h + other_ref[rows, :]
        ms = jnp.mean(hm * hm, axis=-1, keepdims=True)
        hn = hm * lax.rsqrt(ms + RMS_EPS) * hg
        y_ref[0, rows, :] = (hn * jax.nn.sigmoid(og_ref[0, rows, :])).astype(y_ref.dtype)

    def chunk_step(c, m, d, own_ref, other_ref):
        r0 = pl.multiple_of(c * chunk, chunk)
        qb = (q_ref[0, pl.ds(r0, chunk), :] * scale).astype(BF16)
        kb = k_ref[0, pl.ds(r0, chunk), :].astype(BF16)
        vf = v_ref[0, pl.ds(r0, chunk), :]
        cols = cols_ref[c]

        def column(j):
            lane = (3 * d + j) * nc
            return jnp.broadcast_to(cols[:, lane:lane + 1], (chunk, LANES))

        gg_col, b_col, gmax_col = column(0), column(1), column(2)
        gg_row = rows_ref[3 * d, pl.ds(c, 1), :]
        end = chunk - 1 if d == 0 else 0
        total = rows_ref[3 * d + 1, pl.ds(c, 1), :][:, end:end + 1]
        gmax_end = rows_ref[3 * d + 2, pl.ds(c, 1), :][:, end:end + 1]

        m_row = jnp.maximum(gmax_col, m)
        w_intra = jnp.exp(jnp.where(causal[d], gg_row - lanes2(m_row), NEG_BIG))
        w_inter = jnp.exp(m - m_row)
        s = lax.dot_general(qb, kb, (((1,), (1,)), ((), ())),
                            preferred_element_type=F32) * w_intra
        ct = ct_ref[d]
        intra = jnp.dot(s.astype(BF16), jnp.concatenate([vf.astype(BF16), ones], axis=1),
                        preferred_element_type=F32)
        inter = jnp.dot(qb, ct.astype(BF16), preferred_element_type=F32)
        num = intra[:, :dv] + lanes2(w_inter) * inter[:, :dv]
        den = intra[:, dv:] + w_inter * inter[:, dv:]
        inv = 1.0 / jnp.maximum(jnp.abs(den), jnp.exp(-(b_col + m_row)))
        emit(r0, num * lanes2(inv), own_ref, other_ref)

        m_new = jnp.maximum(total + m, total + gmax_end)
        w_src = jnp.exp(total + gg_col - m_new)
        decay = jnp.exp(total + m - m_new)
        wv = jnp.concatenate([lanes2(w_src) * vf, w_src], axis=1).astype(BF16)
        ct_ref[d] = decay * ct + lax.dot_general(
            kb, wv, (((0,), (0,)), ((), ())), preferred_element_type=F32)
        return m_new

    def first_half(c, carry):
        m_f, m_b = carry
        m_f = chunk_step(c, m_f, 0, hf_ref, None)
        m_b = chunk_step(nc - 1 - c, m_b, 1, hb_ref, None)
        return m_f, m_b

    def second_half(c, carry):
        m_f, m_b = carry
        m_f = chunk_step(c, m_f, 0, None, hb_ref)
        m_b = chunk_step(nc - 1 - c, m_b, 1, None, hf_ref)
        return m_f, m_b

    m0 = jnp.full((1, 1), NEG_BIG, F32)
    carry = lax.fori_loop(0, nc // 2, first_half, (m0, m0), unroll=2)
    lax.fori_loop(nc // 2, nc, second_half, carry, unroll=2)


def mlstm_mixer(z, gates_rows, gate_b_rows, head_g, chunk=MLSTM_CHUNK):
    b, s, _ = z.shape
    h, dk, dv = MLSTM_HEADS, MLSTM_DK, MLSTM_DV
    nc = s // chunk
    assert s % chunk == 0 and 6 * nc <= LANES and chunk % LANES == 0 and nc % 2 == 0
    kq = h * dk // dk
    kv = 2 * h * dk // dv
    ko = kv + h
    return pl.pallas_call(
        functools.partial(_mlstm_body, nc=nc, chunk=chunk),
        grid=(b, h),
        in_specs=[
            pl.BlockSpec((1, s, dk), lambda i, j: (i, 0, j)),
            pl.BlockSpec((1, s, dk), lambda i, j: (i, 0, kq + j)),
            pl.BlockSpec((1, s, dv), lambda i, j: (i, 0, kv + j)),
            pl.BlockSpec((1, s, dv), lambda i, j: (i, 0, ko + j)),
            pl.BlockSpec((1, 1, 4, nc, chunk), lambda i, j: (i, j, 0, 0, 0)),
            pl.BlockSpec((1, 4, 1, chunk), lambda i, j: (j, 0, 0, 0)),
            pl.BlockSpec((1, 1, dv), lambda i, j: (j, 0, 0)),
        ],
        out_specs=pl.BlockSpec((1, s, dv), lambda i, j: (i, 0, j)),
        out_shape=jax.ShapeDtypeStruct((b, s, h * dv), BF16),
        scratch_shapes=[
            pltpu.VMEM((6, nc, chunk), F32),
            pltpu.VMEM((nc, chunk, LANES), F32),
            pltpu.VMEM((s, dv), F32),
            pltpu.VMEM((s, dv), F32),
            pltpu.VMEM((2, dk, dv + LANES), F32),
        ],
        compiler_params=_cparams("parallel", "parallel"),
        name="mlstm",
    )(z, z, z, z, gates_rows, gate_b_rows, head_g)


RG_TILE = 512
RG_PAD = SUBLANES
RG_SCAN_UNROLL = 8


def _softplus(x):
    return jnp.maximum(x, 0.0) + jnp.log1p(jnp.exp(-jnp.abs(x)))


def _gelu_tanh(x):
    c = math.sqrt(2.0 / math.pi)
    return x * (0.5 * (1.0 + jnp.tanh(c * (x + 0.044715 * (x * x * x)))))


def _rglru_body(x_ref, gr_ref, cw_ref, cb_ref, wa_ref, wx_ref, ba_ref, bx_ref, lam_ref, y_ref,
                xpad_ref, a_ref, u_ref, as_ref, us_ref, h_ref, *, s):
    nt = s // RG_TILE
    zeros = jnp.zeros((RG_PAD, LANES), F32)
    xpad_ref[pl.ds(0, RG_PAD), :] = zeros
    xpad_ref[pl.ds(RG_PAD + s, RG_PAD), :] = zeros

    def copy_in(t, carry):
        r0 = pl.multiple_of(t * RG_TILE, RG_TILE)
        xpad_ref[pl.ds(RG_PAD + r0, RG_TILE), :] = x_ref[0, pl.ds(r0, RG_TILE), :]
        return carry

    lax.fori_loop(0, nt, copy_in, 0)

    cw = cw_ref[...]
    cb = cb_ref[...]
    half_sp = [0.5 * RGLRU_C * _softplus(-lam_ref[d:d + 1, :]) for d in range(2)]

    def gates(t, carry):
        r0 = pl.multiple_of(t * RG_TILE, RG_TILE)
        xc = None
        for j in range(CONV_WIDTH):
            tap = xpad_ref[pl.ds(r0 + RG_PAD - CONV_LEFT + j, RG_TILE), :] * cw[j:j + 1, :]
            xc = tap if xc is None else xc + tap
        xc = xc + cb
        xcb = xc.astype(BF16)
        half_xc = 0.5 * xc
        for d in range(2):
            t_r = jnp.tanh(jnp.dot(xcb, wa_ref[d, 0], preferred_element_type=F32)
                           + ba_ref[d:d + 1, :])
            t_i = jnp.tanh(jnp.dot(xcb, wx_ref[d, 0], preferred_element_type=F32)
                           + bx_ref[d:d + 1, :])
            neg_log_a = half_sp[d] * t_r + half_sp[d]
            gated_x = half_xc * t_i + half_xc
            a = jnp.exp(-neg_log_a)
            one_minus_a2 = jnp.tanh(neg_log_a) * (1.0 + a * a)
            root = jnp.where(one_minus_a2 > 0.0, one_minus_a2 * lax.rsqrt(one_minus_a2), 0.0)
            a_ref[d, pl.ds(r0, RG_TILE), :] = a
            u_ref[d, pl.ds(r0, RG_TILE), :] = root * gated_x
        return carry

    def block_scan(t, carry):
        r0 = pl.multiple_of(t * RG_TILE, RG_TILE)
        for d in range(2):
            a_prev = u_prev = None
            for r in (range(SUBLANES) if d == 0 else reversed(range(SUBLANES))):
                rows = pl.ds(r0 + r, RG_TILE // SUBLANES, stride=SUBLANES)
                a, u = a_ref[d, rows, :], u_ref[d, rows, :]
                if a_prev is not None:
                    u = a * u_prev + u
                    a = a * a_prev
                as_ref[d, rows, :] = a
                us_ref[d, rows, :] = u
                a_prev, u_prev = a, u
        return carry

    lax.fori_loop(0, nt, gates, 0)
    lax.fori_loop(0, nt, block_scan, 0)

    nblk = s // SUBLANES

    def scan(j, carry):
        h_f, h_b = carry
        for d, last, h in ((0, SUBLANES - 1, h_f), (1, 0, h_b)):
            blk = j if d == 0 else nblk - 1 - j
            rows = pl.ds(pl.multiple_of(blk * SUBLANES, SUBLANES), SUBLANES)
            a, u = as_ref[d, rows, :], us_ref[d, rows, :]
            h_ref[d, rows, :] = a * h + u
            a_end = jnp.broadcast_to(a[last:last + 1, :], (SUBLANES, LANES))
            u_end = jnp.broadcast_to(u[last:last + 1, :], (SUBLANES, LANES))
            if d == 0:
                h_f = a_end * h + u_end
            else:
                h_b = a_end * h + u_end
        return h_f, h_b

    h0 = jnp.zeros((SUBLANES, LANES), F32)
    lax.fori_loop(0, nblk, scan, (h0, h0), unroll=RG_SCAN_UNROLL)

    def finish(t, carry):
        r0 = pl.multiple_of(t * RG_TILE, RG_TILE)
        hr = h_ref[0, pl.ds(r0, RG_TILE), :] + h_ref[1, pl.ds(r0, RG_TILE), :]
        y_ref[0, pl.ds(r0, RG_TILE), :] = (
            hr * _gelu_tanh(gr_ref[0, pl.ds(r0, RG_TILE), :])).astype(y_ref.dtype)
        return carry

    lax.fori_loop(0, nt, finish, 0)


def rglru_mixer(z, xr_block0, gr_block0, conv_w, conv_b, wa, wx, ba, bx, lam):
    b, s, _ = z.shape
    width = RNN_BLOCKS * RNN_BLOCK
    assert s % RG_TILE == 0
    return pl.pallas_call(
        functools.partial(_rglru_body, s=s),
        grid=(b, RNN_BLOCKS),
        in_specs=[
            pl.BlockSpec((1, s, RNN_BLOCK), lambda i, j: (i, 0, xr_block0 + j)),
            pl.BlockSpec((1, s, RNN_BLOCK), lambda i, j: (i, 0, gr_block0 + j)),
            pl.BlockSpec((CONV_WIDTH, RNN_BLOCK), lambda i, j: (0, j)),
            pl.BlockSpec((1, RNN_BLOCK), lambda i, j: (0, j)),
            pl.BlockSpec((2, 1, RNN_BLOCK, RNN_BLOCK), lambda i, j: (0, j, 0, 0)),
            pl.BlockSpec((2, 1, RNN_BLOCK, RNN_BLOCK), lambda i, j: (0, j, 0, 0)),
            pl.BlockSpec((2, RNN_BLOCK), lambda i, j: (0, j)),
            pl.BlockSpec((2, RNN_BLOCK), lambda i, j: (0, j)),
            pl.BlockSpec((2, RNN_BLOCK), lambda i, j: (0, j)),
        ],
        out_specs=pl.BlockSpec((1, s, RNN_BLOCK), lambda i, j: (i, 0, j)),
        out_shape=jax.ShapeDtypeStruct((b, s, width), BF16),
        scratch_shapes=[
            pltpu.VMEM((s + 2 * RG_PAD, RNN_BLOCK), F32),
            pltpu.VMEM((2, s, RNN_BLOCK), F32),
            pltpu.VMEM((2, s, RNN_BLOCK), F32),
            pltpu.VMEM((2, s, RNN_BLOCK), F32),
            pltpu.VMEM((2, s, RNN_BLOCK), F32),
            pltpu.VMEM((2, s, RNN_BLOCK), F32),
        ],
        compiler_params=_cparams("parallel", "parallel"),
        name="rglru",
    )(z, z, conv_w, conv_b, wa, wx, ba, bx, lam)


ATTN_TQ = 128
ATTN_TK = ATTN_TQ + 2 * ATTN_HALF
ATTN_UNROLL = 32
LOG2E = math.log2(math.e)


def _attn_body(q_ref, k_ref, v_ref, slope_ref, o_ref, bias_ref, x4_ref, og_ref, dg_ref, mg_ref,
               *, s):
    qscale = ATTN_DH ** -0.5 * LOG2E

    @pl.when(pl.program_id(1) == 0)
    def _():
        slope = slope_ref[0, 0:1, 0:1] * LOG2E
        qi = lax.broadcasted_iota(jnp.int32, (ATTN_TQ, ATTN_TK), 0)
        kj = lax.broadcasted_iota(jnp.int32, (ATTN_TQ, ATTN_TK), 1)
        for g, dil in enumerate(ATTN_DILATIONS):
            for e in range(3):
                rel = jnp.abs(kj - e * ATTN_HALF - qi)
                pen = slope * (rel * dil).astype(F32)
                bias_ref[g, e] = jnp.where(rel <= ATTN_HALF, -pen, NEG_BIG)

    s4 = s // 4
    srcs = (q_ref, k_ref, v_ref)

    def split4(t, carry):
        c = t // (s4 // ATTN_TK)
        p0 = (t % (s4 // ATTN_TK)) * ATTN_TK
        dst = pl.ds(pl.multiple_of(c * s4 + p0, ATTN_TK), ATTN_TK)
        for a in range(3):
            x = srcs[a][0, pl.ds(c + 4 * p0, ATTN_TK, stride=4), :]
            x4_ref[a, dst, :] = x * qscale if a == 0 else x
        return carry

    lax.fori_loop(0, 4 * (s4 // ATTN_TK), split4, 0)

    ones = jnp.ones((ATTN_TK, ATTN_DH), BF16)

    for g, dil in reversed(list(enumerate(ATTN_DILATIONS))):
        sp = s // dil
        nqb = sp // ATTN_TQ

        def q_block(t, carry, g=g, dil=dil, sp=sp, nqb=nqb):
            r = t // nqb
            p0 = (t % nqb) * ATTN_TQ
            kstart = jnp.clip(p0 - ATTN_HALF, 0, sp - ATTN_TK)
            e = (p0 - kstart) // ATTN_HALF
            if dil == 1:
                qrows = pl.ds(pl.multiple_of(p0, ATTN_TQ), ATTN_TQ)
                krows = pl.ds(pl.multiple_of(kstart, ATTN_HALF), ATTN_TK)
                qf = q_ref[0, qrows, :] * qscale
                kf, vf = k_ref[0, krows, :], v_ref[0, krows, :]
                orows = qrows
            elif dil == 4:
                qrows = pl.ds(pl.multiple_of(r * s4 + p0, ATTN_TQ), ATTN_TQ)
                krows = pl.ds(pl.multiple_of(r * s4 + kstart, ATTN_HALF), ATTN_TK)
                qf, kf, vf = x4_ref[0, qrows, :], x4_ref[1, krows, :], x4_ref[2, krows, :]
                orows = pl.ds(r + 4 * p0, ATTN_TQ, stride=4)
            else:
                base = (r % 4) * s4 + r // 4
                qrows = pl.ds(base + 4 * p0, ATTN_TQ, stride=4)
                krows = pl.ds(base + 4 * kstart, ATTN_TK, stride=4)
                qf, kf, vf = x4_ref[0, qrows, :], x4_ref[1, krows, :], x4_ref[2, krows, :]
                orows = pl.ds(r + dil * p0, ATTN_TQ, stride=dil)
            qb, kb, vb = qf.astype(BF16), kf.astype(BF16), vf.astype(BF16)
            sc = lax.dot_general(qb, kb, (((1,), (1,)), ((), ())),
                                 preferred_element_type=F32) + bias_ref[g, e]
            m = jnp.max(sc, axis=-1, keepdims=True)
            p = jnp.exp2(sc - m).astype(BF16)
            pv = jnp.dot(p, jnp.concatenate([vb, ones], axis=1), preferred_element_type=F32)
            acc, den = pv[:, :ATTN_DH], pv[:, ATTN_DH:]
            if dil != 1:
                og_ref[g - 1, orows, :] = acc
                dg_ref[g - 1, orows, :] = den
                mg_ref[g - 1, orows, :] = jnp.broadcast_to(m, den.shape)
                return carry
            m4, m16 = mg_ref[0, orows, :], mg_ref[1, orows, :]
            mx = jnp.maximum(jnp.maximum(m4, m16), m)
            e1, e4, e16 = jnp.exp2(m - mx), jnp.exp2(m4 - mx), jnp.exp2(m16 - mx)
            num = e1 * acc + e4 * og_ref[0, orows, :] + e16 * og_ref[1, orows, :]
            tot = e1 * den + e4 * dg_ref[0, orows, :] + e16 * dg_ref[1, orows, :]
            o_ref[0, orows, :] = (num * (1.0 / tot)).astype(o_ref.dtype)
            return carry

        lax.fori_loop(0, dil * nqb, q_block, 0, unroll=ATTN_UNROLL)


def dilated_attention(qkv, slopes):
    b, s, _ = qkv.shape
    h, dh = ATTN_HEADS, ATTN_DH
    assert s % (max(ATTN_DILATIONS) * ATTN_TK) == 0
    return pl.pallas_call(
        functools.partial(_attn_body, s=s),
        grid=(h, b),
        in_specs=[
            pl.BlockSpec((1, s, dh), lambda j, i: (i, 0, j)),
            pl.BlockSpec((1, s, dh), lambda j, i: (i, 0, h + j)),
            pl.BlockSpec((1, s, dh), lambda j, i: (i, 0, 2 * h + j)),
            pl.BlockSpec((1, SUBLANES, LANES), lambda j, i: (j, 0, 0)),
        ],
        out_specs=pl.BlockSpec((1, s, dh), lambda j, i: (i, 0, j)),
        out_shape=jax.ShapeDtypeStruct((b, s, h * dh), BF16),
        scratch_shapes=[
            pltpu.VMEM((len(ATTN_DILATIONS), 3, ATTN_TQ, ATTN_TK), F32),
            pltpu.VMEM((3, s, dh), F32),
            pltpu.VMEM((len(ATTN_DILATIONS) - 1, s, dh), F32),
            pltpu.VMEM((len(ATTN_DILATIONS) - 1, s, dh), F32),
            pltpu.VMEM((len(ATTN_DILATIONS) - 1, s, dh), F32),
        ],
        compiler_params=_cparams("parallel", "arbitrary"),
        name="dilated_attn",
    )(qkv, qkv, qkv, slopes)


MIXER_OUT_TM = 512


FFN_DOWN_TM = 256


def _ffn(x, h, w1, w3, w2, layer, next_norm_g, last):
    u = swiglu_up(h, w1, w3, layer)
    return matmul([(u, w2, 0, layer)], res=x, norm_g=next_norm_g,
                  norm_dtype=F32 if last else BF16, keep_sum=not last,
                  tm=FFN_DOWN_TM, tn=w2.shape[-1], name="ffn_down")


def _even_layer(x, bsz, seq, norm_g, w_in, gate_b, conv_w, conv_b, rg_wa, rg_ba, rg_wx, rg_bx,
                rg_lam, head_g, w_out, next_norm_g):
    heads, dk, dv = MLSTM_HEADS, MLSTM_DK, MLSTM_DV
    n_qkvo = 2 * heads * dk + 2 * heads * dv
    n_gate = 4 * heads
    width = RNN_BLOCKS * RNN_BLOCK
    w_in_t = jnp.transpose(w_in)
    w_main_t = jnp.concatenate([w_in_t[:n_qkvo], w_in_t[n_qkvo + n_gate:]], axis=0).astype(BF16)
    w_gate_t = w_in_t[n_qkvo:n_qkvo + n_gate].astype(BF16)
    z, zg_t = norm_matmul(x, norm_g, w_main_t, w_side_t=w_gate_t, w_transposed=True,
                          name="in_proj")
    z = z.reshape(bsz, seq, -1)

    chunk = MLSTM_CHUNK
    nc = seq // chunk
    gates_rows = jnp.transpose(zg_t[:n_gate].reshape(4, heads, bsz, nc, chunk), (2, 1, 0, 3, 4))
    gate_b_rows = jnp.broadcast_to(
        jnp.transpose(gate_b.astype(F32).reshape(4, heads))[:, :, None, None], (heads, 4, 1, chunk))
    y_a = mlstm_mixer(z, gates_rows, gate_b_rows, head_g.astype(F32).reshape(heads, 1, dv), chunk)

    y_b = rglru_mixer(z, n_qkvo // RNN_BLOCK, (n_qkvo + width) // RNN_BLOCK,
                      conv_w.astype(F32), conv_b.astype(F32).reshape(1, width),
                      (0.5 * rg_wa).astype(BF16), (0.5 * rg_wx).astype(BF16),
                      0.5 * rg_ba.astype(F32), 0.5 * rg_bx.astype(F32), rg_lam.astype(F32))

    w_out_b = w_out.astype(BF16)
    m = bsz * seq
    assert heads * dv == width
    return matmul([(y_a.reshape(m, -1), w_out_b, 0),
                   (y_b.reshape(m, -1), w_out_b, 1)], res=x, norm_g=next_norm_g,
                  tm=MIXER_OUT_TM, tn=w_out.shape[1], name="out_proj")


def _odd_layer(x, h, bsz, seq, norm_g, w_qkv, w_o, next_norm_g):
    if h is None:
        qkv = norm_matmul(x, norm_g, w_qkv.astype(BF16), name="qkv_proj")
    else:
        qkv = matmul([(h, w_qkv.astype(BF16))], tn=2048, name="qkv_proj")
    qkv = qkv.reshape(bsz, seq, -1)
    slopes = jnp.exp2(-ALIBI_MAX_BIAS * jnp.arange(1, ATTN_HEADS + 1, dtype=F32) / ATTN_HEADS)
    slopes = jnp.broadcast_to(slopes[:, None, None], (ATTN_HEADS, SUBLANES, LANES))
    o = dilated_attention(qkv, slopes)
    return matmul([(o.reshape(bsz * seq, -1), w_o.astype(BF16))], res=x, norm_g=next_norm_g,
                  tm=MIXER_OUT_TM, tn=w_o.shape[1], name="attn_out_proj")


def kernel(x, e_norm, e_w_in, e_gate_b, e_conv_w, e_conv_b, e_rg_wa, e_rg_ba, e_rg_wx, e_rg_bx,
           e_rg_lam, e_head_g, e_w_out, o_norm, o_w_qkv, o_w_o, f_norm, f_w1, f_w3, f_w2,
           final_norm):
    bsz, seq, d = x.shape
    depth = f_norm.shape[0]
    xs = x.reshape(bsz * seq, d).astype(F32)
    w1, w3, w2 = f_w1.astype(BF16), f_w3.astype(BF16), f_w2.astype(BF16)
    hs = None
    for l in range(depth):
        if l % 2 == 0:
            e = l // 2
            xs, hs = _even_layer(xs, bsz, seq, e_norm[e], e_w_in[e], e_gate_b[e], e_conv_w[e],
                                 e_conv_b[e], e_rg_wa[e], e_rg_ba[e], e_rg_wx[e], e_rg_bx[e],
                                 e_rg_lam[e], e_head_g[e], e_w_out[e], f_norm[l])
        else:
            o = l // 2
            xs, hs = _odd_layer(xs, hs, bsz, seq, o_norm[o], o_w_qkv[o], o_w_o[o], f_norm[l])
        if l == depth - 1:
            out = _ffn(xs, hs, w1, w3, w2, l, final_norm, last=True)
            return out.astype(x.dtype).reshape(bsz, seq, d)
        if (l + 1) % 2:
            xs, hs = _ffn(xs, hs, w1, w3, w2, l, o_norm[(l + 1) // 2], last=False)
        else:
            xs, hs = _ffn(xs, hs, w1, w3, w2, l, None, last=False), None
```

```python
import functools
import math

import jax
import jax.numpy as jnp
from jax import lax
from jax.experimental import pallas as pl
from jax.experimental.pallas import tpu as pltpu

F32 = jnp.float32
BF16 = jnp.bfloat16

RMS_EPS = 1e-6
NEG_BIG = -1e30

MLSTM_HEADS = 4
MLSTM_DK = 128
MLSTM_DV = 256
MLSTM_CHUNK = 256
RNN_BLOCKS = 8
RNN_BLOCK = 128
CONV_WIDTH = 4
CONV_LEFT = 2
RGLRU_C = 8.0
ATTN_HEADS = 16
ATTN_DH = 128
ATTN_HALF = 64
ATTN_DILATIONS = (1, 4, 16)
ALIBI_MAX_BIAS = 8.0

LANES = 128
SUBLANES = 8
VMEM_LIMIT_BYTES = 56 * 1024 * 1024


def _cparams(*semantics):
    return pltpu.CompilerParams(dimension_semantics=semantics,
                                vmem_limit_bytes=VMEM_LIMIT_BYTES)


NORM_ROWS = 128


def _norm_rows_into(x_ref, g_ref, h_ref):
    g = g_ref[...]

    def rows(t, carry):
        sl = pl.ds(pl.multiple_of(t * NORM_ROWS, NORM_ROWS), NORM_ROWS)
        x = x_ref[sl, :]
        ms = jnp.mean(x * x, axis=-1, keepdims=True)
        h_ref[sl, :] = (x * lax.rsqrt(ms + RMS_EPS) * g).astype(h_ref.dtype)
        return carry

    lax.fori_loop(0, x_ref.shape[0] // NORM_ROWS, rows, 0)


def _weight_spec(w, kdim, tn, row_block=0, layer=None, **mode):
    if w.ndim == 2:
        return pl.BlockSpec((kdim, tn), lambda i, j: (row_block, j), **mode)
    return pl.BlockSpec((None, kdim, tn), lambda i, j: (layer, row_block, j), **mode)


def _matmul_body(*refs, n_pairs, has_res, has_norm, keep_sum):
    pairs = [(refs[2 * p], refs[2 * p + 1]) for p in range(n_pairs)]
    rest = list(refs[2 * n_pairs:])
    r_ref = rest.pop(0) if has_res else None
    g_ref = rest.pop(0) if has_norm else None
    y_ref = rest.pop(0) if keep_sum else rest.pop(-1)

    acc = None
    for a_ref, w_ref in pairs:
        d = jnp.dot(a_ref[...], w_ref[...], preferred_element_type=F32)
        acc = d if acc is None else acc + d
    if has_res:
        acc = acc + r_ref[...]
    y_ref[...] = acc.astype(y_ref.dtype)
    if has_norm:
        _norm_rows_into(y_ref, g_ref, rest.pop(0))


def matmul(pairs, res=None, norm_g=None, norm_dtype=BF16, keep_sum=True, tm=1024, tn=1024,
           name="matmul"):
    m = pairs[0][0].shape[0]
    n = pairs[0][1].shape[-1]
    tn = min(tn, n)
    assert m % tm == 0 and n % tn == 0 and (keep_sum or norm_g is not None)
    assert norm_g is None or (tn == n and tm % NORM_ROWS == 0)
    w_mode = dict(pipeline_mode=pl.Buffered(1)) if tn == n else {}
    in_specs, args = [], []
    for a, w, *where in pairs:
        kdim = a.shape[1]
        kb, layer = (list(where) + [0, None])[:2] if where else (0, None)
        in_specs.append(pl.BlockSpec((tm, kdim), lambda i, j: (i, 0)))
        in_specs.append(_weight_spec(w, kdim, tn, kb, layer, **w_mode))
        args += [a, w]
    if res is not None:
        in_specs.append(pl.BlockSpec((tm, tn), lambda i, j: (i, j)))
        args.append(res)
    out_specs, out_shape, scratch = [], [], []
    if keep_sum:
        out_specs.append(pl.BlockSpec((tm, tn), lambda i, j: (i, j)))
        out_shape.append(jax.ShapeDtypeStruct((m, n), F32))
    else:
        scratch.append(pltpu.VMEM((tm, n), F32))
    if norm_g is not None:
        in_specs.append(pl.BlockSpec((1, n), lambda i, j: (0, 0)))
        args.append(norm_g.reshape(1, n).astype(F32))
        out_specs.append(pl.BlockSpec((tm, n), lambda i, j: (i, 0)))
        out_shape.append(jax.ShapeDtypeStruct((m, n), norm_dtype))
    outs = pl.pallas_call(
        functools.partial(_matmul_body, n_pairs=len(pairs), has_res=res is not None,
                          has_norm=norm_g is not None, keep_sum=keep_sum),
        grid=(m // tm, n // tn),
        in_specs=in_specs,
        out_specs=out_specs,
        out_shape=out_shape,
        scratch_shapes=scratch,
        compiler_params=_cparams("parallel", "parallel"),
        name=name,
    )(*args)
    return outs if len(outs) > 1 else outs[0]


SWIGLU_ROWS = 1024


def _swiglu_up_body(a_ref, w1_ref, w3_ref, o_ref):
    rows = min(SWIGLU_ROWS, a_ref.shape[0])
    for r0 in range(0, a_ref.shape[0], rows):
        a = a_ref[pl.ds(r0, rows), :]
        u = jnp.dot(a, w1_ref[...], preferred_element_type=F32)
        v = jnp.dot(a, w3_ref[...], preferred_element_type=F32)
        o_ref[pl.ds(r0, rows), :] = (u * jax.nn.sigmoid(u) * v).astype(o_ref.dtype)


def swiglu_up(a, w1, w3, layer, tm=4096, tn=512):
    m, kdim = a.shape
    n = w1.shape[-1]
    assert m % tm == 0 and n % tn == 0
    return pl.pallas_call(
        _swiglu_up_body,
        grid=(m // tm, n // tn),
        in_specs=[pl.BlockSpec((tm, kdim), lambda i, j: (i, 0)),
                  _weight_spec(w1, kdim, tn, layer=layer),
                  _weight_spec(w3, kdim, tn, layer=layer)],
        out_specs=pl.BlockSpec((tm, tn), lambda i, j: (i, j)),
        out_shape=jax.ShapeDtypeStruct((m, n), BF16),
        compiler_params=_cparams("parallel", "parallel"),
        name="swiglu_up",
    )(a, w1, w3)


NORM_CHUNK = 256


def _norm_matmul_body(*refs, has_side, w_transposed):
    if has_side:
        x_ref, g_ref, w_ref, ws_ref, o_ref, os_ref = refs
    else:
        x_ref, g_ref, w_ref, o_ref = refs
    w_dims = (((1,), (1,)), ((), ())) if w_transposed else (((1,), (0,)), ((), ()))

    g = g_ref[...]
    for r0 in range(0, x_ref.shape[0], NORM_CHUNK):
        sl = pl.ds(r0, NORM_CHUNK)
        x = x_ref[sl, :]
        ms = jnp.mean(x * x, axis=-1, keepdims=True)
        h = (x * lax.rsqrt(ms + RMS_EPS) * g).astype(BF16)
        o_ref[sl, :] = lax.dot_general(h, w_ref[...], w_dims,
                                       preferred_element_type=F32).astype(o_ref.dtype)
        if has_side:
            os_ref[:, sl] = lax.dot_general(ws_ref[...], h, (((1,), (1,)), ((), ())),
                                            preferred_element_type=F32)


def norm_matmul(x, g, w, w_side_t=None, w_transposed=False, out_dtype=F32, tm=512,
                name="norm_matmul"):
    m, d = x.shape
    n = w.shape[0] if w_transposed else w.shape[1]
    assert m % tm == 0 and tm % NORM_CHUNK == 0
    resident = dict(pipeline_mode=pl.Buffered(1))
    in_specs = [pl.BlockSpec((tm, d), lambda i: (i, 0)),
                pl.BlockSpec((1, d), lambda i: (0, 0)),
                pl.BlockSpec(w.shape, lambda i: (0, 0), **resident)]
    args = [x, g.reshape(1, d).astype(F32), w]
    out_specs = [pl.BlockSpec((tm, n), lambda i: (i, 0))]
    out_shape = [jax.ShapeDtypeStruct((m, n), out_dtype)]
    if w_side_t is not None:
        ns = w_side_t.shape[0]
        in_specs.append(pl.BlockSpec((ns, d), lambda i: (0, 0), **resident))
        args.append(w_side_t)
        out_specs.append(pl.BlockSpec((ns, tm), lambda i: (0, i)))
        out_shape.append(jax.ShapeDtypeStruct((ns, m), F32))
    outs = pl.pallas_call(
        functools.partial(_norm_matmul_body, has_side=w_side_t is not None,
                          w_transposed=w_transposed),
        grid=(m // tm,),
        in_specs=in_specs,
        out_specs=out_specs,
        out_shape=out_shape,
        compiler_params=_cparams("parallel"),
        name=name,
    )(*args)
    return outs if w_side_t is not None else outs[0]


def _log_sigmoid(x):
    return jnp.minimum(x, 0.0) - jnp.log1p(jnp.exp(-jnp.abs(x)))


SCAN_RADIX = 4


def _lane_scan(x, combine, identity, reverse):
    width = x.shape[-1]
    lane = lax.broadcasted_iota(jnp.int32, x.shape, 1)

    def shifted(v, sh):
        if reverse:
            return jnp.where(lane < width - sh, pltpu.roll(v, width - sh, axis=1), identity)
        return jnp.where(lane >= sh, pltpu.roll(v, sh, axis=1), identity)

    sh = 1
    while sh < width:
        parts = [shifted(x, k * sh) for k in range(1, SCAN_RADIX) if k * sh < width]
        for p in parts:
            x = combine(x, p)
        sh *= SCAN_RADIX
    return x


def _mlstm_body(q_ref, k_ref, v_ref, og_ref, g_ref, gb_ref, hg_ref, y_ref,
                rows_ref, cols_ref, hf_ref, hb_ref, ct_ref, *, nc, chunk):
    dk, dv = MLSTM_DK, MLSTM_DV
    scale = dk ** -0.5

    g = g_ref[0, 0] + gb_ref[0]
    rows = []
    for d in range(2):
        b = _lane_scan(_log_sigmoid(g[2 * d + 1]), jnp.add, 0.0, reverse=d == 1)
        gg = g[2 * d] - b
        gmax = _lane_scan(gg, jnp.maximum, -jnp.inf, reverse=d == 1)
        rows += [gg, b, gmax]
        for j, r in enumerate((gg, b, gmax)):
            rows_ref[3 * d + j] = r
    rowmat = jnp.concatenate(rows + [jnp.zeros((LANES - 6 * nc, chunk), F32)], axis=0)
    colmat = rowmat.T
    for c in range(nc):
        cols_ref[c] = colmat if c == 0 else pltpu.roll(colmat, LANES - c, axis=1)

    ct_ref[...] = jnp.zeros_like(ct_ref)

    row_id = lax.broadcasted_iota(jnp.int32, (chunk, chunk), 0)
    col_id = lax.broadcasted_iota(jnp.int32, (chunk, chunk), 1)
    causal = (row_id >= col_id, row_id <= col_id)
    ones = jnp.ones((chunk, LANES), BF16)

    def lanes2(x):
        return jnp.concatenate([x] * (dv // LANES), axis=1)

    hg = hg_ref[0]

    def emit(r0, h, own_ref, other_ref):
        rows = pl.ds(r0, chunk)
        if other_ref is None:
            own_ref[rows, :] = h
            return
        hm = h + other_ref[rows, :]
        ms = jnp.mean(hm * hm, axis=-1, keepdims=True)
        hn = hm * lax.rsqrt(ms + RMS_EPS) * hg
        y_ref[0, rows, :] = (hn * jax.nn.sigmoid(og_ref[0, rows, :])).astype(y_ref.dtype)

    def chunk_step(c, m, d, own_ref, other_ref):
        r0 = pl.multiple_of(c * chunk, chunk)
        qb = (q_ref[0, pl.ds(r0, chunk), :] * scale).astype(BF16)
        kb = k_ref[0, pl.ds(r0, chunk), :].astype(BF16)
        vf = v_ref[0, pl.ds(r0, chunk), :]
        cols = cols_ref[c]

        def column(j):
            lane = (3 * d + j) * nc
            return jnp.broadcast_to(cols[:, lane:lane + 1], (chunk, LANES))

        gg_col, b_col, gmax_col = column(0), column(1), column(2)
        gg_row = rows_ref[3 * d, pl.ds(c, 1), :]
        end = chunk - 1 if d == 0 else 0
        total = rows_ref[3 * d + 1, pl.ds(c, 1), :][:, end:end + 1]
        gmax_end = rows_ref[3 * d + 2, pl.ds(c, 1), :][:, end:end + 1]

        m_row = jnp.maximum(gmax_col, m)
        w_intra = jnp.exp(jnp.where(causal[d], gg_row - lanes2(m_row), NEG_BIG))
        w_inter = jnp.exp(m - m_row)
        s = lax.dot_general(qb, kb, (((1,), (1,)), ((), ())),
                            preferred_element_type=F32) * w_intra
        ct = ct_ref[d]
        intra = jnp.dot(s.astype(BF16), jnp.concatenate([vf.astype(BF16), ones], axis=1),
                        preferred_element_type=F32)
        inter = jnp.dot(qb, ct.astype(BF16), preferred_element_type=F32)
        num = intra[:, :dv] + lanes2(w_inter) * inter[:, :dv]
        den = intra[:, dv:] + w_inter * inter[:, dv:]
        inv = 1.0 / jnp.maximum(jnp.abs(den), jnp.exp(-(b_col + m_row)))
        emit(r0, num * lanes2(inv), own_ref, other_ref)

        m_new = jnp.maximum(total + m, total + gmax_end)
        w_src = jnp.exp(total + gg_col - m_new)
        decay = jnp.exp(total + m - m_new)
        wv = jnp.concatenate([lanes2(w_src) * vf, w_src], axis=1).astype(BF16)
        ct_ref[d] = decay * ct + lax.dot_general(
            kb, wv, (((0,), (0,)), ((), ())), preferred_element_type=F32)
        return m_new

    def first_half(c, carry):
        m_f, m_b = carry
        m_f = chunk_step(c, m_f, 0, hf_ref, None)
        m_b = chunk_step(nc - 1 - c, m_b, 1, hb_ref, None)
        return m_f, m_b

    def second_half(c, carry):
        m_f, m_b = carry
        m_f = chunk_step(c, m_f, 0, None, hb_ref)
        m_b = chunk_step(nc - 1 - c, m_b, 1, None, hf_ref)
        return m_f, m_b

    m0 = jnp.full((1, 1), NEG_BIG, F32)
    carry = lax.fori_loop(0, nc // 2, first_half, (m0, m0), unroll=2)
    lax.fori_loop(nc // 2, nc, second_half, carry, unroll=2)


def mlstm_mixer(z, gates_rows, gate_b_rows, head_g, chunk=MLSTM_CHUNK):
    b, s, _ = z.shape
    h, dk, dv = MLSTM_HEADS, MLSTM_DK, MLSTM_DV
    nc = s // chunk
    assert s % chunk == 0 and 6 * nc <= LANES and chunk % LANES == 0 and nc % 2 == 0
    kq = h * dk // dk
    kv = 2 * h * dk // dv
    ko = kv + h
    return pl.pallas_call(
        functools.partial(_mlstm_body, nc=nc, chunk=chunk),
        grid=(b, h),
        in_specs=[
            pl.BlockSpec((1, s, dk), lambda i, j: (i, 0, j)),
            pl.BlockSpec((1, s, dk), lambda i, j: (i, 0, kq + j)),
            pl.BlockSpec((1, s, dv), lambda i, j: (i, 0, kv + j)),
            pl.BlockSpec((1, s, dv), lambda i, j: (i, 0, ko + j)),
            pl.BlockSpec((1, 1, 4, nc, chunk), lambda i, j: (i, j, 0, 0, 0)),
            pl.BlockSpec((1, 4, 1, chunk), lambda i, j: (j, 0, 0, 0)),
            pl.BlockSpec((1, 1, dv), lambda i, j: (j, 0, 0)),
        ],
        out_specs=pl.BlockSpec((1, s, dv), lambda i, j: (i, 0, j)),
        out_shape=jax.ShapeDtypeStruct((b, s, h * dv), BF16),
        scratch_shapes=[
            pltpu.VMEM((6, nc, chunk), F32),
            pltpu.VMEM((nc, chunk, LANES), F32),
            pltpu.VMEM((s, dv), F32),
            pltpu.VMEM((s, dv), F32),
            pltpu.VMEM((2, dk, dv + LANES), F32),
        ],
        compiler_params=_cparams("parallel", "parallel"),
        name="mlstm",
    )(z, z, z, z, gates_rows, gate_b_rows, head_g)


RG_TILE = 512
RG_PAD = SUBLANES
RG_SCAN_UNROLL = 8


def _softplus(x):
    return jnp.maximum(x, 0.0) + jnp.log1p(jnp.exp(-jnp.abs(x)))


def _gelu_tanh(x):
    c = math.sqrt(2.0 / math.pi)
    return x * (0.5 * (1.0 + jnp.tanh(c * (x + 0.044715 * (x * x * x)))))


def _rglru_body(x_ref, gr_ref, cw_ref, cb_ref, wa_ref, wx_ref, ba_ref, bx_ref, lam_ref, y_ref,
                xpad_ref, a_ref, u_ref, as_ref, us_ref, h_ref, *, s):
    nt = s // RG_TILE
    zeros = jnp.zeros((RG_PAD, LANES), F32)
    xpad_ref[pl.ds(0, RG_PAD), :] = zeros
    xpad_ref[pl.ds(RG_PAD + s, RG_PAD), :] = zeros

    def copy_in(t, carry):
        r0 = pl.multiple_of(t * RG_TILE, RG_TILE)
        xpad_ref[pl.ds(RG_PAD + r0, RG_TILE), :] = x_ref[0, pl.ds(r0, RG_TILE), :]
        return carry

    lax.fori_loop(0, nt, copy_in, 0)

    cw = cw_ref[...]
    cb = cb_ref[...]
    half_sp = [0.5 * RGLRU_C * _softplus(-lam_ref[d:d + 1, :]) for d in range(2)]

    def gates(t, carry):
        r0 = pl.multiple_of(t * RG_TILE, RG_TILE)
        xc = None
        for j in range(CONV_WIDTH):
            tap = xpad_ref[pl.ds(r0 + RG_PAD - CONV_LEFT + j, RG_TILE), :] * cw[j:j + 1, :]
            xc = tap if xc is None else xc + tap
        xc = xc + cb
        xcb = xc.astype(BF16)
        half_xc = 0.5 * xc
        for d in range(2):
            t_r = jnp.tanh(jnp.dot(xcb, wa_ref[d, 0], preferred_element_type=F32)
                           + ba_ref[d:d + 1, :])
            t_i = jnp.tanh(jnp.dot(xcb, wx_ref[d, 0], preferred_element_type=F32)
                           + bx_ref[d:d + 1, :])
            neg_log_a = half_sp[d] * t_r + half_sp[d]
            gated_x = half_xc * t_i + half_xc
            a = jnp.exp(-neg_log_a)
            one_minus_a2 = jnp.tanh(neg_log_a) * (1.0 + a * a)
            root = jnp.where(one_minus_a2 > 0.0, one_minus_a2 * lax.rsqrt(one_minus_a2), 0.0)
            a_ref[d, pl.ds(r0, RG_TILE), :] = a
            u_ref[d, pl.ds(r0, RG_TILE), :] = root * gated_x
        return carry

    def block_scan(t, carry):
        r0 = pl.multiple_of(t * RG_TILE, RG_TILE)
        for d in range(2):
            a_prev = u_prev = None
            for r in (range(SUBLANES) if d == 0 else reversed(range(SUBLANES))):
                rows = pl.ds(r0 + r, RG_TILE // SUBLANES, stride=SUBLANES)
                a, u = a_ref[d, rows, :], u_ref[d, rows, :]
                if a_prev is not None:
                    u = a * u_prev + u
                    a = a * a_prev
                as_ref[d, rows, :] = a
                us_ref[d, rows, :] = u
                a_prev, u_prev = a, u
        return carry

    lax.fori_loop(0, nt, gates, 0)
    lax.fori_loop(0, nt, block_scan, 0)

    nblk = s // SUBLANES

    def scan(j, carry):
        h_f, h_b = carry
        for d, last, h in ((0, SUBLANES - 1, h_f), (1, 0, h_b)):
            blk = j if d == 0 else nblk - 1 - j
            rows = pl.ds(pl.multiple_of(blk * SUBLANES, SUBLANES), SUBLANES)
            a, u = as_ref[d, rows, :], us_ref[d, rows, :]
            h_ref[d, rows, :] = a * h + u
            a_end = jnp.broadcast_to(a[last:last + 1, :], (SUBLANES, LANES))
            u_end = jnp.broadcast_to(u[last:last + 1, :], (SUBLANES, LANES))
            if d == 0:
                h_f = a_end * h + u_end
            else:
                h_b = a_end * h + u_end
        return h_f, h_b

    h0 = jnp.zeros((SUBLANES, LANES), F32)
    lax.fori_loop(0, nblk, scan, (h0, h0), unroll=RG_SCAN_UNROLL)

    def finish(t, carry):
        r0 = pl.multiple_of(t * RG_TILE, RG_TILE)
        hr = h_ref[0, pl.ds(r0, RG_TILE), :] + h_ref[1, pl.ds(r0, RG_TILE), :]
        y_ref[0, pl.ds(r0, RG_TILE), :] = (
            hr * _gelu_tanh(gr_ref[0, pl.ds(r0, RG_TILE), :])).astype(y_ref.dtype)
        return carry

    lax.fori_loop(0, nt, finish, 0)


def rglru_mixer(z, xr_block0, gr_block0, conv_w, conv_b, wa, wx, ba, bx, lam):
    b, s, _ = z.shape
    width = RNN_BLOCKS * RNN_BLOCK
    assert s % RG_TILE == 0
    return pl.pallas_call(
        functools.partial(_rglru_body, s=s),
        grid=(b, RNN_BLOCKS),
        in_specs=[
            pl.BlockSpec((1, s, RNN_BLOCK), lambda i, j: (i, 0, xr_block0 + j)),
            pl.BlockSpec((1, s, RNN_BLOCK), lambda i, j: (i, 0, gr_block0 + j)),
            pl.BlockSpec((CONV_WIDTH, RNN_BLOCK), lambda i, j: (0, j)),
            pl.BlockSpec((1, RNN_BLOCK), lambda i, j: (0, j)),
            pl.BlockSpec((2, 1, RNN_BLOCK, RNN_BLOCK), lambda i, j: (0, j, 0, 0)),
            pl.BlockSpec((2, 1, RNN_BLOCK, RNN_BLOCK), lambda i, j: (0, j, 0, 0)),
            pl.BlockSpec((2, RNN_BLOCK), lambda i, j: (0, j)),
            pl.BlockSpec((2, RNN_BLOCK), lambda i, j: (0, j)),
            pl.BlockSpec((2, RNN_BLOCK), lambda i, j: (0, j)),
        ],
        out_specs=pl.BlockSpec((1, s, RNN_BLOCK), lambda i, j: (i, 0, j)),
        out_shape=jax.ShapeDtypeStruct((b, s, width), BF16),
        scratch_shapes=[
            pltpu.VMEM((s + 2 * RG_PAD, RNN_BLOCK), F32),
            pltpu.VMEM((2, s, RNN_BLOCK), F32),
            pltpu.VMEM((2, s, RNN_BLOCK), F32),
            pltpu.VMEM((2, s, RNN_BLOCK), F32),
            pltpu.VMEM((2, s, RNN_BLOCK), F32),
            pltpu.VMEM((2, s, RNN_BLOCK), F32),
        ],
        compiler_params=_cparams("parallel", "parallel"),
        name="rglru",
    )(z, z, conv_w, conv_b, wa, wx, ba, bx, lam)


ATTN_TQ = 128
ATTN_TK = ATTN_TQ + 2 * ATTN_HALF
ATTN_UNROLL = 32
LOG2E = math.log2(math.e)


def _attn_body(q_ref, k_ref, v_ref, slope_ref, o_ref, bias_ref, x4_ref, og_ref, dg_ref, mg_ref,
               *, s):
    qscale = ATTN_DH ** -0.5 * LOG2E

    @pl.when(pl.program_id(1) == 0)
    def _():
        slope = slope_ref[0, 0:1, 0:1] * LOG2E
        qi = lax.broadcasted_iota(jnp.int32, (ATTN_TQ, ATTN_TK), 0)
        kj = lax.broadcasted_iota(jnp.int32, (ATTN_TQ, ATTN_TK), 1)
        for g, dil in enumerate(ATTN_DILATIONS):
            for e in range(3):
                rel = jnp.abs(kj - e * ATTN_HALF - qi)
                pen = slope * (rel * dil).astype(F32)
                bias_ref[g, e] = jnp.where(rel <= ATTN_HALF, -pen, NEG_BIG)

    s4 = s // 4
    srcs = (q_ref, k_ref, v_ref)

    def split4(t, carry):
        c = t // (s4 // ATTN_TK)
        p0 = (t % (s4 // ATTN_TK)) * ATTN_TK
        dst = pl.ds(pl.multiple_of(c * s4 + p0, ATTN_TK), ATTN_TK)
        for a in range(3):
            x = srcs[a][0, pl.ds(c + 4 * p0, ATTN_TK, stride=4), :]
            x4_ref[a, dst, :] = x * qscale if a == 0 else x
        return carry

    lax.fori_loop(0, 4 * (s4 // ATTN_TK), split4, 0)

    ones = jnp.ones((ATTN_TK, ATTN_DH), BF16)

    for g, dil in reversed(list(enumerate(ATTN_DILATIONS))):
        sp = s // dil
        nqb = sp // ATTN_TQ

        def q_block(t, carry, g=g, dil=dil, sp=sp, nqb=nqb):
            r = t // nqb
            p0 = (t % nqb) * ATTN_TQ
            kstart = jnp.clip(p0 - ATTN_HALF, 0, sp - ATTN_TK)
            e = (p0 - kstart) // ATTN_HALF
            if dil == 1:
                qrows = pl.ds(pl.multiple_of(p0, ATTN_TQ), ATTN_TQ)
                krows = pl.ds(pl.multiple_of(kstart, ATTN_HALF), ATTN_TK)
                qf = q_ref[0, qrows, :] * qscale
                kf, vf = k_ref[0, krows, :], v_ref[0, krows, :]
                orows = qrows
            elif dil == 4:
                qrows = pl.ds(pl.multiple_of(r * s4 + p0, ATTN_TQ), ATTN_TQ)
                krows = pl.ds(pl.multiple_of(r * s4 + kstart, ATTN_HALF), ATTN_TK)
                qf, kf, vf = x4_ref[0, qrows, :], x4_ref[1, krows, :], x4_ref[2, krows, :]
                orows = pl.ds(r + 4 * p0, ATTN_TQ, stride=4)
            else:
                base = (r % 4) * s4 + r // 4
                qrows = pl.ds(base + 4 * p0, ATTN_TQ, stride=4)
                krows = pl.ds(base + 4 * kstart, ATTN_TK, stride=4)
                qf, kf, vf = x4_ref[0, qrows, :], x4_ref[1, krows, :], x4_ref[2, krows, :]
                orows = pl.ds(r + dil * p0, ATTN_TQ, stride=dil)
            qb, kb, vb = qf.astype(BF16), kf.astype(BF16), vf.astype(BF16)
            sc = lax.dot_general(qb, kb, (((1,), (1,)), ((), ())),
                                 preferred_element_type=F32) + bias_ref[g, e]
            m = jnp.max(sc, axis=-1, keepdims=True)
            p = jnp.exp2(sc - m).astype(BF16)
            pv = jnp.dot(p, jnp.concatenate([vb, ones], axis=1), preferred_element_type=F32)
            acc, den = pv[:, :ATTN_DH], pv[:, ATTN_DH:]
            if dil != 1:
                og_ref[g - 1, orows, :] = acc
                dg_ref[g - 1, orows, :] = den
                mg_ref[g - 1, orows, :] = jnp.broadcast_to(m, den.shape)
                return carry
            m4, m16 = mg_ref[0, orows, :], mg_ref[1, orows, :]
            mx = jnp.maximum(jnp.maximum(m4, m16), m)
            e1, e4, e16 = jnp.exp2(m - mx), jnp.exp2(m4 - mx), jnp.exp2(m16 - mx)
            num = e1 * acc + e4 * og_ref[0, orows, :] + e16 * og_ref[1, orows, :]
            tot = e1 * den + e4 * dg_ref[0, orows, :] + e16 * dg_ref[1, orows, :]
            o_ref[0, orows, :] = (num * (1.0 / tot)).astype(o_ref.dtype)
            return carry

        lax.fori_loop(0, dil * nqb, q_block, 0, unroll=ATTN_UNROLL)


def dilated_attention(qkv, slopes):
    b, s, _ = qkv.shape
    h, dh = ATTN_HEADS, ATTN_DH
    assert s % (max(ATTN_DILATIONS) * ATTN_TK) == 0
    return pl.pallas_call(
        functools.partial(_attn_body, s=s),
        grid=(h, b),
        in_specs=[
            pl.BlockSpec((1, s, dh), lambda j, i: (i, 0, j)),
            pl.BlockSpec((1, s, dh), lambda j, i: (i, 0, h + j)),
            pl.BlockSpec((1, s, dh), lambda j, i: (i, 0, 2 * h + j)),
            pl.BlockSpec((1, SUBLANES, LANES), lambda j, i: (j, 0, 0)),
        ],
        out_specs=pl.BlockSpec((1, s, dh), lambda j, i: (i, 0, j)),
        out_shape=jax.ShapeDtypeStruct((b, s, h * dh), BF16),
        scratch_shapes=[
            pltpu.VMEM((len(ATTN_DILATIONS), 3, ATTN_TQ, ATTN_TK), F32),
            pltpu.VMEM((3, s, dh), F32),
            pltpu.VMEM((len(ATTN_DILATIONS) - 1, s, dh), F32),
            pltpu.VMEM((len(ATTN_DILATIONS) - 1, s, dh), F32),
            pltpu.VMEM((len(ATTN_DILATIONS) - 1, s, dh), F32),
        ],
        compiler_params=_cparams("parallel", "arbitrary"),
        name="dilated_attn",
    )(qkv, qkv, qkv, slopes)


MIXER_OUT_TM = 512


FFN_DOWN_TM = 256


def _ffn(x, h, w1, w3, w2, layer, next_norm_g, last):
    u = swiglu_up(h, w1, w3, layer)
    return matmul([(u, w2, 0, layer)], res=x, norm_g=next_norm_g,
                  norm_dtype=F32 if last else BF16, keep_sum=not last,
                  tm=FFN_DOWN_TM, tn=w2.shape[-1], name="ffn_down")


def _even_layer(x, bsz, seq, norm_g, w_in, gate_b, conv_w, conv_b, rg_wa, rg_ba, rg_wx, rg_bx,
                rg_lam, head_g, w_out, next_norm_g):
    heads, dk, dv = MLSTM_HEADS, MLSTM_DK, MLSTM_DV
    n_qkvo = 2 * heads * dk + 2 * heads * dv
    n_gate = 4 * heads
    width = RNN_BLOCKS * RNN_BLOCK
    w_in_t = jnp.transpose(w_in)
    w_main_t = jnp.concatenate([w_in_t[:n_qkvo], w_in_t[n_qkvo + n_gate:]], axis=0).astype(BF16)
    w_gate_t = w_in_t[n_qkvo:n_qkvo + n_gate].astype(BF16)
    z, zg_t = norm_matmul(x, norm_g, w_main_t, w_side_t=w_gate_t, w_transposed=True,
                          name="in_proj")
    z = z.reshape(bsz, seq, -1)

    chunk = MLSTM_CHUNK
    nc = seq // chunk
    gates_rows = jnp.transpose(zg_t[:n_gate].reshape(4, heads, bsz, nc, chunk), (2, 1, 0, 3, 4))
    gate_b_rows = jnp.broadcast_to(
        jnp.transpose(gate_b.astype(F32).reshape(4, heads))[:, :, None, None], (heads, 4, 1, chunk))
    y_a = mlstm_mixer(z, gates_rows, gate_b_rows, head_g.astype(F32).reshape(heads, 1, dv), chunk)

    y_b = rglru_mixer(z, n_qkvo // RNN_BLOCK, (n_qkvo + width) // RNN_BLOCK,
                      conv_w.astype(F32), conv_b.astype(F32).reshape(1, width),
                      (0.5 * rg_wa).astype(BF16), (0.5 * rg_wx).astype(BF16),
                      0.5 * rg_ba.astype(F32), 0.5 * rg_bx.astype(F32), rg_lam.astype(F32))

    w_out_b = w_out.astype(BF16)
    m = bsz * seq
    assert heads * dv == width
    return matmul([(y_a.reshape(m, -1), w_out_b, 0),
                   (y_b.reshape(m, -1), w_out_b, 1)], res=x, norm_g=next_norm_g,
                  tm=MIXER_OUT_TM, tn=w_out.shape[1], name="out_proj")


def _odd_layer(x, h, bsz, seq, norm_g, w_qkv, w_o, next_norm_g):
    if h is None:
        qkv = norm_matmul(x, norm_g, w_qkv.astype(BF16), tm=256, name="qkv_proj")
    else:
        qkv = matmul([(h, w_qkv.astype(BF16))], tn=2048, name="qkv_proj")
    qkv = qkv.reshape(bsz, seq, -1)
    slopes = jnp.exp2(-ALIBI_MAX_BIAS * jnp.arange(1, ATTN_HEADS + 1, dtype=F32) / ATTN_HEADS)
    slopes = jnp.broadcast_to(slopes[:, None, None], (ATTN_HEADS, SUBLANES, LANES))
    o = dilated_attention(qkv, slopes)
    return matmul([(o.reshape(bsz * seq, -1), w_o.astype(BF16))], res=x, norm_g=next_norm_g,
                  tm=MIXER_OUT_TM, tn=w_o.shape[1], name="attn_out_proj")


def kernel(x, e_norm, e_w_in, e_gate_b, e_conv_w, e_conv_b, e_rg_wa, e_rg_ba, e_rg_wx, e_rg_bx,
           e_rg_lam, e_head_g, e_w_out, o_norm, o_w_qkv, o_w_o, f_norm, f_w1, f_w3, f_w2,
           final_norm):
    bsz, seq, d = x.shape
    depth = f_norm.shape[0]
    xs = x.reshape(bsz * seq, d).astype(F32)
    w1, w3, w2 = f_w1.astype(BF16), f_w3.astype(BF16), f_w2.astype(BF16)
    hs = None
    for l in range(depth):
        if l % 2 == 0:
            e = l // 2
            xs, hs = _even_layer(xs, bsz, seq, e_norm[e], e_w_in[e], e_gate_b[e], e_conv_w[e],
                                 e_conv_b[e], e_rg_wa[e], e_rg_ba[e], e_rg_wx[e], e_rg_bx[e],
                                 e_rg_lam[e], e_head_g[e], e_w_out[e], f_norm[l])
        else:
            o = l // 2
            xs, hs = _odd_layer(xs, hs, bsz, seq, o_norm[o], o_w_qkv[o], o_w_o[o], f_norm[l])
        if l == depth - 1:
            out = _ffn(xs, hs, w1, w3, w2, l, final_norm, last=True)
            return out.astype(x.dtype).reshape(bsz, seq, d)
        if (l + 1) % 2:
            xs, hs = _ffn(xs, hs, w1, w3, w2, l, o_norm[(l + 1) // 2], last=False)
        else:
            xs, hs = _ffn(xs, hs, w1, w3, w2, l, None, last=False), None
```

```python
import functools
import math

import jax
import jax.numpy as jnp
from jax import lax
from jax.experimental import pallas as pl
from jax.experimental.pallas import tpu as pltpu

F32 = jnp.float32
BF16 = jnp.bfloat16

RMS_EPS = 1e-6
NEG_BIG = -1e30

MLSTM_HEADS = 4
MLSTM_DK = 128
MLSTM_DV = 256
MLSTM_CHUNK = 256
RNN_BLOCKS = 8
RNN_BLOCK = 128
CONV_WIDTH = 4
CONV_LEFT = 2
RGLRU_C = 8.0
ATTN_HEADS = 16
ATTN_DH = 128
ATTN_HALF = 64
ATTN_DILATIONS = (1, 4, 16)
ALIBI_MAX_BIAS = 8.0

LANES = 128
SUBLANES = 8
VMEM_LIMIT_BYTES = 56 * 1024 * 1024


def _cparams(*semantics):
    return pltpu.CompilerParams(dimension_semantics=semantics,
                                vmem_limit_bytes=VMEM_LIMIT_BYTES)


NORM_ROWS = 128


def _norm_rows_into(x_ref, g_ref, h_ref):
    g = g_ref[...]

    def rows(t, carry):
        sl = pl.ds(pl.multiple_of(t * NORM_ROWS, NORM_ROWS), NORM_ROWS)
        x = x_ref[sl, :]
        ms = jnp.mean(x * x, axis=-1, keepdims=True)
        h_ref[sl, :] = (x * lax.rsqrt(ms + RMS_EPS) * g).astype(h_ref.dtype)
        return carry

    lax.fori_loop(0, x_ref.shape[0] // NORM_ROWS, rows, 0)


def _weight_spec(w, kdim, tn, row_block=0, layer=None, **mode):
    if w.ndim == 2:
        return pl.BlockSpec((kdim, tn), lambda i, j: (row_block, j), **mode)
    return pl.BlockSpec((None, kdim, tn), lambda i, j: (layer, row_block, j), **mode)


def _matmul_body(*refs, n_pairs, has_res, has_norm, keep_sum):
    pairs = [(refs[2 * p], refs[2 * p + 1]) for p in range(n_pairs)]
    rest = list(refs[2 * n_pairs:])
    r_ref = rest.pop(0) if has_res else None
    g_ref = rest.pop(0) if has_norm else None
    y_ref = rest.pop(0) if keep_sum else rest.pop(-1)

    acc = None
    for a_ref, w_ref in pairs:
        d = jnp.dot(a_ref[...], w_ref[...], preferred_element_type=F32)
        acc = d if acc is None else acc + d
    if has_res:
        acc = acc + r_ref[...]
    y_ref[...] = acc.astype(y_ref.dtype)
    if has_norm:
        _norm_rows_into(y_ref, g_ref, rest.pop(0))


def matmul(pairs, res=None, norm_g=None, norm_dtype=BF16, keep_sum=True, tm=1024, tn=1024,
           name="matmul"):
    m = pairs[0][0].shape[0]
    n = pairs[0][1].shape[-1]
    tn = min(tn, n)
    assert m % tm == 0 and n % tn == 0 and (keep_sum or norm_g is not None)
    assert norm_g is None or (tn == n and tm % NORM_ROWS == 0)
    w_mode = dict(pipeline_mode=pl.Buffered(1)) if tn == n else {}
    in_specs, args = [], []
    for a, w, *where in pairs:
        kdim = a.shape[1]
        kb, layer = (list(where) + [0, None])[:2] if where else (0, None)
        in_specs.append(pl.BlockSpec((tm, kdim), lambda i, j: (i, 0)))
        in_specs.append(_weight_spec(w, kdim, tn, kb, layer, **w_mode))
        args += [a, w]
    if res is not None:
        in_specs.append(pl.BlockSpec((tm, tn), lambda i, j: (i, j)))
        args.append(res)
    out_specs, out_shape, scratch = [], [], []
    if keep_sum:
        out_specs.append(pl.BlockSpec((tm, tn), lambda i, j: (i, j)))
        out_shape.append(jax.ShapeDtypeStruct((m, n), F32))
    else:
        scratch.append(pltpu.VMEM((tm, n), F32))
    if norm_g is not None:
        in_specs.append(pl.BlockSpec((1, n), lambda i, j: (0, 0)))
        args.append(norm_g.reshape(1, n).astype(F32))
        out_specs.append(pl.BlockSpec((tm, n), lambda i, j: (i, 0)))
        out_shape.append(jax.ShapeDtypeStruct((m, n), norm_dtype))
    outs = pl.pallas_call(
        functools.partial(_matmul_body, n_pairs=len(pairs), has_res=res is not None,
                          has_norm=norm_g is not None, keep_sum=keep_sum),
        grid=(m // tm, n // tn),
        in_specs=in_specs,
        out_specs=out_specs,
        out_shape=out_shape,
        scratch_shapes=scratch,
        compiler_params=_cparams("parallel", "parallel"),
        name=name,
    )(*args)
    return outs if len(outs) > 1 else outs[0]


SWIGLU_ROWS = 1024


def _swiglu_up_body(a_ref, w1_ref, w3_ref, o_ref):
    rows = min(SWIGLU_ROWS, a_ref.shape[0])
    for r0 in range(0, a_ref.shape[0], rows):
        a = a_ref[pl.ds(r0, rows), :]
        u = jnp.dot(a, w1_ref[...], preferred_element_type=F32)
        v = jnp.dot(a, w3_ref[...], preferred_element_type=F32)
        o_ref[pl.ds(r0, rows), :] = (u * jax.nn.sigmoid(u) * v).astype(o_ref.dtype)


def swiglu_up(a, w1, w3, layer, tm=4096, tn=512):
    m, kdim = a.shape
    n = w1.shape[-1]
    assert m % tm == 0 and n % tn == 0
    return pl.pallas_call(
        _swiglu_up_body,
        grid=(m // tm, n // tn),
        in_specs=[pl.BlockSpec((tm, kdim), lambda i, j: (i, 0)),
                  _weight_spec(w1, kdim, tn, layer=layer),
                  _weight_spec(w3, kdim, tn, layer=layer)],
        out_specs=pl.BlockSpec((tm, tn), lambda i, j: (i, j)),
        out_shape=jax.ShapeDtypeStruct((m, n), BF16),
        compiler_params=_cparams("parallel", "parallel"),
        name="swiglu_up",
    )(a, w1, w3)


NORM_CHUNK = 256


def _norm_matmul_body(*refs, has_side, w_transposed):
    if has_side:
        x_ref, g_ref, w_ref, ws_ref, o_ref, os_ref = refs
    else:
        x_ref, g_ref, w_ref, o_ref = refs
    w_dims = (((1,), (1,)), ((), ())) if w_transposed else (((1,), (0,)), ((), ()))

    g = g_ref[...]
    for r0 in range(0, x_ref.shape[0], NORM_CHUNK):
        sl = pl.ds(r0, NORM_CHUNK)
        x = x_ref[sl, :]
        ms = jnp.mean(x * x, axis=-1, keepdims=True)
        h = (x * lax.rsqrt(ms + RMS_EPS) * g).astype(BF16)
        o_ref[sl, :] = lax.dot_general(h, w_ref[...], w_dims,
                                       preferred_element_type=F32).astype(o_ref.dtype)
        if has_side:
            os_ref[:, sl] = lax.dot_general(ws_ref[...], h, (((1,), (1,)), ((), ())),
                                            preferred_element_type=F32)


def norm_matmul(x, g, w, w_side_t=None, w_transposed=False, out_dtype=F32, tm=512,
                name="norm_matmul"):
    m, d = x.shape
    n = w.shape[0] if w_transposed else w.shape[1]
    assert m % tm == 0 and tm % NORM_CHUNK == 0
    resident = dict(pipeline_mode=pl.Buffered(1))
    in_specs = [pl.BlockSpec((tm, d), lambda i: (i, 0)),
                pl.BlockSpec((1, d), lambda i: (0, 0)),
                pl.BlockSpec(w.shape, lambda i: (0, 0), **resident)]
    args = [x, g.reshape(1, d).astype(F32), w]
    out_specs = [pl.BlockSpec((tm, n), lambda i: (i, 0))]
    out_shape = [jax.ShapeDtypeStruct((m, n), out_dtype)]
    if w_side_t is not None:
        ns = w_side_t.shape[0]
        in_specs.append(pl.BlockSpec((ns, d), lambda i: (0, 0), **resident))
        args.append(w_side_t)
        out_specs.append(pl.BlockSpec((ns, tm), lambda i: (0, i)))
        out_shape.append(jax.ShapeDtypeStruct((ns, m), F32))
    outs = pl.pallas_call(
        functools.partial(_norm_matmul_body, has_side=w_side_t is not None,
                          w_transposed=w_transposed),
        grid=(m // tm,),
        in_specs=in_specs,
        out_specs=out_specs,
        out_shape=out_shape,
        compiler_params=_cparams("parallel"),
        name=name,
    )(*args)
    return outs if w_side_t is not None else outs[0]


def _log_sigmoid(x):
    return jnp.minimum(x, 0.0) - jnp.log1p(jnp.exp(-jnp.abs(x)))


SCAN_RADIX = 4


def _lane_scan(x, combine, identity, reverse):
    width = x.shape[-1]
    lane = lax.broadcasted_iota(jnp.int32, x.shape, 1)

    def shifted(v, sh):
        if reverse:
            return jnp.where(lane < width - sh, pltpu.roll(v, width - sh, axis=1), identity)
        return jnp.where(lane >= sh, pltpu.roll(v, sh, axis=1), identity)

    sh = 1
    while sh < width:
        parts = [shifted(x, k * sh) for k in range(1, SCAN_RADIX) if k * sh < width]
        for p in parts:
            x = combine(x, p)
        sh *= SCAN_RADIX
    return x


def _mlstm_body(q_ref, k_ref, v_ref, og_ref, g_ref, gb_ref, hg_ref, y_ref,
                rows_ref, cols_ref, hf_ref, hb_ref, ct_ref, *, nc, chunk):
    dk, dv = MLSTM_DK, MLSTM_DV
    scale = dk ** -0.5

    g = g_ref[0, 0] + gb_ref[0]
    rows = []
    for d in range(2):
        b = _lane_scan(_log_sigmoid(g[2 * d + 1]), jnp.add, 0.0, reverse=d == 1)
        gg = g[2 * d] - b
        gmax = _lane_scan(gg, jnp.maximum, -jnp.inf, reverse=d == 1)
        rows += [gg, b, gmax]
        for j, r in enumerate((gg, b, gmax)):
            rows_ref[3 * d + j] = r
    rowmat = jnp.concatenate(rows + [jnp.zeros((LANES - 6 * nc, chunk), F32)], axis=0)
    colmat = rowmat.T
    for c in range(nc):
        cols_ref[c] = colmat if c == 0 else pltpu.roll(colmat, LANES - c, axis=1)

    ct_ref[...] = jnp.zeros_like(ct_ref)

    row_id = lax.broadcasted_iota(jnp.int32, (chunk, chunk), 0)
    col_id = lax.broadcasted_iota(jnp.int32, (chunk, chunk), 1)
    causal = (row_id >= col_id, row_id <= col_id)
    ones = jnp.ones((chunk, LANES), BF16)

    def lanes2(x):
        return jnp.concatenate([x] * (dv // LANES), axis=1)

    hg = hg_ref[0]

    def emit(r0, h, own_ref, other_ref):
        rows = pl.ds(r0, chunk)
        if other_ref is None:
            own_ref[rows, :] = h
            return
        hm = h + other_ref[rows, :]
        ms = jnp.mean(hm * hm, axis=-1, keepdims=True)
        hn = hm * lax.rsqrt(ms + RMS_EPS) * hg
        y_ref[0, rows, :] = (hn * jax.nn.sigmoid(og_ref[0, rows, :])).astype(y_ref.dtype)

    def chunk_step(c, m, d, own_ref, other_ref):
        r0 = pl.multiple_of(c * chunk, chunk)
        qb = (q_ref[0, pl.ds(r0, chunk), :] * scale).astype(BF16)
        kb = k_ref[0, pl.ds(r0, chunk), :].astype(BF16)
        vf = v_ref[0, pl.ds(r0, chunk), :]
        cols = cols_ref[c]

        def column(j):
            lane = (3 * d + j) * nc
            return jnp.broadcast_to(cols[:, lane:lane + 1], (chunk, LANES))

        gg_col, b_col, gmax_col = column(0), column(1), column(2)
        gg_row = rows_ref[3 * d, pl.ds(c, 1), :]
        end = chunk - 1 if d == 0 else 0
        total = rows_ref[3 * d + 1, pl.ds(c, 1), :][:, end:end + 1]
        gmax_end = rows_ref[3 * d + 2, pl.ds(c, 1), :][:, end:end + 1]

        m_row = jnp.maximum(gmax_col, m)
        w_intra = jnp.exp(jnp.where(causal[d], gg_row - lanes2(m_row), NEG_BIG))
        w_inter = jnp.exp(m - m_row)
        s = lax.dot_general(qb, kb, (((1,), (1,)), ((), ())),
                            preferred_element_type=F32) * w_intra
        ct = ct_ref[d]
        intra = jnp.dot(s.astype(BF16), jnp.concatenate([vf.astype(BF16), ones], axis=1),
                        preferred_element_type=F32)
        inter = jnp.dot(qb, ct.astype(BF16), preferred_element_type=F32)
        num = intra[:, :dv] + lanes2(w_inter) * inter[:, :dv]
        den = intra[:, dv:] + w_inter * inter[:, dv:]
        inv = 1.0 / jnp.maximum(jnp.abs(den), jnp.exp(-(b_col + m_row)))
        emit(r0, num * lanes2(inv), own_ref, other_ref)

        m_new = jnp.maximum(total + m, total + gmax_end)
        w_src = jnp.exp(total + gg_col - m_new)
        decay = jnp.exp(total + m - m_new)
        wv = jnp.concatenate([lanes2(w_src) * vf, w_src], axis=1).astype(BF16)
        ct_ref[d] = decay * ct + lax.dot_general(
            kb, wv, (((0,), (0,)), ((), ())), preferred_element_type=F32)
        return m_new

    def first_half(c, carry):
        m_f, m_b = carry
        m_f = chunk_step(c, m_f, 0, hf_ref, None)
        m_b = chunk_step(nc - 1 - c, m_b, 1, hb_ref, None)
        return m_f, m_b

    def second_half(c, carry):
        m_f, m_b = carry
        m_f = chunk_step(c, m_f, 0, None, hb_ref)
        m_b = chunk_step(nc - 1 - c, m_b, 1, None, hf_ref)
        return m_f, m_b

    m0 = jnp.full((1, 1), NEG_BIG, F32)
    carry = lax.fori_loop(0, nc // 2, first_half, (m0, m0), unroll=2)
    lax.fori_loop(nc // 2, nc, second_half, carry, unroll=2)


def mlstm_mixer(z, gates_rows, gate_b_rows, head_g, chunk=MLSTM_CHUNK):
    b, s, _ = z.shape
    h, dk, dv = MLSTM_HEADS, MLSTM_DK, MLSTM_DV
    nc = s // chunk
    assert s % chunk == 0 and 6 * nc <= LANES and chunk % LANES == 0 and nc % 2 == 0
    kq = h * dk // dk
    kv = 2 * h * dk // dv
    ko = kv + h
    return pl.pallas_call(
        functools.partial(_mlstm_body, nc=nc, chunk=chunk),
        grid=(b, h),
        in_specs=[
            pl.BlockSpec((1, s, dk), lambda i, j: (i, 0, j)),
            pl.BlockSpec((1, s, dk), lambda i, j: (i, 0, kq + j)),
            pl.BlockSpec((1, s, dv), lambda i, j: (i, 0, kv + j)),
            pl.BlockSpec((1, s, dv), lambda i, j: (i, 0, ko + j)),
            pl.BlockSpec((1, 1, 4, nc, chunk), lambda i, j: (i, j, 0, 0, 0)),
            pl.BlockSpec((1, 4, 1, chunk), lambda i, j: (j, 0, 0, 0)),
            pl.BlockSpec((1, 1, dv), lambda i, j: (j, 0, 0)),
        ],
        out_specs=pl.BlockSpec((1, s, dv), lambda i, j: (i, 0, j)),
        out_shape=jax.ShapeDtypeStruct((b, s, h * dv), BF16),
        scratch_shapes=[
            pltpu.VMEM((6, nc, chunk), F32),
            pltpu.VMEM((nc, chunk, LANES), F32),
            pltpu.VMEM((s, dv), F32),
            pltpu.VMEM((s, dv), F32),
            pltpu.VMEM((2, dk, dv + LANES), F32),
        ],
        compiler_params=_cparams("parallel", "parallel"),
        name="mlstm",
    )(z, z, z, z, gates_rows, gate_b_rows, head_g)


RG_TILE = 512
RG_PAD = SUBLANES
RG_SCAN_UNROLL = 8


def _softplus(x):
    return jnp.maximum(x, 0.0) + jnp.log1p(jnp.exp(-jnp.abs(x)))


def _gelu_tanh(x):
    c = math.sqrt(2.0 / math.pi)
    return x * (0.5 * (1.0 + jnp.tanh(c * (x + 0.044715 * (x * x * x)))))


def _rglru_body(x_ref, gr_ref, cw_ref, cb_ref, wa_ref, wx_ref, ba_ref, bx_ref, lam_ref, y_ref,
                xpad_ref, a_ref, u_ref, as_ref, us_ref, h_ref, *, s):
    nt = s // RG_TILE
    zeros = jnp.zeros((RG_PAD, LANES), F32)
    xpad_ref[pl.ds(0, RG_PAD), :] = zeros
    xpad_ref[pl.ds(RG_PAD + s, RG_PAD), :] = zeros

    def copy_in(t, carry):
        r0 = pl.multiple_of(t * RG_TILE, RG_TILE)
        xpad_ref[pl.ds(RG_PAD + r0, RG_TILE), :] = x_ref[0, pl.ds(r0, RG_TILE), :]
        return carry

    lax.fori_loop(0, nt, copy_in, 0)

    cw = cw_ref[...]
    cb = cb_ref[...]
    half_sp = [0.5 * RGLRU_C * _softplus(-lam_ref[d:d + 1, :]) for d in range(2)]

    def gates(t, carry):
        r0 = pl.multiple_of(t * RG_TILE, RG_TILE)
        xc = None
        for j in range(CONV_WIDTH):
            tap = xpad_ref[pl.ds(r0 + RG_PAD - CONV_LEFT + j, RG_TILE), :] * cw[j:j + 1, :]
            xc = tap if xc is None else xc + tap
        xc = xc + cb
        xcb = xc.astype(BF16)
        half_xc = 0.5 * xc
        for d in range(2):
            t_r = jnp.tanh(jnp.dot(xcb, wa_ref[d, 0], preferred_element_type=F32)
                           + ba_ref[d:d + 1, :])
            t_i = jnp.tanh(jnp.dot(xcb, wx_ref[d, 0], preferred_element_type=F32)
                           + bx_ref[d:d + 1, :])
            neg_log_a = half_sp[d] * t_r + half_sp[d]
            gated_x = half_xc * t_i + half_xc
            a = jnp.exp(-neg_log_a)
            one_minus_a2 = jnp.tanh(neg_log_a) * (1.0 + a * a)
            root = jnp.where(one_minus_a2 > 0.0, one_minus_a2 * lax.rsqrt(one_minus_a2), 0.0)
            a_ref[d, pl.ds(r0, RG_TILE), :] = a
            u_ref[d, pl.ds(r0, RG_TILE), :] = root * gated_x
        return carry

    def block_scan(t, carry):
        r0 = pl.multiple_of(t * RG_TILE, RG_TILE)
        for d in range(2):
            a_prev = u_prev = None
            for r in (range(SUBLANES) if d == 0 else reversed(range(SUBLANES))):
                rows = pl.ds(r0 + r, RG_TILE // SUBLANES, stride=SUBLANES)
                a, u = a_ref[d, rows, :], u_ref[d, rows, :]
                if a_prev is not None:
                    u = a * u_prev + u
                    a = a * a_prev
                as_ref[d, rows, :] = a
                us_ref[d, rows, :] = u
                a_prev, u_prev = a, u
        return carry

    lax.fori_loop(0, nt, gates, 0)
    lax.fori_loop(0, nt, block_scan, 0)

    nblk = s // SUBLANES

    def scan(j, carry):
        h_f, h_b = carry
        for d, last, h in ((0, SUBLANES - 1, h_f), (1, 0, h_b)):
            blk = j if d == 0 else nblk - 1 - j
            rows = pl.ds(pl.multiple_of(blk * SUBLANES, SUBLANES), SUBLANES)
            a, u = as_ref[d, rows, :], us_ref[d, rows, :]
            h_ref[d, rows, :] = a * h + u
            a_end = jnp.broadcast_to(a[last:last + 1, :], (SUBLANES, LANES))
            u_end = jnp.broadcast_to(u[last:last + 1, :], (SUBLANES, LANES))
            if d == 0:
                h_f = a_end * h + u_end
            else:
                h_b = a_end * h + u_end
        return h_f, h_b

    h0 = jnp.zeros((SUBLANES, LANES), F32)
    lax.fori_loop(0, nblk, scan, (h0, h0), unroll=RG_SCAN_UNROLL)

    def finish(t, carry):
        r0 = pl.multiple_of(t * RG_TILE, RG_TILE)
        hr = h_ref[0, pl.ds(r0, RG_TILE), :] + h_ref[1, pl.ds(r0, RG_TILE), :]
        y_ref[0, pl.ds(r0, RG_TILE), :] = (
            hr * _gelu_tanh(gr_ref[0, pl.ds(r0, RG_TILE), :])).astype(y_ref.dtype)
        return carry

    lax.fori_loop(0, nt, finish, 0)


def rglru_mixer(z, xr_block0, gr_block0, conv_w, conv_b, wa, wx, ba, bx, lam):
    b, s, _ = z.shape
    width = RNN_BLOCKS * RNN_BLOCK
    assert s % RG_TILE == 0
    return pl.pallas_call(
        functools.partial(_rglru_body, s=s),
        grid=(b, RNN_BLOCKS),
        in_specs=[
            pl.BlockSpec((1, s, RNN_BLOCK), lambda i, j: (i, 0, xr_block0 + j)),
            pl.BlockSpec((1, s, RNN_BLOCK), lambda i, j: (i, 0, gr_block0 + j)),
            pl.BlockSpec((CONV_WIDTH, RNN_BLOCK), lambda i, j: (0, j)),
            pl.BlockSpec((1, RNN_BLOCK), lambda i, j: (0, j)),
            pl.BlockSpec((2, 1, RNN_BLOCK, RNN_BLOCK), lambda i, j: (0, j, 0, 0)),
            pl.BlockSpec((2, 1, RNN_BLOCK, RNN_BLOCK), lambda i, j: (0, j, 0, 0)),
            pl.BlockSpec((2, RNN_BLOCK), lambda i, j: (0, j)),
            pl.BlockSpec((2, RNN_BLOCK), lambda i, j: (0, j)),
            pl.BlockSpec((2, RNN_BLOCK), lambda i, j: (0, j)),
        ],
        out_specs=pl.BlockSpec((1, s, RNN_BLOCK), lambda i, j: (i, 0, j)),
        out_shape=jax.ShapeDtypeStruct((b, s, width), BF16),
        scratch_shapes=[
            pltpu.VMEM((s + 2 * RG_PAD, RNN_BLOCK), F32),
            pltpu.VMEM((2, s, RNN_BLOCK), F32),
            pltpu.VMEM((2, s, RNN_BLOCK), F32),
            pltpu.VMEM((2, s, RNN_BLOCK), F32),
            pltpu.VMEM((2, s, RNN_BLOCK), F32),
            pltpu.VMEM((2, s, RNN_BLOCK), F32),
        ],
        compiler_params=_cparams("parallel", "parallel"),
        name="rglru",
    )(z, z, conv_w, conv_b, wa, wx, ba, bx, lam)


ATTN_TQ = 128
ATTN_TK = ATTN_TQ + 2 * ATTN_HALF
ATTN_UNROLL = 32
LOG2E = math.log2(math.e)


def _attn_body(q_ref, k_ref, v_ref, slope_ref, o_ref, bias_ref, x4_ref, og_ref, dg_ref, mg_ref,
               *, s):
    qscale = ATTN_DH ** -0.5 * LOG2E

    @pl.when(pl.program_id(1) == 0)
    def _():
        slope = slope_ref[0, 0:1, 0:1] * LOG2E
        qi = lax.broadcasted_iota(jnp.int32, (ATTN_TQ, ATTN_TK), 0)
        kj = lax.broadcasted_iota(jnp.int32, (ATTN_TQ, ATTN_TK), 1)
        for g, dil in enumerate(ATTN_DILATIONS):
            for e in range(3):
                rel = jnp.abs(kj - e * ATTN_HALF - qi)
                pen = slope * (rel * dil).astype(F32)
                bias_ref[g, e] = jnp.where(rel <= ATTN_HALF, -pen, NEG_BIG)

    s4 = s // 4
    srcs = (q_ref, k_ref, v_ref)

    def split4(t, carry):
        c = t // (s4 // ATTN_TK)
        p0 = (t % (s4 // ATTN_TK)) * ATTN_TK
        dst = pl.ds(pl.multiple_of(c * s4 + p0, ATTN_TK), ATTN_TK)
        for a in range(3):
            x = srcs[a][0, pl.ds(c + 4 * p0, ATTN_TK, stride=4), :]
            x4_ref[a, dst, :] = x * qscale if a == 0 else x
        return carry

    lax.fori_loop(0, 4 * (s4 // ATTN_TK), split4, 0)

    ones = jnp.ones((ATTN_TK, ATTN_DH), BF16)

    for g, dil in reversed(list(enumerate(ATTN_DILATIONS))):
        sp = s // dil
        nqb = sp // ATTN_TQ

        def q_block(t, carry, g=g, dil=dil, sp=sp, nqb=nqb):
            r = t // nqb
            p0 = (t % nqb) * ATTN_TQ
            kstart = jnp.clip(p0 - ATTN_HALF, 0, sp - ATTN_TK)
            e = (p0 - kstart) // ATTN_HALF
            if dil == 1:
                qrows = pl.ds(pl.multiple_of(p0, ATTN_TQ), ATTN_TQ)
                krows = pl.ds(pl.multiple_of(kstart, ATTN_HALF), ATTN_TK)
                qf = q_ref[0, qrows, :] * qscale
                kf, vf = k_ref[0, krows, :], v_ref[0, krows, :]
                orows = qrows
            elif dil == 4:
                qrows = pl.ds(pl.multiple_of(r * s4 + p0, ATTN_TQ), ATTN_TQ)
                krows = pl.ds(pl.multiple_of(r * s4 + kstart, ATTN_HALF), ATTN_TK)
                qf, kf, vf = x4_ref[0, qrows, :], x4_ref[1, krows, :], x4_ref[2, krows, :]
                orows = pl.ds(r + 4 * p0, ATTN_TQ, stride=4)
            else:
                base = (r % 4) * s4 + r // 4
                qrows = pl.ds(base + 4 * p0, ATTN_TQ, stride=4)
                krows = pl.ds(base + 4 * kstart, ATTN_TK, stride=4)
                qf, kf, vf = x4_ref[0, qrows, :], x4_ref[1, krows, :], x4_ref[2, krows, :]
                orows = pl.ds(r + dil * p0, ATTN_TQ, stride=dil)
            qb, kb, vb = qf.astype(BF16), kf.astype(BF16), vf.astype(BF16)
            sc = lax.dot_general(qb, kb, (((1,), (1,)), ((), ())),
                                 preferred_element_type=F32) + bias_ref[g, e]
            m = jnp.max(sc, axis=-1, keepdims=True)
            p = jnp.exp2(sc - m).astype(BF16)
            pv = jnp.dot(p, jnp.concatenate([vb, ones], axis=1), preferred_element_type=F32)
            acc, den = pv[:, :ATTN_DH], pv[:, ATTN_DH:]
            if dil != 1:
                og_ref[g - 1, orows, :] = acc
                dg_ref[g - 1, orows, :] = den
                mg_ref[g - 1, orows, :] = jnp.broadcast_to(m, den.shape)
                return carry
            m4, m16 = mg_ref[0, orows, :], mg_ref[1, orows, :]
            mx = jnp.maximum(jnp.maximum(m4, m16), m)
            e1, e4, e16 = jnp.exp2(m - mx), jnp.exp2(m4 - mx), jnp.exp2(m16 - mx)
            num = e1 * acc + e4 * og_ref[0, orows, :] + e16 * og_ref[1, orows, :]
            tot = e1 * den + e4 * dg_ref[0, orows, :] + e16 * dg_ref[1, orows, :]
            o_ref[0, orows, :] = (num * (1.0 / tot)).astype(o_ref.dtype)
            return carry

        lax.fori_loop(0, dil * nqb, q_block, 0, unroll=ATTN_UNROLL)


def dilated_attention(qkv, slopes):
    b, s, _ = qkv.shape
    h, dh = ATTN_HEADS, ATTN_DH
    assert s % (max(ATTN_DILATIONS) * ATTN_TK) == 0
    return pl.pallas_call(
        functools.partial(_attn_body, s=s),
        grid=(h, b),
        in_specs=[
            pl.BlockSpec((1, s, dh), lambda j, i: (i, 0, j)),
            pl.BlockSpec((1, s, dh), lambda j, i: (i, 0, h + j)),
            pl.BlockSpec((1, s, dh), lambda j, i: (i, 0, 2 * h + j)),
            pl.BlockSpec((1, SUBLANES, LANES), lambda j, i: (j, 0, 0)),
        ],
        out_specs=pl.BlockSpec((1, s, dh), lambda j, i: (i, 0, j)),
        out_shape=jax.ShapeDtypeStruct((b, s, h * dh), BF16),
        scratch_shapes=[
            pltpu.VMEM((len(ATTN_DILATIONS), 3, ATTN_TQ, ATTN_TK), F32),
            pltpu.VMEM((3, s, dh), F32),
            pltpu.VMEM((len(ATTN_DILATIONS) - 1, s, dh), F32),
            pltpu.VMEM((len(ATTN_DILATIONS) - 1, s, dh), F32),
            pltpu.VMEM((len(ATTN_DILATIONS) - 1, s, dh), F32),
        ],
        compiler_params=_cparams("parallel", "arbitrary"),
        name="dilated_attn",
    )(qkv, qkv, qkv, slopes)


MIXER_OUT_TM = 512


FFN_DOWN_TM = 512


def _ffn(x, h, w1, w3, w2, layer, next_norm_g, last):
    u = swiglu_up(h, w1, w3, layer)
    return matmul([(u, w2, 0, layer)], res=x, norm_g=next_norm_g,
                  norm_dtype=F32 if last else BF16, keep_sum=not last,
                  tm=FFN_DOWN_TM, tn=w2.shape[-1], name="ffn_down")


def _even_layer(x, bsz, seq, norm_g, w_in, gate_b, conv_w, conv_b, rg_wa, rg_ba, rg_wx, rg_bx,
                rg_lam, head_g, w_out, next_norm_g):
    heads, dk, dv = MLSTM_HEADS, MLSTM_DK, MLSTM_DV
    n_qkvo = 2 * heads * dk + 2 * heads * dv
    n_gate = 4 * heads
    width = RNN_BLOCKS * RNN_BLOCK
    w_in_t = jnp.transpose(w_in)
    w_main_t = jnp.concatenate([w_in_t[:n_qkvo], w_in_t[n_qkvo + n_gate:]], axis=0).astype(BF16)
    w_gate_t = w_in_t[n_qkvo:n_qkvo + n_gate].astype(BF16)
    z, zg_t = norm_matmul(x, norm_g, w_main_t, w_side_t=w_gate_t, w_transposed=True,
                          name="in_proj")
    z = z.reshape(bsz, seq, -1)

    chunk = MLSTM_CHUNK
    nc = seq // chunk
    gates_rows = jnp.transpose(zg_t[:n_gate].reshape(4, heads, bsz, nc, chunk), (2, 1, 0, 3, 4))
    gate_b_rows = jnp.broadcast_to(
        jnp.transpose(gate_b.astype(F32).reshape(4, heads))[:, :, None, None], (heads, 4, 1, chunk))
    y_a = mlstm_mixer(z, gates_rows, gate_b_rows, head_g.astype(F32).reshape(heads, 1, dv), chunk)

    y_b = rglru_mixer(z, n_qkvo // RNN_BLOCK, (n_qkvo + width) // RNN_BLOCK,
                      conv_w.astype(F32), conv_b.astype(F32).reshape(1, width),
                      (0.5 * rg_wa).astype(BF16), (0.5 * rg_wx).astype(BF16),
                      0.5 * rg_ba.astype(F32), 0.5 * rg_bx.astype(F32), rg_lam.astype(F32))

    w_out_b = w_out.astype(BF16)
    m = bsz * seq
    assert heads * dv == width
    return matmul([(y_a.reshape(m, -1), w_out_b, 0),
                   (y_b.reshape(m, -1), w_out_b, 1)], res=x, norm_g=next_norm_g,
                  tm=MIXER_OUT_TM, tn=w_out.shape[1], name="out_proj")


def _odd_layer(x, h, bsz, seq, norm_g, w_qkv, w_o, next_norm_g):
    if h is None:
        qkv = norm_matmul(x, norm_g, w_qkv.astype(BF16), tm=256, name="qkv_proj")
    else:
        qkv = matmul([(h, w_qkv.astype(BF16))], tn=2048, name="qkv_proj")
    qkv = qkv.reshape(bsz, seq, -1)
    slopes = jnp.exp2(-ALIBI_MAX_BIAS * jnp.arange(1, ATTN_HEADS + 1, dtype=F32) / ATTN_HEADS)
    slopes = jnp.broadcast_to(slopes[:, None, None], (ATTN_HEADS, SUBLANES, LANES))
    o = dilated_attention(qkv, slopes)
    return matmul([(o.reshape(bsz * seq, -1), w_o.astype(BF16))], res=x, norm_g=next_norm_g,
                  tm=MIXER_OUT_TM, tn=w_o.shape[1], name="attn_out_proj")


def kernel(x, e_norm, e_w_in, e_gate_b, e_conv_w, e_conv_b, e_rg_wa, e_rg_ba, e_rg_wx, e_rg_bx,
           e_rg_lam, e_head_g, e_w_out, o_norm, o_w_qkv, o_w_o, f_norm, f_w1, f_w3, f_w2,
           final_norm):
    bsz, seq, d = x.shape
    depth = f_norm.shape[0]
    xs = x.reshape(bsz * seq, d).astype(F32)
    w1, w3, w2 = f_w1.astype(BF16), f_w3.astype(BF16), f_w2.astype(BF16)
    hs = None
    for l in range(depth):
        if l % 2 == 0:
            e = l // 2
            xs, hs = _even_layer(xs, bsz, seq, e_norm[e], e_w_in[e], e_gate_b[e], e_conv_w[e],
                                 e_conv_b[e], e_rg_wa[e], e_rg_ba[e], e_rg_wx[e], e_rg_bx[e],
                                 e_rg_lam[e], e_head_g[e], e_w_out[e], f_norm[l])
        else:
            o = l // 2
            xs, hs = _odd_layer(xs, hs, bsz, seq, o_norm[o], o_w_qkv[o], o_w_o[o], f_norm[l])
        if l == depth - 1:
            out = _ffn(xs, hs, w1, w3, w2, l, final_norm, last=True)
            return out.astype(x.dtype).reshape(bsz, seq, d)
        if (l + 1) % 2:
            xs, hs = _ffn(xs, hs, w1, w3, w2, l, o_norm[(l + 1) // 2], last=False)
        else:
            xs, hs = _ffn(xs, hs, w1, w3, w2, l, None, last=False), None
```

```python
import functools
import math

import jax
import jax.numpy as jnp
from jax import lax
from jax.experimental import pallas as pl
from jax.experimental.pallas import tpu as pltpu

F32 = jnp.float32
BF16 = jnp.bfloat16

RMS_EPS = 1e-6
NEG_BIG = -1e30

MLSTM_HEADS = 4
MLSTM_DK = 128
MLSTM_DV = 256
MLSTM_CHUNK = 256
RNN_BLOCKS = 8
RNN_BLOCK = 128
CONV_WIDTH = 4
CONV_LEFT = 2
RGLRU_C = 8.0
ATTN_HEADS = 16
ATTN_DH = 128
ATTN_HALF = 64
ATTN_DILATIONS = (1, 4, 16)
ALIBI_MAX_BIAS = 8.0

LANES = 128
SUBLANES = 8
VMEM_LIMIT_BYTES = 56 * 1024 * 1024


def _cparams(*semantics):
    return pltpu.CompilerParams(dimension_semantics=semantics,
                                vmem_limit_bytes=VMEM_LIMIT_BYTES)


NORM_ROWS = 128


def _norm_rows_into(x_ref, g_ref, h_ref):
    g = g_ref[...]

    def rows(t, carry):
        sl = pl.ds(pl.multiple_of(t * NORM_ROWS, NORM_ROWS), NORM_ROWS)
        x = x_ref[sl, :]
        ms = jnp.mean(x * x, axis=-1, keepdims=True)
        h_ref[sl, :] = (x * lax.rsqrt(ms + RMS_EPS) * g).astype(h_ref.dtype)
        return carry

    lax.fori_loop(0, x_ref.shape[0] // NORM_ROWS, rows, 0)


def _weight_spec(w, kdim, tn, row_block=0, layer=None, **mode):
    if w.ndim == 2:
        return pl.BlockSpec((kdim, tn), lambda i, j: (row_block, j), **mode)
    return pl.BlockSpec((None, kdim, tn), lambda i, j: (layer, row_block, j), **mode)


def _matmul_body(*refs, n_pairs, has_res, has_norm, keep_sum):
    pairs = [(refs[2 * p], refs[2 * p + 1]) for p in range(n_pairs)]
    rest = list(refs[2 * n_pairs:])
    r_ref = rest.pop(0) if has_res else None
    g_ref = rest.pop(0) if has_norm else None
    y_ref = rest.pop(0) if keep_sum else rest.pop(-1)

    acc = None
    for a_ref, w_ref in pairs:
        d = jnp.dot(a_ref[...], w_ref[...], preferred_element_type=F32)
        acc = d if acc is None else acc + d
    if has_res:
        acc = acc + r_ref[...]
    y_ref[...] = acc.astype(y_ref.dtype)
    if has_norm:
        _norm_rows_into(y_ref, g_ref, rest.pop(0))


def matmul(pairs, res=None, norm_g=None, norm_dtype=BF16, keep_sum=True, tm=1024, tn=1024,
           name="matmul"):
    m = pairs[0][0].shape[0]
    n = pairs[0][1].shape[-1]
    tn = min(tn, n)
    assert m % tm == 0 and n % tn == 0 and (keep_sum or norm_g is not None)
    assert norm_g is None or (tn == n and tm % NORM_ROWS == 0)
    w_mode = dict(pipeline_mode=pl.Buffered(1)) if tn == n else {}
    in_specs, args = [], []
    for a, w, *where in pairs:
        kdim = a.shape[1]
        kb, layer = (list(where) + [0, None])[:2] if where else (0, None)
        in_specs.append(pl.BlockSpec((tm, kdim), lambda i, j: (i, 0)))
        in_specs.append(_weight_spec(w, kdim, tn, kb, layer, **w_mode))
        args += [a, w]
    if res is not None:
        in_specs.append(pl.BlockSpec((tm, tn), lambda i, j: (i, j)))
        args.append(res)
    out_specs, out_shape, scratch = [], [], []
    if keep_sum:
        out_specs.append(pl.BlockSpec((tm, tn), lambda i, j: (i, j)))
        out_shape.append(jax.ShapeDtypeStruct((m, n), F32))
    else:
        scratch.append(pltpu.VMEM((tm, n), F32))
    if norm_g is not None:
        in_specs.append(pl.BlockSpec((1, n), lambda i, j: (0, 0)))
        args.append(norm_g.reshape(1, n).astype(F32))
        out_specs.append(pl.BlockSpec((tm, n), lambda i, j: (i, 0)))
        out_shape.append(jax.ShapeDtypeStruct((m, n), norm_dtype))
    outs = pl.pallas_call(
        functools.partial(_matmul_body, n_pairs=len(pairs), has_res=res is not None,
                          has_norm=norm_g is not None, keep_sum=keep_sum),
        grid=(m // tm, n // tn),
        in_specs=in_specs,
        out_specs=out_specs,
        out_shape=out_shape,
        scratch_shapes=scratch,
        compiler_params=_cparams("parallel", "parallel"),
        name=name,
    )(*args)
    return outs if len(outs) > 1 else outs[0]


SWIGLU_ROWS = 1024


def _swiglu_up_body(a_ref, w1_ref, w3_ref, o_ref):
    rows = min(SWIGLU_ROWS, a_ref.shape[0])
    for r0 in range(0, a_ref.shape[0], rows):
        a = a_ref[pl.ds(r0, rows), :]
        u = jnp.dot(a, w1_ref[...], preferred_element_type=F32)
        v = jnp.dot(a, w3_ref[...], preferred_element_type=F32)
        o_ref[pl.ds(r0, rows), :] = (u * jax.nn.sigmoid(u) * v).astype(o_ref.dtype)


def swiglu_up(a, w1, w3, layer, tm=4096, tn=512):
    m, kdim = a.shape
    n = w1.shape[-1]
    assert m % tm == 0 and n % tn == 0
    return pl.pallas_call(
        _swiglu_up_body,
        grid=(m // tm, n // tn),
        in_specs=[pl.BlockSpec((tm, kdim), lambda i, j: (i, 0)),
                  _weight_spec(w1, kdim, tn, layer=layer),
                  _weight_spec(w3, kdim, tn, layer=layer)],
        out_specs=pl.BlockSpec((tm, tn), lambda i, j: (i, j)),
        out_shape=jax.ShapeDtypeStruct((m, n), BF16),
        compiler_params=_cparams("parallel", "parallel"),
        name="swiglu_up",
    )(a, w1, w3)


NORM_CHUNK = 256


def _norm_matmul_body(*refs, has_side, w_transposed):
    if has_side:
        x_ref, g_ref, w_ref, ws_ref, o_ref, os_ref = refs
    else:
        x_ref, g_ref, w_ref, o_ref = refs
    w_dims = (((1,), (1,)), ((), ())) if w_transposed else (((1,), (0,)), ((), ()))

    g = g_ref[...]
    for r0 in range(0, x_ref.shape[0], NORM_CHUNK):
        sl = pl.ds(r0, NORM_CHUNK)
        x = x_ref[sl, :]
        ms = jnp.mean(x * x, axis=-1, keepdims=True)
        h = (x * lax.rsqrt(ms + RMS_EPS) * g).astype(BF16)
        o_ref[sl, :] = lax.dot_general(h, w_ref[...], w_dims,
                                       preferred_element_type=F32).astype(o_ref.dtype)
        if has_side:
            os_ref[:, sl] = lax.dot_general(ws_ref[...], h, (((1,), (1,)), ((), ())),
                                            preferred_element_type=F32)


def norm_matmul(x, g, w, w_side_t=None, w_transposed=False, out_dtype=F32, tm=512,
                name="norm_matmul"):
    m, d = x.shape
    n = w.shape[0] if w_transposed else w.shape[1]
    assert m % tm == 0 and tm % NORM_CHUNK == 0
    resident = dict(pipeline_mode=pl.Buffered(1))
    in_specs = [pl.BlockSpec((tm, d), lambda i: (i, 0)),
                pl.BlockSpec((1, d), lambda i: (0, 0)),
                pl.BlockSpec(w.shape, lambda i: (0, 0), **resident)]
    args = [x, g.reshape(1, d).astype(F32), w]
    out_specs = [pl.BlockSpec((tm, n), lambda i: (i, 0))]
    out_shape = [jax.ShapeDtypeStruct((m, n), out_dtype)]
    if w_side_t is not None:
        ns = w_side_t.shape[0]
        in_specs.append(pl.BlockSpec((ns, d), lambda i: (0, 0), **resident))
        args.append(w_side_t)
        out_specs.append(pl.BlockSpec((ns, tm), lambda i: (0, i)))
        out_shape.append(jax.ShapeDtypeStruct((ns, m), F32))
    outs = pl.pallas_call(
        functools.partial(_norm_matmul_body, has_side=w_side_t is not None,
                          w_transposed=w_transposed),
        grid=(m // tm,),
        in_specs=in_specs,
        out_specs=out_specs,
        out_shape=out_shape,
        compiler_params=_cparams("parallel"),
        name=name,
    )(*args)
    return outs if w_side_t is not None else outs[0]


def _log_sigmoid(x):
    return jnp.minimum(x, 0.0) - jnp.log1p(jnp.exp(-jnp.abs(x)))


SCAN_RADIX = 4


def _lane_scan(x, combine, identity, reverse):
    width = x.shape[-1]
    lane = lax.broadcasted_iota(jnp.int32, x.shape, 1)

    def shifted(v, sh):
        if reverse:
            return jnp.where(lane < width - sh, pltpu.roll(v, width - sh, axis=1), identity)
        return jnp.where(lane >= sh, pltpu.roll(v, sh, axis=1), identity)

    sh = 1
    while sh < width:
        parts = [shifted(x, k * sh) for k in range(1, SCAN_RADIX) if k * sh < width]
        for p in parts:
            x = combine(x, p)
        sh *= SCAN_RADIX
    return x


def _mlstm_body(q_ref, k_ref, v_ref, og_ref, g_ref, gb_ref, hg_ref, y_ref,
                rows_ref, cols_ref, hf_ref, hb_ref, ct_ref, *, nc, chunk):
    dk, dv = MLSTM_DK, MLSTM_DV
    scale = dk ** -0.5

    g = g_ref[0, 0] + gb_ref[0]
    rows = []
    for d in range(2):
        b = _lane_scan(_log_sigmoid(g[2 * d + 1]), jnp.add, 0.0, reverse=d == 1)
        gg = g[2 * d] - b
        gmax = _lane_scan(gg, jnp.maximum, -jnp.inf, reverse=d == 1)
        rows += [gg, b, gmax]
        for j, r in enumerate((gg, b, gmax)):
            rows_ref[3 * d + j] = r
    rowmat = jnp.concatenate(rows + [jnp.zeros((LANES - 6 * nc, chunk), F32)], axis=0)
    colmat = rowmat.T
    for c in range(nc):
        cols_ref[c] = colmat if c == 0 else pltpu.roll(colmat, LANES - c, axis=1)

    ct_ref[...] = jnp.zeros_like(ct_ref)

    row_id = lax.broadcasted_iota(jnp.int32, (chunk, chunk), 0)
    col_id = lax.broadcasted_iota(jnp.int32, (chunk, chunk), 1)
    causal = (row_id >= col_id, row_id <= col_id)
    ones = jnp.ones((chunk, LANES), BF16)

    def lanes2(x):
        return jnp.concatenate([x] * (dv // LANES), axis=1)

    hg = hg_ref[0]

    def emit(r0, h, own_ref, other_ref):
        rows = pl.ds(r0, chunk)
        if other_ref is None:
            own_ref[rows, :] = h
            return
        hm = h + other_ref[rows, :]
        ms = jnp.mean(hm * hm, axis=-1, keepdims=True)
        hn = hm * lax.rsqrt(ms + RMS_EPS) * hg
        y_ref[0, rows, :] = (hn * jax.nn.sigmoid(og_ref[0, rows, :])).astype(y_ref.dtype)

    def chunk_step(c, m, d, own_ref, other_ref):
        r0 = pl.multiple_of(c * chunk, chunk)
        qb = (q_ref[0, pl.ds(r0, chunk), :] * scale).astype(BF16)
        kb = k_ref[0, pl.ds(r0, chunk), :].astype(BF16)
        vf = v_ref[0, pl.ds(r0, chunk), :]
        cols = cols_ref[c]

        def column(j):
            lane = (3 * d + j) * nc
            return jnp.broadcast_to(cols[:, lane:lane + 1], (chunk, LANES))

        gg_col, b_col, gmax_col = column(0), column(1), column(2)
        gg_row = rows_ref[3 * d, pl.ds(c, 1), :]
        end = chunk - 1 if d == 0 else 0
        total = rows_ref[3 * d + 1, pl.ds(c, 1), :][:, end:end + 1]
        gmax_end = rows_ref[3 * d + 2, pl.ds(c, 1), :][:, end:end + 1]

        m_row = jnp.maximum(gmax_col, m)
        w_intra = jnp.exp(jnp.where(causal[d], gg_row - lanes2(m_row), NEG_BIG))
        w_inter = jnp.exp(m - m_row)
        s = lax.dot_general(qb, kb, (((1,), (1,)), ((), ())),
                            preferred_element_type=F32) * w_intra
        ct = ct_ref[d]
        intra = jnp.dot(s.astype(BF16), jnp.concatenate([vf.astype(BF16), ones], axis=1),
                        preferred_element_type=F32)
        inter = jnp.dot(qb, ct.astype(BF16), preferred_element_type=F32)
        num = intra[:, :dv] + lanes2(w_inter) * inter[:, :dv]
        den = intra[:, dv:] + w_inter * inter[:, dv:]
        inv = 1.0 / jnp.maximum(jnp.abs(den), jnp.exp(-(b_col + m_row)))
        emit(r0, num * lanes2(inv), own_ref, other_ref)

        m_new = jnp.maximum(total + m, total + gmax_end)
        w_src = jnp.exp(total + gg_col - m_new)
        decay = jnp.exp(total + m - m_new)
        wv = jnp.concatenate([lanes2(w_src) * vf, w_src], axis=1).astype(BF16)
        ct_ref[d] = decay * ct + lax.dot_general(
            kb, wv, (((0,), (0,)), ((), ())), preferred_element_type=F32)
        return m_new

    def first_half(c, carry):
        m_f, m_b = carry
        m_f = chunk_step(c, m_f, 0, hf_ref, None)
        m_b = chunk_step(nc - 1 - c, m_b, 1, hb_ref, None)
        return m_f, m_b

    def second_half(c, carry):
        m_f, m_b = carry
        m_f = chunk_step(c, m_f, 0, None, hb_ref)
        m_b = chunk_step(nc - 1 - c, m_b, 1, None, hf_ref)
        return m_f, m_b

    m0 = jnp.full((1, 1), NEG_BIG, F32)
    carry = lax.fori_loop(0, nc // 2, first_half, (m0, m0), unroll=2)
    lax.fori_loop(nc // 2, nc, second_half, carry, unroll=2)


def mlstm_mixer(z, gates_rows, gate_b_rows, head_g, chunk=MLSTM_CHUNK):
    b, s, _ = z.shape
    h, dk, dv = MLSTM_HEADS, MLSTM_DK, MLSTM_DV
    nc = s // chunk
    assert s % chunk == 0 and 6 * nc <= LANES and chunk % LANES == 0 and nc % 2 == 0
    kq = h * dk // dk
    kv = 2 * h * dk // dv
    ko = kv + h
    return pl.pallas_call(
        functools.partial(_mlstm_body, nc=nc, chunk=chunk),
        grid=(b, h),
        in_specs=[
            pl.BlockSpec((1, s, dk), lambda i, j: (i, 0, j)),
            pl.BlockSpec((1, s, dk), lambda i, j: (i, 0, kq + j)),
            pl.BlockSpec((1, s, dv), lambda i, j: (i, 0, kv + j)),
            pl.BlockSpec((1, s, dv), lambda i, j: (i, 0, ko + j)),
            pl.BlockSpec((1, 1, 4, nc, chunk), lambda i, j: (i, j, 0, 0, 0)),
            pl.BlockSpec((1, 4, 1, chunk), lambda i, j: (j, 0, 0, 0)),
            pl.BlockSpec((1, 1, dv), lambda i, j: (j, 0, 0)),
        ],
        out_specs=pl.BlockSpec((1, s, dv), lambda i, j: (i, 0, j)),
        out_shape=jax.ShapeDtypeStruct((b, s, h * dv), BF16),
        scratch_shapes=[
            pltpu.VMEM((6, nc, chunk), F32),
            pltpu.VMEM((nc, chunk, LANES), F32),
            pltpu.VMEM((s, dv), F32),
            pltpu.VMEM((s, dv), F32),
            pltpu.VMEM((2, dk, dv + LANES), F32),
        ],
        compiler_params=_cparams("parallel", "parallel"),
        name="mlstm",
    )(z, z, z, z, gates_rows, gate_b_rows, head_g)


RG_TILE = 512
RG_PAD = SUBLANES
RG_SCAN_UNROLL = 8


def _softplus(x):
    return jnp.maximum(x, 0.0) + jnp.log1p(jnp.exp(-jnp.abs(x)))


def _gelu_tanh(x):
    c = math.sqrt(2.0 / math.pi)
    return x * (0.5 * (1.0 + jnp.tanh(c * (x + 0.044715 * (x * x * x)))))


def _rglru_body(x_ref, gr_ref, cw_ref, cb_ref, wa_ref, wx_ref, ba_ref, bx_ref, lam_ref, y_ref,
                xpad_ref, a_ref, u_ref, as_ref, us_ref, h_ref, *, s):
    nt = s // RG_TILE
    zeros = jnp.zeros((RG_PAD, LANES), F32)
    xpad_ref[pl.ds(0, RG_PAD), :] = zeros
    xpad_ref[pl.ds(RG_PAD + s, RG_PAD), :] = zeros

    def copy_in(t, carry):
        r0 = pl.multiple_of(t * RG_TILE, RG_TILE)
        xpad_ref[pl.ds(RG_PAD + r0, RG_TILE), :] = x_ref[0, pl.ds(r0, RG_TILE), :]
        return carry

    lax.fori_loop(0, nt, copy_in, 0)

    cw = cw_ref[...]
    cb = cb_ref[...]
    half_sp = [0.5 * RGLRU_C * _softplus(-lam_ref[d:d + 1, :]) for d in range(2)]

    def gates(t, carry):
        r0 = pl.multiple_of(t * RG_TILE, RG_TILE)
        xc = None
        for j in range(CONV_WIDTH):
            tap = xpad_ref[pl.ds(r0 + RG_PAD - CONV_LEFT + j, RG_TILE), :] * cw[j:j + 1, :]
            xc = tap if xc is None else xc + tap
        xc = xc + cb
        xcb = xc.astype(BF16)
        half_xc = 0.5 * xc
        for d in range(2):
            t_r = jnp.tanh(jnp.dot(xcb, wa_ref[d, 0], preferred_element_type=F32)
                           + ba_ref[d:d + 1, :])
            t_i = jnp.tanh(jnp.dot(xcb, wx_ref[d, 0], preferred_element_type=F32)
                           + bx_ref[d:d + 1, :])
            neg_log_a = half_sp[d] * t_r + half_sp[d]
            gated_x = half_xc * t_i + half_xc
            a = jnp.exp(-neg_log_a)
            one_minus_a2 = jnp.tanh(neg_log_a) * (1.0 + a * a)
            root = jnp.where(one_minus_a2 > 0.0, one_minus_a2 * lax.rsqrt(one_minus_a2), 0.0)
            a_ref[d, pl.ds(r0, RG_TILE), :] = a
            u_ref[d, pl.ds(r0, RG_TILE), :] = root * gated_x
        return carry

    def block_scan(t, carry):
        r0 = pl.multiple_of(t * RG_TILE, RG_TILE)
        for d in range(2):
            a_prev = u_prev = None
            for r in (range(SUBLANES) if d == 0 else reversed(range(SUBLANES))):
                rows = pl.ds(r0 + r, RG_TILE // SUBLANES, stride=SUBLANES)
                a, u = a_ref[d, rows, :], u_ref[d, rows, :]
                if a_prev is not None:
                    u = a * u_prev + u
                    a = a * a_prev
                as_ref[d, rows, :] = a
                us_ref[d, rows, :] = u
                a_prev, u_prev = a, u
        return carry

    lax.fori_loop(0, nt, gates, 0)
    lax.fori_loop(0, nt, block_scan, 0)

    nblk = s // SUBLANES

    def scan(j, carry):
        h_f, h_b = carry
        for d, last, h in ((0, SUBLANES - 1, h_f), (1, 0, h_b)):
            blk = j if d == 0 else nblk - 1 - j
            rows = pl.ds(pl.multiple_of(blk * SUBLANES, SUBLANES), SUBLANES)
            a, u = as_ref[d, rows, :], us_ref[d, rows, :]
            h_ref[d, rows, :] = a * h + u
            a_end = jnp.broadcast_to(a[last:last + 1, :], (SUBLANES, LANES))
            u_end = jnp.broadcast_to(u[last:last + 1, :], (SUBLANES, LANES))
            if d == 0:
                h_f = a_end * h + u_end
            else:
                h_b = a_end * h + u_end
        return h_f, h_b

    h0 = jnp.zeros((SUBLANES, LANES), F32)
    lax.fori_loop(0, nblk, scan, (h0, h0), unroll=RG_SCAN_UNROLL)

    def finish(t, carry):
        r0 = pl.multiple_of(t * RG_TILE, RG_TILE)
        hr = h_ref[0, pl.ds(r0, RG_TILE), :] + h_ref[1, pl.ds(r0, RG_TILE), :]
        y_ref[0, pl.ds(r0, RG_TILE), :] = (
            hr * _gelu_tanh(gr_ref[0, pl.ds(r0, RG_TILE), :])).astype(y_ref.dtype)
        return carry

    lax.fori_loop(0, nt, finish, 0)


def rglru_mixer(z, xr_block0, gr_block0, conv_w, conv_b, wa, wx, ba, bx, lam):
    b, s, _ = z.shape
    width = RNN_BLOCKS * RNN_BLOCK
    assert s % RG_TILE == 0
    return pl.pallas_call(
        functools.partial(_rglru_body, s=s),
        grid=(b, RNN_BLOCKS),
        in_specs=[
            pl.BlockSpec((1, s, RNN_BLOCK), lambda i, j: (i, 0, xr_block0 + j)),
            pl.BlockSpec((1, s, RNN_BLOCK), lambda i, j: (i, 0, gr_block0 + j)),
            pl.BlockSpec((CONV_WIDTH, RNN_BLOCK), lambda i, j: (0, j)),
            pl.BlockSpec((1, RNN_BLOCK), lambda i, j: (0, j)),
            pl.BlockSpec((2, 1, RNN_BLOCK, RNN_BLOCK), lambda i, j: (0, j, 0, 0)),
            pl.BlockSpec((2, 1, RNN_BLOCK, RNN_BLOCK), lambda i, j: (0, j, 0, 0)),
            pl.BlockSpec((2, RNN_BLOCK), lambda i, j: (0, j)),
            pl.BlockSpec((2, RNN_BLOCK), lambda i, j: (0, j)),
            pl.BlockSpec((2, RNN_BLOCK), lambda i, j: (0, j)),
        ],
        out_specs=pl.BlockSpec((1, s, RNN_BLOCK), lambda i, j: (i, 0, j)),
        out_shape=jax.ShapeDtypeStruct((b, s, width), BF16),
        scratch_shapes=[
            pltpu.VMEM((s + 2 * RG_PAD, RNN_BLOCK), F32),
            pltpu.VMEM((2, s, RNN_BLOCK), F32),
            pltpu.VMEM((2, s, RNN_BLOCK), F32),
            pltpu.VMEM((2, s, RNN_BLOCK), F32),
            pltpu.VMEM((2, s, RNN_BLOCK), F32),
            pltpu.VMEM((2, s, RNN_BLOCK), F32),
        ],
        compiler_params=_cparams("parallel", "parallel"),
        name="rglru",
    )(z, z, conv_w, conv_b, wa, wx, ba, bx, lam)


ATTN_TQ = 128
ATTN_TK = ATTN_TQ + 2 * ATTN_HALF
ATTN_UNROLL = 32
LOG2E = math.log2(math.e)


def _attn_body(q_ref, k_ref, v_ref, slope_ref, o_ref, bias_ref, x4_ref, og_ref, dg_ref, mg_ref,
               *, s):
    qscale = ATTN_DH ** -0.5 * LOG2E

    @pl.when(pl.program_id(1) == 0)
    def _():
        slope = slope_ref[0, 0:1, 0:1] * LOG2E
        qi = lax.broadcasted_iota(jnp.int32, (ATTN_TQ, ATTN_TK), 0)
        kj = lax.broadcasted_iota(jnp.int32, (ATTN_TQ, ATTN_TK), 1)
        for g, dil in enumerate(ATTN_DILATIONS):
            for e in range(3):
                rel = jnp.abs(kj - e * ATTN_HALF - qi)
                pen = slope * (rel * dil).astype(F32)
                bias_ref[g, e] = jnp.where(rel <= ATTN_HALF, -pen, NEG_BIG)

    s4 = s // 4
    srcs = (q_ref, k_ref, v_ref)

    def split4(t, carry):
        c = t // (s4 // ATTN_TK)
        p0 = (t % (s4 // ATTN_TK)) * ATTN_TK
        dst = pl.ds(pl.multiple_of(c * s4 + p0, ATTN_TK), ATTN_TK)
        for a in range(3):
            x = srcs[a][0, pl.ds(c + 4 * p0, ATTN_TK, stride=4), :]
            x4_ref[a, dst, :] = x * qscale if a == 0 else x
        return carry

    lax.fori_loop(0, 4 * (s4 // ATTN_TK), split4, 0)

    ones = jnp.ones((ATTN_TK, ATTN_DH), BF16)

    for g, dil in reversed(list(enumerate(ATTN_DILATIONS))):
        sp = s // dil
        nqb = sp // ATTN_TQ

        def q_block(t, carry, g=g, dil=dil, sp=sp, nqb=nqb):
            r = t // nqb
            p0 = (t % nqb) * ATTN_TQ
            kstart = jnp.clip(p0 - ATTN_HALF, 0, sp - ATTN_TK)
            e = (p0 - kstart) // ATTN_HALF
            if dil == 1:
                qrows = pl.ds(pl.multiple_of(p0, ATTN_TQ), ATTN_TQ)
                krows = pl.ds(pl.multiple_of(kstart, ATTN_HALF), ATTN_TK)
                qf = q_ref[0, qrows, :] * qscale
                kf, vf = k_ref[0, krows, :], v_ref[0, krows, :]
                orows = qrows
            elif dil == 4:
                qrows = pl.ds(pl.multiple_of(r * s4 + p0, ATTN_TQ), ATTN_TQ)
                krows = pl.ds(pl.multiple_of(r * s4 + kstart, ATTN_HALF), ATTN_TK)
                qf, kf, vf = x4_ref[0, qrows, :], x4_ref[1, krows, :], x4_ref[2, krows, :]
                orows = pl.ds(r + 4 * p0, ATTN_TQ, stride=4)
            else:
                base = (r % 4) * s4 + r // 4
                qrows = pl.ds(base + 4 * p0, ATTN_TQ, stride=4)
                krows = pl.ds(base + 4 * kstart, ATTN_TK, stride=4)
                qf, kf, vf = x4_ref[0, qrows, :], x4_ref[1, krows, :], x4_ref[2, krows, :]
                orows = pl.ds(r + dil * p0, ATTN_TQ, stride=dil)
            qb, kb, vb = qf.astype(BF16), kf.astype(BF16), vf.astype(BF16)
            sc = lax.dot_general(qb, kb, (((1,), (1,)), ((), ())),
                                 preferred_element_type=F32) + bias_ref[g, e]
            m = jnp.max(sc, axis=-1, keepdims=True)
            p = jnp.exp2(sc - m).astype(BF16)
            pv = jnp.dot(p, jnp.concatenate([vb, ones], axis=1), preferred_element_type=F32)
            acc, den = pv[:, :ATTN_DH], pv[:, ATTN_DH:]
            if dil != 1:
                og_ref[g - 1, orows, :] = acc
                dg_ref[g - 1, orows, :] = den
                mg_ref[g - 1, orows, :] = jnp.broadcast_to(m, den.shape)
                return carry
            m4, m16 = mg_ref[0, orows, :], mg_ref[1, orows, :]
            mx = jnp.maximum(jnp.maximum(m4, m16), m)
            e1, e4, e16 = jnp.exp2(m - mx), jnp.exp2(m4 - mx), jnp.exp2(m16 - mx)
            num = e1 * acc + e4 * og_ref[0, orows, :] + e16 * og_ref[1, orows, :]
            tot = e1 * den + e4 * dg_ref[0, orows, :] + e16 * dg_ref[1, orows, :]
            o_ref[0, orows, :] = (num * (1.0 / tot)).astype(o_ref.dtype)
            return carry

        lax.fori_loop(0, dil * nqb, q_block, 0, unroll=ATTN_UNROLL)


def dilated_attention(qkv, slopes):
    b, s, _ = qkv.shape
    h, dh = ATTN_HEADS, ATTN_DH
    assert s % (max(ATTN_DILATIONS) * ATTN_TK) == 0
    return pl.pallas_call(
        functools.partial(_attn_body, s=s),
        grid=(h, b),
        in_specs=[
            pl.BlockSpec((1, s, dh), lambda j, i: (i, 0, j)),
            pl.BlockSpec((1, s, dh), lambda j, i: (i, 0, h + j)),
            pl.BlockSpec((1, s, dh), lambda j, i: (i, 0, 2 * h + j)),
            pl.BlockSpec((1, SUBLANES, LANES), lambda j, i: (j, 0, 0)),
        ],
        out_specs=pl.BlockSpec((1, s, dh), lambda j, i: (i, 0, j)),
        out_shape=jax.ShapeDtypeStruct((b, s, h * dh), BF16),
        scratch_shapes=[
            pltpu.VMEM((len(ATTN_DILATIONS), 3, ATTN_TQ, ATTN_TK), F32),
            pltpu.VMEM((3, s, dh), F32),
            pltpu.VMEM((len(ATTN_DILATIONS) - 1, s, dh), F32),
            pltpu.VMEM((len(ATTN_DILATIONS) - 1, s, dh), F32),
            pltpu.VMEM((len(ATTN_DILATIONS) - 1, s, dh), F32),
        ],
        compiler_params=_cparams("parallel", "arbitrary"),
        name="dilated_attn",
    )(qkv, qkv, qkv, slopes)


MIXER_OUT_TM = 256


FFN_DOWN_TM = 512


def _ffn(x, h, w1, w3, w2, layer, next_norm_g, last):
    u = swiglu_up(h, w1, w3, layer)
    return matmul([(u, w2, 0, layer)], res=x, norm_g=next_norm_g,
                  norm_dtype=F32 if last else BF16, keep_sum=not last,
                  tm=FFN_DOWN_TM, tn=w2.shape[-1], name="ffn_down")


def _even_layer(x, bsz, seq, norm_g, w_in, gate_b, conv_w, conv_b, rg_wa, rg_ba, rg_wx, rg_bx,
                rg_lam, head_g, w_out, next_norm_g):
    heads, dk, dv = MLSTM_HEADS, MLSTM_DK, MLSTM_DV
    n_qkvo = 2 * heads * dk + 2 * heads * dv
    n_gate = 4 * heads
    width = RNN_BLOCKS * RNN_BLOCK
    w_in_t = jnp.transpose(w_in)
    w_main_t = jnp.concatenate([w_in_t[:n_qkvo], w_in_t[n_qkvo + n_gate:]], axis=0).astype(BF16)
    w_gate_t = w_in_t[n_qkvo:n_qkvo + n_gate].astype(BF16)
    z, zg_t = norm_matmul(x, norm_g, w_main_t, w_side_t=w_gate_t, w_transposed=True,
                          name="in_proj")
    z = z.reshape(bsz, seq, -1)

    chunk = MLSTM_CHUNK
    nc = seq // chunk
    gates_rows = jnp.transpose(zg_t[:n_gate].reshape(4, heads, bsz, nc, chunk), (2, 1, 0, 3, 4))
    gate_b_rows = jnp.broadcast_to(
        jnp.transpose(gate_b.astype(F32).reshape(4, heads))[:, :, None, None], (heads, 4, 1, chunk))
    y_a = mlstm_mixer(z, gates_rows, gate_b_rows, head_g.astype(F32).reshape(heads, 1, dv), chunk)

    y_b = rglru_mixer(z, n_qkvo // RNN_BLOCK, (n_qkvo + width) // RNN_BLOCK,
                      conv_w.astype(F32), conv_b.astype(F32).reshape(1, width),
                      (0.5 * rg_wa).astype(BF16), (0.5 * rg_wx).astype(BF16),
                      0.5 * rg_ba.astype(F32), 0.5 * rg_bx.astype(F32), rg_lam.astype(F32))

    w_out_b = w_out.astype(BF16)
    m = bsz * seq
    assert heads * dv == width
    return matmul([(y_a.reshape(m, -1), w_out_b, 0),
                   (y_b.reshape(m, -1), w_out_b, 1)], res=x, norm_g=next_norm_g,
                  tm=MIXER_OUT_TM, tn=w_out.shape[1], name="out_proj")


def _odd_layer(x, h, bsz, seq, norm_g, w_qkv, w_o, next_norm_g):
    if h is None:
        qkv = norm_matmul(x, norm_g, w_qkv.astype(BF16), tm=256, name="qkv_proj")
    else:
        qkv = matmul([(h, w_qkv.astype(BF16))], tn=2048, name="qkv_proj")
    qkv = qkv.reshape(bsz, seq, -1)
    slopes = jnp.exp2(-ALIBI_MAX_BIAS * jnp.arange(1, ATTN_HEADS + 1, dtype=F32) / ATTN_HEADS)
    slopes = jnp.broadcast_to(slopes[:, None, None], (ATTN_HEADS, SUBLANES, LANES))
    o = dilated_attention(qkv, slopes)
    return matmul([(o.reshape(bsz * seq, -1), w_o.astype(BF16))], res=x, norm_g=next_norm_g,
                  tm=MIXER_OUT_TM, tn=w_o.shape[1], name="attn_out_proj")


def kernel(x, e_norm, e_w_in, e_gate_b, e_conv_w, e_conv_b, e_rg_wa, e_rg_ba, e_rg_wx, e_rg_bx,
           e_rg_lam, e_head_g, e_w_out, o_norm, o_w_qkv, o_w_o, f_norm, f_w1, f_w3, f_w2,
           final_norm):
    bsz, seq, d = x.shape
    depth = f_norm.shape[0]
    xs = x.reshape(bsz * seq, d).astype(F32)
    w1, w3, w2 = f_w1.astype(BF16), f_w3.astype(BF16), f_w2.astype(BF16)
    hs = None
    for l in range(depth):
        if l % 2 == 0:
            e = l // 2
            xs, hs = _even_layer(xs, bsz, seq, e_norm[e], e_w_in[e], e_gate_b[e], e_conv_w[e],
                                 e_conv_b[e], e_rg_wa[e], e_rg_ba[e], e_rg_wx[e], e_rg_bx[e],
                                 e_rg_lam[e], e_head_g[e], e_w_out[e], f_norm[l])
        else:
            o = l // 2
            xs, hs = _odd_layer(xs, hs, bsz, seq, o_norm[o], o_w_qkv[o], o_w_o[o], f_norm[l])
        if l == depth - 1:
            out = _ffn(xs, hs, w1, w3, w2, l, final_norm, last=True)
            return out.astype(x.dtype).reshape(bsz, seq, d)
        if (l + 1) % 2:
            xs, hs = _ffn(xs, hs, w1, w3, w2, l, o_norm[(l + 1) // 2], last=False)
        else:
            xs, hs = _ffn(xs, hs, w1, w3, w2, l, None, last=False), None
```

```python
import functools
import math

import jax
import jax.numpy as jnp
from jax import lax
from jax.experimental import pallas as pl
from jax.experimental.pallas import tpu as pltpu

F32 = jnp.float32
BF16 = jnp.bfloat16

RMS_EPS = 1e-6
NEG_BIG = -1e30

MLSTM_HEADS = 4
MLSTM_DK = 128
MLSTM_DV = 256
MLSTM_CHUNK = 256
RNN_BLOCKS = 8
RNN_BLOCK = 128
CONV_WIDTH = 4
CONV_LEFT = 2
RGLRU_C = 8.0
ATTN_HEADS = 16
ATTN_DH = 128
ATTN_HALF = 64
ATTN_DILATIONS = (1, 4, 16)
ALIBI_MAX_BIAS = 8.0

LANES = 128
SUBLANES = 8
VMEM_LIMIT_BYTES = 56 * 1024 * 1024


def _cparams(*semantics):
    return pltpu.CompilerParams(dimension_semantics=semantics,
                                vmem_limit_bytes=VMEM_LIMIT_BYTES)


NORM_ROWS = 128


def _norm_rows_into(x_ref, g_ref, h_ref):
    g = g_ref[...]

    def rows(t, carry):
        sl = pl.ds(pl.multiple_of(t * NORM_ROWS, NORM_ROWS), NORM_ROWS)
        x = x_ref[sl, :]
        ms = jnp.mean(x * x, axis=-1, keepdims=True)
        h_ref[sl, :] = (x * lax.rsqrt(ms + RMS_EPS) * g).astype(h_ref.dtype)
        return carry

    lax.fori_loop(0, x_ref.shape[0] // NORM_ROWS, rows, 0)


def _weight_spec(w, kdim, tn, row_block=0, layer=None, **mode):
    if w.ndim == 2:
        return pl.BlockSpec((kdim, tn), lambda i, j: (row_block, j), **mode)
    return pl.BlockSpec((None, kdim, tn), lambda i, j: (layer, row_block, j), **mode)


def _matmul_body(*refs, n_pairs, has_res, has_norm, keep_sum):
    pairs = [(refs[2 * p], refs[2 * p + 1]) for p in range(n_pairs)]
    rest = list(refs[2 * n_pairs:])
    r_ref = rest.pop(0) if has_res else None
    g_ref = rest.pop(0) if has_norm else None
    y_ref = rest.pop(0) if keep_sum else rest.pop(-1)

    acc = None
    for a_ref, w_ref in pairs:
        d = jnp.dot(a_ref[...], w_ref[...], preferred_element_type=F32)
        acc = d if acc is None else acc + d
    if has_res:
        acc = acc + r_ref[...]
    y_ref[...] = acc.astype(y_ref.dtype)
    if has_norm:
        _norm_rows_into(y_ref, g_ref, rest.pop(0))


def matmul(pairs, res=None, norm_g=None, norm_dtype=BF16, keep_sum=True, tm=1024, tn=1024,
           name="matmul"):
    m = pairs[0][0].shape[0]
    n = pairs[0][1].shape[-1]
    tn = min(tn, n)
    assert m % tm == 0 and n % tn == 0 and (keep_sum or norm_g is not None)
    assert norm_g is None or (tn == n and tm % NORM_ROWS == 0)
    w_mode = dict(pipeline_mode=pl.Buffered(1)) if tn == n else {}
    in_specs, args = [], []
    for a, w, *where in pairs:
        kdim = a.shape[1]
        kb, layer = (list(where) + [0, None])[:2] if where else (0, None)
        in_specs.append(pl.BlockSpec((tm, kdim), lambda i, j: (i, 0)))
        in_specs.append(_weight_spec(w, kdim, tn, kb, layer, **w_mode))
        args += [a, w]
    if res is not None:
        in_specs.append(pl.BlockSpec((tm, tn), lambda i, j: (i, j)))
        args.append(res)
    out_specs, out_shape, scratch = [], [], []
    if keep_sum:
        out_specs.append(pl.BlockSpec((tm, tn), lambda i, j: (i, j)))
        out_shape.append(jax.ShapeDtypeStruct((m, n), F32))
    else:
        scratch.append(pltpu.VMEM((tm, n), F32))
    if norm_g is not None:
        in_specs.append(pl.BlockSpec((1, n), lambda i, j: (0, 0)))
        args.append(norm_g.reshape(1, n).astype(F32))
        out_specs.append(pl.BlockSpec((tm, n), lambda i, j: (i, 0)))
        out_shape.append(jax.ShapeDtypeStruct((m, n), norm_dtype))
    outs = pl.pallas_call(
        functools.partial(_matmul_body, n_pairs=len(pairs), has_res=res is not None,
                          has_norm=norm_g is not None, keep_sum=keep_sum),
        grid=(m // tm, n // tn),
        in_specs=in_specs,
        out_specs=out_specs,
        out_shape=out_shape,
        scratch_shapes=scratch,
        compiler_params=_cparams("parallel", "parallel"),
        name=name,
    )(*args)
    return outs if len(outs) > 1 else outs[0]


SWIGLU_ROWS = 1024


def _swiglu_up_body(a_ref, w1_ref, w3_ref, o_ref):
    rows = min(SWIGLU_ROWS, a_ref.shape[0])
    for r0 in range(0, a_ref.shape[0], rows):
        a = a_ref[pl.ds(r0, rows), :]
        u = jnp.dot(a, w1_ref[...], preferred_element_type=F32)
        v = jnp.dot(a, w3_ref[...], preferred_element_type=F32)
        o_ref[pl.ds(r0, rows), :] = (u * jax.nn.sigmoid(u) * v).astype(o_ref.dtype)


def swiglu_up(a, w1, w3, layer, tm=4096, tn=512):
    m, kdim = a.shape
    n = w1.shape[-1]
    assert m % tm == 0 and n % tn == 0
    return pl.pallas_call(
        _swiglu_up_body,
        grid=(m // tm, n // tn),
        in_specs=[pl.BlockSpec((tm, kdim), lambda i, j: (i, 0)),
                  _weight_spec(w1, kdim, tn, layer=layer),
                  _weight_spec(w3, kdim, tn, layer=layer)],
        out_specs=pl.BlockSpec((tm, tn), lambda i, j: (i, j)),
        out_shape=jax.ShapeDtypeStruct((m, n), BF16),
        compiler_params=_cparams("parallel", "parallel"),
        name="swiglu_up",
    )(a, w1, w3)


NORM_CHUNK = 256


def _norm_matmul_body(*refs, has_side, w_transposed):
    if has_side:
        x_ref, g_ref, w_ref, ws_ref, o_ref, os_ref = refs
    else:
        x_ref, g_ref, w_ref, o_ref = refs
    w_dims = (((1,), (1,)), ((), ())) if w_transposed else (((1,), (0,)), ((), ()))

    g = g_ref[...]
    for r0 in range(0, x_ref.shape[0], NORM_CHUNK):
        sl = pl.ds(r0, NORM_CHUNK)
        x = x_ref[sl, :]
        ms = jnp.mean(x * x, axis=-1, keepdims=True)
        h = (x * lax.rsqrt(ms + RMS_EPS) * g).astype(BF16)
        o_ref[sl, :] = lax.dot_general(h, w_ref[...], w_dims,
                                       preferred_element_type=F32).astype(o_ref.dtype)
        if has_side:
            os_ref[:, sl] = lax.dot_general(ws_ref[...], h, (((1,), (1,)), ((), ())),
                                            preferred_element_type=F32)


def norm_matmul(x, g, w, w_side_t=None, w_transposed=False, out_dtype=F32, tm=512,
                name="norm_matmul"):
    m, d = x.shape
    n = w.shape[0] if w_transposed else w.shape[1]
    assert m % tm == 0 and tm % NORM_CHUNK == 0
    resident = dict(pipeline_mode=pl.Buffered(1))
    in_specs = [pl.BlockSpec((tm, d), lambda i: (i, 0)),
                pl.BlockSpec((1, d), lambda i: (0, 0)),
                pl.BlockSpec(w.shape, lambda i: (0, 0), **resident)]
    args = [x, g.reshape(1, d).astype(F32), w]
    out_specs = [pl.BlockSpec((tm, n), lambda i: (i, 0))]
    out_shape = [jax.ShapeDtypeStruct((m, n), out_dtype)]
    if w_side_t is not None:
        ns = w_side_t.shape[0]
        in_specs.append(pl.BlockSpec((ns, d), lambda i: (0, 0), **resident))
        args.append(w_side_t)
        out_specs.append(pl.BlockSpec((ns, tm), lambda i: (0, i)))
        out_shape.append(jax.ShapeDtypeStruct((ns, m), F32))
    outs = pl.pallas_call(
        functools.partial(_norm_matmul_body, has_side=w_side_t is not None,
                          w_transposed=w_transposed),
        grid=(m // tm,),
        in_specs=in_specs,
        out_specs=out_specs,
        out_shape=out_shape,
        compiler_params=_cparams("parallel"),
        name=name,
    )(*args)
    return outs if w_side_t is not None else outs[0]


def _log_sigmoid(x):
    return jnp.minimum(x, 0.0) - jnp.log1p(jnp.exp(-jnp.abs(x)))


SCAN_RADIX = 4


def _lane_scan(x, combine, identity, reverse):
    width = x.shape[-1]
    lane = lax.broadcasted_iota(jnp.int32, x.shape, 1)

    def shifted(v, sh):
        if reverse:
            return jnp.where(lane < width - sh, pltpu.roll(v, width - sh, axis=1), identity)
        return jnp.where(lane >= sh, pltpu.roll(v, sh, axis=1), identity)

    sh = 1
    while sh < width:
        parts = [shifted(x, k * sh) for k in range(1, SCAN_RADIX) if k * sh < width]
        for p in parts:
            x = combine(x, p)
        sh *= SCAN_RADIX
    return x


def _mlstm_body(q_ref, k_ref, v_ref, og_ref, g_ref, gb_ref, hg_ref, y_ref,
                rows_ref, cols_ref, hf_ref, hb_ref, ct_ref, *, nc, chunk):
    dk, dv = MLSTM_DK, MLSTM_DV
    scale = dk ** -0.5

    g = g_ref[0, 0] + gb_ref[0]
    rows = []
    for d in range(2):
        b = _lane_scan(_log_sigmoid(g[2 * d + 1]), jnp.add, 0.0, reverse=d == 1)
        gg = g[2 * d] - b
        gmax = _lane_scan(gg, jnp.maximum, -jnp.inf, reverse=d == 1)
        rows += [gg, b, gmax]
        for j, r in enumerate((gg, b, gmax)):
            rows_ref[3 * d + j] = r
    rowmat = jnp.concatenate(rows + [jnp.zeros((LANES - 6 * nc, chunk), F32)], axis=0)
    colmat = rowmat.T
    for c in range(nc):
        cols_ref[c] = colmat if c == 0 else pltpu.roll(colmat, LANES - c, axis=1)

    ct_ref[...] = jnp.zeros_like(ct_ref)

    row_id = lax.broadcasted_iota(jnp.int32, (chunk, chunk), 0)
    col_id = lax.broadcasted_iota(jnp.int32, (chunk, chunk), 1)
    causal = (row_id >= col_id, row_id <= col_id)
    ones = jnp.ones((chunk, LANES), BF16)

    def lanes2(x):
        return jnp.concatenate([x] * (dv // LANES), axis=1)

    hg = hg_ref[0]

    def emit(r0, h, own_ref, other_ref):
        rows = pl.ds(r0, chunk)
        if other_ref is None:
            own_ref[rows, :] = h
            return
        hm = h + other_ref[rows, :]
        ms = jnp.mean(hm * hm, axis=-1, keepdims=True)
        hn = hm * lax.rsqrt(ms + RMS_EPS) * hg
        y_ref[0, rows, :] = (hn * jax.nn.sigmoid(og_ref[0, rows, :])).astype(y_ref.dtype)

    def chunk_step(c, m, d, own_ref, other_ref):
        r0 = pl.multiple_of(c * chunk, chunk)
        qb = (q_ref[0, pl.ds(r0, chunk), :] * scale).astype(BF16)
        kb = k_ref[0, pl.ds(r0, chunk), :].astype(BF16)
        vf = v_ref[0, pl.ds(r0, chunk), :]
        cols = cols_ref[c]

        def column(j):
            lane = (3 * d + j) * nc
            return jnp.broadcast_to(cols[:, lane:lane + 1], (chunk, LANES))

        gg_col, b_col, gmax_col = column(0), column(1), column(2)
        gg_row = rows_ref[3 * d, pl.ds(c, 1), :]
        end = chunk - 1 if d == 0 else 0
        total = rows_ref[3 * d + 1, pl.ds(c, 1), :][:, end:end + 1]
        gmax_end = rows_ref[3 * d + 2, pl.ds(c, 1), :][:, end:end + 1]

        m_row = jnp.maximum(gmax_col, m)
        w_intra = jnp.exp(jnp.where(causal[d], gg_row - lanes2(m_row), NEG_BIG))
        w_inter = jnp.exp(m - m_row)
        s = lax.dot_general(qb, kb, (((1,), (1,)), ((), ())),
                            preferred_element_type=F32) * w_intra
        ct = ct_ref[d]
        intra = jnp.dot(s.astype(BF16), jnp.concatenate([vf.astype(BF16), ones], axis=1),
                        preferred_element_type=F32)
        inter = jnp.dot(qb, ct.astype(BF16), preferred_element_type=F32)
        num = intra[:, :dv] + lanes2(w_inter) * inter[:, :dv]
        den = intra[:, dv:] + w_inter * inter[:, dv:]
        inv = 1.0 / jnp.maximum(jnp.abs(den), jnp.exp(-(b_col + m_row)))
        emit(r0, num * lanes2(inv), own_ref, other_ref)

        m_new = jnp.maximum(total + m, total + gmax_end)
        w_src = jnp.exp(total + gg_col - m_new)
        decay = jnp.exp(total + m - m_new)
        wv = jnp.concatenate([lanes2(w_src) * vf, w_src], axis=1).astype(BF16)
        ct_ref[d] = decay * ct + lax.dot_general(
            kb, wv, (((0,), (0,)), ((), ())), preferred_element_type=F32)
        return m_new

    def first_half(c, carry):
        m_f, m_b = carry
        m_f = chunk_step(c, m_f, 0, hf_ref, None)
        m_b = chunk_step(nc - 1 - c, m_b, 1, hb_ref, None)
        return m_f, m_b

    def second_half(c, carry):
        m_f, m_b = carry
        m_f = chunk_step(c, m_f, 0, None, hb_ref)
        m_b = chunk_step(nc - 1 - c, m_b, 1, None, hf_ref)
        return m_f, m_b

    m0 = jnp.full((1, 1), NEG_BIG, F32)
    carry = lax.fori_loop(0, nc // 2, first_half, (m0, m0), unroll=2)
    lax.fori_loop(nc // 2, nc, second_half, carry, unroll=2)


def mlstm_mixer(z, gates_rows, gate_b_rows, head_g, chunk=MLSTM_CHUNK):
    b, s, _ = z.shape
    h, dk, dv = MLSTM_HEADS, MLSTM_DK, MLSTM_DV
    nc = s // chunk
    assert s % chunk == 0 and 6 * nc <= LANES and chunk % LANES == 0 and nc % 2 == 0
    kq = h * dk // dk
    kv = 2 * h * dk // dv
    ko = kv + h
    return pl.pallas_call(
        functools.partial(_mlstm_body, nc=nc, chunk=chunk),
        grid=(b, h),
        in_specs=[
            pl.BlockSpec((1, s, dk), lambda i, j: (i, 0, j)),
            pl.BlockSpec((1, s, dk), lambda i, j: (i, 0, kq + j)),
            pl.BlockSpec((1, s, dv), lambda i, j: (i, 0, kv + j)),
            pl.BlockSpec((1, s, dv), lambda i, j: (i, 0, ko + j)),
            pl.BlockSpec((1, 1, 4, nc, chunk), lambda i, j: (i, j, 0, 0, 0)),
            pl.BlockSpec((1, 4, 1, chunk), lambda i, j: (j, 0, 0, 0)),
            pl.BlockSpec((1, 1, dv), lambda i, j: (j, 0, 0)),
        ],
        out_specs=pl.BlockSpec((1, s, dv), lambda i, j: (i, 0, j)),
        out_shape=jax.ShapeDtypeStruct((b, s, h * dv), BF16),
        scratch_shapes=[
            pltpu.VMEM((6, nc, chunk), F32),
            pltpu.VMEM((nc, chunk, LANES), F32),
            pltpu.VMEM((s, dv), F32),
            pltpu.VMEM((s, dv), F32),
            pltpu.VMEM((2, dk, dv + LANES), F32),
        ],
        compiler_params=_cparams("parallel", "parallel"),
        name="mlstm",
    )(z, z, z, z, gates_rows, gate_b_rows, head_g)


RG_TILE = 512
RG_PAD = SUBLANES
RG_SCAN_UNROLL = 8


def _softplus(x):
    return jnp.maximum(x, 0.0) + jnp.log1p(jnp.exp(-jnp.abs(x)))


def _gelu_tanh(x):
    c = math.sqrt(2.0 / math.pi)
    return x * (0.5 * (1.0 + jnp.tanh(c * (x + 0.044715 * (x * x * x)))))


def _rglru_body(x_ref, gr_ref, cw_ref, cb_ref, wa_ref, wx_ref, ba_ref, bx_ref, lam_ref, y_ref,
                xpad_ref, a_ref, u_ref, as_ref, us_ref, h_ref, *, s):
    nt = s // RG_TILE
    zeros = jnp.zeros((RG_PAD, LANES), F32)
    xpad_ref[pl.ds(0, RG_PAD), :] = zeros
    xpad_ref[pl.ds(RG_PAD + s, RG_PAD), :] = zeros

    def copy_in(t, carry):
        r0 = pl.multiple_of(t * RG_TILE, RG_TILE)
        xpad_ref[pl.ds(RG_PAD + r0, RG_TILE), :] = x_ref[0, pl.ds(r0, RG_TILE), :]
        return carry

    lax.fori_loop(0, nt, copy_in, 0)

    cw = cw_ref[...]
    cb = cb_ref[...]
    half_sp = [0.5 * RGLRU_C * _softplus(-lam_ref[d:d + 1, :]) for d in range(2)]

    def gates(t, carry):
        r0 = pl.multiple_of(t * RG_TILE, RG_TILE)
        xc = None
        for j in range(CONV_WIDTH):
            tap = xpad_ref[pl.ds(r0 + RG_PAD - CONV_LEFT + j, RG_TILE), :] * cw[j:j + 1, :]
            xc = tap if xc is None else xc + tap
        xc = xc + cb
        xcb = xc.astype(BF16)
        half_xc = 0.5 * xc
        for d in range(2):
            t_r = jnp.tanh(jnp.dot(xcb, wa_ref[d, 0], preferred_element_type=F32)
                           + ba_ref[d:d + 1, :])
            t_i = jnp.tanh(jnp.dot(xcb, wx_ref[d, 0], preferred_element_type=F32)
                           + bx_ref[d:d + 1, :])
            neg_log_a = half_sp[d] * t_r + half_sp[d]
            gated_x = half_xc * t_i + half_xc
            a = jnp.exp(-neg_log_a)
            one_minus_a2 = jnp.tanh(neg_log_a) * (1.0 + a * a)
            root = jnp.where(one_minus_a2 > 0.0, one_minus_a2 * lax.rsqrt(one_minus_a2), 0.0)
            a_ref[d, pl.ds(r0, RG_TILE), :] = a
            u_ref[d, pl.ds(r0, RG_TILE), :] = root * gated_x
        return carry

    def block_scan(t, carry):
        r0 = pl.multiple_of(t * RG_TILE, RG_TILE)
        for d in range(2):
            a_prev = u_prev = None
            for r in (range(SUBLANES) if d == 0 else reversed(range(SUBLANES))):
                rows = pl.ds(r0 + r, RG_TILE // SUBLANES, stride=SUBLANES)
                a, u = a_ref[d, rows, :], u_ref[d, rows, :]
                if a_prev is not None:
                    u = a * u_prev + u
                    a = a * a_prev
                as_ref[d, rows, :] = a
                us_ref[d, rows, :] = u
                a_prev, u_prev = a, u
        return carry

    lax.fori_loop(0, nt, gates, 0)
    lax.fori_loop(0, nt, block_scan, 0)

    nblk = s // SUBLANES

    def scan(j, carry):
        h_f, h_b = carry
        for d, last, h in ((0, SUBLANES - 1, h_f), (1, 0, h_b)):
            blk = j if d == 0 else nblk - 1 - j
            rows = pl.ds(pl.multiple_of(blk * SUBLANES, SUBLANES), SUBLANES)
            a, u = as_ref[d, rows, :], us_ref[d, rows, :]
            h_ref[d, rows, :] = a * h + u
            a_end = jnp.broadcast_to(a[last:last + 1, :], (SUBLANES, LANES))
            u_end = jnp.broadcast_to(u[last:last + 1, :], (SUBLANES, LANES))
            if d == 0:
                h_f = a_end * h + u_end
            else:
                h_b = a_end * h + u_end
        return h_f, h_b

    h0 = jnp.zeros((SUBLANES, LANES), F32)
    lax.fori_loop(0, nblk, scan, (h0, h0), unroll=RG_SCAN_UNROLL)

    def finish(t, carry):
        r0 = pl.multiple_of(t * RG_TILE, RG_TILE)
        hr = h_ref[0, pl.ds(r0, RG_TILE), :] + h_ref[1, pl.ds(r0, RG_TILE), :]
        y_ref[0, pl.ds(r0, RG_TILE), :] = (
            hr * _gelu_tanh(gr_ref[0, pl.ds(r0, RG_TILE), :])).astype(y_ref.dtype)
        return carry

    lax.fori_loop(0, nt, finish, 0)


def rglru_mixer(z, xr_block0, gr_block0, conv_w, conv_b, wa, wx, ba, bx, lam):
    b, s, _ = z.shape
    width = RNN_BLOCKS * RNN_BLOCK
    assert s % RG_TILE == 0
    return pl.pallas_call(
        functools.partial(_rglru_body, s=s),
        grid=(b, RNN_BLOCKS),
        in_specs=[
            pl.BlockSpec((1, s, RNN_BLOCK), lambda i, j: (i, 0, xr_block0 + j)),
            pl.BlockSpec((1, s, RNN_BLOCK), lambda i, j: (i, 0, gr_block0 + j)),
            pl.BlockSpec((CONV_WIDTH, RNN_BLOCK), lambda i, j: (0, j)),
            pl.BlockSpec((1, RNN_BLOCK), lambda i, j: (0, j)),
            pl.BlockSpec((2, 1, RNN_BLOCK, RNN_BLOCK), lambda i, j: (0, j, 0, 0)),
            pl.BlockSpec((2, 1, RNN_BLOCK, RNN_BLOCK), lambda i, j: (0, j, 0, 0)),
            pl.BlockSpec((2, RNN_BLOCK), lambda i, j: (0, j)),
            pl.BlockSpec((2, RNN_BLOCK), lambda i, j: (0, j)),
            pl.BlockSpec((2, RNN_BLOCK), lambda i, j: (0, j)),
        ],
        out_specs=pl.BlockSpec((1, s, RNN_BLOCK), lambda i, j: (i, 0, j)),
        out_shape=jax.ShapeDtypeStruct((b, s, width), BF16),
        scratch_shapes=[
            pltpu.VMEM((s + 2 * RG_PAD, RNN_BLOCK), F32),
            pltpu.VMEM((2, s, RNN_BLOCK), F32),
            pltpu.VMEM((2, s, RNN_BLOCK), F32),
            pltpu.VMEM((2, s, RNN_BLOCK), F32),
            pltpu.VMEM((2, s, RNN_BLOCK), F32),
            pltpu.VMEM((2, s, RNN_BLOCK), F32),
        ],
        compiler_params=_cparams("parallel", "parallel"),
        name="rglru",
    )(z, z, conv_w, conv_b, wa, wx, ba, bx, lam)


ATTN_TQ = 128
ATTN_TK = ATTN_TQ + 2 * ATTN_HALF
ATTN_UNROLL = 32
LOG2E = math.log2(math.e)


def _attn_body(q_ref, k_ref, v_ref, slope_ref, o_ref, bias_ref, x4_ref, og_ref, dg_ref, mg_ref,
               *, s):
    qscale = ATTN_DH ** -0.5 * LOG2E

    @pl.when(pl.program_id(1) == 0)
    def _():
        slope = slope_ref[0, 0:1, 0:1] * LOG2E
        qi = lax.broadcasted_iota(jnp.int32, (ATTN_TQ, ATTN_TK), 0)
        kj = lax.broadcasted_iota(jnp.int32, (ATTN_TQ, ATTN_TK), 1)
        for g, dil in enumerate(ATTN_DILATIONS):
            for e in range(3):
                rel = jnp.abs(kj - e * ATTN_HALF - qi)
                pen = slope * (rel * dil).astype(F32)
                bias_ref[g, e] = jnp.where(rel <= ATTN_HALF, -pen, NEG_BIG)

    s4 = s // 4
    srcs = (q_ref, k_ref, v_ref)

    def split4(t, carry):
        c = t // (s4 // ATTN_TK)
        p0 = (t % (s4 // ATTN_TK)) * ATTN_TK
        dst = pl.ds(pl.multiple_of(c * s4 + p0, ATTN_TK), ATTN_TK)
        for a in range(3):
            x = srcs[a][0, pl.ds(c + 4 * p0, ATTN_TK, stride=4), :]
            x4_ref[a, dst, :] = x * qscale if a == 0 else x
        return carry

    lax.fori_loop(0, 4 * (s4 // ATTN_TK), split4, 0)

    ones = jnp.ones((ATTN_TK, ATTN_DH), BF16)

    for g, dil in reversed(list(enumerate(ATTN_DILATIONS))):
        sp = s // dil
        nqb = sp // ATTN_TQ

        def q_block(t, carry, g=g, dil=dil, sp=sp, nqb=nqb):
            r = t // nqb
            p0 = (t % nqb) * ATTN_TQ
            kstart = jnp.clip(p0 - ATTN_HALF, 0, sp - ATTN_TK)
            e = (p0 - kstart) // ATTN_HALF
            if dil == 1:
                qrows = pl.ds(pl.multiple_of(p0, ATTN_TQ), ATTN_TQ)
                krows = pl.ds(pl.multiple_of(kstart, ATTN_HALF), ATTN_TK)
                qf = q_ref[0, qrows, :] * qscale
                kf, vf = k_ref[0, krows, :], v_ref[0, krows, :]
                orows = qrows
            elif dil == 4:
                qrows = pl.ds(pl.multiple_of(r * s4 + p0, ATTN_TQ), ATTN_TQ)
                krows = pl.ds(pl.multiple_of(r * s4 + kstart, ATTN_HALF), ATTN_TK)
                qf, kf, vf = x4_ref[0, qrows, :], x4_ref[1, krows, :], x4_ref[2, krows, :]
                orows = pl.ds(r + 4 * p0, ATTN_TQ, stride=4)
            else:
                base = (r % 4) * s4 + r // 4
                qrows = pl.ds(base + 4 * p0, ATTN_TQ, stride=4)
                krows = pl.ds(base + 4 * kstart, ATTN_TK, stride=4)
                qf, kf, vf = x4_ref[0, qrows, :], x4_ref[1, krows, :], x4_ref[2, krows, :]
                orows = pl.ds(r + dil * p0, ATTN_TQ, stride=dil)
            qb, kb, vb = qf.astype(BF16), kf.astype(BF16), vf.astype(BF16)
            sc = lax.dot_general(qb, kb, (((1,), (1,)), ((), ())),
                                 preferred_element_type=F32) + bias_ref[g, e]
            m = jnp.max(sc, axis=-1, keepdims=True)
            p = jnp.exp2(sc - m).astype(BF16)
            pv = jnp.dot(p, jnp.concatenate([vb, ones], axis=1), preferred_element_type=F32)
            acc, den = pv[:, :ATTN_DH], pv[:, ATTN_DH:]
            if dil != 1:
                og_ref[g - 1, orows, :] = acc
                dg_ref[g - 1, orows, :] = den
                mg_ref[g - 1, orows, :] = jnp.broadcast_to(m, den.shape)
                return carry
            m4, m16 = mg_ref[0, orows, :], mg_ref[1, orows, :]
            mx = jnp.maximum(jnp.maximum(m4, m16), m)
            e1, e4, e16 = jnp.exp2(m - mx), jnp.exp2(m4 - mx), jnp.exp2(m16 - mx)
            num = e1 * acc + e4 * og_ref[0, orows, :] + e16 * og_ref[1, orows, :]
            tot = e1 * den + e4 * dg_ref[0, orows, :] + e16 * dg_ref[1, orows, :]
            o_ref[0, orows, :] = (num * (1.0 / tot)).astype(o_ref.dtype)
            return carry

        lax.fori_loop(0, dil * nqb, q_block, 0, unroll=ATTN_UNROLL)


def dilated_attention(qkv, slopes):
    b, s, _ = qkv.shape
    h, dh = ATTN_HEADS, ATTN_DH
    assert s % (max(ATTN_DILATIONS) * ATTN_TK) == 0
    return pl.pallas_call(
        functools.partial(_attn_body, s=s),
        grid=(h, b),
        in_specs=[
            pl.BlockSpec((1, s, dh), lambda j, i: (i, 0, j)),
            pl.BlockSpec((1, s, dh), lambda j, i: (i, 0, h + j)),
            pl.BlockSpec((1, s, dh), lambda j, i: (i, 0, 2 * h + j)),
            pl.BlockSpec((1, SUBLANES, LANES), lambda j, i: (j, 0, 0)),
        ],
        out_specs=pl.BlockSpec((1, s, dh), lambda j, i: (i, 0, j)),
        out_shape=jax.ShapeDtypeStruct((b, s, h * dh), BF16),
        scratch_shapes=[
            pltpu.VMEM((len(ATTN_DILATIONS), 3, ATTN_TQ, ATTN_TK), F32),
            pltpu.VMEM((3, s, dh), F32),
            pltpu.VMEM((len(ATTN_DILATIONS) - 1, s, dh), F32),
            pltpu.VMEM((len(ATTN_DILATIONS) - 1, s, dh), F32),
            pltpu.VMEM((len(ATTN_DILATIONS) - 1, s, dh), F32),
        ],
        compiler_params=_cparams("parallel", "arbitrary"),
        name="dilated_attn",
    )(qkv, qkv, qkv, slopes)


MIXER_OUT_TM = 512


FFN_DOWN_TM = 512


def _ffn(x, h, w1, w3, w2, layer, next_norm_g, last):
    u = swiglu_up(h, w1, w3, layer)
    return matmul([(u, w2, 0, layer)], res=x, norm_g=next_norm_g,
                  norm_dtype=F32 if last else BF16, keep_sum=not last,
                  tm=FFN_DOWN_TM, tn=w2.shape[-1], name="ffn_down")


def _even_layer(x, bsz, seq, norm_g, w_in, gate_b, conv_w, conv_b, rg_wa, rg_ba, rg_wx, rg_bx,
                rg_lam, head_g, w_out, next_norm_g):
    heads, dk, dv = MLSTM_HEADS, MLSTM_DK, MLSTM_DV
    n_qkvo = 2 * heads * dk + 2 * heads * dv
    n_gate = 4 * heads
    width = RNN_BLOCKS * RNN_BLOCK
    w_in_t = jnp.transpose(w_in)
    w_main_t = jnp.concatenate([w_in_t[:n_qkvo], w_in_t[n_qkvo + n_gate:]], axis=0).astype(BF16)
    w_gate_t = w_in_t[n_qkvo:n_qkvo + n_gate].astype(BF16)
    z, zg_t = norm_matmul(x, norm_g, w_main_t, w_side_t=w_gate_t, w_transposed=True,
                          name="in_proj")
    z = z.reshape(bsz, seq, -1)

    chunk = MLSTM_CHUNK
    nc = seq // chunk
    gates_rows = jnp.transpose(zg_t[:n_gate].reshape(4, heads, bsz, nc, chunk), (2, 1, 0, 3, 4))
    gate_b_rows = jnp.broadcast_to(
        jnp.transpose(gate_b.astype(F32).reshape(4, heads))[:, :, None, None], (heads, 4, 1, chunk))
    y_a = mlstm_mixer(z, gates_rows, gate_b_rows, head_g.astype(F32).reshape(heads, 1, dv), chunk)

    y_b = rglru_mixer(z, n_qkvo // RNN_BLOCK, (n_qkvo + width) // RNN_BLOCK,
                      conv_w.astype(F32), conv_b.astype(F32).reshape(1, width),
                      (0.5 * rg_wa).astype(BF16), (0.5 * rg_wx).astype(BF16),
                      0.5 * rg_ba.astype(F32), 0.5 * rg_bx.astype(F32), rg_lam.astype(F32))

    w_out_b = w_out.astype(BF16)
    m = bsz * seq
    assert heads * dv == width
    return matmul([(y_a.reshape(m, -1), w_out_b, 0),
                   (y_b.reshape(m, -1), w_out_b, 1)], res=x, norm_g=next_norm_g,
                  tm=MIXER_OUT_TM, tn=w_out.shape[1], name="out_proj")


def _odd_layer(x, h, bsz, seq, norm_g, w_qkv, w_o, next_norm_g):
    if h is None:
        qkv = norm_matmul(x, norm_g, w_qkv.astype(BF16), tm=256, name="qkv_proj")
    else:
        qkv = matmul([(h, w_qkv.astype(BF16))], tm=512, tn=w_qkv.shape[1], name="qkv_proj")
    qkv = qkv.reshape(bsz, seq, -1)
    slopes = jnp.exp2(-ALIBI_MAX_BIAS * jnp.arange(1, ATTN_HEADS + 1, dtype=F32) / ATTN_HEADS)
    slopes = jnp.broadcast_to(slopes[:, None, None], (ATTN_HEADS, SUBLANES, LANES))
    o = dilated_attention(qkv, slopes)
    return matmul([(o.reshape(bsz * seq, -1), w_o.astype(BF16))], res=x, norm_g=next_norm_g,
                  tm=MIXER_OUT_TM, tn=w_o.shape[1], name="attn_out_proj")


def kernel(x, e_norm, e_w_in, e_gate_b, e_conv_w, e_conv_b, e_rg_wa, e_rg_ba, e_rg_wx, e_rg_bx,
           e_rg_lam, e_head_g, e_w_out, o_norm, o_w_qkv, o_w_o, f_norm, f_w1, f_w3, f_w2,
           final_norm):
    bsz, seq, d = x.shape
    depth = f_norm.shape[0]
    xs = x.reshape(bsz * seq, d).astype(F32)
    w1, w3, w2 = f_w1.astype(BF16), f_w3.astype(BF16), f_w2.astype(BF16)
    hs = None
    for l in range(depth):
        if l % 2 == 0:
            e = l // 2
            xs, hs = _even_layer(xs, bsz, seq, e_norm[e], e_w_in[e], e_gate_b[e], e_conv_w[e],
                                 e_conv_b[e], e_rg_wa[e], e_rg_ba[e], e_rg_wx[e], e_rg_bx[e],
                                 e_rg_lam[e], e_head_g[e], e_w_out[e], f_norm[l])
        else:
            o = l // 2
            xs, hs = _odd_layer(xs, hs, bsz, seq, o_norm[o], o_w_qkv[o], o_w_o[o], f_norm[l])
        if l == depth - 1:
            out = _ffn(xs, hs, w1, w3, w2, l, final_norm, last=True)
            return out.astype(x.dtype).reshape(bsz, seq, d)
        if (l + 1) % 2:
            xs, hs = _ffn(xs, hs, w1, w3, w2, l, o_norm[(l + 1) // 2], last=False)
        else:
            xs, hs = _ffn(xs, hs, w1, w3, w2, l, None, last=False), None
```

```python
import functools
import math

import jax
import jax.numpy as jnp
from jax import lax
from jax.experimental import pallas as pl
from jax.experimental.pallas import tpu as pltpu

F32 = jnp.float32
BF16 = jnp.bfloat16

RMS_EPS = 1e-6
NEG_BIG = -1e30

MLSTM_HEADS = 4
MLSTM_DK = 128
MLSTM_DV = 256
MLSTM_CHUNK = 256
RNN_BLOCKS = 8
RNN_BLOCK = 128
CONV_WIDTH = 4
CONV_LEFT = 2
RGLRU_C = 8.0
ATTN_HEADS = 16
ATTN_DH = 128
ATTN_HALF = 64
ATTN_DILATIONS = (1, 4, 16)
ALIBI_MAX_BIAS = 8.0

LANES = 128
SUBLANES = 8
VMEM_LIMIT_BYTES = 56 * 1024 * 1024


def _cparams(*semantics):
    return pltpu.CompilerParams(dimension_semantics=semantics,
                                vmem_limit_bytes=VMEM_LIMIT_BYTES)


NORM_CHUNK = 256


def _weight_spec(w, kdim, tn, row_block=0, layer=None, **mode):
    if w.ndim == 2:
        return pl.BlockSpec((kdim, tn), lambda i, j: (row_block, j), **mode)
    return pl.BlockSpec((None, kdim, tn), lambda i, j: (layer, row_block, j), **mode)


def _matmul_body(*refs, n_pairs, has_res, has_norm, keep_sum):
    pairs = [(refs[2 * p], refs[2 * p + 1]) for p in range(n_pairs)]
    rest = list(refs[2 * n_pairs:])
    r_ref = rest.pop(0) if has_res else None
    g_ref = rest.pop(0) if has_norm else None
    y_ref = rest.pop(0) if keep_sum else None
    h_ref = rest.pop(0) if has_norm else None

    tm = pairs[0][0].shape[0]
    rows = min(NORM_CHUNK, tm) if has_norm else tm
    for r0 in range(0, tm, rows):
        sl = pl.ds(r0, rows)
        acc = None
        for a_ref, w_ref in pairs:
            d = jnp.dot(a_ref[sl, :], w_ref[...], preferred_element_type=F32)
            acc = d if acc is None else acc + d
        if has_res:
            acc = acc + r_ref[sl, :]
        if keep_sum:
            y_ref[sl, :] = acc
        if has_norm:
            ms = jnp.mean(acc * acc, axis=-1, keepdims=True)
            h_ref[sl, :] = (acc * lax.rsqrt(ms + RMS_EPS) * g_ref[...]).astype(h_ref.dtype)


def matmul(pairs, res=None, norm_g=None, norm_dtype=BF16, keep_sum=True, tm=1024, tn=1024,
           name="matmul"):
    m = pairs[0][0].shape[0]
    n = pairs[0][1].shape[-1]
    tn = min(tn, n)
    assert m % tm == 0 and n % tn == 0 and (keep_sum or norm_g is not None)
    assert norm_g is None or (tn == n and tm % NORM_CHUNK == 0)
    w_mode = dict(pipeline_mode=pl.Buffered(1)) if tn == n else {}
    in_specs, args = [], []
    for a, w, *where in pairs:
        kdim = a.shape[1]
        kb, layer = (list(where) + [0, None])[:2] if where else (0, None)
        in_specs.append(pl.BlockSpec((tm, kdim), lambda i, j: (i, 0)))
        in_specs.append(_weight_spec(w, kdim, tn, kb, layer, **w_mode))
        args += [a, w]
    if res is not None:
        in_specs.append(pl.BlockSpec((tm, tn), lambda i, j: (i, j)))
        args.append(res)
    out_specs, out_shape = [], []
    if keep_sum:
        out_specs.append(pl.BlockSpec((tm, tn), lambda i, j: (i, j)))
        out_shape.append(jax.ShapeDtypeStruct((m, n), F32))
    if norm_g is not None:
        in_specs.append(pl.BlockSpec((1, n), lambda i, j: (0, 0)))
        args.append(norm_g.reshape(1, n).astype(F32))
        out_specs.append(pl.BlockSpec((tm, n), lambda i, j: (i, 0)))
        out_shape.append(jax.ShapeDtypeStruct((m, n), norm_dtype))
    outs = pl.pallas_call(
        functools.partial(_matmul_body, n_pairs=len(pairs), has_res=res is not None,
                          has_norm=norm_g is not None, keep_sum=keep_sum),
        grid=(m // tm, n // tn),
        in_specs=in_specs,
        out_specs=out_specs,
        out_shape=out_shape,
        compiler_params=_cparams("parallel", "parallel"),
        name=name,
    )(*args)
    return outs if len(outs) > 1 else outs[0]


SWIGLU_ROWS = 1024


def _swiglu_up_body(a_ref, w1_ref, w3_ref, o_ref):
    rows = min(SWIGLU_ROWS, a_ref.shape[0])
    for r0 in range(0, a_ref.shape[0], rows):
        a = a_ref[pl.ds(r0, rows), :]
        u = jnp.dot(a, w1_ref[...], preferred_element_type=F32)
        v = jnp.dot(a, w3_ref[...], preferred_element_type=F32)
        o_ref[pl.ds(r0, rows), :] = (u * jax.nn.sigmoid(u) * v).astype(o_ref.dtype)


def swiglu_up(a, w1, w3, layer, tm=4096, tn=512):
    m, kdim = a.shape
    n = w1.shape[-1]
    assert m % tm == 0 and n % tn == 0
    return pl.pallas_call(
        _swiglu_up_body,
        grid=(m // tm, n // tn),
        in_specs=[pl.BlockSpec((tm, kdim), lambda i, j: (i, 0)),
                  _weight_spec(w1, kdim, tn, layer=layer),
                  _weight_spec(w3, kdim, tn, layer=layer)],
        out_specs=pl.BlockSpec((tm, tn), lambda i, j: (i, j)),
        out_shape=jax.ShapeDtypeStruct((m, n), BF16),
        compiler_params=_cparams("parallel", "parallel"),
        name="swiglu_up",
    )(a, w1, w3)


def _norm_matmul_body(*refs, has_side, w_transposed):
    if has_side:
        x_ref, g_ref, w_ref, ws_ref, o_ref, os_ref = refs
    else:
        x_ref, g_ref, w_ref, o_ref = refs
    w_dims = (((1,), (1,)), ((), ())) if w_transposed else (((1,), (0,)), ((), ()))

    g = g_ref[...]
    for r0 in range(0, x_ref.shape[0], NORM_CHUNK):
        sl = pl.ds(r0, NORM_CHUNK)
        x = x_ref[sl, :]
        ms = jnp.mean(x * x, axis=-1, keepdims=True)
        h = (x * lax.rsqrt(ms + RMS_EPS) * g).astype(BF16)
        o_ref[sl, :] = lax.dot_general(h, w_ref[...], w_dims,
                                       preferred_element_type=F32).astype(o_ref.dtype)
        if has_side:
            os_ref[:, sl] = lax.dot_general(ws_ref[...], h, (((1,), (1,)), ((), ())),
                                            preferred_element_type=F32)


def norm_matmul(x, g, w, w_side_t=None, w_transposed=False, out_dtype=F32, tm=512,
                name="norm_matmul"):
    m, d = x.shape
    n = w.shape[0] if w_transposed else w.shape[1]
    assert m % tm == 0 and tm % NORM_CHUNK == 0
    resident = dict(pipeline_mode=pl.Buffered(1))
    in_specs = [pl.BlockSpec((tm, d), lambda i: (i, 0)),
                pl.BlockSpec((1, d), lambda i: (0, 0)),
                pl.BlockSpec(w.shape, lambda i: (0, 0), **resident)]
    args = [x, g.reshape(1, d).astype(F32), w]
    out_specs = [pl.BlockSpec((tm, n), lambda i: (i, 0))]
    out_shape = [jax.ShapeDtypeStruct((m, n), out_dtype)]
    if w_side_t is not None:
        ns = w_side_t.shape[0]
        in_specs.append(pl.BlockSpec((ns, d), lambda i: (0, 0), **resident))
        args.append(w_side_t)
        out_specs.append(pl.BlockSpec((ns, tm), lambda i: (0, i)))
        out_shape.append(jax.ShapeDtypeStruct((ns, m), F32))
    outs = pl.pallas_call(
        functools.partial(_norm_matmul_body, has_side=w_side_t is not None,
                          w_transposed=w_transposed),
        grid=(m // tm,),
        in_specs=in_specs,
        out_specs=out_specs,
        out_shape=out_shape,
        compiler_params=_cparams("parallel"),
        name=name,
    )(*args)
    return outs if w_side_t is not None else outs[0]


def _log_sigmoid(x):
    return jnp.minimum(x, 0.0) - jnp.log1p(jnp.exp(-jnp.abs(x)))


SCAN_RADIX = 4


def _lane_scan(x, combine, identity, reverse):
    width = x.shape[-1]
    lane = lax.broadcasted_iota(jnp.int32, x.shape, 1)

    def shifted(v, sh):
        if reverse:
            return jnp.where(lane < width - sh, pltpu.roll(v, width - sh, axis=1), identity)
        return jnp.where(lane >= sh, pltpu.roll(v, sh, axis=1), identity)

    sh = 1
    while sh < width:
        parts = [shifted(x, k * sh) for k in range(1, SCAN_RADIX) if k * sh < width]
        for p in parts:
            x = combine(x, p)
        sh *= SCAN_RADIX
    return x


def _mlstm_body(q_ref, k_ref, v_ref, og_ref, g_ref, gb_ref, hg_ref, y_ref,
                rows_ref, cols_ref, hf_ref, hb_ref, ct_ref, *, nc, chunk):
    dk, dv = MLSTM_DK, MLSTM_DV
    scale = dk ** -0.5

    g = g_ref[0, 0] + gb_ref[0]
    rows = []
    for d in range(2):
        b = _lane_scan(_log_sigmoid(g[2 * d + 1]), jnp.add, 0.0, reverse=d == 1)
        gg = g[2 * d] - b
        gmax = _lane_scan(gg, jnp.maximum, -jnp.inf, reverse=d == 1)
        rows += [gg, b, gmax]
        for j, r in enumerate((gg, b, gmax)):
            rows_ref[3 * d + j] = r
    rowmat = jnp.concatenate(rows + [jnp.zeros((LANES - 6 * nc, chunk), F32)], axis=0)
    colmat = rowmat.T
    for c in range(nc):
        cols_ref[c] = colmat if c == 0 else pltpu.roll(colmat, LANES - c, axis=1)

    ct_ref[...] = jnp.zeros_like(ct_ref)

    row_id = lax.broadcasted_iota(jnp.int32, (chunk, chunk), 0)
    col_id = lax.broadcasted_iota(jnp.int32, (chunk, chunk), 1)
    causal = (row_id >= col_id, row_id <= col_id)
    ones = jnp.ones((chunk, LANES), BF16)

    def lanes2(x):
        return jnp.concatenate([x] * (dv // LANES), axis=1)

    hg = hg_ref[0]

    def emit(r0, h, own_ref, other_ref):
        rows = pl.ds(r0, chunk)
        if other_ref is None:
            own_ref[rows, :] = h
            return
        hm = h + other_ref[rows, :]
        ms = jnp.mean(hm * hm, axis=-1, keepdims=True)
        hn = hm * lax.rsqrt(ms + RMS_EPS) * hg
        y_ref[0, rows, :] = (hn * jax.nn.sigmoid(og_ref[0, rows, :])).astype(y_ref.dtype)

    def chunk_step(c, m, d, own_ref, other_ref):
        r0 = pl.multiple_of(c * chunk, chunk)
        qb = (q_ref[0, pl.ds(r0, chunk), :] * scale).astype(BF16)
        kb = k_ref[0, pl.ds(r0, chunk), :].astype(BF16)
        vf = v_ref[0, pl.ds(r0, chunk), :]
        cols = cols_ref[c]

        def column(j):
            lane = (3 * d + j) * nc
            return jnp.broadcast_to(cols[:, lane:lane + 1], (chunk, LANES))

        gg_col, b_col, gmax_col = column(0), column(1), column(2)
        gg_row = rows_ref[3 * d, pl.ds(c, 1), :]
        end = chunk - 1 if d == 0 else 0
        total = rows_ref[3 * d + 1, pl.ds(c, 1), :][:, end:end + 1]
        gmax_end = rows_ref[3 * d + 2, pl.ds(c, 1), :][:, end:end + 1]

        m_row = jnp.maximum(gmax_col, m)
        w_intra = jnp.exp(jnp.where(causal[d], gg_row - lanes2(m_row), NEG_BIG))
        w_inter = jnp.exp(m - m_row)
        s = lax.dot_general(qb, kb, (((1,), (1,)), ((), ())),
                            preferred_element_type=F32) * w_intra
        ct = ct_ref[d]
        intra = jnp.dot(s.astype(BF16), jnp.concatenate([vf.astype(BF16), ones], axis=1),
                        preferred_element_type=F32)
        inter = jnp.dot(qb, ct.astype(BF16), preferred_element_type=F32)
        num = intra[:, :dv] + lanes2(w_inter) * inter[:, :dv]
        den = intra[:, dv:] + w_inter * inter[:, dv:]
        inv = 1.0 / jnp.maximum(jnp.abs(den), jnp.exp(-(b_col + m_row)))
        emit(r0, num * lanes2(inv), own_ref, other_ref)

        m_new = jnp.maximum(total + m, total + gmax_end)
        w_src = jnp.exp(total + gg_col - m_new)
        decay = jnp.exp(total + m - m_new)
        wv = jnp.concatenate([lanes2(w_src) * vf, w_src], axis=1).astype(BF16)
        ct_ref[d] = decay * ct + lax.dot_general(
            kb, wv, (((0,), (0,)), ((), ())), preferred_element_type=F32)
        return m_new

    def first_half(c, carry):
        m_f, m_b = carry
        m_f = chunk_step(c, m_f, 0, hf_ref, None)
        m_b = chunk_step(nc - 1 - c, m_b, 1, hb_ref, None)
        return m_f, m_b

    def second_half(c, carry):
        m_f, m_b = carry
        m_f = chunk_step(c, m_f, 0, None, hb_ref)
        m_b = chunk_step(nc - 1 - c, m_b, 1, None, hf_ref)
        return m_f, m_b

    m0 = jnp.full((1, 1), NEG_BIG, F32)
    carry = lax.fori_loop(0, nc // 2, first_half, (m0, m0), unroll=2)
    lax.fori_loop(nc // 2, nc, second_half, carry, unroll=2)


def mlstm_mixer(z, gates_rows, gate_b_rows, head_g, chunk=MLSTM_CHUNK):
    b, s, _ = z.shape
    h, dk, dv = MLSTM_HEADS, MLSTM_DK, MLSTM_DV
    nc = s // chunk
    assert s % chunk == 0 and 6 * nc <= LANES and chunk % LANES == 0 and nc % 2 == 0
    kq = h * dk // dk
    kv = 2 * h * dk // dv
    ko = kv + h
    return pl.pallas_call(
        functools.partial(_mlstm_body, nc=nc, chunk=chunk),
        grid=(b, h),
        in_specs=[
            pl.BlockSpec((1, s, dk), lambda i, j: (i, 0, j)),
            pl.BlockSpec((1, s, dk), lambda i, j: (i, 0, kq + j)),
            pl.BlockSpec((1, s, dv), lambda i, j: (i, 0, kv + j)),
            pl.BlockSpec((1, s, dv), lambda i, j: (i, 0, ko + j)),
            pl.BlockSpec((1, 1, 4, nc, chunk), lambda i, j: (i, j, 0, 0, 0)),
            pl.BlockSpec((1, 4, 1, chunk), lambda i, j: (j, 0, 0, 0)),
            pl.BlockSpec((1, 1, dv), lambda i, j: (j, 0, 0)),
        ],
        out_specs=pl.BlockSpec((1, s, dv), lambda i, j: (i, 0, j)),
        out_shape=jax.ShapeDtypeStruct((b, s, h * dv), BF16),
        scratch_shapes=[
            pltpu.VMEM((6, nc, chunk), F32),
            pltpu.VMEM((nc, chunk, LANES), F32),
            pltpu.VMEM((s, dv), F32),
            pltpu.VMEM((s, dv), F32),
            pltpu.VMEM((2, dk, dv + LANES), F32),
        ],
        compiler_params=_cparams("parallel", "parallel"),
        name="mlstm",
    )(z, z, z, z, gates_rows, gate_b_rows, head_g)


RG_TILE = 512
RG_PAD = SUBLANES
RG_SCAN_UNROLL = 8


def _softplus(x):
    return jnp.maximum(x, 0.0) + jnp.log1p(jnp.exp(-jnp.abs(x)))


def _gelu_tanh(x):
    c = math.sqrt(2.0 / math.pi)
    return x * (0.5 * (1.0 + jnp.tanh(c * (x + 0.044715 * (x * x * x)))))


def _rglru_body(x_ref, gr_ref, cw_ref, cb_ref, wa_ref, wx_ref, ba_ref, bx_ref, lam_ref, y_ref,
                xpad_ref, a_ref, u_ref, as_ref, us_ref, h_ref, *, s):
    nt = s // RG_TILE
    zeros = jnp.zeros((RG_PAD, LANES), F32)
    xpad_ref[pl.ds(0, RG_PAD), :] = zeros
    xpad_ref[pl.ds(RG_PAD + s, RG_PAD), :] = zeros

    def copy_in(t, carry):
        r0 = pl.multiple_of(t * RG_TILE, RG_TILE)
        xpad_ref[pl.ds(RG_PAD + r0, RG_TILE), :] = x_ref[0, pl.ds(r0, RG_TILE), :]
        return carry

    lax.fori_loop(0, nt, copy_in, 0)

    cw = cw_ref[...]
    cb = cb_ref[...]
    half_sp = [0.5 * RGLRU_C * _softplus(-lam_ref[d:d + 1, :]) for d in range(2)]

    def gates(t, carry):
        r0 = pl.multiple_of(t * RG_TILE, RG_TILE)
        xc = None
        for j in range(CONV_WIDTH):
            tap = xpad_ref[pl.ds(r0 + RG_PAD - CONV_LEFT + j, RG_TILE), :] * cw[j:j + 1, :]
            xc = tap if xc is None else xc + tap
        xc = xc + cb
        xcb = xc.astype(BF16)
        half_xc = 0.5 * xc
        for d in range(2):
            t_r = jnp.tanh(jnp.dot(xcb, wa_ref[d, 0], preferred_element_type=F32)
                           + ba_ref[d:d + 1, :])
            t_i = jnp.tanh(jnp.dot(xcb, wx_ref[d, 0], preferred_element_type=F32)
                           + bx_ref[d:d + 1, :])
            neg_log_a = half_sp[d] * t_r + half_sp[d]
            gated_x = half_xc * t_i + half_xc
            a = jnp.exp(-neg_log_a)
            one_minus_a2 = jnp.tanh(neg_log_a) * (1.0 + a * a)
            root = jnp.where(one_minus_a2 > 0.0, one_minus_a2 * lax.rsqrt(one_minus_a2), 0.0)
            a_ref[d, pl.ds(r0, RG_TILE), :] = a
            u_ref[d, pl.ds(r0, RG_TILE), :] = root * gated_x
        return carry

    def block_scan(t, carry):
        r0 = pl.multiple_of(t * RG_TILE, RG_TILE)
        for d in range(2):
            a_prev = u_prev = None
            for r in (range(SUBLANES) if d == 0 else reversed(range(SUBLANES))):
                rows = pl.ds(r0 + r, RG_TILE // SUBLANES, stride=SUBLANES)
                a, u = a_ref[d, rows, :], u_ref[d, rows, :]
                if a_prev is not None:
                    u = a * u_prev + u
                    a = a * a_prev
                as_ref[d, rows, :] = a
                us_ref[d, rows, :] = u
                a_prev, u_prev = a, u
        return carry

    lax.fori_loop(0, nt, gates, 0)
    lax.fori_loop(0, nt, block_scan, 0)

    nblk = s // SUBLANES

    def scan(j, carry):
        h_f, h_b = carry
        for d, last, h in ((0, SUBLANES - 1, h_f), (1, 0, h_b)):
            blk = j if d == 0 else nblk - 1 - j
            rows = pl.ds(pl.multiple_of(blk * SUBLANES, SUBLANES), SUBLANES)
            a, u = as_ref[d, rows, :], us_ref[d, rows, :]
            h_ref[d, rows, :] = a * h + u
            a_end = jnp.broadcast_to(a[last:last + 1, :], (SUBLANES, LANES))
            u_end = jnp.broadcast_to(u[last:last + 1, :], (SUBLANES, LANES))
            if d == 0:
                h_f = a_end * h + u_end
            else:
                h_b = a_end * h + u_end
        return h_f, h_b

    h0 = jnp.zeros((SUBLANES, LANES), F32)
    lax.fori_loop(0, nblk, scan, (h0, h0), unroll=RG_SCAN_UNROLL)

    def finish(t, carry):
        r0 = pl.multiple_of(t * RG_TILE, RG_TILE)
        hr = h_ref[0, pl.ds(r0, RG_TILE), :] + h_ref[1, pl.ds(r0, RG_TILE), :]
        y_ref[0, pl.ds(r0, RG_TILE), :] = (
            hr * _gelu_tanh(gr_ref[0, pl.ds(r0, RG_TILE), :])).astype(y_ref.dtype)
        return carry

    lax.fori_loop(0, nt, finish, 0)


def rglru_mixer(z, xr_block0, gr_block0, conv_w, conv_b, wa, wx, ba, bx, lam):
    b, s, _ = z.shape
    width = RNN_BLOCKS * RNN_BLOCK
    assert s % RG_TILE == 0
    return pl.pallas_call(
        functools.partial(_rglru_body, s=s),
        grid=(b, RNN_BLOCKS),
        in_specs=[
            pl.BlockSpec((1, s, RNN_BLOCK), lambda i, j: (i, 0, xr_block0 + j)),
            pl.BlockSpec((1, s, RNN_BLOCK), lambda i, j: (i, 0, gr_block0 + j)),
            pl.BlockSpec((CONV_WIDTH, RNN_BLOCK), lambda i, j: (0, j)),
            pl.BlockSpec((1, RNN_BLOCK), lambda i, j: (0, j)),
            pl.BlockSpec((2, 1, RNN_BLOCK, RNN_BLOCK), lambda i, j: (0, j, 0, 0)),
            pl.BlockSpec((2, 1, RNN_BLOCK, RNN_BLOCK), lambda i, j: (0, j, 0, 0)),
            pl.BlockSpec((2, RNN_BLOCK), lambda i, j: (0, j)),
            pl.BlockSpec((2, RNN_BLOCK), lambda i, j: (0, j)),
            pl.BlockSpec((2, RNN_BLOCK), lambda i, j: (0, j)),
        ],
        out_specs=pl.BlockSpec((1, s, RNN_BLOCK), lambda i, j: (i, 0, j)),
        out_shape=jax.ShapeDtypeStruct((b, s, width), BF16),
        scratch_shapes=[
            pltpu.VMEM((s + 2 * RG_PAD, RNN_BLOCK), F32),
            pltpu.VMEM((2, s, RNN_BLOCK), F32),
            pltpu.VMEM((2, s, RNN_BLOCK), F32),
            pltpu.VMEM((2, s, RNN_BLOCK), F32),
            pltpu.VMEM((2, s, RNN_BLOCK), F32),
            pltpu.VMEM((2, s, RNN_BLOCK), F32),
        ],
        compiler_params=_cparams("parallel", "parallel"),
        name="rglru",
    )(z, z, conv_w, conv_b, wa, wx, ba, bx, lam)


ATTN_TQ = 128
ATTN_TK = ATTN_TQ + 2 * ATTN_HALF
ATTN_UNROLL = 32
LOG2E = math.log2(math.e)


def _attn_body(q_ref, k_ref, v_ref, slope_ref, o_ref, bias_ref, x4_ref, og_ref, dg_ref, mg_ref,
               *, s):
    qscale = ATTN_DH ** -0.5 * LOG2E

    @pl.when(pl.program_id(1) == 0)
    def _():
        slope = slope_ref[0, 0:1, 0:1] * LOG2E
        qi = lax.broadcasted_iota(jnp.int32, (ATTN_TQ, ATTN_TK), 0)
        kj = lax.broadcasted_iota(jnp.int32, (ATTN_TQ, ATTN_TK), 1)
        for g, dil in enumerate(ATTN_DILATIONS):
            for e in range(3):
                rel = jnp.abs(kj - e * ATTN_HALF - qi)
                pen = slope * (rel * dil).astype(F32)
                bias_ref[g, e] = jnp.where(rel <= ATTN_HALF, -pen, NEG_BIG)

    s4 = s // 4
    srcs = (q_ref, k_ref, v_ref)

    def split4(t, carry):
        c = t // (s4 // ATTN_TK)
        p0 = (t % (s4 // ATTN_TK)) * ATTN_TK
        dst = pl.ds(pl.multiple_of(c * s4 + p0, ATTN_TK), ATTN_TK)
        for a in range(3):
            x = srcs[a][0, pl.ds(c + 4 * p0, ATTN_TK, stride=4), :]
            x4_ref[a, dst, :] = x * qscale if a == 0 else x
        return carry

    lax.fori_loop(0, 4 * (s4 // ATTN_TK), split4, 0)

    ones = jnp.ones((ATTN_TK, ATTN_DH), BF16)

    for g, dil in reversed(list(enumerate(ATTN_DILATIONS))):
        sp = s // dil
        nqb = sp // ATTN_TQ

        def q_block(t, carry, g=g, dil=dil, sp=sp, nqb=nqb):
            r = t // nqb
            p0 = (t % nqb) * ATTN_TQ
            kstart = jnp.clip(p0 - ATTN_HALF, 0, sp - ATTN_TK)
            e = (p0 - kstart) // ATTN_HALF
            if dil == 1:
                qrows = pl.ds(pl.multiple_of(p0, ATTN_TQ), ATTN_TQ)
                krows = pl.ds(pl.multiple_of(kstart, ATTN_HALF), ATTN_TK)
                qf = q_ref[0, qrows, :] * qscale
                kf, vf = k_ref[0, krows, :], v_ref[0, krows, :]
                orows = qrows
            elif dil == 4:
                qrows = pl.ds(pl.multiple_of(r * s4 + p0, ATTN_TQ), ATTN_TQ)
                krows = pl.ds(pl.multiple_of(r * s4 + kstart, ATTN_HALF), ATTN_TK)
                qf, kf, vf = x4_ref[0, qrows, :], x4_ref[1, krows, :], x4_ref[2, krows, :]
                orows = pl.ds(r + 4 * p0, ATTN_TQ, stride=4)
            else:
                base = (r % 4) * s4 + r // 4
                qrows = pl.ds(base + 4 * p0, ATTN_TQ, stride=4)
                krows = pl.ds(base + 4 * kstart, ATTN_TK, stride=4)
                qf, kf, vf = x4_ref[0, qrows, :], x4_ref[1, krows, :], x4_ref[2, krows, :]
                orows = pl.ds(r + dil * p0, ATTN_TQ, stride=dil)
            qb, kb, vb = qf.astype(BF16), kf.astype(BF16), vf.astype(BF16)
            sc = lax.dot_general(qb, kb, (((1,), (1,)), ((), ())),
                                 preferred_element_type=F32) + bias_ref[g, e]
            m = jnp.max(sc, axis=-1, keepdims=True)
            p = jnp.exp2(sc - m).astype(BF16)
            pv = jnp.dot(p, jnp.concatenate([vb, ones], axis=1), preferred_element_type=F32)
            acc, den = pv[:, :ATTN_DH], pv[:, ATTN_DH:]
            if dil != 1:
                og_ref[g - 1, orows, :] = acc
                dg_ref[g - 1, orows, :] = den
                mg_ref[g - 1, orows, :] = jnp.broadcast_to(m, den.shape)
                return carry
            m4, m16 = mg_ref[0, orows, :], mg_ref[1, orows, :]
            mx = jnp.maximum(jnp.maximum(m4, m16), m)
            e1, e4, e16 = jnp.exp2(m - mx), jnp.exp2(m4 - mx), jnp.exp2(m16 - mx)
            num = e1 * acc + e4 * og_ref[0, orows, :] + e16 * og_ref[1, orows, :]
            tot = e1 * den + e4 * dg_ref[0, orows, :] + e16 * dg_ref[1, orows, :]
            o_ref[0, orows, :] = (num * (1.0 / tot)).astype(o_ref.dtype)
            return carry

        lax.fori_loop(0, dil * nqb, q_block, 0, unroll=ATTN_UNROLL)


def dilated_attention(qkv, slopes):
    b, s, _ = qkv.shape
    h, dh = ATTN_HEADS, ATTN_DH
    assert s % (max(ATTN_DILATIONS) * ATTN_TK) == 0
    return pl.pallas_call(
        functools.partial(_attn_body, s=s),
        grid=(h, b),
        in_specs=[
            pl.BlockSpec((1, s, dh), lambda j, i: (i, 0, j)),
            pl.BlockSpec((1, s, dh), lambda j, i: (i, 0, h + j)),
            pl.BlockSpec((1, s, dh), lambda j, i: (i, 0, 2 * h + j)),
            pl.BlockSpec((1, SUBLANES, LANES), lambda j, i: (j, 0, 0)),
        ],
        out_specs=pl.BlockSpec((1, s, dh), lambda j, i: (i, 0, j)),
        out_shape=jax.ShapeDtypeStruct((b, s, h * dh), BF16),
        scratch_shapes=[
            pltpu.VMEM((len(ATTN_DILATIONS), 3, ATTN_TQ, ATTN_TK), F32),
            pltpu.VMEM((3, s, dh), F32),
            pltpu.VMEM((len(ATTN_DILATIONS) - 1, s, dh), F32),
            pltpu.VMEM((len(ATTN_DILATIONS) - 1, s, dh), F32),
            pltpu.VMEM((len(ATTN_DILATIONS) - 1, s, dh), F32),
        ],
        compiler_params=_cparams("parallel", "arbitrary"),
        name="dilated_attn",
    )(qkv, qkv, qkv, slopes)


MIXER_OUT_TM = 512


FFN_DOWN_TM = 512


def _ffn(x, h, w1, w3, w2, layer, next_norm_g, last):
    u = swiglu_up(h, w1, w3, layer)
    return matmul([(u, w2, 0, layer)], res=x, norm_g=next_norm_g,
                  norm_dtype=F32 if last else BF16, keep_sum=not last,
                  tm=FFN_DOWN_TM, tn=w2.shape[-1], name="ffn_down")


def _even_layer(x, bsz, seq, norm_g, w_in, gate_b, conv_w, conv_b, rg_wa, rg_ba, rg_wx, rg_bx,
                rg_lam, head_g, w_out, next_norm_g):
    heads, dk, dv = MLSTM_HEADS, MLSTM_DK, MLSTM_DV
    n_qkvo = 2 * heads * dk + 2 * heads * dv
    n_gate = 4 * heads
    width = RNN_BLOCKS * RNN_BLOCK
    w_in_t = jnp.transpose(w_in)
    w_main_t = jnp.concatenate([w_in_t[:n_qkvo], w_in_t[n_qkvo + n_gate:]], axis=0).astype(BF16)
    w_gate_t = w_in_t[n_qkvo:n_qkvo + n_gate].astype(BF16)
    z, zg_t = norm_matmul(x, norm_g, w_main_t, w_side_t=w_gate_t, w_transposed=True,
                          name="in_proj")
    z = z.reshape(bsz, seq, -1)

    chunk = MLSTM_CHUNK
    nc = seq // chunk
    gates_rows = jnp.transpose(zg_t[:n_gate].reshape(4, heads, bsz, nc, chunk), (2, 1, 0, 3, 4))
    gate_b_rows = jnp.broadcast_to(
        jnp.transpose(gate_b.astype(F32).reshape(4, heads))[:, :, None, None], (heads, 4, 1, chunk))
    y_a = mlstm_mixer(z, gates_rows, gate_b_rows, head_g.astype(F32).reshape(heads, 1, dv), chunk)

    y_b = rglru_mixer(z, n_qkvo // RNN_BLOCK, (n_qkvo + width) // RNN_BLOCK,
                      conv_w.astype(F32), conv_b.astype(F32).reshape(1, width),
                      (0.5 * rg_wa).astype(BF16), (0.5 * rg_wx).astype(BF16),
                      0.5 * rg_ba.astype(F32), 0.5 * rg_bx.astype(F32), rg_lam.astype(F32))

    w_out_b = w_out.astype(BF16)
    m = bsz * seq
    assert heads * dv == width
    return matmul([(y_a.reshape(m, -1), w_out_b, 0),
                   (y_b.reshape(m, -1), w_out_b, 1)], res=x, norm_g=next_norm_g,
                  tm=MIXER_OUT_TM, tn=w_out.shape[1], name="out_proj")


def _odd_layer(x, h, bsz, seq, norm_g, w_qkv, w_o, next_norm_g):
    if h is None:
        qkv = norm_matmul(x, norm_g, w_qkv.astype(BF16), tm=256, name="qkv_proj")
    else:
        qkv = matmul([(h, w_qkv.astype(BF16))], tm=512, tn=w_qkv.shape[1], name="qkv_proj")
    qkv = qkv.reshape(bsz, seq, -1)
    slopes = jnp.exp2(-ALIBI_MAX_BIAS * jnp.arange(1, ATTN_HEADS + 1, dtype=F32) / ATTN_HEADS)
    slopes = jnp.broadcast_to(slopes[:, None, None], (ATTN_HEADS, SUBLANES, LANES))
    o = dilated_attention(qkv, slopes)
    return matmul([(o.reshape(bsz * seq, -1), w_o.astype(BF16))], res=x, norm_g=next_norm_g,
                  tm=MIXER_OUT_TM, tn=w_o.shape[1], name="attn_out_proj")


def kernel(x, e_norm, e_w_in, e_gate_b, e_conv_w, e_conv_b, e_rg_wa, e_rg_ba, e_rg_wx, e_rg_bx,
           e_rg_lam, e_head_g, e_w_out, o_norm, o_w_qkv, o_w_o, f_norm, f_w1, f_w3, f_w2,
           final_norm):
    bsz, seq, d = x.shape
    depth = f_norm.shape[0]
    xs = x.reshape(bsz * seq, d).astype(F32)
    w1, w3, w2 = f_w1.astype(BF16), f_w3.astype(BF16), f_w2.astype(BF16)
    hs = None
    for l in range(depth):
        if l % 2 == 0:
            e = l // 2
            xs, hs = _even_layer(xs, bsz, seq, e_norm[e], e_w_in[e], e_gate_b[e], e_conv_w[e],
                                 e_conv_b[e], e_rg_wa[e], e_rg_ba[e], e_rg_wx[e], e_rg_bx[e],
                                 e_rg_lam[e], e_head_g[e], e_w_out[e], f_norm[l])
        else:
            o = l // 2
            xs, hs = _odd_layer(xs, hs, bsz, seq, o_norm[o], o_w_qkv[o], o_w_o[o], f_norm[l])
        if l == depth - 1:
            out = _ffn(xs, hs, w1, w3, w2, l, final_norm, last=True)
            return out.astype(x.dtype).reshape(bsz, seq, d)
        if (l + 1) % 2:
            xs, hs = _ffn(xs, hs, w1, w3, w2, l, o_norm[(l + 1) // 2], last=False)
        else:
            xs, hs = _ffn(xs, hs, w1, w3, w2, l, None, last=False), None
```

```python
import functools
import math

import jax
import jax.numpy as jnp
from jax import lax
from jax.experimental import pallas as pl
from jax.experimental.pallas import tpu as pltpu

F32 = jnp.float32
BF16 = jnp.bfloat16

RMS_EPS = 1e-6
NEG_BIG = -1e30

MLSTM_HEADS = 4
MLSTM_DK = 128
MLSTM_DV = 256
MLSTM_CHUNK = 256
RNN_BLOCKS = 8
RNN_BLOCK = 128
CONV_WIDTH = 4
CONV_LEFT = 2
RGLRU_C = 8.0
ATTN_HEADS = 16
ATTN_DH = 128
ATTN_HALF = 64
ATTN_DILATIONS = (1, 4, 16)
ALIBI_MAX_BIAS = 8.0

LANES = 128
SUBLANES = 8
VMEM_LIMIT_BYTES = 56 * 1024 * 1024


def _cparams(*semantics):
    return pltpu.CompilerParams(dimension_semantics=semantics,
                                vmem_limit_bytes=VMEM_LIMIT_BYTES)


NORM_CHUNK = 256


def _weight_spec(w, kdim, tn, row_block=0, layer=None, **mode):
    if w.ndim == 2:
        return pl.BlockSpec((kdim, tn), lambda i, j: (row_block, j), **mode)
    return pl.BlockSpec((None, kdim, tn), lambda i, j: (layer, row_block, j), **mode)


def _matmul_body(*refs, n_pairs, has_res, has_norm, keep_sum):
    pairs = [(refs[2 * p], refs[2 * p + 1]) for p in range(n_pairs)]
    rest = list(refs[2 * n_pairs:])
    r_ref = rest.pop(0) if has_res else None
    g_ref = rest.pop(0) if has_norm else None
    y_ref = rest.pop(0) if keep_sum else None
    h_ref = rest.pop(0) if has_norm else None

    tm = pairs[0][0].shape[0]
    rows = min(NORM_CHUNK, tm) if has_norm else tm
    for r0 in range(0, tm, rows):
        sl = pl.ds(r0, rows)
        acc = None
        for a_ref, w_ref in pairs:
            d = jnp.dot(a_ref[sl, :], w_ref[...], preferred_element_type=F32)
            acc = d if acc is None else acc + d
        if has_res:
            acc = acc + r_ref[sl, :]
        if keep_sum:
            y_ref[sl, :] = acc
        if has_norm:
            ms = jnp.mean(acc * acc, axis=-1, keepdims=True)
            h_ref[sl, :] = (acc * lax.rsqrt(ms + RMS_EPS) * g_ref[...]).astype(h_ref.dtype)


def matmul(pairs, res=None, norm_g=None, norm_dtype=BF16, keep_sum=True, tm=1024, tn=1024,
           name="matmul"):
    m = pairs[0][0].shape[0]
    n = pairs[0][1].shape[-1]
    tn = min(tn, n)
    assert m % tm == 0 and n % tn == 0 and (keep_sum or norm_g is not None)
    assert norm_g is None or (tn == n and tm % NORM_CHUNK == 0)
    w_mode = dict(pipeline_mode=pl.Buffered(1)) if tn == n else {}
    in_specs, args = [], []
    for a, w, *where in pairs:
        kdim = a.shape[1]
        kb, layer = (list(where) + [0, None])[:2] if where else (0, None)
        in_specs.append(pl.BlockSpec((tm, kdim), lambda i, j: (i, 0)))
        in_specs.append(_weight_spec(w, kdim, tn, kb, layer, **w_mode))
        args += [a, w]
    if res is not None:
        in_specs.append(pl.BlockSpec((tm, tn), lambda i, j: (i, j)))
        args.append(res)
    out_specs, out_shape = [], []
    if keep_sum:
        out_specs.append(pl.BlockSpec((tm, tn), lambda i, j: (i, j)))
        out_shape.append(jax.ShapeDtypeStruct((m, n), F32))
    if norm_g is not None:
        in_specs.append(pl.BlockSpec((1, n), lambda i, j: (0, 0)))
        args.append(norm_g.reshape(1, n).astype(F32))
        out_specs.append(pl.BlockSpec((tm, n), lambda i, j: (i, 0)))
        out_shape.append(jax.ShapeDtypeStruct((m, n), norm_dtype))
    outs = pl.pallas_call(
        functools.partial(_matmul_body, n_pairs=len(pairs), has_res=res is not None,
                          has_norm=norm_g is not None, keep_sum=keep_sum),
        grid=(m // tm, n // tn),
        in_specs=in_specs,
        out_specs=out_specs,
        out_shape=out_shape,
        compiler_params=_cparams("parallel", "parallel"),
        name=name,
    )(*args)
    return outs if len(outs) > 1 else outs[0]


SWIGLU_ROWS = 1024


def _swiglu_up_body(a_ref, w1_ref, w3_ref, o_ref):
    rows = min(SWIGLU_ROWS, a_ref.shape[0])
    for r0 in range(0, a_ref.shape[0], rows):
        a = a_ref[pl.ds(r0, rows), :]
        u = jnp.dot(a, w1_ref[...], preferred_element_type=F32)
        v = jnp.dot(a, w3_ref[...], preferred_element_type=F32)
        o_ref[pl.ds(r0, rows), :] = (u * jax.nn.sigmoid(u) * v).astype(o_ref.dtype)


def swiglu_up(a, w1, w3, layer, tm=4096, tn=512):
    m, kdim = a.shape
    n = w1.shape[-1]
    assert m % tm == 0 and n % tn == 0
    return pl.pallas_call(
        _swiglu_up_body,
        grid=(m // tm, n // tn),
        in_specs=[pl.BlockSpec((tm, kdim), lambda i, j: (i, 0)),
                  _weight_spec(w1, kdim, tn, layer=layer),
                  _weight_spec(w3, kdim, tn, layer=layer)],
        out_specs=pl.BlockSpec((tm, tn), lambda i, j: (i, j)),
        out_shape=jax.ShapeDtypeStruct((m, n), BF16),
        compiler_params=_cparams("parallel", "parallel"),
        name="swiglu_up",
    )(a, w1, w3)


def _norm_matmul_body(*refs, has_side, w_transposed):
    if has_side:
        x_ref, g_ref, w_ref, ws_ref, o_ref, os_ref = refs
    else:
        x_ref, g_ref, w_ref, o_ref = refs
    w_dims = (((1,), (1,)), ((), ())) if w_transposed else (((1,), (0,)), ((), ()))

    g = g_ref[...]
    for r0 in range(0, x_ref.shape[0], NORM_CHUNK):
        sl = pl.ds(r0, NORM_CHUNK)
        x = x_ref[sl, :]
        ms = jnp.mean(x * x, axis=-1, keepdims=True)
        h = (x * lax.rsqrt(ms + RMS_EPS) * g).astype(BF16)
        o_ref[sl, :] = lax.dot_general(h, w_ref[...], w_dims,
                                       preferred_element_type=F32).astype(o_ref.dtype)
        if has_side:
            os_ref[:, sl] = lax.dot_general(ws_ref[...], h, (((1,), (1,)), ((), ())),
                                            preferred_element_type=F32)


def norm_matmul(x, g, w, w_side_t=None, w_transposed=False, out_dtype=F32, tm=512,
                name="norm_matmul"):
    m, d = x.shape
    n = w.shape[0] if w_transposed else w.shape[1]
    assert m % tm == 0 and tm % NORM_CHUNK == 0
    resident = dict(pipeline_mode=pl.Buffered(1))
    in_specs = [pl.BlockSpec((tm, d), lambda i: (i, 0)),
                pl.BlockSpec((1, d), lambda i: (0, 0)),
                pl.BlockSpec(w.shape, lambda i: (0, 0), **resident)]
    args = [x, g.reshape(1, d).astype(F32), w]
    out_specs = [pl.BlockSpec((tm, n), lambda i: (i, 0))]
    out_shape = [jax.ShapeDtypeStruct((m, n), out_dtype)]
    if w_side_t is not None:
        ns = w_side_t.shape[0]
        in_specs.append(pl.BlockSpec((ns, d), lambda i: (0, 0), **resident))
        args.append(w_side_t)
        out_specs.append(pl.BlockSpec((ns, tm), lambda i: (0, i)))
        out_shape.append(jax.ShapeDtypeStruct((ns, m), F32))
    outs = pl.pallas_call(
        functools.partial(_norm_matmul_body, has_side=w_side_t is not None,
                          w_transposed=w_transposed),
        grid=(m // tm,),
        in_specs=in_specs,
        out_specs=out_specs,
        out_shape=out_shape,
        compiler_params=_cparams("parallel"),
        name=name,
    )(*args)
    return outs if w_side_t is not None else outs[0]


def _log_sigmoid(x):
    return jnp.minimum(x, 0.0) - jnp.log1p(jnp.exp(-jnp.abs(x)))


SCAN_RADIX = 4


def _lane_scan(x, combine, identity, reverse):
    width = x.shape[-1]
    lane = lax.broadcasted_iota(jnp.int32, x.shape, 1)

    def shifted(v, sh):
        if reverse:
            return jnp.where(lane < width - sh, pltpu.roll(v, width - sh, axis=1), identity)
        return jnp.where(lane >= sh, pltpu.roll(v, sh, axis=1), identity)

    sh = 1
    while sh < width:
        parts = [shifted(x, k * sh) for k in range(1, SCAN_RADIX) if k * sh < width]
        for p in parts:
            x = combine(x, p)
        sh *= SCAN_RADIX
    return x


def _mlstm_body(q_ref, k_ref, v_ref, og_ref, g_ref, gb_ref, hg_ref, y_ref,
                rows_ref, cols_ref, hf_ref, hb_ref, ct_ref, *, nc, chunk):
    dk, dv = MLSTM_DK, MLSTM_DV
    scale = dk ** -0.5

    g = g_ref[0, 0] + gb_ref[0]
    rows = []
    for d in range(2):
        b = _lane_scan(_log_sigmoid(g[2 * d + 1]), jnp.add, 0.0, reverse=d == 1)
        gg = g[2 * d] - b
        gmax = _lane_scan(gg, jnp.maximum, -jnp.inf, reverse=d == 1)
        rows += [gg, b, gmax]
        for j, r in enumerate((gg, b, gmax)):
            rows_ref[3 * d + j] = r
    rowmat = jnp.concatenate(rows + [jnp.zeros((LANES - 6 * nc, chunk), F32)], axis=0)
    colmat = rowmat.T
    for c in range(nc):
        cols_ref[c] = colmat if c == 0 else pltpu.roll(colmat, LANES - c, axis=1)

    ct_ref[...] = jnp.zeros_like(ct_ref)

    row_id = lax.broadcasted_iota(jnp.int32, (chunk, chunk), 0)
    col_id = lax.broadcasted_iota(jnp.int32, (chunk, chunk), 1)
    causal = (row_id >= col_id, row_id <= col_id)
    ones = jnp.ones((chunk, LANES), BF16)

    def lanes2(x):
        return jnp.concatenate([x] * (dv // LANES), axis=1)

    hg = hg_ref[0]

    def emit(r0, h, own_ref, other_ref):
        rows = pl.ds(r0, chunk)
        if other_ref is None:
            own_ref[rows, :] = h
            return
        hm = h + other_ref[rows, :]
        ms = jnp.mean(hm * hm, axis=-1, keepdims=True)
        hn = hm * lax.rsqrt(ms + RMS_EPS) * hg
        y_ref[0, rows, :] = (hn * jax.nn.sigmoid(og_ref[0, rows, :])).astype(y_ref.dtype)

    def chunk_step(c, m, d, own_ref, other_ref):
        r0 = pl.multiple_of(c * chunk, chunk)
        qb = (q_ref[0, pl.ds(r0, chunk), :] * scale).astype(BF16)
        kb = k_ref[0, pl.ds(r0, chunk), :].astype(BF16)
        vf = v_ref[0, pl.ds(r0, chunk), :]
        cols = cols_ref[c]

        def column(j):
            lane = (3 * d + j) * nc
            return jnp.broadcast_to(cols[:, lane:lane + 1], (chunk, LANES))

        gg_col, b_col, gmax_col = column(0), column(1), column(2)
        gg_row = rows_ref[3 * d, pl.ds(c, 1), :]
        end = chunk - 1 if d == 0 else 0
        total = rows_ref[3 * d + 1, pl.ds(c, 1), :][:, end:end + 1]
        gmax_end = rows_ref[3 * d + 2, pl.ds(c, 1), :][:, end:end + 1]

        m_row = jnp.maximum(gmax_col, m)
        w_intra = jnp.exp(jnp.where(causal[d], gg_row - lanes2(m_row), NEG_BIG))
        w_inter = jnp.exp(m - m_row)
        s = lax.dot_general(qb, kb, (((1,), (1,)), ((), ())),
                            preferred_element_type=F32) * w_intra
        ct = ct_ref[d]
        intra = jnp.dot(s.astype(BF16), jnp.concatenate([vf.astype(BF16), ones], axis=1),
                        preferred_element_type=F32)
        inter = jnp.dot(qb, ct.astype(BF16), preferred_element_type=F32)
        num = intra[:, :dv] + lanes2(w_inter) * inter[:, :dv]
        den = intra[:, dv:] + w_inter * inter[:, dv:]
        inv = 1.0 / jnp.maximum(jnp.abs(den), jnp.exp(-(b_col + m_row)))
        emit(r0, num * lanes2(inv), own_ref, other_ref)

        m_new = jnp.maximum(total + m, total + gmax_end)
        w_src = jnp.exp(total + gg_col - m_new)
        decay = jnp.exp(total + m - m_new)
        wv = jnp.concatenate([lanes2(w_src) * vf, w_src], axis=1).astype(BF16)
        ct_ref[d] = decay * ct + lax.dot_general(
            kb, wv, (((0,), (0,)), ((), ())), preferred_element_type=F32)
        return m_new

    def first_half(c, carry):
        m_f, m_b = carry
        m_f = chunk_step(c, m_f, 0, hf_ref, None)
        m_b = chunk_step(nc - 1 - c, m_b, 1, hb_ref, None)
        return m_f, m_b

    def second_half(c, carry):
        m_f, m_b = carry
        m_f = chunk_step(c, m_f, 0, None, hb_ref)
        m_b = chunk_step(nc - 1 - c, m_b, 1, None, hf_ref)
        return m_f, m_b

    m0 = jnp.full((1, 1), NEG_BIG, F32)
    carry = lax.fori_loop(0, nc // 2, first_half, (m0, m0), unroll=2)
    lax.fori_loop(nc // 2, nc, second_half, carry, unroll=2)


def mlstm_mixer(z, gates_rows, gate_b_rows, head_g, chunk=MLSTM_CHUNK):
    b, s, _ = z.shape
    h, dk, dv = MLSTM_HEADS, MLSTM_DK, MLSTM_DV
    nc = s // chunk
    assert s % chunk == 0 and 6 * nc <= LANES and chunk % LANES == 0 and nc % 2 == 0
    kq = h * dk // dk
    kv = 2 * h * dk // dv
    ko = kv + h
    return pl.pallas_call(
        functools.partial(_mlstm_body, nc=nc, chunk=chunk),
        grid=(b, h),
        in_specs=[
            pl.BlockSpec((1, s, dk), lambda i, j: (i, 0, j)),
            pl.BlockSpec((1, s, dk), lambda i, j: (i, 0, kq + j)),
            pl.BlockSpec((1, s, dv), lambda i, j: (i, 0, kv + j)),
            pl.BlockSpec((1, s, dv), lambda i, j: (i, 0, ko + j)),
            pl.BlockSpec((1, 1, 4, nc, chunk), lambda i, j: (i, j, 0, 0, 0)),
            pl.BlockSpec((1, 4, 1, chunk), lambda i, j: (j, 0, 0, 0)),
            pl.BlockSpec((1, 1, dv), lambda i, j: (j, 0, 0)),
        ],
        out_specs=pl.BlockSpec((1, s, dv), lambda i, j: (i, 0, j)),
        out_shape=jax.ShapeDtypeStruct((b, s, h * dv), BF16),
        scratch_shapes=[
            pltpu.VMEM((6, nc, chunk), F32),
            pltpu.VMEM((nc, chunk, LANES), F32),
            pltpu.VMEM((s, dv), F32),
            pltpu.VMEM((s, dv), F32),
            pltpu.VMEM((2, dk, dv + LANES), F32),
        ],
        compiler_params=_cparams("parallel", "parallel"),
        name="mlstm",
    )(z, z, z, z, gates_rows, gate_b_rows, head_g)


RG_TILE = 512
RG_PAD = SUBLANES


def _softplus(x):
    return jnp.maximum(x, 0.0) + jnp.log1p(jnp.exp(-jnp.abs(x)))


def _gelu_tanh(x):
    c = math.sqrt(2.0 / math.pi)
    return x * (0.5 * (1.0 + jnp.tanh(c * (x + 0.044715 * (x * x * x)))))


def _rglru_body(x_ref, gr_ref, cw_ref, cb_ref, wa_ref, wx_ref, ba_ref, bx_ref, lam_ref, y_ref,
                xpad_ref, a_ref, u_ref, h_ref, *, s):
    nt = s // RG_TILE
    zeros = jnp.zeros((RG_PAD, LANES), F32)
    xpad_ref[pl.ds(0, RG_PAD), :] = zeros
    xpad_ref[pl.ds(RG_PAD + s, RG_PAD), :] = zeros

    def copy_in(t, carry):
        r0 = pl.multiple_of(t * RG_TILE, RG_TILE)
        xpad_ref[pl.ds(RG_PAD + r0, RG_TILE), :] = x_ref[0, pl.ds(r0, RG_TILE), :]
        return carry

    lax.fori_loop(0, nt, copy_in, 0)

    cw = cw_ref[...]
    cb = cb_ref[...]
    half_sp = [0.5 * RGLRU_C * _softplus(-lam_ref[d:d + 1, :]) for d in range(2)]

    def gates(t, carry):
        r0 = pl.multiple_of(t * RG_TILE, RG_TILE)
        xc = None
        for j in range(CONV_WIDTH):
            tap = xpad_ref[pl.ds(r0 + RG_PAD - CONV_LEFT + j, RG_TILE), :] * cw[j:j + 1, :]
            xc = tap if xc is None else xc + tap
        xc = xc + cb
        xcb = xc.astype(BF16)
        half_xc = 0.5 * xc
        for d in range(2):
            t_r = jnp.tanh(jnp.dot(xcb, wa_ref[d, 0], preferred_element_type=F32)
                           + ba_ref[d:d + 1, :])
            t_i = jnp.tanh(jnp.dot(xcb, wx_ref[d, 0], preferred_element_type=F32)
                           + bx_ref[d:d + 1, :])
            neg_log_a = half_sp[d] * t_r + half_sp[d]
            gated_x = half_xc * t_i + half_xc
            a = jnp.exp(-neg_log_a)
            one_minus_a2 = jnp.tanh(neg_log_a) * (1.0 + a * a)
            root = jnp.where(one_minus_a2 > 0.0, one_minus_a2 * lax.rsqrt(one_minus_a2), 0.0)
            a_ref[d, pl.ds(r0, RG_TILE), :] = a
            u_ref[d, pl.ds(r0, RG_TILE), :] = root * gated_x
        return carry

    lax.fori_loop(0, nt, gates, 0)

    nb = RG_TILE // SUBLANES
    sub = lax.broadcasted_iota(jnp.int32, (SUBLANES, LANES), 0)

    def compose_sublanes(a, u, reverse):
        sh = 1
        while sh < SUBLANES:
            keep = sub < SUBLANES - sh if reverse else sub >= sh
            amount = SUBLANES - sh if reverse else sh
            a_s = jnp.where(keep, pltpu.roll(a, amount, axis=0), 1.0)
            u_s = jnp.where(keep, pltpu.roll(u, amount, axis=0), 0.0)
            u = a * u_s + u
            a = a * a_s
            sh *= 2
        return a, u

    def tile_recurrence(t, d, carry):
        r0 = pl.multiple_of(t * RG_TILE, RG_TILE)
        reverse = d == 1
        a_rows, u_rows = [None] * SUBLANES, [None] * SUBLANES
        a_prev = u_prev = None
        for r in (reversed(range(SUBLANES)) if reverse else range(SUBLANES)):
            rows = pl.ds(r0 + r, nb, stride=SUBLANES)
            a, u = a_ref[d, rows, :], u_ref[d, rows, :]
            if a_prev is not None:
                u = a * u_prev + u
                a = a * a_prev
            a_rows[r], u_rows[r] = a, u
            a_prev, u_prev = a, u
        entry = [None] * (nb // SUBLANES)
        for v in (reversed(range(nb // SUBLANES)) if reverse else range(nb // SUBLANES)):
            blk = slice(v * SUBLANES, (v + 1) * SUBLANES)
            sa, su = compose_sublanes(a_prev[blk], u_prev[blk], reverse)
            after = sa * carry + su
            if reverse:
                entry[v] = jnp.where(sub < SUBLANES - 1, pltpu.roll(after, SUBLANES - 1, axis=0),
                                     carry)
                carry = after[0:1]
            else:
                entry[v] = jnp.where(sub >= 1, pltpu.roll(after, 1, axis=0), carry)
                carry = after[SUBLANES - 1:SUBLANES]
        h_entry = jnp.concatenate(entry, axis=0)
        for r in range(SUBLANES):
            h_ref[d, pl.ds(r0 + r, nb, stride=SUBLANES), :] = a_rows[r] * h_entry + u_rows[r]
        return carry

    def recurrences(t, carry):
        h_f, h_b = carry
        return tile_recurrence(t, 0, h_f), tile_recurrence(nt - 1 - t, 1, h_b)

    h0 = jnp.zeros((1, LANES), F32)
    lax.fori_loop(0, nt, recurrences, (h0, h0))

    def finish(t, carry):
        r0 = pl.multiple_of(t * RG_TILE, RG_TILE)
        hr = h_ref[0, pl.ds(r0, RG_TILE), :] + h_ref[1, pl.ds(r0, RG_TILE), :]
        y_ref[0, pl.ds(r0, RG_TILE), :] = (
            hr * _gelu_tanh(gr_ref[0, pl.ds(r0, RG_TILE), :])).astype(y_ref.dtype)
        return carry

    lax.fori_loop(0, nt, finish, 0)


def rglru_mixer(z, xr_block0, gr_block0, conv_w, conv_b, wa, wx, ba, bx, lam):
    b, s, _ = z.shape
    width = RNN_BLOCKS * RNN_BLOCK
    assert s % RG_TILE == 0
    return pl.pallas_call(
        functools.partial(_rglru_body, s=s),
        grid=(b, RNN_BLOCKS),
        in_specs=[
            pl.BlockSpec((1, s, RNN_BLOCK), lambda i, j: (i, 0, xr_block0 + j)),
            pl.BlockSpec((1, s, RNN_BLOCK), lambda i, j: (i, 0, gr_block0 + j)),
            pl.BlockSpec((CONV_WIDTH, RNN_BLOCK), lambda i, j: (0, j)),
            pl.BlockSpec((1, RNN_BLOCK), lambda i, j: (0, j)),
            pl.BlockSpec((2, 1, RNN_BLOCK, RNN_BLOCK), lambda i, j: (0, j, 0, 0)),
            pl.BlockSpec((2, 1, RNN_BLOCK, RNN_BLOCK), lambda i, j: (0, j, 0, 0)),
            pl.BlockSpec((2, RNN_BLOCK), lambda i, j: (0, j)),
            pl.BlockSpec((2, RNN_BLOCK), lambda i, j: (0, j)),
            pl.BlockSpec((2, RNN_BLOCK), lambda i, j: (0, j)),
        ],
        out_specs=pl.BlockSpec((1, s, RNN_BLOCK), lambda i, j: (i, 0, j)),
        out_shape=jax.ShapeDtypeStruct((b, s, width), BF16),
        scratch_shapes=[
            pltpu.VMEM((s + 2 * RG_PAD, RNN_BLOCK), F32),
            pltpu.VMEM((2, s, RNN_BLOCK), F32),
            pltpu.VMEM((2, s, RNN_BLOCK), F32),
            pltpu.VMEM((2, s, RNN_BLOCK), F32),
        ],
        compiler_params=_cparams("parallel", "parallel"),
        name="rglru",
    )(z, z, conv_w, conv_b, wa, wx, ba, bx, lam)


ATTN_TQ = 128
ATTN_TK = ATTN_TQ + 2 * ATTN_HALF
ATTN_UNROLL = 32
LOG2E = math.log2(math.e)


def _attn_body(q_ref, k_ref, v_ref, slope_ref, o_ref, bias_ref, x4_ref, og_ref, dg_ref, mg_ref,
               *, s):
    qscale = ATTN_DH ** -0.5 * LOG2E

    @pl.when(pl.program_id(1) == 0)
    def _():
        slope = slope_ref[0, 0:1, 0:1] * LOG2E
        qi = lax.broadcasted_iota(jnp.int32, (ATTN_TQ, ATTN_TK), 0)
        kj = lax.broadcasted_iota(jnp.int32, (ATTN_TQ, ATTN_TK), 1)
        for g, dil in enumerate(ATTN_DILATIONS):
            for e in range(3):
                rel = jnp.abs(kj - e * ATTN_HALF - qi)
                pen = slope * (rel * dil).astype(F32)
                bias_ref[g, e] = jnp.where(rel <= ATTN_HALF, -pen, NEG_BIG)

    s4 = s // 4
    srcs = (q_ref, k_ref, v_ref)

    def split4(t, carry):
        c = t // (s4 // ATTN_TK)
        p0 = (t % (s4 // ATTN_TK)) * ATTN_TK
        dst = pl.ds(pl.multiple_of(c * s4 + p0, ATTN_TK), ATTN_TK)
        for a in range(3):
            x = srcs[a][0, pl.ds(c + 4 * p0, ATTN_TK, stride=4), :]
            x4_ref[a, dst, :] = x * qscale if a == 0 else x
        return carry

    lax.fori_loop(0, 4 * (s4 // ATTN_TK), split4, 0, unroll=4)

    ones = jnp.ones((ATTN_TK, ATTN_DH), BF16)

    for g, dil in reversed(list(enumerate(ATTN_DILATIONS))):
        sp = s // dil
        nqb = sp // ATTN_TQ

        def q_block(t, carry, g=g, dil=dil, sp=sp, nqb=nqb):
            r = t // nqb
            p0 = (t % nqb) * ATTN_TQ
            kstart = jnp.clip(p0 - ATTN_HALF, 0, sp - ATTN_TK)
            e = (p0 - kstart) // ATTN_HALF
            if dil == 1:
                qrows = pl.ds(pl.multiple_of(p0, ATTN_TQ), ATTN_TQ)
                krows = pl.ds(pl.multiple_of(kstart, ATTN_HALF), ATTN_TK)
                qf = q_ref[0, qrows, :] * qscale
                kf, vf = k_ref[0, krows, :], v_ref[0, krows, :]
                orows = qrows
            elif dil == 4:
                qrows = pl.ds(pl.multiple_of(r * s4 + p0, ATTN_TQ), ATTN_TQ)
                krows = pl.ds(pl.multiple_of(r * s4 + kstart, ATTN_HALF), ATTN_TK)
                qf, kf, vf = x4_ref[0, qrows, :], x4_ref[1, krows, :], x4_ref[2, krows, :]
                orows = pl.ds(r + 4 * p0, ATTN_TQ, stride=4)
            else:
                base = (r % 4) * s4 + r // 4
                qrows = pl.ds(base + 4 * p0, ATTN_TQ, stride=4)
                krows = pl.ds(base + 4 * kstart, ATTN_TK, stride=4)
                qf, kf, vf = x4_ref[0, qrows, :], x4_ref[1, krows, :], x4_ref[2, krows, :]
                orows = pl.ds(r + dil * p0, ATTN_TQ, stride=dil)
            qb, kb, vb = qf.astype(BF16), kf.astype(BF16), vf.astype(BF16)
            sc = lax.dot_general(qb, kb, (((1,), (1,)), ((), ())),
                                 preferred_element_type=F32) + bias_ref[g, e]
            m = jnp.max(sc, axis=-1, keepdims=True)
            p = jnp.exp2(sc - m).astype(BF16)
            pv = jnp.dot(p, jnp.concatenate([vb, ones], axis=1), preferred_element_type=F32)
            acc, den = pv[:, :ATTN_DH], pv[:, ATTN_DH:]
            if dil != 1:
                og_ref[g - 1, orows, :] = acc
                dg_ref[g - 1, orows, :] = den
                mg_ref[g - 1, orows, :] = jnp.broadcast_to(m, den.shape)
                return carry
            m4, m16 = mg_ref[0, orows, :], mg_ref[1, orows, :]
            mx = jnp.maximum(jnp.maximum(m4, m16), m)
            e1, e4, e16 = jnp.exp2(m - mx), jnp.exp2(m4 - mx), jnp.exp2(m16 - mx)
            num = e1 * acc + e4 * og_ref[0, orows, :] + e16 * og_ref[1, orows, :]
            tot = e1 * den + e4 * dg_ref[0, orows, :] + e16 * dg_ref[1, orows, :]
            o_ref[0, orows, :] = (num * (1.0 / tot)).astype(o_ref.dtype)
            return carry

        lax.fori_loop(0, dil * nqb, q_block, 0, unroll=ATTN_UNROLL)


def dilated_attention(qkv, slopes):
    b, s, _ = qkv.shape
    h, dh = ATTN_HEADS, ATTN_DH
    assert s % (max(ATTN_DILATIONS) * ATTN_TK) == 0
    return pl.pallas_call(
        functools.partial(_attn_body, s=s),
        grid=(h, b),
        in_specs=[
            pl.BlockSpec((1, s, dh), lambda j, i: (i, 0, j)),
            pl.BlockSpec((1, s, dh), lambda j, i: (i, 0, h + j)),
            pl.BlockSpec((1, s, dh), lambda j, i: (i, 0, 2 * h + j)),
            pl.BlockSpec((1, SUBLANES, LANES), lambda j, i: (j, 0, 0)),
        ],
        out_specs=pl.BlockSpec((1, s, dh), lambda j, i: (i, 0, j)),
        out_shape=jax.ShapeDtypeStruct((b, s, h * dh), BF16),
        scratch_shapes=[
            pltpu.VMEM((len(ATTN_DILATIONS), 3, ATTN_TQ, ATTN_TK), F32),
            pltpu.VMEM((3, s, dh), F32),
            pltpu.VMEM((len(ATTN_DILATIONS) - 1, s, dh), F32),
            pltpu.VMEM((len(ATTN_DILATIONS) - 1, s, dh), F32),
            pltpu.VMEM((len(ATTN_DILATIONS) - 1, s, dh), F32),
        ],
        compiler_params=_cparams("parallel", "arbitrary"),
        name="dilated_attn",
    )(qkv, qkv, qkv, slopes)


MIXER_OUT_TM = 512


FFN_DOWN_TM = 512


def _ffn(x, h, w1, w3, w2, layer, next_norm_g, last):
    u = swiglu_up(h, w1, w3, layer)
    return matmul([(u, w2, 0, layer)], res=x, norm_g=next_norm_g,
                  norm_dtype=F32 if last else BF16, keep_sum=not last,
                  tm=FFN_DOWN_TM, tn=w2.shape[-1], name="ffn_down")


def _even_layer(x, bsz, seq, norm_g, w_in, gate_b, conv_w, conv_b, rg_wa, rg_ba, rg_wx, rg_bx,
                rg_lam, head_g, w_out, next_norm_g):
    heads, dk, dv = MLSTM_HEADS, MLSTM_DK, MLSTM_DV
    n_qkvo = 2 * heads * dk + 2 * heads * dv
    n_gate = 4 * heads
    width = RNN_BLOCKS * RNN_BLOCK
    w_in_t = jnp.transpose(w_in)
    w_main_t = jnp.concatenate([w_in_t[:n_qkvo], w_in_t[n_qkvo + n_gate:]], axis=0).astype(BF16)
    w_gate_t = w_in_t[n_qkvo:n_qkvo + n_gate].astype(BF16)
    z, zg_t = norm_matmul(x, norm_g, w_main_t, w_side_t=w_gate_t, w_transposed=True,
                          name="in_proj")
    z = z.reshape(bsz, seq, -1)

    chunk = MLSTM_CHUNK
    nc = seq // chunk
    gates_rows = jnp.transpose(zg_t[:n_gate].reshape(4, heads, bsz, nc, chunk), (2, 1, 0, 3, 4))
    gate_b_rows = jnp.broadcast_to(
        jnp.transpose(gate_b.astype(F32).reshape(4, heads))[:, :, None, None], (heads, 4, 1, chunk))
    y_a = mlstm_mixer(z, gates_rows, gate_b_rows, head_g.astype(F32).reshape(heads, 1, dv), chunk)

    y_b = rglru_mixer(z, n_qkvo // RNN_BLOCK, (n_qkvo + width) // RNN_BLOCK,
                      conv_w.astype(F32), conv_b.astype(F32).reshape(1, width),
                      (0.5 * rg_wa).astype(BF16), (0.5 * rg_wx).astype(BF16),
                      0.5 * rg_ba.astype(F32), 0.5 * rg_bx.astype(F32), rg_lam.astype(F32))

    w_out_b = w_out.astype(BF16)
    m = bsz * seq
    assert heads * dv == width
    return matmul([(y_a.reshape(m, -1), w_out_b, 0),
                   (y_b.reshape(m, -1), w_out_b, 1)], res=x, norm_g=next_norm_g,
                  tm=MIXER_OUT_TM, tn=w_out.shape[1], name="out_proj")


def _odd_layer(x, h, bsz, seq, norm_g, w_qkv, w_o, next_norm_g):
    if h is None:
        qkv = norm_matmul(x, norm_g, w_qkv.astype(BF16), tm=256, name="qkv_proj")
    else:
        qkv = matmul([(h, w_qkv.astype(BF16))], tm=512, tn=w_qkv.shape[1], name="qkv_proj")
    qkv = qkv.reshape(bsz, seq, -1)
    slopes = jnp.exp2(-ALIBI_MAX_BIAS * jnp.arange(1, ATTN_HEADS + 1, dtype=F32) / ATTN_HEADS)
    slopes = jnp.broadcast_to(slopes[:, None, None], (ATTN_HEADS, SUBLANES, LANES))
    o = dilated_attention(qkv, slopes)
    return matmul([(o.reshape(bsz * seq, -1), w_o.astype(BF16))], res=x, norm_g=next_norm_g,
                  tm=MIXER_OUT_TM, tn=w_o.shape[1], name="attn_out_proj")


def kernel(x, e_norm, e_w_in, e_gate_b, e_conv_w, e_conv_b, e_rg_wa, e_rg_ba, e_rg_wx, e_rg_bx,
           e_rg_lam, e_head_g, e_w_out, o_norm, o_w_qkv, o_w_o, f_norm, f_w1, f_w3, f_w2,
           final_norm):
    bsz, seq, d = x.shape
    depth = f_norm.shape[0]
    xs = x.reshape(bsz * seq, d).astype(F32)
    w1, w3, w2 = f_w1.astype(BF16), f_w3.astype(BF16), f_w2.astype(BF16)
    hs = None
    for l in range(depth):
        if l % 2 == 0:
            e = l // 2
            xs, hs = _even_layer(xs, bsz, seq, e_norm[e], e_w_in[e], e_gate_b[e], e_conv_w[e],
                                 e_conv_b[e], e_rg_wa[e], e_rg_ba[e], e_rg_wx[e], e_rg_bx[e],
                                 e_rg_lam[e], e_head_g[e], e_w_out[e], f_norm[l])
        else:
            o = l // 2
            xs, hs = _odd_layer(xs, hs, bsz, seq, o_norm[o], o_w_qkv[o], o_w_o[o], f_norm[l])
        if l == depth - 1:
            out = _ffn(xs, hs, w1, w3, w2, l, final_norm, last=True)
            return out.astype(x.dtype).reshape(bsz, seq, d)
        if (l + 1) % 2:
            xs, hs = _ffn(xs, hs, w1, w3, w2, l, o_norm[(l + 1) // 2], last=False)
        else:
            xs, hs = _ffn(xs, hs, w1, w3, w2, l, None, last=False), None
```

```python
import functools
import math

import jax
import jax.numpy as jnp
from jax import lax
from jax.experimental import pallas as pl
from jax.experimental.pallas import tpu as pltpu

F32 = jnp.float32
BF16 = jnp.bfloat16

RMS_EPS = 1e-6
NEG_BIG = -1e30

MLSTM_HEADS = 4
MLSTM_DK = 128
MLSTM_DV = 256
MLSTM_CHUNK = 256
RNN_BLOCKS = 8
RNN_BLOCK = 128
CONV_WIDTH = 4
CONV_LEFT = 2
RGLRU_C = 8.0
ATTN_HEADS = 16
ATTN_DH = 128
ATTN_HALF = 64
ATTN_DILATIONS = (1, 4, 16)
ALIBI_MAX_BIAS = 8.0

LANES = 128
SUBLANES = 8
VMEM_LIMIT_BYTES = 56 * 1024 * 1024

RESIDENT_TM = 512
SWIGLU_TM = 4096
SWIGLU_TN = 512
SWIGLU_ROWS = 1024
NORM_CHUNK = 256


def _cparams(*semantics):
    return pltpu.CompilerParams(dimension_semantics=semantics,
                                vmem_limit_bytes=VMEM_LIMIT_BYTES)


def _weight_spec(w, kdim, tn, row_block=0, layer=None, **mode):
    if w.ndim == 2:
        return pl.BlockSpec((kdim, tn), lambda i, j: (row_block, j), **mode)
    return pl.BlockSpec((None, kdim, tn), lambda i, j: (layer, row_block, j), **mode)


def _matmul_body(*refs, n_pairs, has_res, has_norm, keep_sum):
    pairs = [(refs[2 * p], refs[2 * p + 1]) for p in range(n_pairs)]
    rest = list(refs[2 * n_pairs:])
    r_ref = rest.pop(0) if has_res else None
    g_ref = rest.pop(0) if has_norm else None
    y_ref = rest.pop(0) if keep_sum else None
    h_ref = rest.pop(0) if has_norm else None

    tm = pairs[0][0].shape[0]
    rows = min(NORM_CHUNK, tm) if has_norm else tm
    for r0 in range(0, tm, rows):
        sl = pl.ds(r0, rows)
        acc = None
        for a_ref, w_ref in pairs:
            d = jnp.dot(a_ref[sl, :], w_ref[...], preferred_element_type=F32)
            acc = d if acc is None else acc + d
        if has_res:
            acc = acc + r_ref[sl, :]
        if keep_sum:
            y_ref[sl, :] = acc
        if has_norm:
            ms = jnp.mean(acc * acc, axis=-1, keepdims=True)
            h_ref[sl, :] = (acc * lax.rsqrt(ms + RMS_EPS) * g_ref[...]).astype(h_ref.dtype)


def matmul(pairs, *, tm, tn, res=None, norm_g=None, norm_dtype=BF16, keep_sum=True,
           name="matmul"):
    m = pairs[0][0].shape[0]
    n = pairs[0][1].shape[-1]
    tn = min(tn, n)
    assert m % tm == 0 and n % tn == 0 and (keep_sum or norm_g is not None)
    assert norm_g is None or (tn == n and tm % NORM_CHUNK == 0)
    w_mode = dict(pipeline_mode=pl.Buffered(1)) if tn == n else {}
    in_specs, args = [], []
    for a, w, *where in pairs:
        kdim = a.shape[1]
        kb, layer = (list(where) + [0, None])[:2] if where else (0, None)
        in_specs.append(pl.BlockSpec((tm, kdim), lambda i, j: (i, 0)))
        in_specs.append(_weight_spec(w, kdim, tn, kb, layer, **w_mode))
        args += [a, w]
    if res is not None:
        in_specs.append(pl.BlockSpec((tm, tn), lambda i, j: (i, j)))
        args.append(res)
    out_specs, out_shape = [], []
    if keep_sum:
        out_specs.append(pl.BlockSpec((tm, tn), lambda i, j: (i, j)))
        out_shape.append(jax.ShapeDtypeStruct((m, n), F32))
    if norm_g is not None:
        in_specs.append(pl.BlockSpec((1, n), lambda i, j: (0, 0)))
        args.append(norm_g.reshape(1, n).astype(F32))
        out_specs.append(pl.BlockSpec((tm, n), lambda i, j: (i, 0)))
        out_shape.append(jax.ShapeDtypeStruct((m, n), norm_dtype))
    outs = pl.pallas_call(
        functools.partial(_matmul_body, n_pairs=len(pairs), has_res=res is not None,
                          has_norm=norm_g is not None, keep_sum=keep_sum),
        grid=(m // tm, n // tn),
        in_specs=in_specs,
        out_specs=out_specs,
        out_shape=out_shape,
        compiler_params=_cparams("parallel", "parallel"),
        name=name,
    )(*args)
    return outs if len(outs) > 1 else outs[0]


def _swiglu_up_body(a_ref, w1_ref, w3_ref, o_ref):
    rows = min(SWIGLU_ROWS, a_ref.shape[0])
    for r0 in range(0, a_ref.shape[0], rows):
        a = a_ref[pl.ds(r0, rows), :]
        u = jnp.dot(a, w1_ref[...], preferred_element_type=F32)
        v = jnp.dot(a, w3_ref[...], preferred_element_type=F32)
        o_ref[pl.ds(r0, rows), :] = (u * jax.nn.sigmoid(u) * v).astype(o_ref.dtype)


def swiglu_up(a, w1, w3, layer, tm=SWIGLU_TM, tn=SWIGLU_TN):
    m, kdim = a.shape
    n = w1.shape[-1]
    assert m % tm == 0 and n % tn == 0
    return pl.pallas_call(
        _swiglu_up_body,
        grid=(m // tm, n // tn),
        in_specs=[pl.BlockSpec((tm, kdim), lambda i, j: (i, 0)),
                  _weight_spec(w1, kdim, tn, layer=layer),
                  _weight_spec(w3, kdim, tn, layer=layer)],
        out_specs=pl.BlockSpec((tm, tn), lambda i, j: (i, j)),
        out_shape=jax.ShapeDtypeStruct((m, n), BF16),
        compiler_params=_cparams("parallel", "parallel"),
        name="swiglu_up",
    )(a, w1, w3)


def _norm_matmul_body(*refs, has_side, w_transposed):
    if has_side:
        x_ref, g_ref, w_ref, ws_ref, o_ref, os_ref = refs
    else:
        x_ref, g_ref, w_ref, o_ref = refs
    w_dims = (((1,), (1,)), ((), ())) if w_transposed else (((1,), (0,)), ((), ()))

    g = g_ref[...]
    for r0 in range(0, x_ref.shape[0], NORM_CHUNK):
        sl = pl.ds(r0, NORM_CHUNK)
        x = x_ref[sl, :]
        ms = jnp.mean(x * x, axis=-1, keepdims=True)
        h = (x * lax.rsqrt(ms + RMS_EPS) * g).astype(BF16)
        o_ref[sl, :] = lax.dot_general(h, w_ref[...], w_dims,
                                       preferred_element_type=F32).astype(o_ref.dtype)
        if has_side:
            os_ref[:, sl] = lax.dot_general(ws_ref[...], h, (((1,), (1,)), ((), ())),
                                            preferred_element_type=F32)


def norm_matmul(x, g, w, w_side_t=None, w_transposed=False, out_dtype=F32, tm=RESIDENT_TM,
                name="norm_matmul"):
    m, d = x.shape
    n = w.shape[0] if w_transposed else w.shape[1]
    assert m % tm == 0 and tm % NORM_CHUNK == 0
    resident = dict(pipeline_mode=pl.Buffered(1))
    in_specs = [pl.BlockSpec((tm, d), lambda i: (i, 0)),
                pl.BlockSpec((1, d), lambda i: (0, 0)),
                pl.BlockSpec(w.shape, lambda i: (0, 0), **resident)]
    args = [x, g.reshape(1, d).astype(F32), w]
    out_specs = [pl.BlockSpec((tm, n), lambda i: (i, 0))]
    out_shape = [jax.ShapeDtypeStruct((m, n), out_dtype)]
    if w_side_t is not None:
        ns = w_side_t.shape[0]
        in_specs.append(pl.BlockSpec((ns, d), lambda i: (0, 0), **resident))
        args.append(w_side_t)
        out_specs.append(pl.BlockSpec((ns, tm), lambda i: (0, i)))
        out_shape.append(jax.ShapeDtypeStruct((ns, m), F32))
    outs = pl.pallas_call(
        functools.partial(_norm_matmul_body, has_side=w_side_t is not None,
                          w_transposed=w_transposed),
        grid=(m // tm,),
        in_specs=in_specs,
        out_specs=out_specs,
        out_shape=out_shape,
        compiler_params=_cparams("parallel"),
        name=name,
    )(*args)
    return outs if w_side_t is not None else outs[0]


def _log_sigmoid(x):
    return jnp.minimum(x, 0.0) - jnp.log1p(jnp.exp(-jnp.abs(x)))


SCAN_RADIX = 4


def _lane_scan(x, combine, identity, reverse):
    width = x.shape[-1]
    lane = lax.broadcasted_iota(jnp.int32, x.shape, 1)

    def shifted(v, sh):
        if reverse:
            return jnp.where(lane < width - sh, pltpu.roll(v, width - sh, axis=1), identity)
        return jnp.where(lane >= sh, pltpu.roll(v, sh, axis=1), identity)

    sh = 1
    while sh < width:
        parts = [shifted(x, k * sh) for k in range(1, SCAN_RADIX) if k * sh < width]
        for p in parts:
            x = combine(x, p)
        sh *= SCAN_RADIX
    return x


def _mlstm_body(q_ref, k_ref, v_ref, og_ref, g_ref, gb_ref, hg_ref, y_ref,
                rows_ref, cols_ref, hf_ref, hb_ref, ct_ref, *, nc, chunk):
    dk, dv = MLSTM_DK, MLSTM_DV
    scale = dk ** -0.5

    g = g_ref[0, 0] + gb_ref[0]
    rows = []
    for d in range(2):
        b = _lane_scan(_log_sigmoid(g[2 * d + 1]), jnp.add, 0.0, reverse=d == 1)
        gg = g[2 * d] - b
        gmax = _lane_scan(gg, jnp.maximum, -jnp.inf, reverse=d == 1)
        rows += [gg, b, gmax]
        for j, r in enumerate((gg, b, gmax)):
            rows_ref[3 * d + j] = r
    rowmat = jnp.concatenate(rows + [jnp.zeros((LANES - 6 * nc, chunk), F32)], axis=0)
    colmat = rowmat.T
    for c in range(nc):
        cols_ref[c] = colmat if c == 0 else pltpu.roll(colmat, LANES - c, axis=1)

    ct_ref[...] = jnp.zeros_like(ct_ref)

    row_id = lax.broadcasted_iota(jnp.int32, (chunk, chunk), 0)
    col_id = lax.broadcasted_iota(jnp.int32, (chunk, chunk), 1)
    causal = (row_id >= col_id, row_id <= col_id)
    ones = jnp.ones((chunk, LANES), BF16)

    def lanes2(x):
        return jnp.concatenate([x] * (dv // LANES), axis=1)

    hg = hg_ref[0]

    def emit(r0, h, own_ref, other_ref):
        rows = pl.ds(r0, chunk)
        if other_ref is None:
            own_ref[rows, :] = h
            return
        hm = h + other_ref[rows, :]
        ms = jnp.mean(hm * hm, axis=-1, keepdims=True)
        hn = hm * lax.rsqrt(ms + RMS_EPS) * hg
        y_ref[0, rows, :] = (hn * jax.nn.sigmoid(og_ref[0, rows, :])).astype(y_ref.dtype)

    def chunk_step(c, m, d, own_ref, other_ref):
        r0 = pl.multiple_of(c * chunk, chunk)
        qb = (q_ref[0, pl.ds(r0, chunk), :] * scale).astype(BF16)
        kb = k_ref[0, pl.ds(r0, chunk), :].astype(BF16)
        vf = v_ref[0, pl.ds(r0, chunk), :]
        cols = cols_ref[c]

        def column(j):
            lane = (3 * d + j) * nc
            return jnp.broadcast_to(cols[:, lane:lane + 1], (chunk, LANES))

        gg_col, b_col, gmax_col = column(0), column(1), column(2)
        gg_row = rows_ref[3 * d, pl.ds(c, 1), :]
        end = chunk - 1 if d == 0 else 0
        total = rows_ref[3 * d + 1, pl.ds(c, 1), :][:, end:end + 1]
        gmax_end = rows_ref[3 * d + 2, pl.ds(c, 1), :][:, end:end + 1]

        m_row = jnp.maximum(gmax_col, m)
        w_intra = jnp.exp(jnp.where(causal[d], gg_row - lanes2(m_row), NEG_BIG))
        w_inter = jnp.exp(m - m_row)
        s = lax.dot_general(qb, kb, (((1,), (1,)), ((), ())),
                            preferred_element_type=F32) * w_intra
        ct = ct_ref[d]
        intra = jnp.dot(s.astype(BF16), jnp.concatenate([vf.astype(BF16), ones], axis=1),
                        preferred_element_type=F32)
        inter = jnp.dot(qb, ct.astype(BF16), preferred_element_type=F32)
        num = intra[:, :dv] + lanes2(w_inter) * inter[:, :dv]
        den = intra[:, dv:] + w_inter * inter[:, dv:]
        inv = 1.0 / jnp.maximum(jnp.abs(den), jnp.exp(-(b_col + m_row)))
        emit(r0, num * lanes2(inv), own_ref, other_ref)

        m_new = jnp.maximum(total + m, total + gmax_end)
        w_src = jnp.exp(total + gg_col - m_new)
        decay = jnp.exp(total + m - m_new)
        wv = jnp.concatenate([lanes2(w_src) * vf, w_src], axis=1).astype(BF16)
        ct_ref[d] = decay * ct + lax.dot_general(
            kb, wv, (((0,), (0,)), ((), ())), preferred_element_type=F32)
        return m_new

    def first_half(c, carry):
        m_f, m_b = carry
        m_f = chunk_step(c, m_f, 0, hf_ref, None)
        m_b = chunk_step(nc - 1 - c, m_b, 1, hb_ref, None)
        return m_f, m_b

    def second_half(c, carry):
        m_f, m_b = carry
        m_f = chunk_step(c, m_f, 0, None, hb_ref)
        m_b = chunk_step(nc - 1 - c, m_b, 1, None, hf_ref)
        return m_f, m_b

    m0 = jnp.full((1, 1), NEG_BIG, F32)
    carry = lax.fori_loop(0, nc // 2, first_half, (m0, m0), unroll=2)
    lax.fori_loop(nc // 2, nc, second_half, carry, unroll=2)


def mlstm_mixer(z, gates_rows, gate_b_rows, head_g, chunk=MLSTM_CHUNK):
    b, s, _ = z.shape
    h, dk, dv = MLSTM_HEADS, MLSTM_DK, MLSTM_DV
    nc = s // chunk
    assert s % chunk == 0 and 6 * nc <= LANES and chunk % LANES == 0 and nc % 2 == 0
    kq = h * dk // dk
    kv = 2 * h * dk // dv
    ko = kv + h
    return pl.pallas_call(
        functools.partial(_mlstm_body, nc=nc, chunk=chunk),
        grid=(b, h),
        in_specs=[
            pl.BlockSpec((1, s, dk), lambda i, j: (i, 0, j)),
            pl.BlockSpec((1, s, dk), lambda i, j: (i, 0, kq + j)),
            pl.BlockSpec((1, s, dv), lambda i, j: (i, 0, kv + j)),
            pl.BlockSpec((1, s, dv), lambda i, j: (i, 0, ko + j)),
            pl.BlockSpec((1, 1, 4, nc, chunk), lambda i, j: (i, j, 0, 0, 0)),
            pl.BlockSpec((1, 4, 1, chunk), lambda i, j: (j, 0, 0, 0)),
            pl.BlockSpec((1, 1, dv), lambda i, j: (j, 0, 0)),
        ],
        out_specs=pl.BlockSpec((1, s, dv), lambda i, j: (i, 0, j)),
        out_shape=jax.ShapeDtypeStruct((b, s, h * dv), BF16),
        scratch_shapes=[
            pltpu.VMEM((6, nc, chunk), F32),
            pltpu.VMEM((nc, chunk, LANES), F32),
            pltpu.VMEM((s, dv), F32),
            pltpu.VMEM((s, dv), F32),
            pltpu.VMEM((2, dk, dv + LANES), F32),
        ],
        compiler_params=_cparams("parallel", "parallel"),
        name="mlstm",
    )(z, z, z, z, gates_rows, gate_b_rows, head_g)


RG_TILE = 512
RG_PAD = SUBLANES


def _softplus(x):
    return jnp.maximum(x, 0.0) + jnp.log1p(jnp.exp(-jnp.abs(x)))


def _gelu_tanh(x):
    c = math.sqrt(2.0 / math.pi)
    return x * (0.5 * (1.0 + jnp.tanh(c * (x + 0.044715 * (x * x * x)))))


def _rglru_body(x_ref, gr_ref, cw_ref, cb_ref, wa_ref, wx_ref, ba_ref, bx_ref, lam_ref, y_ref,
                xpad_ref, a_ref, u_ref, h_ref, *, s):
    nt = s // RG_TILE
    zeros = jnp.zeros((RG_PAD, LANES), F32)
    xpad_ref[pl.ds(0, RG_PAD), :] = zeros
    xpad_ref[pl.ds(RG_PAD + s, RG_PAD), :] = zeros

    def copy_in(t, carry):
        r0 = pl.multiple_of(t * RG_TILE, RG_TILE)
        xpad_ref[pl.ds(RG_PAD + r0, RG_TILE), :] = x_ref[0, pl.ds(r0, RG_TILE), :]
        return carry

    lax.fori_loop(0, nt, copy_in, 0)

    cw = cw_ref[...]
    cb = cb_ref[...]
    half_sp = [0.5 * RGLRU_C * _softplus(-lam_ref[d:d + 1, :]) for d in range(2)]

    def gates(t, carry):
        r0 = pl.multiple_of(t * RG_TILE, RG_TILE)
        xc = None
        for j in range(CONV_WIDTH):
            tap = xpad_ref[pl.ds(r0 + RG_PAD - CONV_LEFT + j, RG_TILE), :] * cw[j:j + 1, :]
            xc = tap if xc is None else xc + tap
        xc = xc + cb
        xcb = xc.astype(BF16)
        half_xc = 0.5 * xc
        for d in range(2):
            t_r = jnp.tanh(jnp.dot(xcb, wa_ref[d, 0], preferred_element_type=F32)
                           + ba_ref[d:d + 1, :])
            t_i = jnp.tanh(jnp.dot(xcb, wx_ref[d, 0], preferred_element_type=F32)
                           + bx_ref[d:d + 1, :])
            neg_log_a = half_sp[d] * t_r + half_sp[d]
            gated_x = half_xc * t_i + half_xc
            a = jnp.exp(-neg_log_a)
            one_minus_a2 = jnp.tanh(neg_log_a) * (1.0 + a * a)
            root = jnp.where(one_minus_a2 > 0.0, one_minus_a2 * lax.rsqrt(one_minus_a2), 0.0)
            a_ref[d, pl.ds(r0, RG_TILE), :] = a
            u_ref[d, pl.ds(r0, RG_TILE), :] = root * gated_x
        return carry

    lax.fori_loop(0, nt, gates, 0)

    nb = RG_TILE // SUBLANES
    sub = lax.broadcasted_iota(jnp.int32, (SUBLANES, LANES), 0)

    def compose_sublanes(a, u, reverse):
        sh = 1
        while sh < SUBLANES:
            keep = sub < SUBLANES - sh if reverse else sub >= sh
            amount = SUBLANES - sh if reverse else sh
            a_s = jnp.where(keep, pltpu.roll(a, amount, axis=0), 1.0)
            u_s = jnp.where(keep, pltpu.roll(u, amount, axis=0), 0.0)
            u = a * u_s + u
            a = a * a_s
            sh *= 2
        return a, u

    def tile_recurrence(t, d, carry):
        r0 = pl.multiple_of(t * RG_TILE, RG_TILE)
        reverse = d == 1
        a_rows, u_rows = [None] * SUBLANES, [None] * SUBLANES
        a_prev = u_prev = None
        for r in (reversed(range(SUBLANES)) if reverse else range(SUBLANES)):
            rows = pl.ds(r0 + r, nb, stride=SUBLANES)
            a, u = a_ref[d, rows, :], u_ref[d, rows, :]
            if a_prev is not None:
                u = a * u_prev + u
                a = a * a_prev
            a_rows[r], u_rows[r] = a, u
            a_prev, u_prev = a, u
        entry = [None] * (nb // SUBLANES)
        for v in (reversed(range(nb // SUBLANES)) if reverse else range(nb // SUBLANES)):
            blk = slice(v * SUBLANES, (v + 1) * SUBLANES)
            sa, su = compose_sublanes(a_prev[blk], u_prev[blk], reverse)
            after = sa * carry + su
            if reverse:
                entry[v] = jnp.where(sub < SUBLANES - 1, pltpu.roll(after, SUBLANES - 1, axis=0),
                                     carry)
                carry = after[0:1]
            else:
                entry[v] = jnp.where(sub >= 1, pltpu.roll(after, 1, axis=0), carry)
                carry = after[SUBLANES - 1:SUBLANES]
        h_entry = jnp.concatenate(entry, axis=0)
        for r in range(SUBLANES):
            h_ref[d, pl.ds(r0 + r, nb, stride=SUBLANES), :] = a_rows[r] * h_entry + u_rows[r]
        return carry

    def recurrences(t, carry):
        h_f, h_b = carry
        return tile_recurrence(t, 0, h_f), tile_recurrence(nt - 1 - t, 1, h_b)

    h0 = jnp.zeros((1, LANES), F32)
    lax.fori_loop(0, nt, recurrences, (h0, h0))

    def finish(t, carry):
        r0 = pl.multiple_of(t * RG_TILE, RG_TILE)
        hr = h_ref[0, pl.ds(r0, RG_TILE), :] + h_ref[1, pl.ds(r0, RG_TILE), :]
        y_ref[0, pl.ds(r0, RG_TILE), :] = (
            hr * _gelu_tanh(gr_ref[0, pl.ds(r0, RG_TILE), :])).astype(y_ref.dtype)
        return carry

    lax.fori_loop(0, nt, finish, 0)


def rglru_mixer(z, xr_block0, gr_block0, conv_w, conv_b, wa, wx, ba, bx, lam):
    b, s, _ = z.shape
    width = RNN_BLOCKS * RNN_BLOCK
    assert s % RG_TILE == 0
    return pl.pallas_call(
        functools.partial(_rglru_body, s=s),
        grid=(b, RNN_BLOCKS),
        in_specs=[
            pl.BlockSpec((1, s, RNN_BLOCK), lambda i, j: (i, 0, xr_block0 + j)),
            pl.BlockSpec((1, s, RNN_BLOCK), lambda i, j: (i, 0, gr_block0 + j)),
            pl.BlockSpec((CONV_WIDTH, RNN_BLOCK), lambda i, j: (0, j)),
            pl.BlockSpec((1, RNN_BLOCK), lambda i, j: (0, j)),
            pl.BlockSpec((2, 1, RNN_BLOCK, RNN_BLOCK), lambda i, j: (0, j, 0, 0)),
            pl.BlockSpec((2, 1, RNN_BLOCK, RNN_BLOCK), lambda i, j: (0, j, 0, 0)),
            pl.BlockSpec((2, RNN_BLOCK), lambda i, j: (0, j)),
            pl.BlockSpec((2, RNN_BLOCK), lambda i, j: (0, j)),
            pl.BlockSpec((2, RNN_BLOCK), lambda i, j: (0, j)),
        ],
        out_specs=pl.BlockSpec((1, s, RNN_BLOCK), lambda i, j: (i, 0, j)),
        out_shape=jax.ShapeDtypeStruct((b, s, width), BF16),
        scratch_shapes=[
            pltpu.VMEM((s + 2 * RG_PAD, RNN_BLOCK), F32),
            pltpu.VMEM((2, s, RNN_BLOCK), F32),
            pltpu.VMEM((2, s, RNN_BLOCK), F32),
            pltpu.VMEM((2, s, RNN_BLOCK), F32),
        ],
        compiler_params=_cparams("parallel", "parallel"),
        name="rglru",
    )(z, z, conv_w, conv_b, wa, wx, ba, bx, lam)


ATTN_TQ = 128
ATTN_TK = ATTN_TQ + 2 * ATTN_HALF
ATTN_UNROLL = 32
LOG2E = math.log2(math.e)


def _attn_body(q_ref, k_ref, v_ref, slope_ref, o_ref, bias_ref, x4_ref, og_ref, dg_ref, mg_ref,
               *, s):
    qscale = ATTN_DH ** -0.5 * LOG2E

    @pl.when(pl.program_id(1) == 0)
    def _():
        slope = slope_ref[0, 0:1, 0:1] * LOG2E
        qi = lax.broadcasted_iota(jnp.int32, (ATTN_TQ, ATTN_TK), 0)
        kj = lax.broadcasted_iota(jnp.int32, (ATTN_TQ, ATTN_TK), 1)
        for g, dil in enumerate(ATTN_DILATIONS):
            for e in range(3):
                rel = jnp.abs(kj - e * ATTN_HALF - qi)
                pen = slope * (rel * dil).astype(F32)
                bias_ref[g, e] = jnp.where(rel <= ATTN_HALF, -pen, NEG_BIG)

    s4 = s // 4
    srcs = (q_ref, k_ref, v_ref)

    def split4(t, carry):
        c = t // (s4 // ATTN_TK)
        p0 = (t % (s4 // ATTN_TK)) * ATTN_TK
        dst = pl.ds(pl.multiple_of(c * s4 + p0, ATTN_TK), ATTN_TK)
        for a in range(3):
            x = srcs[a][0, pl.ds(c + 4 * p0, ATTN_TK, stride=4), :]
            x4_ref[a, dst, :] = x * qscale if a == 0 else x
        return carry

    lax.fori_loop(0, 4 * (s4 // ATTN_TK), split4, 0, unroll=4)

    ones = jnp.ones((ATTN_TK, ATTN_DH), BF16)

    for g, dil in reversed(list(enumerate(ATTN_DILATIONS))):
        sp = s // dil
        nqb = sp // ATTN_TQ

        def q_block(t, carry, g=g, dil=dil, sp=sp, nqb=nqb):
            r = t // nqb
            p0 = (t % nqb) * ATTN_TQ
            kstart = jnp.clip(p0 - ATTN_HALF, 0, sp - ATTN_TK)
            e = (p0 - kstart) // ATTN_HALF
            if dil == 1:
                qrows = pl.ds(pl.multiple_of(p0, ATTN_TQ), ATTN_TQ)
                krows = pl.ds(pl.multiple_of(kstart, ATTN_HALF), ATTN_TK)
                qf = q_ref[0, qrows, :] * qscale
                kf, vf = k_ref[0, krows, :], v_ref[0, krows, :]
                orows = qrows
            elif dil == 4:
                qrows = pl.ds(pl.multiple_of(r * s4 + p0, ATTN_TQ), ATTN_TQ)
                krows = pl.ds(pl.multiple_of(r * s4 + kstart, ATTN_HALF), ATTN_TK)
                qf, kf, vf = x4_ref[0, qrows, :], x4_ref[1, krows, :], x4_ref[2, krows, :]
                orows = pl.ds(r + 4 * p0, ATTN_TQ, stride=4)
            else:
                base = (r % 4) * s4 + r // 4
                qrows = pl.ds(base + 4 * p0, ATTN_TQ, stride=4)
                krows = pl.ds(base + 4 * kstart, ATTN_TK, stride=4)
                qf, kf, vf = x4_ref[0, qrows, :], x4_ref[1, krows, :], x4_ref[2, krows, :]
                orows = pl.ds(r + dil * p0, ATTN_TQ, stride=dil)
            qb, kb, vb = qf.astype(BF16), kf.astype(BF16), vf.astype(BF16)
            sc = lax.dot_general(qb, kb, (((1,), (1,)), ((), ())),
                                 preferred_element_type=F32) + bias_ref[g, e]
            m = jnp.max(sc, axis=-1, keepdims=True)
            p = jnp.exp2(sc - m).astype(BF16)
            pv = jnp.dot(p, jnp.concatenate([vb, ones], axis=1), preferred_element_type=F32)
            acc, den = pv[:, :ATTN_DH], pv[:, ATTN_DH:]
            if dil != 1:
                og_ref[g - 1, orows, :] = acc
                dg_ref[g - 1, orows, :] = den
                mg_ref[g - 1, orows, :] = jnp.broadcast_to(m, den.shape)
                return carry
            m4, m16 = mg_ref[0, orows, :], mg_ref[1, orows, :]
            mx = jnp.maximum(jnp.maximum(m4, m16), m)
            e1, e4, e16 = jnp.exp2(m - mx), jnp.exp2(m4 - mx), jnp.exp2(m16 - mx)
            num = e1 * acc + e4 * og_ref[0, orows, :] + e16 * og_ref[1, orows, :]
            tot = e1 * den + e4 * dg_ref[0, orows, :] + e16 * dg_ref[1, orows, :]
            o_ref[0, orows, :] = (num * (1.0 / tot)).astype(o_ref.dtype)
            return carry

        lax.fori_loop(0, dil * nqb, q_block, 0, unroll=ATTN_UNROLL)


def dilated_attention(qkv, slopes):
    b, s, _ = qkv.shape
    h, dh = ATTN_HEADS, ATTN_DH
    assert s % (max(ATTN_DILATIONS) * ATTN_TK) == 0
    return pl.pallas_call(
        functools.partial(_attn_body, s=s),
        grid=(h, b),
        in_specs=[
            pl.BlockSpec((1, s, dh), lambda j, i: (i, 0, j)),
            pl.BlockSpec((1, s, dh), lambda j, i: (i, 0, h + j)),
            pl.BlockSpec((1, s, dh), lambda j, i: (i, 0, 2 * h + j)),
            pl.BlockSpec((1, SUBLANES, LANES), lambda j, i: (j, 0, 0)),
        ],
        out_specs=pl.BlockSpec((1, s, dh), lambda j, i: (i, 0, j)),
        out_shape=jax.ShapeDtypeStruct((b, s, h * dh), BF16),
        scratch_shapes=[
            pltpu.VMEM((len(ATTN_DILATIONS), 3, ATTN_TQ, ATTN_TK), F32),
            pltpu.VMEM((3, s, dh), F32),
            pltpu.VMEM((len(ATTN_DILATIONS) - 1, s, dh), F32),
            pltpu.VMEM((len(ATTN_DILATIONS) - 1, s, dh), F32),
            pltpu.VMEM((len(ATTN_DILATIONS) - 1, s, dh), F32),
        ],
        compiler_params=_cparams("parallel", "arbitrary"),
        name="dilated_attn",
    )(qkv, qkv, qkv, slopes)


def _ffn(x, h, w1, w3, w2, layer, next_norm_g, last):
    u = swiglu_up(h, w1, w3, layer)
    return matmul([(u, w2, 0, layer)], res=x, norm_g=next_norm_g,
                  norm_dtype=F32 if last else BF16, keep_sum=not last,
                  tm=RESIDENT_TM, tn=w2.shape[-1], name="ffn_down")


def _even_layer(x, bsz, seq, norm_g, w_in, gate_b, conv_w, conv_b, rg_wa, rg_ba, rg_wx, rg_bx,
                rg_lam, head_g, w_out, next_norm_g):
    heads, dk, dv = MLSTM_HEADS, MLSTM_DK, MLSTM_DV
    n_qkvo = 2 * heads * dk + 2 * heads * dv
    n_gate = 4 * heads
    width = RNN_BLOCKS * RNN_BLOCK
    w_in_t = jnp.transpose(w_in)
    w_main_t = jnp.concatenate([w_in_t[:n_qkvo], w_in_t[n_qkvo + n_gate:]], axis=0).astype(BF16)
    w_gate_t = w_in_t[n_qkvo:n_qkvo + n_gate].astype(BF16)
    z, zg_t = norm_matmul(x, norm_g, w_main_t, w_side_t=w_gate_t, w_transposed=True,
                          name="in_proj")
    z = z.reshape(bsz, seq, -1)

    chunk = MLSTM_CHUNK
    nc = seq // chunk
    gates_rows = jnp.transpose(zg_t[:n_gate].reshape(4, heads, bsz, nc, chunk), (2, 1, 0, 3, 4))
    gate_b_rows = jnp.broadcast_to(
        jnp.transpose(gate_b.astype(F32).reshape(4, heads))[:, :, None, None], (heads, 4, 1, chunk))
    y_a = mlstm_mixer(z, gates_rows, gate_b_rows, head_g.astype(F32).reshape(heads, 1, dv), chunk)

    y_b = rglru_mixer(z, n_qkvo // RNN_BLOCK, (n_qkvo + width) // RNN_BLOCK,
                      conv_w.astype(F32), conv_b.astype(F32).reshape(1, width),
                      (0.5 * rg_wa).astype(BF16), (0.5 * rg_wx).astype(BF16),
                      0.5 * rg_ba.astype(F32), 0.5 * rg_bx.astype(F32), rg_lam.astype(F32))

    w_out_b = w_out.astype(BF16)
    m = bsz * seq
    assert heads * dv == width
    return matmul([(y_a.reshape(m, -1), w_out_b, 0),
                   (y_b.reshape(m, -1), w_out_b, 1)], res=x, norm_g=next_norm_g,
                  tm=RESIDENT_TM, tn=w_out.shape[1], name="out_proj")


def _odd_layer(x, h, bsz, seq, norm_g, w_qkv, w_o, next_norm_g):
    if h is None:
        qkv = norm_matmul(x, norm_g, w_qkv.astype(BF16), tm=RESIDENT_TM // 2, name="qkv_proj")
    else:
        qkv = matmul([(h, w_qkv.astype(BF16))], tm=RESIDENT_TM, tn=w_qkv.shape[1],
                     name="qkv_proj")
    qkv = qkv.reshape(bsz, seq, -1)
    slopes = jnp.exp2(-ALIBI_MAX_BIAS * jnp.arange(1, ATTN_HEADS + 1, dtype=F32) / ATTN_HEADS)
    slopes = jnp.broadcast_to(slopes[:, None, None], (ATTN_HEADS, SUBLANES, LANES))
    o = dilated_attention(qkv, slopes)
    return matmul([(o.reshape(bsz * seq, -1), w_o.astype(BF16))], res=x, norm_g=next_norm_g,
                  tm=RESIDENT_TM, tn=w_o.shape[1], name="attn_out_proj")


def kernel(x, e_norm, e_w_in, e_gate_b, e_conv_w, e_conv_b, e_rg_wa, e_rg_ba, e_rg_wx, e_rg_bx,
           e_rg_lam, e_head_g, e_w_out, o_norm, o_w_qkv, o_w_o, f_norm, f_w1, f_w3, f_w2,
           final_norm):
    bsz, seq, d = x.shape
    depth = f_norm.shape[0]
    xs = x.reshape(bsz * seq, d).astype(F32)
    w1, w3, w2 = f_w1.astype(BF16), f_w3.astype(BF16), f_w2.astype(BF16)
    hs = None
    for l in range(depth):
        if l % 2 == 0:
            e = l // 2
            xs, hs = _even_layer(xs, bsz, seq, e_norm[e], e_w_in[e], e_gate_b[e], e_conv_w[e],
                                 e_conv_b[e], e_rg_wa[e], e_rg_ba[e], e_rg_wx[e], e_rg_bx[e],
                                 e_rg_lam[e], e_head_g[e], e_w_out[e], f_norm[l])
        else:
            o = l // 2
            xs, hs = _odd_layer(xs, hs, bsz, seq, o_norm[o], o_w_qkv[o], o_w_o[o], f_norm[l])
        if l == depth - 1:
            out = _ffn(xs, hs, w1, w3, w2, l, final_norm, last=True)
            return out.astype(x.dtype).reshape(bsz, seq, d)
        if (l + 1) % 2:
            xs, hs = _ffn(xs, hs, w1, w3, w2, l, o_norm[(l + 1) // 2], last=False)
        else:
            xs, hs = _ffn(xs, hs, w1, w3, w2, l, None, last=False), None
```

```python
import functools
import math

import jax
import jax.numpy as jnp
from jax import lax
from jax.experimental import pallas as pl
from jax.experimental.pallas import tpu as pltpu

F32 = jnp.float32
BF16 = jnp.bfloat16

RMS_EPS = 1e-6
NEG_BIG = -1e30
LOG2E = math.log2(math.e)

MLSTM_HEADS = 4
MLSTM_DK = 128
MLSTM_DV = 256
MLSTM_CHUNK = 256
RNN_BLOCKS = 8
RNN_BLOCK = 128
CONV_WIDTH = 4
CONV_LEFT = 2
RGLRU_C = 8.0
ATTN_HEADS = 16
ATTN_DH = 128
ATTN_HALF = 64
ATTN_DILATIONS = (1, 4, 16)
ALIBI_MAX_BIAS = 8.0

LANES = 128
SUBLANES = 8
VMEM_LIMIT_BYTES = 56 * 1024 * 1024

RESIDENT_TM = 512
SWIGLU_TM = 4096
SWIGLU_TN = 512
SWIGLU_ROWS = 1024
NORM_CHUNK = 256


def _cparams(*semantics):
    return pltpu.CompilerParams(dimension_semantics=semantics,
                                vmem_limit_bytes=VMEM_LIMIT_BYTES)


def _weight_spec(w, kdim, tn, row_block=0, layer=None, **mode):
    if w.ndim == 2:
        return pl.BlockSpec((kdim, tn), lambda i, j: (row_block, j), **mode)
    return pl.BlockSpec((None, kdim, tn), lambda i, j: (layer, row_block, j), **mode)


def _matmul_body(*refs, n_pairs, has_res, has_norm, keep_sum):
    pairs = [(refs[2 * p], refs[2 * p + 1]) for p in range(n_pairs)]
    rest = list(refs[2 * n_pairs:])
    r_ref = rest.pop(0) if has_res else None
    g_ref = rest.pop(0) if has_norm else None
    y_ref = rest.pop(0) if keep_sum else None
    h_ref = rest.pop(0) if has_norm else None

    tm = pairs[0][0].shape[0]
    rows = min(NORM_CHUNK, tm) if has_norm else tm
    for r0 in range(0, tm, rows):
        sl = pl.ds(r0, rows)
        acc = None
        for a_ref, w_ref in pairs:
            d = jnp.dot(a_ref[sl, :], w_ref[...], preferred_element_type=F32)
            acc = d if acc is None else acc + d
        if has_res:
            acc = acc + r_ref[sl, :]
        if keep_sum:
            y_ref[sl, :] = acc
        if has_norm:
            ms = jnp.mean(acc * acc, axis=-1, keepdims=True)
            h_ref[sl, :] = (acc * lax.rsqrt(ms + RMS_EPS) * g_ref[...]).astype(h_ref.dtype)


def matmul(pairs, *, tm, tn, res=None, norm_g=None, norm_dtype=BF16, keep_sum=True,
           name="matmul"):
    m = pairs[0][0].shape[0]
    n = pairs[0][1].shape[-1]
    tn = min(tn, n)
    assert m % tm == 0 and n % tn == 0 and (keep_sum or norm_g is not None)
    assert norm_g is None or (tn == n and tm % NORM_CHUNK == 0)
    w_mode = dict(pipeline_mode=pl.Buffered(1)) if tn == n else {}
    in_specs, args = [], []
    for a, w, *where in pairs:
        kdim = a.shape[1]
        kb, layer = (list(where) + [0, None])[:2] if where else (0, None)
        in_specs.append(pl.BlockSpec((tm, kdim), lambda i, j: (i, 0)))
        in_specs.append(_weight_spec(w, kdim, tn, kb, layer, **w_mode))
        args += [a, w]
    if res is not None:
        in_specs.append(pl.BlockSpec((tm, tn), lambda i, j: (i, j)))
        args.append(res)
    out_specs, out_shape = [], []
    if keep_sum:
        out_specs.append(pl.BlockSpec((tm, tn), lambda i, j: (i, j)))
        out_shape.append(jax.ShapeDtypeStruct((m, n), F32))
    if norm_g is not None:
        in_specs.append(pl.BlockSpec((1, n), lambda i, j: (0, 0)))
        args.append(norm_g.reshape(1, n).astype(F32))
        out_specs.append(pl.BlockSpec((tm, n), lambda i, j: (i, 0)))
        out_shape.append(jax.ShapeDtypeStruct((m, n), norm_dtype))
    outs = pl.pallas_call(
        functools.partial(_matmul_body, n_pairs=len(pairs), has_res=res is not None,
                          has_norm=norm_g is not None, keep_sum=keep_sum),
        grid=(m // tm, n // tn),
        in_specs=in_specs,
        out_specs=out_specs,
        out_shape=out_shape,
        compiler_params=_cparams("parallel", "parallel"),
        name=name,
    )(*args)
    return outs if len(outs) > 1 else outs[0]


def _swiglu_up_body(a_ref, w1_ref, w3_ref, o_ref):
    rows = min(SWIGLU_ROWS, a_ref.shape[0])
    for r0 in range(0, a_ref.shape[0], rows):
        a = a_ref[pl.ds(r0, rows), :]
        u = jnp.dot(a, w1_ref[...], preferred_element_type=F32)
        v = jnp.dot(a, w3_ref[...], preferred_element_type=F32)
        o_ref[pl.ds(r0, rows), :] = (u * jax.nn.sigmoid(u) * v).astype(o_ref.dtype)


def swiglu_up(a, w1, w3, layer, tm=SWIGLU_TM, tn=SWIGLU_TN):
    m, kdim = a.shape
    n = w1.shape[-1]
    assert m % tm == 0 and n % tn == 0
    return pl.pallas_call(
        _swiglu_up_body,
        grid=(m // tm, n // tn),
        in_specs=[pl.BlockSpec((tm, kdim), lambda i, j: (i, 0)),
                  _weight_spec(w1, kdim, tn, layer=layer),
                  _weight_spec(w3, kdim, tn, layer=layer)],
        out_specs=pl.BlockSpec((tm, tn), lambda i, j: (i, j)),
        out_shape=jax.ShapeDtypeStruct((m, n), BF16),
        compiler_params=_cparams("parallel", "parallel"),
        name="swiglu_up",
    )(a, w1, w3)


def _norm_matmul_body(*refs, has_side, w_transposed):
    if has_side:
        x_ref, g_ref, w_ref, ws_ref, o_ref, os_ref = refs
    else:
        x_ref, g_ref, w_ref, o_ref = refs
    w_dims = (((1,), (1,)), ((), ())) if w_transposed else (((1,), (0,)), ((), ()))

    g = g_ref[...]
    for r0 in range(0, x_ref.shape[0], NORM_CHUNK):
        sl = pl.ds(r0, NORM_CHUNK)
        x = x_ref[sl, :]
        ms = jnp.mean(x * x, axis=-1, keepdims=True)
        h = (x * lax.rsqrt(ms + RMS_EPS) * g).astype(BF16)
        o_ref[sl, :] = lax.dot_general(h, w_ref[...], w_dims,
                                       preferred_element_type=F32).astype(o_ref.dtype)
        if has_side:
            os_ref[:, sl] = lax.dot_general(ws_ref[...], h, (((1,), (1,)), ((), ())),
                                            preferred_element_type=F32)


def norm_matmul(x, g, w, w_side_t=None, w_transposed=False, out_dtype=F32, tm=RESIDENT_TM,
                name="norm_matmul"):
    m, d = x.shape
    n = w.shape[0] if w_transposed else w.shape[1]
    assert m % tm == 0 and tm % NORM_CHUNK == 0
    resident = dict(pipeline_mode=pl.Buffered(1))
    in_specs = [pl.BlockSpec((tm, d), lambda i: (i, 0)),
                pl.BlockSpec((1, d), lambda i: (0, 0)),
                pl.BlockSpec(w.shape, lambda i: (0, 0), **resident)]
    args = [x, g.reshape(1, d).astype(F32), w]
    out_specs = [pl.BlockSpec((tm, n), lambda i: (i, 0))]
    out_shape = [jax.ShapeDtypeStruct((m, n), out_dtype)]
    if w_side_t is not None:
        ns = w_side_t.shape[0]
        in_specs.append(pl.BlockSpec((ns, d), lambda i: (0, 0), **resident))
        args.append(w_side_t)
        out_specs.append(pl.BlockSpec((ns, tm), lambda i: (0, i)))
        out_shape.append(jax.ShapeDtypeStruct((ns, m), F32))
    outs = pl.pallas_call(
        functools.partial(_norm_matmul_body, has_side=w_side_t is not None,
                          w_transposed=w_transposed),
        grid=(m // tm,),
        in_specs=in_specs,
        out_specs=out_specs,
        out_shape=out_shape,
        compiler_params=_cparams("parallel"),
        name=name,
    )(*args)
    return outs if w_side_t is not None else outs[0]


def _log_sigmoid(x):
    return jnp.minimum(x, 0.0) - jnp.log1p(jnp.exp(-jnp.abs(x)))


SCAN_RADIX = 4


def _lane_scan(x, combine, identity, reverse):
    width = x.shape[-1]
    lane = lax.broadcasted_iota(jnp.int32, x.shape, 1)

    def shifted(v, sh):
        if reverse:
            return jnp.where(lane < width - sh, pltpu.roll(v, width - sh, axis=1), identity)
        return jnp.where(lane >= sh, pltpu.roll(v, sh, axis=1), identity)

    sh = 1
    while sh < width:
        parts = [shifted(x, k * sh) for k in range(1, SCAN_RADIX) if k * sh < width]
        for p in parts:
            x = combine(x, p)
        sh *= SCAN_RADIX
    return x


def _mlstm_body(q_ref, k_ref, v_ref, og_ref, g_ref, gb_ref, hg_ref, y_ref,
                rows_ref, cols_ref, hf_ref, hb_ref, ct_ref, *, nc, chunk):
    dk, dv = MLSTM_DK, MLSTM_DV
    scale = dk ** -0.5

    g = g_ref[0, 0] + gb_ref[0]
    rows = []
    for d in range(2):
        b = _lane_scan(_log_sigmoid(g[2 * d + 1]), jnp.add, 0.0, reverse=d == 1)
        gg = g[2 * d] - b
        gmax = _lane_scan(gg, jnp.maximum, -jnp.inf, reverse=d == 1)
        scaled = [gg * LOG2E, b * LOG2E, gmax * LOG2E]
        rows += scaled
        for j, r in enumerate(scaled):
            rows_ref[3 * d + j] = r
    rowmat = jnp.concatenate(rows + [jnp.zeros((LANES - 6 * nc, chunk), F32)], axis=0)
    colmat = rowmat.T
    for c in range(nc):
        cols_ref[c] = colmat if c == 0 else pltpu.roll(colmat, LANES - c, axis=1)

    ct_ref[...] = jnp.zeros_like(ct_ref)

    row_id = lax.broadcasted_iota(jnp.int32, (chunk, chunk), 0)
    col_id = lax.broadcasted_iota(jnp.int32, (chunk, chunk), 1)
    causal = (row_id >= col_id, row_id <= col_id)
    ones = jnp.ones((chunk, LANES), BF16)

    def lanes2(x):
        return jnp.concatenate([x] * (dv // LANES), axis=1)

    hg = hg_ref[0]

    def emit(r0, h, own_ref, other_ref):
        rows = pl.ds(r0, chunk)
        if other_ref is None:
            own_ref[rows, :] = h
            return
        hm = h + other_ref[rows, :]
        ms = jnp.mean(hm * hm, axis=-1, keepdims=True)
        hn = hm * lax.rsqrt(ms + RMS_EPS) * hg
        y_ref[0, rows, :] = (hn * jax.nn.sigmoid(og_ref[0, rows, :])).astype(y_ref.dtype)

    def chunk_step(c, m, d, own_ref, other_ref):
        r0 = pl.multiple_of(c * chunk, chunk)
        qb = (q_ref[0, pl.ds(r0, chunk), :] * scale).astype(BF16)
        kb = k_ref[0, pl.ds(r0, chunk), :].astype(BF16)
        vf = v_ref[0, pl.ds(r0, chunk), :]
        cols = cols_ref[c]

        def column(j):
            lane = (3 * d + j) * nc
            return jnp.broadcast_to(cols[:, lane:lane + 1], (chunk, LANES))

        gg_col, b_col, gmax_col = column(0), column(1), column(2)
        gg_row = rows_ref[3 * d, pl.ds(c, 1), :]
        end = chunk - 1 if d == 0 else 0
        total = rows_ref[3 * d + 1, pl.ds(c, 1), :][:, end:end + 1]
        gmax_end = rows_ref[3 * d + 2, pl.ds(c, 1), :][:, end:end + 1]

        m_row = jnp.maximum(gmax_col, m)
        w_intra = jnp.exp2(jnp.where(causal[d], gg_row - lanes2(m_row), NEG_BIG))
        w_inter = jnp.exp2(m - m_row)
        s = lax.dot_general(qb, kb, (((1,), (1,)), ((), ())),
                            preferred_element_type=F32) * w_intra
        ct = ct_ref[d]
        intra = jnp.dot(s.astype(BF16), jnp.concatenate([vf.astype(BF16), ones], axis=1),
                        preferred_element_type=F32)
        inter = jnp.dot(qb, ct.astype(BF16), preferred_element_type=F32)
        num = intra[:, :dv] + lanes2(w_inter) * inter[:, :dv]
        den = intra[:, dv:] + w_inter * inter[:, dv:]
        inv = 1.0 / jnp.maximum(jnp.abs(den), jnp.exp2(-(b_col + m_row)))
        emit(r0, num * lanes2(inv), own_ref, other_ref)

        m_new = jnp.maximum(total + m, total + gmax_end)
        w_src = jnp.exp2(total + gg_col - m_new)
        decay = jnp.exp2(total + m - m_new)
        wv = jnp.concatenate([lanes2(w_src) * vf, w_src], axis=1).astype(BF16)
        ct_ref[d] = decay * ct + lax.dot_general(
            kb, wv, (((0,), (0,)), ((), ())), preferred_element_type=F32)
        return m_new

    def first_half(c, carry):
        m_f, m_b = carry
        m_f = chunk_step(c, m_f, 0, hf_ref, None)
        m_b = chunk_step(nc - 1 - c, m_b, 1, hb_ref, None)
        return m_f, m_b

    def second_half(c, carry):
        m_f, m_b = carry
        m_f = chunk_step(c, m_f, 0, None, hb_ref)
        m_b = chunk_step(nc - 1 - c, m_b, 1, None, hf_ref)
        return m_f, m_b

    m0 = jnp.full((1, 1), NEG_BIG, F32)
    carry = lax.fori_loop(0, nc // 2, first_half, (m0, m0), unroll=2)
    lax.fori_loop(nc // 2, nc, second_half, carry, unroll=2)


def mlstm_mixer(z, gates_rows, gate_b_rows, head_g, chunk=MLSTM_CHUNK):
    b, s, _ = z.shape
    h, dk, dv = MLSTM_HEADS, MLSTM_DK, MLSTM_DV
    nc = s // chunk
    assert s % chunk == 0 and 6 * nc <= LANES and chunk % LANES == 0 and nc % 2 == 0
    kq = h * dk // dk
    kv = 2 * h * dk // dv
    ko = kv + h
    return pl.pallas_call(
        functools.partial(_mlstm_body, nc=nc, chunk=chunk),
        grid=(b, h),
        in_specs=[
            pl.BlockSpec((1, s, dk), lambda i, j: (i, 0, j)),
            pl.BlockSpec((1, s, dk), lambda i, j: (i, 0, kq + j)),
            pl.BlockSpec((1, s, dv), lambda i, j: (i, 0, kv + j)),
            pl.BlockSpec((1, s, dv), lambda i, j: (i, 0, ko + j)),
            pl.BlockSpec((1, 1, 4, nc, chunk), lambda i, j: (i, j, 0, 0, 0)),
            pl.BlockSpec((1, 4, 1, chunk), lambda i, j: (j, 0, 0, 0)),
            pl.BlockSpec((1, 1, dv), lambda i, j: (j, 0, 0)),
        ],
        out_specs=pl.BlockSpec((1, s, dv), lambda i, j: (i, 0, j)),
        out_shape=jax.ShapeDtypeStruct((b, s, h * dv), BF16),
        scratch_shapes=[
            pltpu.VMEM((6, nc, chunk), F32),
            pltpu.VMEM((nc, chunk, LANES), F32),
            pltpu.VMEM((s, dv), F32),
            pltpu.VMEM((s, dv), F32),
            pltpu.VMEM((2, dk, dv + LANES), F32),
        ],
        compiler_params=_cparams("parallel", "parallel"),
        name="mlstm",
    )(z, z, z, z, gates_rows, gate_b_rows, head_g)


RG_TILE = 512
RG_PAD = SUBLANES


def _softplus(x):
    return jnp.maximum(x, 0.0) + jnp.log1p(jnp.exp(-jnp.abs(x)))


def _gelu_tanh(x):
    c = math.sqrt(2.0 / math.pi)
    return x * (0.5 * (1.0 + jnp.tanh(c * (x + 0.044715 * (x * x * x)))))


def _rglru_body(x_ref, gr_ref, cw_ref, cb_ref, wa_ref, wx_ref, ba_ref, bx_ref, lam_ref, y_ref,
                xpad_ref, a_ref, u_ref, h_ref, *, s):
    nt = s // RG_TILE
    zeros = jnp.zeros((RG_PAD, LANES), F32)
    xpad_ref[pl.ds(0, RG_PAD), :] = zeros
    xpad_ref[pl.ds(RG_PAD + s, RG_PAD), :] = zeros

    def copy_in(t, carry):
        r0 = pl.multiple_of(t * RG_TILE, RG_TILE)
        xpad_ref[pl.ds(RG_PAD + r0, RG_TILE), :] = x_ref[0, pl.ds(r0, RG_TILE), :]
        return carry

    lax.fori_loop(0, nt, copy_in, 0)

    cw = cw_ref[...]
    cb = cb_ref[...]
    half_sp = [0.5 * RGLRU_C * _softplus(-lam_ref[d:d + 1, :]) for d in range(2)]

    def gates(t, carry):
        r0 = pl.multiple_of(t * RG_TILE, RG_TILE)
        xc = None
        for j in range(CONV_WIDTH):
            tap = xpad_ref[pl.ds(r0 + RG_PAD - CONV_LEFT + j, RG_TILE), :] * cw[j:j + 1, :]
            xc = tap if xc is None else xc + tap
        xc = xc + cb
        xcb = xc.astype(BF16)
        half_xc = 0.5 * xc
        for d in range(2):
            t_r = jnp.tanh(jnp.dot(xcb, wa_ref[d, 0], preferred_element_type=F32)
                           + ba_ref[d:d + 1, :])
            t_i = jnp.tanh(jnp.dot(xcb, wx_ref[d, 0], preferred_element_type=F32)
                           + bx_ref[d:d + 1, :])
            neg_log_a = half_sp[d] * t_r + half_sp[d]
            gated_x = half_xc * t_i + half_xc
            a = jnp.exp(-neg_log_a)
            one_minus_a2 = jnp.tanh(neg_log_a) * (1.0 + a * a)
            root = jnp.where(one_minus_a2 > 0.0, one_minus_a2 * lax.rsqrt(one_minus_a2), 0.0)
            a_ref[d, pl.ds(r0, RG_TILE), :] = a
            u_ref[d, pl.ds(r0, RG_TILE), :] = root * gated_x
        return carry

    lax.fori_loop(0, nt, gates, 0)

    nb = RG_TILE // SUBLANES
    sub = lax.broadcasted_iota(jnp.int32, (SUBLANES, LANES), 0)

    def compose_sublanes(a, u, reverse):
        sh = 1
        while sh < SUBLANES:
            keep = sub < SUBLANES - sh if reverse else sub >= sh
            amount = SUBLANES - sh if reverse else sh
            a_s = jnp.where(keep, pltpu.roll(a, amount, axis=0), 1.0)
            u_s = jnp.where(keep, pltpu.roll(u, amount, axis=0), 0.0)
            u = a * u_s + u
            a = a * a_s
            sh *= 2
        return a, u

    def tile_recurrence(t, d, carry):
        r0 = pl.multiple_of(t * RG_TILE, RG_TILE)
        reverse = d == 1
        a_rows, u_rows = [None] * SUBLANES, [None] * SUBLANES
        a_prev = u_prev = None
        for r in (reversed(range(SUBLANES)) if reverse else range(SUBLANES)):
            rows = pl.ds(r0 + r, nb, stride=SUBLANES)
            a, u = a_ref[d, rows, :], u_ref[d, rows, :]
            if a_prev is not None:
                u = a * u_prev + u
                a = a * a_prev
            a_rows[r], u_rows[r] = a, u
            a_prev, u_prev = a, u
        entry = [None] * (nb // SUBLANES)
        for v in (reversed(range(nb // SUBLANES)) if reverse else range(nb // SUBLANES)):
            blk = slice(v * SUBLANES, (v + 1) * SUBLANES)
            sa, su = compose_sublanes(a_prev[blk], u_prev[blk], reverse)
            after = sa * carry + su
            if reverse:
                entry[v] = jnp.where(sub < SUBLANES - 1, pltpu.roll(after, SUBLANES - 1, axis=0),
                                     carry)
                carry = after[0:1]
            else:
                entry[v] = jnp.where(sub >= 1, pltpu.roll(after, 1, axis=0), carry)
                carry = after[SUBLANES - 1:SUBLANES]
        h_entry = jnp.concatenate(entry, axis=0)
        for r in range(SUBLANES):
            h_ref[d, pl.ds(r0 + r, nb, stride=SUBLANES), :] = a_rows[r] * h_entry + u_rows[r]
        return carry

    def recurrences(t, carry):
        h_f, h_b = carry
        return tile_recurrence(t, 0, h_f), tile_recurrence(nt - 1 - t, 1, h_b)

    h0 = jnp.zeros((1, LANES), F32)
    lax.fori_loop(0, nt, recurrences, (h0, h0))

    def finish(t, carry):
        r0 = pl.multiple_of(t * RG_TILE, RG_TILE)
        hr = h_ref[0, pl.ds(r0, RG_TILE), :] + h_ref[1, pl.ds(r0, RG_TILE), :]
        y_ref[0, pl.ds(r0, RG_TILE), :] = (
            hr * _gelu_tanh(gr_ref[0, pl.ds(r0, RG_TILE), :])).astype(y_ref.dtype)
        return carry

    lax.fori_loop(0, nt, finish, 0)


def rglru_mixer(z, xr_block0, gr_block0, conv_w, conv_b, wa, wx, ba, bx, lam):
    b, s, _ = z.shape
    width = RNN_BLOCKS * RNN_BLOCK
    assert s % RG_TILE == 0
    return pl.pallas_call(
        functools.partial(_rglru_body, s=s),
        grid=(b, RNN_BLOCKS),
        in_specs=[
            pl.BlockSpec((1, s, RNN_BLOCK), lambda i, j: (i, 0, xr_block0 + j)),
            pl.BlockSpec((1, s, RNN_BLOCK), lambda i, j: (i, 0, gr_block0 + j)),
            pl.BlockSpec((CONV_WIDTH, RNN_BLOCK), lambda i, j: (0, j)),
            pl.BlockSpec((1, RNN_BLOCK), lambda i, j: (0, j)),
            pl.BlockSpec((2, 1, RNN_BLOCK, RNN_BLOCK), lambda i, j: (0, j, 0, 0)),
            pl.BlockSpec((2, 1, RNN_BLOCK, RNN_BLOCK), lambda i, j: (0, j, 0, 0)),
            pl.BlockSpec((2, RNN_BLOCK), lambda i, j: (0, j)),
            pl.BlockSpec((2, RNN_BLOCK), lambda i, j: (0, j)),
            pl.BlockSpec((2, RNN_BLOCK), lambda i, j: (0, j)),
        ],
        out_specs=pl.BlockSpec((1, s, RNN_BLOCK), lambda i, j: (i, 0, j)),
        out_shape=jax.ShapeDtypeStruct((b, s, width), BF16),
        scratch_shapes=[
            pltpu.VMEM((s + 2 * RG_PAD, RNN_BLOCK), F32),
            pltpu.VMEM((2, s, RNN_BLOCK), F32),
            pltpu.VMEM((2, s, RNN_BLOCK), F32),
            pltpu.VMEM((2, s, RNN_BLOCK), F32),
        ],
        compiler_params=_cparams("parallel", "parallel"),
        name="rglru",
    )(z, z, conv_w, conv_b, wa, wx, ba, bx, lam)


ATTN_TQ = 128
ATTN_TK = ATTN_TQ + 2 * ATTN_HALF
ATTN_UNROLL = 32


def _attn_body(q_ref, k_ref, v_ref, slope_ref, o_ref, bias_ref, x4_ref, og_ref, dg_ref, mg_ref,
               *, s):
    qscale = ATTN_DH ** -0.5 * LOG2E

    @pl.when(pl.program_id(1) == 0)
    def _():
        slope = slope_ref[0, 0:1, 0:1] * LOG2E
        qi = lax.broadcasted_iota(jnp.int32, (ATTN_TQ, ATTN_TK), 0)
        kj = lax.broadcasted_iota(jnp.int32, (ATTN_TQ, ATTN_TK), 1)
        for g, dil in enumerate(ATTN_DILATIONS):
            for e in range(3):
                rel = jnp.abs(kj - e * ATTN_HALF - qi)
                pen = slope * (rel * dil).astype(F32)
                bias_ref[g, e] = jnp.where(rel <= ATTN_HALF, -pen, NEG_BIG)

    s4 = s // 4
    srcs = (q_ref, k_ref, v_ref)

    def split4(t, carry):
        c = t // (s4 // ATTN_TK)
        p0 = (t % (s4 // ATTN_TK)) * ATTN_TK
        dst = pl.ds(pl.multiple_of(c * s4 + p0, ATTN_TK), ATTN_TK)
        for a in range(3):
            x = srcs[a][0, pl.ds(c + 4 * p0, ATTN_TK, stride=4), :]
            x4_ref[a, dst, :] = x * qscale if a == 0 else x
        return carry

    lax.fori_loop(0, 4 * (s4 // ATTN_TK), split4, 0, unroll=4)

    ones = jnp.ones((ATTN_TK, ATTN_DH), BF16)

    for g, dil in reversed(list(enumerate(ATTN_DILATIONS))):
        sp = s // dil
        nqb = sp // ATTN_TQ

        def q_block(t, carry, g=g, dil=dil, sp=sp, nqb=nqb):
            r = t // nqb
            p0 = (t % nqb) * ATTN_TQ
            kstart = jnp.clip(p0 - ATTN_HALF, 0, sp - ATTN_TK)
            e = (p0 - kstart) // ATTN_HALF
            if dil == 1:
                qrows = pl.ds(pl.multiple_of(p0, ATTN_TQ), ATTN_TQ)
                krows = pl.ds(pl.multiple_of(kstart, ATTN_HALF), ATTN_TK)
                qf = q_ref[0, qrows, :] * qscale
                kf, vf = k_ref[0, krows, :], v_ref[0, krows, :]
                orows = qrows
            elif dil == 4:
                qrows = pl.ds(pl.multiple_of(r * s4 + p0, ATTN_TQ), ATTN_TQ)
                krows = pl.ds(pl.multiple_of(r * s4 + kstart, ATTN_HALF), ATTN_TK)
                qf, kf, vf = x4_ref[0, qrows, :], x4_ref[1, krows, :], x4_ref[2, krows, :]
                orows = pl.ds(r + 4 * p0, ATTN_TQ, stride=4)
            else:
                base = (r % 4) * s4 + r // 4
                qrows = pl.ds(base + 4 * p0, ATTN_TQ, stride=4)
                krows = pl.ds(base + 4 * kstart, ATTN_TK, stride=4)
                qf, kf, vf = x4_ref[0, qrows, :], x4_ref[1, krows, :], x4_ref[2, krows, :]
                orows = pl.ds(r + dil * p0, ATTN_TQ, stride=dil)
            qb, kb, vb = qf.astype(BF16), kf.astype(BF16), vf.astype(BF16)
            sc = lax.dot_general(qb, kb, (((1,), (1,)), ((), ())),
                                 preferred_element_type=F32) + bias_ref[g, e]
            m = jnp.max(sc, axis=-1, keepdims=True)
            p = jnp.exp2(sc - m).astype(BF16)
            pv = jnp.dot(p, jnp.concatenate([vb, ones], axis=1), preferred_element_type=F32)
            acc, den = pv[:, :ATTN_DH], pv[:, ATTN_DH:]
            if dil != 1:
                og_ref[g - 1, orows, :] = acc
                dg_ref[g - 1, orows, :] = den
                mg_ref[g - 1, orows, :] = jnp.broadcast_to(m, den.shape)
                return carry
            m4, m16 = mg_ref[0, orows, :], mg_ref[1, orows, :]
            mx = jnp.maximum(jnp.maximum(m4, m16), m)
            e1, e4, e16 = jnp.exp2(m - mx), jnp.exp2(m4 - mx), jnp.exp2(m16 - mx)
            num = e1 * acc + e4 * og_ref[0, orows, :] + e16 * og_ref[1, orows, :]
            tot = e1 * den + e4 * dg_ref[0, orows, :] + e16 * dg_ref[1, orows, :]
            o_ref[0, orows, :] = (num * (1.0 / tot)).astype(o_ref.dtype)
            return carry

        lax.fori_loop(0, dil * nqb, q_block, 0, unroll=ATTN_UNROLL)


def dilated_attention(qkv, slopes):
    b, s, _ = qkv.shape
    h, dh = ATTN_HEADS, ATTN_DH
    assert s % (max(ATTN_DILATIONS) * ATTN_TK) == 0
    return pl.pallas_call(
        functools.partial(_attn_body, s=s),
        grid=(h, b),
        in_specs=[
            pl.BlockSpec((1, s, dh), lambda j, i: (i, 0, j)),
            pl.BlockSpec((1, s, dh), lambda j, i: (i, 0, h + j)),
            pl.BlockSpec((1, s, dh), lambda j, i: (i, 0, 2 * h + j)),
            pl.BlockSpec((1, SUBLANES, LANES), lambda j, i: (j, 0, 0)),
        ],
        out_specs=pl.BlockSpec((1, s, dh), lambda j, i: (i, 0, j)),
        out_shape=jax.ShapeDtypeStruct((b, s, h * dh), BF16),
        scratch_shapes=[
            pltpu.VMEM((len(ATTN_DILATIONS), 3, ATTN_TQ, ATTN_TK), F32),
            pltpu.VMEM((3, s, dh), F32),
            pltpu.VMEM((len(ATTN_DILATIONS) - 1, s, dh), F32),
            pltpu.VMEM((len(ATTN_DILATIONS) - 1, s, dh), F32),
            pltpu.VMEM((len(ATTN_DILATIONS) - 1, s, dh), F32),
        ],
        compiler_params=_cparams("parallel", "arbitrary"),
        name="dilated_attn",
    )(qkv, qkv, qkv, slopes)


def _ffn(x, h, w1, w3, w2, layer, next_norm_g, last):
    u = swiglu_up(h, w1, w3, layer)
    return matmul([(u, w2, 0, layer)], res=x, norm_g=next_norm_g,
                  norm_dtype=F32 if last else BF16, keep_sum=not last,
                  tm=RESIDENT_TM, tn=w2.shape[-1], name="ffn_down")


def _even_layer(x, bsz, seq, norm_g, w_in, gate_b, conv_w, conv_b, rg_wa, rg_ba, rg_wx, rg_bx,
                rg_lam, head_g, w_out, next_norm_g):
    heads, dk, dv = MLSTM_HEADS, MLSTM_DK, MLSTM_DV
    n_qkvo = 2 * heads * dk + 2 * heads * dv
    n_gate = 4 * heads
    width = RNN_BLOCKS * RNN_BLOCK
    w_in_t = jnp.transpose(w_in)
    w_main_t = jnp.concatenate([w_in_t[:n_qkvo], w_in_t[n_qkvo + n_gate:]], axis=0).astype(BF16)
    w_gate_t = w_in_t[n_qkvo:n_qkvo + n_gate].astype(BF16)
    z, zg_t = norm_matmul(x, norm_g, w_main_t, w_side_t=w_gate_t, w_transposed=True,
                          name="in_proj")
    z = z.reshape(bsz, seq, -1)

    chunk = MLSTM_CHUNK
    nc = seq // chunk
    gates_rows = jnp.transpose(zg_t[:n_gate].reshape(4, heads, bsz, nc, chunk), (2, 1, 0, 3, 4))
    gate_b_rows = jnp.broadcast_to(
        jnp.transpose(gate_b.astype(F32).reshape(4, heads))[:, :, None, None], (heads, 4, 1, chunk))
    y_a = mlstm_mixer(z, gates_rows, gate_b_rows, head_g.astype(F32).reshape(heads, 1, dv), chunk)

    y_b = rglru_mixer(z, n_qkvo // RNN_BLOCK, (n_qkvo + width) // RNN_BLOCK,
                      conv_w.astype(F32), conv_b.astype(F32).reshape(1, width),
                      (0.5 * rg_wa).astype(BF16), (0.5 * rg_wx).astype(BF16),
                      0.5 * rg_ba.astype(F32), 0.5 * rg_bx.astype(F32), rg_lam.astype(F32))

    w_out_b = w_out.astype(BF16)
    m = bsz * seq
    assert heads * dv == width
    return matmul([(y_a.reshape(m, -1), w_out_b, 0),
                   (y_b.reshape(m, -1), w_out_b, 1)], res=x, norm_g=next_norm_g,
                  tm=RESIDENT_TM, tn=w_out.shape[1], name="out_proj")


def _odd_layer(x, h, bsz, seq, norm_g, w_qkv, w_o, next_norm_g):
    if h is None:
        qkv = norm_matmul(x, norm_g, w_qkv.astype(BF16), tm=RESIDENT_TM // 2, name="qkv_proj")
    else:
        qkv = matmul([(h, w_qkv.astype(BF16))], tm=RESIDENT_TM, tn=w_qkv.shape[1],
                     name="qkv_proj")
    qkv = qkv.reshape(bsz, seq, -1)
    slopes = jnp.exp2(-ALIBI_MAX_BIAS * jnp.arange(1, ATTN_HEADS + 1, dtype=F32) / ATTN_HEADS)
    slopes = jnp.broadcast_to(slopes[:, None, None], (ATTN_HEADS, SUBLANES, LANES))
    o = dilated_attention(qkv, slopes)
    return matmul([(o.reshape(bsz * seq, -1), w_o.astype(BF16))], res=x, norm_g=next_norm_g,
                  tm=RESIDENT_TM, tn=w_o.shape[1], name="attn_out_proj")


def kernel(x, e_norm, e_w_in, e_gate_b, e_conv_w, e_conv_b, e_rg_wa, e_rg_ba, e_rg_wx, e_rg_bx,
           e_rg_lam, e_head_g, e_w_out, o_norm, o_w_qkv, o_w_o, f_norm, f_w1, f_w3, f_w2,
           final_norm):
    bsz, seq, d = x.shape
    depth = f_norm.shape[0]
    xs = x.reshape(bsz * seq, d).astype(F32)
    w1, w3, w2 = f_w1.astype(BF16), f_w3.astype(BF16), f_w2.astype(BF16)
    hs = None
    for l in range(depth):
        if l % 2 == 0:
            e = l // 2
            xs, hs = _even_layer(xs, bsz, seq, e_norm[e], e_w_in[e], e_gate_b[e], e_conv_w[e],
                                 e_conv_b[e], e_rg_wa[e], e_rg_ba[e], e_rg_wx[e], e_rg_bx[e],
                                 e_rg_lam[e], e_head_g[e], e_w_out[e], f_norm[l])
        else:
            o = l // 2
            xs, hs = _odd_layer(xs, hs, bsz, seq, o_norm[o], o_w_qkv[o], o_w_o[o], f_norm[l])
        if l == depth - 1:
            out = _ffn(xs, hs, w1, w3, w2, l, final_norm, last=True)
            return out.astype(x.dtype).reshape(bsz, seq, d)
        if (l + 1) % 2:
            xs, hs = _ffn(xs, hs, w1, w3, w2, l, o_norm[(l + 1) // 2], last=False)
        else:
            xs, hs = _ffn(xs, hs, w1, w3, w2, l, None, last=False), None
```

```python
import functools
import math

import jax
import jax.numpy as jnp
from jax import lax
from jax.experimental import pallas as pl
from jax.experimental.pallas import tpu as pltpu

F32 = jnp.float32
BF16 = jnp.bfloat16

RMS_EPS = 1e-6
NEG_BIG = -1e30
LOG2E = math.log2(math.e)

MLSTM_HEADS = 4
MLSTM_DK = 128
MLSTM_DV = 256
MLSTM_CHUNK = 256
RNN_BLOCKS = 8
RNN_BLOCK = 128
CONV_WIDTH = 4
CONV_LEFT = 2
RGLRU_C = 8.0
ATTN_HEADS = 16
ATTN_DH = 128
ATTN_HALF = 64
ATTN_DILATIONS = (1, 4, 16)
ALIBI_MAX_BIAS = 8.0

LANES = 128
SUBLANES = 8
VMEM_LIMIT_BYTES = 56 * 1024 * 1024

RESIDENT_TM = 512
SWIGLU_TM = 4096
SWIGLU_TN = 512
SWIGLU_ROWS = 1024
NORM_CHUNK = 256


def _cparams(*semantics):
    return pltpu.CompilerParams(dimension_semantics=semantics,
                                vmem_limit_bytes=VMEM_LIMIT_BYTES)


def _weight_spec(w, kdim, tn, row_block=0, layer=None, **mode):
    if w.ndim == 2:
        return pl.BlockSpec((kdim, tn), lambda i, j: (row_block, j), **mode)
    return pl.BlockSpec((None, kdim, tn), lambda i, j: (layer, row_block, j), **mode)


def _matmul_body(*refs, n_pairs, has_res, has_norm, keep_sum):
    pairs = [(refs[2 * p], refs[2 * p + 1]) for p in range(n_pairs)]
    rest = list(refs[2 * n_pairs:])
    r_ref = rest.pop(0) if has_res else None
    g_ref = rest.pop(0) if has_norm else None
    y_ref = rest.pop(0) if keep_sum else None
    h_ref = rest.pop(0) if has_norm else None

    tm = pairs[0][0].shape[0]
    rows = min(NORM_CHUNK, tm) if has_norm else tm
    for r0 in range(0, tm, rows):
        sl = pl.ds(r0, rows)
        acc = None
        for a_ref, w_ref in pairs:
            d = jnp.dot(a_ref[sl, :], w_ref[...], preferred_element_type=F32)
            acc = d if acc is None else acc + d
        if has_res:
            acc = acc + r_ref[sl, :]
        if keep_sum:
            y_ref[sl, :] = acc
        if has_norm:
            ms = jnp.mean(acc * acc, axis=-1, keepdims=True)
            h_ref[sl, :] = (acc * lax.rsqrt(ms + RMS_EPS) * g_ref[...]).astype(h_ref.dtype)


def matmul(pairs, *, tm, tn, res=None, norm_g=None, norm_dtype=BF16, keep_sum=True,
           name="matmul"):
    m = pairs[0][0].shape[0]
    n = pairs[0][1].shape[-1]
    tn = min(tn, n)
    assert m % tm == 0 and n % tn == 0 and (keep_sum or norm_g is not None)
    assert norm_g is None or (tn == n and tm % NORM_CHUNK == 0)
    w_mode = dict(pipeline_mode=pl.Buffered(1)) if tn == n else {}
    in_specs, args = [], []
    for a, w, *where in pairs:
        kdim = a.shape[1]
        kb, layer = (list(where) + [0, None])[:2] if where else (0, None)
        in_specs.append(pl.BlockSpec((tm, kdim), lambda i, j: (i, 0)))
        in_specs.append(_weight_spec(w, kdim, tn, kb, layer, **w_mode))
        args += [a, w]
    if res is not None:
        in_specs.append(pl.BlockSpec((tm, tn), lambda i, j: (i, j)))
        args.append(res)
    out_specs, out_shape = [], []
    if keep_sum:
        out_specs.append(pl.BlockSpec((tm, tn), lambda i, j: (i, j)))
        out_shape.append(jax.ShapeDtypeStruct((m, n), F32))
    if norm_g is not None:
        in_specs.append(pl.BlockSpec((1, n), lambda i, j: (0, 0)))
        args.append(norm_g.reshape(1, n).astype(F32))
        out_specs.append(pl.BlockSpec((tm, n), lambda i, j: (i, 0)))
        out_shape.append(jax.ShapeDtypeStruct((m, n), norm_dtype))
    outs = pl.pallas_call(
        functools.partial(_matmul_body, n_pairs=len(pairs), has_res=res is not None,
                          has_norm=norm_g is not None, keep_sum=keep_sum),
        grid=(m // tm, n // tn),
        in_specs=in_specs,
        out_specs=out_specs,
        out_shape=out_shape,
        compiler_params=_cparams("parallel", "parallel"),
        name=name,
    )(*args)
    return outs if len(outs) > 1 else outs[0]


def _swiglu_up_body(a_ref, w1_ref, w3_ref, o_ref):
    rows = min(SWIGLU_ROWS, a_ref.shape[0])
    for r0 in range(0, a_ref.shape[0], rows):
        a = a_ref[pl.ds(r0, rows), :]
        u = jnp.dot(a, w1_ref[...], preferred_element_type=F32)
        v = jnp.dot(a, w3_ref[...], preferred_element_type=F32)
        o_ref[pl.ds(r0, rows), :] = (u * jax.nn.sigmoid(u) * v).astype(o_ref.dtype)


def swiglu_up(a, w1, w3, layer, tm=SWIGLU_TM, tn=SWIGLU_TN):
    m, kdim = a.shape
    n = w1.shape[-1]
    assert m % tm == 0 and n % tn == 0
    return pl.pallas_call(
        _swiglu_up_body,
        grid=(m // tm, n // tn),
        in_specs=[pl.BlockSpec((tm, kdim), lambda i, j: (i, 0)),
                  _weight_spec(w1, kdim, tn, layer=layer),
                  _weight_spec(w3, kdim, tn, layer=layer)],
        out_specs=pl.BlockSpec((tm, tn), lambda i, j: (i, j)),
        out_shape=jax.ShapeDtypeStruct((m, n), BF16),
        compiler_params=_cparams("parallel", "parallel"),
        name="swiglu_up",
    )(a, w1, w3)


def _norm_matmul_body(*refs, has_side, w_transposed):
    if has_side:
        x_ref, g_ref, w_ref, ws_ref, o_ref, os_ref = refs
    else:
        x_ref, g_ref, w_ref, o_ref = refs
    w_dims = (((1,), (1,)), ((), ())) if w_transposed else (((1,), (0,)), ((), ()))

    g = g_ref[...]
    for r0 in range(0, x_ref.shape[0], NORM_CHUNK):
        sl = pl.ds(r0, NORM_CHUNK)
        x = x_ref[sl, :]
        ms = jnp.mean(x * x, axis=-1, keepdims=True)
        h = (x * lax.rsqrt(ms + RMS_EPS) * g).astype(BF16)
        o_ref[sl, :] = lax.dot_general(h, w_ref[...], w_dims,
                                       preferred_element_type=F32).astype(o_ref.dtype)
        if has_side:
            os_ref[:, sl] = lax.dot_general(ws_ref[...], h, (((1,), (1,)), ((), ())),
                                            preferred_element_type=F32)


def norm_matmul(x, g, w, w_side_t=None, w_transposed=False, out_dtype=F32, tm=RESIDENT_TM,
                name="norm_matmul"):
    m, d = x.shape
    n = w.shape[0] if w_transposed else w.shape[1]
    assert m % tm == 0 and tm % NORM_CHUNK == 0
    resident = dict(pipeline_mode=pl.Buffered(1))
    in_specs = [pl.BlockSpec((tm, d), lambda i: (i, 0)),
                pl.BlockSpec((1, d), lambda i: (0, 0)),
                pl.BlockSpec(w.shape, lambda i: (0, 0), **resident)]
    args = [x, g.reshape(1, d).astype(F32), w]
    out_specs = [pl.BlockSpec((tm, n), lambda i: (i, 0))]
    out_shape = [jax.ShapeDtypeStruct((m, n), out_dtype)]
    if w_side_t is not None:
        ns = w_side_t.shape[0]
        in_specs.append(pl.BlockSpec((ns, d), lambda i: (0, 0), **resident))
        args.append(w_side_t)
        out_specs.append(pl.BlockSpec((ns, tm), lambda i: (0, i)))
        out_shape.append(jax.ShapeDtypeStruct((ns, m), F32))
    outs = pl.pallas_call(
        functools.partial(_norm_matmul_body, has_side=w_side_t is not None,
                          w_transposed=w_transposed),
        grid=(m // tm,),
        in_specs=in_specs,
        out_specs=out_specs,
        out_shape=out_shape,
        compiler_params=_cparams("parallel"),
        name=name,
    )(*args)
    return outs if w_side_t is not None else outs[0]


def _log_sigmoid(x):
    return jnp.minimum(x, 0.0) - jnp.log1p(jnp.exp(-jnp.abs(x)))


SCAN_RADIX = 4


def _lane_scan(x, combine, identity, reverse):
    width = x.shape[-1]
    lane = lax.broadcasted_iota(jnp.int32, x.shape, 1)

    def shifted(v, sh):
        if reverse:
            return jnp.where(lane < width - sh, pltpu.roll(v, width - sh, axis=1), identity)
        return jnp.where(lane >= sh, pltpu.roll(v, sh, axis=1), identity)

    sh = 1
    while sh < width:
        parts = [shifted(x, k * sh) for k in range(1, SCAN_RADIX) if k * sh < width]
        for p in parts:
            x = combine(x, p)
        sh *= SCAN_RADIX
    return x


def _mlstm_body(q_ref, k_ref, v_ref, og_ref, g_ref, gb_ref, hg_ref, y_ref,
                rows_ref, cols_ref, hf_ref, hb_ref, ct_ref, *, nc, chunk):
    dk, dv = MLSTM_DK, MLSTM_DV
    scale = dk ** -0.5

    g = g_ref[0, 0] + gb_ref[0]
    rows = []
    for d in range(2):
        b = _lane_scan(_log_sigmoid(g[2 * d + 1]), jnp.add, 0.0, reverse=d == 1)
        gg = g[2 * d] - b
        gmax = _lane_scan(gg, jnp.maximum, -jnp.inf, reverse=d == 1)
        scaled = [gg * LOG2E, b * LOG2E, gmax * LOG2E]
        rows += scaled
        for j, r in enumerate(scaled):
            rows_ref[3 * d + j] = r
    rowmat = jnp.concatenate(rows + [jnp.zeros((LANES - 6 * nc, chunk), F32)], axis=0)
    colmat = rowmat.T
    for c in range(nc):
        cols_ref[c] = colmat if c == 0 else pltpu.roll(colmat, LANES - c, axis=1)

    ct_ref[...] = jnp.zeros_like(ct_ref)

    row_id = lax.broadcasted_iota(jnp.int32, (chunk, chunk), 0)
    col_id = lax.broadcasted_iota(jnp.int32, (chunk, chunk), 1)
    causal = (row_id >= col_id, row_id <= col_id)
    ones = jnp.ones((chunk, LANES), BF16)

    def lanes2(x):
        return jnp.concatenate([x] * (dv // LANES), axis=1)

    hg = hg_ref[0]

    def emit(r0, h, own_ref, other_ref):
        rows = pl.ds(r0, chunk)
        if other_ref is None:
            own_ref[rows, :] = h
            return
        hm = h + other_ref[rows, :]
        ms = jnp.mean(hm * hm, axis=-1, keepdims=True)
        hn = hm * lax.rsqrt(ms + RMS_EPS) * hg
        y_ref[0, rows, :] = (hn * jax.nn.sigmoid(og_ref[0, rows, :])).astype(y_ref.dtype)

    def chunk_step(c, m, d, own_ref, other_ref):
        r0 = pl.multiple_of(c * chunk, chunk)
        qb = (q_ref[0, pl.ds(r0, chunk), :] * scale).astype(BF16)
        kb = k_ref[0, pl.ds(r0, chunk), :].astype(BF16)
        vf = v_ref[0, pl.ds(r0, chunk), :]
        cols = cols_ref[c]

        def column(j):
            lane = (3 * d + j) * nc
            return jnp.broadcast_to(cols[:, lane:lane + 1], (chunk, LANES))

        gg_col, b_col, gmax_col = column(0), column(1), column(2)
        gg_row = rows_ref[3 * d, pl.ds(c, 1), :]
        end = chunk - 1 if d == 0 else 0
        total = rows_ref[3 * d + 1, pl.ds(c, 1), :][:, end:end + 1]
        gmax_end = rows_ref[3 * d + 2, pl.ds(c, 1), :][:, end:end + 1]

        m_row = jnp.maximum(gmax_col, m)
        w_intra = jnp.exp2(jnp.where(causal[d], gg_row - lanes2(m_row), NEG_BIG))
        w_inter = jnp.exp2(m - m_row)
        s = lax.dot_general(qb, kb, (((1,), (1,)), ((), ())),
                            preferred_element_type=F32) * w_intra
        ct = ct_ref[d]
        intra = jnp.dot(s.astype(BF16), jnp.concatenate([vf.astype(BF16), ones], axis=1),
                        preferred_element_type=F32)
        inter = jnp.dot(qb, ct.astype(BF16), preferred_element_type=F32)
        num = intra[:, :dv] + lanes2(w_inter) * inter[:, :dv]
        den = intra[:, dv:] + w_inter * inter[:, dv:]
        inv = 1.0 / jnp.maximum(jnp.abs(den), jnp.exp2(-(b_col + m_row)))
        emit(r0, num * lanes2(inv), own_ref, other_ref)

        m_new = jnp.maximum(total + m, total + gmax_end)
        w_src = jnp.exp2(total + gg_col - m_new)
        decay = jnp.exp2(total + m - m_new)
        wv = jnp.concatenate([lanes2(w_src) * vf, w_src], axis=1).astype(BF16)
        ct_ref[d] = decay * ct + lax.dot_general(
            kb, wv, (((0,), (0,)), ((), ())), preferred_element_type=F32)
        return m_new

    def first_half(c, carry):
        m_f, m_b = carry
        m_f = chunk_step(c, m_f, 0, hf_ref, None)
        m_b = chunk_step(nc - 1 - c, m_b, 1, hb_ref, None)
        return m_f, m_b

    def second_half(c, carry):
        m_f, m_b = carry
        m_f = chunk_step(c, m_f, 0, None, hb_ref)
        m_b = chunk_step(nc - 1 - c, m_b, 1, None, hf_ref)
        return m_f, m_b

    m0 = jnp.full((1, 1), NEG_BIG, F32)
    carry = lax.fori_loop(0, nc // 2, first_half, (m0, m0), unroll=4)
    lax.fori_loop(nc // 2, nc, second_half, carry, unroll=4)


def mlstm_mixer(z, gates_rows, gate_b_rows, head_g, chunk=MLSTM_CHUNK):
    b, s, _ = z.shape
    h, dk, dv = MLSTM_HEADS, MLSTM_DK, MLSTM_DV
    nc = s // chunk
    assert s % chunk == 0 and 6 * nc <= LANES and chunk % LANES == 0 and nc % 2 == 0
    kq = h * dk // dk
    kv = 2 * h * dk // dv
    ko = kv + h
    return pl.pallas_call(
        functools.partial(_mlstm_body, nc=nc, chunk=chunk),
        grid=(b, h),
        in_specs=[
            pl.BlockSpec((1, s, dk), lambda i, j: (i, 0, j)),
            pl.BlockSpec((1, s, dk), lambda i, j: (i, 0, kq + j)),
            pl.BlockSpec((1, s, dv), lambda i, j: (i, 0, kv + j)),
            pl.BlockSpec((1, s, dv), lambda i, j: (i, 0, ko + j)),
            pl.BlockSpec((1, 1, 4, nc, chunk), lambda i, j: (i, j, 0, 0, 0)),
            pl.BlockSpec((1, 4, 1, chunk), lambda i, j: (j, 0, 0, 0)),
            pl.BlockSpec((1, 1, dv), lambda i, j: (j, 0, 0)),
        ],
        out_specs=pl.BlockSpec((1, s, dv), lambda i, j: (i, 0, j)),
        out_shape=jax.ShapeDtypeStruct((b, s, h * dv), BF16),
        scratch_shapes=[
            pltpu.VMEM((6, nc, chunk), F32),
            pltpu.VMEM((nc, chunk, LANES), F32),
            pltpu.VMEM((s, dv), F32),
            pltpu.VMEM((s, dv), F32),
            pltpu.VMEM((2, dk, dv + LANES), F32),
        ],
        compiler_params=_cparams("parallel", "parallel"),
        name="mlstm",
    )(z, z, z, z, gates_rows, gate_b_rows, head_g)


RG_TILE = 512
RG_PAD = SUBLANES


def _softplus(x):
    return jnp.maximum(x, 0.0) + jnp.log1p(jnp.exp(-jnp.abs(x)))


def _gelu_tanh(x):
    c = math.sqrt(2.0 / math.pi)
    return x * (0.5 * (1.0 + jnp.tanh(c * (x + 0.044715 * (x * x * x)))))


def _rglru_body(x_ref, gr_ref, cw_ref, cb_ref, wa_ref, wx_ref, ba_ref, bx_ref, lam_ref, y_ref,
                xpad_ref, a_ref, u_ref, h_ref, *, s):
    nt = s // RG_TILE
    zeros = jnp.zeros((RG_PAD, LANES), F32)
    xpad_ref[pl.ds(0, RG_PAD), :] = zeros
    xpad_ref[pl.ds(RG_PAD + s, RG_PAD), :] = zeros

    def copy_in(t, carry):
        r0 = pl.multiple_of(t * RG_TILE, RG_TILE)
        xpad_ref[pl.ds(RG_PAD + r0, RG_TILE), :] = x_ref[0, pl.ds(r0, RG_TILE), :]
        return carry

    lax.fori_loop(0, nt, copy_in, 0)

    cw = cw_ref[...]
    cb = cb_ref[...]
    half_sp = [0.5 * RGLRU_C * _softplus(-lam_ref[d:d + 1, :]) for d in range(2)]

    def gates(t, carry):
        r0 = pl.multiple_of(t * RG_TILE, RG_TILE)
        xc = None
        for j in range(CONV_WIDTH):
            tap = xpad_ref[pl.ds(r0 + RG_PAD - CONV_LEFT + j, RG_TILE), :] * cw[j:j + 1, :]
            xc = tap if xc is None else xc + tap
        xc = xc + cb
        xcb = xc.astype(BF16)
        half_xc = 0.5 * xc
        for d in range(2):
            t_r = jnp.tanh(jnp.dot(xcb, wa_ref[d, 0], preferred_element_type=F32)
                           + ba_ref[d:d + 1, :])
            t_i = jnp.tanh(jnp.dot(xcb, wx_ref[d, 0], preferred_element_type=F32)
                           + bx_ref[d:d + 1, :])
            neg_log_a = half_sp[d] * t_r + half_sp[d]
            gated_x = half_xc * t_i + half_xc
            a = jnp.exp(-neg_log_a)
            one_minus_a2 = jnp.tanh(neg_log_a) * (1.0 + a * a)
            root = jnp.where(one_minus_a2 > 0.0, one_minus_a2 * lax.rsqrt(one_minus_a2), 0.0)
            a_ref[d, pl.ds(r0, RG_TILE), :] = a
            u_ref[d, pl.ds(r0, RG_TILE), :] = root * gated_x
        return carry

    lax.fori_loop(0, nt, gates, 0)

    nb = RG_TILE // SUBLANES
    sub = lax.broadcasted_iota(jnp.int32, (SUBLANES, LANES), 0)

    def compose_sublanes(a, u, reverse):
        sh = 1
        while sh < SUBLANES:
            keep = sub < SUBLANES - sh if reverse else sub >= sh
            amount = SUBLANES - sh if reverse else sh
            a_s = jnp.where(keep, pltpu.roll(a, amount, axis=0), 1.0)
            u_s = jnp.where(keep, pltpu.roll(u, amount, axis=0), 0.0)
            u = a * u_s + u
            a = a * a_s
            sh *= 2
        return a, u

    def tile_recurrence(t, d, carry):
        r0 = pl.multiple_of(t * RG_TILE, RG_TILE)
        reverse = d == 1
        a_rows, u_rows = [None] * SUBLANES, [None] * SUBLANES
        a_prev = u_prev = None
        for r in (reversed(range(SUBLANES)) if reverse else range(SUBLANES)):
            rows = pl.ds(r0 + r, nb, stride=SUBLANES)
            a, u = a_ref[d, rows, :], u_ref[d, rows, :]
            if a_prev is not None:
                u = a * u_prev + u
                a = a * a_prev
            a_rows[r], u_rows[r] = a, u
            a_prev, u_prev = a, u
        entry = [None] * (nb // SUBLANES)
        for v in (reversed(range(nb // SUBLANES)) if reverse else range(nb // SUBLANES)):
            blk = slice(v * SUBLANES, (v + 1) * SUBLANES)
            sa, su = compose_sublanes(a_prev[blk], u_prev[blk], reverse)
            after = sa * carry + su
            if reverse:
                entry[v] = jnp.where(sub < SUBLANES - 1, pltpu.roll(after, SUBLANES - 1, axis=0),
                                     carry)
                carry = after[0:1]
            else:
                entry[v] = jnp.where(sub >= 1, pltpu.roll(after, 1, axis=0), carry)
                carry = after[SUBLANES - 1:SUBLANES]
        h_entry = jnp.concatenate(entry, axis=0)
        for r in range(SUBLANES):
            h_ref[d, pl.ds(r0 + r, nb, stride=SUBLANES), :] = a_rows[r] * h_entry + u_rows[r]
        return carry

    def recurrences(t, carry):
        h_f, h_b = carry
        return tile_recurrence(t, 0, h_f), tile_recurrence(nt - 1 - t, 1, h_b)

    h0 = jnp.zeros((1, LANES), F32)
    lax.fori_loop(0, nt, recurrences, (h0, h0))

    def finish(t, carry):
        r0 = pl.multiple_of(t * RG_TILE, RG_TILE)
        hr = h_ref[0, pl.ds(r0, RG_TILE), :] + h_ref[1, pl.ds(r0, RG_TILE), :]
        y_ref[0, pl.ds(r0, RG_TILE), :] = (
            hr * _gelu_tanh(gr_ref[0, pl.ds(r0, RG_TILE), :])).astype(y_ref.dtype)
        return carry

    lax.fori_loop(0, nt, finish, 0)


def rglru_mixer(z, xr_block0, gr_block0, conv_w, conv_b, wa, wx, ba, bx, lam):
    b, s, _ = z.shape
    width = RNN_BLOCKS * RNN_BLOCK
    assert s % RG_TILE == 0
    return pl.pallas_call(
        functools.partial(_rglru_body, s=s),
        grid=(b, RNN_BLOCKS),
        in_specs=[
            pl.BlockSpec((1, s, RNN_BLOCK), lambda i, j: (i, 0, xr_block0 + j)),
            pl.BlockSpec((1, s, RNN_BLOCK), lambda i, j: (i, 0, gr_block0 + j)),
            pl.BlockSpec((CONV_WIDTH, RNN_BLOCK), lambda i, j: (0, j)),
            pl.BlockSpec((1, RNN_BLOCK), lambda i, j: (0, j)),
            pl.BlockSpec((2, 1, RNN_BLOCK, RNN_BLOCK), lambda i, j: (0, j, 0, 0)),
            pl.BlockSpec((2, 1, RNN_BLOCK, RNN_BLOCK), lambda i, j: (0, j, 0, 0)),
            pl.BlockSpec((2, RNN_BLOCK), lambda i, j: (0, j)),
            pl.BlockSpec((2, RNN_BLOCK), lambda i, j: (0, j)),
            pl.BlockSpec((2, RNN_BLOCK), lambda i, j: (0, j)),
        ],
        out_specs=pl.BlockSpec((1, s, RNN_BLOCK), lambda i, j: (i, 0, j)),
        out_shape=jax.ShapeDtypeStruct((b, s, width), BF16),
        scratch_shapes=[
            pltpu.VMEM((s + 2 * RG_PAD, RNN_BLOCK), F32),
            pltpu.VMEM((2, s, RNN_BLOCK), F32),
            pltpu.VMEM((2, s, RNN_BLOCK), F32),
            pltpu.VMEM((2, s, RNN_BLOCK), F32),
        ],
        compiler_params=_cparams("parallel", "parallel"),
        name="rglru",
    )(z, z, conv_w, conv_b, wa, wx, ba, bx, lam)


ATTN_TQ = 128
ATTN_TK = ATTN_TQ + 2 * ATTN_HALF
ATTN_UNROLL = 32


def _attn_body(q_ref, k_ref, v_ref, slope_ref, o_ref, bias_ref, x4_ref, og_ref, dg_ref, mg_ref,
               *, s):
    qscale = ATTN_DH ** -0.5 * LOG2E

    @pl.when(pl.program_id(1) == 0)
    def _():
        slope = slope_ref[0, 0:1, 0:1] * LOG2E
        qi = lax.broadcasted_iota(jnp.int32, (ATTN_TQ, ATTN_TK), 0)
        kj = lax.broadcasted_iota(jnp.int32, (ATTN_TQ, ATTN_TK), 1)
        for g, dil in enumerate(ATTN_DILATIONS):
            for e in range(3):
                rel = jnp.abs(kj - e * ATTN_HALF - qi)
                pen = slope * (rel * dil).astype(F32)
                bias_ref[g, e] = jnp.where(rel <= ATTN_HALF, -pen, NEG_BIG)

    s4 = s // 4
    srcs = (q_ref, k_ref, v_ref)

    def split4(t, carry):
        c = t // (s4 // ATTN_TK)
        p0 = (t % (s4 // ATTN_TK)) * ATTN_TK
        dst = pl.ds(pl.multiple_of(c * s4 + p0, ATTN_TK), ATTN_TK)
        for a in range(3):
            x = srcs[a][0, pl.ds(c + 4 * p0, ATTN_TK, stride=4), :]
            x4_ref[a, dst, :] = x * qscale if a == 0 else x
        return carry

    lax.fori_loop(0, 4 * (s4 // ATTN_TK), split4, 0, unroll=4)

    ones = jnp.ones((ATTN_TK, ATTN_DH), BF16)

    for g, dil in reversed(list(enumerate(ATTN_DILATIONS))):
        sp = s // dil
        nqb = sp // ATTN_TQ

        def q_block(t, carry, g=g, dil=dil, sp=sp, nqb=nqb):
            r = t // nqb
            p0 = (t % nqb) * ATTN_TQ
            kstart = jnp.clip(p0 - ATTN_HALF, 0, sp - ATTN_TK)
            e = (p0 - kstart) // ATTN_HALF
            if dil == 1:
                qrows = pl.ds(pl.multiple_of(p0, ATTN_TQ), ATTN_TQ)
                krows = pl.ds(pl.multiple_of(kstart, ATTN_HALF), ATTN_TK)
                qf = q_ref[0, qrows, :] * qscale
                kf, vf = k_ref[0, krows, :], v_ref[0, krows, :]
                orows = qrows
            elif dil == 4:
                qrows = pl.ds(pl.multiple_of(r * s4 + p0, ATTN_TQ), ATTN_TQ)
                krows = pl.ds(pl.multiple_of(r * s4 + kstart, ATTN_HALF), ATTN_TK)
                qf, kf, vf = x4_ref[0, qrows, :], x4_ref[1, krows, :], x4_ref[2, krows, :]
                orows = pl.ds(r + 4 * p0, ATTN_TQ, stride=4)
            else:
                base = (r % 4) * s4 + r // 4
                qrows = pl.ds(base + 4 * p0, ATTN_TQ, stride=4)
                krows = pl.ds(base + 4 * kstart, ATTN_TK, stride=4)
                qf, kf, vf = x4_ref[0, qrows, :], x4_ref[1, krows, :], x4_ref[2, krows, :]
                orows = pl.ds(r + dil * p0, ATTN_TQ, stride=dil)
            qb, kb, vb = qf.astype(BF16), kf.astype(BF16), vf.astype(BF16)
            sc = lax.dot_general(qb, kb, (((1,), (1,)), ((), ())),
                                 preferred_element_type=F32) + bias_ref[g, e]
            m = jnp.max(sc, axis=-1, keepdims=True)
            p = jnp.exp2(sc - m).astype(BF16)
            pv = jnp.dot(p, jnp.concatenate([vb, ones], axis=1), preferred_element_type=F32)
            acc, den = pv[:, :ATTN_DH], pv[:, ATTN_DH:]
            if dil != 1:
                og_ref[g - 1, orows, :] = acc
                dg_ref[g - 1, orows, :] = den
                mg_ref[g - 1, orows, :] = jnp.broadcast_to(m, den.shape)
                return carry
            m4, m16 = mg_ref[0, orows, :], mg_ref[1, orows, :]
            mx = jnp.maximum(jnp.maximum(m4, m16), m)
            e1, e4, e16 = jnp.exp2(m - mx), jnp.exp2(m4 - mx), jnp.exp2(m16 - mx)
            num = e1 * acc + e4 * og_ref[0, orows, :] + e16 * og_ref[1, orows, :]
            tot = e1 * den + e4 * dg_ref[0, orows, :] + e16 * dg_ref[1, orows, :]
            o_ref[0, orows, :] = (num * (1.0 / tot)).astype(o_ref.dtype)
            return carry

        lax.fori_loop(0, dil * nqb, q_block, 0, unroll=ATTN_UNROLL)


def dilated_attention(qkv, slopes):
    b, s, _ = qkv.shape
    h, dh = ATTN_HEADS, ATTN_DH
    assert s % (max(ATTN_DILATIONS) * ATTN_TK) == 0
    return pl.pallas_call(
        functools.partial(_attn_body, s=s),
        grid=(h, b),
        in_specs=[
            pl.BlockSpec((1, s, dh), lambda j, i: (i, 0, j)),
            pl.BlockSpec((1, s, dh), lambda j, i: (i, 0, h + j)),
            pl.BlockSpec((1, s, dh), lambda j, i: (i, 0, 2 * h + j)),
            pl.BlockSpec((1, SUBLANES, LANES), lambda j, i: (j, 0, 0)),
        ],
        out_specs=pl.BlockSpec((1, s, dh), lambda j, i: (i, 0, j)),
        out_shape=jax.ShapeDtypeStruct((b, s, h * dh), BF16),
        scratch_shapes=[
            pltpu.VMEM((len(ATTN_DILATIONS), 3, ATTN_TQ, ATTN_TK), F32),
            pltpu.VMEM((3, s, dh), F32),
            pltpu.VMEM((len(ATTN_DILATIONS) - 1, s, dh), F32),
            pltpu.VMEM((len(ATTN_DILATIONS) - 1, s, dh), F32),
            pltpu.VMEM((len(ATTN_DILATIONS) - 1, s, dh), F32),
        ],
        compiler_params=_cparams("parallel", "arbitrary"),
        name="dilated_attn",
    )(qkv, qkv, qkv, slopes)


def _ffn(x, h, w1, w3, w2, layer, next_norm_g, last):
    u = swiglu_up(h, w1, w3, layer)
    return matmul([(u, w2, 0, layer)], res=x, norm_g=next_norm_g,
                  norm_dtype=F32 if last else BF16, keep_sum=not last,
                  tm=RESIDENT_TM, tn=w2.shape[-1], name="ffn_down")


def _even_layer(x, bsz, seq, norm_g, w_in, gate_b, conv_w, conv_b, rg_wa, rg_ba, rg_wx, rg_bx,
                rg_lam, head_g, w_out, next_norm_g):
    heads, dk, dv = MLSTM_HEADS, MLSTM_DK, MLSTM_DV
    n_qkvo = 2 * heads * dk + 2 * heads * dv
    n_gate = 4 * heads
    width = RNN_BLOCKS * RNN_BLOCK
    w_in_t = jnp.transpose(w_in)
    w_main_t = jnp.concatenate([w_in_t[:n_qkvo], w_in_t[n_qkvo + n_gate:]], axis=0).astype(BF16)
    w_gate_t = w_in_t[n_qkvo:n_qkvo + n_gate].astype(BF16)
    z, zg_t = norm_matmul(x, norm_g, w_main_t, w_side_t=w_gate_t, w_transposed=True,
                          name="in_proj")
    z = z.reshape(bsz, seq, -1)

    chunk = MLSTM_CHUNK
    nc = seq // chunk
    gates_rows = jnp.transpose(zg_t[:n_gate].reshape(4, heads, bsz, nc, chunk), (2, 1, 0, 3, 4))
    gate_b_rows = jnp.broadcast_to(
        jnp.transpose(gate_b.astype(F32).reshape(4, heads))[:, :, None, None], (heads, 4, 1, chunk))
    y_a = mlstm_mixer(z, gates_rows, gate_b_rows, head_g.astype(F32).reshape(heads, 1, dv), chunk)

    y_b = rglru_mixer(z, n_qkvo // RNN_BLOCK, (n_qkvo + width) // RNN_BLOCK,
                      conv_w.astype(F32), conv_b.astype(F32).reshape(1, width),
                      (0.5 * rg_wa).astype(BF16), (0.5 * rg_wx).astype(BF16),
                      0.5 * rg_ba.astype(F32), 0.5 * rg_bx.astype(F32), rg_lam.astype(F32))

    w_out_b = w_out.astype(BF16)
    m = bsz * seq
    assert heads * dv == width
    return matmul([(y_a.reshape(m, -1), w_out_b, 0),
                   (y_b.reshape(m, -1), w_out_b, 1)], res=x, norm_g=next_norm_g,
                  tm=RESIDENT_TM, tn=w_out.shape[1], name="out_proj")


def _odd_layer(x, h, bsz, seq, norm_g, w_qkv, w_o, next_norm_g):
    if h is None:
        qkv = norm_matmul(x, norm_g, w_qkv.astype(BF16), tm=RESIDENT_TM // 2, name="qkv_proj")
    else:
        qkv = matmul([(h, w_qkv.astype(BF16))], tm=RESIDENT_TM, tn=w_qkv.shape[1],
                     name="qkv_proj")
    qkv = qkv.reshape(bsz, seq, -1)
    slopes = jnp.exp2(-ALIBI_MAX_BIAS * jnp.arange(1, ATTN_HEADS + 1, dtype=F32) / ATTN_HEADS)
    slopes = jnp.broadcast_to(slopes[:, None, None], (ATTN_HEADS, SUBLANES, LANES))
    o = dilated_attention(qkv, slopes)
    return matmul([(o.reshape(bsz * seq, -1), w_o.astype(BF16))], res=x, norm_g=next_norm_g,
                  tm=RESIDENT_TM, tn=w_o.shape[1], name="attn_out_proj")


def kernel(x, e_norm, e_w_in, e_gate_b, e_conv_w, e_conv_b, e_rg_wa, e_rg_ba, e_rg_wx, e_rg_bx,
           e_rg_lam, e_head_g, e_w_out, o_norm, o_w_qkv, o_w_o, f_norm, f_w1, f_w3, f_w2,
           final_norm):
    bsz, seq, d = x.shape
    depth = f_norm.shape[0]
    xs = x.reshape(bsz * seq, d).astype(F32)
    w1, w3, w2 = f_w1.astype(BF16), f_w3.astype(BF16), f_w2.astype(BF16)
    hs = None
    for l in range(depth):
        if l % 2 == 0:
            e = l // 2
            xs, hs = _even_layer(xs, bsz, seq, e_norm[e], e_w_in[e], e_gate_b[e], e_conv_w[e],
                                 e_conv_b[e], e_rg_wa[e], e_rg_ba[e], e_rg_wx[e], e_rg_bx[e],
                                 e_rg_lam[e], e_head_g[e], e_w_out[e], f_norm[l])
        else:
            o = l // 2
            xs, hs = _odd_layer(xs, hs, bsz, seq, o_norm[o], o_w_qkv[o], o_w_o[o], f_norm[l])
        if l == depth - 1:
            out = _ffn(xs, hs, w1, w3, w2, l, final_norm, last=True)
            return out.astype(x.dtype).reshape(bsz, seq, d)
        if (l + 1) % 2:
            xs, hs = _ffn(xs, hs, w1, w3, w2, l, o_norm[(l + 1) // 2], last=False)
        else:
            xs, hs = _ffn(xs, hs, w1, w3, w2, l, None, last=False), None
```

```python
import functools
import math

import jax
import jax.numpy as jnp
from jax import lax
from jax.experimental import pallas as pl
from jax.experimental.pallas import tpu as pltpu

F32 = jnp.float32
BF16 = jnp.bfloat16

RMS_EPS = 1e-6
NEG_BIG = -1e30
LOG2E = math.log2(math.e)

MLSTM_HEADS = 4
MLSTM_DK = 128
MLSTM_DV = 256
MLSTM_CHUNK = 256
RNN_BLOCKS = 8
RNN_BLOCK = 128
CONV_WIDTH = 4
CONV_LEFT = 2
RGLRU_C = 8.0
ATTN_HEADS = 16
ATTN_DH = 128
ATTN_HALF = 64
ATTN_DILATIONS = (1, 4, 16)
ALIBI_MAX_BIAS = 8.0

LANES = 128
SUBLANES = 8
VMEM_LIMIT_BYTES = 56 * 1024 * 1024

RESIDENT_TM = 512
SWIGLU_TM = 4096
SWIGLU_TN = 512
SWIGLU_ROWS = 1024
NORM_CHUNK = 256


def _cparams(*semantics):
    return pltpu.CompilerParams(dimension_semantics=semantics,
                                vmem_limit_bytes=VMEM_LIMIT_BYTES)


def _weight_spec(w, kdim, tn, row_block=0, layer=None, **mode):
    if w.ndim == 2:
        return pl.BlockSpec((kdim, tn), lambda i, j: (row_block, j), **mode)
    return pl.BlockSpec((None, kdim, tn), lambda i, j: (layer, row_block, j), **mode)


def _matmul_body(*refs, n_pairs, has_res, has_norm, keep_sum):
    pairs = [(refs[2 * p], refs[2 * p + 1]) for p in range(n_pairs)]
    rest = list(refs[2 * n_pairs:])
    r_ref = rest.pop(0) if has_res else None
    g_ref = rest.pop(0) if has_norm else None
    y_ref = rest.pop(0) if keep_sum else None
    h_ref = rest.pop(0) if has_norm else None

    tm = pairs[0][0].shape[0]
    rows = min(NORM_CHUNK, tm) if has_norm else tm
    for r0 in range(0, tm, rows):
        sl = pl.ds(r0, rows)
        acc = None
        for a_ref, w_ref in pairs:
            d = jnp.dot(a_ref[sl, :], w_ref[...], preferred_element_type=F32)
            acc = d if acc is None else acc + d
        if has_res:
            acc = acc + r_ref[sl, :]
        if keep_sum:
            y_ref[sl, :] = acc
        if has_norm:
            ms = jnp.mean(acc * acc, axis=-1, keepdims=True)
            h_ref[sl, :] = (acc * lax.rsqrt(ms + RMS_EPS) * g_ref[...]).astype(h_ref.dtype)


def matmul(pairs, *, tm, tn, res=None, norm_g=None, norm_dtype=BF16, keep_sum=True,
           name="matmul"):
    m = pairs[0][0].shape[0]
    n = pairs[0][1].shape[-1]
    tn = min(tn, n)
    assert m % tm == 0 and n % tn == 0 and (keep_sum or norm_g is not None)
    assert norm_g is None or (tn == n and tm % NORM_CHUNK == 0)
    w_mode = dict(pipeline_mode=pl.Buffered(1)) if tn == n else {}
    in_specs, args = [], []
    for a, w, *where in pairs:
        kdim = a.shape[1]
        kb, layer = (list(where) + [0, None])[:2] if where else (0, None)
        in_specs.append(pl.BlockSpec((tm, kdim), lambda i, j: (i, 0)))
        in_specs.append(_weight_spec(w, kdim, tn, kb, layer, **w_mode))
        args += [a, w]
    if res is not None:
        in_specs.append(pl.BlockSpec((tm, tn), lambda i, j: (i, j)))
        args.append(res)
    out_specs, out_shape = [], []
    if keep_sum:
        out_specs.append(pl.BlockSpec((tm, tn), lambda i, j: (i, j)))
        out_shape.append(jax.ShapeDtypeStruct((m, n), F32))
    if norm_g is not None:
        in_specs.append(pl.BlockSpec((1, n), lambda i, j: (0, 0)))
        args.append(norm_g.reshape(1, n).astype(F32))
        out_specs.append(pl.BlockSpec((tm, n), lambda i, j: (i, 0)))
        out_shape.append(jax.ShapeDtypeStruct((m, n), norm_dtype))
    outs = pl.pallas_call(
        functools.partial(_matmul_body, n_pairs=len(pairs), has_res=res is not None,
                          has_norm=norm_g is not None, keep_sum=keep_sum),
        grid=(m // tm, n // tn),
        in_specs=in_specs,
        out_specs=out_specs,
        out_shape=out_shape,
        compiler_params=_cparams("parallel", "parallel"),
        name=name,
    )(*args)
    return outs if len(outs) > 1 else outs[0]


def _swiglu_up_body(a_ref, w1_ref, w3_ref, o_ref):
    rows = min(SWIGLU_ROWS, a_ref.shape[0])
    for r0 in range(0, a_ref.shape[0], rows):
        a = a_ref[pl.ds(r0, rows), :]
        u = jnp.dot(a, w1_ref[...], preferred_element_type=F32)
        v = jnp.dot(a, w3_ref[...], preferred_element_type=F32)
        o_ref[pl.ds(r0, rows), :] = (u * jax.nn.sigmoid(u) * v).astype(o_ref.dtype)


def swiglu_up(a, w1, w3, layer, tm=SWIGLU_TM, tn=SWIGLU_TN):
    m, kdim = a.shape
    n = w1.shape[-1]
    assert m % tm == 0 and n % tn == 0
    return pl.pallas_call(
        _swiglu_up_body,
        grid=(m // tm, n // tn),
        in_specs=[pl.BlockSpec((tm, kdim), lambda i, j: (i, 0)),
                  _weight_spec(w1, kdim, tn, layer=layer),
                  _weight_spec(w3, kdim, tn, layer=layer)],
        out_specs=pl.BlockSpec((tm, tn), lambda i, j: (i, j)),
        out_shape=jax.ShapeDtypeStruct((m, n), BF16),
        compiler_params=_cparams("parallel", "parallel"),
        name="swiglu_up",
    )(a, w1, w3)


def _norm_matmul_body(*refs, has_side, w_transposed):
    if has_side:
        x_ref, g_ref, w_ref, ws_ref, o_ref, os_ref = refs
    else:
        x_ref, g_ref, w_ref, o_ref = refs
    w_dims = (((1,), (1,)), ((), ())) if w_transposed else (((1,), (0,)), ((), ()))

    g = g_ref[...]
    for r0 in range(0, x_ref.shape[0], NORM_CHUNK):
        sl = pl.ds(r0, NORM_CHUNK)
        x = x_ref[sl, :]
        ms = jnp.mean(x * x, axis=-1, keepdims=True)
        h = (x * lax.rsqrt(ms + RMS_EPS) * g).astype(BF16)
        o_ref[sl, :] = lax.dot_general(h, w_ref[...], w_dims,
                                       preferred_element_type=F32).astype(o_ref.dtype)
        if has_side:
            os_ref[:, sl] = lax.dot_general(ws_ref[...], h, (((1,), (1,)), ((), ())),
                                            preferred_element_type=F32)


def norm_matmul(x, g, w, w_side_t=None, w_transposed=False, out_dtype=F32, tm=RESIDENT_TM,
                name="norm_matmul"):
    m, d = x.shape
    n = w.shape[0] if w_transposed else w.shape[1]
    assert m % tm == 0 and tm % NORM_CHUNK == 0
    resident = dict(pipeline_mode=pl.Buffered(1))
    in_specs = [pl.BlockSpec((tm, d), lambda i: (i, 0)),
                pl.BlockSpec((1, d), lambda i: (0, 0)),
                pl.BlockSpec(w.shape, lambda i: (0, 0), **resident)]
    args = [x, g.reshape(1, d).astype(F32), w]
    out_specs = [pl.BlockSpec((tm, n), lambda i: (i, 0))]
    out_shape = [jax.ShapeDtypeStruct((m, n), out_dtype)]
    if w_side_t is not None:
        ns = w_side_t.shape[0]
        in_specs.append(pl.BlockSpec((ns, d), lambda i: (0, 0), **resident))
        args.append(w_side_t)
        out_specs.append(pl.BlockSpec((ns, tm), lambda i: (0, i)))
        out_shape.append(jax.ShapeDtypeStruct((ns, m), F32))
    outs = pl.pallas_call(
        functools.partial(_norm_matmul_body, has_side=w_side_t is not None,
                          w_transposed=w_transposed),
        grid=(m // tm,),
        in_specs=in_specs,
        out_specs=out_specs,
        out_shape=out_shape,
        compiler_params=_cparams("parallel"),
        name=name,
    )(*args)
    return outs if w_side_t is not None else outs[0]


def _log_sigmoid(x):
    return jnp.minimum(x, 0.0) - jnp.log1p(jnp.exp(-jnp.abs(x)))


SCAN_RADIX = 4


def _lane_scan(x, combine, identity, reverse):
    width = x.shape[-1]
    lane = lax.broadcasted_iota(jnp.int32, x.shape, 1)

    def shifted(v, sh):
        if reverse:
            return jnp.where(lane < width - sh, pltpu.roll(v, width - sh, axis=1), identity)
        return jnp.where(lane >= sh, pltpu.roll(v, sh, axis=1), identity)

    sh = 1
    while sh < width:
        parts = [shifted(x, k * sh) for k in range(1, SCAN_RADIX) if k * sh < width]
        for p in parts:
            x = combine(x, p)
        sh *= SCAN_RADIX
    return x


def _mlstm_body(q_ref, k_ref, v_ref, og_ref, g_ref, gb_ref, hg_ref, y_ref,
                rows_ref, cols_ref, hf_ref, hb_ref, ct_ref, *, nc, chunk):
    dk, dv = MLSTM_DK, MLSTM_DV
    scale = dk ** -0.5

    g = g_ref[0, 0] + gb_ref[0]
    rows = []
    for d in range(2):
        b = _lane_scan(_log_sigmoid(g[2 * d + 1]), jnp.add, 0.0, reverse=d == 1)
        gg = g[2 * d] - b
        gmax = _lane_scan(gg, jnp.maximum, -jnp.inf, reverse=d == 1)
        scaled = [gg * LOG2E, b * LOG2E, gmax * LOG2E]
        rows += scaled
        for j, r in enumerate(scaled):
            rows_ref[3 * d + j] = r
    rowmat = jnp.concatenate(rows + [jnp.zeros((LANES - 6 * nc, chunk), F32)], axis=0)
    colmat = rowmat.T
    for c in range(nc):
        cols_ref[c] = colmat if c == 0 else pltpu.roll(colmat, LANES - c, axis=1)

    ct_ref[...] = jnp.zeros_like(ct_ref)

    row_id = lax.broadcasted_iota(jnp.int32, (chunk, chunk), 0)
    col_id = lax.broadcasted_iota(jnp.int32, (chunk, chunk), 1)
    causal = (row_id >= col_id, row_id <= col_id)
    ones = jnp.ones((chunk, LANES), BF16)

    def lanes2(x):
        return jnp.concatenate([x] * (dv // LANES), axis=1)

    hg = hg_ref[0]

    def emit(r0, h, own_ref, other_ref):
        rows = pl.ds(r0, chunk)
        if other_ref is None:
            own_ref[rows, :] = h
            return
        hm = h + other_ref[rows, :]
        ms = jnp.mean(hm * hm, axis=-1, keepdims=True)
        hn = hm * lax.rsqrt(ms + RMS_EPS) * hg
        y_ref[0, rows, :] = (hn * jax.nn.sigmoid(og_ref[0, rows, :])).astype(y_ref.dtype)

    def chunk_step(c, m, d, own_ref, other_ref):
        r0 = pl.multiple_of(c * chunk, chunk)
        qb = (q_ref[0, pl.ds(r0, chunk), :] * scale).astype(BF16)
        kb = k_ref[0, pl.ds(r0, chunk), :].astype(BF16)
        vf = v_ref[0, pl.ds(r0, chunk), :]
        cols = cols_ref[c]

        def column(j):
            lane = (3 * d + j) * nc
            return jnp.broadcast_to(cols[:, lane:lane + 1], (chunk, LANES))

        gg_col, b_col, gmax_col = column(0), column(1), column(2)
        gg_row = rows_ref[3 * d, pl.ds(c, 1), :]
        end = chunk - 1 if d == 0 else 0
        total = rows_ref[3 * d + 1, pl.ds(c, 1), :][:, end:end + 1]
        gmax_end = rows_ref[3 * d + 2, pl.ds(c, 1), :][:, end:end + 1]

        m_row = jnp.maximum(gmax_col, m)
        w_intra = jnp.exp2(jnp.where(causal[d], gg_row - lanes2(m_row), NEG_BIG))
        w_inter = jnp.exp2(m - m_row)
        s = lax.dot_general(qb, kb, (((1,), (1,)), ((), ())),
                            preferred_element_type=F32) * w_intra
        ct = ct_ref[d]
        intra = jnp.dot(s.astype(BF16), jnp.concatenate([vf.astype(BF16), ones], axis=1),
                        preferred_element_type=F32)
        inter = jnp.dot(qb, ct.astype(BF16), preferred_element_type=F32)
        num = intra[:, :dv] + lanes2(w_inter) * inter[:, :dv]
        den = intra[:, dv:] + w_inter * inter[:, dv:]
        inv = 1.0 / jnp.maximum(jnp.abs(den), jnp.exp2(-(b_col + m_row)))
        emit(r0, num * lanes2(inv), own_ref, other_ref)

        m_new = jnp.maximum(total + m, total + gmax_end)
        w_src = jnp.exp2(total + gg_col - m_new)
        decay = jnp.exp2(total + m - m_new)
        wv = jnp.concatenate([lanes2(w_src) * vf, w_src], axis=1).astype(BF16)
        ct_ref[d] = decay * ct + lax.dot_general(
            kb, wv, (((0,), (0,)), ((), ())), preferred_element_type=F32)
        return m_new

    def first_half(c, carry):
        m_f, m_b = carry
        m_f = chunk_step(c, m_f, 0, hf_ref, None)
        m_b = chunk_step(nc - 1 - c, m_b, 1, hb_ref, None)
        return m_f, m_b

    def second_half(c, carry):
        m_f, m_b = carry
        m_f = chunk_step(c, m_f, 0, None, hb_ref)
        m_b = chunk_step(nc - 1 - c, m_b, 1, None, hf_ref)
        return m_f, m_b

    m0 = jnp.full((1, 1), NEG_BIG, F32)
    carry = lax.fori_loop(0, nc // 2, first_half, (m0, m0), unroll=8)
    lax.fori_loop(nc // 2, nc, second_half, carry, unroll=8)


def mlstm_mixer(z, gates_rows, gate_b_rows, head_g, chunk=MLSTM_CHUNK):
    b, s, _ = z.shape
    h, dk, dv = MLSTM_HEADS, MLSTM_DK, MLSTM_DV
    nc = s // chunk
    assert s % chunk == 0 and 6 * nc <= LANES and chunk % LANES == 0 and nc % 2 == 0
    kq = h * dk // dk
    kv = 2 * h * dk // dv
    ko = kv + h
    return pl.pallas_call(
        functools.partial(_mlstm_body, nc=nc, chunk=chunk),
        grid=(b, h),
        in_specs=[
            pl.BlockSpec((1, s, dk), lambda i, j: (i, 0, j)),
            pl.BlockSpec((1, s, dk), lambda i, j: (i, 0, kq + j)),
            pl.BlockSpec((1, s, dv), lambda i, j: (i, 0, kv + j)),
            pl.BlockSpec((1, s, dv), lambda i, j: (i, 0, ko + j)),
            pl.BlockSpec((1, 1, 4, nc, chunk), lambda i, j: (i, j, 0, 0, 0)),
            pl.BlockSpec((1, 4, 1, chunk), lambda i, j: (j, 0, 0, 0)),
            pl.BlockSpec((1, 1, dv), lambda i, j: (j, 0, 0)),
        ],
        out_specs=pl.BlockSpec((1, s, dv), lambda i, j: (i, 0, j)),
        out_shape=jax.ShapeDtypeStruct((b, s, h * dv), BF16),
        scratch_shapes=[
            pltpu.VMEM((6, nc, chunk), F32),
            pltpu.VMEM((nc, chunk, LANES), F32),
            pltpu.VMEM((s, dv), F32),
            pltpu.VMEM((s, dv), F32),
            pltpu.VMEM((2, dk, dv + LANES), F32),
        ],
        compiler_params=_cparams("parallel", "parallel"),
        name="mlstm",
    )(z, z, z, z, gates_rows, gate_b_rows, head_g)


RG_TILE = 512
RG_PAD = SUBLANES


def _softplus(x):
    return jnp.maximum(x, 0.0) + jnp.log1p(jnp.exp(-jnp.abs(x)))


def _gelu_tanh(x):
    c = math.sqrt(2.0 / math.pi)
    return x * (0.5 * (1.0 + jnp.tanh(c * (x + 0.044715 * (x * x * x)))))


def _rglru_body(x_ref, gr_ref, cw_ref, cb_ref, wa_ref, wx_ref, ba_ref, bx_ref, lam_ref, y_ref,
                xpad_ref, a_ref, u_ref, h_ref, *, s):
    nt = s // RG_TILE
    zeros = jnp.zeros((RG_PAD, LANES), F32)
    xpad_ref[pl.ds(0, RG_PAD), :] = zeros
    xpad_ref[pl.ds(RG_PAD + s, RG_PAD), :] = zeros

    def copy_in(t, carry):
        r0 = pl.multiple_of(t * RG_TILE, RG_TILE)
        xpad_ref[pl.ds(RG_PAD + r0, RG_TILE), :] = x_ref[0, pl.ds(r0, RG_TILE), :]
        return carry

    lax.fori_loop(0, nt, copy_in, 0)

    cw = cw_ref[...]
    cb = cb_ref[...]
    half_sp = [0.5 * RGLRU_C * _softplus(-lam_ref[d:d + 1, :]) for d in range(2)]

    def gates(t, carry):
        r0 = pl.multiple_of(t * RG_TILE, RG_TILE)
        xc = None
        for j in range(CONV_WIDTH):
            tap = xpad_ref[pl.ds(r0 + RG_PAD - CONV_LEFT + j, RG_TILE), :] * cw[j:j + 1, :]
            xc = tap if xc is None else xc + tap
        xc = xc + cb
        xcb = xc.astype(BF16)
        half_xc = 0.5 * xc
        for d in range(2):
            t_r = jnp.tanh(jnp.dot(xcb, wa_ref[d, 0], preferred_element_type=F32)
                           + ba_ref[d:d + 1, :])
            t_i = jnp.tanh(jnp.dot(xcb, wx_ref[d, 0], preferred_element_type=F32)
                           + bx_ref[d:d + 1, :])
            neg_log_a = half_sp[d] * t_r + half_sp[d]
            gated_x = half_xc * t_i + half_xc
            a = jnp.exp(-neg_log_a)
            one_minus_a2 = jnp.tanh(neg_log_a) * (1.0 + a * a)
            root = jnp.where(one_minus_a2 > 0.0, one_minus_a2 * lax.rsqrt(one_minus_a2), 0.0)
            a_ref[d, pl.ds(r0, RG_TILE), :] = a
            u_ref[d, pl.ds(r0, RG_TILE), :] = root * gated_x
        return carry

    lax.fori_loop(0, nt, gates, 0)

    nb = RG_TILE // SUBLANES
    sub = lax.broadcasted_iota(jnp.int32, (SUBLANES, LANES), 0)

    def compose_sublanes(a, u, reverse):
        sh = 1
        while sh < SUBLANES:
            keep = sub < SUBLANES - sh if reverse else sub >= sh
            amount = SUBLANES - sh if reverse else sh
            a_s = jnp.where(keep, pltpu.roll(a, amount, axis=0), 1.0)
            u_s = jnp.where(keep, pltpu.roll(u, amount, axis=0), 0.0)
            u = a * u_s + u
            a = a * a_s
            sh *= 2
        return a, u

    def tile_recurrence(t, d, carry):
        r0 = pl.multiple_of(t * RG_TILE, RG_TILE)
        reverse = d == 1
        a_rows, u_rows = [None] * SUBLANES, [None] * SUBLANES
        a_prev = u_prev = None
        for r in (reversed(range(SUBLANES)) if reverse else range(SUBLANES)):
            rows = pl.ds(r0 + r, nb, stride=SUBLANES)
            a, u = a_ref[d, rows, :], u_ref[d, rows, :]
            if a_prev is not None:
                u = a * u_prev + u
                a = a * a_prev
            a_rows[r], u_rows[r] = a, u
            a_prev, u_prev = a, u
        entry = [None] * (nb // SUBLANES)
        for v in (reversed(range(nb // SUBLANES)) if reverse else range(nb // SUBLANES)):
            blk = slice(v * SUBLANES, (v + 1) * SUBLANES)
            sa, su = compose_sublanes(a_prev[blk], u_prev[blk], reverse)
            after = sa * carry + su
            if reverse:
                entry[v] = jnp.where(sub < SUBLANES - 1, pltpu.roll(after, SUBLANES - 1, axis=0),
                                     carry)
                carry = after[0:1]
            else:
                entry[v] = jnp.where(sub >= 1, pltpu.roll(after, 1, axis=0), carry)
                carry = after[SUBLANES - 1:SUBLANES]
        h_entry = jnp.concatenate(entry, axis=0)
        for r in range(SUBLANES):
            h_ref[d, pl.ds(r0 + r, nb, stride=SUBLANES), :] = a_rows[r] * h_entry + u_rows[r]
        return carry

    def recurrences(t, carry):
        h_f, h_b = carry
        return tile_recurrence(t, 0, h_f), tile_recurrence(nt - 1 - t, 1, h_b)

    h0 = jnp.zeros((1, LANES), F32)
    lax.fori_loop(0, nt, recurrences, (h0, h0))

    def finish(t, carry):
        r0 = pl.multiple_of(t * RG_TILE, RG_TILE)
        hr = h_ref[0, pl.ds(r0, RG_TILE), :] + h_ref[1, pl.ds(r0, RG_TILE), :]
        y_ref[0, pl.ds(r0, RG_TILE), :] = (
            hr * _gelu_tanh(gr_ref[0, pl.ds(r0, RG_TILE), :])).astype(y_ref.dtype)
        return carry

    lax.fori_loop(0, nt, finish, 0)


def rglru_mixer(z, xr_block0, gr_block0, conv_w, conv_b, wa, wx, ba, bx, lam):
    b, s, _ = z.shape
    width = RNN_BLOCKS * RNN_BLOCK
    assert s % RG_TILE == 0
    return pl.pallas_call(
        functools.partial(_rglru_body, s=s),
        grid=(b, RNN_BLOCKS),
        in_specs=[
            pl.BlockSpec((1, s, RNN_BLOCK), lambda i, j: (i, 0, xr_block0 + j)),
            pl.BlockSpec((1, s, RNN_BLOCK), lambda i, j: (i, 0, gr_block0 + j)),
            pl.BlockSpec((CONV_WIDTH, RNN_BLOCK), lambda i, j: (0, j)),
            pl.BlockSpec((1, RNN_BLOCK), lambda i, j: (0, j)),
            pl.BlockSpec((2, 1, RNN_BLOCK, RNN_BLOCK), lambda i, j: (0, j, 0, 0)),
            pl.BlockSpec((2, 1, RNN_BLOCK, RNN_BLOCK), lambda i, j: (0, j, 0, 0)),
            pl.BlockSpec((2, RNN_BLOCK), lambda i, j: (0, j)),
            pl.BlockSpec((2, RNN_BLOCK), lambda i, j: (0, j)),
            pl.BlockSpec((2, RNN_BLOCK), lambda i, j: (0, j)),
        ],
        out_specs=pl.BlockSpec((1, s, RNN_BLOCK), lambda i, j: (i, 0, j)),
        out_shape=jax.ShapeDtypeStruct((b, s, width), BF16),
        scratch_shapes=[
            pltpu.VMEM((s + 2 * RG_PAD, RNN_BLOCK), F32),
            pltpu.VMEM((2, s, RNN_BLOCK), F32),
            pltpu.VMEM((2, s, RNN_BLOCK), F32),
            pltpu.VMEM((2, s, RNN_BLOCK), F32),
        ],
        compiler_params=_cparams("parallel", "parallel"),
        name="rglru",
    )(z, z, conv_w, conv_b, wa, wx, ba, bx, lam)


ATTN_TQ = 128
ATTN_TK = ATTN_TQ + 2 * ATTN_HALF
ATTN_UNROLL = 32


def _attn_body(q_ref, k_ref, v_ref, slope_ref, o_ref, bias_ref, x4_ref, og_ref, dg_ref, mg_ref,
               *, s):
    qscale = ATTN_DH ** -0.5 * LOG2E

    @pl.when(pl.program_id(1) == 0)
    def _():
        slope = slope_ref[0, 0:1, 0:1] * LOG2E
        qi = lax.broadcasted_iota(jnp.int32, (ATTN_TQ, ATTN_TK), 0)
        kj = lax.broadcasted_iota(jnp.int32, (ATTN_TQ, ATTN_TK), 1)
        for g, dil in enumerate(ATTN_DILATIONS):
            for e in range(3):
                rel = jnp.abs(kj - e * ATTN_HALF - qi)
                pen = slope * (rel * dil).astype(F32)
                bias_ref[g, e] = jnp.where(rel <= ATTN_HALF, -pen, NEG_BIG)

    s4 = s // 4
    srcs = (q_ref, k_ref, v_ref)

    def split4(t, carry):
        c = t // (s4 // ATTN_TK)
        p0 = (t % (s4 // ATTN_TK)) * ATTN_TK
        dst = pl.ds(pl.multiple_of(c * s4 + p0, ATTN_TK), ATTN_TK)
        for a in range(3):
            x = srcs[a][0, pl.ds(c + 4 * p0, ATTN_TK, stride=4), :]
            x4_ref[a, dst, :] = x * qscale if a == 0 else x
        return carry

    lax.fori_loop(0, 4 * (s4 // ATTN_TK), split4, 0, unroll=4)

    ones = jnp.ones((ATTN_TK, ATTN_DH), BF16)

    for g, dil in reversed(list(enumerate(ATTN_DILATIONS))):
        sp = s // dil
        nqb = sp // ATTN_TQ

        def q_block(t, carry, g=g, dil=dil, sp=sp, nqb=nqb):
            r = t // nqb
            p0 = (t % nqb) * ATTN_TQ
            kstart = jnp.clip(p0 - ATTN_HALF, 0, sp - ATTN_TK)
            e = (p0 - kstart) // ATTN_HALF
            if dil == 1:
                qrows = pl.ds(pl.multiple_of(p0, ATTN_TQ), ATTN_TQ)
                krows = pl.ds(pl.multiple_of(kstart, ATTN_HALF), ATTN_TK)
                qf = q_ref[0, qrows, :] * qscale
                kf, vf = k_ref[0, krows, :], v_ref[0, krows, :]
                orows = qrows
            elif dil == 4:
                qrows = pl.ds(pl.multiple_of(r * s4 + p0, ATTN_TQ), ATTN_TQ)
                krows = pl.ds(pl.multiple_of(r * s4 + kstart, ATTN_HALF), ATTN_TK)
                qf, kf, vf = x4_ref[0, qrows, :], x4_ref[1, krows, :], x4_ref[2, krows, :]
                orows = pl.ds(r + 4 * p0, ATTN_TQ, stride=4)
            else:
                base = (r % 4) * s4 + r // 4
                qrows = pl.ds(base + 4 * p0, ATTN_TQ, stride=4)
                krows = pl.ds(base + 4 * kstart, ATTN_TK, stride=4)
                qf, kf, vf = x4_ref[0, qrows, :], x4_ref[1, krows, :], x4_ref[2, krows, :]
                orows = pl.ds(r + dil * p0, ATTN_TQ, stride=dil)
            qb, kb, vb = qf.astype(BF16), kf.astype(BF16), vf.astype(BF16)
            sc = lax.dot_general(qb, kb, (((1,), (1,)), ((), ())),
                                 preferred_element_type=F32) + bias_ref[g, e]
            m = jnp.max(sc, axis=-1, keepdims=True)
            p = jnp.exp2(sc - m).astype(BF16)
            pv = jnp.dot(p, jnp.concatenate([vb, ones], axis=1), preferred_element_type=F32)
            acc, den = pv[:, :ATTN_DH], pv[:, ATTN_DH:]
            if dil != 1:
                og_ref[g - 1, orows, :] = acc
                dg_ref[g - 1, orows, :] = den
                mg_ref[g - 1, orows, :] = jnp.broadcast_to(m, den.shape)
                return carry
            m4, m16 = mg_ref[0, orows, :], mg_ref[1, orows, :]
            mx = jnp.maximum(jnp.maximum(m4, m16), m)
            e1, e4, e16 = jnp.exp2(m - mx), jnp.exp2(m4 - mx), jnp.exp2(m16 - mx)
            num = e1 * acc + e4 * og_ref[0, orows, :] + e16 * og_ref[1, orows, :]
            tot = e1 * den + e4 * dg_ref[0, orows, :] + e16 * dg_ref[1, orows, :]
            o_ref[0, orows, :] = (num * (1.0 / tot)).astype(o_ref.dtype)
            return carry

        lax.fori_loop(0, dil * nqb, q_block, 0, unroll=ATTN_UNROLL)


def dilated_attention(qkv, slopes):
    b, s, _ = qkv.shape
    h, dh = ATTN_HEADS, ATTN_DH
    assert s % (max(ATTN_DILATIONS) * ATTN_TK) == 0
    return pl.pallas_call(
        functools.partial(_attn_body, s=s),
        grid=(h, b),
        in_specs=[
            pl.BlockSpec((1, s, dh), lambda j, i: (i, 0, j)),
            pl.BlockSpec((1, s, dh), lambda j, i: (i, 0, h + j)),
            pl.BlockSpec((1, s, dh), lambda j, i: (i, 0, 2 * h + j)),
            pl.BlockSpec((1, SUBLANES, LANES), lambda j, i: (j, 0, 0)),
        ],
        out_specs=pl.BlockSpec((1, s, dh), lambda j, i: (i, 0, j)),
        out_shape=jax.ShapeDtypeStruct((b, s, h * dh), BF16),
        scratch_shapes=[
            pltpu.VMEM((len(ATTN_DILATIONS), 3, ATTN_TQ, ATTN_TK), F32),
            pltpu.VMEM((3, s, dh), F32),
            pltpu.VMEM((len(ATTN_DILATIONS) - 1, s, dh), F32),
            pltpu.VMEM((len(ATTN_DILATIONS) - 1, s, dh), F32),
            pltpu.VMEM((len(ATTN_DILATIONS) - 1, s, dh), F32),
        ],
        compiler_params=_cparams("parallel", "arbitrary"),
        name="dilated_attn",
    )(qkv, qkv, qkv, slopes)


def _ffn(x, h, w1, w3, w2, layer, next_norm_g, last):
    u = swiglu_up(h, w1, w3, layer)
    return matmul([(u, w2, 0, layer)], res=x, norm_g=next_norm_g,
                  norm_dtype=F32 if last else BF16, keep_sum=not last,
                  tm=RESIDENT_TM, tn=w2.shape[-1], name="ffn_down")


def _even_layer(x, bsz, seq, norm_g, w_in, gate_b, conv_w, conv_b, rg_wa, rg_ba, rg_wx, rg_bx,
                rg_lam, head_g, w_out, next_norm_g):
    heads, dk, dv = MLSTM_HEADS, MLSTM_DK, MLSTM_DV
    n_qkvo = 2 * heads * dk + 2 * heads * dv
    n_gate = 4 * heads
    width = RNN_BLOCKS * RNN_BLOCK
    w_in_t = jnp.transpose(w_in)
    w_main_t = jnp.concatenate([w_in_t[:n_qkvo], w_in_t[n_qkvo + n_gate:]], axis=0).astype(BF16)
    w_gate_t = w_in_t[n_qkvo:n_qkvo + n_gate].astype(BF16)
    z, zg_t = norm_matmul(x, norm_g, w_main_t, w_side_t=w_gate_t, w_transposed=True,
                          name="in_proj")
    z = z.reshape(bsz, seq, -1)

    chunk = MLSTM_CHUNK
    nc = seq // chunk
    gates_rows = jnp.transpose(zg_t[:n_gate].reshape(4, heads, bsz, nc, chunk), (2, 1, 0, 3, 4))
    gate_b_rows = jnp.broadcast_to(
        jnp.transpose(gate_b.astype(F32).reshape(4, heads))[:, :, None, None], (heads, 4, 1, chunk))
    y_a = mlstm_mixer(z, gates_rows, gate_b_rows, head_g.astype(F32).reshape(heads, 1, dv), chunk)

    y_b = rglru_mixer(z, n_qkvo // RNN_BLOCK, (n_qkvo + width) // RNN_BLOCK,
                      conv_w.astype(F32), conv_b.astype(F32).reshape(1, width),
                      (0.5 * rg_wa).astype(BF16), (0.5 * rg_wx).astype(BF16),
                      0.5 * rg_ba.astype(F32), 0.5 * rg_bx.astype(F32), rg_lam.astype(F32))

    w_out_b = w_out.astype(BF16)
    m = bsz * seq
    assert heads * dv == width
    return matmul([(y_a.reshape(m, -1), w_out_b, 0),
                   (y_b.reshape(m, -1), w_out_b, 1)], res=x, norm_g=next_norm_g,
                  tm=RESIDENT_TM, tn=w_out.shape[1], name="out_proj")


def _odd_layer(x, h, bsz, seq, norm_g, w_qkv, w_o, next_norm_g):
    if h is None:
        qkv = norm_matmul(x, norm_g, w_qkv.astype(BF16), tm=RESIDENT_TM // 2, name="qkv_proj")
    else:
        qkv = matmul([(h, w_qkv.astype(BF16))], tm=RESIDENT_TM, tn=w_qkv.shape[1],
                     name="qkv_proj")
    qkv = qkv.reshape(bsz, seq, -1)
    slopes = jnp.exp2(-ALIBI_MAX_BIAS * jnp.arange(1, ATTN_HEADS + 1, dtype=F32) / ATTN_HEADS)
    slopes = jnp.broadcast_to(slopes[:, None, None], (ATTN_HEADS, SUBLANES, LANES))
    o = dilated_attention(qkv, slopes)
    return matmul([(o.reshape(bsz * seq, -1), w_o.astype(BF16))], res=x, norm_g=next_norm_g,
                  tm=RESIDENT_TM, tn=w_o.shape[1], name="attn_out_proj")


def kernel(x, e_norm, e_w_in, e_gate_b, e_conv_w, e_conv_b, e_rg_wa, e_rg_ba, e_rg_wx, e_rg_bx,
           e_rg_lam, e_head_g, e_w_out, o_norm, o_w_qkv, o_w_o, f_norm, f_w1, f_w3, f_w2,
           final_norm):
    bsz, seq, d = x.shape
    depth = f_norm.shape[0]
    xs = x.reshape(bsz * seq, d).astype(F32)
    w1, w3, w2 = f_w1.astype(BF16), f_w3.astype(BF16), f_w2.astype(BF16)
    hs = None
    for l in range(depth):
        if l % 2 == 0:
            e = l // 2
            xs, hs = _even_layer(xs, bsz, seq, e_norm[e], e_w_in[e], e_gate_b[e], e_conv_w[e],
                                 e_conv_b[e], e_rg_wa[e], e_rg_ba[e], e_rg_wx[e], e_rg_bx[e],
                                 e_rg_lam[e], e_head_g[e], e_w_out[e], f_norm[l])
        else:
            o = l // 2
            xs, hs = _odd_layer(xs, hs, bsz, seq, o_norm[o], o_w_qkv[o], o_w_o[o], f_norm[l])
        if l == depth - 1:
            out = _ffn(xs, hs, w1, w3, w2, l, final_norm, last=True)
            return out.astype(x.dtype).reshape(bsz, seq, d)
        if (l + 1) % 2:
            xs, hs = _ffn(xs, hs, w1, w3, w2, l, o_norm[(l + 1) // 2], last=False)
        else:
            xs, hs = _ffn(xs, hs, w1, w3, w2, l, None, last=False), None
```

```python
import functools
import math

import jax
import jax.numpy as jnp
from jax import lax
from jax.experimental import pallas as pl
from jax.experimental.pallas import tpu as pltpu

F32 = jnp.float32
BF16 = jnp.bfloat16

RMS_EPS = 1e-6
NEG_BIG = -1e30
LOG2E = math.log2(math.e)

MLSTM_HEADS = 4
MLSTM_DK = 128
MLSTM_DV = 256
MLSTM_CHUNK = 256
RNN_BLOCKS = 8
RNN_BLOCK = 128
CONV_WIDTH = 4
CONV_LEFT = 2
RGLRU_C = 8.0
ATTN_HEADS = 16
ATTN_DH = 128
ATTN_HALF = 64
ATTN_DILATIONS = (1, 4, 16)
ALIBI_MAX_BIAS = 8.0

LANES = 128
SUBLANES = 8
VMEM_LIMIT_BYTES = 56 * 1024 * 1024

RESIDENT_TM = 512
SWIGLU_TM = 4096
SWIGLU_TN = 512
SWIGLU_ROWS = 1024
NORM_CHUNK = 256


def _cparams(*semantics):
    return pltpu.CompilerParams(dimension_semantics=semantics,
                                vmem_limit_bytes=VMEM_LIMIT_BYTES)


def _weight_spec(w, kdim, tn, row_block=0, layer=None, **mode):
    if w.ndim == 2:
        return pl.BlockSpec((kdim, tn), lambda i, j: (row_block, j), **mode)
    return pl.BlockSpec((None, kdim, tn), lambda i, j: (layer, row_block, j), **mode)


def _matmul_body(*refs, n_pairs, has_res, has_norm, keep_sum):
    pairs = [(refs[2 * p], refs[2 * p + 1]) for p in range(n_pairs)]
    rest = list(refs[2 * n_pairs:])
    r_ref = rest.pop(0) if has_res else None
    g_ref = rest.pop(0) if has_norm else None
    y_ref = rest.pop(0) if keep_sum else None
    h_ref = rest.pop(0) if has_norm else None

    tm = pairs[0][0].shape[0]
    rows = min(NORM_CHUNK, tm) if has_norm else tm
    for r0 in range(0, tm, rows):
        sl = pl.ds(r0, rows)
        acc = None
        for a_ref, w_ref in pairs:
            d = jnp.dot(a_ref[sl, :], w_ref[...], preferred_element_type=F32)
            acc = d if acc is None else acc + d
        if has_res:
            acc = acc + r_ref[sl, :]
        if keep_sum:
            y_ref[sl, :] = acc
        if has_norm:
            ms = jnp.mean(acc * acc, axis=-1, keepdims=True)
            h_ref[sl, :] = (acc * lax.rsqrt(ms + RMS_EPS) * g_ref[...]).astype(h_ref.dtype)


def matmul(pairs, *, tm, tn, res=None, norm_g=None, norm_dtype=BF16, keep_sum=True,
           name="matmul"):
    m = pairs[0][0].shape[0]
    n = pairs[0][1].shape[-1]
    tn = min(tn, n)
    assert m % tm == 0 and n % tn == 0 and (keep_sum or norm_g is not None)
    assert norm_g is None or (tn == n and tm % NORM_CHUNK == 0)
    w_mode = dict(pipeline_mode=pl.Buffered(1)) if tn == n else {}
    in_specs, args = [], []
    for a, w, *where in pairs:
        kdim = a.shape[1]
        kb, layer = (list(where) + [0, None])[:2] if where else (0, None)
        in_specs.append(pl.BlockSpec((tm, kdim), lambda i, j: (i, 0)))
        in_specs.append(_weight_spec(w, kdim, tn, kb, layer, **w_mode))
        args += [a, w]
    if res is not None:
        in_specs.append(pl.BlockSpec((tm, tn), lambda i, j: (i, j)))
        args.append(res)
    out_specs, out_shape = [], []
    if keep_sum:
        out_specs.append(pl.BlockSpec((tm, tn), lambda i, j: (i, j)))
        out_shape.append(jax.ShapeDtypeStruct((m, n), F32))
    if norm_g is not None:
        in_specs.append(pl.BlockSpec((1, n), lambda i, j: (0, 0)))
        args.append(norm_g.reshape(1, n).astype(F32))
        out_specs.append(pl.BlockSpec((tm, n), lambda i, j: (i, 0)))
        out_shape.append(jax.ShapeDtypeStruct((m, n), norm_dtype))
    outs = pl.pallas_call(
        functools.partial(_matmul_body, n_pairs=len(pairs), has_res=res is not None,
                          has_norm=norm_g is not None, keep_sum=keep_sum),
        grid=(m // tm, n // tn),
        in_specs=in_specs,
        out_specs=out_specs,
        out_shape=out_shape,
        compiler_params=_cparams("parallel", "parallel"),
        name=name,
    )(*args)
    return outs if len(outs) > 1 else outs[0]


def _swiglu_up_body(a_ref, w1_ref, w3_ref, o_ref):
    rows = min(SWIGLU_ROWS, a_ref.shape[0])
    for r0 in range(0, a_ref.shape[0], rows):
        a = a_ref[pl.ds(r0, rows), :]
        u = jnp.dot(a, w1_ref[...], preferred_element_type=F32)
        v = jnp.dot(a, w3_ref[...], preferred_element_type=F32)
        o_ref[pl.ds(r0, rows), :] = (u * jax.nn.sigmoid(u) * v).astype(o_ref.dtype)


def swiglu_up(a, w1, w3, layer, tm=SWIGLU_TM, tn=SWIGLU_TN):
    m, kdim = a.shape
    n = w1.shape[-1]
    assert m % tm == 0 and n % tn == 0
    return pl.pallas_call(
        _swiglu_up_body,
        grid=(m // tm, n // tn),
        in_specs=[pl.BlockSpec((tm, kdim), lambda i, j: (i, 0)),
                  _weight_spec(w1, kdim, tn, layer=layer),
                  _weight_spec(w3, kdim, tn, layer=layer)],
        out_specs=pl.BlockSpec((tm, tn), lambda i, j: (i, j)),
        out_shape=jax.ShapeDtypeStruct((m, n), BF16),
        compiler_params=_cparams("parallel", "parallel"),
        name="swiglu_up",
    )(a, w1, w3)


def _norm_matmul_body(*refs, has_side, w_transposed):
    if has_side:
        x_ref, g_ref, w_ref, ws_ref, o_ref, os_ref = refs
    else:
        x_ref, g_ref, w_ref, o_ref = refs
    w_dims = (((1,), (1,)), ((), ())) if w_transposed else (((1,), (0,)), ((), ()))

    g = g_ref[...]
    for r0 in range(0, x_ref.shape[0], NORM_CHUNK):
        sl = pl.ds(r0, NORM_CHUNK)
        x = x_ref[sl, :]
        ms = jnp.mean(x * x, axis=-1, keepdims=True)
        h = (x * lax.rsqrt(ms + RMS_EPS) * g).astype(BF16)
        o_ref[sl, :] = lax.dot_general(h, w_ref[...], w_dims,
                                       preferred_element_type=F32).astype(o_ref.dtype)
        if has_side:
            os_ref[:, sl] = lax.dot_general(ws_ref[...], h, (((1,), (1,)), ((), ())),
                                            preferred_element_type=F32)


def norm_matmul(x, g, w, w_side_t=None, w_transposed=False, out_dtype=F32, tm=RESIDENT_TM,
                name="norm_matmul"):
    m, d = x.shape
    n = w.shape[0] if w_transposed else w.shape[1]
    assert m % tm == 0 and tm % NORM_CHUNK == 0
    resident = dict(pipeline_mode=pl.Buffered(1))
    in_specs = [pl.BlockSpec((tm, d), lambda i: (i, 0)),
                pl.BlockSpec((1, d), lambda i: (0, 0)),
                pl.BlockSpec(w.shape, lambda i: (0, 0), **resident)]
    args = [x, g.reshape(1, d).astype(F32), w]
    out_specs = [pl.BlockSpec((tm, n), lambda i: (i, 0))]
    out_shape = [jax.ShapeDtypeStruct((m, n), out_dtype)]
    if w_side_t is not None:
        ns = w_side_t.shape[0]
        in_specs.append(pl.BlockSpec((ns, d), lambda i: (0, 0), **resident))
        args.append(w_side_t)
        out_specs.append(pl.BlockSpec((ns, tm), lambda i: (0, i)))
        out_shape.append(jax.ShapeDtypeStruct((ns, m), F32))
    outs = pl.pallas_call(
        functools.partial(_norm_matmul_body, has_side=w_side_t is not None,
                          w_transposed=w_transposed),
        grid=(m // tm,),
        in_specs=in_specs,
        out_specs=out_specs,
        out_shape=out_shape,
        compiler_params=_cparams("parallel"),
        name=name,
    )(*args)
    return outs if w_side_t is not None else outs[0]


def _log_sigmoid(x):
    return jnp.minimum(x, 0.0) - jnp.log1p(jnp.exp(-jnp.abs(x)))


SCAN_RADIX = 4


def _lane_scan(x, combine, identity, reverse):
    width = x.shape[-1]
    lane = lax.broadcasted_iota(jnp.int32, x.shape, 1)

    def shifted(v, sh):
        if reverse:
            return jnp.where(lane < width - sh, pltpu.roll(v, width - sh, axis=1), identity)
        return jnp.where(lane >= sh, pltpu.roll(v, sh, axis=1), identity)

    sh = 1
    while sh < width:
        parts = [shifted(x, k * sh) for k in range(1, SCAN_RADIX) if k * sh < width]
        for p in parts:
            x = combine(x, p)
        sh *= SCAN_RADIX
    return x


def _mlstm_body(q_ref, k_ref, v_ref, og_ref, g_ref, gb_ref, hg_ref, y_ref,
                rows_ref, cols_ref, hf_ref, hb_ref, ct_ref, *, nc, chunk):
    dk, dv = MLSTM_DK, MLSTM_DV
    scale = dk ** -0.5

    g = g_ref[0, 0] + gb_ref[0]
    rows = []
    for d in range(2):
        b = _lane_scan(_log_sigmoid(g[2 * d + 1]), jnp.add, 0.0, reverse=d == 1)
        gg = g[2 * d] - b
        gmax = _lane_scan(gg, jnp.maximum, -jnp.inf, reverse=d == 1)
        scaled = [gg * LOG2E, b * LOG2E, gmax * LOG2E]
        rows += scaled
        for j, r in enumerate(scaled):
            rows_ref[3 * d + j] = r
    rowmat = jnp.concatenate(rows + [jnp.zeros((LANES - 6 * nc, chunk), F32)], axis=0)
    colmat = rowmat.T
    for c in range(nc):
        cols_ref[c] = colmat if c == 0 else pltpu.roll(colmat, LANES - c, axis=1)

    ct_ref[...] = jnp.zeros_like(ct_ref)

    row_id = lax.broadcasted_iota(jnp.int32, (chunk, chunk), 0)
    col_id = lax.broadcasted_iota(jnp.int32, (chunk, chunk), 1)
    causal = (row_id >= col_id, row_id <= col_id)
    ones = jnp.ones((chunk, LANES), BF16)

    def lanes2(x):
        return jnp.concatenate([x] * (dv // LANES), axis=1)

    hg = hg_ref[0]

    def emit(r0, h, own_ref, other_ref):
        rows = pl.ds(r0, chunk)
        if other_ref is None:
            own_ref[rows, :] = h
            return
        hm = h + other_ref[rows, :]
        ms = jnp.mean(hm * hm, axis=-1, keepdims=True)
        hn = hm * lax.rsqrt(ms + RMS_EPS) * hg
        y_ref[0, rows, :] = (hn * jax.nn.sigmoid(og_ref[0, rows, :])).astype(y_ref.dtype)

    def chunk_step(c, m, d, own_ref, other_ref):
        r0 = pl.multiple_of(c * chunk, chunk)
        qb = (q_ref[0, pl.ds(r0, chunk), :] * scale).astype(BF16)
        kb = k_ref[0, pl.ds(r0, chunk), :].astype(BF16)
        vf = v_ref[0, pl.ds(r0, chunk), :]
        cols = cols_ref[c]

        def column(j):
            lane = (3 * d + j) * nc
            return jnp.broadcast_to(cols[:, lane:lane + 1], (chunk, LANES))

        gg_col, b_col, gmax_col = column(0), column(1), column(2)
        gg_row = rows_ref[3 * d, pl.ds(c, 1), :]
        end = chunk - 1 if d == 0 else 0
        total = rows_ref[3 * d + 1, pl.ds(c, 1), :][:, end:end + 1]
        gmax_end = rows_ref[3 * d + 2, pl.ds(c, 1), :][:, end:end + 1]

        m_row = jnp.maximum(gmax_col, m)
        w_intra = jnp.exp2(jnp.where(causal[d], gg_row - lanes2(m_row), NEG_BIG))
        w_inter = jnp.exp2(m - m_row)
        s = lax.dot_general(qb, kb, (((1,), (1,)), ((), ())),
                            preferred_element_type=F32) * w_intra
        ct = ct_ref[d]
        intra = jnp.dot(s.astype(BF16), jnp.concatenate([vf.astype(BF16), ones], axis=1),
                        preferred_element_type=F32)
        inter = jnp.dot(qb, ct.astype(BF16), preferred_element_type=F32)
        num = intra[:, :dv] + lanes2(w_inter) * inter[:, :dv]
        den = intra[:, dv:] + w_inter * inter[:, dv:]
        inv = 1.0 / jnp.maximum(jnp.abs(den), jnp.exp2(-(b_col + m_row)))
        emit(r0, num * lanes2(inv), own_ref, other_ref)

        m_new = jnp.maximum(total + m, total + gmax_end)
        w_src = jnp.exp2(total + gg_col - m_new)
        decay = jnp.exp2(total + m - m_new)
        wv = jnp.concatenate([lanes2(w_src) * vf, w_src], axis=1).astype(BF16)
        ct_ref[d] = decay * ct + lax.dot_general(
            kb, wv, (((0,), (0,)), ((), ())), preferred_element_type=F32)
        return m_new

    def first_half(c, carry):
        m_f, m_b = carry
        m_f = chunk_step(c, m_f, 0, hf_ref, None)
        m_b = chunk_step(nc - 1 - c, m_b, 1, hb_ref, None)
        return m_f, m_b

    def second_half(c, carry):
        m_f, m_b = carry
        m_f = chunk_step(c, m_f, 0, None, hb_ref)
        m_b = chunk_step(nc - 1 - c, m_b, 1, None, hf_ref)
        return m_f, m_b

    m0 = jnp.full((1, 1), NEG_BIG, F32)
    carry = lax.fori_loop(0, nc // 2, first_half, (m0, m0), unroll=8)
    lax.fori_loop(nc // 2, nc, second_half, carry, unroll=8)


def mlstm_mixer(z, gates_rows, gate_b_rows, head_g, chunk=MLSTM_CHUNK):
    b, s, _ = z.shape
    h, dk, dv = MLSTM_HEADS, MLSTM_DK, MLSTM_DV
    nc = s // chunk
    assert s % chunk == 0 and 6 * nc <= LANES and chunk % LANES == 0 and nc % 2 == 0
    kq = h * dk // dk
    kv = 2 * h * dk // dv
    ko = kv + h
    return pl.pallas_call(
        functools.partial(_mlstm_body, nc=nc, chunk=chunk),
        grid=(b, h),
        in_specs=[
            pl.BlockSpec((1, s, dk), lambda i, j: (i, 0, j)),
            pl.BlockSpec((1, s, dk), lambda i, j: (i, 0, kq + j)),
            pl.BlockSpec((1, s, dv), lambda i, j: (i, 0, kv + j)),
            pl.BlockSpec((1, s, dv), lambda i, j: (i, 0, ko + j)),
            pl.BlockSpec((1, 1, 4, nc, chunk), lambda i, j: (i, j, 0, 0, 0)),
            pl.BlockSpec((1, 4, 1, chunk), lambda i, j: (j, 0, 0, 0)),
            pl.BlockSpec((1, 1, dv), lambda i, j: (j, 0, 0)),
        ],
        out_specs=pl.BlockSpec((1, s, dv), lambda i, j: (i, 0, j)),
        out_shape=jax.ShapeDtypeStruct((b, s, h * dv), BF16),
        scratch_shapes=[
            pltpu.VMEM((6, nc, chunk), F32),
            pltpu.VMEM((nc, chunk, LANES), F32),
            pltpu.VMEM((s, dv), F32),
            pltpu.VMEM((s, dv), F32),
            pltpu.VMEM((2, dk, dv + LANES), F32),
        ],
        compiler_params=_cparams("parallel", "parallel"),
        name="mlstm",
    )(z, z, z, z, gates_rows, gate_b_rows, head_g)


RG_TILE = 512
RG_PAD = SUBLANES


def _softplus(x):
    return jnp.maximum(x, 0.0) + jnp.log1p(jnp.exp(-jnp.abs(x)))


def _gelu_tanh(x):
    c = math.sqrt(2.0 / math.pi)
    return x * (0.5 * (1.0 + jnp.tanh(c * (x + 0.044715 * (x * x * x)))))


def _rglru_body(x_ref, gr_ref, cw_ref, cb_ref, wa_ref, wx_ref, ba_ref, bx_ref, lam_ref, y_ref,
                xpad_ref, a_ref, u_ref, h_ref, *, s):
    nt = s // RG_TILE
    zeros = jnp.zeros((RG_PAD, LANES), F32)
    xpad_ref[pl.ds(0, RG_PAD), :] = zeros
    xpad_ref[pl.ds(RG_PAD + s, RG_PAD), :] = zeros

    def copy_in(t, carry):
        r0 = pl.multiple_of(t * RG_TILE, RG_TILE)
        xpad_ref[pl.ds(RG_PAD + r0, RG_TILE), :] = x_ref[0, pl.ds(r0, RG_TILE), :]
        return carry

    lax.fori_loop(0, nt, copy_in, 0)

    cw = cw_ref[...]
    cb = cb_ref[...]
    half_sp = [0.5 * RGLRU_C * _softplus(-lam_ref[d:d + 1, :]) for d in range(2)]

    def gates(t, carry):
        r0 = pl.multiple_of(t * RG_TILE, RG_TILE)
        xc = None
        for j in range(CONV_WIDTH):
            tap = xpad_ref[pl.ds(r0 + RG_PAD - CONV_LEFT + j, RG_TILE), :] * cw[j:j + 1, :]
            xc = tap if xc is None else xc + tap
        xc = xc + cb
        xcb = xc.astype(BF16)
        half_xc = 0.5 * xc
        for d in range(2):
            t_r = jnp.tanh(jnp.dot(xcb, wa_ref[d, 0], preferred_element_type=F32)
                           + ba_ref[d:d + 1, :])
            t_i = jnp.tanh(jnp.dot(xcb, wx_ref[d, 0], preferred_element_type=F32)
                           + bx_ref[d:d + 1, :])
            neg_log_a = half_sp[d] * t_r + half_sp[d]
            gated_x = half_xc * t_i + half_xc
            a = jnp.exp(-neg_log_a)
            one_minus_a2 = jnp.tanh(neg_log_a) * (1.0 + a * a)
            root = jnp.where(one_minus_a2 > 0.0, one_minus_a2 * lax.rsqrt(one_minus_a2), 0.0)
            a_ref[d, pl.ds(r0, RG_TILE), :] = a
            u_ref[d, pl.ds(r0, RG_TILE), :] = root * gated_x
        return carry

    lax.fori_loop(0, nt, gates, 0, unroll=2)

    nb = RG_TILE // SUBLANES
    sub = lax.broadcasted_iota(jnp.int32, (SUBLANES, LANES), 0)

    def compose_sublanes(a, u, reverse):
        sh = 1
        while sh < SUBLANES:
            keep = sub < SUBLANES - sh if reverse else sub >= sh
            amount = SUBLANES - sh if reverse else sh
            a_s = jnp.where(keep, pltpu.roll(a, amount, axis=0), 1.0)
            u_s = jnp.where(keep, pltpu.roll(u, amount, axis=0), 0.0)
            u = a * u_s + u
            a = a * a_s
            sh *= 2
        return a, u

    def tile_recurrence(t, d, carry):
        r0 = pl.multiple_of(t * RG_TILE, RG_TILE)
        reverse = d == 1
        a_rows, u_rows = [None] * SUBLANES, [None] * SUBLANES
        a_prev = u_prev = None
        for r in (reversed(range(SUBLANES)) if reverse else range(SUBLANES)):
            rows = pl.ds(r0 + r, nb, stride=SUBLANES)
            a, u = a_ref[d, rows, :], u_ref[d, rows, :]
            if a_prev is not None:
                u = a * u_prev + u
                a = a * a_prev
            a_rows[r], u_rows[r] = a, u
            a_prev, u_prev = a, u
        entry = [None] * (nb // SUBLANES)
        for v in (reversed(range(nb // SUBLANES)) if reverse else range(nb // SUBLANES)):
            blk = slice(v * SUBLANES, (v + 1) * SUBLANES)
            sa, su = compose_sublanes(a_prev[blk], u_prev[blk], reverse)
            after = sa * carry + su
            if reverse:
                entry[v] = jnp.where(sub < SUBLANES - 1, pltpu.roll(after, SUBLANES - 1, axis=0),
                                     carry)
                carry = after[0:1]
            else:
                entry[v] = jnp.where(sub >= 1, pltpu.roll(after, 1, axis=0), carry)
                carry = after[SUBLANES - 1:SUBLANES]
        h_entry = jnp.concatenate(entry, axis=0)
        for r in range(SUBLANES):
            h_ref[d, pl.ds(r0 + r, nb, stride=SUBLANES), :] = a_rows[r] * h_entry + u_rows[r]
        return carry

    def recurrences(t, carry):
        h_f, h_b = carry
        return tile_recurrence(t, 0, h_f), tile_recurrence(nt - 1 - t, 1, h_b)

    h0 = jnp.zeros((1, LANES), F32)
    lax.fori_loop(0, nt, recurrences, (h0, h0), unroll=True)

    def finish(t, carry):
        r0 = pl.multiple_of(t * RG_TILE, RG_TILE)
        hr = h_ref[0, pl.ds(r0, RG_TILE), :] + h_ref[1, pl.ds(r0, RG_TILE), :]
        y_ref[0, pl.ds(r0, RG_TILE), :] = (
            hr * _gelu_tanh(gr_ref[0, pl.ds(r0, RG_TILE), :])).astype(y_ref.dtype)
        return carry

    lax.fori_loop(0, nt, finish, 0, unroll=2)


def rglru_mixer(z, xr_block0, gr_block0, conv_w, conv_b, wa, wx, ba, bx, lam):
    b, s, _ = z.shape
    width = RNN_BLOCKS * RNN_BLOCK
    assert s % RG_TILE == 0
    return pl.pallas_call(
        functools.partial(_rglru_body, s=s),
        grid=(b, RNN_BLOCKS),
        in_specs=[
            pl.BlockSpec((1, s, RNN_BLOCK), lambda i, j: (i, 0, xr_block0 + j)),
            pl.BlockSpec((1, s, RNN_BLOCK), lambda i, j: (i, 0, gr_block0 + j)),
            pl.BlockSpec((CONV_WIDTH, RNN_BLOCK), lambda i, j: (0, j)),
            pl.BlockSpec((1, RNN_BLOCK), lambda i, j: (0, j)),
            pl.BlockSpec((2, 1, RNN_BLOCK, RNN_BLOCK), lambda i, j: (0, j, 0, 0)),
            pl.BlockSpec((2, 1, RNN_BLOCK, RNN_BLOCK), lambda i, j: (0, j, 0, 0)),
            pl.BlockSpec((2, RNN_BLOCK), lambda i, j: (0, j)),
            pl.BlockSpec((2, RNN_BLOCK), lambda i, j: (0, j)),
            pl.BlockSpec((2, RNN_BLOCK), lambda i, j: (0, j)),
        ],
        out_specs=pl.BlockSpec((1, s, RNN_BLOCK), lambda i, j: (i, 0, j)),
        out_shape=jax.ShapeDtypeStruct((b, s, width), BF16),
        scratch_shapes=[
            pltpu.VMEM((s + 2 * RG_PAD, RNN_BLOCK), F32),
            pltpu.VMEM((2, s, RNN_BLOCK), F32),
            pltpu.VMEM((2, s, RNN_BLOCK), F32),
            pltpu.VMEM((2, s, RNN_BLOCK), F32),
        ],
        compiler_params=_cparams("parallel", "parallel"),
        name="rglru",
    )(z, z, conv_w, conv_b, wa, wx, ba, bx, lam)


ATTN_TQ = 128
ATTN_TK = ATTN_TQ + 2 * ATTN_HALF
ATTN_UNROLL = 32


def _attn_body(q_ref, k_ref, v_ref, slope_ref, o_ref, bias_ref, x4_ref, og_ref, dg_ref, mg_ref,
               *, s):
    qscale = ATTN_DH ** -0.5 * LOG2E

    @pl.when(pl.program_id(1) == 0)
    def _():
        slope = slope_ref[0, 0:1, 0:1] * LOG2E
        qi = lax.broadcasted_iota(jnp.int32, (ATTN_TQ, ATTN_TK), 0)
        kj = lax.broadcasted_iota(jnp.int32, (ATTN_TQ, ATTN_TK), 1)
        for g, dil in enumerate(ATTN_DILATIONS):
            for e in range(3):
                rel = jnp.abs(kj - e * ATTN_HALF - qi)
                pen = slope * (rel * dil).astype(F32)
                bias_ref[g, e] = jnp.where(rel <= ATTN_HALF, -pen, NEG_BIG)

    s4 = s // 4
    srcs = (q_ref, k_ref, v_ref)

    def split4(t, carry):
        c = t // (s4 // ATTN_TK)
        p0 = (t % (s4 // ATTN_TK)) * ATTN_TK
        dst = pl.ds(pl.multiple_of(c * s4 + p0, ATTN_TK), ATTN_TK)
        for a in range(3):
            x = srcs[a][0, pl.ds(c + 4 * p0, ATTN_TK, stride=4), :]
            x4_ref[a, dst, :] = x * qscale if a == 0 else x
        return carry

    lax.fori_loop(0, 4 * (s4 // ATTN_TK), split4, 0, unroll=4)

    ones = jnp.ones((ATTN_TK, ATTN_DH), BF16)

    for g, dil in reversed(list(enumerate(ATTN_DILATIONS))):
        sp = s // dil
        nqb = sp // ATTN_TQ

        def q_block(t, carry, g=g, dil=dil, sp=sp, nqb=nqb):
            r = t // nqb
            p0 = (t % nqb) * ATTN_TQ
            kstart = jnp.clip(p0 - ATTN_HALF, 0, sp - ATTN_TK)
            e = (p0 - kstart) // ATTN_HALF
            if dil == 1:
                qrows = pl.ds(pl.multiple_of(p0, ATTN_TQ), ATTN_TQ)
                krows = pl.ds(pl.multiple_of(kstart, ATTN_HALF), ATTN_TK)
                qf = q_ref[0, qrows, :] * qscale
                kf, vf = k_ref[0, krows, :], v_ref[0, krows, :]
                orows = qrows
            elif dil == 4:
                qrows = pl.ds(pl.multiple_of(r * s4 + p0, ATTN_TQ), ATTN_TQ)
                krows = pl.ds(pl.multiple_of(r * s4 + kstart, ATTN_HALF), ATTN_TK)
                qf, kf, vf = x4_ref[0, qrows, :], x4_ref[1, krows, :], x4_ref[2, krows, :]
                orows = pl.ds(r + 4 * p0, ATTN_TQ, stride=4)
            else:
                base = (r % 4) * s4 + r // 4
                qrows = pl.ds(base + 4 * p0, ATTN_TQ, stride=4)
                krows = pl.ds(base + 4 * kstart, ATTN_TK, stride=4)
                qf, kf, vf = x4_ref[0, qrows, :], x4_ref[1, krows, :], x4_ref[2, krows, :]
                orows = pl.ds(r + dil * p0, ATTN_TQ, stride=dil)
            qb, kb, vb = qf.astype(BF16), kf.astype(BF16), vf.astype(BF16)
            sc = lax.dot_general(qb, kb, (((1,), (1,)), ((), ())),
                                 preferred_element_type=F32) + bias_ref[g, e]
            m = jnp.max(sc, axis=-1, keepdims=True)
            p = jnp.exp2(sc - m).astype(BF16)
            pv = jnp.dot(p, jnp.concatenate([vb, ones], axis=1), preferred_element_type=F32)
            acc, den = pv[:, :ATTN_DH], pv[:, ATTN_DH:]
            if dil != 1:
                og_ref[g - 1, orows, :] = acc
                dg_ref[g - 1, orows, :] = den
                mg_ref[g - 1, orows, :] = jnp.broadcast_to(m, den.shape)
                return carry
            m4, m16 = mg_ref[0, orows, :], mg_ref[1, orows, :]
            mx = jnp.maximum(jnp.maximum(m4, m16), m)
            e1, e4, e16 = jnp.exp2(m - mx), jnp.exp2(m4 - mx), jnp.exp2(m16 - mx)
            num = e1 * acc + e4 * og_ref[0, orows, :] + e16 * og_ref[1, orows, :]
            tot = e1 * den + e4 * dg_ref[0, orows, :] + e16 * dg_ref[1, orows, :]
            o_ref[0, orows, :] = (num * (1.0 / tot)).astype(o_ref.dtype)
            return carry

        lax.fori_loop(0, dil * nqb, q_block, 0, unroll=ATTN_UNROLL)


def dilated_attention(qkv, slopes):
    b, s, _ = qkv.shape
    h, dh = ATTN_HEADS, ATTN_DH
    assert s % (max(ATTN_DILATIONS) * ATTN_TK) == 0
    return pl.pallas_call(
        functools.partial(_attn_body, s=s),
        grid=(h, b),
        in_specs=[
            pl.BlockSpec((1, s, dh), lambda j, i: (i, 0, j)),
            pl.BlockSpec((1, s, dh), lambda j, i: (i, 0, h + j)),
            pl.BlockSpec((1, s, dh), lambda j, i: (i, 0, 2 * h + j)),
            pl.BlockSpec((1, SUBLANES, LANES), lambda j, i: (j, 0, 0)),
        ],
        out_specs=pl.BlockSpec((1, s, dh), lambda j, i: (i, 0, j)),
        out_shape=jax.ShapeDtypeStruct((b, s, h * dh), BF16),
        scratch_shapes=[
            pltpu.VMEM((len(ATTN_DILATIONS), 3, ATTN_TQ, ATTN_TK), F32),
            pltpu.VMEM((3, s, dh), F32),
            pltpu.VMEM((len(ATTN_DILATIONS) - 1, s, dh), F32),
            pltpu.VMEM((len(ATTN_DILATIONS) - 1, s, dh), F32),
            pltpu.VMEM((len(ATTN_DILATIONS) - 1, s, dh), F32),
        ],
        compiler_params=_cparams("parallel", "arbitrary"),
        name="dilated_attn",
    )(qkv, qkv, qkv, slopes)


def _ffn(x, h, w1, w3, w2, layer, next_norm_g, last):
    u = swiglu_up(h, w1, w3, layer)
    return matmul([(u, w2, 0, layer)], res=x, norm_g=next_norm_g,
                  norm_dtype=F32 if last else BF16, keep_sum=not last,
                  tm=RESIDENT_TM, tn=w2.shape[-1], name="ffn_down")


def _even_layer(x, bsz, seq, norm_g, w_in, gate_b, conv_w, conv_b, rg_wa, rg_ba, rg_wx, rg_bx,
                rg_lam, head_g, w_out, next_norm_g):
    heads, dk, dv = MLSTM_HEADS, MLSTM_DK, MLSTM_DV
    n_qkvo = 2 * heads * dk + 2 * heads * dv
    n_gate = 4 * heads
    width = RNN_BLOCKS * RNN_BLOCK
    w_in_t = jnp.transpose(w_in)
    w_main_t = jnp.concatenate([w_in_t[:n_qkvo], w_in_t[n_qkvo + n_gate:]], axis=0).astype(BF16)
    w_gate_t = w_in_t[n_qkvo:n_qkvo + n_gate].astype(BF16)
    z, zg_t = norm_matmul(x, norm_g, w_main_t, w_side_t=w_gate_t, w_transposed=True,
                          name="in_proj")
    z = z.reshape(bsz, seq, -1)

    chunk = MLSTM_CHUNK
    nc = seq // chunk
    gates_rows = jnp.transpose(zg_t[:n_gate].reshape(4, heads, bsz, nc, chunk), (2, 1, 0, 3, 4))
    gate_b_rows = jnp.broadcast_to(
        jnp.transpose(gate_b.astype(F32).reshape(4, heads))[:, :, None, None], (heads, 4, 1, chunk))
    y_a = mlstm_mixer(z, gates_rows, gate_b_rows, head_g.astype(F32).reshape(heads, 1, dv), chunk)

    y_b = rglru_mixer(z, n_qkvo // RNN_BLOCK, (n_qkvo + width) // RNN_BLOCK,
                      conv_w.astype(F32), conv_b.astype(F32).reshape(1, width),
                      (0.5 * rg_wa).astype(BF16), (0.5 * rg_wx).astype(BF16),
                      0.5 * rg_ba.astype(F32), 0.5 * rg_bx.astype(F32), rg_lam.astype(F32))

    w_out_b = w_out.astype(BF16)
    m = bsz * seq
    assert heads * dv == width
    return matmul([(y_a.reshape(m, -1), w_out_b, 0),
                   (y_b.reshape(m, -1), w_out_b, 1)], res=x, norm_g=next_norm_g,
                  tm=RESIDENT_TM, tn=w_out.shape[1], name="out_proj")


def _odd_layer(x, h, bsz, seq, norm_g, w_qkv, w_o, next_norm_g):
    if h is None:
        qkv = norm_matmul(x, norm_g, w_qkv.astype(BF16), tm=RESIDENT_TM // 2, name="qkv_proj")
    else:
        qkv = matmul([(h, w_qkv.astype(BF16))], tm=RESIDENT_TM, tn=w_qkv.shape[1],
                     name="qkv_proj")
    qkv = qkv.reshape(bsz, seq, -1)
    slopes = jnp.exp2(-ALIBI_MAX_BIAS * jnp.arange(1, ATTN_HEADS + 1, dtype=F32) / ATTN_HEADS)
    slopes = jnp.broadcast_to(slopes[:, None, None], (ATTN_HEADS, SUBLANES, LANES))
    o = dilated_attention(qkv, slopes)
    return matmul([(o.reshape(bsz * seq, -1), w_o.astype(BF16))], res=x, norm_g=next_norm_g,
                  tm=RESIDENT_TM, tn=w_o.shape[1], name="attn_out_proj")


def kernel(x, e_norm, e_w_in, e_gate_b, e_conv_w, e_conv_b, e_rg_wa, e_rg_ba, e_rg_wx, e_rg_bx,
           e_rg_lam, e_head_g, e_w_out, o_norm, o_w_qkv, o_w_o, f_norm, f_w1, f_w3, f_w2,
           final_norm):
    bsz, seq, d = x.shape
    depth = f_norm.shape[0]
    xs = x.reshape(bsz * seq, d).astype(F32)
    w1, w3, w2 = f_w1.astype(BF16), f_w3.astype(BF16), f_w2.astype(BF16)
    hs = None
    for l in range(depth):
        if l % 2 == 0:
            e = l // 2
            xs, hs = _even_layer(xs, bsz, seq, e_norm[e], e_w_in[e], e_gate_b[e], e_conv_w[e],
                                 e_conv_b[e], e_rg_wa[e], e_rg_ba[e], e_rg_wx[e], e_rg_bx[e],
                                 e_rg_lam[e], e_head_g[e], e_w_out[e], f_norm[l])
        else:
            o = l // 2
            xs, hs = _odd_layer(xs, hs, bsz, seq, o_norm[o], o_w_qkv[o], o_w_o[o], f_norm[l])
        if l == depth - 1:
            out = _ffn(xs, hs, w1, w3, w2, l, final_norm, last=True)
            return out.astype(x.dtype).reshape(bsz, seq, d)
        if (l + 1) % 2:
            xs, hs = _ffn(xs, hs, w1, w3, w2, l, o_norm[(l + 1) // 2], last=False)
        else:
            xs, hs = _ffn(xs, hs, w1, w3, w2, l, None, last=False), None
```

```python
import functools
import math

import jax
import jax.numpy as jnp
from jax import lax
from jax.experimental import pallas as pl
from jax.experimental.pallas import tpu as pltpu

F32 = jnp.float32
BF16 = jnp.bfloat16

RMS_EPS = 1e-6
NEG_BIG = -1e30
LOG2E = math.log2(math.e)

MLSTM_HEADS = 4
MLSTM_DK = 128
MLSTM_DV = 256
MLSTM_CHUNK = 256
RNN_BLOCKS = 8
RNN_BLOCK = 128
CONV_WIDTH = 4
CONV_LEFT = 2
RGLRU_C = 8.0
ATTN_HEADS = 16
ATTN_DH = 128
ATTN_HALF = 64
ATTN_DILATIONS = (1, 4, 16)
ALIBI_MAX_BIAS = 8.0

LANES = 128
SUBLANES = 8
VMEM_LIMIT_BYTES = 56 * 1024 * 1024

RESIDENT_TM = 512
SWIGLU_TM = 4096
SWIGLU_TN = 512
SWIGLU_ROWS = 1024
NORM_CHUNK = 256


def _cparams(*semantics):
    return pltpu.CompilerParams(dimension_semantics=semantics,
                                vmem_limit_bytes=VMEM_LIMIT_BYTES)


def _weight_spec(w, kdim, tn, row_block=0, layer=None, **mode):
    if w.ndim == 2:
        return pl.BlockSpec((kdim, tn), lambda i, j: (row_block, j), **mode)
    return pl.BlockSpec((None, kdim, tn), lambda i, j: (layer, row_block, j), **mode)


def _matmul_body(*refs, n_pairs, has_res, has_norm, keep_sum):
    pairs = [(refs[2 * p], refs[2 * p + 1]) for p in range(n_pairs)]
    rest = list(refs[2 * n_pairs:])
    r_ref = rest.pop(0) if has_res else None
    g_ref = rest.pop(0) if has_norm else None
    y_ref = rest.pop(0) if keep_sum else None
    h_ref = rest.pop(0) if has_norm else None

    tm = pairs[0][0].shape[0]
    rows = min(NORM_CHUNK, tm) if has_norm else tm
    for r0 in range(0, tm, rows):
        sl = pl.ds(r0, rows)
        acc = None
        for a_ref, w_ref in pairs:
            d = jnp.dot(a_ref[sl, :], w_ref[...], preferred_element_type=F32)
            acc = d if acc is None else acc + d
        if has_res:
            acc = acc + r_ref[sl, :]
        if keep_sum:
            y_ref[sl, :] = acc
        if has_norm:
            ms = jnp.mean(acc * acc, axis=-1, keepdims=True)
            h_ref[sl, :] = (acc * lax.rsqrt(ms + RMS_EPS) * g_ref[...]).astype(h_ref.dtype)


def matmul(pairs, *, tm, tn, res=None, norm_g=None, norm_dtype=BF16, keep_sum=True,
           name="matmul"):
    m = pairs[0][0].shape[0]
    n = pairs[0][1].shape[-1]
    tn = min(tn, n)
    assert m % tm == 0 and n % tn == 0 and (keep_sum or norm_g is not None)
    assert norm_g is None or (tn == n and tm % NORM_CHUNK == 0)
    w_mode = dict(pipeline_mode=pl.Buffered(1)) if tn == n else {}
    in_specs, args = [], []
    for a, w, *where in pairs:
        kdim = a.shape[1]
        kb, layer = (list(where) + [0, None])[:2] if where else (0, None)
        in_specs.append(pl.BlockSpec((tm, kdim), lambda i, j: (i, 0)))
        in_specs.append(_weight_spec(w, kdim, tn, kb, layer, **w_mode))
        args += [a, w]
    if res is not None:
        in_specs.append(pl.BlockSpec((tm, tn), lambda i, j: (i, j)))
        args.append(res)
    out_specs, out_shape = [], []
    if keep_sum:
        out_specs.append(pl.BlockSpec((tm, tn), lambda i, j: (i, j)))
        out_shape.append(jax.ShapeDtypeStruct((m, n), F32))
    if norm_g is not None:
        in_specs.append(pl.BlockSpec((1, n), lambda i, j: (0, 0)))
        args.append(norm_g.reshape(1, n).astype(F32))
        out_specs.append(pl.BlockSpec((tm, n), lambda i, j: (i, 0)))
        out_shape.append(jax.ShapeDtypeStruct((m, n), norm_dtype))
    outs = pl.pallas_call(
        functools.partial(_matmul_body, n_pairs=len(pairs), has_res=res is not None,
                          has_norm=norm_g is not None, keep_sum=keep_sum),
        grid=(m // tm, n // tn),
        in_specs=in_specs,
        out_specs=out_specs,
        out_shape=out_shape,
        compiler_params=_cparams("parallel", "parallel"),
        name=name,
    )(*args)
    return outs if len(outs) > 1 else outs[0]


def _swiglu_up_body(a_ref, w1_ref, w3_ref, o_ref):
    rows = min(SWIGLU_ROWS, a_ref.shape[0])
    for r0 in range(0, a_ref.shape[0], rows):
        a = a_ref[pl.ds(r0, rows), :]
        u = jnp.dot(a, w1_ref[...], preferred_element_type=F32)
        v = jnp.dot(a, w3_ref[...], preferred_element_type=F32)
        o_ref[pl.ds(r0, rows), :] = (u * jax.nn.sigmoid(u) * v).astype(o_ref.dtype)


def swiglu_up(a, w1, w3, layer, tm=SWIGLU_TM, tn=SWIGLU_TN):
    m, kdim = a.shape
    n = w1.shape[-1]
    assert m % tm == 0 and n % tn == 0
    return pl.pallas_call(
        _swiglu_up_body,
        grid=(m // tm, n // tn),
        in_specs=[pl.BlockSpec((tm, kdim), lambda i, j: (i, 0)),
                  _weight_spec(w1, kdim, tn, layer=layer),
                  _weight_spec(w3, kdim, tn, layer=layer)],
        out_specs=pl.BlockSpec((tm, tn), lambda i, j: (i, j)),
        out_shape=jax.ShapeDtypeStruct((m, n), BF16),
        compiler_params=_cparams("parallel", "parallel"),
        name="swiglu_up",
    )(a, w1, w3)


def _norm_matmul_body(*refs, has_side, w_transposed):
    if has_side:
        x_ref, g_ref, w_ref, ws_ref, o_ref, os_ref = refs
    else:
        x_ref, g_ref, w_ref, o_ref = refs
    w_dims = (((1,), (1,)), ((), ())) if w_transposed else (((1,), (0,)), ((), ()))

    g = g_ref[...]
    for r0 in range(0, x_ref.shape[0], NORM_CHUNK):
        sl = pl.ds(r0, NORM_CHUNK)
        x = x_ref[sl, :]
        ms = jnp.mean(x * x, axis=-1, keepdims=True)
        h = (x * lax.rsqrt(ms + RMS_EPS) * g).astype(BF16)
        o_ref[sl, :] = lax.dot_general(h, w_ref[...], w_dims,
                                       preferred_element_type=F32).astype(o_ref.dtype)
        if has_side:
            os_ref[:, sl] = lax.dot_general(ws_ref[...], h, (((1,), (1,)), ((), ())),
                                            preferred_element_type=F32)


def norm_matmul(x, g, w, w_side_t=None, w_transposed=False, out_dtype=F32, tm=RESIDENT_TM,
                name="norm_matmul"):
    m, d = x.shape
    n = w.shape[0] if w_transposed else w.shape[1]
    assert m % tm == 0 and tm % NORM_CHUNK == 0
    resident = dict(pipeline_mode=pl.Buffered(1))
    in_specs = [pl.BlockSpec((tm, d), lambda i: (i, 0)),
                pl.BlockSpec((1, d), lambda i: (0, 0)),
                pl.BlockSpec(w.shape, lambda i: (0, 0), **resident)]
    args = [x, g.reshape(1, d).astype(F32), w]
    out_specs = [pl.BlockSpec((tm, n), lambda i: (i, 0))]
    out_shape = [jax.ShapeDtypeStruct((m, n), out_dtype)]
    if w_side_t is not None:
        ns = w_side_t.shape[0]
        in_specs.append(pl.BlockSpec((ns, d), lambda i: (0, 0), **resident))
        args.append(w_side_t)
        out_specs.append(pl.BlockSpec((ns, tm), lambda i: (0, i)))
        out_shape.append(jax.ShapeDtypeStruct((ns, m), F32))
    outs = pl.pallas_call(
        functools.partial(_norm_matmul_body, has_side=w_side_t is not None,
                          w_transposed=w_transposed),
        grid=(m // tm,),
        in_specs=in_specs,
        out_specs=out_specs,
        out_shape=out_shape,
        compiler_params=_cparams("parallel"),
        name=name,
    )(*args)
    return outs if w_side_t is not None else outs[0]


def _log_sigmoid(x):
    return jnp.minimum(x, 0.0) - jnp.log1p(jnp.exp(-jnp.abs(x)))


SCAN_RADIX = 4


def _lane_scan(x, combine, identity, reverse):
    width = x.shape[-1]
    lane = lax.broadcasted_iota(jnp.int32, x.shape, 1)

    def shifted(v, sh):
        if reverse:
            return jnp.where(lane < width - sh, pltpu.roll(v, width - sh, axis=1), identity)
        return jnp.where(lane >= sh, pltpu.roll(v, sh, axis=1), identity)

    sh = 1
    while sh < width:
        parts = [shifted(x, k * sh) for k in range(1, SCAN_RADIX) if k * sh < width]
        for p in parts:
            x = combine(x, p)
        sh *= SCAN_RADIX
    return x


def _mlstm_body(q_ref, k_ref, v_ref, og_ref, g_ref, gb_ref, hg_ref, y_ref,
                rows_ref, cols_ref, hf_ref, hb_ref, ct_ref, *, nc, chunk):
    dk, dv = MLSTM_DK, MLSTM_DV
    scale = dk ** -0.5

    g = g_ref[0, 0] + gb_ref[0]
    rows = []
    for d in range(2):
        b = _lane_scan(_log_sigmoid(g[2 * d + 1]), jnp.add, 0.0, reverse=d == 1)
        gg = g[2 * d] - b
        gmax = _lane_scan(gg, jnp.maximum, -jnp.inf, reverse=d == 1)
        scaled = [gg * LOG2E, b * LOG2E, gmax * LOG2E]
        rows += scaled
        for j, r in enumerate(scaled):
            rows_ref[3 * d + j] = r
    rowmat = jnp.concatenate(rows + [jnp.zeros((LANES - 6 * nc, chunk), F32)], axis=0)
    colmat = rowmat.T
    for c in range(nc):
        cols_ref[c] = colmat if c == 0 else pltpu.roll(colmat, LANES - c, axis=1)

    ct_ref[...] = jnp.zeros_like(ct_ref)

    row_id = lax.broadcasted_iota(jnp.int32, (chunk, chunk), 0)
    col_id = lax.broadcasted_iota(jnp.int32, (chunk, chunk), 1)
    causal = (row_id >= col_id, row_id <= col_id)
    ones = jnp.ones((chunk, LANES), BF16)

    def lanes2(x):
        return jnp.concatenate([x] * (dv // LANES), axis=1)

    hg = hg_ref[0]

    def emit(r0, h, own_ref, other_ref):
        rows = pl.ds(r0, chunk)
        if other_ref is None:
            own_ref[rows, :] = h
            return
        hm = h + other_ref[rows, :]
        ms = jnp.mean(hm * hm, axis=-1, keepdims=True)
        hn = hm * lax.rsqrt(ms + RMS_EPS) * hg
        y_ref[0, rows, :] = (hn * jax.nn.sigmoid(og_ref[0, rows, :])).astype(y_ref.dtype)

    def chunk_step(c, m, d, own_ref, other_ref):
        r0 = pl.multiple_of(c * chunk, chunk)
        qb = (q_ref[0, pl.ds(r0, chunk), :] * scale).astype(BF16)
        kb = k_ref[0, pl.ds(r0, chunk), :].astype(BF16)
        vf = v_ref[0, pl.ds(r0, chunk), :]
        cols = cols_ref[c]

        def column(j):
            lane = (3 * d + j) * nc
            return jnp.broadcast_to(cols[:, lane:lane + 1], (chunk, LANES))

        gg_col, b_col, gmax_col = column(0), column(1), column(2)
        gg_row = rows_ref[3 * d, pl.ds(c, 1), :]
        end = chunk - 1 if d == 0 else 0
        total = rows_ref[3 * d + 1, pl.ds(c, 1), :][:, end:end + 1]
        gmax_end = rows_ref[3 * d + 2, pl.ds(c, 1), :][:, end:end + 1]

        m_row = jnp.maximum(gmax_col, m)
        w_intra = jnp.exp2(jnp.where(causal[d], gg_row - lanes2(m_row), NEG_BIG))
        w_inter = jnp.exp2(m - m_row)
        s = lax.dot_general(qb, kb, (((1,), (1,)), ((), ())),
                            preferred_element_type=F32) * w_intra
        ct = ct_ref[d]
        intra = jnp.dot(s.astype(BF16), jnp.concatenate([vf.astype(BF16), ones], axis=1),
                        preferred_element_type=F32)
        inter = jnp.dot(qb, ct.astype(BF16), preferred_element_type=F32)
        num = intra[:, :dv] + lanes2(w_inter) * inter[:, :dv]
        den = intra[:, dv:] + w_inter * inter[:, dv:]
        inv = 1.0 / jnp.maximum(jnp.abs(den), jnp.exp2(-(b_col + m_row)))
        emit(r0, num * lanes2(inv), own_ref, other_ref)

        m_new = jnp.maximum(total + m, total + gmax_end)
        w_src = jnp.exp2(total + gg_col - m_new)
        decay = jnp.exp2(total + m - m_new)
        wv = jnp.concatenate([lanes2(w_src) * vf, w_src], axis=1).astype(BF16)
        ct_ref[d] = decay * ct + lax.dot_general(
            kb, wv, (((0,), (0,)), ((), ())), preferred_element_type=F32)
        return m_new

    def first_half(c, carry):
        m_f, m_b = carry
        m_f = chunk_step(c, m_f, 0, hf_ref, None)
        m_b = chunk_step(nc - 1 - c, m_b, 1, hb_ref, None)
        return m_f, m_b

    def second_half(c, carry):
        m_f, m_b = carry
        m_f = chunk_step(c, m_f, 0, None, hb_ref)
        m_b = chunk_step(nc - 1 - c, m_b, 1, None, hf_ref)
        return m_f, m_b

    m0 = jnp.full((1, 1), NEG_BIG, F32)
    carry = lax.fori_loop(0, nc // 2, first_half, (m0, m0), unroll=8)
    lax.fori_loop(nc // 2, nc, second_half, carry, unroll=8)


def mlstm_mixer(z, gates_rows, gate_b_rows, head_g, chunk=MLSTM_CHUNK):
    b, s, _ = z.shape
    h, dk, dv = MLSTM_HEADS, MLSTM_DK, MLSTM_DV
    nc = s // chunk
    assert s % chunk == 0 and 6 * nc <= LANES and chunk % LANES == 0 and nc % 2 == 0
    kq = h * dk // dk
    kv = 2 * h * dk // dv
    ko = kv + h
    return pl.pallas_call(
        functools.partial(_mlstm_body, nc=nc, chunk=chunk),
        grid=(b, h),
        in_specs=[
            pl.BlockSpec((1, s, dk), lambda i, j: (i, 0, j)),
            pl.BlockSpec((1, s, dk), lambda i, j: (i, 0, kq + j)),
            pl.BlockSpec((1, s, dv), lambda i, j: (i, 0, kv + j)),
            pl.BlockSpec((1, s, dv), lambda i, j: (i, 0, ko + j)),
            pl.BlockSpec((1, 1, 4, nc, chunk), lambda i, j: (i, j, 0, 0, 0)),
            pl.BlockSpec((1, 4, 1, chunk), lambda i, j: (j, 0, 0, 0)),
            pl.BlockSpec((1, 1, dv), lambda i, j: (j, 0, 0)),
        ],
        out_specs=pl.BlockSpec((1, s, dv), lambda i, j: (i, 0, j)),
        out_shape=jax.ShapeDtypeStruct((b, s, h * dv), BF16),
        scratch_shapes=[
            pltpu.VMEM((6, nc, chunk), F32),
            pltpu.VMEM((nc, chunk, LANES), F32),
            pltpu.VMEM((s, dv), F32),
            pltpu.VMEM((s, dv), F32),
            pltpu.VMEM((2, dk, dv + LANES), F32),
        ],
        compiler_params=_cparams("parallel", "parallel"),
        name="mlstm",
    )(z, z, z, z, gates_rows, gate_b_rows, head_g)


RG_TILE = 512
RG_PAD = SUBLANES


def _softplus(x):
    return jnp.maximum(x, 0.0) + jnp.log1p(jnp.exp(-jnp.abs(x)))


def _gelu_tanh(x):
    c = math.sqrt(2.0 / math.pi)
    return x * (0.5 * (1.0 + jnp.tanh(c * (x + 0.044715 * (x * x * x)))))


def _rglru_body(x_ref, gr_ref, cw_ref, cb_ref, wa_ref, wx_ref, ba_ref, bx_ref, lam_ref, y_ref,
                xpad_ref, a_ref, u_ref, h_ref, *, s):
    nt = s // RG_TILE
    zeros = jnp.zeros((RG_PAD, LANES), F32)
    xpad_ref[pl.ds(0, RG_PAD), :] = zeros
    xpad_ref[pl.ds(RG_PAD + s, RG_PAD), :] = zeros

    def copy_in(t, carry):
        r0 = pl.multiple_of(t * RG_TILE, RG_TILE)
        xpad_ref[pl.ds(RG_PAD + r0, RG_TILE), :] = x_ref[0, pl.ds(r0, RG_TILE), :]
        return carry

    lax.fori_loop(0, nt, copy_in, 0)

    cw = cw_ref[...]
    cb = cb_ref[...]
    half_sp = [0.5 * RGLRU_C * _softplus(-lam_ref[d:d + 1, :]) for d in range(2)]

    def gates(t, carry):
        r0 = pl.multiple_of(t * RG_TILE, RG_TILE)
        xc = None
        for j in range(CONV_WIDTH):
            tap = xpad_ref[pl.ds(r0 + RG_PAD - CONV_LEFT + j, RG_TILE), :] * cw[j:j + 1, :]
            xc = tap if xc is None else xc + tap
        xc = xc + cb
        xcb = xc.astype(BF16)
        half_xc = 0.5 * xc
        for d in range(2):
            t_r = jnp.tanh(jnp.dot(xcb, wa_ref[d, 0], preferred_element_type=F32)
                           + ba_ref[d:d + 1, :])
            t_i = jnp.tanh(jnp.dot(xcb, wx_ref[d, 0], preferred_element_type=F32)
                           + bx_ref[d:d + 1, :])
            neg_log_a = half_sp[d] * t_r + half_sp[d]
            gated_x = half_xc * t_i + half_xc
            a = jnp.exp(-neg_log_a)
            one_minus_a2 = jnp.tanh(neg_log_a) * (1.0 + a * a)
            root = jnp.where(one_minus_a2 > 0.0, one_minus_a2 * lax.rsqrt(one_minus_a2), 0.0)
            a_ref[d, pl.ds(r0, RG_TILE), :] = a
            u_ref[d, pl.ds(r0, RG_TILE), :] = root * gated_x
        return carry

    lax.fori_loop(0, nt, gates, 0, unroll=2)

    nb = RG_TILE // SUBLANES
    sub = lax.broadcasted_iota(jnp.int32, (SUBLANES, LANES), 0)

    def compose_sublanes(a, u, reverse):
        sh = 1
        while sh < SUBLANES:
            keep = sub < SUBLANES - sh if reverse else sub >= sh
            amount = SUBLANES - sh if reverse else sh
            a_s = jnp.where(keep, pltpu.roll(a, amount, axis=0), 1.0)
            u_s = jnp.where(keep, pltpu.roll(u, amount, axis=0), 0.0)
            u = a * u_s + u
            a = a * a_s
            sh *= 2
        return a, u

    def tile_recurrence(t, d, carry):
        r0 = pl.multiple_of(t * RG_TILE, RG_TILE)
        reverse = d == 1
        a_rows, u_rows = [None] * SUBLANES, [None] * SUBLANES
        a_prev = u_prev = None
        for r in (reversed(range(SUBLANES)) if reverse else range(SUBLANES)):
            rows = pl.ds(r0 + r, nb, stride=SUBLANES)
            a, u = a_ref[d, rows, :], u_ref[d, rows, :]
            if a_prev is not None:
                u = a * u_prev + u
                a = a * a_prev
            a_rows[r], u_rows[r] = a, u
            a_prev, u_prev = a, u
        entry = [None] * (nb // SUBLANES)
        for v in (reversed(range(nb // SUBLANES)) if reverse else range(nb // SUBLANES)):
            blk = slice(v * SUBLANES, (v + 1) * SUBLANES)
            sa, su = compose_sublanes(a_prev[blk], u_prev[blk], reverse)
            after = sa * carry + su
            if reverse:
                entry[v] = jnp.where(sub < SUBLANES - 1, pltpu.roll(after, SUBLANES - 1, axis=0),
                                     carry)
                carry = after[0:1]
            else:
                entry[v] = jnp.where(sub >= 1, pltpu.roll(after, 1, axis=0), carry)
                carry = after[SUBLANES - 1:SUBLANES]
        h_entry = jnp.concatenate(entry, axis=0)
        for r in range(SUBLANES):
            h_ref[d, pl.ds(r0 + r, nb, stride=SUBLANES), :] = a_rows[r] * h_entry + u_rows[r]
        return carry

    def recurrences(t, carry):
        h_f, h_b = carry
        return tile_recurrence(t, 0, h_f), tile_recurrence(nt - 1 - t, 1, h_b)

    h0 = jnp.zeros((1, LANES), F32)
    lax.fori_loop(0, nt, recurrences, (h0, h0), unroll=True)

    def finish(t, carry):
        r0 = pl.multiple_of(t * RG_TILE, RG_TILE)
        hr = h_ref[0, pl.ds(r0, RG_TILE), :] + h_ref[1, pl.ds(r0, RG_TILE), :]
        y_ref[0, pl.ds(r0, RG_TILE), :] = (
            hr * _gelu_tanh(gr_ref[0, pl.ds(r0, RG_TILE), :])).astype(y_ref.dtype)
        return carry

    lax.fori_loop(0, nt, finish, 0, unroll=2)


def rglru_mixer(z, xr_block0, gr_block0, conv_w, conv_b, wa, wx, ba, bx, lam):
    b, s, _ = z.shape
    width = RNN_BLOCKS * RNN_BLOCK
    assert s % RG_TILE == 0
    return pl.pallas_call(
        functools.partial(_rglru_body, s=s),
        grid=(b, RNN_BLOCKS),
        in_specs=[
            pl.BlockSpec((1, s, RNN_BLOCK), lambda i, j: (i, 0, xr_block0 + j)),
            pl.BlockSpec((1, s, RNN_BLOCK), lambda i, j: (i, 0, gr_block0 + j)),
            pl.BlockSpec((CONV_WIDTH, RNN_BLOCK), lambda i, j: (0, j)),
            pl.BlockSpec((1, RNN_BLOCK), lambda i, j: (0, j)),
            pl.BlockSpec((2, 1, RNN_BLOCK, RNN_BLOCK), lambda i, j: (0, j, 0, 0)),
            pl.BlockSpec((2, 1, RNN_BLOCK, RNN_BLOCK), lambda i, j: (0, j, 0, 0)),
            pl.BlockSpec((2, RNN_BLOCK), lambda i, j: (0, j)),
            pl.BlockSpec((2, RNN_BLOCK), lambda i, j: (0, j)),
            pl.BlockSpec((2, RNN_BLOCK), lambda i, j: (0, j)),
        ],
        out_specs=pl.BlockSpec((1, s, RNN_BLOCK), lambda i, j: (i, 0, j)),
        out_shape=jax.ShapeDtypeStruct((b, s, width), BF16),
        scratch_shapes=[
            pltpu.VMEM((s + 2 * RG_PAD, RNN_BLOCK), F32),
            pltpu.VMEM((2, s, RNN_BLOCK), F32),
            pltpu.VMEM((2, s, RNN_BLOCK), F32),
            pltpu.VMEM((2, s, RNN_BLOCK), F32),
        ],
        compiler_params=_cparams("parallel", "parallel"),
        name="rglru",
    )(z, z, conv_w, conv_b, wa, wx, ba, bx, lam)


ATTN_TQ = 128
ATTN_TK = ATTN_TQ + 2 * ATTN_HALF
ATTN_UNROLL = 32


def _attn_body(q_ref, k_ref, v_ref, slope_ref, o_ref, bias_ref, x4_ref, og_ref, dg_ref, mg_ref,
               *, s):
    qscale = ATTN_DH ** -0.5 * LOG2E

    @pl.when(pl.program_id(1) == 0)
    def _():
        slope = slope_ref[0, 0:1, 0:1] * LOG2E
        qi = lax.broadcasted_iota(jnp.int32, (ATTN_TQ, ATTN_TK), 0)
        kj = lax.broadcasted_iota(jnp.int32, (ATTN_TQ, ATTN_TK), 1)
        for g, dil in enumerate(ATTN_DILATIONS):
            for e in range(3):
                rel = jnp.abs(kj - e * ATTN_HALF - qi)
                pen = slope * (rel * dil).astype(F32)
                bias_ref[g, e] = jnp.where(rel <= ATTN_HALF, -pen, NEG_BIG)

    s4 = s // 4
    srcs = (q_ref, k_ref, v_ref)

    def split4(t, carry):
        c = t // (s4 // ATTN_TK)
        p0 = (t % (s4 // ATTN_TK)) * ATTN_TK
        dst = pl.ds(pl.multiple_of(c * s4 + p0, ATTN_TK), ATTN_TK)
        for a in range(3):
            x = srcs[a][0, pl.ds(c + 4 * p0, ATTN_TK, stride=4), :]
            x4_ref[a, dst, :] = x * qscale if a == 0 else x
        return carry

    lax.fori_loop(0, 4 * (s4 // ATTN_TK), split4, 0, unroll=True)

    ones = jnp.ones((ATTN_TK, ATTN_DH), BF16)

    for g, dil in reversed(list(enumerate(ATTN_DILATIONS))):
        sp = s // dil
        nqb = sp // ATTN_TQ

        def q_block(t, carry, g=g, dil=dil, sp=sp, nqb=nqb):
            r = t // nqb
            p0 = (t % nqb) * ATTN_TQ
            kstart = jnp.clip(p0 - ATTN_HALF, 0, sp - ATTN_TK)
            e = (p0 - kstart) // ATTN_HALF
            if dil == 1:
                qrows = pl.ds(pl.multiple_of(p0, ATTN_TQ), ATTN_TQ)
                krows = pl.ds(pl.multiple_of(kstart, ATTN_HALF), ATTN_TK)
                qf = q_ref[0, qrows, :] * qscale
                kf, vf = k_ref[0, krows, :], v_ref[0, krows, :]
                orows = qrows
            elif dil == 4:
                qrows = pl.ds(pl.multiple_of(r * s4 + p0, ATTN_TQ), ATTN_TQ)
                krows = pl.ds(pl.multiple_of(r * s4 + kstart, ATTN_HALF), ATTN_TK)
                qf, kf, vf = x4_ref[0, qrows, :], x4_ref[1, krows, :], x4_ref[2, krows, :]
                orows = pl.ds(r + 4 * p0, ATTN_TQ, stride=4)
            else:
                base = (r % 4) * s4 + r // 4
                qrows = pl.ds(base + 4 * p0, ATTN_TQ, stride=4)
                krows = pl.ds(base + 4 * kstart, ATTN_TK, stride=4)
                qf, kf, vf = x4_ref[0, qrows, :], x4_ref[1, krows, :], x4_ref[2, krows, :]
                orows = pl.ds(r + dil * p0, ATTN_TQ, stride=dil)
            qb, kb, vb = qf.astype(BF16), kf.astype(BF16), vf.astype(BF16)
            sc = lax.dot_general(qb, kb, (((1,), (1,)), ((), ())),
                                 preferred_element_type=F32) + bias_ref[g, e]
            m = jnp.max(sc, axis=-1, keepdims=True)
            p = jnp.exp2(sc - m).astype(BF16)
            pv = jnp.dot(p, jnp.concatenate([vb, ones], axis=1), preferred_element_type=F32)
            acc, den = pv[:, :ATTN_DH], pv[:, ATTN_DH:]
            if dil != 1:
                og_ref[g - 1, orows, :] = acc
                dg_ref[g - 1, orows, :] = den
                mg_ref[g - 1, orows, :] = jnp.broadcast_to(m, den.shape)
                return carry
            m4, m16 = mg_ref[0, orows, :], mg_ref[1, orows, :]
            mx = jnp.maximum(jnp.maximum(m4, m16), m)
            e1, e4, e16 = jnp.exp2(m - mx), jnp.exp2(m4 - mx), jnp.exp2(m16 - mx)
            num = e1 * acc + e4 * og_ref[0, orows, :] + e16 * og_ref[1, orows, :]
            tot = e1 * den + e4 * dg_ref[0, orows, :] + e16 * dg_ref[1, orows, :]
            o_ref[0, orows, :] = (num * (1.0 / tot)).astype(o_ref.dtype)
            return carry

        lax.fori_loop(0, dil * nqb, q_block, 0, unroll=ATTN_UNROLL)


def dilated_attention(qkv, slopes):
    b, s, _ = qkv.shape
    h, dh = ATTN_HEADS, ATTN_DH
    assert s % (max(ATTN_DILATIONS) * ATTN_TK) == 0
    return pl.pallas_call(
        functools.partial(_attn_body, s=s),
        grid=(h, b),
        in_specs=[
            pl.BlockSpec((1, s, dh), lambda j, i: (i, 0, j)),
            pl.BlockSpec((1, s, dh), lambda j, i: (i, 0, h + j)),
            pl.BlockSpec((1, s, dh), lambda j, i: (i, 0, 2 * h + j)),
            pl.BlockSpec((1, SUBLANES, LANES), lambda j, i: (j, 0, 0)),
        ],
        out_specs=pl.BlockSpec((1, s, dh), lambda j, i: (i, 0, j)),
        out_shape=jax.ShapeDtypeStruct((b, s, h * dh), BF16),
        scratch_shapes=[
            pltpu.VMEM((len(ATTN_DILATIONS), 3, ATTN_TQ, ATTN_TK), F32),
            pltpu.VMEM((3, s, dh), F32),
            pltpu.VMEM((len(ATTN_DILATIONS) - 1, s, dh), F32),
            pltpu.VMEM((len(ATTN_DILATIONS) - 1, s, dh), F32),
            pltpu.VMEM((len(ATTN_DILATIONS) - 1, s, dh), F32),
        ],
        compiler_params=_cparams("parallel", "arbitrary"),
        name="dilated_attn",
    )(qkv, qkv, qkv, slopes)


def _ffn(x, h, w1, w3, w2, layer, next_norm_g, last):
    u = swiglu_up(h, w1, w3, layer)
    return matmul([(u, w2, 0, layer)], res=x, norm_g=next_norm_g,
                  norm_dtype=F32 if last else BF16, keep_sum=not last,
                  tm=RESIDENT_TM, tn=w2.shape[-1], name="ffn_down")


def _even_layer(x, bsz, seq, norm_g, w_in, gate_b, conv_w, conv_b, rg_wa, rg_ba, rg_wx, rg_bx,
                rg_lam, head_g, w_out, next_norm_g):
    heads, dk, dv = MLSTM_HEADS, MLSTM_DK, MLSTM_DV
    n_qkvo = 2 * heads * dk + 2 * heads * dv
    n_gate = 4 * heads
    width = RNN_BLOCKS * RNN_BLOCK
    w_in_t = jnp.transpose(w_in)
    w_main_t = jnp.concatenate([w_in_t[:n_qkvo], w_in_t[n_qkvo + n_gate:]], axis=0).astype(BF16)
    w_gate_t = w_in_t[n_qkvo:n_qkvo + n_gate].astype(BF16)
    z, zg_t = norm_matmul(x, norm_g, w_main_t, w_side_t=w_gate_t, w_transposed=True,
                          name="in_proj")
    z = z.reshape(bsz, seq, -1)

    chunk = MLSTM_CHUNK
    nc = seq // chunk
    gates_rows = jnp.transpose(zg_t[:n_gate].reshape(4, heads, bsz, nc, chunk), (2, 1, 0, 3, 4))
    gate_b_rows = jnp.broadcast_to(
        jnp.transpose(gate_b.astype(F32).reshape(4, heads))[:, :, None, None], (heads, 4, 1, chunk))
    y_a = mlstm_mixer(z, gates_rows, gate_b_rows, head_g.astype(F32).reshape(heads, 1, dv), chunk)

    y_b = rglru_mixer(z, n_qkvo // RNN_BLOCK, (n_qkvo + width) // RNN_BLOCK,
                      conv_w.astype(F32), conv_b.astype(F32).reshape(1, width),
                      (0.5 * rg_wa).astype(BF16), (0.5 * rg_wx).astype(BF16),
                      0.5 * rg_ba.astype(F32), 0.5 * rg_bx.astype(F32), rg_lam.astype(F32))

    w_out_b = w_out.astype(BF16)
    m = bsz * seq
    assert heads * dv == width
    return matmul([(y_a.reshape(m, -1), w_out_b, 0),
                   (y_b.reshape(m, -1), w_out_b, 1)], res=x, norm_g=next_norm_g,
                  tm=RESIDENT_TM, tn=w_out.shape[1], name="out_proj")


def _odd_layer(x, h, bsz, seq, norm_g, w_qkv, w_o, next_norm_g):
    if h is None:
        qkv = norm_matmul(x, norm_g, w_qkv.astype(BF16), tm=RESIDENT_TM // 2, name="qkv_proj")
    else:
        qkv = matmul([(h, w_qkv.astype(BF16))], tm=RESIDENT_TM, tn=w_qkv.shape[1],
                     name="qkv_proj")
    qkv = qkv.reshape(bsz, seq, -1)
    slopes = jnp.exp2(-ALIBI_MAX_BIAS * jnp.arange(1, ATTN_HEADS + 1, dtype=F32) / ATTN_HEADS)
    slopes = jnp.broadcast_to(slopes[:, None, None], (ATTN_HEADS, SUBLANES, LANES))
    o = dilated_attention(qkv, slopes)
    return matmul([(o.reshape(bsz * seq, -1), w_o.astype(BF16))], res=x, norm_g=next_norm_g,
                  tm=RESIDENT_TM, tn=w_o.shape[1], name="attn_out_proj")


def kernel(x, e_norm, e_w_in, e_gate_b, e_conv_w, e_conv_b, e_rg_wa, e_rg_ba, e_rg_wx, e_rg_bx,
           e_rg_lam, e_head_g, e_w_out, o_norm, o_w_qkv, o_w_o, f_norm, f_w1, f_w3, f_w2,
           final_norm):
    bsz, seq, d = x.shape
    depth = f_norm.shape[0]
    xs = x.reshape(bsz * seq, d).astype(F32)
    w1, w3, w2 = f_w1.astype(BF16), f_w3.astype(BF16), f_w2.astype(BF16)
    hs = None
    for l in range(depth):
        if l % 2 == 0:
            e = l // 2
            xs, hs = _even_layer(xs, bsz, seq, e_norm[e], e_w_in[e], e_gate_b[e], e_conv_w[e],
                                 e_conv_b[e], e_rg_wa[e], e_rg_ba[e], e_rg_wx[e], e_rg_bx[e],
                                 e_rg_lam[e], e_head_g[e], e_w_out[e], f_norm[l])
        else:
            o = l // 2
            xs, hs = _odd_layer(xs, hs, bsz, seq, o_norm[o], o_w_qkv[o], o_w_o[o], f_norm[l])
        if l == depth - 1:
            out = _ffn(xs, hs, w1, w3, w2, l, final_norm, last=True)
            return out.astype(x.dtype).reshape(bsz, seq, d)
        if (l + 1) % 2:
            xs, hs = _ffn(xs, hs, w1, w3, w2, l, o_norm[(l + 1) // 2], last=False)
        else:
            xs, hs = _ffn(xs, hs, w1, w3, w2, l, None, last=False), None
```
